```python
import math
import jax, jax.numpy as jnp
from jax import lax
import numpy as np

D_MODEL = 2048
BATCH = 4
SEQ = 2048
DEPTH = 2

N_META = 16
BLK = 128
ROPE_THETA = 10000.0
NORM_EPS = 1e-5
NEG = -1e30
SWA_HEADS = 12
SWA_KV_HEADS = 4
SWA_HEAD_DIM = 64
SWA_WINDOW = 128
SWA_WIDTH = SWA_HEADS * SWA_HEAD_DIM
SWA_KV_WIDTH = SWA_KV_HEADS * SWA_HEAD_DIM
SSM_WIDTH = 768
SSM_GROUP = 16
SSM_GROUPS = SSM_WIDTH // SSM_GROUP
SSM_STATE = 64
DT_MIN = 1e-3
DT_MAX = 1e-1
DIFF_HEADS = 4
DIFF_QK_DIM = 64
DIFF_V_DIM = 128
DIFF_WIDTH = DIFF_HEADS * DIFF_V_DIM
DIFF_QK_WIDTH = DIFF_HEADS * 2 * DIFF_QK_DIM
MIX_WIDTH = SWA_WIDTH + SSM_WIDTH + DIFF_WIDTH
IN_SPLITS = (SWA_WIDTH, SWA_KV_WIDTH, SWA_KV_WIDTH, SSM_WIDTH, DIFF_QK_WIDTH, DIFF_QK_WIDTH, DIFF_WIDTH)
IN_WIDTH = sum(IN_SPLITS)
N_EXPERT_GROUPS = 4
EXPERTS_PER_GROUP = 8
N_EXPERTS = N_EXPERT_GROUPS * EXPERTS_PER_GROUP
TOP_K = 2
D_EXPERT = 512

kernel_name = "hymba_style_swa_s5_diffattn_hiermoe"


def _split_points(sizes):
    pts, acc = [], 0
    for s in sizes[:-1]:
        acc += s
        pts.append(acc)
    return pts


def rms_norm(x, g):
    xf = x.astype(jnp.float32)
    y = xf * lax.rsqrt(jnp.mean(xf * xf, axis=-1, keepdims=True) + NORM_EPS)
    return (y * g.astype(jnp.float32)).astype(x.dtype)


def rope(x, pos):
    half = x.shape[-1] // 2
    inv = ROPE_THETA ** (-jnp.arange(half, dtype=jnp.float32) / half)
    ang = pos.astype(jnp.float32)[:, None] * inv[None, :]
    cos = jnp.cos(ang)[None, :, None, :]
    sin = jnp.sin(ang)[None, :, None, :]
    xf = x.astype(jnp.float32)
    x1, x2 = xf[..., :half], xf[..., half:]
    return jnp.concatenate([x1 * cos - x2 * sin, x2 * cos + x1 * sin], axis=-1).astype(x.dtype)


def pad_front(t, pad):
    return jnp.pad(t, ((0, 0), (pad, 0)) + ((0, 0),) * (t.ndim - 2))


def sliding_window_attention(q, k, v, sinks, pad):
    b, lp, hq, d = q.shape
    hkv = k.shape[2]
    grp = hq // hkv
    nb = lp // BLK
    qb = q.reshape(b, nb, BLK, hkv, grp, d)

    def with_prev(t):
        t = t.reshape(b, nb, BLK, hkv, d)
        prev = jnp.concatenate([jnp.zeros_like(t[:, :1]), t[:, :-1]], axis=1)
        return jnp.concatenate([prev, t], axis=2)

    kw, vw = with_prev(k), with_prev(v)
    s = jnp.einsum('bnqhgd,bnkhd->bhgnqk', qb, kw, preferred_element_type=jnp.float32) * (d ** -0.5)
    qi = jnp.arange(nb)[:, None] * BLK + jnp.arange(BLK)[None, :]
    kj = (jnp.arange(nb)[:, None] - 1) * BLK + jnp.arange(2 * BLK)[None, :]
    delta = qi[:, :, None] - kj[:, None, :]
    ok = (delta >= 0) & (delta < SWA_WINDOW) & (kj[:, None, :] >= pad)
    s = jnp.where(ok, s, NEG)
    sink = jnp.broadcast_to(sinks.astype(jnp.float32).reshape(hkv, grp)[None, :, :, None, None, None],
                            s.shape[:-1] + (1,))
    p = jax.nn.softmax(jnp.concatenate([s, sink], axis=-1), axis=-1)[..., :-1]
    o = jnp.einsum('bhgnqk,bnkhd->bnqhgd', p.astype(v.dtype), vw)
    return o.reshape(b, lp, hq * d)


def s5_ssm(u, lam_re, lam_im, b_re, b_im, c_re, c_im, d_skip, log_dt, w_glu):
    bsz, L, _ = u.shape
    G, P, H = SSM_GROUPS, SSM_STATE, SSM_GROUP
    f32 = jnp.float32
    uf = u.astype(f32).reshape(bsz, L, G, H)
    dt = jnp.exp(log_dt.astype(f32))[:, None]
    lr, li = lam_re.astype(f32), lam_im.astype(f32)
    mag = jnp.exp(lr * dt)
    abar_re, abar_im = mag * jnp.cos(li * dt), mag * jnp.sin(li * dt)
    den = lr * lr + li * li
    nr = abar_re - 1.0
    f_re = (nr * lr + abar_im * li) / den
    f_im = (abar_im * lr - nr * li) / den
    br, bi = b_re.astype(f32), b_im.astype(f32)
    bbar_re = f_re[..., None] * br - f_im[..., None] * bi
    bbar_im = f_re[..., None] * bi + f_im[..., None] * br
    bu_re = jnp.einsum('gph,blgh->lbgp', bbar_re, uf)
    bu_im = jnp.einsum('gph,blgh->lbgp', bbar_im, uf)
    a_re = jnp.broadcast_to(abar_re[None, None], (L, 1, G, P))
    a_im = jnp.broadcast_to(abar_im[None, None], (L, 1, G, P))

    def combine(e1, e2):
        a1r, a1i, b1r, b1i = e1
        a2r, a2i, b2r, b2i = e2
        return (a2r * a1r - a2i * a1i,
                a2r * a1i + a2i * a1r,
                a2r * b1r - a2i * b1i + b2r,
                a2r * b1i + a2i * b1r + b2i)

    _, _, xr, xi = lax.associative_scan(combine, (a_re, a_im, bu_re, bu_im), axis=0)
    y = (jnp.einsum('ghp,lbgp->blgh', c_re.astype(f32), xr)
         - jnp.einsum('ghp,lbgp->blgh', c_im.astype(f32), xi))
    y = y + d_skip.astype(f32).reshape(G, H) * uf
    y = jax.nn.gelu(y).reshape(bsz, L, SSM_WIDTH)
    y = y * jax.nn.sigmoid(y @ w_glu.astype(f32))
    return y.astype(u.dtype)


def differential_attention(q, k, v, lam, lam_init, subln_g, pad):
    b, lp = q.shape[:2]
    nb = lp // BLK
    scale = q.shape[-1] ** -0.5
    outs = []
    for n in range(nb):
        ke = (n + 1) * BLK
        s = jnp.einsum('bqhsd,bkhsd->bhsqk', q[:, n * BLK:ke], k[:, :ke],
                       preferred_element_type=jnp.float32) * scale
        qi = n * BLK + jnp.arange(BLK)
        kj = jnp.arange(ke)
        ok = (kj[None, :] <= qi[:, None]) & (kj[None, :] >= pad)
        p = jax.nn.softmax(jnp.where(ok, s, NEG), axis=-1)
        a = p[:, :, 0] - lam * p[:, :, 1]
        outs.append(jnp.einsum('bhqk,bkhd->bqhd', a.astype(v.dtype), v[:, :ke]))
    o = jnp.concatenate(outs, axis=1)
    o = rms_norm(o, subln_g) * (1.0 - lam_init)
    return o.reshape(b, lp, -1)


def hybrid_mixer(xn, pos, layer, w_in, swa_sinks, lam_re, lam_im, b_re, b_im, c_re, c_im, d_skip, log_dt,
                 w_glu, lq1, lk1, lq2, lk2, subln_g, w_out):
    b, L, _ = xn.shape
    pad = BLK - N_META
    proj = xn @ w_in
    qa, ka, va, us, qd, kd, vd = jnp.split(proj, _split_points(IN_SPLITS), axis=-1)
    qa = rope(qa.reshape(b, L, SWA_HEADS, SWA_HEAD_DIM), pos)
    ka = rope(ka.reshape(b, L, SWA_KV_HEADS, SWA_HEAD_DIM), pos)
    va = va.reshape(b, L, SWA_KV_HEADS, SWA_HEAD_DIM)
    o_a = sliding_window_attention(pad_front(qa, pad), pad_front(ka, pad), pad_front(va, pad),
                                   swa_sinks, pad)[:, pad:]
    o_b = s5_ssm(us, lam_re, lam_im, b_re, b_im, c_re, c_im, d_skip, log_dt, w_glu)
    qd = rope(qd.reshape(b, L, 2 * DIFF_HEADS, DIFF_QK_DIM), pos).reshape(b, L, DIFF_HEADS, 2, DIFF_QK_DIM)
    kd = rope(kd.reshape(b, L, 2 * DIFF_HEADS, DIFF_QK_DIM), pos).reshape(b, L, DIFF_HEADS, 2, DIFF_QK_DIM)
    vd = vd.reshape(b, L, DIFF_HEADS, DIFF_V_DIM)
    lam_init = 0.8 - 0.6 * math.exp(-0.3 * layer)
    f32 = jnp.float32
    lam = (jnp.exp(jnp.sum(lq1.astype(f32) * lk1.astype(f32)))
           - jnp.exp(jnp.sum(lq2.astype(f32) * lk2.astype(f32))) + lam_init)
    o_c = differential_attention(pad_front(qd, pad), pad_front(kd, pad), pad_front(vd, pad),
                                 lam, lam_init, subln_g, pad)[:, pad:]
    return jnp.concatenate([o_a, o_b, o_c], axis=-1) @ w_out


def hierarchical_moe(h, rg_w, rg_b, re_w, re_b, w_gate, w_up, w_down):
    n_tok, d = h.shape
    f32 = jnp.float32
    g_logits = (h @ rg_w + rg_b).astype(f32)
    g_prob = jax.nn.softmax(g_logits, axis=-1)
    g_sel = jnp.argmax(g_logits, axis=-1)
    p_g = jnp.take_along_axis(g_prob, g_sel[:, None], axis=1)
    e_logits = (h @ re_w + re_b).astype(f32).reshape(n_tok, N_EXPERT_GROUPS, EXPERTS_PER_GROUP)
    e_logits = jnp.take_along_axis(e_logits, g_sel[:, None, None], axis=1)[:, 0]
    top_w, top_i = lax.top_k(jax.nn.softmax(e_logits, axis=-1), TOP_K)
    top_w = top_w / jnp.sum(top_w, axis=-1, keepdims=True) * p_g
    expert = g_sel[:, None] * EXPERTS_PER_GROUP + top_i

    n_asg = n_tok * TOP_K
    e_flat = expert.reshape(n_asg)
    w_flat = top_w.reshape(n_asg)
    t_flat = jnp.arange(n_asg) // TOP_K
    order = jnp.argsort(e_flat)
    e_s, t_s, w_s = e_flat[order], t_flat[order], w_flat[order]
    counts = jnp.bincount(e_flat, length=N_EXPERTS)
    start = jnp.cumsum(counts) - counts
    padded = (counts + BLK - 1) // BLK * BLK
    pend = jnp.cumsum(padded)
    pstart = pend - padded
    dest = pstart[e_s] + (jnp.arange(n_asg) - start[e_s])
    n_blk = -(-n_asg // BLK) + N_EXPERTS
    rows = n_blk * BLK
    xbuf = jnp.zeros((rows, d), h.dtype).at[dest].set(h[t_s])
    blk_expert = jnp.clip(jnp.searchsorted(pend, jnp.arange(n_blk) * BLK, side='right'), 0, N_EXPERTS - 1)

    def expert_block(args):
        xb, e = args
        return (jax.nn.silu(xb @ w_gate[e]) * (xb @ w_up[e])) @ w_down[e]

    ybuf = lax.map(expert_block, (xbuf.reshape(n_blk, BLK, d), blk_expert)).reshape(rows, d)
    contrib = ybuf[dest] * w_s[:, None].astype(h.dtype)
    return jnp.zeros((n_tok, d), h.dtype).at[t_s].add(contrib)


def setup_inputs(seed: int = 0) -> dict:
    key = jax.random.key(seed)
    ks = jax.random.split(key, 29)
    f32 = jnp.float32

    def nrm(k, shape, scale):
        return jax.random.normal(k, shape, f32) * scale

    G, P, H = SSM_GROUPS, SSM_STATE, SSM_GROUP
    n_idx = jnp.arange(P, dtype=f32)
    return {
        "x": nrm(ks[0], (BATCH, SEQ, D_MODEL), 1.0),
        "meta_tokens": nrm(ks[1], (N_META, D_MODEL), 1.0),
        "attn_norm_g": 1.0 + nrm(ks[2], (DEPTH, D_MODEL), 0.02),
        "ffn_norm_g": 1.0 + nrm(ks[3], (DEPTH, D_MODEL), 0.02),
        "w_in": nrm(ks[4], (DEPTH, D_MODEL, IN_WIDTH), D_MODEL ** -0.5),
        "swa_sinks": nrm(ks[5], (DEPTH, SWA_HEADS), 0.5),
        "ssm_lambda_re": -0.5 + nrm(ks[6], (DEPTH, G, P), 0.01),
        "ssm_lambda_im": jnp.pi * n_idx + nrm(ks[7], (DEPTH, G, P), 0.01),
        "ssm_b_re": nrm(ks[8], (DEPTH, G, P, H), (2 * H) ** -0.5),
        "ssm_b_im": nrm(ks[9], (DEPTH, G, P, H), (2 * H) ** -0.5),
        "ssm_c_re": nrm(ks[10], (DEPTH, G, H, P), (2 * P) ** -0.5),
        "ssm_c_im": nrm(ks[11], (DEPTH, G, H, P), (2 * P) ** -0.5),
        "ssm_d": nrm(ks[12], (DEPTH, SSM_WIDTH), 1.0),
        "ssm_log_dt": jax.random.uniform(ks[13], (DEPTH, G), f32, math.log(DT_MIN), math.log(DT_MAX)),
        "ssm_w_glu": nrm(ks[14], (DEPTH, SSM_WIDTH, SSM_WIDTH), SSM_WIDTH ** -0.5),
        "diff_lambda_q1": nrm(ks[15], (DEPTH, DIFF_QK_DIM), 0.1),
        "diff_lambda_k1": nrm(ks[16], (DEPTH, DIFF_QK_DIM), 0.1),
        "diff_lambda_q2": nrm(ks[17], (DEPTH, DIFF_QK_DIM), 0.1),
        "diff_lambda_k2": nrm(ks[18], (DEPTH, DIFF_QK_DIM), 0.1),
        "diff_subln_g": 1.0 + nrm(ks[19], (DEPTH, DIFF_V_DIM), 0.02),
        "w_out": nrm(ks[20], (DEPTH, MIX_WIDTH, D_MODEL), MIX_WIDTH ** -0.5),
        "router_group_w": nrm(ks[21], (DEPTH, D_MODEL, N_EXPERT_GROUPS), D_MODEL ** -0.5),
        "router_group_b": nrm(ks[22], (DEPTH, N_EXPERT_GROUPS), 0.01),
        "router_expert_w": nrm(ks[23], (DEPTH, D_MODEL, N_EXPERTS), D_MODEL ** -0.5),
        "router_expert_b": nrm(ks[24], (DEPTH, N_EXPERTS), 0.01),
        "moe_w_gate": nrm(ks[25], (DEPTH, N_EXPERTS, D_MODEL, D_EXPERT), D_MODEL ** -0.5),
        "moe_w_up": nrm(ks[26], (DEPTH, N_EXPERTS, D_MODEL, D_EXPERT), D_MODEL ** -0.5),
        "moe_w_down": nrm(ks[27], (DEPTH, N_EXPERTS, D_EXPERT, D_MODEL), D_EXPERT ** -0.5),
        "final_norm_g": 1.0 + nrm(ks[28], (D_MODEL,), 0.02),
    }


def reference(x, meta_tokens, attn_norm_g, ffn_norm_g, w_in, swa_sinks, ssm_lambda_re, ssm_lambda_im,
              ssm_b_re, ssm_b_im, ssm_c_re, ssm_c_im, ssm_d, ssm_log_dt, ssm_w_glu, diff_lambda_q1,
              diff_lambda_k1, diff_lambda_q2, diff_lambda_k2, diff_subln_g, w_out, router_group_w,
              router_group_b, router_expert_w, router_expert_b, moe_w_gate, moe_w_up, moe_w_down,
              final_norm_g):
    b, _, d = x.shape
    meta = jnp.broadcast_to(meta_tokens[None].astype(x.dtype), (b, N_META, d))
    h = jnp.concatenate([meta, x], axis=1)
    pos = jnp.arange(h.shape[1], dtype=jnp.int32)
    for l in range(DEPTH):
        h = h + hybrid_mixer(rms_norm(h, attn_norm_g[l]), pos, l, w_in[l], swa_sinks[l],
                             ssm_lambda_re[l], ssm_lambda_im[l], ssm_b_re[l], ssm_b_im[l],
                             ssm_c_re[l], ssm_c_im[l], ssm_d[l], ssm_log_dt[l], ssm_w_glu[l],
                             diff_lambda_q1[l], diff_lambda_k1[l], diff_lambda_q2[l], diff_lambda_k2[l],
                             diff_subln_g[l], w_out[l])
        hn = rms_norm(h, ffn_norm_g[l]).reshape(-1, d)
        h = h + hierarchical_moe(hn, router_group_w[l], router_group_b[l], router_expert_w[l],
                                 router_expert_b[l], moe_w_gate[l], moe_w_up[l],
                                 moe_w_down[l]).reshape(h.shape)
    return rms_norm(h, final_norm_g)[:, N_META:]
```

```python
import functools
import math

import jax
import jax.numpy as jnp
from jax import lax
from jax.experimental import pallas as pl
from jax.experimental.pallas import tpu as pltpu

D_MODEL = 2048
N_META = 16
BLK = 128
PAD = BLK - N_META
ROPE_THETA = 10000.0
NORM_EPS = 1e-5
NEG = -1e30
SWA_HEADS = 12
SWA_KV_HEADS = 4
HEAD_DIM = 64
SWA_WIDTH = SWA_HEADS * HEAD_DIM
SSM_WIDTH = 768
SSM_GROUP = 16
SSM_GROUPS = SSM_WIDTH // SSM_GROUP
SSM_STATE = 64
DIFF_HEADS = 4
DIFF_V_DIM = 128
DIFF_WIDTH = DIFF_HEADS * DIFF_V_DIM
IN_WIDTH = 3584
N_EXPERT_GROUPS = 4
EXPERTS_PER_GROUP = 8
N_EXPERTS = N_EXPERT_GROUPS * EXPERTS_PER_GROUP
D_EXPERT = 512

COL_KA = 768
COL_VA = 1024
COL_US = 1280
COL_QD = 2048
COL_KD = 2560
COL_VD = 3072

LANES = 128
SSM_T = 16
SSM_PACK = LANES // SSM_GROUP
SSM_PACKS = SSM_GROUPS // SSM_PACK
SSM_PSTATE = SSM_PACK * SSM_STATE
MOE_BM = 128
VMEM_LIMIT = 56 * 1024 * 1024

F32 = jnp.float32
BF16 = jnp.bfloat16


def _row_tile(rows):
    for t in (512, 256, 128):
        if rows % t == 0:
            return t
    raise ValueError(f"row count {rows} is not a multiple of 128")


def _params(sem, vmem=VMEM_LIMIT):
    return pltpu.CompilerParams(dimension_semantics=sem, vmem_limit_bytes=vmem)


IN_TN = 512
ROPE_TILES = (0, 1, 4, 5)


def _norm_inproj_kernel(x_ref, g_ref, w_ref, cos_ref, sin_ref, o_ref, xn_ref):
    j = pl.program_id(1)

    @pl.when(j == 0)
    def _():
        x = x_ref[...]
        ms = jnp.mean(x * x, axis=-1, keepdims=True)
        xn_ref[...] = (x * lax.rsqrt(ms + NORM_EPS) * g_ref[...]).astype(BF16)

    acc = jnp.dot(xn_ref[...], w_ref[...], preferred_element_type=F32)
    is_rope = functools.reduce(jnp.logical_or, [j == t for t in ROPE_TILES])

    @pl.when(is_rope)
    def _():
        width = acc.shape[1]
        lane = lax.broadcasted_iota(jnp.int32, acc.shape, 1)
        first_half = (lane & (HEAD_DIM - 1)) < (HEAD_DIM // 2)
        partner = jnp.where(first_half, pltpu.roll(acc, width - HEAD_DIM // 2, 1), pltpu.roll(acc, HEAD_DIM // 2, 1))
        reps = width // LANES
        cos = jnp.tile(cos_ref[...], (1, reps))
        sin = jnp.tile(sin_ref[...], (1, reps))
        o_ref[...] = (acc * cos + partner * sin).astype(o_ref.dtype)

    @pl.when(jnp.logical_not(is_rope))
    def _():
        o_ref[...] = acc.astype(o_ref.dtype)


def _norm_inproj(h, g, w, cos, sin):
    rows = h.shape[0]
    tm = _row_tile(rows)
    return pl.pallas_call(
        _norm_inproj_kernel,
        grid=(rows // tm, IN_WIDTH // IN_TN),
        in_specs=[
            pl.BlockSpec((tm, D_MODEL), lambda i, j: (i, 0)),
            pl.BlockSpec((1, D_MODEL), lambda i, j: (0, 0)),
            pl.BlockSpec((D_MODEL, IN_TN), lambda i, j: (0, j)),
            pl.BlockSpec((tm, LANES), lambda i, j: (i, 0)),
            pl.BlockSpec((tm, LANES), lambda i, j: (i, 0)),
        ],
        out_specs=pl.BlockSpec((tm, IN_TN), lambda i, j: (i, j)),
        out_shape=jax.ShapeDtypeStruct((rows, IN_WIDTH), BF16),
        scratch_shapes=[pltpu.VMEM((tm, D_MODEL), BF16)],
        compiler_params=_params(("parallel", "arbitrary")),
        name="norm_inproj",
    )(h, g, w, cos, sin)


def _swa_kernel(sink_ref, q_ref, kp_ref, kc_ref, vp_ref, vc_ref, o_ref):
    n = pl.program_id(1)
    shape = (BLK, 2 * BLK)
    qi = lax.broadcasted_iota(jnp.int32, shape, 0)
    c = lax.broadcasted_iota(jnp.int32, shape, 1)
    kj = (n - 1) * BLK + c
    delta = qi + BLK - c
    ok = (delta >= 0) & (delta < BLK) & (kj >= PAD)
    grp = SWA_HEADS // SWA_KV_HEADS
    outs = []
    for j in range(SWA_KV_HEADS):
        cols = slice(j * HEAD_DIM, (j + 1) * HEAD_DIM)
        kk = jnp.concatenate([kp_ref[:, cols], kc_ref[:, cols]], axis=0)
        vv = jnp.concatenate([vp_ref[:, cols], vc_ref[:, cols]], axis=0)
        for g in range(grp):
            hd = j * grp + g
            qh = q_ref[:, hd * HEAD_DIM:(hd + 1) * HEAD_DIM]
            s = lax.dot_general(qh, kk, (((1,), (1,)), ((), ())), preferred_element_type=F32) * (HEAD_DIM ** -0.5)
            s = jnp.where(ok, s, NEG)
            sink = sink_ref[hd]
            m = jnp.maximum(jnp.max(s, axis=-1, keepdims=True), sink)
            p = jnp.exp(s - m)
            denom = jnp.sum(p, axis=-1, keepdims=True) + jnp.exp(sink - m)
            outs.append(jnp.dot(p.astype(BF16), vv, preferred_element_type=F32) / denom)
    o_ref[...] = jnp.concatenate(outs, axis=1).astype(o_ref.dtype)


def _swa(proj, sinks, batch, nb):
    rows = proj.shape[0]
    kvw = SWA_KV_HEADS * HEAD_DIM

    def cur(col):
        return lambda b, n: (b * nb + n, col)

    def prev(col):
        return lambda b, n: (b * nb + jnp.maximum(n - 1, 0), col)

    return pl.pallas_call(
        _swa_kernel,
        grid=(batch, nb),
        in_specs=[
            pl.BlockSpec(memory_space=pltpu.SMEM),
            pl.BlockSpec((BLK, SWA_WIDTH), cur(0)),
            pl.BlockSpec((BLK, kvw), prev(COL_KA // kvw)),
            pl.BlockSpec((BLK, kvw), cur(COL_KA // kvw)),
            pl.BlockSpec((BLK, kvw), prev(COL_VA // kvw)),
            pl.BlockSpec((BLK, kvw), cur(COL_VA // kvw)),
        ],
        out_specs=pl.BlockSpec((BLK, SWA_WIDTH), cur(0)),
        out_shape=jax.ShapeDtypeStruct((rows, SWA_WIDTH), BF16),
        compiler_params=_params(("parallel", "parallel")),
        name="swa",
    )(sinks, proj, proj, proj, proj, proj)


def _diff_kernel(lam_ref, g_ref, q_ref, k_ref, v_ref, o_ref, *, lam_init):
    n = pl.program_id(2)
    half = HEAD_DIM
    q = q_ref[...]
    qs = (q[:, :half], q[:, half:])
    row = lax.broadcasted_iota(jnp.int32, (BLK, BLK), 0)
    col = lax.broadcasted_iota(jnp.int32, (BLK, BLK), 1)
    qi = n * BLK + row

    def body(j, carry):
        start = pl.multiple_of(j * BLK, BLK)
        kb = k_ref[pl.ds(start, BLK), :]
        vb = v_ref[pl.ds(start, BLK), :]
        kj = j * BLK + col
        ok = (kj <= qi) & (kj >= PAD)
        new = []
        for s_idx in range(2):
            m, l, a = carry[s_idx]
            kx = kb[:, s_idx * half:(s_idx + 1) * half]
            s = lax.dot_general(qs[s_idx], kx, (((1,), (1,)), ((), ())), preferred_element_type=F32) * (half ** -0.5)
            s = jnp.where(ok, s, NEG)
            mn = jnp.maximum(m, jnp.max(s, axis=-1, keepdims=True))
            alpha = jnp.exp(m - mn)
            p = jnp.exp(s - mn)
            l = alpha * l + jnp.sum(p, axis=-1, keepdims=True)
            a = alpha * a + jnp.dot(p.astype(BF16), vb, preferred_element_type=F32)
            new.append((mn, l, a))
        return tuple(new)

    init = tuple((jnp.full((BLK, 1), NEG, F32), jnp.zeros((BLK, 1), F32), jnp.zeros((BLK, DIFF_V_DIM), F32))
                 for _ in range(2))
    (_, l0, a0), (_, l1, a1) = lax.fori_loop(0, n + 1, body, init)
    lv = lam_ref[...]
    lam = (jnp.exp(jnp.sum(lv[0:1] * lv[1:2], axis=-1, keepdims=True))
           - jnp.exp(jnp.sum(lv[2:3] * lv[3:4], axis=-1, keepdims=True)) + lam_init)
    o = a0 / l0 - lam * (a1 / l1)
    o = o * lax.rsqrt(jnp.mean(o * o, axis=-1, keepdims=True) + NORM_EPS) * g_ref[...]
    o_ref[...] = (o * (1.0 - lam_init)).astype(o_ref.dtype)


def _diff_attn(proj, lam_vecs, subln_g, lam_init, batch, nb):
    rows = proj.shape[0]
    lp = nb * BLK
    return pl.pallas_call(
        functools.partial(_diff_kernel, lam_init=lam_init),
        grid=(batch, DIFF_HEADS, nb),
        in_specs=[
            pl.BlockSpec((4, HEAD_DIM), lambda b, h, n: (0, 0)),
            pl.BlockSpec((1, DIFF_V_DIM), lambda b, h, n: (0, 0)),
            pl.BlockSpec((BLK, LANES), lambda b, h, n: (b * nb + n, COL_QD // LANES + h)),
            pl.BlockSpec((lp, LANES), lambda b, h, n: (b, COL_KD // LANES + h)),
            pl.BlockSpec((lp, LANES), lambda b, h, n: (b, COL_VD // LANES + h)),
        ],
        out_specs=pl.BlockSpec((BLK, DIFF_V_DIM), lambda b, h, n: (b * nb + n, h)),
        out_shape=jax.ShapeDtypeStruct((rows, DIFF_WIDTH), BF16),
        compiler_params=_params(("parallel", "parallel", "parallel")),
        name="diff_attn",
    )(lam_vecs, subln_g, proj, proj, proj)


def _ssm_row_chunk(rows):
    for c in (544, 512, 384, 256, 128, 64, 32, 16):
        if rows % c == 0:
            return c
    raise ValueError(rows)


def _ssm_kernel(u_ref, k_ref, bw_ref, cw_ref, a_ref, d_ref, o_ref, up_ref, y_ref, e_ref, s_ref, *, batch, lp):
    rows = batch * lp
    n_chunks = lp // SSM_T
    tot_chunks = rows // SSM_T
    up_ref[0:SSM_T, :] = jnp.zeros((SSM_T, LANES), F32)
    up_ref[SSM_T:, :] = u_ref[...].astype(F32)
    for b in range(batch):
        up_ref[SSM_T + b * lp:SSM_T + b * lp + PAD, :] = jnp.zeros((PAD, LANES), F32)

    ch = _ssm_row_chunk(rows)
    rmod = lax.broadcasted_iota(jnp.int32, (ch, LANES), 0) & (SSM_T - 1)
    for r0 in range(0, rows, ch):
        pieces = []
        for k in range(SSM_T):
            sl = up_ref[SSM_T - k + r0:SSM_T - k + r0 + ch, :]
            pieces.append(jnp.where(rmod >= k, sl, 0.0).astype(BF16))
        y_ref[r0:r0 + ch, :] = jnp.dot(jnp.concatenate(pieces, axis=1), k_ref[...], preferred_element_type=F32)

    pieces = [up_ref[pl.ds(SSM_T + i, tot_chunks, stride=SSM_T), :].astype(BF16) for i in range(SSM_T)]
    e_ref[...] = jnp.dot(jnp.concatenate(pieces, axis=1), bw_ref[...], preferred_element_type=F32)

    a_re = a_ref[:, :SSM_PSTATE]
    a_im = a_ref[:, SSM_PSTATE:]

    def step(c, carry):
        new = []
        for b in range(batch):
            sr, si = carry[b]
            idx = b * n_chunks + c
            s_ref[pl.ds(idx, 1), :SSM_PSTATE] = sr
            s_ref[pl.ds(idx, 1), SSM_PSTATE:] = si
            er = e_ref[pl.ds(idx, 1), :SSM_PSTATE]
            ei = e_ref[pl.ds(idx, 1), SSM_PSTATE:]
            new.append((a_re * sr - a_im * si + er, a_re * si + a_im * sr + ei))
        return tuple(new)

    zero = jnp.zeros((1, SSM_PSTATE), F32)
    lax.fori_loop(0, n_chunks, step, tuple((zero, zero) for _ in range(batch)))

    sb = s_ref[...].astype(BF16)
    for j in range(SSM_T):
        z = jnp.dot(sb, cw_ref[j], preferred_element_type=F32)
        idx = pl.ds(j, tot_chunks, stride=SSM_T)
        y_ref[idx, :] = y_ref[idx, :] + z

    y = y_ref[...] + d_ref[...] * up_ref[SSM_T:, :]
    o_ref[...] = jax.nn.gelu(y, approximate=True).astype(o_ref.dtype)


def _ssm(proj, kflat, bw, cw, a_t, dskip, batch, lp):
    rows = proj.shape[0]
    tot_chunks = rows // SSM_T
    return pl.pallas_call(
        functools.partial(_ssm_kernel, batch=batch, lp=lp),
        grid=(SSM_PACKS,),
        in_specs=[
            pl.BlockSpec((rows, LANES), lambda p: (0, COL_US // LANES + p)),
            pl.BlockSpec((None, SSM_T * LANES, LANES), lambda p: (p, 0, 0)),
            pl.BlockSpec((None, SSM_T * LANES, 2 * SSM_PSTATE), lambda p: (p, 0, 0)),
            pl.BlockSpec((None, SSM_T, 2 * SSM_PSTATE, LANES), lambda p: (p, 0, 0, 0)),
            pl.BlockSpec((None, 1, 2 * SSM_PSTATE), lambda p: (p, 0, 0)),
            pl.BlockSpec((None, 1, LANES), lambda p: (p, 0, 0)),
        ],
        out_specs=pl.BlockSpec((rows, LANES), lambda p: (0, p)),
        out_shape=jax.ShapeDtypeStruct((rows, SSM_WIDTH), F32),
        scratch_shapes=[
            pltpu.VMEM((rows + SSM_T, LANES), F32),
            pltpu.VMEM((rows, LANES), F32),
            pltpu.VMEM((tot_chunks, 2 * SSM_PSTATE), F32),
            pltpu.VMEM((tot_chunks, 2 * SSM_PSTATE), F32),
        ],
        compiler_params=_params(("parallel",)),
        name="ssm",
    )(proj, kflat, bw, cw, a_t, dskip)


def _ssm_operators(lam_re, lam_im, b_re, b_im, c_re, c_im, log_dt):
    g, p, h = SSM_GROUPS, SSM_STATE, SSM_GROUP
    dt = jnp.exp(log_dt.astype(F32))[:, None]
    lr, li = lam_re.astype(F32), lam_im.astype(F32)
    mag = jnp.exp(lr * dt)
    abar_re, abar_im = mag * jnp.cos(li * dt), mag * jnp.sin(li * dt)
    den = lr * lr + li * li
    nr = abar_re - 1.0
    f_re = (nr * lr + abar_im * li) / den
    f_im = (abar_im * lr - nr * li) / den
    br, bi = b_re.astype(F32), b_im.astype(F32)
    bbar_re = f_re[..., None] * br - f_im[..., None] * bi
    bbar_im = f_re[..., None] * bi + f_im[..., None] * br
    ks = jnp.arange(SSM_T + 1, dtype=F32)[:, None, None]
    pmag = jnp.exp(ks * (lr * dt)[None])
    pw_re = pmag * jnp.cos(ks * (li * dt)[None])
    pw_im = pmag * jnp.sin(ks * (li * dt)[None])
    cr, ci = c_re.astype(F32), c_im.astype(F32)

    eye = jnp.eye(SSM_PACK, dtype=F32)

    def block_diag(x):
        lead = x.shape[:-3]
        m, n = x.shape[-2:]
        x = x.reshape(lead + (SSM_PACKS, SSM_PACK, m, n))
        x = jnp.einsum('...pqmn,qr->p...qmrn', x, eye)
        return x.reshape((SSM_PACKS,) + lead + (SSM_PACK * m, SSM_PACK * n))

    cb_re = cr[:, :, :, None] * bbar_re[:, None, :, :] - ci[:, :, :, None] * bbar_im[:, None, :, :]
    cb_im = cr[:, :, :, None] * bbar_im[:, None, :, :] + ci[:, :, :, None] * bbar_re[:, None, :, :]
    klag = (jnp.einsum('kgs,gosi->kgio', pw_re[:SSM_T], cb_re)
            - jnp.einsum('kgs,gosi->kgio', pw_im[:SSM_T], cb_im))
    kflat = block_diag(klag).reshape(SSM_PACKS, SSM_T * LANES, LANES)

    rev_re, rev_im = pw_re[SSM_T - 1::-1], pw_im[SSM_T - 1::-1]
    inj_re = rev_re[..., None] * bbar_re[None] - rev_im[..., None] * bbar_im[None]
    inj_im = rev_re[..., None] * bbar_im[None] + rev_im[..., None] * bbar_re[None]
    inj_re = block_diag(jnp.swapaxes(inj_re, -1, -2))
    inj_im = block_diag(jnp.swapaxes(inj_im, -1, -2))
    bw = jnp.concatenate([inj_re, inj_im], axis=-1).reshape(SSM_PACKS, SSM_T * LANES, 2 * SSM_PSTATE)

    ca_re = cr[None] * pw_re[1:, :, None, :] - ci[None] * pw_im[1:, :, None, :]
    ca_im = cr[None] * pw_im[1:, :, None, :] + ci[None] * pw_re[1:, :, None, :]
    out_re = block_diag(jnp.swapaxes(ca_re, -1, -2))
    out_im = block_diag(jnp.swapaxes(-ca_im, -1, -2))
    cw = jnp.concatenate([out_re, out_im], axis=-2)

    a_t = jnp.concatenate([pw_re[SSM_T].reshape(SSM_PACKS, 1, SSM_PSTATE),
                           pw_im[SSM_T].reshape(SSM_PACKS, 1, SSM_PSTATE)], axis=-1)
    return kflat.astype(BF16), bw.astype(BF16), cw.astype(BF16), a_t


def _glu_kernel(y_ref, w_ref, o_ref):
    y = y_ref[...]
    gate = jnp.dot(y.astype(BF16), w_ref[...], preferred_element_type=F32)
    o_ref[...] = (y * jax.nn.sigmoid(gate)).astype(o_ref.dtype)


def _glu(y, w):
    rows = y.shape[0]
    tm = _row_tile(rows)
    return pl.pallas_call(
        _glu_kernel,
        grid=(rows // tm,),
        in_specs=[pl.BlockSpec((tm, SSM_WIDTH), lambda i: (i, 0)),
                  pl.BlockSpec((SSM_WIDTH, SSM_WIDTH), lambda i: (0, 0))],
        out_specs=pl.BlockSpec((tm, SSM_WIDTH), lambda i: (i, 0)),
        out_shape=jax.ShapeDtypeStruct((rows, SSM_WIDTH), BF16),
        compiler_params=_params(("parallel",)),
        name="ssm_glu",
    )(y, w)


OUT_TN = 1024


def _outproj_kernel(h_ref, a_ref, b_ref, c_ref, wa_ref, wb_ref, wc_ref, o_ref):
    acc = jnp.dot(a_ref[...], wa_ref[...], preferred_element_type=F32)
    acc += jnp.dot(b_ref[...], wb_ref[...], preferred_element_type=F32)
    acc += jnp.dot(c_ref[...], wc_ref[...], preferred_element_type=F32)
    o_ref[...] = h_ref[...] + acc


def _outproj(h, oa, ob, oc, wa, wb, wc):
    rows = h.shape[0]
    tm = _row_tile(rows)
    return pl.pallas_call(
        _outproj_kernel,
        grid=(rows // tm, D_MODEL // OUT_TN),
        in_specs=[
            pl.BlockSpec((tm, OUT_TN), lambda i, j: (i, j)),
            pl.BlockSpec((tm, SWA_WIDTH), lambda i, j: (i, 0)),
            pl.BlockSpec((tm, SSM_WIDTH), lambda i, j: (i, 0)),
            pl.BlockSpec((tm, DIFF_WIDTH), lambda i, j: (i, 0)),
            pl.BlockSpec((SWA_WIDTH, OUT_TN), lambda i, j: (0, j)),
            pl.BlockSpec((SSM_WIDTH, OUT_TN), lambda i, j: (0, j)),
            pl.BlockSpec((DIFF_WIDTH, OUT_TN), lambda i, j: (0, j)),
        ],
        out_specs=pl.BlockSpec((tm, OUT_TN), lambda i, j: (i, j)),
        out_shape=jax.ShapeDtypeStruct((rows, D_MODEL), F32),
        compiler_params=_params(("parallel", "parallel")),
        name="outproj",
    )(h, oa, ob, oc, wa, wb, wc)


def _router_kernel(h_ref, g_ref, rw_ref, rb_ref, hn_ref, wt_ref, id_ref):
    x = h_ref[...]
    ms = jnp.mean(x * x, axis=-1, keepdims=True)
    hn = x * lax.rsqrt(ms + NORM_EPS) * g_ref[...]
    hn_ref[...] = hn
    logits = jnp.dot(hn, rw_ref[...], preferred_element_type=F32, precision=lax.Precision.HIGHEST) + rb_ref[...]
    lane = lax.broadcasted_iota(jnp.int32, logits.shape, 1)

    def first_argmax(v, vmax):
        return jnp.min(jnp.where(v == vmax, lane, LANES), axis=-1, keepdims=True)

    gl = jnp.where(lane < N_EXPERT_GROUPS, logits, NEG)
    gmax = jnp.max(gl, axis=-1, keepdims=True)
    g_sel = first_argmax(gl, gmax)
    p_g = 1.0 / jnp.sum(jnp.exp(gl - gmax), axis=-1, keepdims=True)
    e_lane = lane - N_EXPERT_GROUPS
    in_group = (e_lane >= 0) & (e_lane < N_EXPERTS) & (jnp.right_shift(e_lane, 3) == g_sel)
    el = jnp.where(in_group, logits, NEG)
    m1 = jnp.max(el, axis=-1, keepdims=True)
    i1 = first_argmax(el, m1)
    el2 = jnp.where(lane == i1, NEG, el)
    m2 = jnp.max(el2, axis=-1, keepdims=True)
    i2 = first_argmax(el2, m2)
    z = jnp.sum(jnp.exp(el - m1), axis=-1, keepdims=True)
    p1 = 1.0 / z
    p2 = jnp.exp(m2 - m1) / z
    w1 = p1 / (p1 + p2) * p_g
    w2 = p2 / (p1 + p2) * p_g
    wt_ref[...] = jnp.where(lane == 0, w1, jnp.where(lane == 1, w2, 0.0))
    id_ref[...] = jnp.where(lane == 0, i1 - N_EXPERT_GROUPS, jnp.where(lane == 1, i2 - N_EXPERT_GROUPS, 0))


def _router(h, g, rw, rb):
    rows = h.shape[0]
    tm = _row_tile(rows)
    return pl.pallas_call(
        _router_kernel,
        grid=(rows // tm,),
        in_specs=[
            pl.BlockSpec((tm, D_MODEL), lambda i: (i, 0)),
            pl.BlockSpec((1, D_MODEL), lambda i: (0, 0)),
            pl.BlockSpec((D_MODEL, LANES), lambda i: (0, 0)),
            pl.BlockSpec((1, LANES), lambda i: (0, 0)),
        ],
        out_specs=[
            pl.BlockSpec((tm, D_MODEL), lambda i: (i, 0)),
            pl.BlockSpec((tm, LANES), lambda i: (i, 0)),
            pl.BlockSpec((tm, LANES), lambda i: (i, 0)),
        ],
        out_shape=[
            jax.ShapeDtypeStruct((rows, D_MODEL), F32),
            jax.ShapeDtypeStruct((rows, LANES), F32),
            jax.ShapeDtypeStruct((rows, LANES), jnp.int32),
        ],
        compiler_params=_params(("parallel",)),
        name="router",
    )(h, g, rw, rb)


def _dispatch_plan(ids, n_blk):
    e = ids[:, :2].reshape(-1)
    n_asg = e.shape[0]
    onehot = (e[:, None] == jnp.arange(N_EXPERTS, dtype=jnp.int32)[None, :]).astype(jnp.int32)
    csum = jnp.cumsum(onehot, axis=0)
    counts = csum[-1]
    rank = jnp.take_along_axis(csum, e[:, None], axis=1)[:, 0] - 1
    padded = (counts + MOE_BM - 1) // MOE_BM * MOE_BM
    pend = jnp.cumsum(padded)
    pstart = pend - padded
    dest = (pstart[e] + rank).astype(jnp.int32)
    src = jnp.zeros((n_blk * MOE_BM,), jnp.int32).at[dest].set(jnp.arange(n_asg, dtype=jnp.int32) // 2)
    blk_expert = jnp.clip(jnp.searchsorted(pend, jnp.arange(n_blk, dtype=jnp.int32) * MOE_BM, side='right'),
                          0, N_EXPERTS - 1).astype(jnp.int32)
    n_used = (pend[-1] // MOE_BM).astype(jnp.int32).reshape(1)
    return dest, src, blk_expert, n_used


def _row_gather_copy(src_hbm, dst_buf, sem, src_row, dst_row):
    return pltpu.make_async_copy(src_hbm.at[pl.ds(src_row, 1), :], dst_buf.at[pl.ds(dst_row, 1), :], sem)


def _block_wait_copy(src_hbm, dst_buf, sem):
    return pltpu.make_async_copy(src_hbm.at[pl.ds(0, dst_buf.shape[0]), :], dst_buf, sem)


def _expert_kernel(be_ref, src_ref, nused_ref, hn_hbm, wg_ref, wu_ref, wd_ref, o_ref, xbuf, sem):
    i = pl.program_id(0)
    n_used = nused_ref[0]
    slot = i % 2

    def start_gather(blk, slot_):
        def issue(r, carry):
            _row_gather_copy(hn_hbm, xbuf.at[slot_], sem.at[slot_], src_ref[blk * MOE_BM + r], r).start()
            return carry
        lax.fori_loop(0, MOE_BM, issue, 0, unroll=8)

    @pl.when((i == 0) & (n_used > 0))
    def _():
        start_gather(0, 0)

    @pl.when(i + 1 < n_used)
    def _():
        start_gather(i + 1, 1 - slot)

    @pl.when(i < n_used)
    def _():
        _block_wait_copy(hn_hbm, xbuf.at[slot], sem.at[slot]).wait()
        x = xbuf[slot].astype(BF16)
        gate = jnp.dot(x, wg_ref[...], preferred_element_type=F32)
        up = jnp.dot(x, wu_ref[...], preferred_element_type=F32)
        act = (jax.nn.silu(gate) * up).astype(BF16)
        o_ref[...] = jnp.dot(act, wd_ref[...], preferred_element_type=F32)

    @pl.when(i >= n_used)
    def _():
        o_ref[...] = jnp.zeros(o_ref.shape, o_ref.dtype)


def _experts(hn, blk_expert, src, n_used, wg, wu, wd, n_blk):
    grid_spec = pltpu.PrefetchScalarGridSpec(
        num_scalar_prefetch=3,
        grid=(n_blk,),
        in_specs=[
            pl.BlockSpec(memory_space=pl.ANY),
            pl.BlockSpec((None, D_MODEL, D_EXPERT), lambda i, be, s, nu: (be[i], 0, 0)),
            pl.BlockSpec((None, D_MODEL, D_EXPERT), lambda i, be, s, nu: (be[i], 0, 0)),
            pl.BlockSpec((None, D_EXPERT, D_MODEL), lambda i, be, s, nu: (be[i], 0, 0)),
        ],
        out_specs=pl.BlockSpec((MOE_BM, D_MODEL), lambda i, be, s, nu: (i, 0)),
        scratch_shapes=[pltpu.VMEM((2, MOE_BM, D_MODEL), F32), pltpu.SemaphoreType.DMA((2,))],
    )
    return pl.pallas_call(
        _expert_kernel,
        grid_spec=grid_spec,
        out_shape=jax.ShapeDtypeStruct((n_blk * MOE_BM, D_MODEL), F32),
        compiler_params=_params(("arbitrary",)),
        name="moe_experts",
    )(blk_expert, src, n_used, hn, wg, wu, wd)


def _combine_kernel(dest_ref, h_ref, wt_ref, y_hbm, o_ref, ybuf, sem):
    i = pl.program_id(0)
    n = pl.num_programs(0)
    slot = i % 2

    def start_gather(tile, slot_):
        def issue(r, carry):
            t = tile * MOE_BM + r
            for k in range(2):
                _row_gather_copy(y_hbm, ybuf.at[slot_, k], sem.at[slot_], dest_ref[2 * t + k], r).start()
            return carry
        lax.fori_loop(0, MOE_BM, issue, 0, unroll=8)

    @pl.when(i == 0)
    def _():
        start_gather(0, 0)

    @pl.when(i + 1 < n)
    def _():
        start_gather(i + 1, 1 - slot)

    for k in range(2):
        _block_wait_copy(y_hbm, ybuf.at[slot, k], sem.at[slot]).wait()
    wt = wt_ref[...]
    o_ref[...] = h_ref[...] + wt[:, 0:1] * ybuf[slot, 0] + wt[:, 1:2] * ybuf[slot, 1]


def _combine(h, wts, ybuf, dest):
    rows = h.shape[0]
    grid_spec = pltpu.PrefetchScalarGridSpec(
        num_scalar_prefetch=1,
        grid=(rows // MOE_BM,),
        in_specs=[
            pl.BlockSpec((MOE_BM, D_MODEL), lambda i, d: (i, 0)),
            pl.BlockSpec((MOE_BM, LANES), lambda i, d: (i, 0)),
            pl.BlockSpec(memory_space=pl.ANY),
        ],
        out_specs=pl.BlockSpec((MOE_BM, D_MODEL), lambda i, d: (i, 0)),
        scratch_shapes=[pltpu.VMEM((2, 2, MOE_BM, D_MODEL), F32), pltpu.SemaphoreType.DMA((2,))],
    )
    return pl.pallas_call(
        _combine_kernel,
        grid_spec=grid_spec,
        out_shape=jax.ShapeDtypeStruct((rows, D_MODEL), F32),
        compiler_params=_params(("arbitrary",)),
        name="moe_combine",
    )(dest, h, wts, ybuf)


def _moe(h, g, rw, rb, wg, wu, wd):
    rows = h.shape[0]
    n_blk = -(-(2 * rows) // MOE_BM) + N_EXPERTS
    hn, wts, ids = _router(h, g, rw, rb)
    dest, src, blk_expert, n_used = _dispatch_plan(ids, n_blk)
    ybuf = _experts(hn, blk_expert, src, n_used, wg, wu, wd, n_blk)
    return _combine(h, wts, ybuf, dest)


def _final_norm_kernel(x_ref, g_ref, o_ref):
    x = x_ref[...]
    ms = jnp.mean(x * x, axis=-1, keepdims=True)
    o_ref[...] = x * lax.rsqrt(ms + NORM_EPS) * g_ref[...]


def _final_norm(h, g, batch, nb):
    seq_blocks = nb - 1
    return pl.pallas_call(
        _final_norm_kernel,
        grid=(batch, seq_blocks),
        in_specs=[pl.BlockSpec((BLK, D_MODEL), lambda b, j: (b * nb + 1 + j, 0)),
                  pl.BlockSpec((1, D_MODEL), lambda b, j: (0, 0))],
        out_specs=pl.BlockSpec((BLK, D_MODEL), lambda b, j: (b * seq_blocks + j, 0)),
        out_shape=jax.ShapeDtypeStruct((batch * seq_blocks * BLK, D_MODEL), F32),
        compiler_params=_params(("parallel", "parallel")),
        name="final_norm",
    )(h, g)


def _rope_tables(batch, lp):
    half = HEAD_DIM // 2
    inv = ROPE_THETA ** (-jnp.arange(half, dtype=F32) / half)
    pos = (jnp.arange(lp, dtype=jnp.int32) - PAD).astype(F32)
    ang = pos[:, None] * inv[None, :]
    cos = jnp.tile(jnp.cos(ang), (batch, LANES // half))
    sin = jnp.sin(ang)
    sin = jnp.tile(jnp.concatenate([-sin, sin], axis=1), (batch, LANES // HEAD_DIM))
    return cos, sin


def kernel(x, meta_tokens, attn_norm_g, ffn_norm_g, w_in, swa_sinks, ssm_lambda_re, ssm_lambda_im, ssm_b_re, ssm_b_im, ssm_c_re, ssm_c_im, ssm_d, ssm_log_dt, ssm_w_glu, diff_lambda_q1, diff_lambda_k1, diff_lambda_q2, diff_lambda_k2, diff_subln_g, w_out, router_group_w, router_group_b, router_expert_w, router_expert_b, moe_w_gate, moe_w_up, moe_w_down, final_norm_g):
    batch, seq, d = x.shape
    depth = w_in.shape[0]
    lp = PAD + N_META + seq
    nb = lp // BLK
    meta = jnp.broadcast_to(meta_tokens[None].astype(x.dtype), (batch, N_META, d))
    h = jnp.concatenate([jnp.zeros((batch, PAD, d), x.dtype), meta, x], axis=1).reshape(batch * lp, d)
    cos, sin = _rope_tables(batch, lp)
    for l in range(depth):
        proj = _norm_inproj(h, attn_norm_g[l][None], w_in[l].astype(BF16), cos, sin)
        o_a = _swa(proj, swa_sinks[l].astype(F32), batch, nb)
        kflat, bw, cw, a_t = _ssm_operators(ssm_lambda_re[l], ssm_lambda_im[l], ssm_b_re[l], ssm_b_im[l],
                                            ssm_c_re[l], ssm_c_im[l], ssm_log_dt[l])
        yg = _ssm(proj, kflat, bw, cw, a_t, ssm_d[l].astype(F32).reshape(SSM_PACKS, 1, LANES), batch, lp)
        o_b = _glu(yg, ssm_w_glu[l].astype(BF16))
        lam_init = 0.8 - 0.6 * math.exp(-0.3 * l)
        lam_vecs = jnp.stack([diff_lambda_q1[l], diff_lambda_k1[l], diff_lambda_q2[l], diff_lambda_k2[l]]).astype(F32)
        o_c = _diff_attn(proj, lam_vecs, diff_subln_g[l][None].astype(F32), lam_init, batch, nb)
        wo = w_out[l].astype(BF16)
        h = _outproj(h, o_a, o_b, o_c, wo[:SWA_WIDTH], wo[SWA_WIDTH:SWA_WIDTH + SSM_WIDTH],
                     wo[SWA_WIDTH + SSM_WIDTH:])
        rw = jnp.concatenate([router_group_w[l], router_expert_w[l]], axis=1).astype(F32)
        rw = jnp.pad(rw, ((0, 0), (0, LANES - rw.shape[1])))
        rb = jnp.concatenate([router_group_b[l], router_expert_b[l]]).astype(F32)
        rb = jnp.pad(rb, (0, LANES - rb.shape[0]))[None]
        h = _moe(h, ffn_norm_g[l][None], rw, rb, moe_w_gate[l].astype(BF16), moe_w_up[l].astype(BF16),
                 moe_w_down[l].astype(BF16))
    out = _final_norm(h, final_norm_g[None], batch, nb)
    return out.reshape(batch, seq, d)
```

```python
import functools
import math

import jax
import jax.numpy as jnp
from jax import lax
from jax.experimental import pallas as pl
from jax.experimental.pallas import tpu as pltpu

D_MODEL = 2048
N_META = 16
BLK = 128
PAD = BLK - N_META
ROPE_THETA = 10000.0
NORM_EPS = 1e-5
NEG = -1e30
SWA_HEADS = 12
SWA_KV_HEADS = 4
HEAD_DIM = 64
SWA_WIDTH = SWA_HEADS * HEAD_DIM
SSM_WIDTH = 768
SSM_GROUP = 16
SSM_GROUPS = SSM_WIDTH // SSM_GROUP
SSM_STATE = 64
DIFF_HEADS = 4
DIFF_V_DIM = 128
DIFF_WIDTH = DIFF_HEADS * DIFF_V_DIM
IN_WIDTH = 3584
N_EXPERT_GROUPS = 4
EXPERTS_PER_GROUP = 8
N_EXPERTS = N_EXPERT_GROUPS * EXPERTS_PER_GROUP
D_EXPERT = 512

COL_KA = 768
COL_VA = 1024
COL_US = 1280
COL_QD = 2048
COL_KD = 2560
COL_VD = 3072

LANES = 128
SSM_T = 16
SSM_PACK = LANES // SSM_GROUP
SSM_PACKS = SSM_GROUPS // SSM_PACK
SSM_PSTATE = SSM_PACK * SSM_STATE
MOE_BM = 128
VMEM_LIMIT = 56 * 1024 * 1024

F32 = jnp.float32
BF16 = jnp.bfloat16


def _row_tile(rows):
    for t in (512, 256, 128):
        if rows % t == 0:
            return t
    raise ValueError(f"row count {rows} is not a multiple of 128")


def _params(sem, vmem=VMEM_LIMIT):
    return pltpu.CompilerParams(dimension_semantics=sem, vmem_limit_bytes=vmem)


IN_TN = 512
ROPE_TILES = (0, 1, 4, 5)


def _norm_inproj_kernel(x_ref, g_ref, w_ref, cos_ref, sin_ref, o_ref, xn_ref):
    j = pl.program_id(1)

    @pl.when(j == 0)
    def _():
        x = x_ref[...]
        ms = jnp.mean(x * x, axis=-1, keepdims=True)
        xn_ref[...] = (x * lax.rsqrt(ms + NORM_EPS) * g_ref[...]).astype(BF16)

    acc = jnp.dot(xn_ref[...], w_ref[...], preferred_element_type=F32)
    is_rope = functools.reduce(jnp.logical_or, [j == t for t in ROPE_TILES])

    @pl.when(is_rope)
    def _():
        width = acc.shape[1]
        lane = lax.broadcasted_iota(jnp.int32, acc.shape, 1)
        first_half = (lane & (HEAD_DIM - 1)) < (HEAD_DIM // 2)
        partner = jnp.where(first_half, pltpu.roll(acc, width - HEAD_DIM // 2, 1), pltpu.roll(acc, HEAD_DIM // 2, 1))
        reps = width // LANES
        cos = jnp.tile(cos_ref[...], (1, reps))
        sin = jnp.tile(sin_ref[...], (1, reps))
        o_ref[...] = (acc * cos + partner * sin).astype(o_ref.dtype)

    @pl.when(jnp.logical_not(is_rope))
    def _():
        o_ref[...] = acc.astype(o_ref.dtype)


def _norm_inproj(h, g, w, cos, sin):
    rows = h.shape[0]
    tm = _row_tile(rows)
    return pl.pallas_call(
        _norm_inproj_kernel,
        grid=(rows // tm, IN_WIDTH // IN_TN),
        in_specs=[
            pl.BlockSpec((tm, D_MODEL), lambda i, j: (i, 0)),
            pl.BlockSpec((1, D_MODEL), lambda i, j: (0, 0)),
            pl.BlockSpec((D_MODEL, IN_TN), lambda i, j: (0, j)),
            pl.BlockSpec((tm, LANES), lambda i, j: (i, 0)),
            pl.BlockSpec((tm, LANES), lambda i, j: (i, 0)),
        ],
        out_specs=pl.BlockSpec((tm, IN_TN), lambda i, j: (i, j)),
        out_shape=jax.ShapeDtypeStruct((rows, IN_WIDTH), BF16),
        scratch_shapes=[pltpu.VMEM((tm, D_MODEL), BF16)],
        compiler_params=_params(("parallel", "arbitrary")),
        name="norm_inproj",
    )(h, g, w, cos, sin)


def _swa_kernel(sink_ref, q_ref, kp_ref, kc_ref, vp_ref, vc_ref, o_ref):
    n = pl.program_id(1)
    shape = (BLK, 2 * BLK)
    qi = lax.broadcasted_iota(jnp.int32, shape, 0)
    c = lax.broadcasted_iota(jnp.int32, shape, 1)
    kj = (n - 1) * BLK + c
    delta = qi + BLK - c
    ok = (delta >= 0) & (delta < BLK) & (kj >= PAD)
    grp = SWA_HEADS // SWA_KV_HEADS
    outs = []
    for j in range(SWA_KV_HEADS):
        cols = slice(j * HEAD_DIM, (j + 1) * HEAD_DIM)
        kk = jnp.concatenate([kp_ref[:, cols], kc_ref[:, cols]], axis=0)
        vv = jnp.concatenate([vp_ref[:, cols], vc_ref[:, cols]], axis=0)
        for g in range(grp):
            hd = j * grp + g
            qh = q_ref[:, hd * HEAD_DIM:(hd + 1) * HEAD_DIM]
            s = lax.dot_general(qh, kk, (((1,), (1,)), ((), ())), preferred_element_type=F32) * (HEAD_DIM ** -0.5)
            s = jnp.where(ok, s, NEG)
            sink = sink_ref[hd]
            m = jnp.maximum(jnp.max(s, axis=-1, keepdims=True), sink)
            p = jnp.exp(s - m)
            denom = jnp.sum(p, axis=-1, keepdims=True) + jnp.exp(sink - m)
            outs.append(jnp.dot(p.astype(BF16), vv, preferred_element_type=F32) / denom)
    o_ref[...] = jnp.concatenate(outs, axis=1).astype(o_ref.dtype)


def _swa(proj, sinks, batch, nb):
    rows = proj.shape[0]
    kvw = SWA_KV_HEADS * HEAD_DIM

    def cur(col):
        return lambda b, n: (b * nb + n, col)

    def prev(col):
        return lambda b, n: (b * nb + jnp.maximum(n - 1, 0), col)

    return pl.pallas_call(
        _swa_kernel,
        grid=(batch, nb),
        in_specs=[
            pl.BlockSpec(memory_space=pltpu.SMEM),
            pl.BlockSpec((BLK, SWA_WIDTH), cur(0)),
            pl.BlockSpec((BLK, kvw), prev(COL_KA // kvw)),
            pl.BlockSpec((BLK, kvw), cur(COL_KA // kvw)),
            pl.BlockSpec((BLK, kvw), prev(COL_VA // kvw)),
            pl.BlockSpec((BLK, kvw), cur(COL_VA // kvw)),
        ],
        out_specs=pl.BlockSpec((BLK, SWA_WIDTH), cur(0)),
        out_shape=jax.ShapeDtypeStruct((rows, SWA_WIDTH), BF16),
        compiler_params=_params(("parallel", "parallel")),
        name="swa",
    )(sinks, proj, proj, proj, proj, proj)


def _diff_kernel(lam_ref, g_ref, q_ref, k_ref, v_ref, o_ref, *, lam_init, nb):
    row = lax.broadcasted_iota(jnp.int32, (BLK, BLK), 0)
    col = lax.broadcasted_iota(jnp.int32, (BLK, BLK), 1)
    causal_ok = col <= row
    pad_ok = col >= PAD
    lane = lax.broadcasted_iota(jnp.int32, (BLK, LANES), 1)
    lv = lam_ref[...]
    lam = (jnp.exp(jnp.sum(lv[0:1] * lv[1:2], axis=-1, keepdims=True))
           - jnp.exp(jnp.sum(lv[2:3] * lv[3:4], axis=-1, keepdims=True)) + lam_init)
    gain = g_ref[...] * (1.0 - lam_init)
    scale = HEAD_DIM ** -0.5
    for n in range(nb):
        ke = (n + 1) * BLK
        q = q_ref[n * BLK:(n + 1) * BLK, :] * scale
        kx = k_ref[0:ke, :]
        vx = v_ref[0:ke, :]
        heads = []
        for s_idx in range(2):
            qx = jnp.where((lane >= HEAD_DIM) == bool(s_idx), q, jnp.zeros_like(q))
            s = lax.dot_general(qx, kx, (((1,), (1,)), ((), ())), preferred_element_type=F32)
            if n == 0:
                s = jnp.where(causal_ok & pad_ok, s, NEG)
            else:
                first = jnp.where(pad_ok, s[:, :BLK], NEG)
                last = jnp.where(causal_ok, s[:, ke - BLK:], NEG)
                middle = [s[:, BLK:ke - BLK]] if n > 1 else []
                s = jnp.concatenate([first] + middle + [last], axis=1)
            m = jnp.max(s, axis=-1, keepdims=True)
            p = jnp.exp(s - m)
            denom = jnp.sum(p, axis=-1, keepdims=True)
            heads.append(jnp.dot(p.astype(BF16), vx, preferred_element_type=F32) / denom)
        o = heads[0] - lam * heads[1]
        o = o * lax.rsqrt(jnp.mean(o * o, axis=-1, keepdims=True) + NORM_EPS) * gain
        o_ref[n * BLK:(n + 1) * BLK, :] = o.astype(o_ref.dtype)


def _diff_attn(proj, lam_vecs, subln_g, lam_init, batch, nb):
    rows = proj.shape[0]
    lp = nb * BLK
    return pl.pallas_call(
        functools.partial(_diff_kernel, lam_init=lam_init, nb=nb),
        grid=(batch, DIFF_HEADS),
        in_specs=[
            pl.BlockSpec((4, HEAD_DIM), lambda b, h: (0, 0)),
            pl.BlockSpec((1, DIFF_V_DIM), lambda b, h: (0, 0)),
            pl.BlockSpec((lp, LANES), lambda b, h: (b, COL_QD // LANES + h)),
            pl.BlockSpec((lp, LANES), lambda b, h: (b, COL_KD // LANES + h)),
            pl.BlockSpec((lp, LANES), lambda b, h: (b, COL_VD // LANES + h)),
        ],
        out_specs=pl.BlockSpec((lp, DIFF_V_DIM), lambda b, h: (b, h)),
        out_shape=jax.ShapeDtypeStruct((rows, DIFF_WIDTH), BF16),
        compiler_params=_params(("parallel", "parallel")),
        name="diff_attn",
    )(lam_vecs, subln_g, proj, proj, proj)


def _ssm_row_chunk(rows):
    for c in (544, 512, 384, 256, 128, 64, 32, 16):
        if rows % c == 0:
            return c
    raise ValueError(rows)


def _ssm_expand_operators(kc_ref, bc_ref, cc_ref, k_ref, bw_ref, cw_ref):
    def iota(shape, dim):
        return lax.broadcasted_iota(jnp.int32, shape, dim)

    def group(idx, width):
        return jnp.right_shift(idx, width.bit_length() - 1) & (SSM_PACK - 1)

    hs, ps, tl = SSM_GROUP, SSM_STATE, SSM_T * LANES
    spread_h = ((iota((LANES, hs), 0) & (hs - 1)) == iota((LANES, hs), 1)).astype(BF16)
    kt = jnp.dot(spread_h, kc_ref[...], preferred_element_type=F32)
    same = group(iota((LANES, tl), 0), hs) == group(iota((LANES, tl), 1), hs)
    k_ref[...] = jnp.where(same, kt, 0.0).astype(BF16)
    r = iota((2 * ps, 2 * SSM_PSTATE), 0)
    c = iota((2 * ps, 2 * SSM_PSTATE), 1)
    spread_s = ((jnp.right_shift(r, 6) == jnp.right_shift(c, 9)) & ((r & (ps - 1)) == (c & (ps - 1)))).astype(BF16)
    bw = jnp.dot(bc_ref[...], spread_s, preferred_element_type=F32)
    shape = (tl, 2 * SSM_PSTATE)
    same = group(iota(shape, 0), hs) == group(iota(shape, 1), ps)
    bw_ref[...] = jnp.where(same, bw, 0.0).astype(BF16)
    shape = (LANES, 2 * SSM_PSTATE)
    same = group(iota(shape, 0), hs) == group(iota(shape, 1), ps)
    for j in range(SSM_T):
        ct = jnp.dot(spread_h, cc_ref[j], preferred_element_type=F32)
        cw_ref[j] = jnp.where(same, ct, 0.0).astype(BF16)


def _ssm_kernel(u_ref, kc_ref, bc_ref, cc_ref, a_ref, d_ref, o_ref, up_ref, y_ref, e_ref, s_ref, k_ref, bw_ref,
                cw_ref, *, batch, lp):
    rows = batch * lp
    n_chunks = lp // SSM_T
    tot_chunks = rows // SSM_T
    _ssm_expand_operators(kc_ref, bc_ref, cc_ref, k_ref, bw_ref, cw_ref)
    contract_lanes = (((1,), (1,)), ((), ()))
    up_ref[0:SSM_T, :] = jnp.zeros((SSM_T, LANES), F32)
    up_ref[SSM_T:, :] = u_ref[...].astype(F32)
    for b in range(batch):
        up_ref[SSM_T + b * lp:SSM_T + b * lp + PAD, :] = jnp.zeros((PAD, LANES), F32)

    ch = _ssm_row_chunk(rows)
    rmod = lax.broadcasted_iota(jnp.int32, (ch, LANES), 0) & (SSM_T - 1)
    for r0 in range(0, rows, ch):
        pieces = []
        for k in range(SSM_T):
            sl = up_ref[SSM_T - k + r0:SSM_T - k + r0 + ch, :]
            pieces.append(jnp.where(rmod >= k, sl, 0.0).astype(BF16))
        y_ref[r0:r0 + ch, :] = lax.dot_general(jnp.concatenate(pieces, axis=1), k_ref[...], contract_lanes,
                                               preferred_element_type=F32)

    pieces = [up_ref[pl.ds(SSM_T + i, tot_chunks, stride=SSM_T), :].astype(BF16) for i in range(SSM_T)]
    e_ref[...] = jnp.dot(jnp.concatenate(pieces, axis=1), bw_ref[...], preferred_element_type=F32)

    a_re = a_ref[:, :SSM_PSTATE]
    a_im = a_ref[:, SSM_PSTATE:]

    def step(c, carry):
        new = []
        for b in range(batch):
            sr, si = carry[b]
            idx = b * n_chunks + c
            s_ref[pl.ds(idx, 1), :SSM_PSTATE] = sr
            s_ref[pl.ds(idx, 1), SSM_PSTATE:] = si
            er = e_ref[pl.ds(idx, 1), :SSM_PSTATE]
            ei = e_ref[pl.ds(idx, 1), SSM_PSTATE:]
            new.append((a_re * sr - a_im * si + er, a_re * si + a_im * sr + ei))
        return tuple(new)

    zero = jnp.zeros((1, SSM_PSTATE), F32)
    lax.fori_loop(0, n_chunks, step, tuple((zero, zero) for _ in range(batch)))

    sb = s_ref[...].astype(BF16)
    for j in range(SSM_T):
        z = lax.dot_general(sb, cw_ref[j], contract_lanes, preferred_element_type=F32)
        idx = pl.ds(j, tot_chunks, stride=SSM_T)
        y_ref[idx, :] = y_ref[idx, :] + z

    y = y_ref[...] + d_ref[...] * up_ref[SSM_T:, :]
    o_ref[...] = jax.nn.gelu(y, approximate=True).astype(o_ref.dtype)


def _ssm(proj, kc, bc, cc, a_t, dskip, batch, lp):
    rows = proj.shape[0]
    tot_chunks = rows // SSM_T
    return pl.pallas_call(
        functools.partial(_ssm_kernel, batch=batch, lp=lp),
        grid=(SSM_PACKS,),
        in_specs=[
            pl.BlockSpec((rows, LANES), lambda p: (0, COL_US // LANES + p)),
            pl.BlockSpec((None, SSM_GROUP, SSM_T * LANES), lambda p: (p, 0, 0)),
            pl.BlockSpec((None, SSM_T * LANES, 2 * SSM_STATE), lambda p: (p, 0, 0)),
            pl.BlockSpec((None, SSM_T, SSM_GROUP, 2 * SSM_PSTATE), lambda p: (p, 0, 0, 0)),
            pl.BlockSpec((None, 1, 2 * SSM_PSTATE), lambda p: (p, 0, 0)),
            pl.BlockSpec((None, 1, LANES), lambda p: (p, 0, 0)),
        ],
        out_specs=pl.BlockSpec((rows, LANES), lambda p: (0, p)),
        out_shape=jax.ShapeDtypeStruct((rows, SSM_WIDTH), F32),
        scratch_shapes=[
            pltpu.VMEM((rows + SSM_T, LANES), F32),
            pltpu.VMEM((rows, LANES), F32),
            pltpu.VMEM((tot_chunks, 2 * SSM_PSTATE), F32),
            pltpu.VMEM((tot_chunks, 2 * SSM_PSTATE), F32),
            pltpu.VMEM((LANES, SSM_T * LANES), BF16),
            pltpu.VMEM((SSM_T * LANES, 2 * SSM_PSTATE), BF16),
            pltpu.VMEM((SSM_T, LANES, 2 * SSM_PSTATE), BF16),
        ],
        compiler_params=_params(("parallel",)),
        name="ssm",
    )(proj, kc, bc, cc, a_t, dskip)


def _ssm_operators(lam_re, lam_im, b_re, b_im, c_re, c_im, log_dt):
    p, h = SSM_STATE, SSM_GROUP
    dt = jnp.exp(log_dt.astype(F32))[:, None]
    lr, li = lam_re.astype(F32), lam_im.astype(F32)
    mag = jnp.exp(lr * dt)
    abar_re, abar_im = mag * jnp.cos(li * dt), mag * jnp.sin(li * dt)
    den = lr * lr + li * li
    nr = abar_re - 1.0
    f_re = (nr * lr + abar_im * li) / den
    f_im = (abar_im * lr - nr * li) / den
    br, bi = b_re.astype(F32), b_im.astype(F32)
    bbar_re = f_re[..., None] * br - f_im[..., None] * bi
    bbar_im = f_re[..., None] * bi + f_im[..., None] * br
    ks = jnp.arange(SSM_T + 1, dtype=F32)[:, None, None]
    pmag = jnp.exp(ks * (lr * dt)[None])
    pw_re = pmag * jnp.cos(ks * (li * dt)[None])
    pw_im = pmag * jnp.sin(ks * (li * dt)[None])
    cr, ci = c_re.astype(F32), c_im.astype(F32)

    packs, q = SSM_PACKS, SSM_PACK

    cb_re = cr[:, :, :, None] * bbar_re[:, None, :, :] - ci[:, :, :, None] * bbar_im[:, None, :, :]
    cb_im = cr[:, :, :, None] * bbar_im[:, None, :, :] + ci[:, :, :, None] * bbar_re[:, None, :, :]
    klag = (jnp.einsum('kgs,gosi->kgio', pw_re[:SSM_T], cb_re)
            - jnp.einsum('kgs,gosi->kgio', pw_im[:SSM_T], cb_im))
    kc = klag.reshape(SSM_T, packs, q, h, h).transpose(1, 4, 0, 2, 3).reshape(packs, h, SSM_T * LANES)

    rev_re, rev_im = pw_re[SSM_T - 1::-1], pw_im[SSM_T - 1::-1]
    inj_re = rev_re[..., None] * bbar_re[None] - rev_im[..., None] * bbar_im[None]
    inj_im = rev_re[..., None] * bbar_im[None] + rev_im[..., None] * bbar_re[None]
    inj = jnp.stack([inj_re, inj_im]).reshape(2, SSM_T, packs, q, p, h)
    bc = inj.transpose(2, 1, 3, 5, 0, 4).reshape(packs, SSM_T * LANES, 2 * p)

    ca_re = cr[None] * pw_re[1:, :, None, :] - ci[None] * pw_im[1:, :, None, :]
    ca_im = cr[None] * pw_im[1:, :, None, :] + ci[None] * pw_re[1:, :, None, :]
    ca = jnp.stack([ca_re, -ca_im]).reshape(2, SSM_T, packs, q, h, p)
    cc = ca.transpose(2, 1, 4, 0, 3, 5).reshape(packs, SSM_T, h, 2 * SSM_PSTATE)

    a_t = jnp.concatenate([pw_re[SSM_T].reshape(packs, 1, SSM_PSTATE),
                           pw_im[SSM_T].reshape(packs, 1, SSM_PSTATE)], axis=-1)
    return kc.astype(BF16), bc.astype(BF16), cc.astype(BF16), a_t


def _glu_kernel(y_ref, w_ref, o_ref):
    y = y_ref[...]
    gate = jnp.dot(y.astype(BF16), w_ref[...], preferred_element_type=F32)
    o_ref[...] = (y * jax.nn.sigmoid(gate)).astype(o_ref.dtype)


def _glu(y, w):
    rows = y.shape[0]
    tm = _row_tile(rows)
    return pl.pallas_call(
        _glu_kernel,
        grid=(rows // tm,),
        in_specs=[pl.BlockSpec((tm, SSM_WIDTH), lambda i: (i, 0)),
                  pl.BlockSpec((SSM_WIDTH, SSM_WIDTH), lambda i: (0, 0))],
        out_specs=pl.BlockSpec((tm, SSM_WIDTH), lambda i: (i, 0)),
        out_shape=jax.ShapeDtypeStruct((rows, SSM_WIDTH), BF16),
        compiler_params=_params(("parallel",)),
        name="ssm_glu",
    )(y, w)


OUT_TN = 1024


def _outproj_kernel(h_ref, a_ref, b_ref, c_ref, wa_ref, wb_ref, wc_ref, o_ref):
    acc = jnp.dot(a_ref[...], wa_ref[...], preferred_element_type=F32)
    acc += jnp.dot(b_ref[...], wb_ref[...], preferred_element_type=F32)
    acc += jnp.dot(c_ref[...], wc_ref[...], preferred_element_type=F32)
    o_ref[...] = h_ref[...] + acc


def _outproj(h, oa, ob, oc, wa, wb, wc):
    rows = h.shape[0]
    tm = _row_tile(rows)
    return pl.pallas_call(
        _outproj_kernel,
        grid=(rows // tm, D_MODEL // OUT_TN),
        in_specs=[
            pl.BlockSpec((tm, OUT_TN), lambda i, j: (i, j)),
            pl.BlockSpec((tm, SWA_WIDTH), lambda i, j: (i, 0)),
            pl.BlockSpec((tm, SSM_WIDTH), lambda i, j: (i, 0)),
            pl.BlockSpec((tm, DIFF_WIDTH), lambda i, j: (i, 0)),
            pl.BlockSpec((SWA_WIDTH, OUT_TN), lambda i, j: (0, j)),
            pl.BlockSpec((SSM_WIDTH, OUT_TN), lambda i, j: (0, j)),
            pl.BlockSpec((DIFF_WIDTH, OUT_TN), lambda i, j: (0, j)),
        ],
        out_specs=pl.BlockSpec((tm, OUT_TN), lambda i, j: (i, j)),
        out_shape=jax.ShapeDtypeStruct((rows, D_MODEL), F32),
        compiler_params=_params(("parallel", "parallel")),
        name="outproj",
    )(h, oa, ob, oc, wa, wb, wc)


def _router_kernel(h_ref, g_ref, rw_ref, rb_ref, hn_ref, wt_ref, id_ref):
    x = h_ref[...]
    ms = jnp.mean(x * x, axis=-1, keepdims=True)
    hn = x * lax.rsqrt(ms + NORM_EPS) * g_ref[...]
    hn_ref[...] = hn
    logits = jnp.dot(hn, rw_ref[...], preferred_element_type=F32, precision=lax.Precision.HIGHEST) + rb_ref[...]
    lane = lax.broadcasted_iota(jnp.int32, logits.shape, 1)

    def first_argmax(v, vmax):
        return jnp.min(jnp.where(v == vmax, lane, LANES), axis=-1, keepdims=True)

    gl = jnp.where(lane < N_EXPERT_GROUPS, logits, NEG)
    gmax = jnp.max(gl, axis=-1, keepdims=True)
    g_sel = first_argmax(gl, gmax)
    p_g = 1.0 / jnp.sum(jnp.exp(gl - gmax), axis=-1, keepdims=True)
    e_lane = lane - N_EXPERT_GROUPS
    in_group = (e_lane >= 0) & (e_lane < N_EXPERTS) & (jnp.right_shift(e_lane, 3) == g_sel)
    el = jnp.where(in_group, logits, NEG)
    m1 = jnp.max(el, axis=-1, keepdims=True)
    i1 = first_argmax(el, m1)
    el2 = jnp.where(lane == i1, NEG, el)
    m2 = jnp.max(el2, axis=-1, keepdims=True)
    i2 = first_argmax(el2, m2)
    z = jnp.sum(jnp.exp(el - m1), axis=-1, keepdims=True)
    p1 = 1.0 / z
    p2 = jnp.exp(m2 - m1) / z
    w1 = p1 / (p1 + p2) * p_g
    w2 = p2 / (p1 + p2) * p_g
    wt_ref[...] = jnp.where(lane == 0, w1, jnp.where(lane == 1, w2, 0.0))
    id_ref[...] = jnp.where(lane == 0, i1 - N_EXPERT_GROUPS, jnp.where(lane == 1, i2 - N_EXPERT_GROUPS, 0))


def _router(h, g, rw, rb):
    rows = h.shape[0]
    tm = _row_tile(rows)
    return pl.pallas_call(
        _router_kernel,
        grid=(rows // tm,),
        in_specs=[
            pl.BlockSpec((tm, D_MODEL), lambda i: (i, 0)),
            pl.BlockSpec((1, D_MODEL), lambda i: (0, 0)),
            pl.BlockSpec((D_MODEL, LANES), lambda i: (0, 0)),
            pl.BlockSpec((1, LANES), lambda i: (0, 0)),
        ],
        out_specs=[
            pl.BlockSpec((tm, D_MODEL), lambda i: (i, 0)),
            pl.BlockSpec((tm, LANES), lambda i: (i, 0)),
            pl.BlockSpec((tm, LANES), lambda i: (i, 0)),
        ],
        out_shape=[
            jax.ShapeDtypeStruct((rows, D_MODEL), F32),
            jax.ShapeDtypeStruct((rows, LANES), F32),
            jax.ShapeDtypeStruct((rows, LANES), jnp.int32),
        ],
        compiler_params=_params(("parallel",)),
        name="router",
    )(h, g, rw, rb)


def _rank_kernel(ids_ref, rank_ref, cnt_ref, run_ref):
    i = pl.program_id(0)

    @pl.when(i == 0)
    def _():
        run_ref[...] = jnp.zeros(run_ref.shape, F32)

    ids = ids_ref[...]
    tm = ids.shape[0]
    lane = lax.broadcasted_iota(jnp.int32, ids.shape, 1)
    oh0 = lane == ids[:, 0:1]
    oh1 = lane == ids[:, 1:2]
    hits = (oh0 | oh1).astype(BF16)
    r = lax.broadcasted_iota(jnp.int32, (tm, tm), 0)
    c = lax.broadcasted_iota(jnp.int32, (tm, tm), 1)
    before = jnp.dot((c < r).astype(BF16), hits, preferred_element_type=F32) + run_ref[...]
    rank0 = jnp.sum(jnp.where(oh0, before, 0.0), axis=-1, keepdims=True)
    rank1 = jnp.sum(jnp.where(oh1, before, 0.0), axis=-1, keepdims=True)
    rank_ref[...] = jnp.where(lane == 0, rank0, jnp.where(lane == 1, rank1, 0.0)).astype(jnp.int32)
    run_ref[...] += jnp.sum(hits.astype(F32), axis=0, keepdims=True)
    cnt_ref[...] = run_ref[...].astype(jnp.int32)


def _assignment_ranks(ids):
    rows = ids.shape[0]
    tm = _row_tile(rows)
    return pl.pallas_call(
        _rank_kernel,
        grid=(rows // tm,),
        in_specs=[pl.BlockSpec((tm, LANES), lambda i: (i, 0))],
        out_specs=[pl.BlockSpec((tm, LANES), lambda i: (i, 0)), pl.BlockSpec((1, LANES), lambda i: (0, 0))],
        out_shape=[jax.ShapeDtypeStruct((rows, LANES), jnp.int32), jax.ShapeDtypeStruct((1, LANES), jnp.int32)],
        scratch_shapes=[pltpu.VMEM((1, LANES), F32)],
        compiler_params=_params(("arbitrary",)),
        name="moe_ranks",
    )(ids)


def _dispatch_plan(ids, n_blk):
    ranks, cnt = _assignment_ranks(ids)
    e = ids[:, :2].reshape(-1)
    n_asg = e.shape[0]
    counts = cnt[0, :N_EXPERTS]
    padded = (counts + MOE_BM - 1) // MOE_BM * MOE_BM
    pend = jnp.cumsum(padded)
    pstart = pend - padded
    onehot = e[:, None] == jnp.arange(N_EXPERTS, dtype=jnp.int32)[None, :]
    dest = jnp.sum(jnp.where(onehot, pstart[None, :], 0), axis=1).astype(jnp.int32) + ranks[:, :2].reshape(-1)
    src = jnp.zeros((n_blk * MOE_BM,), jnp.int32).at[dest].set(jnp.arange(n_asg, dtype=jnp.int32) // 2)
    blk_start = jnp.arange(n_blk, dtype=jnp.int32) * MOE_BM
    blk_expert = jnp.minimum(jnp.sum((pend[None, :] <= blk_start[:, None]).astype(jnp.int32), axis=1), N_EXPERTS - 1)
    n_used = (pend[-1] // MOE_BM).astype(jnp.int32).reshape(1)
    return dest, src, blk_expert, n_used


def _row_gather_copy(src_hbm, dst_buf, sem, src_row, dst_row):
    return pltpu.make_async_copy(src_hbm.at[pl.ds(src_row, 1), :], dst_buf.at[pl.ds(dst_row, 1), :], sem)


def _block_wait_copy(src_hbm, dst_buf, sem):
    return pltpu.make_async_copy(src_hbm.at[pl.ds(0, dst_buf.shape[0]), :], dst_buf, sem)


def _expert_kernel(be_ref, src_ref, nused_ref, hn_hbm, wg_ref, wu_ref, wd_ref, o_ref, xbuf, wgb, wub, wdb, sem):
    i = pl.program_id(0)
    n_used = nused_ref[0]
    slot = i % 2

    def start_gather(blk, slot_):
        for r in range(MOE_BM):
            _row_gather_copy(hn_hbm, xbuf.at[slot_], sem.at[slot_], src_ref[blk * MOE_BM + r], r).start()

    @pl.when(i == 0)
    def _():
        start_gather(0, 0)

    @pl.when(i <= n_used)
    def _():
        _block_wait_copy(hn_hbm, xbuf.at[slot], sem.at[slot]).wait()

    @pl.when(i < n_used)
    def _():
        @pl.when((i == 0) | (be_ref[i] != be_ref[jnp.maximum(i - 1, 0)]))
        def _():
            wgb[...] = wg_ref[...].astype(BF16)
            wub[...] = wu_ref[...].astype(BF16)
            wdb[...] = wd_ref[...].astype(BF16)

        x = xbuf[slot].astype(BF16)
        start_gather(i + 1, 1 - slot)
        gate = jnp.dot(x, wgb[...], preferred_element_type=F32)
        up = jnp.dot(x, wub[...], preferred_element_type=F32)
        act = (jax.nn.silu(gate) * up).astype(BF16)
        o_ref[...] = jnp.dot(act, wdb[...], preferred_element_type=F32)

    @pl.when(i >= n_used)
    def _():
        o_ref[...] = jnp.zeros(o_ref.shape, o_ref.dtype)


def _experts(hn, blk_expert, src, n_used, wg, wu, wd, layer, n_blk):
    def wmap(i, be, s, nu):
        return (layer, be[i], 0, 0)

    grid_spec = pltpu.PrefetchScalarGridSpec(
        num_scalar_prefetch=3,
        grid=(n_blk,),
        in_specs=[
            pl.BlockSpec(memory_space=pl.ANY),
            pl.BlockSpec((None, None, D_MODEL, D_EXPERT), wmap),
            pl.BlockSpec((None, None, D_MODEL, D_EXPERT), wmap),
            pl.BlockSpec((None, None, D_EXPERT, D_MODEL), wmap),
        ],
        out_specs=pl.BlockSpec((MOE_BM, D_MODEL), lambda i, be, s, nu: (i, 0)),
        scratch_shapes=[
            pltpu.VMEM((2, MOE_BM, D_MODEL), F32),
            pltpu.VMEM((D_MODEL, D_EXPERT), BF16),
            pltpu.VMEM((D_MODEL, D_EXPERT), BF16),
            pltpu.VMEM((D_EXPERT, D_MODEL), BF16),
            pltpu.SemaphoreType.DMA((2,)),
        ],
    )
    return pl.pallas_call(
        _expert_kernel,
        grid_spec=grid_spec,
        out_shape=jax.ShapeDtypeStruct((n_blk * MOE_BM, D_MODEL), F32),
        compiler_params=_params(("arbitrary",)),
        name="moe_experts",
    )(blk_expert, src, n_used, hn, wg, wu, wd)


def _combine_kernel(dest_ref, h_ref, wt_ref, y_hbm, o_ref, ybuf, sem):
    i = pl.program_id(0)
    n = pl.num_programs(0)
    slot = i % 2

    def start_gather(tile, slot_):
        for r in range(MOE_BM):
            for k in range(2):
                row = dest_ref[2 * (tile * MOE_BM + r) + k]
                _row_gather_copy(y_hbm, ybuf.at[slot_, k], sem.at[slot_], row, r).start()

    @pl.when(i == 0)
    def _():
        start_gather(0, 0)

    @pl.when(i + 1 < n)
    def _():
        start_gather(i + 1, 1 - slot)

    for k in range(2):
        _block_wait_copy(y_hbm, ybuf.at[slot, k], sem.at[slot]).wait()
    wt = wt_ref[...]
    o_ref[...] = h_ref[...] + wt[:, 0:1] * ybuf[slot, 0] + wt[:, 1:2] * ybuf[slot, 1]


def _combine(h, wts, ybuf, dest):
    rows = h.shape[0]
    grid_spec = pltpu.PrefetchScalarGridSpec(
        num_scalar_prefetch=1,
        grid=(rows // MOE_BM,),
        in_specs=[
            pl.BlockSpec((MOE_BM, D_MODEL), lambda i, d: (i, 0)),
            pl.BlockSpec((MOE_BM, LANES), lambda i, d: (i, 0)),
            pl.BlockSpec(memory_space=pl.ANY),
        ],
        out_specs=pl.BlockSpec((MOE_BM, D_MODEL), lambda i, d: (i, 0)),
        scratch_shapes=[pltpu.VMEM((2, 2, MOE_BM, D_MODEL), F32), pltpu.SemaphoreType.DMA((2,))],
    )
    return pl.pallas_call(
        _combine_kernel,
        grid_spec=grid_spec,
        out_shape=jax.ShapeDtypeStruct((rows, D_MODEL), F32),
        compiler_params=_params(("arbitrary",)),
        name="moe_combine",
    )(dest, h, wts, ybuf)


def _moe(h, g, rw, rb, wg, wu, wd, layer):
    rows = h.shape[0]
    n_blk = -(-(2 * rows) // MOE_BM) + N_EXPERTS
    hn, wts, ids = _router(h, g, rw, rb)
    dest, src, blk_expert, n_used = _dispatch_plan(ids, n_blk)
    ybuf = _experts(hn, blk_expert, src, n_used, wg, wu, wd, layer, n_blk)
    return _combine(h, wts, ybuf, dest)


def _final_norm_kernel(x_ref, g_ref, o_ref):
    x = x_ref[...]
    ms = jnp.mean(x * x, axis=-1, keepdims=True)
    o_ref[...] = x * lax.rsqrt(ms + NORM_EPS) * g_ref[...]


def _final_norm(h, g, batch, nb):
    seq_blocks = nb - 1
    return pl.pallas_call(
        _final_norm_kernel,
        grid=(batch, seq_blocks),
        in_specs=[pl.BlockSpec((BLK, D_MODEL), lambda b, j: (b * nb + 1 + j, 0)),
                  pl.BlockSpec((1, D_MODEL), lambda b, j: (0, 0))],
        out_specs=pl.BlockSpec((BLK, D_MODEL), lambda b, j: (b * seq_blocks + j, 0)),
        out_shape=jax.ShapeDtypeStruct((batch * seq_blocks * BLK, D_MODEL), F32),
        compiler_params=_params(("parallel", "parallel")),
        name="final_norm",
    )(h, g)


def _rope_tables(batch, lp):
    half = HEAD_DIM // 2
    inv = ROPE_THETA ** (-jnp.arange(half, dtype=F32) / half)
    pos = (jnp.arange(lp, dtype=jnp.int32) - PAD).astype(F32)
    ang = pos[:, None] * inv[None, :]
    cos = jnp.tile(jnp.cos(ang), (batch, LANES // half))
    sin = jnp.sin(ang)
    sin = jnp.tile(jnp.concatenate([-sin, sin], axis=1), (batch, LANES // HEAD_DIM))
    return cos, sin


def kernel(x, meta_tokens, attn_norm_g, ffn_norm_g, w_in, swa_sinks, ssm_lambda_re, ssm_lambda_im, ssm_b_re, ssm_b_im, ssm_c_re, ssm_c_im, ssm_d, ssm_log_dt, ssm_w_glu, diff_lambda_q1, diff_lambda_k1, diff_lambda_q2, diff_lambda_k2, diff_subln_g, w_out, router_group_w, router_group_b, router_expert_w, router_expert_b, moe_w_gate, moe_w_up, moe_w_down, final_norm_g):
    batch, seq, d = x.shape
    depth = w_in.shape[0]
    lp = PAD + N_META + seq
    nb = lp // BLK
    meta = jnp.broadcast_to(meta_tokens[None].astype(x.dtype), (batch, N_META, d))
    h = jnp.concatenate([jnp.zeros((batch, PAD, d), x.dtype), meta, x], axis=1).reshape(batch * lp, d)
    cos, sin = _rope_tables(batch, lp)
    for l in range(depth):
        proj = _norm_inproj(h, attn_norm_g[l][None], w_in[l].astype(BF16), cos, sin)
        o_a = _swa(proj, swa_sinks[l].astype(F32), batch, nb)
        kflat, bw, cw, a_t = _ssm_operators(ssm_lambda_re[l], ssm_lambda_im[l], ssm_b_re[l], ssm_b_im[l],
                                            ssm_c_re[l], ssm_c_im[l], ssm_log_dt[l])
        yg = _ssm(proj, kflat, bw, cw, a_t, ssm_d[l].astype(F32).reshape(SSM_PACKS, 1, LANES), batch, lp)
        o_b = _glu(yg, ssm_w_glu[l].astype(BF16))
        lam_init = 0.8 - 0.6 * math.exp(-0.3 * l)
        lam_vecs = jnp.stack([diff_lambda_q1[l], diff_lambda_k1[l], diff_lambda_q2[l], diff_lambda_k2[l]]).astype(F32)
        o_c = _diff_attn(proj, lam_vecs, diff_subln_g[l][None].astype(F32), lam_init, batch, nb)
        wo = w_out[l].astype(BF16)
        h = _outproj(h, o_a, o_b, o_c, wo[:SWA_WIDTH], wo[SWA_WIDTH:SWA_WIDTH + SSM_WIDTH],
                     wo[SWA_WIDTH + SSM_WIDTH:])
        rw = jnp.concatenate([router_group_w[l], router_expert_w[l]], axis=1).astype(F32)
        rw = jnp.pad(rw, ((0, 0), (0, LANES - rw.shape[1])))
        rb = jnp.concatenate([router_group_b[l], router_expert_b[l]]).astype(F32)
        rb = jnp.pad(rb, (0, LANES - rb.shape[0]))[None]
        h = _moe(h, ffn_norm_g[l][None], rw, rb, moe_w_gate, moe_w_up, moe_w_down, l)
    out = _final_norm(h, final_norm_g[None], batch, nb)
    return out.reshape(batch, seq, d)
```

```python
import functools
import math

import jax
import jax.numpy as jnp
from jax import lax
from jax.experimental import pallas as pl
from jax.experimental.pallas import tpu as pltpu

D_MODEL = 2048
N_META = 16
BLK = 128
PAD = BLK - N_META
ROPE_THETA = 10000.0
NORM_EPS = 1e-5
NEG = -1e30
SWA_HEADS = 12
SWA_KV_HEADS = 4
HEAD_DIM = 64
SWA_WIDTH = SWA_HEADS * HEAD_DIM
SSM_WIDTH = 768
SSM_GROUP = 16
SSM_GROUPS = SSM_WIDTH // SSM_GROUP
SSM_STATE = 64
DIFF_HEADS = 4
DIFF_V_DIM = 128
DIFF_WIDTH = DIFF_HEADS * DIFF_V_DIM
IN_WIDTH = 3584
N_EXPERT_GROUPS = 4
EXPERTS_PER_GROUP = 8
N_EXPERTS = N_EXPERT_GROUPS * EXPERTS_PER_GROUP
D_EXPERT = 512

COL_KA = 768
COL_VA = 1024
COL_US = 1280
COL_QD = 2048
COL_KD = 2560
COL_VD = 3072

LANES = 128
SSM_T = 16
SSM_PACK = LANES // SSM_GROUP
SSM_PACKS = SSM_GROUPS // SSM_PACK
SSM_PSTATE = SSM_PACK * SSM_STATE
MOE_BM = 128
VMEM_LIMIT = 56 * 1024 * 1024

F32 = jnp.float32
BF16 = jnp.bfloat16


def _row_tile(rows):
    for t in (512, 256, 128):
        if rows % t == 0:
            return t
    raise ValueError(f"row count {rows} is not a multiple of 128")


def _params(sem, vmem=VMEM_LIMIT):
    return pltpu.CompilerParams(dimension_semantics=sem, vmem_limit_bytes=vmem)


def _rms_norm(x, g):
    return x * lax.rsqrt(jnp.mean(x * x, axis=-1, keepdims=True) + NORM_EPS) * g


IN_TN = 512
ROPE_TILES = (0, 1, 4, 5)


def _norm_inproj_kernel(x_ref, g_ref, w_ref, cos_ref, sin_ref, o_ref, xn_ref):
    x = x_ref[...]
    ms = jnp.mean(x * x, axis=-1, keepdims=True)
    xn_ref[...] = (x * lax.rsqrt(ms + NORM_EPS) * g_ref[...]).astype(BF16)
    shape = (x.shape[0], IN_TN)
    lane = lax.broadcasted_iota(jnp.int32, shape, 1)
    first_half = (lane & (HEAD_DIM - 1)) < (HEAD_DIM // 2)
    reps = IN_TN // LANES
    cos = jnp.tile(cos_ref[...], (1, reps))
    sin = jnp.tile(sin_ref[...], (1, reps))
    for j in range(IN_WIDTH // IN_TN):
        cols = slice(j * IN_TN, (j + 1) * IN_TN)
        acc = jnp.dot(xn_ref[...], w_ref[:, cols], preferred_element_type=F32)
        if j in ROPE_TILES:
            partner = jnp.where(first_half, pltpu.roll(acc, IN_TN - HEAD_DIM // 2, 1),
                                pltpu.roll(acc, HEAD_DIM // 2, 1))
            acc = acc * cos + partner * sin
        o_ref[:, cols] = acc.astype(o_ref.dtype)


def _norm_inproj(h, g, w, cos, sin):
    rows = h.shape[0]
    tm = _row_tile(rows)
    return pl.pallas_call(
        _norm_inproj_kernel,
        grid=(rows // tm,),
        in_specs=[
            pl.BlockSpec((tm, D_MODEL), lambda i: (i, 0)),
            pl.BlockSpec((1, D_MODEL), lambda i: (0, 0)),
            pl.BlockSpec((D_MODEL, IN_WIDTH), lambda i: (0, 0)),
            pl.BlockSpec((tm, LANES), lambda i: (i, 0)),
            pl.BlockSpec((tm, LANES), lambda i: (i, 0)),
        ],
        out_specs=pl.BlockSpec((tm, IN_WIDTH), lambda i: (i, 0)),
        out_shape=jax.ShapeDtypeStruct((rows, IN_WIDTH), BF16),
        scratch_shapes=[pltpu.VMEM((tm, D_MODEL), BF16)],
        compiler_params=_params(("parallel",)),
        name="norm_inproj",
    )(h, g, w, cos, sin)


def _swa_kernel(sink_ref, q_ref, kp_ref, kc_ref, vp_ref, vc_ref, o_ref):
    n = pl.program_id(1)
    shape = (BLK, 2 * BLK)
    qi = lax.broadcasted_iota(jnp.int32, shape, 0)
    c = lax.broadcasted_iota(jnp.int32, shape, 1)
    kj = (n - 1) * BLK + c
    delta = qi + BLK - c
    ok = (delta >= 0) & (delta < BLK) & (kj >= PAD)
    grp = SWA_HEADS // SWA_KV_HEADS
    outs = []
    for j in range(SWA_KV_HEADS):
        cols = slice(j * HEAD_DIM, (j + 1) * HEAD_DIM)
        kk = jnp.concatenate([kp_ref[:, cols], kc_ref[:, cols]], axis=0)
        vv = jnp.concatenate([vp_ref[:, cols], vc_ref[:, cols]], axis=0)
        for g in range(grp):
            hd = j * grp + g
            qh = q_ref[:, hd * HEAD_DIM:(hd + 1) * HEAD_DIM]
            s = lax.dot_general(qh, kk, (((1,), (1,)), ((), ())), preferred_element_type=F32) * (HEAD_DIM ** -0.5)
            s = jnp.where(ok, s, NEG)
            sink = sink_ref[hd]
            m = jnp.maximum(jnp.max(s, axis=-1, keepdims=True), sink)
            p = jnp.exp(s - m)
            denom = jnp.sum(p, axis=-1, keepdims=True) + jnp.exp(sink - m)
            outs.append(jnp.dot(p.astype(BF16), vv, preferred_element_type=F32) / denom)
    o_ref[...] = jnp.concatenate(outs, axis=1).astype(o_ref.dtype)


def _swa(proj, sinks, batch, nb):
    rows = proj.shape[0]
    kvw = SWA_KV_HEADS * HEAD_DIM

    def cur(col):
        return lambda b, n: (b * nb + n, col)

    def prev(col):
        return lambda b, n: (b * nb + jnp.maximum(n - 1, 0), col)

    return pl.pallas_call(
        _swa_kernel,
        grid=(batch, nb),
        in_specs=[
            pl.BlockSpec(memory_space=pltpu.SMEM),
            pl.BlockSpec((BLK, SWA_WIDTH), cur(0)),
            pl.BlockSpec((BLK, kvw), prev(COL_KA // kvw)),
            pl.BlockSpec((BLK, kvw), cur(COL_KA // kvw)),
            pl.BlockSpec((BLK, kvw), prev(COL_VA // kvw)),
            pl.BlockSpec((BLK, kvw), cur(COL_VA // kvw)),
        ],
        out_specs=pl.BlockSpec((BLK, SWA_WIDTH), cur(0)),
        out_shape=jax.ShapeDtypeStruct((rows, SWA_WIDTH), BF16),
        compiler_params=_params(("parallel", "parallel")),
        name="swa",
    )(sinks, proj, proj, proj, proj, proj)


def _diff_kernel(lam_ref, g_ref, q_ref, k_ref, v_ref, o_ref, *, lam_init, nb):
    row = lax.broadcasted_iota(jnp.int32, (BLK, BLK), 0)
    col = lax.broadcasted_iota(jnp.int32, (BLK, BLK), 1)
    causal_ok = col <= row
    pad_ok = col >= PAD
    lane = lax.broadcasted_iota(jnp.int32, (BLK, LANES), 1)
    lv = lam_ref[...]
    lam = (jnp.exp(jnp.sum(lv[0:1] * lv[1:2], axis=-1, keepdims=True))
           - jnp.exp(jnp.sum(lv[2:3] * lv[3:4], axis=-1, keepdims=True)) + lam_init)
    gain = g_ref[...] * (1.0 - lam_init)
    scale = HEAD_DIM ** -0.5
    for n in range(nb):
        ke = (n + 1) * BLK
        q = q_ref[n * BLK:(n + 1) * BLK, :] * scale
        kx = k_ref[0:ke, :]
        vx = v_ref[0:ke, :]
        heads = []
        for s_idx in range(2):
            qx = jnp.where((lane >= HEAD_DIM) == bool(s_idx), q, jnp.zeros_like(q))
            s = lax.dot_general(qx, kx, (((1,), (1,)), ((), ())), preferred_element_type=F32)
            if n == 0:
                s = jnp.where(causal_ok & pad_ok, s, NEG)
            else:
                first = jnp.where(pad_ok, s[:, :BLK], NEG)
                last = jnp.where(causal_ok, s[:, ke - BLK:], NEG)
                middle = [s[:, BLK:ke - BLK]] if n > 1 else []
                s = jnp.concatenate([first] + middle + [last], axis=1)
            m = jnp.max(s, axis=-1, keepdims=True)
            p = jnp.exp(s - m)
            denom = jnp.sum(p, axis=-1, keepdims=True)
            heads.append(jnp.dot(p.astype(BF16), vx, preferred_element_type=F32) / denom)
        o = heads[0] - lam * heads[1]
        o = o * lax.rsqrt(jnp.mean(o * o, axis=-1, keepdims=True) + NORM_EPS) * gain
        o_ref[n * BLK:(n + 1) * BLK, :] = o.astype(o_ref.dtype)


def _diff_attn(proj, lam_vecs, subln_g, lam_init, batch, nb):
    rows = proj.shape[0]
    lp = nb * BLK
    return pl.pallas_call(
        functools.partial(_diff_kernel, lam_init=lam_init, nb=nb),
        grid=(batch, DIFF_HEADS),
        in_specs=[
            pl.BlockSpec((4, HEAD_DIM), lambda b, h: (0, 0)),
            pl.BlockSpec((1, DIFF_V_DIM), lambda b, h: (0, 0)),
            pl.BlockSpec((lp, LANES), lambda b, h: (b, COL_QD // LANES + h)),
            pl.BlockSpec((lp, LANES), lambda b, h: (b, COL_KD // LANES + h)),
            pl.BlockSpec((lp, LANES), lambda b, h: (b, COL_VD // LANES + h)),
        ],
        out_specs=pl.BlockSpec((lp, DIFF_V_DIM), lambda b, h: (b, h)),
        out_shape=jax.ShapeDtypeStruct((rows, DIFF_WIDTH), BF16),
        compiler_params=_params(("parallel", "parallel")),
        name="diff_attn",
    )(lam_vecs, subln_g, proj, proj, proj)


def _ssm_row_chunk(rows):
    for c in (544, 512, 384, 256, 128, 64, 32, 16):
        if rows % c == 0:
            return c
    raise ValueError(rows)


def _ssm_expand_operators(kc_ref, bc_ref, cc_ref, k_ref, bw_ref, cw_ref):
    def iota(shape, dim):
        return lax.broadcasted_iota(jnp.int32, shape, dim)

    def group(idx, width):
        return jnp.right_shift(idx, width.bit_length() - 1) & (SSM_PACK - 1)

    hs, ps, tl = SSM_GROUP, SSM_STATE, SSM_T * LANES
    spread_h = ((iota((LANES, hs), 0) & (hs - 1)) == iota((LANES, hs), 1)).astype(BF16)
    kt = jnp.dot(spread_h, kc_ref[...], preferred_element_type=F32)
    same = group(iota((LANES, tl), 0), hs) == group(iota((LANES, tl), 1), hs)
    k_ref[...] = jnp.where(same, kt, 0.0).astype(BF16)
    r = iota((2 * ps, 2 * SSM_PSTATE), 0)
    c = iota((2 * ps, 2 * SSM_PSTATE), 1)
    spread_s = ((jnp.right_shift(r, 6) == jnp.right_shift(c, 9)) & ((r & (ps - 1)) == (c & (ps - 1)))).astype(BF16)
    bw = jnp.dot(bc_ref[...], spread_s, preferred_element_type=F32)
    shape = (tl, 2 * SSM_PSTATE)
    same = group(iota(shape, 0), hs) == group(iota(shape, 1), ps)
    bw_ref[...] = jnp.where(same, bw, 0.0).astype(BF16)
    shape = (LANES, 2 * SSM_PSTATE)
    same = group(iota(shape, 0), hs) == group(iota(shape, 1), ps)
    for j in range(SSM_T):
        ct = jnp.dot(spread_h, cc_ref[j], preferred_element_type=F32)
        cw_ref[j] = jnp.where(same, ct, 0.0).astype(BF16)


def _ssm_kernel(u_ref, kc_ref, bc_ref, cc_ref, a_ref, d_ref, o_ref, up_ref, y_ref, e_ref, s_ref, k_ref, bw_ref,
                cw_ref, *, batch, lp):
    rows = batch * lp
    n_chunks = lp // SSM_T
    tot_chunks = rows // SSM_T
    _ssm_expand_operators(kc_ref, bc_ref, cc_ref, k_ref, bw_ref, cw_ref)
    contract_lanes = (((1,), (1,)), ((), ()))
    up_ref[0:SSM_T, :] = jnp.zeros((SSM_T, LANES), F32)
    up_ref[SSM_T:, :] = u_ref[...].astype(F32)
    for b in range(batch):
        up_ref[SSM_T + b * lp:SSM_T + b * lp + PAD, :] = jnp.zeros((PAD, LANES), F32)

    ch = _ssm_row_chunk(rows)
    rmod = lax.broadcasted_iota(jnp.int32, (ch, LANES), 0) & (SSM_T - 1)
    for r0 in range(0, rows, ch):
        pieces = []
        for k in range(SSM_T):
            sl = up_ref[SSM_T - k + r0:SSM_T - k + r0 + ch, :]
            pieces.append(jnp.where(rmod >= k, sl, 0.0).astype(BF16))
        y_ref[r0:r0 + ch, :] = lax.dot_general(jnp.concatenate(pieces, axis=1), k_ref[...], contract_lanes,
                                               preferred_element_type=F32)

    pieces = [up_ref[pl.ds(SSM_T + i, tot_chunks, stride=SSM_T), :].astype(BF16) for i in range(SSM_T)]
    e_ref[...] = jnp.dot(jnp.concatenate(pieces, axis=1), bw_ref[...], preferred_element_type=F32)

    a_re = a_ref[:, :SSM_PSTATE]
    a_im = a_ref[:, SSM_PSTATE:]

    def step(c, carry):
        new = []
        for b in range(batch):
            sr, si = carry[b]
            idx = b * n_chunks + c
            s_ref[pl.ds(idx, 1), :SSM_PSTATE] = sr
            s_ref[pl.ds(idx, 1), SSM_PSTATE:] = si
            er = e_ref[pl.ds(idx, 1), :SSM_PSTATE]
            ei = e_ref[pl.ds(idx, 1), SSM_PSTATE:]
            new.append((a_re * sr - a_im * si + er, a_re * si + a_im * sr + ei))
        return tuple(new)

    zero = jnp.zeros((1, SSM_PSTATE), F32)
    lax.fori_loop(0, n_chunks, step, tuple((zero, zero) for _ in range(batch)))

    sb = s_ref[...].astype(BF16)
    for j in range(SSM_T):
        z = lax.dot_general(sb, cw_ref[j], contract_lanes, preferred_element_type=F32)
        idx = pl.ds(j, tot_chunks, stride=SSM_T)
        y_ref[idx, :] = y_ref[idx, :] + z

    y = y_ref[...] + d_ref[...] * up_ref[SSM_T:, :]
    o_ref[...] = jax.nn.gelu(y, approximate=True).astype(o_ref.dtype)


def _ssm(proj, kc, bc, cc, a_t, dskip, batch, lp):
    rows = proj.shape[0]
    tot_chunks = rows // SSM_T
    return pl.pallas_call(
        functools.partial(_ssm_kernel, batch=batch, lp=lp),
        grid=(SSM_PACKS,),
        in_specs=[
            pl.BlockSpec((rows, LANES), lambda p: (0, COL_US // LANES + p)),
            pl.BlockSpec((None, SSM_GROUP, SSM_T * LANES), lambda p: (p, 0, 0)),
            pl.BlockSpec((None, SSM_T * LANES, 2 * SSM_STATE), lambda p: (p, 0, 0)),
            pl.BlockSpec((None, SSM_T, SSM_GROUP, 2 * SSM_PSTATE), lambda p: (p, 0, 0, 0)),
            pl.BlockSpec((None, 1, 2 * SSM_PSTATE), lambda p: (p, 0, 0)),
            pl.BlockSpec((None, 1, LANES), lambda p: (p, 0, 0)),
        ],
        out_specs=pl.BlockSpec((rows, LANES), lambda p: (0, p)),
        out_shape=jax.ShapeDtypeStruct((rows, SSM_WIDTH), F32),
        scratch_shapes=[
            pltpu.VMEM((rows + SSM_T, LANES), F32),
            pltpu.VMEM((rows, LANES), F32),
            pltpu.VMEM((tot_chunks, 2 * SSM_PSTATE), F32),
            pltpu.VMEM((tot_chunks, 2 * SSM_PSTATE), F32),
            pltpu.VMEM((LANES, SSM_T * LANES), BF16),
            pltpu.VMEM((SSM_T * LANES, 2 * SSM_PSTATE), BF16),
            pltpu.VMEM((SSM_T, LANES, 2 * SSM_PSTATE), BF16),
        ],
        compiler_params=_params(("parallel",)),
        name="ssm",
    )(proj, kc, bc, cc, a_t, dskip)


def _ssm_operators(lam_re, lam_im, b_re, b_im, c_re, c_im, log_dt):
    p, h = SSM_STATE, SSM_GROUP
    dt = jnp.exp(log_dt.astype(F32))[:, None]
    lr, li = lam_re.astype(F32), lam_im.astype(F32)
    mag = jnp.exp(lr * dt)
    abar_re, abar_im = mag * jnp.cos(li * dt), mag * jnp.sin(li * dt)
    den = lr * lr + li * li
    nr = abar_re - 1.0
    f_re = (nr * lr + abar_im * li) / den
    f_im = (abar_im * lr - nr * li) / den
    br, bi = b_re.astype(F32), b_im.astype(F32)
    bbar_re = f_re[..., None] * br - f_im[..., None] * bi
    bbar_im = f_re[..., None] * bi + f_im[..., None] * br
    ks = jnp.arange(SSM_T + 1, dtype=F32)[:, None, None]
    pmag = jnp.exp(ks * (lr * dt)[None])
    pw_re = pmag * jnp.cos(ks * (li * dt)[None])
    pw_im = pmag * jnp.sin(ks * (li * dt)[None])
    cr, ci = c_re.astype(F32), c_im.astype(F32)

    packs, q = SSM_PACKS, SSM_PACK

    cb_re = cr[:, :, :, None] * bbar_re[:, None, :, :] - ci[:, :, :, None] * bbar_im[:, None, :, :]
    cb_im = cr[:, :, :, None] * bbar_im[:, None, :, :] + ci[:, :, :, None] * bbar_re[:, None, :, :]
    klag = (jnp.einsum('kgs,gosi->kgio', pw_re[:SSM_T], cb_re)
            - jnp.einsum('kgs,gosi->kgio', pw_im[:SSM_T], cb_im))
    kc = klag.reshape(SSM_T, packs, q, h, h).transpose(1, 4, 0, 2, 3).reshape(packs, h, SSM_T * LANES)

    rev_re, rev_im = pw_re[SSM_T - 1::-1], pw_im[SSM_T - 1::-1]
    inj_re = rev_re[..., None] * bbar_re[None] - rev_im[..., None] * bbar_im[None]
    inj_im = rev_re[..., None] * bbar_im[None] + rev_im[..., None] * bbar_re[None]
    inj = jnp.stack([inj_re, inj_im]).reshape(2, SSM_T, packs, q, p, h)
    bc = inj.transpose(2, 1, 3, 5, 0, 4).reshape(packs, SSM_T * LANES, 2 * p)

    ca_re = cr[None] * pw_re[1:, :, None, :] - ci[None] * pw_im[1:, :, None, :]
    ca_im = cr[None] * pw_im[1:, :, None, :] + ci[None] * pw_re[1:, :, None, :]
    ca = jnp.stack([ca_re, -ca_im]).reshape(2, SSM_T, packs, q, h, p)
    cc = ca.transpose(2, 1, 4, 0, 3, 5).reshape(packs, SSM_T, h, 2 * SSM_PSTATE)

    a_t = jnp.concatenate([pw_re[SSM_T].reshape(packs, 1, SSM_PSTATE),
                           pw_im[SSM_T].reshape(packs, 1, SSM_PSTATE)], axis=-1)
    return kc.astype(BF16), bc.astype(BF16), cc.astype(BF16), a_t


OUT_TN = 512


def _split_bf16(x):
    hi = x.astype(BF16)
    return hi, (x - hi.astype(F32)).astype(BF16)


def _route(hn, rwh_ref, rwl_ref, rb_ref):
    hi, lo = _split_bf16(hn)
    logits = (jnp.dot(hi, rwh_ref[...], preferred_element_type=F32)
              + (jnp.dot(lo, rwh_ref[...], preferred_element_type=F32)
                 + jnp.dot(hi, rwl_ref[...], preferred_element_type=F32))) + rb_ref[...]
    lane = lax.broadcasted_iota(jnp.int32, logits.shape, 1)

    def first_argmax(v, vmax):
        return jnp.min(jnp.where(v == vmax, lane, LANES), axis=-1, keepdims=True)

    gl = jnp.where(lane < N_EXPERT_GROUPS, logits, NEG)
    gmax = jnp.max(gl, axis=-1, keepdims=True)
    g_sel = first_argmax(gl, gmax)
    p_g = 1.0 / jnp.sum(jnp.exp(gl - gmax), axis=-1, keepdims=True)
    e_lane = lane - N_EXPERT_GROUPS
    in_group = (e_lane >= 0) & (e_lane < N_EXPERTS) & (jnp.right_shift(e_lane, 3) == g_sel)
    el = jnp.where(in_group, logits, NEG)
    m1 = jnp.max(el, axis=-1, keepdims=True)
    i1 = first_argmax(el, m1)
    el2 = jnp.where(lane == i1, NEG, el)
    m2 = jnp.max(el2, axis=-1, keepdims=True)
    i2 = first_argmax(el2, m2)
    z = jnp.sum(jnp.exp(el - m1), axis=-1, keepdims=True)
    p1 = 1.0 / z
    p2 = jnp.exp(m2 - m1) / z
    w1 = p1 / (p1 + p2) * p_g
    w2 = p2 / (p1 + p2) * p_g
    wts = jnp.where(lane == 0, w1, jnp.where(lane == 1, w2, 0.0))
    ids = jnp.where(lane == 0, i1 - N_EXPERT_GROUPS, jnp.where(lane == 1, i2 - N_EXPERT_GROUPS, 0))
    return wts, ids


def _assignment_ranks(ids, run_ref):
    tm = ids.shape[0]
    lane = lax.broadcasted_iota(jnp.int32, ids.shape, 1)
    oh0 = lane == ids[:, 0:1]
    oh1 = lane == ids[:, 1:2]
    hits = (oh0 | oh1).astype(BF16)
    r = lax.broadcasted_iota(jnp.int32, (tm, tm), 0)
    c = lax.broadcasted_iota(jnp.int32, (tm, tm), 1)
    before = jnp.dot((c < r).astype(BF16), hits, preferred_element_type=F32) + run_ref[...]
    rank0 = jnp.sum(jnp.where(oh0, before, 0.0), axis=-1, keepdims=True)
    rank1 = jnp.sum(jnp.where(oh1, before, 0.0), axis=-1, keepdims=True)
    run_ref[...] += jnp.sum(hits.astype(F32), axis=0, keepdims=True)
    return jnp.where(lane == 0, rank0, jnp.where(lane == 1, rank1, 0.0)).astype(jnp.int32)


def _mixer_out_kernel(h_ref, a_ref, y_ref, c_ref, wo_ref, wglu_ref, g_ref, rwh_ref, rwl_ref, rb_ref,
                      o_ref, wt_ref, id_ref, rank_ref, cnt_ref, run_ref):
    @pl.when(pl.program_id(0) == 0)
    def _():
        run_ref[...] = jnp.zeros(run_ref.shape, F32)

    y = y_ref[...]
    gate = jnp.dot(y.astype(BF16), wglu_ref[...], preferred_element_type=F32)
    b = (y * jax.nn.sigmoid(gate)).astype(BF16)
    mixed = jnp.concatenate([a_ref[...], b, c_ref[...]], axis=1)
    for j in range(D_MODEL // OUT_TN):
        cols = slice(j * OUT_TN, (j + 1) * OUT_TN)
        o_ref[:, cols] = h_ref[:, cols] + jnp.dot(mixed, wo_ref[:, cols], preferred_element_type=F32)
    x = o_ref[...]
    hn = _rms_norm(x, g_ref[...])
    wts, ids = _route(hn, rwh_ref, rwl_ref, rb_ref)
    wt_ref[...] = wts
    id_ref[...] = ids
    rank_ref[...] = _assignment_ranks(ids, run_ref)
    cnt_ref[...] = run_ref[...].astype(jnp.int32)


def _mixer_out(h, oa, yg, oc, wo, wglu, g, rw_hi, rw_lo, rb):
    rows = h.shape[0]
    tm = _row_tile(rows)

    def row_spec(width):
        return pl.BlockSpec((tm, width), lambda i: (i, 0))

    def whole(shape):
        return pl.BlockSpec(shape, lambda i: (0, 0))

    return pl.pallas_call(
        _mixer_out_kernel,
        grid=(rows // tm,),
        in_specs=[
            row_spec(D_MODEL), row_spec(SWA_WIDTH), row_spec(SSM_WIDTH), row_spec(DIFF_WIDTH),
            whole((D_MODEL, D_MODEL)), whole((SSM_WIDTH, SSM_WIDTH)), whole((1, D_MODEL)),
            whole((D_MODEL, LANES)), whole((D_MODEL, LANES)), whole((1, LANES)),
        ],
        out_specs=[row_spec(D_MODEL), row_spec(LANES), row_spec(LANES), row_spec(LANES), whole((1, LANES))],
        out_shape=[
            jax.ShapeDtypeStruct((rows, D_MODEL), F32),
            jax.ShapeDtypeStruct((rows, LANES), F32),
            jax.ShapeDtypeStruct((rows, LANES), jnp.int32),
            jax.ShapeDtypeStruct((rows, LANES), jnp.int32),
            jax.ShapeDtypeStruct((1, LANES), jnp.int32),
        ],
        scratch_shapes=[pltpu.VMEM((1, LANES), F32)],
        compiler_params=_params(("arbitrary",)),
        name="mixer_out",
    )(h, oa, yg, oc, wo, wglu, g, rw_hi, rw_lo, rb)


def _dispatch_plan(ids, ranks, cnt, n_blk):
    e = ids[:, :2].reshape(-1)
    n_asg = e.shape[0]
    counts = cnt[0, :N_EXPERTS]
    padded = (counts + MOE_BM - 1) // MOE_BM * MOE_BM
    pend = jnp.cumsum(padded)
    pstart = pend - padded
    onehot = e[:, None] == jnp.arange(N_EXPERTS, dtype=jnp.int32)[None, :]
    dest = jnp.sum(jnp.where(onehot, pstart[None, :], 0), axis=1).astype(jnp.int32) + ranks[:, :2].reshape(-1)
    src = jnp.zeros((n_blk * MOE_BM,), jnp.int32).at[dest].set(jnp.arange(n_asg, dtype=jnp.int32) // 2)
    blk_start = jnp.arange(n_blk, dtype=jnp.int32) * MOE_BM
    blk_expert = jnp.minimum(jnp.sum((pend[None, :] <= blk_start[:, None]).astype(jnp.int32), axis=1), N_EXPERTS - 1)
    n_used = (pend[-1] // MOE_BM).astype(jnp.int32).reshape(1)
    ex = jnp.arange(N_EXPERTS, dtype=jnp.int32)
    nonempty = counts > 0
    run_parity = (jnp.cumsum(nonempty.astype(jnp.int32)) - 1) & 1
    later = (ex[None, :] > ex[:, None]) & nonempty[None, :]
    nxt = jnp.min(jnp.where(later, ex[None, :], N_EXPERTS), axis=1)
    nxt = jnp.where(nxt == N_EXPERTS, -1, nxt)
    plan = dict(src=src, blk_expert=blk_expert, n_used=n_used, blk_parity=run_parity[blk_expert].astype(jnp.int32),
                blk_next=nxt[blk_expert].astype(jnp.int32))
    return dest, plan


def _row_gather_copy(src_hbm, dst_buf, sem, src_row, dst_row):
    return pltpu.make_async_copy(src_hbm.at[pl.ds(src_row, 1), :], dst_buf.at[pl.ds(dst_row, 1), :], sem)


def _block_wait_copy(src_hbm, dst_buf, sem):
    return pltpu.make_async_copy(src_hbm.at[pl.ds(0, dst_buf.shape[0]), :], dst_buf, sem)


GATHER_SLOTS = 3


def _expert_kernel(be_ref, par_ref, nxt_ref, src_ref, nused_ref, hn_hbm, g_ref, wg_hbm, wu_hbm, wd_hbm, o_ref,
                   xbuf, wgf, wuf, wdf, wgb, wub, wdb, gsem, wsem, *, layer):
    i = pl.program_id(0)
    n_used = nused_ref[0]
    slot = i % GATHER_SLOTS

    def start_gather(blk, slot_):
        for r in range(MOE_BM):
            _row_gather_copy(hn_hbm, xbuf.at[slot_], gsem.at[slot_], src_ref[blk * MOE_BM + r], r).start()

    def weight_copies(expert, wslot):
        pairs = ((wg_hbm, wgf), (wu_hbm, wuf), (wd_hbm, wdf))
        return [pltpu.make_async_copy(w.at[layer, expert], buf.at[wslot], wsem.at[wslot]) for w, buf in pairs]

    @pl.when(i == 0)
    def _():
        for c in weight_copies(be_ref[0], 0):
            c.start()
        start_gather(0, 0)
        start_gather(jnp.minimum(1, n_used), 1)

    @pl.when(i <= n_used + 1)
    def _():
        _block_wait_copy(hn_hbm, xbuf.at[slot], gsem.at[slot]).wait()

    @pl.when(i < n_used)
    def _():
        @pl.when((i == 0) | (be_ref[i] != be_ref[jnp.maximum(i - 1, 0)]))
        def _():
            wslot = par_ref[i]
            for c in weight_copies(be_ref[i], wslot):
                c.wait()

            @pl.when(nxt_ref[i] >= 0)
            def _():
                for c in weight_copies(nxt_ref[i], 1 - wslot):
                    c.start()

            wgb[...] = wgf[wslot].astype(BF16)
            wub[...] = wuf[wslot].astype(BF16)
            wdb[...] = wdf[wslot].astype(BF16)

        x = _rms_norm(xbuf[slot], g_ref[...]).astype(BF16)
        start_gather(jnp.minimum(i + 2, n_used), (i + 2) % GATHER_SLOTS)
        gate = jnp.dot(x, wgb[...], preferred_element_type=F32)
        up = jnp.dot(x, wub[...], preferred_element_type=F32)
        act = (jax.nn.silu(gate) * up).astype(BF16)
        o_ref[...] = jnp.dot(act, wdb[...], preferred_element_type=F32)

    @pl.when(i >= n_used)
    def _():
        o_ref[...] = jnp.zeros(o_ref.shape, o_ref.dtype)


def _experts(h, g, plan, wg, wu, wd, layer, n_blk):
    hbm = pl.BlockSpec(memory_space=pl.ANY)
    grid_spec = pltpu.PrefetchScalarGridSpec(
        num_scalar_prefetch=5,
        grid=(n_blk,),
        in_specs=[hbm, pl.BlockSpec((1, D_MODEL), lambda i, *_: (0, 0)), hbm, hbm, hbm],
        out_specs=pl.BlockSpec((MOE_BM, D_MODEL), lambda i, *_: (i, 0)),
        scratch_shapes=[
            pltpu.VMEM((GATHER_SLOTS, MOE_BM, D_MODEL), F32),
            pltpu.VMEM((2, D_MODEL, D_EXPERT), F32),
            pltpu.VMEM((2, D_MODEL, D_EXPERT), F32),
            pltpu.VMEM((2, D_EXPERT, D_MODEL), F32),
            pltpu.VMEM((D_MODEL, D_EXPERT), BF16),
            pltpu.VMEM((D_MODEL, D_EXPERT), BF16),
            pltpu.VMEM((D_EXPERT, D_MODEL), BF16),
            pltpu.SemaphoreType.DMA((GATHER_SLOTS,)),
            pltpu.SemaphoreType.DMA((2,)),
        ],
    )
    return pl.pallas_call(
        functools.partial(_expert_kernel, layer=layer),
        grid_spec=grid_spec,
        out_shape=jax.ShapeDtypeStruct((n_blk * MOE_BM, D_MODEL), F32),
        compiler_params=_params(("arbitrary",)),
        name="moe_experts",
    )(plan["blk_expert"], plan["blk_parity"], plan["blk_next"], plan["src"], plan["n_used"], h, g, wg, wu, wd)


def _combine_kernel(dest_ref, h_ref, wt_ref, y_hbm, o_ref, ybuf, sem):
    i = pl.program_id(0)
    n = pl.num_programs(0)
    slot = i % 2

    def start_gather(tile, slot_):
        for r in range(MOE_BM):
            for k in range(2):
                row = dest_ref[2 * (tile * MOE_BM + r) + k]
                _row_gather_copy(y_hbm, ybuf.at[slot_, k], sem.at[slot_], row, r).start()

    @pl.when(i == 0)
    def _():
        start_gather(0, 0)

    @pl.when(i + 1 < n)
    def _():
        start_gather(i + 1, 1 - slot)

    for k in range(2):
        _block_wait_copy(y_hbm, ybuf.at[slot, k], sem.at[slot]).wait()
    wt = wt_ref[...]
    o_ref[...] = h_ref[...] + wt[:, 0:1] * ybuf[slot, 0] + wt[:, 1:2] * ybuf[slot, 1]


def _combine(h, wts, ybuf, dest):
    rows = h.shape[0]
    grid_spec = pltpu.PrefetchScalarGridSpec(
        num_scalar_prefetch=1,
        grid=(rows // MOE_BM,),
        in_specs=[
            pl.BlockSpec((MOE_BM, D_MODEL), lambda i, d: (i, 0)),
            pl.BlockSpec((MOE_BM, LANES), lambda i, d: (i, 0)),
            pl.BlockSpec(memory_space=pl.ANY),
        ],
        out_specs=pl.BlockSpec((MOE_BM, D_MODEL), lambda i, d: (i, 0)),
        scratch_shapes=[pltpu.VMEM((2, 2, MOE_BM, D_MODEL), F32), pltpu.SemaphoreType.DMA((2,))],
    )
    return pl.pallas_call(
        _combine_kernel,
        grid_spec=grid_spec,
        out_shape=jax.ShapeDtypeStruct((rows, D_MODEL), F32),
        compiler_params=_params(("arbitrary",)),
        name="moe_combine",
    )(dest, h, wts, ybuf)


def _moe(h, wts, ids, ranks, cnt, g, wg, wu, wd, layer):
    rows = h.shape[0]
    n_blk = -(-(2 * rows) // MOE_BM) + N_EXPERTS + 1
    dest, plan = _dispatch_plan(ids, ranks, cnt, n_blk)
    ybuf = _experts(h, g, plan, wg, wu, wd, layer, n_blk)
    return _combine(h, wts, ybuf, dest)


def _final_norm_kernel(x_ref, g_ref, o_ref):
    x = x_ref[...]
    ms = jnp.mean(x * x, axis=-1, keepdims=True)
    o_ref[...] = x * lax.rsqrt(ms + NORM_EPS) * g_ref[...]


def _final_norm(h, g, batch, nb):
    seq_blocks = nb - 1
    return pl.pallas_call(
        _final_norm_kernel,
        grid=(batch, seq_blocks),
        in_specs=[pl.BlockSpec((BLK, D_MODEL), lambda b, j: (b * nb + 1 + j, 0)),
                  pl.BlockSpec((1, D_MODEL), lambda b, j: (0, 0))],
        out_specs=pl.BlockSpec((BLK, D_MODEL), lambda b, j: (b * seq_blocks + j, 0)),
        out_shape=jax.ShapeDtypeStruct((batch * seq_blocks * BLK, D_MODEL), F32),
        compiler_params=_params(("parallel", "parallel")),
        name="final_norm",
    )(h, g)


def _rope_tables(batch, lp):
    half = HEAD_DIM // 2
    inv = ROPE_THETA ** (-jnp.arange(half, dtype=F32) / half)
    pos = (jnp.arange(lp, dtype=jnp.int32) - PAD).astype(F32)
    ang = pos[:, None] * inv[None, :]
    cos = jnp.tile(jnp.cos(ang), (batch, LANES // half))
    sin = jnp.sin(ang)
    sin = jnp.tile(jnp.concatenate([-sin, sin], axis=1), (batch, LANES // HEAD_DIM))
    return cos, sin


def kernel(x, meta_tokens, attn_norm_g, ffn_norm_g, w_in, swa_sinks, ssm_lambda_re, ssm_lambda_im, ssm_b_re, ssm_b_im, ssm_c_re, ssm_c_im, ssm_d, ssm_log_dt, ssm_w_glu, diff_lambda_q1, diff_lambda_k1, diff_lambda_q2, diff_lambda_k2, diff_subln_g, w_out, router_group_w, router_group_b, router_expert_w, router_expert_b, moe_w_gate, moe_w_up, moe_w_down, final_norm_g):
    batch, seq, d = x.shape
    depth = w_in.shape[0]
    lp = PAD + N_META + seq
    nb = lp // BLK
    meta = jnp.broadcast_to(meta_tokens[None].astype(x.dtype), (batch, N_META, d))
    h = jnp.concatenate([jnp.zeros((batch, PAD, d), x.dtype), meta, x], axis=1).reshape(batch * lp, d)
    cos, sin = _rope_tables(batch, lp)
    for l in range(depth):
        proj = _norm_inproj(h, attn_norm_g[l][None], w_in[l].astype(BF16), cos, sin)
        o_a = _swa(proj, swa_sinks[l].astype(F32), batch, nb)
        kflat, bw, cw, a_t = _ssm_operators(ssm_lambda_re[l], ssm_lambda_im[l], ssm_b_re[l], ssm_b_im[l],
                                            ssm_c_re[l], ssm_c_im[l], ssm_log_dt[l])
        yg = _ssm(proj, kflat, bw, cw, a_t, ssm_d[l].astype(F32).reshape(SSM_PACKS, 1, LANES), batch, lp)
        lam_init = 0.8 - 0.6 * math.exp(-0.3 * l)
        lam_vecs = jnp.stack([diff_lambda_q1[l], diff_lambda_k1[l], diff_lambda_q2[l], diff_lambda_k2[l]]).astype(F32)
        o_c = _diff_attn(proj, lam_vecs, diff_subln_g[l][None].astype(F32), lam_init, batch, nb)
        wo = w_out[l].astype(BF16)
        rw = jnp.concatenate([router_group_w[l], router_expert_w[l]], axis=1).astype(F32)
        rw = jnp.pad(rw, ((0, 0), (0, LANES - rw.shape[1])))
        rb = jnp.concatenate([router_group_b[l], router_expert_b[l]]).astype(F32)
        rb = jnp.pad(rb, (0, LANES - rb.shape[0]))[None]
        rw_hi, rw_lo = _split_bf16(rw)
        g_ffn = ffn_norm_g[l][None].astype(F32)
        h, wts, ids, ranks, cnt = _mixer_out(h, o_a, yg, o_c, wo, ssm_w_glu[l].astype(BF16), g_ffn, rw_hi, rw_lo, rb)
        h = _moe(h, wts, ids, ranks, cnt, g_ffn, moe_w_gate, moe_w_up, moe_w_down, l)
    out = _final_norm(h, final_norm_g[None], batch, nb)
    return out.reshape(batch, seq, d)
```

```python
import functools
import math

import jax
import jax.numpy as jnp
from jax import lax
from jax.experimental import pallas as pl
from jax.experimental.pallas import tpu as pltpu

D_MODEL = 2048
N_META = 16
BLK = 128
PAD = BLK - N_META
ROPE_THETA = 10000.0
NORM_EPS = 1e-5
NEG = -1e30
SWA_HEADS = 12
SWA_KV_HEADS = 4
HEAD_DIM = 64
SWA_WIDTH = SWA_HEADS * HEAD_DIM
SSM_WIDTH = 768
SSM_GROUP = 16
SSM_GROUPS = SSM_WIDTH // SSM_GROUP
SSM_STATE = 64
DIFF_HEADS = 4
DIFF_V_DIM = 128
DIFF_WIDTH = DIFF_HEADS * DIFF_V_DIM
IN_WIDTH = 3584
N_EXPERT_GROUPS = 4
EXPERTS_PER_GROUP = 8
N_EXPERTS = N_EXPERT_GROUPS * EXPERTS_PER_GROUP
D_EXPERT = 512

COL_KA = 768
COL_VA = 1024
COL_US = 1280
COL_QD = 2048
COL_KD = 2560
COL_VD = 3072

LANES = 128
SSM_T = 16
SSM_PACK = LANES // SSM_GROUP
SSM_PACKS = SSM_GROUPS // SSM_PACK
SSM_PSTATE = SSM_PACK * SSM_STATE
MOE_BM = 128
VMEM_LIMIT = 56 * 1024 * 1024

F32 = jnp.float32
BF16 = jnp.bfloat16


def _row_tile(rows):
    for t in (512, 256, 128):
        if rows % t == 0:
            return t
    raise ValueError(f"row count {rows} is not a multiple of 128")


def _params(sem, vmem=VMEM_LIMIT):
    return pltpu.CompilerParams(dimension_semantics=sem, vmem_limit_bytes=vmem)


def _rms_norm(x, g):
    return x * lax.rsqrt(jnp.mean(x * x, axis=-1, keepdims=True) + NORM_EPS) * g


IN_TN = 512
ROPE_TILES = (0, 1, 4, 5)


def _norm_inproj_kernel(x_ref, g_ref, w_ref, cos_ref, sin_ref, o_ref, xn_ref):
    x = x_ref[...]
    ms = jnp.mean(x * x, axis=-1, keepdims=True)
    xn_ref[...] = (x * lax.rsqrt(ms + NORM_EPS) * g_ref[...]).astype(BF16)
    shape = (x.shape[0], IN_TN)
    lane = lax.broadcasted_iota(jnp.int32, shape, 1)
    first_half = (lane & (HEAD_DIM - 1)) < (HEAD_DIM // 2)
    reps = IN_TN // LANES
    cos = jnp.tile(cos_ref[...], (1, reps))
    sin = jnp.tile(sin_ref[...], (1, reps))
    for j in range(IN_WIDTH // IN_TN):
        cols = slice(j * IN_TN, (j + 1) * IN_TN)
        acc = jnp.dot(xn_ref[...], w_ref[:, cols], preferred_element_type=F32)
        if j in ROPE_TILES:
            partner = jnp.where(first_half, pltpu.roll(acc, IN_TN - HEAD_DIM // 2, 1),
                                pltpu.roll(acc, HEAD_DIM // 2, 1))
            acc = acc * cos + partner * sin
        o_ref[:, cols] = acc.astype(o_ref.dtype)


def _norm_inproj(h, g, w, cos, sin):
    rows = h.shape[0]
    tm = _row_tile(rows)
    return pl.pallas_call(
        _norm_inproj_kernel,
        grid=(rows // tm,),
        in_specs=[
            pl.BlockSpec((tm, D_MODEL), lambda i: (i, 0)),
            pl.BlockSpec((1, D_MODEL), lambda i: (0, 0)),
            pl.BlockSpec((D_MODEL, IN_WIDTH), lambda i: (0, 0)),
            pl.BlockSpec((tm, LANES), lambda i: (i, 0)),
            pl.BlockSpec((tm, LANES), lambda i: (i, 0)),
        ],
        out_specs=pl.BlockSpec((tm, IN_WIDTH), lambda i: (i, 0)),
        out_shape=jax.ShapeDtypeStruct((rows, IN_WIDTH), BF16),
        scratch_shapes=[pltpu.VMEM((tm, D_MODEL), BF16)],
        compiler_params=_params(("parallel",)),
        name="norm_inproj",
    )(h, g, w, cos, sin)


def _swa_kernel(sink_ref, q_ref, kp_ref, kc_ref, vp_ref, vc_ref, o_ref):
    n = pl.program_id(1)
    shape = (BLK, 2 * BLK)
    qi = lax.broadcasted_iota(jnp.int32, shape, 0)
    c = lax.broadcasted_iota(jnp.int32, shape, 1)
    kj = (n - 1) * BLK + c
    delta = qi + BLK - c
    ok = (delta >= 0) & (delta < BLK) & (kj >= PAD)
    grp = SWA_HEADS // SWA_KV_HEADS
    outs = []
    for j in range(SWA_KV_HEADS):
        cols = slice(j * HEAD_DIM, (j + 1) * HEAD_DIM)
        kk = jnp.concatenate([kp_ref[:, cols], kc_ref[:, cols]], axis=0)
        vv = jnp.concatenate([vp_ref[:, cols], vc_ref[:, cols]], axis=0)
        for g in range(grp):
            hd = j * grp + g
            qh = q_ref[:, hd * HEAD_DIM:(hd + 1) * HEAD_DIM]
            s = lax.dot_general(qh, kk, (((1,), (1,)), ((), ())), preferred_element_type=F32) * (HEAD_DIM ** -0.5)
            s = jnp.where(ok, s, NEG)
            sink = sink_ref[hd]
            m = jnp.maximum(jnp.max(s, axis=-1, keepdims=True), sink)
            p = jnp.exp(s - m)
            denom = jnp.sum(p, axis=-1, keepdims=True) + jnp.exp(sink - m)
            outs.append(jnp.dot(p.astype(BF16), vv, preferred_element_type=F32) / denom)
    o_ref[...] = jnp.concatenate(outs, axis=1).astype(o_ref.dtype)


def _swa(proj, sinks, batch, nb):
    rows = proj.shape[0]
    kvw = SWA_KV_HEADS * HEAD_DIM

    def cur(col):
        return lambda b, n: (b * nb + n, col)

    def prev(col):
        return lambda b, n: (b * nb + jnp.maximum(n - 1, 0), col)

    return pl.pallas_call(
        _swa_kernel,
        grid=(batch, nb),
        in_specs=[
            pl.BlockSpec(memory_space=pltpu.SMEM),
            pl.BlockSpec((BLK, SWA_WIDTH), cur(0)),
            pl.BlockSpec((BLK, kvw), prev(COL_KA // kvw)),
            pl.BlockSpec((BLK, kvw), cur(COL_KA // kvw)),
            pl.BlockSpec((BLK, kvw), prev(COL_VA // kvw)),
            pl.BlockSpec((BLK, kvw), cur(COL_VA // kvw)),
        ],
        out_specs=pl.BlockSpec((BLK, SWA_WIDTH), cur(0)),
        out_shape=jax.ShapeDtypeStruct((rows, SWA_WIDTH), BF16),
        compiler_params=_params(("parallel", "parallel")),
        name="swa",
    )(sinks, proj, proj, proj, proj, proj)


def _diff_kernel(lam_ref, g_ref, q_ref, k_ref, v_ref, o_ref, *, lam_init, nb):
    row = lax.broadcasted_iota(jnp.int32, (BLK, BLK), 0)
    col = lax.broadcasted_iota(jnp.int32, (BLK, BLK), 1)
    causal_ok = col <= row
    pad_ok = col >= PAD
    lane = lax.broadcasted_iota(jnp.int32, (BLK, LANES), 1)
    lv = lam_ref[...]
    lam = (jnp.exp(jnp.sum(lv[0:1] * lv[1:2], axis=-1, keepdims=True))
           - jnp.exp(jnp.sum(lv[2:3] * lv[3:4], axis=-1, keepdims=True)) + lam_init)
    gain = g_ref[...] * (1.0 - lam_init)
    scale = HEAD_DIM ** -0.5
    for n in range(nb):
        ke = (n + 1) * BLK
        q = q_ref[n * BLK:(n + 1) * BLK, :] * scale
        kx = k_ref[0:ke, :]
        vx = v_ref[0:ke, :]
        heads = []
        for s_idx in range(2):
            qx = jnp.where((lane >= HEAD_DIM) == bool(s_idx), q, jnp.zeros_like(q))
            s = lax.dot_general(qx, kx, (((1,), (1,)), ((), ())), preferred_element_type=F32)
            if n == 0:
                s = jnp.where(causal_ok & pad_ok, s, NEG)
            else:
                first = jnp.where(pad_ok, s[:, :BLK], NEG)
                last = jnp.where(causal_ok, s[:, ke - BLK:], NEG)
                middle = [s[:, BLK:ke - BLK]] if n > 1 else []
                s = jnp.concatenate([first] + middle + [last], axis=1)
            m = jnp.max(s, axis=-1, keepdims=True)
            p = jnp.exp(s - m)
            denom = jnp.sum(p, axis=-1, keepdims=True)
            heads.append(jnp.dot(p.astype(BF16), vx, preferred_element_type=F32) / denom)
        o = heads[0] - lam * heads[1]
        o = o * lax.rsqrt(jnp.mean(o * o, axis=-1, keepdims=True) + NORM_EPS) * gain
        o_ref[n * BLK:(n + 1) * BLK, :] = o.astype(o_ref.dtype)


def _diff_attn(proj, lam_vecs, subln_g, lam_init, batch, nb):
    rows = proj.shape[0]
    lp = nb * BLK
    return pl.pallas_call(
        functools.partial(_diff_kernel, lam_init=lam_init, nb=nb),
        grid=(batch, DIFF_HEADS),
        in_specs=[
            pl.BlockSpec((4, HEAD_DIM), lambda b, h: (0, 0)),
            pl.BlockSpec((1, DIFF_V_DIM), lambda b, h: (0, 0)),
            pl.BlockSpec((lp, LANES), lambda b, h: (b, COL_QD // LANES + h)),
            pl.BlockSpec((lp, LANES), lambda b, h: (b, COL_KD // LANES + h)),
            pl.BlockSpec((lp, LANES), lambda b, h: (b, COL_VD // LANES + h)),
        ],
        out_specs=pl.BlockSpec((lp, DIFF_V_DIM), lambda b, h: (b, h)),
        out_shape=jax.ShapeDtypeStruct((rows, DIFF_WIDTH), BF16),
        compiler_params=_params(("parallel", "parallel")),
        name="diff_attn",
    )(lam_vecs, subln_g, proj, proj, proj)


def _ssm_row_chunk(rows):
    for c in (544, 512, 384, 256, 128, 64, 32, 16):
        if rows % c == 0:
            return c
    raise ValueError(rows)


def _ssm_expand_operators(kc_ref, bc_ref, cc_ref, k_ref, bw_ref, cw_ref):
    def iota(shape, dim):
        return lax.broadcasted_iota(jnp.int32, shape, dim)

    def group(idx, width):
        return jnp.right_shift(idx, width.bit_length() - 1) & (SSM_PACK - 1)

    hs, ps, tl = SSM_GROUP, SSM_STATE, SSM_T * LANES
    spread_h = ((iota((LANES, hs), 0) & (hs - 1)) == iota((LANES, hs), 1)).astype(BF16)
    kt = jnp.dot(spread_h, kc_ref[...], preferred_element_type=F32)
    same = group(iota((LANES, tl), 0), hs) == group(iota((LANES, tl), 1), hs)
    k_ref[...] = jnp.where(same, kt, 0.0).astype(BF16)
    r = iota((2 * ps, 2 * SSM_PSTATE), 0)
    c = iota((2 * ps, 2 * SSM_PSTATE), 1)
    spread_s = ((jnp.right_shift(r, 6) == jnp.right_shift(c, 9)) & ((r & (ps - 1)) == (c & (ps - 1)))).astype(BF16)
    bw = jnp.dot(bc_ref[...], spread_s, preferred_element_type=F32)
    shape = (tl, 2 * SSM_PSTATE)
    same = group(iota(shape, 0), hs) == group(iota(shape, 1), ps)
    bw_ref[...] = jnp.where(same, bw, 0.0).astype(BF16)
    shape = (LANES, 2 * SSM_PSTATE)
    same = group(iota(shape, 0), hs) == group(iota(shape, 1), ps)
    for j in range(SSM_T):
        ct = jnp.dot(spread_h, cc_ref[j], preferred_element_type=F32)
        cw_ref[j] = jnp.where(same, ct, 0.0).astype(BF16)


def _ssm_kernel(u_ref, kc_ref, bc_ref, cc_ref, a_ref, d_ref, o_ref, up_ref, y_ref, e_ref, s_ref, k_ref, bw_ref,
                cw_ref, *, batch, lp):
    rows = batch * lp
    n_chunks = lp // SSM_T
    tot_chunks = rows // SSM_T
    _ssm_expand_operators(kc_ref, bc_ref, cc_ref, k_ref, bw_ref, cw_ref)
    contract_lanes = (((1,), (1,)), ((), ()))
    up_ref[0:SSM_T, :] = jnp.zeros((SSM_T, LANES), F32)
    up_ref[SSM_T:, :] = u_ref[...].astype(F32)
    for b in range(batch):
        up_ref[SSM_T + b * lp:SSM_T + b * lp + PAD, :] = jnp.zeros((PAD, LANES), F32)

    ch = _ssm_row_chunk(rows)
    rmod = lax.broadcasted_iota(jnp.int32, (ch, LANES), 0) & (SSM_T - 1)
    for r0 in range(0, rows, ch):
        pieces = []
        for k in range(SSM_T):
            sl = up_ref[SSM_T - k + r0:SSM_T - k + r0 + ch, :]
            pieces.append(jnp.where(rmod >= k, sl, 0.0).astype(BF16))
        y_ref[r0:r0 + ch, :] = lax.dot_general(jnp.concatenate(pieces, axis=1), k_ref[...], contract_lanes,
                                               preferred_element_type=F32)

    pieces = [up_ref[pl.ds(SSM_T + i, tot_chunks, stride=SSM_T), :].astype(BF16) for i in range(SSM_T)]
    e_ref[...] = jnp.dot(jnp.concatenate(pieces, axis=1), bw_ref[...], preferred_element_type=F32)

    a_re = a_ref[:, :SSM_PSTATE]
    a_im = a_ref[:, SSM_PSTATE:]

    def step(c, carry):
        new = []
        for b in range(batch):
            sr, si = carry[b]
            idx = b * n_chunks + c
            s_ref[pl.ds(idx, 1), :SSM_PSTATE] = sr
            s_ref[pl.ds(idx, 1), SSM_PSTATE:] = si
            er = e_ref[pl.ds(idx, 1), :SSM_PSTATE]
            ei = e_ref[pl.ds(idx, 1), SSM_PSTATE:]
            new.append((a_re * sr - a_im * si + er, a_re * si + a_im * sr + ei))
        return tuple(new)

    zero = jnp.zeros((1, SSM_PSTATE), F32)
    lax.fori_loop(0, n_chunks, step, tuple((zero, zero) for _ in range(batch)))

    sb = s_ref[...].astype(BF16)
    for j in range(SSM_T):
        z = lax.dot_general(sb, cw_ref[j], contract_lanes, preferred_element_type=F32)
        idx = pl.ds(j, tot_chunks, stride=SSM_T)
        y_ref[idx, :] = y_ref[idx, :] + z

    y = y_ref[...] + d_ref[...] * up_ref[SSM_T:, :]
    o_ref[...] = jax.nn.gelu(y, approximate=True).astype(o_ref.dtype)


def _ssm(proj, kc, bc, cc, a_t, dskip, batch, lp):
    rows = proj.shape[0]
    tot_chunks = rows // SSM_T
    return pl.pallas_call(
        functools.partial(_ssm_kernel, batch=batch, lp=lp),
        grid=(SSM_PACKS,),
        in_specs=[
            pl.BlockSpec((rows, LANES), lambda p: (0, COL_US // LANES + p)),
            pl.BlockSpec((None, SSM_GROUP, SSM_T * LANES), lambda p: (p, 0, 0)),
            pl.BlockSpec((None, SSM_T * LANES, 2 * SSM_STATE), lambda p: (p, 0, 0)),
            pl.BlockSpec((None, SSM_T, SSM_GROUP, 2 * SSM_PSTATE), lambda p: (p, 0, 0, 0)),
            pl.BlockSpec((None, 1, 2 * SSM_PSTATE), lambda p: (p, 0, 0)),
            pl.BlockSpec((None, 1, LANES), lambda p: (p, 0, 0)),
        ],
        out_specs=pl.BlockSpec((rows, LANES), lambda p: (0, p)),
        out_shape=jax.ShapeDtypeStruct((rows, SSM_WIDTH), F32),
        scratch_shapes=[
            pltpu.VMEM((rows + SSM_T, LANES), F32),
            pltpu.VMEM((rows, LANES), F32),
            pltpu.VMEM((tot_chunks, 2 * SSM_PSTATE), F32),
            pltpu.VMEM((tot_chunks, 2 * SSM_PSTATE), F32),
            pltpu.VMEM((LANES, SSM_T * LANES), BF16),
            pltpu.VMEM((SSM_T * LANES, 2 * SSM_PSTATE), BF16),
            pltpu.VMEM((SSM_T, LANES, 2 * SSM_PSTATE), BF16),
        ],
        compiler_params=_params(("parallel",)),
        name="ssm",
    )(proj, kc, bc, cc, a_t, dskip)


def _ssm_operators(lam_re, lam_im, b_re, b_im, c_re, c_im, log_dt):
    p, h = SSM_STATE, SSM_GROUP
    dt = jnp.exp(log_dt.astype(F32))[:, None]
    lr, li = lam_re.astype(F32), lam_im.astype(F32)
    mag = jnp.exp(lr * dt)
    abar_re, abar_im = mag * jnp.cos(li * dt), mag * jnp.sin(li * dt)
    den = lr * lr + li * li
    nr = abar_re - 1.0
    f_re = (nr * lr + abar_im * li) / den
    f_im = (abar_im * lr - nr * li) / den
    br, bi = b_re.astype(F32), b_im.astype(F32)
    bbar_re = f_re[..., None] * br - f_im[..., None] * bi
    bbar_im = f_re[..., None] * bi + f_im[..., None] * br
    ks = jnp.arange(SSM_T + 1, dtype=F32)[:, None, None]
    pmag = jnp.exp(ks * (lr * dt)[None])
    pw_re = pmag * jnp.cos(ks * (li * dt)[None])
    pw_im = pmag * jnp.sin(ks * (li * dt)[None])
    cr, ci = c_re.astype(F32), c_im.astype(F32)

    packs, q = SSM_PACKS, SSM_PACK

    cb_re = cr[:, :, :, None] * bbar_re[:, None, :, :] - ci[:, :, :, None] * bbar_im[:, None, :, :]
    cb_im = cr[:, :, :, None] * bbar_im[:, None, :, :] + ci[:, :, :, None] * bbar_re[:, None, :, :]
    klag = (jnp.einsum('kgs,gosi->kgio', pw_re[:SSM_T], cb_re)
            - jnp.einsum('kgs,gosi->kgio', pw_im[:SSM_T], cb_im))
    kc = klag.reshape(SSM_T, packs, q, h, h).transpose(1, 4, 0, 2, 3).reshape(packs, h, SSM_T * LANES)

    rev_re, rev_im = pw_re[SSM_T - 1::-1], pw_im[SSM_T - 1::-1]
    inj_re = rev_re[..., None] * bbar_re[None] - rev_im[..., None] * bbar_im[None]
    inj_im = rev_re[..., None] * bbar_im[None] + rev_im[..., None] * bbar_re[None]
    inj = jnp.stack([inj_re, inj_im]).reshape(2, SSM_T, packs, q, p, h)
    bc = inj.transpose(2, 1, 3, 5, 0, 4).reshape(packs, SSM_T * LANES, 2 * p)

    ca_re = cr[None] * pw_re[1:, :, None, :] - ci[None] * pw_im[1:, :, None, :]
    ca_im = cr[None] * pw_im[1:, :, None, :] + ci[None] * pw_re[1:, :, None, :]
    ca = jnp.stack([ca_re, -ca_im]).reshape(2, SSM_T, packs, q, h, p)
    cc = ca.transpose(2, 1, 4, 0, 3, 5).reshape(packs, SSM_T, h, 2 * SSM_PSTATE)

    a_t = jnp.concatenate([pw_re[SSM_T].reshape(packs, 1, SSM_PSTATE),
                           pw_im[SSM_T].reshape(packs, 1, SSM_PSTATE)], axis=-1)
    return kc.astype(BF16), bc.astype(BF16), cc.astype(BF16), a_t


OUT_TN = 512


def _split_bf16(x):
    hi = x.astype(BF16)
    return hi, (x - hi.astype(F32)).astype(BF16)


def _route(hn, rwh_ref, rwl_ref, rb_ref):
    hi, lo = _split_bf16(hn)
    logits = (jnp.dot(hi, rwh_ref[...], preferred_element_type=F32)
              + (jnp.dot(lo, rwh_ref[...], preferred_element_type=F32)
                 + jnp.dot(hi, rwl_ref[...], preferred_element_type=F32))) + rb_ref[...]
    lane = lax.broadcasted_iota(jnp.int32, logits.shape, 1)

    def first_argmax(v, vmax):
        return jnp.min(jnp.where(v == vmax, lane, LANES), axis=-1, keepdims=True)

    gl = jnp.where(lane < N_EXPERT_GROUPS, logits, NEG)
    gmax = jnp.max(gl, axis=-1, keepdims=True)
    g_sel = first_argmax(gl, gmax)
    p_g = 1.0 / jnp.sum(jnp.exp(gl - gmax), axis=-1, keepdims=True)
    e_lane = lane - N_EXPERT_GROUPS
    in_group = (e_lane >= 0) & (e_lane < N_EXPERTS) & (jnp.right_shift(e_lane, 3) == g_sel)
    el = jnp.where(in_group, logits, NEG)
    m1 = jnp.max(el, axis=-1, keepdims=True)
    i1 = first_argmax(el, m1)
    el2 = jnp.where(lane == i1, NEG, el)
    m2 = jnp.max(el2, axis=-1, keepdims=True)
    i2 = first_argmax(el2, m2)
    z = jnp.sum(jnp.exp(el - m1), axis=-1, keepdims=True)
    p1 = 1.0 / z
    p2 = jnp.exp(m2 - m1) / z
    w1 = p1 / (p1 + p2) * p_g
    w2 = p2 / (p1 + p2) * p_g
    wts = jnp.where(lane == 0, w1, jnp.where(lane == 1, w2, 0.0))
    ids = jnp.where(lane == 0, i1 - N_EXPERT_GROUPS, jnp.where(lane == 1, i2 - N_EXPERT_GROUPS, 0))
    return wts, ids


def _assignment_ranks(ids, run_ref):
    tm = ids.shape[0]
    lane = lax.broadcasted_iota(jnp.int32, ids.shape, 1)
    oh0 = lane == ids[:, 0:1]
    oh1 = lane == ids[:, 1:2]
    hits = (oh0 | oh1).astype(BF16)
    r = lax.broadcasted_iota(jnp.int32, (tm, tm), 0)
    c = lax.broadcasted_iota(jnp.int32, (tm, tm), 1)
    before = jnp.dot((c < r).astype(BF16), hits, preferred_element_type=F32) + run_ref[...]
    rank0 = jnp.sum(jnp.where(oh0, before, 0.0), axis=-1, keepdims=True)
    rank1 = jnp.sum(jnp.where(oh1, before, 0.0), axis=-1, keepdims=True)
    run_ref[...] += jnp.sum(hits.astype(F32), axis=0, keepdims=True)
    return jnp.where(lane == 0, rank0, jnp.where(lane == 1, rank1, 0.0)).astype(jnp.int32)


def _mixer_out_kernel(h_ref, a_ref, y_ref, c_ref, wo_ref, wglu_ref, g_ref, rwh_ref, rwl_ref, rb_ref,
                      o_ref, wt_ref, id_ref, rank_ref, cnt_ref, run_ref):
    @pl.when(pl.program_id(0) == 0)
    def _():
        run_ref[...] = jnp.zeros(run_ref.shape, F32)

    y = y_ref[...]
    gate = jnp.dot(y.astype(BF16), wglu_ref[...], preferred_element_type=F32)
    b = (y * jax.nn.sigmoid(gate)).astype(BF16)
    mixed = jnp.concatenate([a_ref[...], b, c_ref[...]], axis=1)
    for j in range(D_MODEL // OUT_TN):
        cols = slice(j * OUT_TN, (j + 1) * OUT_TN)
        o_ref[:, cols] = h_ref[:, cols] + jnp.dot(mixed, wo_ref[:, cols], preferred_element_type=F32)
    x = o_ref[...]
    hn = _rms_norm(x, g_ref[...])
    wts, ids = _route(hn, rwh_ref, rwl_ref, rb_ref)
    wt_ref[...] = wts
    id_ref[...] = ids
    rank_ref[...] = _assignment_ranks(ids, run_ref)
    cnt_ref[...] = run_ref[...].astype(jnp.int32)


def _mixer_out(h, oa, yg, oc, wo, wglu, g, rw_hi, rw_lo, rb):
    rows = h.shape[0]
    tm = _row_tile(rows)

    def row_spec(width):
        return pl.BlockSpec((tm, width), lambda i: (i, 0))

    def whole(shape):
        return pl.BlockSpec(shape, lambda i: (0, 0))

    return pl.pallas_call(
        _mixer_out_kernel,
        grid=(rows // tm,),
        in_specs=[
            row_spec(D_MODEL), row_spec(SWA_WIDTH), row_spec(SSM_WIDTH), row_spec(DIFF_WIDTH),
            whole((D_MODEL, D_MODEL)), whole((SSM_WIDTH, SSM_WIDTH)), whole((1, D_MODEL)),
            whole((D_MODEL, LANES)), whole((D_MODEL, LANES)), whole((1, LANES)),
        ],
        out_specs=[row_spec(D_MODEL), row_spec(LANES), row_spec(LANES), row_spec(LANES), whole((1, LANES))],
        out_shape=[
            jax.ShapeDtypeStruct((rows, D_MODEL), F32),
            jax.ShapeDtypeStruct((rows, LANES), F32),
            jax.ShapeDtypeStruct((rows, LANES), jnp.int32),
            jax.ShapeDtypeStruct((rows, LANES), jnp.int32),
            jax.ShapeDtypeStruct((1, LANES), jnp.int32),
        ],
        scratch_shapes=[pltpu.VMEM((1, LANES), F32)],
        compiler_params=_params(("arbitrary",)),
        name="mixer_out",
    )(h, oa, yg, oc, wo, wglu, g, rw_hi, rw_lo, rb)


def _dispatch_plan(ids, ranks, cnt, n_blk):
    e = ids[:, :2].reshape(-1)
    n_asg = e.shape[0]
    counts = cnt[0, :N_EXPERTS]
    padded = (counts + MOE_BM - 1) // MOE_BM * MOE_BM
    pend = jnp.cumsum(padded)
    pstart = pend - padded
    onehot = e[:, None] == jnp.arange(N_EXPERTS, dtype=jnp.int32)[None, :]
    dest = jnp.sum(jnp.where(onehot, pstart[None, :], 0), axis=1).astype(jnp.int32) + ranks[:, :2].reshape(-1)
    src = jnp.zeros((n_blk * MOE_BM,), jnp.int32).at[dest].set(jnp.arange(n_asg, dtype=jnp.int32) // 2)
    blk_start = jnp.arange(n_blk, dtype=jnp.int32) * MOE_BM
    blk_expert = jnp.minimum(jnp.sum((pend[None, :] <= blk_start[:, None]).astype(jnp.int32), axis=1), N_EXPERTS - 1)
    n_used = (pend[-1] // MOE_BM).astype(jnp.int32).reshape(1)
    ex = jnp.arange(N_EXPERTS, dtype=jnp.int32)
    nonempty = counts > 0
    run_parity = (jnp.cumsum(nonempty.astype(jnp.int32)) - 1) & 1
    later = (ex[None, :] > ex[:, None]) & nonempty[None, :]
    nxt = jnp.min(jnp.where(later, ex[None, :], N_EXPERTS), axis=1)
    nxt = jnp.where(nxt == N_EXPERTS, -1, nxt)
    plan = dict(src=src, blk_expert=blk_expert, n_used=n_used, blk_parity=run_parity[blk_expert].astype(jnp.int32),
                blk_next=nxt[blk_expert].astype(jnp.int32))
    return dest, plan


def _row_gather_copy(src_hbm, dst_buf, sem, src_row, dst_row):
    return pltpu.make_async_copy(src_hbm.at[pl.ds(src_row, 1), :], dst_buf.at[pl.ds(dst_row, 1), :], sem)


def _block_wait_copy(src_hbm, dst_buf, sem):
    return pltpu.make_async_copy(src_hbm.at[pl.ds(0, dst_buf.shape[0]), :], dst_buf, sem)


GATHER_SLOTS = 3
WEIGHT_DMA_PRIORITY = 1


def _expert_kernel(be_ref, par_ref, nxt_ref, src_ref, nused_ref, hn_hbm, g_ref, wg_hbm, wu_hbm, wd_hbm, o_ref,
                   xbuf, wgf, wuf, wdf, wgb, wub, wdb, gsem, wsem, *, layer):
    i = pl.program_id(0)
    n_used = nused_ref[0]
    slot = i % GATHER_SLOTS

    def start_gather(blk, slot_):
        for r in range(MOE_BM):
            _row_gather_copy(hn_hbm, xbuf.at[slot_], gsem.at[slot_], src_ref[blk * MOE_BM + r], r).start()

    def weight_copies(expert, wslot):
        pairs = ((wg_hbm, wgf), (wu_hbm, wuf), (wd_hbm, wdf))
        return [pltpu.make_async_copy(w.at[layer, expert], buf.at[wslot], wsem.at[wslot]) for w, buf in pairs]

    @pl.when(i == 0)
    def _():
        for c in weight_copies(be_ref[0], 0):
            c.start(priority=WEIGHT_DMA_PRIORITY)
        start_gather(0, 0)
        start_gather(jnp.minimum(1, n_used), 1)

    @pl.when(i <= n_used + 1)
    def _():
        _block_wait_copy(hn_hbm, xbuf.at[slot], gsem.at[slot]).wait()

    @pl.when(i < n_used)
    def _():
        @pl.when((i == 0) | (be_ref[i] != be_ref[jnp.maximum(i - 1, 0)]))
        def _():
            wslot = par_ref[i]
            for c in weight_copies(be_ref[i], wslot):
                c.wait()

            @pl.when(nxt_ref[i] >= 0)
            def _():
                for c in weight_copies(nxt_ref[i], 1 - wslot):
                    c.start(priority=WEIGHT_DMA_PRIORITY)

            wgb[...] = wgf[wslot].astype(BF16)
            wub[...] = wuf[wslot].astype(BF16)
            wdb[...] = wdf[wslot].astype(BF16)

        x = _rms_norm(xbuf[slot], g_ref[...]).astype(BF16)
        start_gather(jnp.minimum(i + 2, n_used), (i + 2) % GATHER_SLOTS)
        gate = jnp.dot(x, wgb[...], preferred_element_type=F32)
        up = jnp.dot(x, wub[...], preferred_element_type=F32)
        act = (jax.nn.silu(gate) * up).astype(BF16)
        o_ref[...] = jnp.dot(act, wdb[...], preferred_element_type=F32)

    @pl.when(i >= n_used)
    def _():
        o_ref[...] = jnp.zeros(o_ref.shape, o_ref.dtype)


def _experts(h, g, plan, wg, wu, wd, layer, n_blk):
    hbm = pl.BlockSpec(memory_space=pl.ANY)
    grid_spec = pltpu.PrefetchScalarGridSpec(
        num_scalar_prefetch=5,
        grid=(n_blk,),
        in_specs=[hbm, pl.BlockSpec((1, D_MODEL), lambda i, *_: (0, 0)), hbm, hbm, hbm],
        out_specs=pl.BlockSpec((MOE_BM, D_MODEL), lambda i, *_: (i, 0)),
        scratch_shapes=[
            pltpu.VMEM((GATHER_SLOTS, MOE_BM, D_MODEL), F32),
            pltpu.VMEM((2, D_MODEL, D_EXPERT), F32),
            pltpu.VMEM((2, D_MODEL, D_EXPERT), F32),
            pltpu.VMEM((2, D_EXPERT, D_MODEL), F32),
            pltpu.VMEM((D_MODEL, D_EXPERT), BF16),
            pltpu.VMEM((D_MODEL, D_EXPERT), BF16),
            pltpu.VMEM((D_EXPERT, D_MODEL), BF16),
            pltpu.SemaphoreType.DMA((GATHER_SLOTS,)),
            pltpu.SemaphoreType.DMA((2,)),
        ],
    )
    return pl.pallas_call(
        functools.partial(_expert_kernel, layer=layer),
        grid_spec=grid_spec,
        out_shape=jax.ShapeDtypeStruct((n_blk * MOE_BM, D_MODEL), F32),
        compiler_params=_params(("arbitrary",)),
        name="moe_experts",
    )(plan["blk_expert"], plan["blk_parity"], plan["blk_next"], plan["src"], plan["n_used"], h, g, wg, wu, wd)


def _combine_kernel(dest_ref, h_ref, wt_ref, y_hbm, o_ref, ybuf, sem):
    i = pl.program_id(0)
    n = pl.num_programs(0)
    slot = i % 2

    def start_gather(tile, slot_):
        for r in range(MOE_BM):
            for k in range(2):
                row = dest_ref[2 * (tile * MOE_BM + r) + k]
                _row_gather_copy(y_hbm, ybuf.at[slot_, k], sem.at[slot_], row, r).start(priority=k)

    @pl.when(i == 0)
    def _():
        start_gather(0, 0)

    @pl.when(i + 1 < n)
    def _():
        start_gather(i + 1, 1 - slot)

    for k in range(2):
        _block_wait_copy(y_hbm, ybuf.at[slot, k], sem.at[slot]).wait()
    wt = wt_ref[...]
    o_ref[...] = h_ref[...] + wt[:, 0:1] * ybuf[slot, 0] + wt[:, 1:2] * ybuf[slot, 1]


def _combine(h, wts, ybuf, dest):
    rows = h.shape[0]
    grid_spec = pltpu.PrefetchScalarGridSpec(
        num_scalar_prefetch=1,
        grid=(rows // MOE_BM,),
        in_specs=[
            pl.BlockSpec((MOE_BM, D_MODEL), lambda i, d: (i, 0)),
            pl.BlockSpec((MOE_BM, LANES), lambda i, d: (i, 0)),
            pl.BlockSpec(memory_space=pl.ANY),
        ],
        out_specs=pl.BlockSpec((MOE_BM, D_MODEL), lambda i, d: (i, 0)),
        scratch_shapes=[pltpu.VMEM((2, 2, MOE_BM, D_MODEL), F32), pltpu.SemaphoreType.DMA((2,))],
    )
    return pl.pallas_call(
        _combine_kernel,
        grid_spec=grid_spec,
        out_shape=jax.ShapeDtypeStruct((rows, D_MODEL), F32),
        compiler_params=_params(("arbitrary",)),
        name="moe_combine",
    )(dest, h, wts, ybuf)


def _moe(h, wts, ids, ranks, cnt, g, wg, wu, wd, layer):
    rows = h.shape[0]
    n_blk = -(-(2 * rows) // MOE_BM) + N_EXPERTS + 1
    dest, plan = _dispatch_plan(ids, ranks, cnt, n_blk)
    ybuf = _experts(h, g, plan, wg, wu, wd, layer, n_blk)
    return _combine(h, wts, ybuf, dest)


def _final_norm_kernel(x_ref, g_ref, o_ref):
    x = x_ref[...]
    ms = jnp.mean(x * x, axis=-1, keepdims=True)
    o_ref[...] = x * lax.rsqrt(ms + NORM_EPS) * g_ref[...]


def _final_norm(h, g, batch, nb):
    seq_blocks = nb - 1
    return pl.pallas_call(
        _final_norm_kernel,
        grid=(batch, seq_blocks),
        in_specs=[pl.BlockSpec((BLK, D_MODEL), lambda b, j: (b * nb + 1 + j, 0)),
                  pl.BlockSpec((1, D_MODEL), lambda b, j: (0, 0))],
        out_specs=pl.BlockSpec((BLK, D_MODEL), lambda b, j: (b * seq_blocks + j, 0)),
        out_shape=jax.ShapeDtypeStruct((batch * seq_blocks * BLK, D_MODEL), F32),
        compiler_params=_params(("parallel", "parallel")),
        name="final_norm",
    )(h, g)


def _rope_tables(batch, lp):
    half = HEAD_DIM // 2
    inv = ROPE_THETA ** (-jnp.arange(half, dtype=F32) / half)
    pos = (jnp.arange(lp, dtype=jnp.int32) - PAD).astype(F32)
    ang = pos[:, None] * inv[None, :]
    cos = jnp.tile(jnp.cos(ang), (batch, LANES // half))
    sin = jnp.sin(ang)
    sin = jnp.tile(jnp.concatenate([-sin, sin], axis=1), (batch, LANES // HEAD_DIM))
    return cos, sin


def kernel(x, meta_tokens, attn_norm_g, ffn_norm_g, w_in, swa_sinks, ssm_lambda_re, ssm_lambda_im, ssm_b_re, ssm_b_im, ssm_c_re, ssm_c_im, ssm_d, ssm_log_dt, ssm_w_glu, diff_lambda_q1, diff_lambda_k1, diff_lambda_q2, diff_lambda_k2, diff_subln_g, w_out, router_group_w, router_group_b, router_expert_w, router_expert_b, moe_w_gate, moe_w_up, moe_w_down, final_norm_g):
    batch, seq, d = x.shape
    depth = w_in.shape[0]
    lp = PAD + N_META + seq
    nb = lp // BLK
    meta = jnp.broadcast_to(meta_tokens[None].astype(x.dtype), (batch, N_META, d))
    h = jnp.concatenate([jnp.zeros((batch, PAD, d), x.dtype), meta, x], axis=1).reshape(batch * lp, d)
    cos, sin = _rope_tables(batch, lp)
    for l in range(depth):
        proj = _norm_inproj(h, attn_norm_g[l][None], w_in[l].astype(BF16), cos, sin)
        o_a = _swa(proj, swa_sinks[l].astype(F32), batch, nb)
        kflat, bw, cw, a_t = _ssm_operators(ssm_lambda_re[l], ssm_lambda_im[l], ssm_b_re[l], ssm_b_im[l],
                                            ssm_c_re[l], ssm_c_im[l], ssm_log_dt[l])
        yg = _ssm(proj, kflat, bw, cw, a_t, ssm_d[l].astype(F32).reshape(SSM_PACKS, 1, LANES), batch, lp)
        lam_init = 0.8 - 0.6 * math.exp(-0.3 * l)
        lam_vecs = jnp.stack([diff_lambda_q1[l], diff_lambda_k1[l], diff_lambda_q2[l], diff_lambda_k2[l]]).astype(F32)
        o_c = _diff_attn(proj, lam_vecs, diff_subln_g[l][None].astype(F32), lam_init, batch, nb)
        wo = w_out[l].astype(BF16)
        rw = jnp.concatenate([router_group_w[l], router_expert_w[l]], axis=1).astype(F32)
        rw = jnp.pad(rw, ((0, 0), (0, LANES - rw.shape[1])))
        rb = jnp.concatenate([router_group_b[l], router_expert_b[l]]).astype(F32)
        rb = jnp.pad(rb, (0, LANES - rb.shape[0]))[None]
        rw_hi, rw_lo = _split_bf16(rw)
        g_ffn = ffn_norm_g[l][None].astype(F32)
        h, wts, ids, ranks, cnt = _mixer_out(h, o_a, yg, o_c, wo, ssm_w_glu[l].astype(BF16), g_ffn, rw_hi, rw_lo, rb)
        h = _moe(h, wts, ids, ranks, cnt, g_ffn, moe_w_gate, moe_w_up, moe_w_down, l)
    out = _final_norm(h, final_norm_g[None], batch, nb)
    return out.reshape(batch, seq, d)
```

```python
import functools
import math

import jax
import jax.numpy as jnp
from jax import lax
from jax.experimental import pallas as pl
from jax.experimental.pallas import tpu as pltpu

D_MODEL = 2048
N_META = 16
BLK = 128
PAD = BLK - N_META
ROPE_THETA = 10000.0
NORM_EPS = 1e-5
NEG = -1e30
SWA_HEADS = 12
SWA_KV_HEADS = 4
HEAD_DIM = 64
SWA_WIDTH = SWA_HEADS * HEAD_DIM
SSM_WIDTH = 768
SSM_GROUP = 16
SSM_GROUPS = SSM_WIDTH // SSM_GROUP
SSM_STATE = 64
DIFF_HEADS = 4
DIFF_V_DIM = 128
DIFF_WIDTH = DIFF_HEADS * DIFF_V_DIM
IN_WIDTH = 3584
N_EXPERT_GROUPS = 4
EXPERTS_PER_GROUP = 8
N_EXPERTS = N_EXPERT_GROUPS * EXPERTS_PER_GROUP
D_EXPERT = 512

COL_KA = 768
COL_VA = 1024
COL_US = 1280
COL_QD = 2048
COL_KD = 2560
COL_VD = 3072

LANES = 128
SSM_T = 16
SSM_PACK = LANES // SSM_GROUP
SSM_PACKS = SSM_GROUPS // SSM_PACK
SSM_PSTATE = SSM_PACK * SSM_STATE
MOE_BM = 128
VMEM_LIMIT = 56 * 1024 * 1024

F32 = jnp.float32
BF16 = jnp.bfloat16


def _row_tile(rows):
    for t in (512, 256, 128):
        if rows % t == 0:
            return t
    raise ValueError(f"row count {rows} is not a multiple of 128")


def _params(sem, vmem=VMEM_LIMIT):
    return pltpu.CompilerParams(dimension_semantics=sem, vmem_limit_bytes=vmem)


def _rms_norm(x, g):
    return x * lax.rsqrt(jnp.mean(x * x, axis=-1, keepdims=True) + NORM_EPS) * g


IN_TN = 512
ROPE_TILES = (0, 1, 4, 5)


def _norm_inproj_kernel(x_ref, g_ref, w_ref, cos_ref, sin_ref, o_ref, xn_ref):
    x = x_ref[...]
    ms = jnp.mean(x * x, axis=-1, keepdims=True)
    xn_ref[...] = (x * lax.rsqrt(ms + NORM_EPS) * g_ref[...]).astype(BF16)
    shape = (x.shape[0], IN_TN)
    lane = lax.broadcasted_iota(jnp.int32, shape, 1)
    first_half = (lane & (HEAD_DIM - 1)) < (HEAD_DIM // 2)
    reps = IN_TN // LANES
    cos = jnp.tile(cos_ref[...], (1, reps))
    sin = jnp.tile(sin_ref[...], (1, reps))
    for j in range(IN_WIDTH // IN_TN):
        cols = slice(j * IN_TN, (j + 1) * IN_TN)
        acc = jnp.dot(xn_ref[...], w_ref[:, cols], preferred_element_type=F32)
        if j in ROPE_TILES:
            partner = jnp.where(first_half, pltpu.roll(acc, IN_TN - HEAD_DIM // 2, 1),
                                pltpu.roll(acc, HEAD_DIM // 2, 1))
            acc = acc * cos + partner * sin
        o_ref[:, cols] = acc.astype(o_ref.dtype)


def _norm_inproj(h, g, w, cos, sin):
    rows = h.shape[0]
    tm = _row_tile(rows)
    return pl.pallas_call(
        _norm_inproj_kernel,
        grid=(rows // tm,),
        in_specs=[
            pl.BlockSpec((tm, D_MODEL), lambda i: (i, 0)),
            pl.BlockSpec((1, D_MODEL), lambda i: (0, 0)),
            pl.BlockSpec((D_MODEL, IN_WIDTH), lambda i: (0, 0)),
            pl.BlockSpec((tm, LANES), lambda i: (i, 0)),
            pl.BlockSpec((tm, LANES), lambda i: (i, 0)),
        ],
        out_specs=pl.BlockSpec((tm, IN_WIDTH), lambda i: (i, 0)),
        out_shape=jax.ShapeDtypeStruct((rows, IN_WIDTH), BF16),
        scratch_shapes=[pltpu.VMEM((tm, D_MODEL), BF16)],
        compiler_params=_params(("parallel",)),
        name="norm_inproj",
    )(h, g, w, cos, sin)


def _swa_kernel(sink_ref, q_ref, kp_ref, kc_ref, vp_ref, vc_ref, o_ref):
    n = pl.program_id(1)
    shape = (BLK, 2 * BLK)
    qi = lax.broadcasted_iota(jnp.int32, shape, 0)
    c = lax.broadcasted_iota(jnp.int32, shape, 1)
    kj = (n - 1) * BLK + c
    delta = qi + BLK - c
    ok = (delta >= 0) & (delta < BLK) & (kj >= PAD)
    grp = SWA_HEADS // SWA_KV_HEADS
    outs = []
    for j in range(SWA_KV_HEADS):
        cols = slice(j * HEAD_DIM, (j + 1) * HEAD_DIM)
        kk = jnp.concatenate([kp_ref[:, cols], kc_ref[:, cols]], axis=0)
        vv = jnp.concatenate([vp_ref[:, cols], vc_ref[:, cols]], axis=0)
        for g in range(grp):
            hd = j * grp + g
            qh = q_ref[:, hd * HEAD_DIM:(hd + 1) * HEAD_DIM]
            s = lax.dot_general(qh, kk, (((1,), (1,)), ((), ())), preferred_element_type=F32) * (HEAD_DIM ** -0.5)
            s = jnp.where(ok, s, NEG)
            sink = sink_ref[hd]
            m = jnp.maximum(jnp.max(s, axis=-1, keepdims=True), sink)
            p = jnp.exp(s - m)
            denom = jnp.sum(p, axis=-1, keepdims=True) + jnp.exp(sink - m)
            outs.append(jnp.dot(p.astype(BF16), vv, preferred_element_type=F32) / denom)
    o_ref[...] = jnp.concatenate(outs, axis=1).astype(o_ref.dtype)


def _swa(proj, sinks, batch, nb):
    rows = proj.shape[0]
    kvw = SWA_KV_HEADS * HEAD_DIM

    def cur(col):
        return lambda b, n: (b * nb + n, col)

    def prev(col):
        return lambda b, n: (b * nb + jnp.maximum(n - 1, 0), col)

    return pl.pallas_call(
        _swa_kernel,
        grid=(batch, nb),
        in_specs=[
            pl.BlockSpec(memory_space=pltpu.SMEM),
            pl.BlockSpec((BLK, SWA_WIDTH), cur(0)),
            pl.BlockSpec((BLK, kvw), prev(COL_KA // kvw)),
            pl.BlockSpec((BLK, kvw), cur(COL_KA // kvw)),
            pl.BlockSpec((BLK, kvw), prev(COL_VA // kvw)),
            pl.BlockSpec((BLK, kvw), cur(COL_VA // kvw)),
        ],
        out_specs=pl.BlockSpec((BLK, SWA_WIDTH), cur(0)),
        out_shape=jax.ShapeDtypeStruct((rows, SWA_WIDTH), BF16),
        compiler_params=_params(("parallel", "parallel")),
        name="swa",
    )(sinks, proj, proj, proj, proj, proj)


def _diff_kernel(lam_ref, g_ref, q_ref, k_ref, v_ref, o_ref, *, lam_init, nb):
    row = lax.broadcasted_iota(jnp.int32, (BLK, BLK), 0)
    col = lax.broadcasted_iota(jnp.int32, (BLK, BLK), 1)
    causal_ok = col <= row
    pad_ok = col >= PAD
    lane = lax.broadcasted_iota(jnp.int32, (BLK, LANES), 1)
    lv = lam_ref[...]
    lam = (jnp.exp(jnp.sum(lv[0:1] * lv[1:2], axis=-1, keepdims=True))
           - jnp.exp(jnp.sum(lv[2:3] * lv[3:4], axis=-1, keepdims=True)) + lam_init)
    gain = g_ref[...] * (1.0 - lam_init)
    scale = HEAD_DIM ** -0.5
    for n in range(nb):
        ke = (n + 1) * BLK
        q = q_ref[n * BLK:(n + 1) * BLK, :] * scale
        kx = k_ref[0:ke, :]
        vx = v_ref[0:ke, :]
        heads = []
        for s_idx in range(2):
            qx = jnp.where((lane >= HEAD_DIM) == bool(s_idx), q, jnp.zeros_like(q))
            s = lax.dot_general(qx, kx, (((1,), (1,)), ((), ())), preferred_element_type=F32)
            if n == 0:
                s = jnp.where(causal_ok & pad_ok, s, NEG)
            else:
                first = jnp.where(pad_ok, s[:, :BLK], NEG)
                last = jnp.where(causal_ok, s[:, ke - BLK:], NEG)
                middle = [s[:, BLK:ke - BLK]] if n > 1 else []
                s = jnp.concatenate([first] + middle + [last], axis=1)
            m = jnp.max(s, axis=-1, keepdims=True)
            p = jnp.exp(s - m)
            denom = jnp.sum(p, axis=-1, keepdims=True)
            heads.append(jnp.dot(p.astype(BF16), vx, preferred_element_type=F32) / denom)
        o = heads[0] - lam * heads[1]
        o = o * lax.rsqrt(jnp.mean(o * o, axis=-1, keepdims=True) + NORM_EPS) * gain
        o_ref[n * BLK:(n + 1) * BLK, :] = o.astype(o_ref.dtype)


def _diff_attn(proj, lam_vecs, subln_g, lam_init, batch, nb):
    rows = proj.shape[0]
    lp = nb * BLK
    return pl.pallas_call(
        functools.partial(_diff_kernel, lam_init=lam_init, nb=nb),
        grid=(batch, DIFF_HEADS),
        in_specs=[
            pl.BlockSpec((4, HEAD_DIM), lambda b, h: (0, 0)),
            pl.BlockSpec((1, DIFF_V_DIM), lambda b, h: (0, 0)),
            pl.BlockSpec((lp, LANES), lambda b, h: (b, COL_QD // LANES + h)),
            pl.BlockSpec((lp, LANES), lambda b, h: (b, COL_KD // LANES + h)),
            pl.BlockSpec((lp, LANES), lambda b, h: (b, COL_VD // LANES + h)),
        ],
        out_specs=pl.BlockSpec((lp, DIFF_V_DIM), lambda b, h: (b, h)),
        out_shape=jax.ShapeDtypeStruct((rows, DIFF_WIDTH), BF16),
        compiler_params=_params(("parallel", "parallel")),
        name="diff_attn",
    )(lam_vecs, subln_g, proj, proj, proj)


def _ssm_row_chunk(rows):
    for c in (544, 512, 384, 256, 128, 64, 32, 16):
        if rows % c == 0:
            return c
    raise ValueError(rows)


def _ssm_expand_operators(kc_ref, bc_ref, cc_ref, k_ref, bw_ref, cw_ref):
    def iota(shape, dim):
        return lax.broadcasted_iota(jnp.int32, shape, dim)

    def group(idx, width):
        return jnp.right_shift(idx, width.bit_length() - 1) & (SSM_PACK - 1)

    hs, ps, tl = SSM_GROUP, SSM_STATE, SSM_T * LANES
    spread_h = ((iota((LANES, hs), 0) & (hs - 1)) == iota((LANES, hs), 1)).astype(BF16)
    kt = jnp.dot(spread_h, kc_ref[...], preferred_element_type=F32)
    same = group(iota((LANES, tl), 0), hs) == group(iota((LANES, tl), 1), hs)
    k_ref[...] = jnp.where(same, kt, 0.0).astype(BF16)
    r = iota((2 * ps, 2 * SSM_PSTATE), 0)
    c = iota((2 * ps, 2 * SSM_PSTATE), 1)
    spread_s = ((jnp.right_shift(r, 6) == jnp.right_shift(c, 9)) & ((r & (ps - 1)) == (c & (ps - 1)))).astype(BF16)
    bw = jnp.dot(bc_ref[...], spread_s, preferred_element_type=F32)
    shape = (tl, 2 * SSM_PSTATE)
    same = group(iota(shape, 0), hs) == group(iota(shape, 1), ps)
    bw_ref[...] = jnp.where(same, bw, 0.0).astype(BF16)
    shape = (LANES, 2 * SSM_PSTATE)
    same = group(iota(shape, 0), hs) == group(iota(shape, 1), ps)
    for j in range(SSM_T):
        ct = jnp.dot(spread_h, cc_ref[j], preferred_element_type=F32)
        cw_ref[j] = jnp.where(same, ct, 0.0).astype(BF16)


def _ssm_kernel(u_ref, kc_ref, bc_ref, cc_ref, a_ref, d_ref, o_ref, up_ref, y_ref, e_ref, s_ref, k_ref, bw_ref,
                cw_ref, *, batch, lp):
    rows = batch * lp
    n_chunks = lp // SSM_T
    tot_chunks = rows // SSM_T
    _ssm_expand_operators(kc_ref, bc_ref, cc_ref, k_ref, bw_ref, cw_ref)
    contract_lanes = (((1,), (1,)), ((), ()))
    up_ref[0:SSM_T, :] = jnp.zeros((SSM_T, LANES), F32)
    up_ref[SSM_T:, :] = u_ref[...].astype(F32)
    for b in range(batch):
        up_ref[SSM_T + b * lp:SSM_T + b * lp + PAD, :] = jnp.zeros((PAD, LANES), F32)

    ch = _ssm_row_chunk(rows)
    rmod = lax.broadcasted_iota(jnp.int32, (ch, LANES), 0) & (SSM_T - 1)
    for r0 in range(0, rows, ch):
        pieces = []
        for k in range(SSM_T):
            sl = up_ref[SSM_T - k + r0:SSM_T - k + r0 + ch, :]
            pieces.append(jnp.where(rmod >= k, sl, 0.0).astype(BF16))
        y_ref[r0:r0 + ch, :] = lax.dot_general(jnp.concatenate(pieces, axis=1), k_ref[...], contract_lanes,
                                               preferred_element_type=F32)

    pieces = [up_ref[pl.ds(SSM_T + i, tot_chunks, stride=SSM_T), :].astype(BF16) for i in range(SSM_T)]
    e_ref[...] = jnp.dot(jnp.concatenate(pieces, axis=1), bw_ref[...], preferred_element_type=F32)

    a_re = a_ref[:, :SSM_PSTATE]
    a_im = a_ref[:, SSM_PSTATE:]

    def step(c, carry):
        new = []
        for b in range(batch):
            sr, si = carry[b]
            idx = b * n_chunks + c
            s_ref[pl.ds(idx, 1), :SSM_PSTATE] = sr
            s_ref[pl.ds(idx, 1), SSM_PSTATE:] = si
            er = e_ref[pl.ds(idx, 1), :SSM_PSTATE]
            ei = e_ref[pl.ds(idx, 1), SSM_PSTATE:]
            new.append((a_re * sr - a_im * si + er, a_re * si + a_im * sr + ei))
        return tuple(new)

    zero = jnp.zeros((1, SSM_PSTATE), F32)
    lax.fori_loop(0, n_chunks, step, tuple((zero, zero) for _ in range(batch)))

    sb = s_ref[...].astype(BF16)
    for j in range(SSM_T):
        z = lax.dot_general(sb, cw_ref[j], contract_lanes, preferred_element_type=F32)
        idx = pl.ds(j, tot_chunks, stride=SSM_T)
        y_ref[idx, :] = y_ref[idx, :] + z

    y = y_ref[...] + d_ref[...] * up_ref[SSM_T:, :]
    o_ref[...] = jax.nn.gelu(y, approximate=True).astype(o_ref.dtype)


def _ssm(proj, kc, bc, cc, a_t, dskip, batch, lp):
    rows = proj.shape[0]
    tot_chunks = rows // SSM_T
    return pl.pallas_call(
        functools.partial(_ssm_kernel, batch=batch, lp=lp),
        grid=(SSM_PACKS,),
        in_specs=[
            pl.BlockSpec((rows, LANES), lambda p: (0, COL_US // LANES + p)),
            pl.BlockSpec((None, SSM_GROUP, SSM_T * LANES), lambda p: (p, 0, 0)),
            pl.BlockSpec((None, SSM_T * LANES, 2 * SSM_STATE), lambda p: (p, 0, 0)),
            pl.BlockSpec((None, SSM_T, SSM_GROUP, 2 * SSM_PSTATE), lambda p: (p, 0, 0, 0)),
            pl.BlockSpec((None, 1, 2 * SSM_PSTATE), lambda p: (p, 0, 0)),
            pl.BlockSpec((None, 1, LANES), lambda p: (p, 0, 0)),
        ],
        out_specs=pl.BlockSpec((rows, LANES), lambda p: (0, p)),
        out_shape=jax.ShapeDtypeStruct((rows, SSM_WIDTH), F32),
        scratch_shapes=[
            pltpu.VMEM((rows + SSM_T, LANES), F32),
            pltpu.VMEM((rows, LANES), F32),
            pltpu.VMEM((tot_chunks, 2 * SSM_PSTATE), F32),
            pltpu.VMEM((tot_chunks, 2 * SSM_PSTATE), F32),
            pltpu.VMEM((LANES, SSM_T * LANES), BF16),
            pltpu.VMEM((SSM_T * LANES, 2 * SSM_PSTATE), BF16),
            pltpu.VMEM((SSM_T, LANES, 2 * SSM_PSTATE), BF16),
        ],
        compiler_params=_params(("parallel",)),
        name="ssm",
    )(proj, kc, bc, cc, a_t, dskip)


def _ssm_operators(lam_re, lam_im, b_re, b_im, c_re, c_im, log_dt):
    p, h = SSM_STATE, SSM_GROUP
    dt = jnp.exp(log_dt.astype(F32))[:, None]
    lr, li = lam_re.astype(F32), lam_im.astype(F32)
    mag = jnp.exp(lr * dt)
    abar_re, abar_im = mag * jnp.cos(li * dt), mag * jnp.sin(li * dt)
    den = lr * lr + li * li
    nr = abar_re - 1.0
    f_re = (nr * lr + abar_im * li) / den
    f_im = (abar_im * lr - nr * li) / den
    br, bi = b_re.astype(F32), b_im.astype(F32)
    bbar_re = f_re[..., None] * br - f_im[..., None] * bi
    bbar_im = f_re[..., None] * bi + f_im[..., None] * br
    ks = jnp.arange(SSM_T + 1, dtype=F32)[:, None, None]
    pmag = jnp.exp(ks * (lr * dt)[None])
    pw_re = pmag * jnp.cos(ks * (li * dt)[None])
    pw_im = pmag * jnp.sin(ks * (li * dt)[None])
    cr, ci = c_re.astype(F32), c_im.astype(F32)

    packs, q = SSM_PACKS, SSM_PACK

    cb_re = cr[:, :, :, None] * bbar_re[:, None, :, :] - ci[:, :, :, None] * bbar_im[:, None, :, :]
    cb_im = cr[:, :, :, None] * bbar_im[:, None, :, :] + ci[:, :, :, None] * bbar_re[:, None, :, :]
    klag = (jnp.einsum('kgs,gosi->kgio', pw_re[:SSM_T], cb_re)
            - jnp.einsum('kgs,gosi->kgio', pw_im[:SSM_T], cb_im))
    kc = klag.reshape(SSM_T, packs, q, h, h).transpose(1, 4, 0, 2, 3).reshape(packs, h, SSM_T * LANES)

    rev_re, rev_im = pw_re[SSM_T - 1::-1], pw_im[SSM_T - 1::-1]
    inj_re = rev_re[..., None] * bbar_re[None] - rev_im[..., None] * bbar_im[None]
    inj_im = rev_re[..., None] * bbar_im[None] + rev_im[..., None] * bbar_re[None]
    inj = jnp.stack([inj_re, inj_im]).reshape(2, SSM_T, packs, q, p, h)
    bc = inj.transpose(2, 1, 3, 5, 0, 4).reshape(packs, SSM_T * LANES, 2 * p)

    ca_re = cr[None] * pw_re[1:, :, None, :] - ci[None] * pw_im[1:, :, None, :]
    ca_im = cr[None] * pw_im[1:, :, None, :] + ci[None] * pw_re[1:, :, None, :]
    ca = jnp.stack([ca_re, -ca_im]).reshape(2, SSM_T, packs, q, h, p)
    cc = ca.transpose(2, 1, 4, 0, 3, 5).reshape(packs, SSM_T, h, 2 * SSM_PSTATE)

    a_t = jnp.concatenate([pw_re[SSM_T].reshape(packs, 1, SSM_PSTATE),
                           pw_im[SSM_T].reshape(packs, 1, SSM_PSTATE)], axis=-1)
    return kc.astype(BF16), bc.astype(BF16), cc.astype(BF16), a_t


OUT_TN = 512


def _split_bf16(x):
    hi = x.astype(BF16)
    return hi, (x - hi.astype(F32)).astype(BF16)


def _route(hn, rwh_ref, rwl_ref, rb_ref):
    hi, lo = _split_bf16(hn)
    logits = (jnp.dot(hi, rwh_ref[...], preferred_element_type=F32)
              + (jnp.dot(lo, rwh_ref[...], preferred_element_type=F32)
                 + jnp.dot(hi, rwl_ref[...], preferred_element_type=F32))) + rb_ref[...]
    lane = lax.broadcasted_iota(jnp.int32, logits.shape, 1)

    def first_argmax(v, vmax):
        return jnp.min(jnp.where(v == vmax, lane, LANES), axis=-1, keepdims=True)

    gl = jnp.where(lane < N_EXPERT_GROUPS, logits, NEG)
    gmax = jnp.max(gl, axis=-1, keepdims=True)
    g_sel = first_argmax(gl, gmax)
    p_g = 1.0 / jnp.sum(jnp.exp(gl - gmax), axis=-1, keepdims=True)
    e_lane = lane - N_EXPERT_GROUPS
    in_group = (e_lane >= 0) & (e_lane < N_EXPERTS) & (jnp.right_shift(e_lane, 3) == g_sel)
    el = jnp.where(in_group, logits, NEG)
    m1 = jnp.max(el, axis=-1, keepdims=True)
    i1 = first_argmax(el, m1)
    el2 = jnp.where(lane == i1, NEG, el)
    m2 = jnp.max(el2, axis=-1, keepdims=True)
    i2 = first_argmax(el2, m2)
    z = jnp.sum(jnp.exp(el - m1), axis=-1, keepdims=True)
    p1 = 1.0 / z
    p2 = jnp.exp(m2 - m1) / z
    w1 = p1 / (p1 + p2) * p_g
    w2 = p2 / (p1 + p2) * p_g
    wts = jnp.where(lane == 0, w1, jnp.where(lane == 1, w2, 0.0))
    ids = jnp.where(lane == 0, i1 - N_EXPERT_GROUPS, jnp.where(lane == 1, i2 - N_EXPERT_GROUPS, 0))
    return wts, ids


def _assignment_ranks(ids, run_ref):
    tm = ids.shape[0]
    lane = lax.broadcasted_iota(jnp.int32, ids.shape, 1)
    oh0 = lane == ids[:, 0:1]
    oh1 = lane == ids[:, 1:2]
    hits = (oh0 | oh1).astype(BF16)
    r = lax.broadcasted_iota(jnp.int32, (tm, tm), 0)
    c = lax.broadcasted_iota(jnp.int32, (tm, tm), 1)
    before = jnp.dot((c < r).astype(BF16), hits, preferred_element_type=F32) + run_ref[...]
    rank0 = jnp.sum(jnp.where(oh0, before, 0.0), axis=-1, keepdims=True)
    rank1 = jnp.sum(jnp.where(oh1, before, 0.0), axis=-1, keepdims=True)
    run_ref[...] += jnp.sum(hits.astype(F32), axis=0, keepdims=True)
    return jnp.where(lane == 0, rank0, jnp.where(lane == 1, rank1, 0.0)).astype(jnp.int32)


def _mixer_out_kernel(h_ref, a_ref, y_ref, c_ref, wo_ref, wglu_ref, g_ref, rwh_ref, rwl_ref, rb_ref,
                      o_ref, hs_ref, wt_ref, id_ref, rank_ref, cnt_ref, run_ref):
    @pl.when(pl.program_id(0) == 0)
    def _():
        run_ref[...] = jnp.zeros(run_ref.shape, F32)

    y = y_ref[...]
    gate = jnp.dot(y.astype(BF16), wglu_ref[...], preferred_element_type=F32)
    b = (y * jax.nn.sigmoid(gate)).astype(BF16)
    mixed = jnp.concatenate([a_ref[...], b, c_ref[...]], axis=1)
    for j in range(D_MODEL // OUT_TN):
        cols = slice(j * OUT_TN, (j + 1) * OUT_TN)
        o_ref[:, cols] = h_ref[:, cols] + jnp.dot(mixed, wo_ref[:, cols], preferred_element_type=F32)
    x = o_ref[...]
    hn = _rms_norm(x, g_ref[...])
    _store_slabs(hs_ref, hn)
    wts, ids = _route(hn, rwh_ref, rwl_ref, rb_ref)
    wt_ref[...] = wts
    id_ref[...] = ids
    rank_ref[...] = _assignment_ranks(ids, run_ref)
    cnt_ref[...] = run_ref[...].astype(jnp.int32)


def _mixer_out(h, oa, yg, oc, wo, wglu, g, rw_hi, rw_lo, rb):
    rows = h.shape[0]
    tm = _row_tile(rows)

    def row_spec(width):
        return pl.BlockSpec((tm, width), lambda i: (i, 0))

    def whole(shape):
        return pl.BlockSpec(shape, lambda i: (0, 0))

    return pl.pallas_call(
        _mixer_out_kernel,
        grid=(rows // tm,),
        in_specs=[
            row_spec(D_MODEL), row_spec(SWA_WIDTH), row_spec(SSM_WIDTH), row_spec(DIFF_WIDTH),
            whole((D_MODEL, D_MODEL)), whole((SSM_WIDTH, SSM_WIDTH)), whole((1, D_MODEL)),
            whole((D_MODEL, LANES)), whole((D_MODEL, LANES)), whole((1, LANES)),
        ],
        out_specs=[row_spec(D_MODEL), pl.BlockSpec((tm * SLAB, LANES), lambda i: (i, 0)), row_spec(LANES),
                   row_spec(LANES), row_spec(LANES), whole((1, LANES))],
        out_shape=[
            jax.ShapeDtypeStruct((rows, D_MODEL), F32),
            jax.ShapeDtypeStruct((rows * SLAB, LANES), F32),
            jax.ShapeDtypeStruct((rows, LANES), F32),
            jax.ShapeDtypeStruct((rows, LANES), jnp.int32),
            jax.ShapeDtypeStruct((rows, LANES), jnp.int32),
            jax.ShapeDtypeStruct((1, LANES), jnp.int32),
        ],
        scratch_shapes=[pltpu.VMEM((1, LANES), F32)],
        compiler_params=_params(("arbitrary",)),
        name="mixer_out",
    )(h, oa, yg, oc, wo, wglu, g, rw_hi, rw_lo, rb)


def _dispatch_plan(ids, ranks, cnt, n_blk):
    e = ids[:, :2].reshape(-1)
    n_asg = e.shape[0]
    counts = cnt[0, :N_EXPERTS]
    padded = (counts + MOE_BM - 1) // MOE_BM * MOE_BM
    pend = jnp.cumsum(padded)
    pstart = pend - padded
    onehot = e[:, None] == jnp.arange(N_EXPERTS, dtype=jnp.int32)[None, :]
    dest = jnp.sum(jnp.where(onehot, pstart[None, :], 0), axis=1).astype(jnp.int32) + ranks[:, :2].reshape(-1)
    src = jnp.zeros((n_blk * MOE_BM,), jnp.int32).at[dest].set(jnp.arange(n_asg, dtype=jnp.int32) // 2)
    blk_start = jnp.arange(n_blk, dtype=jnp.int32) * MOE_BM
    blk_expert = jnp.minimum(jnp.sum((pend[None, :] <= blk_start[:, None]).astype(jnp.int32), axis=1), N_EXPERTS - 1)
    n_used = (pend[-1] // MOE_BM).astype(jnp.int32).reshape(1)
    ex = jnp.arange(N_EXPERTS, dtype=jnp.int32)
    nonempty = counts > 0
    run_parity = (jnp.cumsum(nonempty.astype(jnp.int32)) - 1) & 1
    later = (ex[None, :] > ex[:, None]) & nonempty[None, :]
    nxt = jnp.min(jnp.where(later, ex[None, :], N_EXPERTS), axis=1)
    nxt = jnp.where(nxt == N_EXPERTS, -1, nxt)
    plan = dict(src=src, blk_expert=blk_expert, n_used=n_used, blk_parity=run_parity[blk_expert].astype(jnp.int32),
                blk_next=nxt[blk_expert].astype(jnp.int32))
    return dest, plan


SLAB = D_MODEL // LANES


def _store_slabs(ref, x):
    rows = x.shape[0]
    for s in range(SLAB):
        ref[pl.ds(s, rows, stride=SLAB), :] = x[:, s * LANES:(s + 1) * LANES]


def _load_slabs(ref, rows):
    return jnp.concatenate([ref[pl.ds(s, rows, stride=SLAB), :] for s in range(SLAB)], axis=1)


def _slab_gather_copy(src_hbm, dst_buf, sem, src_row, dst_row):
    src = src_hbm.at[pl.ds(pl.multiple_of(src_row * SLAB, SLAB), SLAB), :]
    return pltpu.make_async_copy(src, dst_buf.at[pl.ds(dst_row * SLAB, SLAB), :], sem)


def _block_wait_copy(src_hbm, dst_buf, sem):
    return pltpu.make_async_copy(src_hbm.at[pl.ds(0, dst_buf.shape[0]), :], dst_buf, sem)


GATHER_SLOTS = 3
WEIGHT_DMA_PRIORITY = 1


def _expert_kernel(be_ref, par_ref, nxt_ref, src_ref, nused_ref, hn_hbm, wg_hbm, wu_hbm, wd_hbm, o_ref,
                   xbuf, wgf, wuf, wdf, wgb, wub, wdb, gsem, wsem, *, layer):
    i = pl.program_id(0)
    n_used = nused_ref[0]
    slot = i % GATHER_SLOTS

    def start_gather(blk, slot_):
        for r in range(MOE_BM):
            _slab_gather_copy(hn_hbm, xbuf.at[slot_], gsem.at[slot_], src_ref[blk * MOE_BM + r], r).start()

    def weight_copies(expert, wslot):
        pairs = ((wg_hbm, wgf), (wu_hbm, wuf), (wd_hbm, wdf))
        return [pltpu.make_async_copy(w.at[layer, expert], buf.at[wslot], wsem.at[wslot]) for w, buf in pairs]

    @pl.when(i == 0)
    def _():
        for c in weight_copies(be_ref[0], 0):
            c.start(priority=WEIGHT_DMA_PRIORITY)
        start_gather(0, 0)
        start_gather(jnp.minimum(1, n_used), 1)

    @pl.when(i <= n_used + 1)
    def _():
        _block_wait_copy(hn_hbm, xbuf.at[slot], gsem.at[slot]).wait()

    @pl.when(i < n_used)
    def _():
        @pl.when((i == 0) | (be_ref[i] != be_ref[jnp.maximum(i - 1, 0)]))
        def _():
            wslot = par_ref[i]
            for c in weight_copies(be_ref[i], wslot):
                c.wait()

            @pl.when(nxt_ref[i] >= 0)
            def _():
                for c in weight_copies(nxt_ref[i], 1 - wslot):
                    c.start(priority=WEIGHT_DMA_PRIORITY)

            wgb[...] = wgf[wslot].astype(BF16)
            wub[...] = wuf[wslot].astype(BF16)
            wdb[...] = wdf[wslot].astype(BF16)

        x = _load_slabs(xbuf.at[slot], MOE_BM).astype(BF16)
        start_gather(jnp.minimum(i + 2, n_used), (i + 2) % GATHER_SLOTS)
        gate = jnp.dot(x, wgb[...], preferred_element_type=F32)
        up = jnp.dot(x, wub[...], preferred_element_type=F32)
        act = (jax.nn.silu(gate) * up).astype(BF16)
        _store_slabs(o_ref, jnp.dot(act, wdb[...], preferred_element_type=F32))

    @pl.when(i >= n_used)
    def _():
        o_ref[...] = jnp.zeros(o_ref.shape, o_ref.dtype)


def _experts(hn_slabs, plan, wg, wu, wd, layer, n_blk):
    hbm = pl.BlockSpec(memory_space=pl.ANY)
    blk_rows = MOE_BM * SLAB
    grid_spec = pltpu.PrefetchScalarGridSpec(
        num_scalar_prefetch=5,
        grid=(n_blk,),
        in_specs=[hbm, hbm, hbm, hbm],
        out_specs=pl.BlockSpec((blk_rows, LANES), lambda i, *_: (i, 0)),
        scratch_shapes=[
            pltpu.VMEM((GATHER_SLOTS, blk_rows, LANES), F32),
            pltpu.VMEM((2, D_MODEL, D_EXPERT), F32),
            pltpu.VMEM((2, D_MODEL, D_EXPERT), F32),
            pltpu.VMEM((2, D_EXPERT, D_MODEL), F32),
            pltpu.VMEM((D_MODEL, D_EXPERT), BF16),
            pltpu.VMEM((D_MODEL, D_EXPERT), BF16),
            pltpu.VMEM((D_EXPERT, D_MODEL), BF16),
            pltpu.SemaphoreType.DMA((GATHER_SLOTS,)),
            pltpu.SemaphoreType.DMA((2,)),
        ],
    )
    return pl.pallas_call(
        functools.partial(_expert_kernel, layer=layer),
        grid_spec=grid_spec,
        out_shape=jax.ShapeDtypeStruct((n_blk * blk_rows, LANES), F32),
        compiler_params=_params(("arbitrary",)),
        name="moe_experts",
    )(plan["blk_expert"], plan["blk_parity"], plan["blk_next"], plan["src"], plan["n_used"], hn_slabs, wg, wu, wd)


def _combine_kernel(dest_ref, h_ref, wt_ref, y_hbm, o_ref, ybuf, sem):
    i = pl.program_id(0)
    n = pl.num_programs(0)
    slot = i % 2

    def start_gather(tile, slot_):
        for r in range(MOE_BM):
            for k in range(2):
                row = dest_ref[2 * (tile * MOE_BM + r) + k]
                _slab_gather_copy(y_hbm, ybuf.at[slot_, k], sem.at[slot_], row, r).start(priority=k)

    @pl.when(i == 0)
    def _():
        start_gather(0, 0)

    @pl.when(i + 1 < n)
    def _():
        start_gather(i + 1, 1 - slot)

    for k in range(2):
        _block_wait_copy(y_hbm, ybuf.at[slot, k], sem.at[slot]).wait()
    wt = wt_ref[...]
    y0 = _load_slabs(ybuf.at[slot, 0], MOE_BM)
    y1 = _load_slabs(ybuf.at[slot, 1], MOE_BM)
    o_ref[...] = h_ref[...] + wt[:, 0:1] * y0 + wt[:, 1:2] * y1


def _combine(h, wts, y_slabs, dest):
    rows = h.shape[0]
    grid_spec = pltpu.PrefetchScalarGridSpec(
        num_scalar_prefetch=1,
        grid=(rows // MOE_BM,),
        in_specs=[
            pl.BlockSpec((MOE_BM, D_MODEL), lambda i, d: (i, 0)),
            pl.BlockSpec((MOE_BM, LANES), lambda i, d: (i, 0)),
            pl.BlockSpec(memory_space=pl.ANY),
        ],
        out_specs=pl.BlockSpec((MOE_BM, D_MODEL), lambda i, d: (i, 0)),
        scratch_shapes=[pltpu.VMEM((2, 2, MOE_BM * SLAB, LANES), F32), pltpu.SemaphoreType.DMA((2,))],
    )
    return pl.pallas_call(
        _combine_kernel,
        grid_spec=grid_spec,
        out_shape=jax.ShapeDtypeStruct((rows, D_MODEL), F32),
        compiler_params=_params(("arbitrary",)),
        name="moe_combine",
    )(dest, h, wts, y_slabs)


def _moe(h, hn_slabs, wts, ids, ranks, cnt, wg, wu, wd, layer):
    rows = h.shape[0]
    n_blk = -(-(2 * rows) // MOE_BM) + N_EXPERTS + 1
    dest, plan = _dispatch_plan(ids, ranks, cnt, n_blk)
    y_slabs = _experts(hn_slabs, plan, wg, wu, wd, layer, n_blk)
    return _combine(h, wts, y_slabs, dest)


def _final_norm_kernel(x_ref, g_ref, o_ref):
    x = x_ref[...]
    ms = jnp.mean(x * x, axis=-1, keepdims=True)
    o_ref[...] = x * lax.rsqrt(ms + NORM_EPS) * g_ref[...]


def _final_norm(h, g, batch, nb):
    seq_blocks = nb - 1
    return pl.pallas_call(
        _final_norm_kernel,
        grid=(batch, seq_blocks),
        in_specs=[pl.BlockSpec((BLK, D_MODEL), lambda b, j: (b * nb + 1 + j, 0)),
                  pl.BlockSpec((1, D_MODEL), lambda b, j: (0, 0))],
        out_specs=pl.BlockSpec((BLK, D_MODEL), lambda b, j: (b * seq_blocks + j, 0)),
        out_shape=jax.ShapeDtypeStruct((batch * seq_blocks * BLK, D_MODEL), F32),
        compiler_params=_params(("parallel", "parallel")),
        name="final_norm",
    )(h, g)


def _rope_tables(batch, lp):
    half = HEAD_DIM // 2
    inv = ROPE_THETA ** (-jnp.arange(half, dtype=F32) / half)
    pos = (jnp.arange(lp, dtype=jnp.int32) - PAD).astype(F32)
    ang = pos[:, None] * inv[None, :]
    cos = jnp.tile(jnp.cos(ang), (batch, LANES // half))
    sin = jnp.sin(ang)
    sin = jnp.tile(jnp.concatenate([-sin, sin], axis=1), (batch, LANES // HEAD_DIM))
    return cos, sin


def kernel(x, meta_tokens, attn_norm_g, ffn_norm_g, w_in, swa_sinks, ssm_lambda_re, ssm_lambda_im, ssm_b_re, ssm_b_im, ssm_c_re, ssm_c_im, ssm_d, ssm_log_dt, ssm_w_glu, diff_lambda_q1, diff_lambda_k1, diff_lambda_q2, diff_lambda_k2, diff_subln_g, w_out, router_group_w, router_group_b, router_expert_w, router_expert_b, moe_w_gate, moe_w_up, moe_w_down, final_norm_g):
    batch, seq, d = x.shape
    depth = w_in.shape[0]
    lp = PAD + N_META + seq
    nb = lp // BLK
    meta = jnp.broadcast_to(meta_tokens[None].astype(x.dtype), (batch, N_META, d))
    h = jnp.concatenate([jnp.zeros((batch, PAD, d), x.dtype), meta, x], axis=1).reshape(batch * lp, d)
    cos, sin = _rope_tables(batch, lp)
    for l in range(depth):
        proj = _norm_inproj(h, attn_norm_g[l][None], w_in[l].astype(BF16), cos, sin)
        o_a = _swa(proj, swa_sinks[l].astype(F32), batch, nb)
        kflat, bw, cw, a_t = _ssm_operators(ssm_lambda_re[l], ssm_lambda_im[l], ssm_b_re[l], ssm_b_im[l],
                                            ssm_c_re[l], ssm_c_im[l], ssm_log_dt[l])
        yg = _ssm(proj, kflat, bw, cw, a_t, ssm_d[l].astype(F32).reshape(SSM_PACKS, 1, LANES), batch, lp)
        lam_init = 0.8 - 0.6 * math.exp(-0.3 * l)
        lam_vecs = jnp.stack([diff_lambda_q1[l], diff_lambda_k1[l], diff_lambda_q2[l], diff_lambda_k2[l]]).astype(F32)
        o_c = _diff_attn(proj, lam_vecs, diff_subln_g[l][None].astype(F32), lam_init, batch, nb)
        wo = w_out[l].astype(BF16)
        rw = jnp.concatenate([router_group_w[l], router_expert_w[l]], axis=1).astype(F32)
        rw = jnp.pad(rw, ((0, 0), (0, LANES - rw.shape[1])))
        rb = jnp.concatenate([router_group_b[l], router_expert_b[l]]).astype(F32)
        rb = jnp.pad(rb, (0, LANES - rb.shape[0]))[None]
        rw_hi, rw_lo = _split_bf16(rw)
        g_ffn = ffn_norm_g[l][None].astype(F32)
        h, hn_slabs, wts, ids, ranks, cnt = _mixer_out(h, o_a, yg, o_c, wo, ssm_w_glu[l].astype(BF16), g_ffn, rw_hi,
                                                       rw_lo, rb)
        h = _moe(h, hn_slabs, wts, ids, ranks, cnt, moe_w_gate, moe_w_up, moe_w_down, l)
    out = _final_norm(h, final_norm_g[None], batch, nb)
    return out.reshape(batch, seq, d)
```

```python
import functools
import math

import jax
import jax.numpy as jnp
from jax import lax
from jax.experimental import pallas as pl
from jax.experimental.pallas import tpu as pltpu

D_MODEL = 2048
N_META = 16
BLK = 128
PAD = BLK - N_META
ROPE_THETA = 10000.0
NORM_EPS = 1e-5
NEG = -1e30
SWA_HEADS = 12
SWA_KV_HEADS = 4
HEAD_DIM = 64
SWA_WIDTH = SWA_HEADS * HEAD_DIM
SSM_WIDTH = 768
SSM_GROUP = 16
SSM_GROUPS = SSM_WIDTH // SSM_GROUP
SSM_STATE = 64
DIFF_HEADS = 4
DIFF_V_DIM = 128
DIFF_WIDTH = DIFF_HEADS * DIFF_V_DIM
IN_WIDTH = 3584
N_EXPERT_GROUPS = 4
EXPERTS_PER_GROUP = 8
N_EXPERTS = N_EXPERT_GROUPS * EXPERTS_PER_GROUP
D_EXPERT = 512

COL_KA = 768
COL_VA = 1024
COL_US = 1280
COL_QD = 2048
COL_KD = 2560
COL_VD = 3072

LANES = 128
SSM_T = 16
SSM_PACK = LANES // SSM_GROUP
SSM_PACKS = SSM_GROUPS // SSM_PACK
SSM_PSTATE = SSM_PACK * SSM_STATE
MOE_BM = 128
VMEM_LIMIT = 56 * 1024 * 1024

F32 = jnp.float32
BF16 = jnp.bfloat16


def _row_tile(rows):
    for t in (512, 256, 128):
        if rows % t == 0:
            return t
    raise ValueError(f"row count {rows} is not a multiple of 128")


def _params(sem, vmem=VMEM_LIMIT):
    return pltpu.CompilerParams(dimension_semantics=sem, vmem_limit_bytes=vmem)


def _rms_norm(x, g):
    return x * lax.rsqrt(jnp.mean(x * x, axis=-1, keepdims=True) + NORM_EPS) * g


IN_TN = 512
ROPE_TILES = (0, 1, 4, 5)


def _norm_inproj_kernel(x_ref, g_ref, w_ref, cos_ref, sin_ref, o_ref, xn_ref):
    x = x_ref[...]
    ms = jnp.mean(x * x, axis=-1, keepdims=True)
    xn_ref[...] = (x * lax.rsqrt(ms + NORM_EPS) * g_ref[...]).astype(BF16)
    shape = (x.shape[0], IN_TN)
    lane = lax.broadcasted_iota(jnp.int32, shape, 1)
    first_half = (lane & (HEAD_DIM - 1)) < (HEAD_DIM // 2)
    reps = IN_TN // LANES
    cos = jnp.tile(cos_ref[...], (1, reps))
    sin = jnp.tile(sin_ref[...], (1, reps))
    for j in range(IN_WIDTH // IN_TN):
        cols = slice(j * IN_TN, (j + 1) * IN_TN)
        acc = jnp.dot(xn_ref[...], w_ref[:, cols], preferred_element_type=F32)
        if j in ROPE_TILES:
            partner = jnp.where(first_half, pltpu.roll(acc, IN_TN - HEAD_DIM // 2, 1),
                                pltpu.roll(acc, HEAD_DIM // 2, 1))
            acc = acc * cos + partner * sin
        o_ref[:, cols] = acc.astype(o_ref.dtype)


def _norm_inproj(h, g, w, cos, sin):
    rows = h.shape[0]
    tm = _row_tile(rows)
    return pl.pallas_call(
        _norm_inproj_kernel,
        grid=(rows // tm,),
        in_specs=[
            pl.BlockSpec((tm, D_MODEL), lambda i: (i, 0)),
            pl.BlockSpec((1, D_MODEL), lambda i: (0, 0)),
            pl.BlockSpec((D_MODEL, IN_WIDTH), lambda i: (0, 0)),
            pl.BlockSpec((tm, LANES), lambda i: (i, 0)),
            pl.BlockSpec((tm, LANES), lambda i: (i, 0)),
        ],
        out_specs=pl.BlockSpec((tm, IN_WIDTH), lambda i: (i, 0)),
        out_shape=jax.ShapeDtypeStruct((rows, IN_WIDTH), BF16),
        scratch_shapes=[pltpu.VMEM((tm, D_MODEL), BF16)],
        compiler_params=_params(("parallel",)),
        name="norm_inproj",
    )(h, g, w, cos, sin)


def _swa_kernel(sink_ref, q_ref, kp_ref, kc_ref, vp_ref, vc_ref, o_ref):
    n = pl.program_id(1)
    shape = (BLK, 2 * BLK)
    qi = lax.broadcasted_iota(jnp.int32, shape, 0)
    c = lax.broadcasted_iota(jnp.int32, shape, 1)
    kj = (n - 1) * BLK + c
    delta = qi + BLK - c
    ok = (delta >= 0) & (delta < BLK) & (kj >= PAD)
    grp = SWA_HEADS // SWA_KV_HEADS
    outs = []
    for j in range(SWA_KV_HEADS):
        cols = slice(j * HEAD_DIM, (j + 1) * HEAD_DIM)
        kk = jnp.concatenate([kp_ref[:, cols], kc_ref[:, cols]], axis=0)
        vv = jnp.concatenate([vp_ref[:, cols], vc_ref[:, cols]], axis=0)
        for g in range(grp):
            hd = j * grp + g
            qh = q_ref[:, hd * HEAD_DIM:(hd + 1) * HEAD_DIM]
            s = lax.dot_general(qh, kk, (((1,), (1,)), ((), ())), preferred_element_type=F32) * (HEAD_DIM ** -0.5)
            s = jnp.where(ok, s, NEG)
            sink = sink_ref[hd]
            m = jnp.maximum(jnp.max(s, axis=-1, keepdims=True), sink)
            p = jnp.exp(s - m)
            denom = jnp.sum(p, axis=-1, keepdims=True) + jnp.exp(sink - m)
            outs.append(jnp.dot(p.astype(BF16), vv, preferred_element_type=F32) / denom)
    o_ref[...] = jnp.concatenate(outs, axis=1).astype(o_ref.dtype)


def _swa(proj, sinks, batch, nb):
    rows = proj.shape[0]
    kvw = SWA_KV_HEADS * HEAD_DIM

    def cur(col):
        return lambda b, n: (b * nb + n, col)

    def prev(col):
        return lambda b, n: (b * nb + jnp.maximum(n - 1, 0), col)

    return pl.pallas_call(
        _swa_kernel,
        grid=(batch, nb),
        in_specs=[
            pl.BlockSpec(memory_space=pltpu.SMEM),
            pl.BlockSpec((BLK, SWA_WIDTH), cur(0)),
            pl.BlockSpec((BLK, kvw), prev(COL_KA // kvw)),
            pl.BlockSpec((BLK, kvw), cur(COL_KA // kvw)),
            pl.BlockSpec((BLK, kvw), prev(COL_VA // kvw)),
            pl.BlockSpec((BLK, kvw), cur(COL_VA // kvw)),
        ],
        out_specs=pl.BlockSpec((BLK, SWA_WIDTH), cur(0)),
        out_shape=jax.ShapeDtypeStruct((rows, SWA_WIDTH), BF16),
        compiler_params=_params(("parallel", "parallel")),
        name="swa",
    )(sinks, proj, proj, proj, proj, proj)


def _diff_kernel(lam_ref, g_ref, q_ref, k_ref, v_ref, o_ref, *, lam_init, nb):
    row = lax.broadcasted_iota(jnp.int32, (BLK, BLK), 0)
    col = lax.broadcasted_iota(jnp.int32, (BLK, BLK), 1)
    causal_ok = col <= row
    pad_ok = col >= PAD
    lane = lax.broadcasted_iota(jnp.int32, (BLK, LANES), 1)
    lv = lam_ref[...]
    lam = (jnp.exp(jnp.sum(lv[0:1] * lv[1:2], axis=-1, keepdims=True))
           - jnp.exp(jnp.sum(lv[2:3] * lv[3:4], axis=-1, keepdims=True)) + lam_init)
    gain = g_ref[...] * (1.0 - lam_init)
    scale = HEAD_DIM ** -0.5
    for n in range(nb):
        ke = (n + 1) * BLK
        q = q_ref[n * BLK:(n + 1) * BLK, :] * scale
        kx = k_ref[0:ke, :]
        vx = v_ref[0:ke, :]
        heads = []
        for s_idx in range(2):
            qx = jnp.where((lane >= HEAD_DIM) == bool(s_idx), q, jnp.zeros_like(q))
            s = lax.dot_general(qx, kx, (((1,), (1,)), ((), ())), preferred_element_type=F32)
            if n == 0:
                s = jnp.where(causal_ok & pad_ok, s, NEG)
            else:
                first = jnp.where(pad_ok, s[:, :BLK], NEG)
                last = jnp.where(causal_ok, s[:, ke - BLK:], NEG)
                middle = [s[:, BLK:ke - BLK]] if n > 1 else []
                s = jnp.concatenate([first] + middle + [last], axis=1)
            m = jnp.max(s, axis=-1, keepdims=True)
            p = jnp.exp(s - m)
            denom = jnp.sum(p, axis=-1, keepdims=True)
            heads.append(jnp.dot(p.astype(BF16), vx, preferred_element_type=F32) / denom)
        o = heads[0] - lam * heads[1]
        o = o * lax.rsqrt(jnp.mean(o * o, axis=-1, keepdims=True) + NORM_EPS) * gain
        o_ref[n * BLK:(n + 1) * BLK, :] = o.astype(o_ref.dtype)


def _diff_attn(proj, lam_vecs, subln_g, lam_init, batch, nb):
    rows = proj.shape[0]
    lp = nb * BLK
    return pl.pallas_call(
        functools.partial(_diff_kernel, lam_init=lam_init, nb=nb),
        grid=(batch, DIFF_HEADS),
        in_specs=[
            pl.BlockSpec((4, HEAD_DIM), lambda b, h: (0, 0)),
            pl.BlockSpec((1, DIFF_V_DIM), lambda b, h: (0, 0)),
            pl.BlockSpec((lp, LANES), lambda b, h: (b, COL_QD // LANES + h)),
            pl.BlockSpec((lp, LANES), lambda b, h: (b, COL_KD // LANES + h)),
            pl.BlockSpec((lp, LANES), lambda b, h: (b, COL_VD // LANES + h)),
        ],
        out_specs=pl.BlockSpec((lp, DIFF_V_DIM), lambda b, h: (b, h)),
        out_shape=jax.ShapeDtypeStruct((rows, DIFF_WIDTH), BF16),
        compiler_params=_params(("parallel", "parallel")),
        name="diff_attn",
    )(lam_vecs, subln_g, proj, proj, proj)


def _ssm_row_chunk(rows):
    for c in (544, 512, 384, 256, 128, 64, 32, 16):
        if rows % c == 0:
            return c
    raise ValueError(rows)


def _ssm_expand_operators(kc_ref, bc_ref, cc_ref, k_ref, bw_ref, cw_ref):
    def iota(shape, dim):
        return lax.broadcasted_iota(jnp.int32, shape, dim)

    def group(idx, width):
        return jnp.right_shift(idx, width.bit_length() - 1) & (SSM_PACK - 1)

    hs, ps, tl = SSM_GROUP, SSM_STATE, SSM_T * LANES
    spread_h = ((iota((LANES, hs), 0) & (hs - 1)) == iota((LANES, hs), 1)).astype(BF16)
    kt = jnp.dot(spread_h, kc_ref[...], preferred_element_type=F32)
    same = group(iota((LANES, tl), 0), hs) == group(iota((LANES, tl), 1), hs)
    k_ref[...] = jnp.where(same, kt, 0.0).astype(BF16)
    r = iota((2 * ps, 2 * SSM_PSTATE), 0)
    c = iota((2 * ps, 2 * SSM_PSTATE), 1)
    spread_s = ((jnp.right_shift(r, 6) == jnp.right_shift(c, 9)) & ((r & (ps - 1)) == (c & (ps - 1)))).astype(BF16)
    bw = jnp.dot(bc_ref[...], spread_s, preferred_element_type=F32)
    shape = (tl, 2 * SSM_PSTATE)
    same = group(iota(shape, 0), hs) == group(iota(shape, 1), ps)
    bw_ref[...] = jnp.where(same, bw, 0.0).astype(BF16)
    shape = (LANES, 2 * SSM_PSTATE)
    same = group(iota(shape, 0), hs) == group(iota(shape, 1), ps)
    for j in range(SSM_T):
        ct = jnp.dot(spread_h, cc_ref[j], preferred_element_type=F32)
        cw_ref[j] = jnp.where(same, ct, 0.0).astype(BF16)


def _ssm_kernel(u_ref, kc_ref, bc_ref, cc_ref, a_ref, d_ref, o_ref, up_ref, y_ref, e_ref, s_ref, k_ref, bw_ref,
                cw_ref, *, batch, lp):
    rows = batch * lp
    n_chunks = lp // SSM_T
    tot_chunks = rows // SSM_T
    _ssm_expand_operators(kc_ref, bc_ref, cc_ref, k_ref, bw_ref, cw_ref)
    contract_lanes = (((1,), (1,)), ((), ()))
    up_ref[0:SSM_T, :] = jnp.zeros((SSM_T, LANES), F32)
    up_ref[SSM_T:, :] = u_ref[...].astype(F32)
    for b in range(batch):
        up_ref[SSM_T + b * lp:SSM_T + b * lp + PAD, :] = jnp.zeros((PAD, LANES), F32)

    ch = _ssm_row_chunk(rows)
    rmod = lax.broadcasted_iota(jnp.int32, (ch, LANES), 0) & (SSM_T - 1)
    for r0 in range(0, rows, ch):
        pieces = []
        for k in range(SSM_T):
            sl = up_ref[SSM_T - k + r0:SSM_T - k + r0 + ch, :]
            pieces.append(jnp.where(rmod >= k, sl, 0.0).astype(BF16))
        y_ref[r0:r0 + ch, :] = lax.dot_general(jnp.concatenate(pieces, axis=1), k_ref[...], contract_lanes,
                                               preferred_element_type=F32)

    pieces = [up_ref[pl.ds(SSM_T + i, tot_chunks, stride=SSM_T), :].astype(BF16) for i in range(SSM_T)]
    e_ref[...] = jnp.dot(jnp.concatenate(pieces, axis=1), bw_ref[...], preferred_element_type=F32)

    a_re = a_ref[:, :SSM_PSTATE]
    a_im = a_ref[:, SSM_PSTATE:]

    def step(c, carry):
        new = []
        for b in range(batch):
            sr, si = carry[b]
            idx = b * n_chunks + c
            s_ref[pl.ds(idx, 1), :SSM_PSTATE] = sr
            s_ref[pl.ds(idx, 1), SSM_PSTATE:] = si
            er = e_ref[pl.ds(idx, 1), :SSM_PSTATE]
            ei = e_ref[pl.ds(idx, 1), SSM_PSTATE:]
            new.append((a_re * sr - a_im * si + er, a_re * si + a_im * sr + ei))
        return tuple(new)

    zero = jnp.zeros((1, SSM_PSTATE), F32)
    lax.fori_loop(0, n_chunks, step, tuple((zero, zero) for _ in range(batch)))

    sb = s_ref[...].astype(BF16)
    for j in range(SSM_T):
        z = lax.dot_general(sb, cw_ref[j], contract_lanes, preferred_element_type=F32)
        idx = pl.ds(j, tot_chunks, stride=SSM_T)
        y_ref[idx, :] = y_ref[idx, :] + z

    y = y_ref[...] + d_ref[...] * up_ref[SSM_T:, :]
    o_ref[...] = jax.nn.gelu(y, approximate=True).astype(o_ref.dtype)


def _ssm(proj, kc, bc, cc, a_t, dskip, batch, lp):
    rows = proj.shape[0]
    tot_chunks = rows // SSM_T
    return pl.pallas_call(
        functools.partial(_ssm_kernel, batch=batch, lp=lp),
        grid=(SSM_PACKS,),
        in_specs=[
            pl.BlockSpec((rows, LANES), lambda p: (0, COL_US // LANES + p)),
            pl.BlockSpec((None, SSM_GROUP, SSM_T * LANES), lambda p: (p, 0, 0)),
            pl.BlockSpec((None, SSM_T * LANES, 2 * SSM_STATE), lambda p: (p, 0, 0)),
            pl.BlockSpec((None, SSM_T, SSM_GROUP, 2 * SSM_PSTATE), lambda p: (p, 0, 0, 0)),
            pl.BlockSpec((None, 1, 2 * SSM_PSTATE), lambda p: (p, 0, 0)),
            pl.BlockSpec((None, 1, LANES), lambda p: (p, 0, 0)),
        ],
        out_specs=pl.BlockSpec((rows, LANES), lambda p: (0, p)),
        out_shape=jax.ShapeDtypeStruct((rows, SSM_WIDTH), F32),
        scratch_shapes=[
            pltpu.VMEM((rows + SSM_T, LANES), F32),
            pltpu.VMEM((rows, LANES), F32),
            pltpu.VMEM((tot_chunks, 2 * SSM_PSTATE), F32),
            pltpu.VMEM((tot_chunks, 2 * SSM_PSTATE), F32),
            pltpu.VMEM((LANES, SSM_T * LANES), BF16),
            pltpu.VMEM((SSM_T * LANES, 2 * SSM_PSTATE), BF16),
            pltpu.VMEM((SSM_T, LANES, 2 * SSM_PSTATE), BF16),
        ],
        compiler_params=_params(("parallel",)),
        name="ssm",
    )(proj, kc, bc, cc, a_t, dskip)


def _ssm_operators(lam_re, lam_im, b_re, b_im, c_re, c_im, log_dt):
    p, h = SSM_STATE, SSM_GROUP
    dt = jnp.exp(log_dt.astype(F32))[:, None]
    lr, li = lam_re.astype(F32), lam_im.astype(F32)
    mag = jnp.exp(lr * dt)
    abar_re, abar_im = mag * jnp.cos(li * dt), mag * jnp.sin(li * dt)
    den = lr * lr + li * li
    nr = abar_re - 1.0
    f_re = (nr * lr + abar_im * li) / den
    f_im = (abar_im * lr - nr * li) / den
    br, bi = b_re.astype(F32), b_im.astype(F32)
    bbar_re = f_re[..., None] * br - f_im[..., None] * bi
    bbar_im = f_re[..., None] * bi + f_im[..., None] * br
    ks = jnp.arange(SSM_T + 1, dtype=F32)[:, None, None]
    pmag = jnp.exp(ks * (lr * dt)[None])
    pw_re = pmag * jnp.cos(ks * (li * dt)[None])
    pw_im = pmag * jnp.sin(ks * (li * dt)[None])
    cr, ci = c_re.astype(F32), c_im.astype(F32)

    packs, q = SSM_PACKS, SSM_PACK

    cb_re = cr[:, :, :, None] * bbar_re[:, None, :, :] - ci[:, :, :, None] * bbar_im[:, None, :, :]
    cb_im = cr[:, :, :, None] * bbar_im[:, None, :, :] + ci[:, :, :, None] * bbar_re[:, None, :, :]
    klag = (jnp.einsum('kgs,gosi->kgoi', pw_re[:SSM_T], cb_re)
            - jnp.einsum('kgs,gosi->kgoi', pw_im[:SSM_T], cb_im))
    kc = klag.reshape(SSM_T, packs, q, h, h).transpose(1, 3, 0, 2, 4).reshape(packs, h, SSM_T * LANES)

    rev_re, rev_im = pw_re[SSM_T - 1::-1, :, None, :], pw_im[SSM_T - 1::-1, :, None, :]
    bt_re, bt_im = jnp.swapaxes(bbar_re, 1, 2)[None], jnp.swapaxes(bbar_im, 1, 2)[None]
    inj = jnp.concatenate([rev_re * bt_re - rev_im * bt_im, rev_re * bt_im + rev_im * bt_re], axis=-1)
    bc = inj.reshape(SSM_T, packs, q * h, 2 * p).transpose(1, 0, 2, 3).reshape(packs, SSM_T * LANES, 2 * p)

    ca_re = cr[None] * pw_re[1:, :, None, :] - ci[None] * pw_im[1:, :, None, :]
    ca_im = cr[None] * pw_im[1:, :, None, :] + ci[None] * pw_re[1:, :, None, :]

    def by_pack(x):
        return x.reshape(SSM_T, packs, q, h, p).transpose(1, 0, 3, 2, 4)

    cc = jnp.stack([by_pack(ca_re), by_pack(-ca_im)], axis=3).reshape(packs, SSM_T, h, 2 * SSM_PSTATE)

    a_t = jnp.concatenate([pw_re[SSM_T].reshape(packs, 1, SSM_PSTATE),
                           pw_im[SSM_T].reshape(packs, 1, SSM_PSTATE)], axis=-1)
    return kc.astype(BF16), bc.astype(BF16), cc.astype(BF16), a_t


OUT_TN = 512


def _split_bf16(x):
    hi = x.astype(BF16)
    return hi, (x - hi.astype(F32)).astype(BF16)


def _route(hn, rwh_ref, rwl_ref, rb_ref):
    hi, lo = _split_bf16(hn)
    logits = (jnp.dot(hi, rwh_ref[...], preferred_element_type=F32)
              + (jnp.dot(lo, rwh_ref[...], preferred_element_type=F32)
                 + jnp.dot(hi, rwl_ref[...], preferred_element_type=F32))) + rb_ref[...]
    lane = lax.broadcasted_iota(jnp.int32, logits.shape, 1)

    def first_argmax(v, vmax):
        return jnp.min(jnp.where(v == vmax, lane, LANES), axis=-1, keepdims=True)

    gl = jnp.where(lane < N_EXPERT_GROUPS, logits, NEG)
    gmax = jnp.max(gl, axis=-1, keepdims=True)
    g_sel = first_argmax(gl, gmax)
    p_g = 1.0 / jnp.sum(jnp.exp(gl - gmax), axis=-1, keepdims=True)
    e_lane = lane - N_EXPERT_GROUPS
    in_group = (e_lane >= 0) & (e_lane < N_EXPERTS) & (jnp.right_shift(e_lane, 3) == g_sel)
    el = jnp.where(in_group, logits, NEG)
    m1 = jnp.max(el, axis=-1, keepdims=True)
    i1 = first_argmax(el, m1)
    el2 = jnp.where(lane == i1, NEG, el)
    m2 = jnp.max(el2, axis=-1, keepdims=True)
    i2 = first_argmax(el2, m2)
    z = jnp.sum(jnp.exp(el - m1), axis=-1, keepdims=True)
    p1 = 1.0 / z
    p2 = jnp.exp(m2 - m1) / z
    w1 = p1 / (p1 + p2) * p_g
    w2 = p2 / (p1 + p2) * p_g
    wts = jnp.where(lane == 0, w1, jnp.where(lane == 1, w2, 0.0))
    ids = jnp.where(lane == 0, i1 - N_EXPERT_GROUPS, jnp.where(lane == 1, i2 - N_EXPERT_GROUPS, 0))
    return wts, ids


def _assignment_ranks(ids, run_ref):
    tm = ids.shape[0]
    lane = lax.broadcasted_iota(jnp.int32, ids.shape, 1)
    oh0 = lane == ids[:, 0:1]
    oh1 = lane == ids[:, 1:2]
    hits = (oh0 | oh1).astype(BF16)
    r = lax.broadcasted_iota(jnp.int32, (tm, tm), 0)
    c = lax.broadcasted_iota(jnp.int32, (tm, tm), 1)
    before = jnp.dot((c < r).astype(BF16), hits, preferred_element_type=F32) + run_ref[...]
    rank0 = jnp.sum(jnp.where(oh0, before, 0.0), axis=-1, keepdims=True)
    rank1 = jnp.sum(jnp.where(oh1, before, 0.0), axis=-1, keepdims=True)
    run_ref[...] += jnp.sum(hits.astype(F32), axis=0, keepdims=True)
    return jnp.where(lane == 0, rank0, jnp.where(lane == 1, rank1, 0.0)).astype(jnp.int32)


def _mixer_out_kernel(h_ref, a_ref, y_ref, c_ref, wo_ref, wglu_ref, g_ref, rwh_ref, rwl_ref, rb_ref,
                      o_ref, hs_ref, wt_ref, id_ref, rank_ref, cnt_ref, run_ref):
    @pl.when(pl.program_id(0) == 0)
    def _():
        run_ref[...] = jnp.zeros(run_ref.shape, F32)

    y = y_ref[...]
    gate = jnp.dot(y.astype(BF16), wglu_ref[...], preferred_element_type=F32)
    b = (y * jax.nn.sigmoid(gate)).astype(BF16)
    mixed = jnp.concatenate([a_ref[...], b, c_ref[...]], axis=1)
    for j in range(D_MODEL // OUT_TN):
        cols = slice(j * OUT_TN, (j + 1) * OUT_TN)
        o_ref[:, cols] = h_ref[:, cols] + jnp.dot(mixed, wo_ref[:, cols], preferred_element_type=F32)
    x = o_ref[...]
    hn = _rms_norm(x, g_ref[...])
    _store_slabs(hs_ref, hn)
    wts, ids = _route(hn, rwh_ref, rwl_ref, rb_ref)
    wt_ref[...] = wts
    id_ref[...] = ids
    rank_ref[...] = _assignment_ranks(ids, run_ref)
    cnt_ref[...] = run_ref[...].astype(jnp.int32)


def _mixer_out(h, oa, yg, oc, wo, wglu, g, rw_hi, rw_lo, rb):
    rows = h.shape[0]
    tm = _row_tile(rows)

    def row_spec(width):
        return pl.BlockSpec((tm, width), lambda i: (i, 0))

    def whole(shape):
        return pl.BlockSpec(shape, lambda i: (0, 0))

    return pl.pallas_call(
        _mixer_out_kernel,
        grid=(rows // tm,),
        in_specs=[
            row_spec(D_MODEL), row_spec(SWA_WIDTH), row_spec(SSM_WIDTH), row_spec(DIFF_WIDTH),
            whole((D_MODEL, D_MODEL)), whole((SSM_WIDTH, SSM_WIDTH)), whole((1, D_MODEL)),
            whole((D_MODEL, LANES)), whole((D_MODEL, LANES)), whole((1, LANES)),
        ],
        out_specs=[row_spec(D_MODEL), pl.BlockSpec((tm * SLAB, LANES), lambda i: (i, 0)), row_spec(LANES),
                   row_spec(LANES), row_spec(LANES), whole((1, LANES))],
        out_shape=[
            jax.ShapeDtypeStruct((rows, D_MODEL), F32),
            jax.ShapeDtypeStruct((rows * SLAB, LANES), F32),
            jax.ShapeDtypeStruct((rows, LANES), F32),
            jax.ShapeDtypeStruct((rows, LANES), jnp.int32),
            jax.ShapeDtypeStruct((rows, LANES), jnp.int32),
            jax.ShapeDtypeStruct((1, LANES), jnp.int32),
        ],
        scratch_shapes=[pltpu.VMEM((1, LANES), F32)],
        compiler_params=_params(("arbitrary",)),
        name="mixer_out",
    )(h, oa, yg, oc, wo, wglu, g, rw_hi, rw_lo, rb)


def _dest_kernel(ids_ref, rank_ref, cnt_ref, dest_ref):
    shift = MOE_BM.bit_length() - 1
    cnt = jnp.broadcast_to(cnt_ref[...], (8, LANES))
    padded = jnp.left_shift(jnp.right_shift(cnt + (MOE_BM - 1), shift), shift)
    lane8 = lax.broadcasted_iota(jnp.int32, (8, LANES), 1)
    ends = padded
    k = 1
    while k < LANES:
        ends = ends + jnp.where(lane8 >= k, pltpu.roll(ends, k, 1), 0)
        k *= 2
    pstart = (ends - padded)[0:1].astype(F32)
    ids = ids_ref[...]
    lane = lax.broadcasted_iota(jnp.int32, ids.shape, 1)
    start0 = jnp.sum(jnp.where(lane == ids[:, 0:1], pstart, 0.0), axis=-1, keepdims=True)
    start1 = jnp.sum(jnp.where(lane == ids[:, 1:2], pstart, 0.0), axis=-1, keepdims=True)
    starts = jnp.where(lane == 0, start0, jnp.where(lane == 1, start1, 0.0)).astype(jnp.int32)
    dest_ref[...] = starts + rank_ref[...]


def _assignment_dest(ids, ranks, cnt):
    rows = ids.shape[0]
    tm = _row_tile(rows)
    row_spec = pl.BlockSpec((tm, LANES), lambda i: (i, 0))
    return pl.pallas_call(
        _dest_kernel,
        grid=(rows // tm,),
        in_specs=[row_spec, row_spec, pl.BlockSpec((1, LANES), lambda i: (0, 0))],
        out_specs=row_spec,
        out_shape=jax.ShapeDtypeStruct((rows, LANES), jnp.int32),
        compiler_params=_params(("parallel",)),
        name="moe_dest",
    )(ids, ranks, cnt)


def _dispatch_plan(ids, ranks, cnt, n_blk):
    dest = _assignment_dest(ids, ranks, cnt)[:, :2].reshape(-1)
    counts = cnt[0, :N_EXPERTS]
    padded = (counts + MOE_BM - 1) // MOE_BM * MOE_BM
    pend = jnp.cumsum(padded)
    blk_start = jnp.arange(n_blk, dtype=jnp.int32) * MOE_BM
    blk_expert = jnp.minimum(jnp.sum((pend[None, :] <= blk_start[:, None]).astype(jnp.int32), axis=1), N_EXPERTS - 1)
    n_used = (pend[-1] // MOE_BM).astype(jnp.int32).reshape(1)
    ex = jnp.arange(N_EXPERTS, dtype=jnp.int32)
    nonempty = counts > 0
    run_parity = (jnp.cumsum(nonempty.astype(jnp.int32)) - 1) & 1
    later = (ex[None, :] > ex[:, None]) & nonempty[None, :]
    nxt = jnp.min(jnp.where(later, ex[None, :], N_EXPERTS), axis=1)
    nxt = jnp.where(nxt == N_EXPERTS, -1, nxt)
    plan = dict(dest=dest, blk_expert=blk_expert, n_used=n_used, blk_parity=run_parity[blk_expert].astype(jnp.int32),
                blk_next=nxt[blk_expert].astype(jnp.int32))
    return plan


SLAB = D_MODEL // LANES


def _store_slabs(ref, x):
    rows = x.shape[0]
    for s in range(SLAB):
        ref[pl.ds(s, rows, stride=SLAB), :] = x[:, s * LANES:(s + 1) * LANES]


def _load_slabs(ref, rows):
    return jnp.concatenate([ref[pl.ds(s, rows, stride=SLAB), :] for s in range(SLAB)], axis=1)


def _slab_gather_copy(src_hbm, dst_buf, sem, src_row, dst_row):
    src = src_hbm.at[pl.ds(pl.multiple_of(src_row * SLAB, SLAB), SLAB), :]
    return pltpu.make_async_copy(src, dst_buf.at[pl.ds(dst_row * SLAB, SLAB), :], sem)


def _block_wait_copy(src_hbm, dst_buf, sem):
    return pltpu.make_async_copy(src_hbm.at[pl.ds(0, dst_buf.shape[0]), :], dst_buf, sem)


GATHER_SLOTS = 3
WEIGHT_DMA_PRIORITY = 1


def _expert_kernel(be_ref, par_ref, nxt_ref, dest_ref, nused_ref, hn_hbm, wg_hbm, wu_hbm, wd_hbm, o_ref,
                   src_ref, xbuf, wgf, wuf, wdf, wgb, wub, wdb, gsem, wsem, *, layer):
    i = pl.program_id(0)
    n_used = nused_ref[0]
    slot = i % GATHER_SLOTS

    def start_gather(blk, slot_):
        for r in range(MOE_BM):
            _slab_gather_copy(hn_hbm, xbuf.at[slot_], gsem.at[slot_], src_ref[blk * MOE_BM + r], r).start()

    def weight_copies(expert, wslot):
        pairs = ((wg_hbm, wgf), (wu_hbm, wuf), (wd_hbm, wdf))
        return [pltpu.make_async_copy(w.at[layer, expert], buf.at[wslot], wsem.at[wslot]) for w, buf in pairs]

    @pl.when(i == 0)
    def _():
        for c in weight_copies(be_ref[0], 0):
            c.start(priority=WEIGHT_DMA_PRIORITY)

        def clear(j, carry):
            src_ref[j] = 0
            return carry

        def invert(t, carry):
            src_ref[dest_ref[t]] = jnp.right_shift(t, 1)
            return carry

        lax.fori_loop(0, src_ref.shape[0], clear, 0, unroll=8)
        lax.fori_loop(0, dest_ref.shape[0], invert, 0, unroll=8)
        start_gather(0, 0)
        start_gather(jnp.minimum(1, n_used), 1)

    @pl.when(i <= n_used + 1)
    def _():
        _block_wait_copy(hn_hbm, xbuf.at[slot], gsem.at[slot]).wait()

    @pl.when(i < n_used)
    def _():
        @pl.when((i == 0) | (be_ref[i] != be_ref[jnp.maximum(i - 1, 0)]))
        def _():
            wslot = par_ref[i]
            for c in weight_copies(be_ref[i], wslot):
                c.wait()

            @pl.when(nxt_ref[i] >= 0)
            def _():
                for c in weight_copies(nxt_ref[i], 1 - wslot):
                    c.start(priority=WEIGHT_DMA_PRIORITY)

            wgb[...] = wgf[wslot].astype(BF16)
            wub[...] = wuf[wslot].astype(BF16)
            wdb[...] = wdf[wslot].astype(BF16)

        x = _load_slabs(xbuf.at[slot], MOE_BM).astype(BF16)
        start_gather(jnp.minimum(i + 2, n_used), (i + 2) % GATHER_SLOTS)
        gate = jnp.dot(x, wgb[...], preferred_element_type=F32)
        up = jnp.dot(x, wub[...], preferred_element_type=F32)
        act = (jax.nn.silu(gate) * up).astype(BF16)
        _store_slabs(o_ref, jnp.dot(act, wdb[...], preferred_element_type=F32))

    @pl.when(i >= n_used)
    def _():
        o_ref[...] = jnp.zeros(o_ref.shape, o_ref.dtype)


def _experts(hn_slabs, plan, wg, wu, wd, layer, n_blk):
    hbm = pl.BlockSpec(memory_space=pl.ANY)
    blk_rows = MOE_BM * SLAB
    grid_spec = pltpu.PrefetchScalarGridSpec(
        num_scalar_prefetch=5,
        grid=(n_blk,),
        in_specs=[hbm, hbm, hbm, hbm],
        out_specs=pl.BlockSpec((blk_rows, LANES), lambda i, *_: (i, 0)),
        scratch_shapes=[
            pltpu.SMEM((n_blk * MOE_BM,), jnp.int32),
            pltpu.VMEM((GATHER_SLOTS, blk_rows, LANES), F32),
            pltpu.VMEM((2, D_MODEL, D_EXPERT), F32),
            pltpu.VMEM((2, D_MODEL, D_EXPERT), F32),
            pltpu.VMEM((2, D_EXPERT, D_MODEL), F32),
            pltpu.VMEM((D_MODEL, D_EXPERT), BF16),
            pltpu.VMEM((D_MODEL, D_EXPERT), BF16),
            pltpu.VMEM((D_EXPERT, D_MODEL), BF16),
            pltpu.SemaphoreType.DMA((GATHER_SLOTS,)),
            pltpu.SemaphoreType.DMA((2,)),
        ],
    )
    return pl.pallas_call(
        functools.partial(_expert_kernel, layer=layer),
        grid_spec=grid_spec,
        out_shape=jax.ShapeDtypeStruct((n_blk * blk_rows, LANES), F32),
        compiler_params=_params(("arbitrary",)),
        name="moe_experts",
    )(plan["blk_expert"], plan["blk_parity"], plan["blk_next"], plan["dest"], plan["n_used"], hn_slabs, wg, wu, wd)


def _combine_kernel(dest_ref, h_ref, wt_ref, g_ref, y_hbm, o_ref, ybuf, sem, *, tile_of_step, final_norm):
    i = pl.program_id(0)
    n = pl.num_programs(0)
    slot = i % 2

    def start_gather(step, slot_):
        tile = tile_of_step(step)
        for r in range(MOE_BM):
            for k in range(2):
                row = dest_ref[2 * (tile * MOE_BM + r) + k]
                _slab_gather_copy(y_hbm, ybuf.at[slot_, k], sem.at[slot_], row, r).start(priority=k)

    @pl.when(i == 0)
    def _():
        start_gather(0, 0)

    @pl.when(i + 1 < n)
    def _():
        start_gather(i + 1, 1 - slot)

    for k in range(2):
        _block_wait_copy(y_hbm, ybuf.at[slot, k], sem.at[slot]).wait()
    wt = wt_ref[...]
    y0 = _load_slabs(ybuf.at[slot, 0], MOE_BM)
    y1 = _load_slabs(ybuf.at[slot, 1], MOE_BM)
    out = h_ref[...] + wt[:, 0:1] * y0 + wt[:, 1:2] * y1
    o_ref[...] = _rms_norm(out, g_ref[...]) if final_norm else out


def _combine(h, wts, y_slabs, dest, g, batch, nb, final_norm):
    if final_norm:
        per_batch = nb - 1
        n_steps = batch * per_batch

        def tile_of_step(i):
            return (i // per_batch) * nb + 1 + i % per_batch
    else:
        n_steps = batch * nb

        def tile_of_step(i):
            return i

    grid_spec = pltpu.PrefetchScalarGridSpec(
        num_scalar_prefetch=1,
        grid=(n_steps,),
        in_specs=[
            pl.BlockSpec((MOE_BM, D_MODEL), lambda i, d: (tile_of_step(i), 0)),
            pl.BlockSpec((MOE_BM, LANES), lambda i, d: (tile_of_step(i), 0)),
            pl.BlockSpec((1, D_MODEL), lambda i, d: (0, 0)),
            pl.BlockSpec(memory_space=pl.ANY),
        ],
        out_specs=pl.BlockSpec((MOE_BM, D_MODEL), lambda i, d: (i, 0)),
        scratch_shapes=[pltpu.VMEM((2, 2, MOE_BM * SLAB, LANES), F32), pltpu.SemaphoreType.DMA((2,))],
    )
    return pl.pallas_call(
        functools.partial(_combine_kernel, tile_of_step=tile_of_step, final_norm=final_norm),
        grid_spec=grid_spec,
        out_shape=jax.ShapeDtypeStruct((n_steps * MOE_BM, D_MODEL), F32),
        compiler_params=_params(("arbitrary",)),
        name="moe_combine",
    )(dest, h, wts, g, y_slabs)


def _moe(h, hn_slabs, wts, ids, ranks, cnt, wg, wu, wd, layer, final_g, batch, nb):
    rows = h.shape[0]
    n_blk = -(-(2 * rows) // MOE_BM) + N_EXPERTS + 1
    plan = _dispatch_plan(ids, ranks, cnt, n_blk)
    y_slabs = _experts(hn_slabs, plan, wg, wu, wd, layer, n_blk)
    final_norm = final_g is not None
    g = final_g if final_norm else jnp.ones((1, D_MODEL), F32)
    return _combine(h, wts, y_slabs, plan["dest"], g, batch, nb, final_norm)


def _rope_tables(batch, lp):
    half = HEAD_DIM // 2
    inv = ROPE_THETA ** (-jnp.arange(half, dtype=F32) / half)
    pos = (jnp.arange(lp, dtype=jnp.int32) - PAD).astype(F32)
    ang = pos[:, None] * inv[None, :]
    cos = jnp.tile(jnp.cos(ang), (batch, LANES // half))
    sin = jnp.sin(ang)
    sin = jnp.tile(jnp.concatenate([-sin, sin], axis=1), (batch, LANES // HEAD_DIM))
    return cos, sin


def kernel(x, meta_tokens, attn_norm_g, ffn_norm_g, w_in, swa_sinks, ssm_lambda_re, ssm_lambda_im, ssm_b_re, ssm_b_im, ssm_c_re, ssm_c_im, ssm_d, ssm_log_dt, ssm_w_glu, diff_lambda_q1, diff_lambda_k1, diff_lambda_q2, diff_lambda_k2, diff_subln_g, w_out, router_group_w, router_group_b, router_expert_w, router_expert_b, moe_w_gate, moe_w_up, moe_w_down, final_norm_g):
    batch, seq, d = x.shape
    depth = w_in.shape[0]
    lp = PAD + N_META + seq
    nb = lp // BLK
    meta = jnp.broadcast_to(meta_tokens[None].astype(x.dtype), (batch, N_META, d))
    h = jnp.concatenate([jnp.zeros((batch, PAD, d), x.dtype), meta, x], axis=1).reshape(batch * lp, d)
    cos, sin = _rope_tables(batch, lp)
    for l in range(depth):
        proj = _norm_inproj(h, attn_norm_g[l][None], w_in[l].astype(BF16), cos, sin)
        o_a = _swa(proj, swa_sinks[l].astype(F32), batch, nb)
        kflat, bw, cw, a_t = _ssm_operators(ssm_lambda_re[l], ssm_lambda_im[l], ssm_b_re[l], ssm_b_im[l],
                                            ssm_c_re[l], ssm_c_im[l], ssm_log_dt[l])
        yg = _ssm(proj, kflat, bw, cw, a_t, ssm_d[l].astype(F32).reshape(SSM_PACKS, 1, LANES), batch, lp)
        lam_init = 0.8 - 0.6 * math.exp(-0.3 * l)
        lam_vecs = jnp.stack([diff_lambda_q1[l], diff_lambda_k1[l], diff_lambda_q2[l], diff_lambda_k2[l]]).astype(F32)
        o_c = _diff_attn(proj, lam_vecs, diff_subln_g[l][None].astype(F32), lam_init, batch, nb)
        wo = w_out[l].astype(BF16)
        rw = jnp.concatenate([router_group_w[l], router_expert_w[l]], axis=1).astype(F32)
        rw = jnp.pad(rw, ((0, 0), (0, LANES - rw.shape[1])))
        rb = jnp.concatenate([router_group_b[l], router_expert_b[l]]).astype(F32)
        rb = jnp.pad(rb, (0, LANES - rb.shape[0]))[None]
        rw_hi, rw_lo = _split_bf16(rw)
        g_ffn = ffn_norm_g[l][None].astype(F32)
        h, hn_slabs, wts, ids, ranks, cnt = _mixer_out(h, o_a, yg, o_c, wo, ssm_w_glu[l].astype(BF16), g_ffn, rw_hi,
                                                       rw_lo, rb)
        final_g = final_norm_g[None].astype(F32) if l == depth - 1 else None
        h = _moe(h, hn_slabs, wts, ids, ranks, cnt, moe_w_gate, moe_w_up, moe_w_down, l, final_g, batch, nb)
    return h.reshape(batch, seq, d)
```

```python
import functools
import math

import jax
import jax.numpy as jnp
from jax import lax
from jax.experimental import pallas as pl
from jax.experimental.pallas import tpu as pltpu

D_MODEL = 2048
N_META = 16
BLK = 128
PAD = BLK - N_META
ROPE_THETA = 10000.0
NORM_EPS = 1e-5
NEG = -1e30
SWA_HEADS = 12
SWA_KV_HEADS = 4
HEAD_DIM = 64
SWA_WIDTH = SWA_HEADS * HEAD_DIM
SSM_WIDTH = 768
SSM_GROUP = 16
SSM_GROUPS = SSM_WIDTH // SSM_GROUP
SSM_STATE = 64
DIFF_HEADS = 4
DIFF_V_DIM = 128
DIFF_WIDTH = DIFF_HEADS * DIFF_V_DIM
IN_WIDTH = 3584
N_EXPERT_GROUPS = 4
EXPERTS_PER_GROUP = 8
N_EXPERTS = N_EXPERT_GROUPS * EXPERTS_PER_GROUP
D_EXPERT = 512

COL_KA = 768
COL_VA = 1024
COL_US = 1280
COL_QD = 2048
COL_KD = 2560
COL_VD = 3072

LANES = 128
SSM_T = 16
SSM_PACK = LANES // SSM_GROUP
SSM_PACKS = SSM_GROUPS // SSM_PACK
SSM_PSTATE = SSM_PACK * SSM_STATE
MOE_BM = 256
VMEM_LIMIT = 56 * 1024 * 1024

F32 = jnp.float32
BF16 = jnp.bfloat16


def _row_tile(rows):
    for t in (512, 256, 128):
        if rows % t == 0:
            return t
    raise ValueError(f"row count {rows} is not a multiple of 128")


def _params(sem, vmem=VMEM_LIMIT):
    return pltpu.CompilerParams(dimension_semantics=sem, vmem_limit_bytes=vmem)


def _rms_norm(x, g):
    return x * lax.rsqrt(jnp.mean(x * x, axis=-1, keepdims=True) + NORM_EPS) * g


IN_TN = 512
ROPE_TILES = (0, 1, 4, 5)


def _norm_inproj_kernel(x_ref, g_ref, w_ref, cos_ref, sin_ref, o_ref, xn_ref):
    x = x_ref[...]
    ms = jnp.mean(x * x, axis=-1, keepdims=True)
    xn_ref[...] = (x * lax.rsqrt(ms + NORM_EPS) * g_ref[...]).astype(BF16)
    shape = (x.shape[0], IN_TN)
    lane = lax.broadcasted_iota(jnp.int32, shape, 1)
    first_half = (lane & (HEAD_DIM - 1)) < (HEAD_DIM // 2)
    reps = IN_TN // LANES
    cos = jnp.tile(cos_ref[...], (1, reps))
    sin = jnp.tile(sin_ref[...], (1, reps))
    for j in range(IN_WIDTH // IN_TN):
        cols = slice(j * IN_TN, (j + 1) * IN_TN)
        acc = jnp.dot(xn_ref[...], w_ref[:, cols], preferred_element_type=F32)
        if j in ROPE_TILES:
            partner = jnp.where(first_half, pltpu.roll(acc, IN_TN - HEAD_DIM // 2, 1),
                                pltpu.roll(acc, HEAD_DIM // 2, 1))
            acc = acc * cos + partner * sin
        o_ref[:, cols] = acc.astype(o_ref.dtype)


def _norm_inproj(h, g, w, cos, sin):
    rows = h.shape[0]
    tm = _row_tile(rows)
    return pl.pallas_call(
        _norm_inproj_kernel,
        grid=(rows // tm,),
        in_specs=[
            pl.BlockSpec((tm, D_MODEL), lambda i: (i, 0)),
            pl.BlockSpec((1, D_MODEL), lambda i: (0, 0)),
            pl.BlockSpec((D_MODEL, IN_WIDTH), lambda i: (0, 0)),
            pl.BlockSpec((tm, LANES), lambda i: (i, 0)),
            pl.BlockSpec((tm, LANES), lambda i: (i, 0)),
        ],
        out_specs=pl.BlockSpec((tm, IN_WIDTH), lambda i: (i, 0)),
        out_shape=jax.ShapeDtypeStruct((rows, IN_WIDTH), BF16),
        scratch_shapes=[pltpu.VMEM((tm, D_MODEL), BF16)],
        compiler_params=_params(("parallel",)),
        name="norm_inproj",
    )(h, g, w, cos, sin)


def _swa_kernel(sink_ref, q_ref, kp_ref, kc_ref, vp_ref, vc_ref, o_ref):
    n = pl.program_id(1)
    shape = (BLK, 2 * BLK)
    qi = lax.broadcasted_iota(jnp.int32, shape, 0)
    c = lax.broadcasted_iota(jnp.int32, shape, 1)
    kj = (n - 1) * BLK + c
    delta = qi + BLK - c
    ok = (delta >= 0) & (delta < BLK) & (kj >= PAD)
    grp = SWA_HEADS // SWA_KV_HEADS
    outs = []
    for j in range(SWA_KV_HEADS):
        cols = slice(j * HEAD_DIM, (j + 1) * HEAD_DIM)
        kk = jnp.concatenate([kp_ref[:, cols], kc_ref[:, cols]], axis=0)
        vv = jnp.concatenate([vp_ref[:, cols], vc_ref[:, cols]], axis=0)
        for g in range(grp):
            hd = j * grp + g
            qh = q_ref[:, hd * HEAD_DIM:(hd + 1) * HEAD_DIM]
            s = lax.dot_general(qh, kk, (((1,), (1,)), ((), ())), preferred_element_type=F32) * (HEAD_DIM ** -0.5)
            s = jnp.where(ok, s, NEG)
            sink = sink_ref[hd]
            m = jnp.maximum(jnp.max(s, axis=-1, keepdims=True), sink)
            p = jnp.exp(s - m)
            denom = jnp.sum(p, axis=-1, keepdims=True) + jnp.exp(sink - m)
            outs.append(jnp.dot(p.astype(BF16), vv, preferred_element_type=F32) / denom)
    o_ref[...] = jnp.concatenate(outs, axis=1).astype(o_ref.dtype)


def _swa(proj, sinks, batch, nb):
    rows = proj.shape[0]
    kvw = SWA_KV_HEADS * HEAD_DIM

    def cur(col):
        return lambda b, n: (b * nb + n, col)

    def prev(col):
        return lambda b, n: (b * nb + jnp.maximum(n - 1, 0), col)

    return pl.pallas_call(
        _swa_kernel,
        grid=(batch, nb),
        in_specs=[
            pl.BlockSpec(memory_space=pltpu.SMEM),
            pl.BlockSpec((BLK, SWA_WIDTH), cur(0)),
            pl.BlockSpec((BLK, kvw), prev(COL_KA // kvw)),
            pl.BlockSpec((BLK, kvw), cur(COL_KA // kvw)),
            pl.BlockSpec((BLK, kvw), prev(COL_VA // kvw)),
            pl.BlockSpec((BLK, kvw), cur(COL_VA // kvw)),
        ],
        out_specs=pl.BlockSpec((BLK, SWA_WIDTH), cur(0)),
        out_shape=jax.ShapeDtypeStruct((rows, SWA_WIDTH), BF16),
        compiler_params=_params(("parallel", "parallel")),
        name="swa",
    )(sinks, proj, proj, proj, proj, proj)


def _diff_kernel(lam_ref, g_ref, q_ref, k_ref, v_ref, o_ref, *, lam_init, nb):
    row = lax.broadcasted_iota(jnp.int32, (BLK, BLK), 0)
    col = lax.broadcasted_iota(jnp.int32, (BLK, BLK), 1)
    causal_ok = col <= row
    pad_ok = col >= PAD
    lane = lax.broadcasted_iota(jnp.int32, (BLK, LANES), 1)
    lv = lam_ref[...]
    lam = (jnp.exp(jnp.sum(lv[0:1] * lv[1:2], axis=-1, keepdims=True))
           - jnp.exp(jnp.sum(lv[2:3] * lv[3:4], axis=-1, keepdims=True)) + lam_init)
    gain = g_ref[...] * (1.0 - lam_init)
    scale = HEAD_DIM ** -0.5
    for n in range(nb):
        ke = (n + 1) * BLK
        q = q_ref[n * BLK:(n + 1) * BLK, :] * scale
        kx = k_ref[0:ke, :]
        vx = v_ref[0:ke, :]
        heads = []
        for s_idx in range(2):
            qx = jnp.where((lane >= HEAD_DIM) == bool(s_idx), q, jnp.zeros_like(q))
            s = lax.dot_general(qx, kx, (((1,), (1,)), ((), ())), preferred_element_type=F32)
            if n == 0:
                s = jnp.where(causal_ok & pad_ok, s, NEG)
            else:
                first = jnp.where(pad_ok, s[:, :BLK], NEG)
                last = jnp.where(causal_ok, s[:, ke - BLK:], NEG)
                middle = [s[:, BLK:ke - BLK]] if n > 1 else []
                s = jnp.concatenate([first] + middle + [last], axis=1)
            m = jnp.max(s, axis=-1, keepdims=True)
            p = jnp.exp(s - m)
            denom = jnp.sum(p, axis=-1, keepdims=True)
            heads.append(jnp.dot(p.astype(BF16), vx, preferred_element_type=F32) / denom)
        o = heads[0] - lam * heads[1]
        o = o * lax.rsqrt(jnp.mean(o * o, axis=-1, keepdims=True) + NORM_EPS) * gain
        o_ref[n * BLK:(n + 1) * BLK, :] = o.astype(o_ref.dtype)


def _diff_attn(proj, lam_vecs, subln_g, lam_init, batch, nb):
    rows = proj.shape[0]
    lp = nb * BLK
    return pl.pallas_call(
        functools.partial(_diff_kernel, lam_init=lam_init, nb=nb),
        grid=(batch, DIFF_HEADS),
        in_specs=[
            pl.BlockSpec((4, HEAD_DIM), lambda b, h: (0, 0)),
            pl.BlockSpec((1, DIFF_V_DIM), lambda b, h: (0, 0)),
            pl.BlockSpec((lp, LANES), lambda b, h: (b, COL_QD // LANES + h)),
            pl.BlockSpec((lp, LANES), lambda b, h: (b, COL_KD // LANES + h)),
            pl.BlockSpec((lp, LANES), lambda b, h: (b, COL_VD // LANES + h)),
        ],
        out_specs=pl.BlockSpec((lp, DIFF_V_DIM), lambda b, h: (b, h)),
        out_shape=jax.ShapeDtypeStruct((rows, DIFF_WIDTH), BF16),
        compiler_params=_params(("parallel", "parallel")),
        name="diff_attn",
    )(lam_vecs, subln_g, proj, proj, proj)


def _ssm_row_chunk(rows):
    for c in (544, 512, 384, 256, 128, 64, 32, 16):
        if rows % c == 0:
            return c
    raise ValueError(rows)


def _ssm_expand_operators(kc_ref, bc_ref, cc_ref, k_ref, bw_ref, cw_ref):
    def iota(shape, dim):
        return lax.broadcasted_iota(jnp.int32, shape, dim)

    def group(idx, width):
        return jnp.right_shift(idx, width.bit_length() - 1) & (SSM_PACK - 1)

    hs, ps, tl = SSM_GROUP, SSM_STATE, SSM_T * LANES
    spread_h = ((iota((LANES, hs), 0) & (hs - 1)) == iota((LANES, hs), 1)).astype(BF16)
    kt = jnp.dot(spread_h, kc_ref[...], preferred_element_type=F32)
    same = group(iota((LANES, tl), 0), hs) == group(iota((LANES, tl), 1), hs)
    k_ref[...] = jnp.where(same, kt, 0.0).astype(BF16)
    spread_s = (iota((ps, SSM_PSTATE), 0) == (iota((ps, SSM_PSTATE), 1) & (ps - 1))).astype(BF16)
    shape = (tl, SSM_PSTATE)
    same = group(iota(shape, 0), hs) == group(iota(shape, 1), ps)
    for part in range(2):
        bw = jnp.dot(bc_ref[part], spread_s, preferred_element_type=F32)
        bw_ref[:, part * SSM_PSTATE:(part + 1) * SSM_PSTATE] = jnp.where(same, bw, 0.0).astype(BF16)
    shape = (LANES, SSM_PSTATE)
    same = group(iota(shape, 0), hs) == group(iota(shape, 1), ps)
    for j in range(SSM_T):
        for part in range(2):
            ct = jnp.dot(spread_h, cc_ref[part, j], preferred_element_type=F32)
            cw_ref[j, :, part * SSM_PSTATE:(part + 1) * SSM_PSTATE] = jnp.where(same, ct, 0.0).astype(BF16)


def _ssm_kernel(u_ref, kc_ref, bc_ref, cc_ref, a_ref, d_ref, o_ref, up_ref, y_ref, e_ref, s_ref, k_ref, bw_ref,
                cw_ref, *, batch, lp):
    rows = batch * lp
    n_chunks = lp // SSM_T
    tot_chunks = rows // SSM_T
    _ssm_expand_operators(kc_ref, bc_ref, cc_ref, k_ref, bw_ref, cw_ref)
    contract_lanes = (((1,), (1,)), ((), ()))
    up_ref[0:SSM_T, :] = jnp.zeros((SSM_T, LANES), F32)
    up_ref[SSM_T:, :] = u_ref[...].astype(F32)
    for b in range(batch):
        up_ref[SSM_T + b * lp:SSM_T + b * lp + PAD, :] = jnp.zeros((PAD, LANES), F32)

    ch = _ssm_row_chunk(rows)
    rmod = lax.broadcasted_iota(jnp.int32, (ch, LANES), 0) & (SSM_T - 1)
    for r0 in range(0, rows, ch):
        pieces = []
        for k in range(SSM_T):
            sl = up_ref[SSM_T - k + r0:SSM_T - k + r0 + ch, :]
            pieces.append(jnp.where(rmod >= k, sl, 0.0).astype(BF16))
        y_ref[r0:r0 + ch, :] = lax.dot_general(jnp.concatenate(pieces, axis=1), k_ref[...], contract_lanes,
                                               preferred_element_type=F32)

    pieces = [up_ref[pl.ds(SSM_T + i, tot_chunks, stride=SSM_T), :].astype(BF16) for i in range(SSM_T)]
    e_ref[...] = jnp.dot(jnp.concatenate(pieces, axis=1), bw_ref[...], preferred_element_type=F32)

    a_re = a_ref[:, :SSM_PSTATE]
    a_im = a_ref[:, SSM_PSTATE:]

    def step(c, carry):
        new = []
        for b in range(batch):
            sr, si = carry[b]
            idx = b * n_chunks + c
            s_ref[pl.ds(idx, 1), :SSM_PSTATE] = sr
            s_ref[pl.ds(idx, 1), SSM_PSTATE:] = si
            er = e_ref[pl.ds(idx, 1), :SSM_PSTATE]
            ei = e_ref[pl.ds(idx, 1), SSM_PSTATE:]
            new.append((a_re * sr - a_im * si + er, a_re * si + a_im * sr + ei))
        return tuple(new)

    zero = jnp.zeros((1, SSM_PSTATE), F32)
    lax.fori_loop(0, n_chunks, step, tuple((zero, zero) for _ in range(batch)))

    sb = s_ref[...].astype(BF16)
    for j in range(SSM_T):
        z = lax.dot_general(sb, cw_ref[j], contract_lanes, preferred_element_type=F32)
        idx = pl.ds(j, tot_chunks, stride=SSM_T)
        y_ref[idx, :] = y_ref[idx, :] + z

    y = y_ref[...] + d_ref[...] * up_ref[SSM_T:, :]
    o_ref[...] = jax.nn.gelu(y, approximate=True).astype(o_ref.dtype)


def _ssm(proj, kc, bc, cc, a_t, dskip, batch, lp):
    rows = proj.shape[0]
    tot_chunks = rows // SSM_T
    return pl.pallas_call(
        functools.partial(_ssm_kernel, batch=batch, lp=lp),
        grid=(SSM_PACKS,),
        in_specs=[
            pl.BlockSpec((rows, LANES), lambda p: (0, COL_US // LANES + p)),
            pl.BlockSpec((None, SSM_GROUP, SSM_T * LANES), lambda p: (p, 0, 0)),
            pl.BlockSpec((None, 2, SSM_T * LANES, SSM_STATE), lambda p: (p, 0, 0, 0)),
            pl.BlockSpec((None, 2, SSM_T, SSM_GROUP, SSM_PSTATE), lambda p: (p, 0, 0, 0, 0)),
            pl.BlockSpec((None, 1, 2 * SSM_PSTATE), lambda p: (p, 0, 0)),
            pl.BlockSpec((None, 1, LANES), lambda p: (p, 0, 0)),
        ],
        out_specs=pl.BlockSpec((rows, LANES), lambda p: (0, p)),
        out_shape=jax.ShapeDtypeStruct((rows, SSM_WIDTH), F32),
        scratch_shapes=[
            pltpu.VMEM((rows + SSM_T, LANES), F32),
            pltpu.VMEM((rows, LANES), F32),
            pltpu.VMEM((tot_chunks, 2 * SSM_PSTATE), F32),
            pltpu.VMEM((tot_chunks, 2 * SSM_PSTATE), F32),
            pltpu.VMEM((LANES, SSM_T * LANES), BF16),
            pltpu.VMEM((SSM_T * LANES, 2 * SSM_PSTATE), BF16),
            pltpu.VMEM((SSM_T, LANES, 2 * SSM_PSTATE), BF16),
        ],
        compiler_params=_params(("parallel",)),
        name="ssm",
    )(proj, kc, bc, cc, a_t, dskip)


def _ssm_operators(lam_re, lam_im, b_re, b_im, c_re, c_im, log_dt):
    p, h = SSM_STATE, SSM_GROUP
    dt = jnp.exp(log_dt.astype(F32))[:, None]
    lr, li = lam_re.astype(F32), lam_im.astype(F32)
    mag = jnp.exp(lr * dt)
    abar_re, abar_im = mag * jnp.cos(li * dt), mag * jnp.sin(li * dt)
    den = lr * lr + li * li
    nr = abar_re - 1.0
    f_re = (nr * lr + abar_im * li) / den
    f_im = (abar_im * lr - nr * li) / den
    br, bi = b_re.astype(F32), b_im.astype(F32)
    bbar_re = f_re[..., None] * br - f_im[..., None] * bi
    bbar_im = f_re[..., None] * bi + f_im[..., None] * br
    ks = jnp.arange(SSM_T + 1, dtype=F32)[:, None, None]
    pmag = jnp.exp(ks * (lr * dt)[None])
    pw_re = pmag * jnp.cos(ks * (li * dt)[None])
    pw_im = pmag * jnp.sin(ks * (li * dt)[None])
    cr, ci = c_re.astype(F32), c_im.astype(F32)

    packs, q = SSM_PACKS, SSM_PACK

    cb_re = cr[:, :, :, None] * bbar_re[:, None, :, :] - ci[:, :, :, None] * bbar_im[:, None, :, :]
    cb_im = cr[:, :, :, None] * bbar_im[:, None, :, :] + ci[:, :, :, None] * bbar_re[:, None, :, :]
    klag = (jnp.einsum('kgs,gosi->kgoi', pw_re[:SSM_T], cb_re)
            - jnp.einsum('kgs,gosi->kgoi', pw_im[:SSM_T], cb_im))
    kc = klag.reshape(SSM_T, packs, q, h, h).transpose(1, 3, 0, 2, 4).reshape(packs, h, SSM_T * LANES)

    rev_re, rev_im = pw_re[SSM_T - 1::-1, :, None, :], pw_im[SSM_T - 1::-1, :, None, :]
    bt_re, bt_im = jnp.swapaxes(bbar_re, 1, 2)[None], jnp.swapaxes(bbar_im, 1, 2)[None]

    def inj_by_pack(x):
        return x.reshape(SSM_T, packs, q * h, p).transpose(1, 0, 2, 3).reshape(packs, SSM_T * LANES, p)

    bc = jnp.stack([inj_by_pack(rev_re * bt_re - rev_im * bt_im), inj_by_pack(rev_re * bt_im + rev_im * bt_re)], axis=1)

    ca_re = cr[None] * pw_re[1:, :, None, :] - ci[None] * pw_im[1:, :, None, :]
    ca_im = cr[None] * pw_im[1:, :, None, :] + ci[None] * pw_re[1:, :, None, :]

    def out_by_pack(x):
        return x.reshape(SSM_T, packs, q, h, p).transpose(1, 0, 3, 2, 4).reshape(packs, SSM_T, h, SSM_PSTATE)

    cc = jnp.stack([out_by_pack(ca_re), out_by_pack(-ca_im)], axis=1)

    a_t = jnp.concatenate([pw_re[SSM_T].reshape(packs, 1, SSM_PSTATE),
                           pw_im[SSM_T].reshape(packs, 1, SSM_PSTATE)], axis=-1)
    return kc.astype(BF16), bc.astype(BF16), cc.astype(BF16), a_t


OUT_TN = 512


def _split_bf16(x):
    hi = x.astype(BF16)
    return hi, (x - hi.astype(F32)).astype(BF16)


def _route(hn, rwh_ref, rwl_ref, rb_ref):
    hi, lo = _split_bf16(hn)
    logits = (jnp.dot(hi, rwh_ref[...], preferred_element_type=F32)
              + (jnp.dot(lo, rwh_ref[...], preferred_element_type=F32)
                 + jnp.dot(hi, rwl_ref[...], preferred_element_type=F32))) + rb_ref[...]
    lane = lax.broadcasted_iota(jnp.int32, logits.shape, 1)

    def first_argmax(v, vmax):
        return jnp.min(jnp.where(v == vmax, lane, LANES), axis=-1, keepdims=True)

    gl = jnp.where(lane < N_EXPERT_GROUPS, logits, NEG)
    gmax = jnp.max(gl, axis=-1, keepdims=True)
    g_sel = first_argmax(gl, gmax)
    p_g = 1.0 / jnp.sum(jnp.exp(gl - gmax), axis=-1, keepdims=True)
    e_lane = lane - N_EXPERT_GROUPS
    in_group = (e_lane >= 0) & (e_lane < N_EXPERTS) & (jnp.right_shift(e_lane, 3) == g_sel)
    el = jnp.where(in_group, logits, NEG)
    m1 = jnp.max(el, axis=-1, keepdims=True)
    i1 = first_argmax(el, m1)
    el2 = jnp.where(lane == i1, NEG, el)
    m2 = jnp.max(el2, axis=-1, keepdims=True)
    i2 = first_argmax(el2, m2)
    z = jnp.sum(jnp.exp(el - m1), axis=-1, keepdims=True)
    p1 = 1.0 / z
    p2 = jnp.exp(m2 - m1) / z
    w1 = p1 / (p1 + p2) * p_g
    w2 = p2 / (p1 + p2) * p_g
    wts = jnp.where(lane == 0, w1, jnp.where(lane == 1, w2, 0.0))
    ids = jnp.where(lane == 0, i1 - N_EXPERT_GROUPS, jnp.where(lane == 1, i2 - N_EXPERT_GROUPS, 0))
    return wts, ids


def _assignment_ranks(ids, run_ref):
    tm = ids.shape[0]
    lane = lax.broadcasted_iota(jnp.int32, ids.shape, 1)
    oh0 = lane == ids[:, 0:1]
    oh1 = lane == ids[:, 1:2]
    hits = (oh0 | oh1).astype(BF16)
    r = lax.broadcasted_iota(jnp.int32, (tm, tm), 0)
    c = lax.broadcasted_iota(jnp.int32, (tm, tm), 1)
    before = jnp.dot((c < r).astype(BF16), hits, preferred_element_type=F32) + run_ref[...]
    rank0 = jnp.sum(jnp.where(oh0, before, 0.0), axis=-1, keepdims=True)
    rank1 = jnp.sum(jnp.where(oh1, before, 0.0), axis=-1, keepdims=True)
    run_ref[...] += jnp.sum(hits.astype(F32), axis=0, keepdims=True)
    return jnp.where(lane == 0, rank0, jnp.where(lane == 1, rank1, 0.0)).astype(jnp.int32)


def _mixer_out_kernel(h_ref, a_ref, y_ref, c_ref, wo_ref, wglu_ref, g_ref, rwh_ref, rwl_ref, rb_ref,
                      o_ref, hs_ref, wt_ref, id_ref, rank_ref, cnt_ref, run_ref):
    @pl.when(pl.program_id(0) == 0)
    def _():
        run_ref[...] = jnp.zeros(run_ref.shape, F32)

    y = y_ref[...]
    gate = jnp.dot(y.astype(BF16), wglu_ref[...], preferred_element_type=F32)
    b = (y * jax.nn.sigmoid(gate)).astype(BF16)
    mixed = jnp.concatenate([a_ref[...], b, c_ref[...]], axis=1)
    for j in range(D_MODEL // OUT_TN):
        cols = slice(j * OUT_TN, (j + 1) * OUT_TN)
        o_ref[:, cols] = h_ref[:, cols] + jnp.dot(mixed, wo_ref[:, cols], preferred_element_type=F32)
    x = o_ref[...]
    hn = _rms_norm(x, g_ref[...])
    _store_slabs(hs_ref, hn)
    wts, ids = _route(hn, rwh_ref, rwl_ref, rb_ref)
    wt_ref[...] = wts
    id_ref[...] = ids
    rank_ref[...] = _assignment_ranks(ids, run_ref)
    cnt_ref[...] = run_ref[...].astype(jnp.int32)


def _mixer_out(h, oa, yg, oc, wo, wglu, g, rw_hi, rw_lo, rb):
    rows = h.shape[0]
    tm = _row_tile(rows)

    def row_spec(width):
        return pl.BlockSpec((tm, width), lambda i: (i, 0))

    def whole(shape):
        return pl.BlockSpec(shape, lambda i: (0, 0))

    return pl.pallas_call(
        _mixer_out_kernel,
        grid=(rows // tm,),
        in_specs=[
            row_spec(D_MODEL), row_spec(SWA_WIDTH), row_spec(SSM_WIDTH), row_spec(DIFF_WIDTH),
            whole((D_MODEL, D_MODEL)), whole((SSM_WIDTH, SSM_WIDTH)), whole((1, D_MODEL)),
            whole((D_MODEL, LANES)), whole((D_MODEL, LANES)), whole((1, LANES)),
        ],
        out_specs=[row_spec(D_MODEL), pl.BlockSpec((tm * SLAB, LANES), lambda i: (i, 0)), row_spec(LANES),
                   row_spec(LANES), row_spec(LANES), whole((1, LANES))],
        out_shape=[
            jax.ShapeDtypeStruct((rows, D_MODEL), F32),
            jax.ShapeDtypeStruct((rows * SLAB, LANES), F32),
            jax.ShapeDtypeStruct((rows, LANES), F32),
            jax.ShapeDtypeStruct((rows, LANES), jnp.int32),
            jax.ShapeDtypeStruct((rows, LANES), jnp.int32),
            jax.ShapeDtypeStruct((1, LANES), jnp.int32),
        ],
        scratch_shapes=[pltpu.VMEM((1, LANES), F32)],
        compiler_params=_params(("arbitrary",)),
        name="mixer_out",
    )(h, oa, yg, oc, wo, wglu, g, rw_hi, rw_lo, rb)


def _dest_kernel(ids_ref, rank_ref, cnt_ref, dest_ref):
    shift = MOE_BM.bit_length() - 1
    cnt = jnp.broadcast_to(cnt_ref[...], (8, LANES))
    padded = jnp.left_shift(jnp.right_shift(cnt + (MOE_BM - 1), shift), shift)
    lane8 = lax.broadcasted_iota(jnp.int32, (8, LANES), 1)
    ends = padded
    k = 1
    while k < LANES:
        ends = ends + jnp.where(lane8 >= k, pltpu.roll(ends, k, 1), 0)
        k *= 2
    pstart = (ends - padded)[0:1].astype(F32)
    ids = ids_ref[...]
    lane = lax.broadcasted_iota(jnp.int32, ids.shape, 1)
    start0 = jnp.sum(jnp.where(lane == ids[:, 0:1], pstart, 0.0), axis=-1, keepdims=True)
    start1 = jnp.sum(jnp.where(lane == ids[:, 1:2], pstart, 0.0), axis=-1, keepdims=True)
    starts = jnp.where(lane == 0, start0, jnp.where(lane == 1, start1, 0.0)).astype(jnp.int32)
    dest_ref[...] = starts + rank_ref[...]


def _assignment_dest(ids, ranks, cnt):
    rows = ids.shape[0]
    tm = _row_tile(rows)
    row_spec = pl.BlockSpec((tm, LANES), lambda i: (i, 0))
    return pl.pallas_call(
        _dest_kernel,
        grid=(rows // tm,),
        in_specs=[row_spec, row_spec, pl.BlockSpec((1, LANES), lambda i: (0, 0))],
        out_specs=row_spec,
        out_shape=jax.ShapeDtypeStruct((rows, LANES), jnp.int32),
        compiler_params=_params(("parallel",)),
        name="moe_dest",
    )(ids, ranks, cnt)


def _dispatch_plan(ids, ranks, cnt, n_blk):
    dest = _assignment_dest(ids, ranks, cnt)[:, :2].reshape(-1)
    counts = cnt[0, :N_EXPERTS]
    padded = (counts + MOE_BM - 1) // MOE_BM * MOE_BM
    pend = jnp.cumsum(padded)
    blk_start = jnp.arange(n_blk, dtype=jnp.int32) * MOE_BM
    blk_expert = jnp.minimum(jnp.sum((pend[None, :] <= blk_start[:, None]).astype(jnp.int32), axis=1), N_EXPERTS - 1)
    n_used = (pend[-1] // MOE_BM).astype(jnp.int32).reshape(1)
    ex = jnp.arange(N_EXPERTS, dtype=jnp.int32)
    nonempty = counts > 0
    run_parity = (jnp.cumsum(nonempty.astype(jnp.int32)) - 1) & 1
    later = (ex[None, :] > ex[:, None]) & nonempty[None, :]
    nxt = jnp.min(jnp.where(later, ex[None, :], N_EXPERTS), axis=1)
    nxt = jnp.where(nxt == N_EXPERTS, -1, nxt)
    pad_lo = jnp.concatenate([pend - padded + counts, pend[-1:]])
    pad_hi = jnp.concatenate([pend, pend[-1:] + MOE_BM])
    plan = dict(dest=dest, blk_expert=blk_expert, n_used=n_used, blk_parity=run_parity[blk_expert].astype(jnp.int32),
                blk_next=nxt[blk_expert].astype(jnp.int32), pad_rows=jnp.concatenate([pad_lo, pad_hi]).astype(jnp.int32))
    return plan


SLAB = D_MODEL // LANES


def _store_slabs(ref, x):
    rows = x.shape[0]
    for s in range(SLAB):
        ref[pl.ds(s, rows, stride=SLAB), :] = x[:, s * LANES:(s + 1) * LANES]


def _load_slabs(ref, rows):
    return jnp.concatenate([ref[pl.ds(s, rows, stride=SLAB), :] for s in range(SLAB)], axis=1)


def _slab_gather_copy(src_hbm, dst_buf, sem, src_row, dst_row):
    src = src_hbm.at[pl.ds(pl.multiple_of(src_row * SLAB, SLAB), SLAB), :]
    return pltpu.make_async_copy(src, dst_buf.at[pl.ds(dst_row * SLAB, SLAB), :], sem)


def _block_wait_copy(src_hbm, dst_buf, sem):
    return pltpu.make_async_copy(src_hbm.at[pl.ds(0, dst_buf.shape[0]), :], dst_buf, sem)


GATHER_SLOTS = 3
WEIGHT_DMA_PRIORITY = 1


def _expert_kernel(be_ref, par_ref, nxt_ref, dest_ref, pad_ref, nused_ref, hn_hbm, wg_hbm, wu_hbm, wd_hbm, o_ref,
                   src_ref, xbuf, wgf, wuf, wdf, wgb, wub, wdb, gsem, wsem, *, layer):
    i = pl.program_id(0)
    n_used = nused_ref[0]
    slot = i % GATHER_SLOTS

    def start_gather(blk, slot_):
        for r in range(MOE_BM):
            _slab_gather_copy(hn_hbm, xbuf.at[slot_], gsem.at[slot_], src_ref[blk * MOE_BM + r], r).start()

    def weight_copies(expert, wslot):
        pairs = ((wg_hbm, wgf), (wu_hbm, wuf), (wd_hbm, wdf))
        return [pltpu.make_async_copy(w.at[layer, expert], buf.at[wslot], wsem.at[wslot]) for w, buf in pairs]

    @pl.when(i == 0)
    def _():
        for c in weight_copies(be_ref[0], 0):
            c.start(priority=WEIGHT_DMA_PRIORITY)

        def clear(j, carry):
            src_ref[j] = 0
            return carry

        def invert(t, carry):
            src_ref[dest_ref[t]] = jnp.right_shift(t, 1)
            return carry

        n_ranges = pad_ref.shape[0] // 2
        for e in range(n_ranges):
            lax.fori_loop(pad_ref[e], pad_ref[n_ranges + e], clear, 0)
        lax.fori_loop(0, dest_ref.shape[0], invert, 0, unroll=8)
        start_gather(0, 0)
        start_gather(jnp.minimum(1, n_used), 1)

    @pl.when(i <= n_used + 1)
    def _():
        _block_wait_copy(hn_hbm, xbuf.at[slot], gsem.at[slot]).wait()

    @pl.when(i < n_used)
    def _():
        @pl.when((i == 0) | (be_ref[i] != be_ref[jnp.maximum(i - 1, 0)]))
        def _():
            wslot = par_ref[i]
            for c in weight_copies(be_ref[i], wslot):
                c.wait()

            @pl.when(nxt_ref[i] >= 0)
            def _():
                for c in weight_copies(nxt_ref[i], 1 - wslot):
                    c.start(priority=WEIGHT_DMA_PRIORITY)

            wgb[...] = wgf[wslot].astype(BF16)
            wub[...] = wuf[wslot].astype(BF16)
            wdb[...] = wdf[wslot].astype(BF16)

        x = _load_slabs(xbuf.at[slot], MOE_BM).astype(BF16)
        start_gather(jnp.minimum(i + 2, n_used), (i + 2) % GATHER_SLOTS)
        gate = jnp.dot(x, wgb[...], preferred_element_type=F32)
        up = jnp.dot(x, wub[...], preferred_element_type=F32)
        act = (jax.nn.silu(gate) * up).astype(BF16)
        _store_slabs(o_ref, jnp.dot(act, wdb[...], preferred_element_type=F32))

    @pl.when(i >= n_used)
    def _():
        o_ref[...] = jnp.zeros(o_ref.shape, o_ref.dtype)


def _experts(hn_slabs, plan, wg, wu, wd, layer, n_blk):
    hbm = pl.BlockSpec(memory_space=pl.ANY)
    blk_rows = MOE_BM * SLAB
    grid_spec = pltpu.PrefetchScalarGridSpec(
        num_scalar_prefetch=6,
        grid=(n_blk,),
        in_specs=[hbm, hbm, hbm, hbm],
        out_specs=pl.BlockSpec((blk_rows, LANES), lambda i, *_: (i, 0)),
        scratch_shapes=[
            pltpu.SMEM((n_blk * MOE_BM,), jnp.int32),
            pltpu.VMEM((GATHER_SLOTS, blk_rows, LANES), F32),
            pltpu.VMEM((2, D_MODEL, D_EXPERT), F32),
            pltpu.VMEM((2, D_MODEL, D_EXPERT), F32),
            pltpu.VMEM((2, D_EXPERT, D_MODEL), F32),
            pltpu.VMEM((D_MODEL, D_EXPERT), BF16),
            pltpu.VMEM((D_MODEL, D_EXPERT), BF16),
            pltpu.VMEM((D_EXPERT, D_MODEL), BF16),
            pltpu.SemaphoreType.DMA((GATHER_SLOTS,)),
            pltpu.SemaphoreType.DMA((2,)),
        ],
    )
    return pl.pallas_call(
        functools.partial(_expert_kernel, layer=layer),
        grid_spec=grid_spec,
        out_shape=jax.ShapeDtypeStruct((n_blk * blk_rows, LANES), F32),
        compiler_params=_params(("arbitrary",)),
        name="moe_experts",
    )(plan["blk_expert"], plan["blk_parity"], plan["blk_next"], plan["dest"], plan["pad_rows"], plan["n_used"],
      hn_slabs, wg, wu, wd)


def _combine_kernel(dest_ref, h_ref, wt_ref, g_ref, y_hbm, o_ref, ybuf, sem, *, tile_of_step, final_norm):
    i = pl.program_id(0)
    n = pl.num_programs(0)
    slot = i % 2

    def start_gather(step, slot_):
        tile = tile_of_step(step)
        for r in range(BLK):
            for k in range(2):
                row = dest_ref[2 * (tile * BLK + r) + k]
                _slab_gather_copy(y_hbm, ybuf.at[slot_, k], sem.at[slot_], row, r).start(priority=k)

    @pl.when(i == 0)
    def _():
        start_gather(0, 0)

    @pl.when(i + 1 < n)
    def _():
        start_gather(i + 1, 1 - slot)

    for k in range(2):
        _block_wait_copy(y_hbm, ybuf.at[slot, k], sem.at[slot]).wait()
    wt = wt_ref[...]
    y0 = _load_slabs(ybuf.at[slot, 0], BLK)
    y1 = _load_slabs(ybuf.at[slot, 1], BLK)
    out = h_ref[...] + wt[:, 0:1] * y0 + wt[:, 1:2] * y1
    o_ref[...] = _rms_norm(out, g_ref[...]) if final_norm else out


def _combine(h, wts, y_slabs, dest, g, batch, nb, final_norm):
    if final_norm:
        per_batch = nb - 1
        n_steps = batch * per_batch

        def tile_of_step(i):
            return (i // per_batch) * nb + 1 + i % per_batch
    else:
        n_steps = batch * nb

        def tile_of_step(i):
            return i

    grid_spec = pltpu.PrefetchScalarGridSpec(
        num_scalar_prefetch=1,
        grid=(n_steps,),
        in_specs=[
            pl.BlockSpec((BLK, D_MODEL), lambda i, d: (tile_of_step(i), 0)),
            pl.BlockSpec((BLK, LANES), lambda i, d: (tile_of_step(i), 0)),
            pl.BlockSpec((1, D_MODEL), lambda i, d: (0, 0)),
            pl.BlockSpec(memory_space=pl.ANY),
        ],
        out_specs=pl.BlockSpec((BLK, D_MODEL), lambda i, d: (i, 0)),
        scratch_shapes=[pltpu.VMEM((2, 2, BLK * SLAB, LANES), F32), pltpu.SemaphoreType.DMA((2,))],
    )
    return pl.pallas_call(
        functools.partial(_combine_kernel, tile_of_step=tile_of_step, final_norm=final_norm),
        grid_spec=grid_spec,
        out_shape=jax.ShapeDtypeStruct((n_steps * BLK, D_MODEL), F32),
        compiler_params=_params(("arbitrary",)),
        name="moe_combine",
    )(dest, h, wts, g, y_slabs)


def _moe(h, hn_slabs, wts, ids, ranks, cnt, wg, wu, wd, layer, final_g, batch, nb):
    rows = h.shape[0]
    n_blk = -(-(2 * rows) // MOE_BM) + N_EXPERTS + 1
    plan = _dispatch_plan(ids, ranks, cnt, n_blk)
    y_slabs = _experts(hn_slabs, plan, wg, wu, wd, layer, n_blk)
    final_norm = final_g is not None
    g = final_g if final_norm else jnp.ones((1, D_MODEL), F32)
    return _combine(h, wts, y_slabs, plan["dest"], g, batch, nb, final_norm)


def _rope_tables(batch, lp):
    half = HEAD_DIM // 2
    inv = ROPE_THETA ** (-jnp.arange(half, dtype=F32) / half)
    pos = (jnp.arange(lp, dtype=jnp.int32) - PAD).astype(F32)
    ang = pos[:, None] * inv[None, :]
    cos = jnp.tile(jnp.cos(ang), (batch, LANES // half))
    sin = jnp.sin(ang)
    sin = jnp.tile(jnp.concatenate([-sin, sin], axis=1), (batch, LANES // HEAD_DIM))
    return cos, sin


def kernel(x, meta_tokens, attn_norm_g, ffn_norm_g, w_in, swa_sinks, ssm_lambda_re, ssm_lambda_im, ssm_b_re, ssm_b_im, ssm_c_re, ssm_c_im, ssm_d, ssm_log_dt, ssm_w_glu, diff_lambda_q1, diff_lambda_k1, diff_lambda_q2, diff_lambda_k2, diff_subln_g, w_out, router_group_w, router_group_b, router_expert_w, router_expert_b, moe_w_gate, moe_w_up, moe_w_down, final_norm_g):
    batch, seq, d = x.shape
    depth = w_in.shape[0]
    lp = PAD + N_META + seq
    nb = lp // BLK
    meta = jnp.broadcast_to(meta_tokens[None].astype(x.dtype), (batch, N_META, d))
    h = jnp.concatenate([jnp.zeros((batch, PAD, d), x.dtype), meta, x], axis=1).reshape(batch * lp, d)
    cos, sin = _rope_tables(batch, lp)
    for l in range(depth):
        proj = _norm_inproj(h, attn_norm_g[l][None], w_in[l].astype(BF16), cos, sin)
        o_a = _swa(proj, swa_sinks[l].astype(F32), batch, nb)
        kflat, bw, cw, a_t = _ssm_operators(ssm_lambda_re[l], ssm_lambda_im[l], ssm_b_re[l], ssm_b_im[l],
                                            ssm_c_re[l], ssm_c_im[l], ssm_log_dt[l])
        yg = _ssm(proj, kflat, bw, cw, a_t, ssm_d[l].astype(F32).reshape(SSM_PACKS, 1, LANES), batch, lp)
        lam_init = 0.8 - 0.6 * math.exp(-0.3 * l)
        lam_vecs = jnp.stack([diff_lambda_q1[l], diff_lambda_k1[l], diff_lambda_q2[l], diff_lambda_k2[l]]).astype(F32)
        o_c = _diff_attn(proj, lam_vecs, diff_subln_g[l][None].astype(F32), lam_init, batch, nb)
        wo = w_out[l].astype(BF16)
        rw = jnp.concatenate([router_group_w[l], router_expert_w[l]], axis=1).astype(F32)
        rw = jnp.pad(rw, ((0, 0), (0, LANES - rw.shape[1])))
        rb = jnp.concatenate([router_group_b[l], router_expert_b[l]]).astype(F32)
        rb = jnp.pad(rb, (0, LANES - rb.shape[0]))[None]
        rw_hi, rw_lo = _split_bf16(rw)
        g_ffn = ffn_norm_g[l][None].astype(F32)
        h, hn_slabs, wts, ids, ranks, cnt = _mixer_out(h, o_a, yg, o_c, wo, ssm_w_glu[l].astype(BF16), g_ffn, rw_hi,
                                                       rw_lo, rb)
        final_g = final_norm_g[None].astype(F32) if l == depth - 1 else None
        h = _moe(h, hn_slabs, wts, ids, ranks, cnt, moe_w_gate, moe_w_up, moe_w_down, l, final_g, batch, nb)
    return h.reshape(batch, seq, d)
```

```python
import functools
import math

import jax
import jax.numpy as jnp
from jax import lax
from jax.experimental import pallas as pl
from jax.experimental.pallas import tpu as pltpu

D_MODEL = 2048
N_META = 16
BLK = 128
PAD = BLK - N_META
ROPE_THETA = 10000.0
NORM_EPS = 1e-5
NEG = -1e30
SWA_HEADS = 12
SWA_KV_HEADS = 4
HEAD_DIM = 64
SWA_WIDTH = SWA_HEADS * HEAD_DIM
SSM_WIDTH = 768
SSM_GROUP = 16
SSM_GROUPS = SSM_WIDTH // SSM_GROUP
SSM_STATE = 64
DIFF_HEADS = 4
DIFF_V_DIM = 128
DIFF_WIDTH = DIFF_HEADS * DIFF_V_DIM
IN_WIDTH = 3584
N_EXPERT_GROUPS = 4
EXPERTS_PER_GROUP = 8
N_EXPERTS = N_EXPERT_GROUPS * EXPERTS_PER_GROUP
D_EXPERT = 512

COL_KA = 768
COL_VA = 1024
COL_US = 1280
COL_QD = 2048
COL_KD = 2560
COL_VD = 3072

LANES = 128
SSM_T = 16
SSM_PACK = LANES // SSM_GROUP
SSM_PACKS = SSM_GROUPS // SSM_PACK
SSM_PSTATE = SSM_PACK * SSM_STATE
MOE_BM = 256
VMEM_LIMIT = 56 * 1024 * 1024

F32 = jnp.float32
BF16 = jnp.bfloat16


def _row_tile(rows):
    for t in (512, 256, 128):
        if rows % t == 0:
            return t
    raise ValueError(f"row count {rows} is not a multiple of 128")


def _params(sem, vmem=VMEM_LIMIT):
    return pltpu.CompilerParams(dimension_semantics=sem, vmem_limit_bytes=vmem)


def _rms_norm(x, g):
    return x * lax.rsqrt(jnp.mean(x * x, axis=-1, keepdims=True) + NORM_EPS) * g


IN_TN = 512
ROPE_TILES = (0, 1, 4, 5)


def _norm_inproj_kernel(x_ref, g_ref, w_ref, cos_ref, sin_ref, o_ref, xn_ref):
    x = x_ref[...]
    ms = jnp.mean(x * x, axis=-1, keepdims=True)
    xn_ref[...] = (x * lax.rsqrt(ms + NORM_EPS) * g_ref[...]).astype(BF16)
    shape = (x.shape[0], IN_TN)
    lane = lax.broadcasted_iota(jnp.int32, shape, 1)
    first_half = (lane & (HEAD_DIM - 1)) < (HEAD_DIM // 2)
    reps = IN_TN // LANES
    cos = jnp.tile(cos_ref[...], (1, reps))
    sin = jnp.tile(sin_ref[...], (1, reps))
    for j in range(IN_WIDTH // IN_TN):
        cols = slice(j * IN_TN, (j + 1) * IN_TN)
        acc = jnp.dot(xn_ref[...], w_ref[:, cols], preferred_element_type=F32)
        if j in ROPE_TILES:
            partner = jnp.where(first_half, pltpu.roll(acc, IN_TN - HEAD_DIM // 2, 1),
                                pltpu.roll(acc, HEAD_DIM // 2, 1))
            acc = acc * cos + partner * sin
        o_ref[:, cols] = acc.astype(o_ref.dtype)


def _norm_inproj(h, g, w, cos, sin):
    rows = h.shape[0]
    tm = _row_tile(rows)
    return pl.pallas_call(
        _norm_inproj_kernel,
        grid=(rows // tm,),
        in_specs=[
            pl.BlockSpec((tm, D_MODEL), lambda i: (i, 0)),
            pl.BlockSpec((1, D_MODEL), lambda i: (0, 0)),
            pl.BlockSpec((D_MODEL, IN_WIDTH), lambda i: (0, 0)),
            pl.BlockSpec((tm, LANES), lambda i: (i, 0)),
            pl.BlockSpec((tm, LANES), lambda i: (i, 0)),
        ],
        out_specs=pl.BlockSpec((tm, IN_WIDTH), lambda i: (i, 0)),
        out_shape=jax.ShapeDtypeStruct((rows, IN_WIDTH), BF16),
        scratch_shapes=[pltpu.VMEM((tm, D_MODEL), BF16)],
        compiler_params=_params(("parallel",)),
        name="norm_inproj",
    )(h, g, w, cos, sin)


def _swa_kernel(sink_ref, q_ref, kp_ref, kc_ref, vp_ref, vc_ref, o_ref):
    n = pl.program_id(1)
    shape = (BLK, 2 * BLK)
    qi = lax.broadcasted_iota(jnp.int32, shape, 0)
    c = lax.broadcasted_iota(jnp.int32, shape, 1)
    kj = (n - 1) * BLK + c
    delta = qi + BLK - c
    ok = (delta >= 0) & (delta < BLK) & (kj >= PAD)
    grp = SWA_HEADS // SWA_KV_HEADS
    outs = []
    for j in range(SWA_KV_HEADS):
        cols = slice(j * HEAD_DIM, (j + 1) * HEAD_DIM)
        kk = jnp.concatenate([kp_ref[:, cols], kc_ref[:, cols]], axis=0)
        vv = jnp.concatenate([vp_ref[:, cols], vc_ref[:, cols]], axis=0)
        for g in range(grp):
            hd = j * grp + g
            qh = q_ref[:, hd * HEAD_DIM:(hd + 1) * HEAD_DIM]
            s = lax.dot_general(qh, kk, (((1,), (1,)), ((), ())), preferred_element_type=F32) * (HEAD_DIM ** -0.5)
            s = jnp.where(ok, s, NEG)
            sink = sink_ref[hd]
            m = jnp.maximum(jnp.max(s, axis=-1, keepdims=True), sink)
            p = jnp.exp(s - m)
            denom = jnp.sum(p, axis=-1, keepdims=True) + jnp.exp(sink - m)
            outs.append(jnp.dot(p.astype(BF16), vv, preferred_element_type=F32) / denom)
    o_ref[...] = jnp.concatenate(outs, axis=1).astype(o_ref.dtype)


def _swa(proj, sinks, batch, nb):
    rows = proj.shape[0]
    kvw = SWA_KV_HEADS * HEAD_DIM

    def cur(col):
        return lambda b, n: (b * nb + n, col)

    def prev(col):
        return lambda b, n: (b * nb + jnp.maximum(n - 1, 0), col)

    return pl.pallas_call(
        _swa_kernel,
        grid=(batch, nb),
        in_specs=[
            pl.BlockSpec(memory_space=pltpu.SMEM),
            pl.BlockSpec((BLK, SWA_WIDTH), cur(0)),
            pl.BlockSpec((BLK, kvw), prev(COL_KA // kvw)),
            pl.BlockSpec((BLK, kvw), cur(COL_KA // kvw)),
            pl.BlockSpec((BLK, kvw), prev(COL_VA // kvw)),
            pl.BlockSpec((BLK, kvw), cur(COL_VA // kvw)),
        ],
        out_specs=pl.BlockSpec((BLK, SWA_WIDTH), cur(0)),
        out_shape=jax.ShapeDtypeStruct((rows, SWA_WIDTH), BF16),
        compiler_params=_params(("parallel", "parallel")),
        name="swa",
    )(sinks, proj, proj, proj, proj, proj)


def _diff_kernel(lam_ref, g_ref, q_ref, k_ref, v_ref, o_ref, *, lam_init, nb):
    row = lax.broadcasted_iota(jnp.int32, (BLK, BLK), 0)
    col = lax.broadcasted_iota(jnp.int32, (BLK, BLK), 1)
    causal_ok = col <= row
    pad_ok = col >= PAD
    lane = lax.broadcasted_iota(jnp.int32, (BLK, LANES), 1)
    lv = lam_ref[...]
    lam = (jnp.exp(jnp.sum(lv[0:1] * lv[1:2], axis=-1, keepdims=True))
           - jnp.exp(jnp.sum(lv[2:3] * lv[3:4], axis=-1, keepdims=True)) + lam_init)
    gain = g_ref[...] * (1.0 - lam_init)
    scale = HEAD_DIM ** -0.5
    for n in range(nb):
        ke = (n + 1) * BLK
        q = q_ref[n * BLK:(n + 1) * BLK, :] * scale
        kx = k_ref[0:ke, :]
        vx = v_ref[0:ke, :]
        heads = []
        for s_idx in range(2):
            qx = jnp.where((lane >= HEAD_DIM) == bool(s_idx), q, jnp.zeros_like(q))
            s = lax.dot_general(qx, kx, (((1,), (1,)), ((), ())), preferred_element_type=F32)
            if n == 0:
                s = jnp.where(causal_ok & pad_ok, s, NEG)
            else:
                first = jnp.where(pad_ok, s[:, :BLK], NEG)
                last = jnp.where(causal_ok, s[:, ke - BLK:], NEG)
                middle = [s[:, BLK:ke - BLK]] if n > 1 else []
                s = jnp.concatenate([first] + middle + [last], axis=1)
            m = jnp.max(s, axis=-1, keepdims=True)
            p = jnp.exp(s - m)
            denom = jnp.sum(p, axis=-1, keepdims=True)
            heads.append(jnp.dot(p.astype(BF16), vx, preferred_element_type=F32) / denom)
        o = heads[0] - lam * heads[1]
        o = o * lax.rsqrt(jnp.mean(o * o, axis=-1, keepdims=True) + NORM_EPS) * gain
        o_ref[n * BLK:(n + 1) * BLK, :] = o.astype(o_ref.dtype)


def _diff_attn(proj, lam_vecs, subln_g, lam_init, batch, nb):
    rows = proj.shape[0]
    lp = nb * BLK
    return pl.pallas_call(
        functools.partial(_diff_kernel, lam_init=lam_init, nb=nb),
        grid=(batch, DIFF_HEADS),
        in_specs=[
            pl.BlockSpec((4, HEAD_DIM), lambda b, h: (0, 0)),
            pl.BlockSpec((1, DIFF_V_DIM), lambda b, h: (0, 0)),
            pl.BlockSpec((lp, LANES), lambda b, h: (b, COL_QD // LANES + h)),
            pl.BlockSpec((lp, LANES), lambda b, h: (b, COL_KD // LANES + h)),
            pl.BlockSpec((lp, LANES), lambda b, h: (b, COL_VD // LANES + h)),
        ],
        out_specs=pl.BlockSpec((lp, DIFF_V_DIM), lambda b, h: (b, h)),
        out_shape=jax.ShapeDtypeStruct((rows, DIFF_WIDTH), BF16),
        compiler_params=_params(("parallel", "parallel")),
        name="diff_attn",
    )(lam_vecs, subln_g, proj, proj, proj)


def _ssm_row_chunk(rows):
    for c in (544, 512, 384, 256, 128, 64, 32, 16):
        if rows % c == 0:
            return c
    raise ValueError(rows)


def _ssm_expand_operators(kc_ref, bc_ref, cc_ref, k_ref, bw_ref, cw_ref):
    def iota(shape, dim):
        return lax.broadcasted_iota(jnp.int32, shape, dim)

    def group(idx, width):
        return jnp.right_shift(idx, width.bit_length() - 1) & (SSM_PACK - 1)

    hs, ps, tl = SSM_GROUP, SSM_STATE, SSM_T * LANES
    spread_h = ((iota((LANES, hs), 0) & (hs - 1)) == iota((LANES, hs), 1)).astype(BF16)
    kt = jnp.dot(spread_h, kc_ref[...], preferred_element_type=F32)
    same = group(iota((LANES, tl), 0), hs) == group(iota((LANES, tl), 1), hs)
    k_ref[...] = jnp.where(same, kt, 0.0).astype(BF16)
    spread_s = (iota((ps, SSM_PSTATE), 0) == (iota((ps, SSM_PSTATE), 1) & (ps - 1))).astype(BF16)
    shape = (tl, SSM_PSTATE)
    same = group(iota(shape, 0), hs) == group(iota(shape, 1), ps)
    for part in range(2):
        bw = jnp.dot(bc_ref[part], spread_s, preferred_element_type=F32)
        bw_ref[:, part * SSM_PSTATE:(part + 1) * SSM_PSTATE] = jnp.where(same, bw, 0.0).astype(BF16)
    shape = (LANES, SSM_PSTATE)
    same = group(iota(shape, 0), hs) == group(iota(shape, 1), ps)
    for j in range(SSM_T):
        for part in range(2):
            ct = jnp.dot(spread_h, cc_ref[part, j], preferred_element_type=F32)
            cw_ref[j, :, part * SSM_PSTATE:(part + 1) * SSM_PSTATE] = jnp.where(same, ct, 0.0).astype(BF16)


def _ssm_kernel(u_ref, kc_ref, bc_ref, cc_ref, a_ref, d_ref, o_ref, up_ref, y_ref, e_ref, s_ref, k_ref, bw_ref,
                cw_ref, *, batch, lp):
    rows = batch * lp
    n_chunks = lp // SSM_T
    tot_chunks = rows // SSM_T
    _ssm_expand_operators(kc_ref, bc_ref, cc_ref, k_ref, bw_ref, cw_ref)
    contract_lanes = (((1,), (1,)), ((), ()))
    up_ref[0:SSM_T, :] = jnp.zeros((SSM_T, LANES), F32)
    up_ref[SSM_T:, :] = u_ref[...].astype(F32)
    for b in range(batch):
        up_ref[SSM_T + b * lp:SSM_T + b * lp + PAD, :] = jnp.zeros((PAD, LANES), F32)

    ch = _ssm_row_chunk(rows)
    rmod = lax.broadcasted_iota(jnp.int32, (ch, LANES), 0) & (SSM_T - 1)
    for r0 in range(0, rows, ch):
        pieces = []
        for k in range(SSM_T):
            sl = up_ref[SSM_T - k + r0:SSM_T - k + r0 + ch, :]
            pieces.append(jnp.where(rmod >= k, sl, 0.0).astype(BF16))
        y_ref[r0:r0 + ch, :] = lax.dot_general(jnp.concatenate(pieces, axis=1), k_ref[...], contract_lanes,
                                               preferred_element_type=F32)

    pieces = [up_ref[pl.ds(SSM_T + i, tot_chunks, stride=SSM_T), :].astype(BF16) for i in range(SSM_T)]
    e_ref[...] = jnp.dot(jnp.concatenate(pieces, axis=1), bw_ref[...], preferred_element_type=F32)

    a_re = a_ref[:, :SSM_PSTATE]
    a_im = a_ref[:, SSM_PSTATE:]

    def step(c, carry):
        new = []
        for b in range(batch):
            sr, si = carry[b]
            idx = b * n_chunks + c
            s_ref[pl.ds(idx, 1), :SSM_PSTATE] = sr
            s_ref[pl.ds(idx, 1), SSM_PSTATE:] = si
            er = e_ref[pl.ds(idx, 1), :SSM_PSTATE]
            ei = e_ref[pl.ds(idx, 1), SSM_PSTATE:]
            new.append((a_re * sr - a_im * si + er, a_re * si + a_im * sr + ei))
        return tuple(new)

    zero = jnp.zeros((1, SSM_PSTATE), F32)
    lax.fori_loop(0, n_chunks, step, tuple((zero, zero) for _ in range(batch)))

    sb = s_ref[...].astype(BF16)
    for j in range(SSM_T):
        z = lax.dot_general(sb, cw_ref[j], contract_lanes, preferred_element_type=F32)
        idx = pl.ds(j, tot_chunks, stride=SSM_T)
        y_ref[idx, :] = y_ref[idx, :] + z

    y = y_ref[...] + d_ref[...] * up_ref[SSM_T:, :]
    o_ref[...] = jax.nn.gelu(y, approximate=True).astype(o_ref.dtype)


def _ssm(proj, kc, bc, cc, a_t, dskip, batch, lp):
    rows = proj.shape[0]
    tot_chunks = rows // SSM_T
    return pl.pallas_call(
        functools.partial(_ssm_kernel, batch=batch, lp=lp),
        grid=(SSM_PACKS,),
        in_specs=[
            pl.BlockSpec((rows, LANES), lambda p: (0, COL_US // LANES + p)),
            pl.BlockSpec((None, SSM_GROUP, SSM_T * LANES), lambda p: (p, 0, 0)),
            pl.BlockSpec((None, 2, SSM_T * LANES, SSM_STATE), lambda p: (p, 0, 0, 0)),
            pl.BlockSpec((None, 2, SSM_T, SSM_GROUP, SSM_PSTATE), lambda p: (p, 0, 0, 0, 0)),
            pl.BlockSpec((None, 1, 2 * SSM_PSTATE), lambda p: (p, 0, 0)),
            pl.BlockSpec((None, 1, LANES), lambda p: (p, 0, 0)),
        ],
        out_specs=pl.BlockSpec((rows, LANES), lambda p: (0, p)),
        out_shape=jax.ShapeDtypeStruct((rows, SSM_WIDTH), F32),
        scratch_shapes=[
            pltpu.VMEM((rows + SSM_T, LANES), F32),
            pltpu.VMEM((rows, LANES), F32),
            pltpu.VMEM((tot_chunks, 2 * SSM_PSTATE), F32),
            pltpu.VMEM((tot_chunks, 2 * SSM_PSTATE), F32),
            pltpu.VMEM((LANES, SSM_T * LANES), BF16),
            pltpu.VMEM((SSM_T * LANES, 2 * SSM_PSTATE), BF16),
            pltpu.VMEM((SSM_T, LANES, 2 * SSM_PSTATE), BF16),
        ],
        compiler_params=_params(("parallel",)),
        name="ssm",
    )(proj, kc, bc, cc, a_t, dskip)


def _ssm_operators(lam_re, lam_im, b_re, b_im, c_re, c_im, log_dt):
    p, h = SSM_STATE, SSM_GROUP
    dt = jnp.exp(log_dt.astype(F32))[:, None]
    lr, li = lam_re.astype(F32), lam_im.astype(F32)
    mag = jnp.exp(lr * dt)
    abar_re, abar_im = mag * jnp.cos(li * dt), mag * jnp.sin(li * dt)
    den = lr * lr + li * li
    nr = abar_re - 1.0
    f_re = (nr * lr + abar_im * li) / den
    f_im = (abar_im * lr - nr * li) / den
    br, bi = b_re.astype(F32), b_im.astype(F32)
    bbar_re = f_re[..., None] * br - f_im[..., None] * bi
    bbar_im = f_re[..., None] * bi + f_im[..., None] * br
    ks = jnp.arange(SSM_T + 1, dtype=F32)[:, None, None]
    pmag = jnp.exp(ks * (lr * dt)[None])
    pw_re = pmag * jnp.cos(ks * (li * dt)[None])
    pw_im = pmag * jnp.sin(ks * (li * dt)[None])
    cr, ci = c_re.astype(F32), c_im.astype(F32)

    packs, q = SSM_PACKS, SSM_PACK

    cb_re = cr[:, :, :, None] * bbar_re[:, None, :, :] - ci[:, :, :, None] * bbar_im[:, None, :, :]
    cb_im = cr[:, :, :, None] * bbar_im[:, None, :, :] + ci[:, :, :, None] * bbar_re[:, None, :, :]
    klag = (jnp.einsum('kgs,gosi->kgoi', pw_re[:SSM_T], cb_re)
            - jnp.einsum('kgs,gosi->kgoi', pw_im[:SSM_T], cb_im))
    kc = klag.reshape(SSM_T, packs, q, h, h).transpose(1, 3, 0, 2, 4).reshape(packs, h, SSM_T * LANES)

    rev_re, rev_im = pw_re[SSM_T - 1::-1, :, None, :], pw_im[SSM_T - 1::-1, :, None, :]
    bt_re, bt_im = jnp.swapaxes(bbar_re, 1, 2)[None], jnp.swapaxes(bbar_im, 1, 2)[None]

    def inj_by_pack(x):
        return x.reshape(SSM_T, packs, q * h, p).transpose(1, 0, 2, 3).reshape(packs, SSM_T * LANES, p)

    bc = jnp.stack([inj_by_pack(rev_re * bt_re - rev_im * bt_im), inj_by_pack(rev_re * bt_im + rev_im * bt_re)], axis=1)

    ca_re = cr[None] * pw_re[1:, :, None, :] - ci[None] * pw_im[1:, :, None, :]
    ca_im = cr[None] * pw_im[1:, :, None, :] + ci[None] * pw_re[1:, :, None, :]

    def out_by_pack(x):
        return x.reshape(SSM_T, packs, q, h, p).transpose(1, 0, 3, 2, 4).reshape(packs, SSM_T, h, SSM_PSTATE)

    cc = jnp.stack([out_by_pack(ca_re), out_by_pack(-ca_im)], axis=1)

    a_t = jnp.concatenate([pw_re[SSM_T].reshape(packs, 1, SSM_PSTATE),
                           pw_im[SSM_T].reshape(packs, 1, SSM_PSTATE)], axis=-1)
    return kc.astype(BF16), bc.astype(BF16), cc.astype(BF16), a_t


OUT_TN = 512


def _split_bf16(x):
    hi = x.astype(BF16)
    return hi, (x - hi.astype(F32)).astype(BF16)


def _route(hn, rwh_ref, rwl_ref, rb_ref):
    hi, lo = _split_bf16(hn)
    logits = (jnp.dot(hi, rwh_ref[...], preferred_element_type=F32)
              + (jnp.dot(lo, rwh_ref[...], preferred_element_type=F32)
                 + jnp.dot(hi, rwl_ref[...], preferred_element_type=F32))) + rb_ref[...]
    lane = lax.broadcasted_iota(jnp.int32, logits.shape, 1)

    def first_argmax(v, vmax):
        return jnp.min(jnp.where(v == vmax, lane, LANES), axis=-1, keepdims=True)

    gl = jnp.where(lane < N_EXPERT_GROUPS, logits, NEG)
    gmax = jnp.max(gl, axis=-1, keepdims=True)
    g_sel = first_argmax(gl, gmax)
    p_g = 1.0 / jnp.sum(jnp.exp(gl - gmax), axis=-1, keepdims=True)
    e_lane = lane - N_EXPERT_GROUPS
    in_group = (e_lane >= 0) & (e_lane < N_EXPERTS) & (jnp.right_shift(e_lane, 3) == g_sel)
    el = jnp.where(in_group, logits, NEG)
    m1 = jnp.max(el, axis=-1, keepdims=True)
    i1 = first_argmax(el, m1)
    el2 = jnp.where(lane == i1, NEG, el)
    m2 = jnp.max(el2, axis=-1, keepdims=True)
    i2 = first_argmax(el2, m2)
    z = jnp.sum(jnp.exp(el - m1), axis=-1, keepdims=True)
    p1 = 1.0 / z
    p2 = jnp.exp(m2 - m1) / z
    w1 = p1 / (p1 + p2) * p_g
    w2 = p2 / (p1 + p2) * p_g
    wts = jnp.where(lane == 0, w1, jnp.where(lane == 1, w2, 0.0))
    ids = jnp.where(lane == 0, i1 - N_EXPERT_GROUPS, jnp.where(lane == 1, i2 - N_EXPERT_GROUPS, 0))
    return wts, ids


def _assignment_ranks(ids):
    tm = ids.shape[0]
    lane = lax.broadcasted_iota(jnp.int32, ids.shape, 1)
    oh0 = lane == ids[:, 0:1]
    oh1 = lane == ids[:, 1:2]
    hits = (oh0 | oh1).astype(BF16)
    r = lax.broadcasted_iota(jnp.int32, (tm, tm), 0)
    c = lax.broadcasted_iota(jnp.int32, (tm, tm), 1)
    before = jnp.dot((c < r).astype(BF16), hits, preferred_element_type=F32)
    rank0 = jnp.sum(jnp.where(oh0, before, 0.0), axis=-1, keepdims=True)
    rank1 = jnp.sum(jnp.where(oh1, before, 0.0), axis=-1, keepdims=True)
    ranks = jnp.where(lane == 0, rank0, jnp.where(lane == 1, rank1, 0.0)).astype(jnp.int32)
    return ranks, jnp.sum(hits.astype(F32), axis=0, keepdims=True).astype(jnp.int32)


def _mixer_out_kernel(h_ref, a_ref, y_ref, c_ref, wo_ref, wglu_ref, g_ref, rwh_ref, rwl_ref, rb_ref,
                      o_ref, hn_ref, wt_ref, id_ref, rank_ref, cnt_ref):
    y = y_ref[...]
    gate = jnp.dot(y.astype(BF16), wglu_ref[...], preferred_element_type=F32)
    b = (y * jax.nn.sigmoid(gate)).astype(BF16)
    mixed = jnp.concatenate([a_ref[...], b, c_ref[...]], axis=1)
    for j in range(D_MODEL // OUT_TN):
        cols = slice(j * OUT_TN, (j + 1) * OUT_TN)
        o_ref[:, cols] = h_ref[:, cols] + jnp.dot(mixed, wo_ref[:, cols], preferred_element_type=F32)
    x = o_ref[...]
    hn = _rms_norm(x, g_ref[...])
    hn_ref[...] = hn.astype(BF16)
    wts, ids = _route(hn, rwh_ref, rwl_ref, rb_ref)
    wt_ref[...] = wts
    id_ref[...] = ids
    ranks, counts = _assignment_ranks(ids)
    rank_ref[...] = ranks
    cnt_ref[...] = jnp.broadcast_to(counts, cnt_ref.shape)


def _mixer_out(h, oa, yg, oc, wo, wglu, g, rw_hi, rw_lo, rb):
    rows = h.shape[0]
    tm = _row_tile(rows)

    def row_spec(width):
        return pl.BlockSpec((tm, width), lambda i: (i, 0))

    def whole(shape):
        return pl.BlockSpec(shape, lambda i: (0, 0))

    return pl.pallas_call(
        _mixer_out_kernel,
        grid=(rows // tm,),
        in_specs=[
            row_spec(D_MODEL), row_spec(SWA_WIDTH), row_spec(SSM_WIDTH), row_spec(DIFF_WIDTH),
            whole((D_MODEL, D_MODEL)), whole((SSM_WIDTH, SSM_WIDTH)), whole((1, D_MODEL)),
            whole((D_MODEL, LANES)), whole((D_MODEL, LANES)), whole((1, LANES)),
        ],
        out_specs=[row_spec(D_MODEL), row_spec(D_MODEL), row_spec(LANES), row_spec(LANES), row_spec(LANES),
                   pl.BlockSpec((None, 8, LANES), lambda i: (i, 0, 0))],
        out_shape=[
            jax.ShapeDtypeStruct((rows, D_MODEL), F32),
            jax.ShapeDtypeStruct((rows, D_MODEL), BF16),
            jax.ShapeDtypeStruct((rows, LANES), F32),
            jax.ShapeDtypeStruct((rows, LANES), jnp.int32),
            jax.ShapeDtypeStruct((rows, LANES), jnp.int32),
            jax.ShapeDtypeStruct((rows // tm, 8, LANES), jnp.int32),
        ],
        compiler_params=_params(("parallel",)),
        name="mixer_out",
    )(h, oa, yg, oc, wo, wglu, g, rw_hi, rw_lo, rb)


MOE_UNIT = 8


def _moe_tile_rows(tt):
    return 2 * tt + N_EXPERTS * MOE_UNIT


def _moe_tables(tile_cnt, n_blk):
    cnt = tile_cnt[:, 0, :N_EXPERTS]
    seg = (cnt + MOE_UNIT - 1) // MOE_UNIT * MOE_UNIT
    size = jnp.sum(seg, axis=0)
    padded = (size + MOE_BM - 1) // MOE_BM * MOE_BM
    pend = jnp.cumsum(padded)
    pstart = pend - padded
    base = pstart[None, :] + jnp.cumsum(seg, axis=0) - seg
    loc = jnp.cumsum(seg, axis=1) - seg
    units = seg // MOE_UNIT
    blk_start = jnp.arange(n_blk, dtype=jnp.int32) * MOE_BM
    blk_expert = jnp.minimum(jnp.sum((pend[None, :] <= blk_start[:, None]).astype(jnp.int32), axis=1), N_EXPERTS - 1)
    ex = jnp.arange(N_EXPERTS, dtype=jnp.int32)
    nonempty = size > 0
    run_parity = (jnp.cumsum(nonempty.astype(jnp.int32)) - 1) & 1
    later = (ex[None, :] > ex[:, None]) & nonempty[None, :]
    nxt = jnp.min(jnp.where(later, ex[None, :], N_EXPERTS), axis=1)
    nxt = jnp.where(nxt == N_EXPERTS, -1, nxt)
    i32 = jnp.int32
    loc_lanes = jnp.pad(loc, ((0, 0), (0, LANES - N_EXPERTS)))
    return dict(
        base=base.reshape(-1).astype(i32), loc=loc.reshape(-1).astype(i32), units=units.reshape(-1).astype(i32),
        tile_units=jnp.sum(units, axis=1).astype(i32),
        tail_start=(pstart + size).astype(i32), tail_units=((padded - size) // MOE_UNIT).astype(i32),
        loc_lanes=jnp.broadcast_to(loc_lanes[:, None, :], (loc.shape[0], 8, LANES)).astype(i32),
        blk_expert=blk_expert.astype(i32), blk_parity=run_parity[blk_expert].astype(i32),
        blk_next=nxt[blk_expert].astype(i32), n_used=(pend[-1] // MOE_BM).astype(i32).reshape(1))


def _segment_copies(tile, hbm, buf, sem, to_hbm, base_ref, loc_ref, units_ref):
    for e in range(N_EXPERTS):
        k = tile * N_EXPERTS + e
        row_hbm, row_buf = base_ref[k], loc_ref[k]

        def start(u, carry):
            a = hbm.at[pl.ds(pl.multiple_of(row_hbm + u * MOE_UNIT, MOE_UNIT), MOE_UNIT), :]
            b = buf.at[pl.ds(pl.multiple_of(row_buf + u * MOE_UNIT, MOE_UNIT), MOE_UNIT), :]
            (pltpu.make_async_copy(b, a, sem) if to_hbm else pltpu.make_async_copy(a, b, sem)).start()
            return carry

        lax.fori_loop(0, units_ref[k], start, 0)


def _wait_unit_copies(count, hbm, buf, sem):
    def wait(u, carry):
        pltpu.make_async_copy(hbm.at[pl.ds(0, MOE_UNIT), :], buf.at[pl.ds(0, MOE_UNIT), :], sem).wait()
        return carry

    lax.fori_loop(0, count, wait, 0)


def _local_rows(ids, ranks, loc_lanes):
    lane = lax.broadcasted_iota(jnp.int32, ids.shape, 1)
    loc = loc_lanes.astype(F32)
    start0 = jnp.sum(jnp.where(lane == ids[:, 0:1], loc, 0.0), axis=-1, keepdims=True).astype(jnp.int32)
    start1 = jnp.sum(jnp.where(lane == ids[:, 1:2], loc, 0.0), axis=-1, keepdims=True).astype(jnp.int32)
    return start0 + ranks[:, 0:1], start1 + ranks[:, 1:2]


def _dispatch_kernel(base_ref, loc_ref, units_ref, tile_units_ref, tail_start_ref, tail_units_ref,
                     hn_ref, ids_ref, rank_ref, locv_ref, x_hbm, lrow_ref, ybuf, zbuf, sem):
    j = pl.program_id(0)
    n = pl.num_programs(0)
    slot = j % 2
    tt, rt = hn_ref.shape[0], ybuf.shape[1]

    @pl.when(j >= 2)
    def _():
        _wait_unit_copies(tile_units_ref[j - 2], x_hbm, ybuf.at[slot], sem.at[slot])

    row0, row1 = _local_rows(ids_ref[...], rank_ref[...], locv_ref[0:1, :])
    lane = lax.broadcasted_iota(jnp.int32, lrow_ref.shape, 1)
    lrow_ref[...] = jnp.where(lane == 0, row0, jnp.where(lane == 1, row1, 0))
    col = lax.broadcasted_iota(jnp.int32, (tt, rt), 1)
    onehot = ((col == row0) | (col == row1)).astype(BF16)
    ybuf[slot] = lax.dot_general(onehot, hn_ref[...], (((0,), (0,)), ((), ())), preferred_element_type=F32)
    _segment_copies(j, x_hbm, ybuf.at[slot], sem.at[slot], True, base_ref, loc_ref, units_ref)

    @pl.when(j == n - 1)
    def _():
        zbuf[...] = jnp.zeros(zbuf.shape, F32)
        total = 0
        for e in range(N_EXPERTS):
            row = tail_start_ref[e]

            def start(u, carry):
                dst = x_hbm.at[pl.ds(pl.multiple_of(row + u * MOE_UNIT, MOE_UNIT), MOE_UNIT), :]
                pltpu.make_async_copy(zbuf, dst, sem.at[2]).start()
                return carry

            lax.fori_loop(0, tail_units_ref[e], start, 0)
            total = total + tail_units_ref[e]
        _wait_unit_copies(total, x_hbm, zbuf, sem.at[2])

        @pl.when(j >= 1)
        def _():
            _wait_unit_copies(tile_units_ref[jnp.maximum(j - 1, 0)], x_hbm, ybuf.at[1 - slot], sem.at[1 - slot])

        _wait_unit_copies(tile_units_ref[j], x_hbm, ybuf.at[slot], sem.at[slot])


def _dispatch(hn, ids, ranks, tables, tt, n_rows):
    rows = hn.shape[0]
    rt = _moe_tile_rows(tt)
    row_spec = pl.BlockSpec((tt, LANES), lambda j, *_: (j, 0))
    grid_spec = pltpu.PrefetchScalarGridSpec(
        num_scalar_prefetch=6,
        grid=(rows // tt,),
        in_specs=[pl.BlockSpec((tt, D_MODEL), lambda j, *_: (j, 0)), row_spec, row_spec,
                  pl.BlockSpec((None, 8, LANES), lambda j, *_: (j, 0, 0))],
        out_specs=[pl.BlockSpec(memory_space=pl.ANY), row_spec],
        scratch_shapes=[pltpu.VMEM((2, rt, D_MODEL), F32), pltpu.VMEM((MOE_UNIT, D_MODEL), F32),
                        pltpu.SemaphoreType.DMA((3,))],
    )
    return pl.pallas_call(
        _dispatch_kernel,
        grid_spec=grid_spec,
        out_shape=[jax.ShapeDtypeStruct((n_rows, D_MODEL), F32),
                   jax.ShapeDtypeStruct((rows, LANES), jnp.int32)],
        compiler_params=_params(("arbitrary",)),
        name="moe_dispatch",
    )(tables["base"], tables["loc"], tables["units"], tables["tile_units"], tables["tail_start"],
      tables["tail_units"], hn, ids, ranks, tables["loc_lanes"])


WEIGHT_DMA_PRIORITY = 1


def _expert_kernel(be_ref, par_ref, nxt_ref, nused_ref, x_ref, wg_hbm, wu_hbm, wd_hbm, o_ref,
                   wgf, wuf, wdf, wgb, wub, wdb, wsem, *, layer):
    i = pl.program_id(0)
    n_used = nused_ref[0]

    def weight_copies(expert, wslot):
        pairs = ((wg_hbm, wgf), (wu_hbm, wuf), (wd_hbm, wdf))
        return [pltpu.make_async_copy(w.at[layer, expert], buf.at[wslot], wsem.at[wslot]) for w, buf in pairs]

    @pl.when(i == 0)
    def _():
        for c in weight_copies(be_ref[0], 0):
            c.start(priority=WEIGHT_DMA_PRIORITY)

    @pl.when(i < n_used)
    def _():
        @pl.when((i == 0) | (be_ref[i] != be_ref[jnp.maximum(i - 1, 0)]))
        def _():
            wslot = par_ref[i]
            for c in weight_copies(be_ref[i], wslot):
                c.wait()

            @pl.when(nxt_ref[i] >= 0)
            def _():
                for c in weight_copies(nxt_ref[i], 1 - wslot):
                    c.start(priority=WEIGHT_DMA_PRIORITY)

            wgb[...] = wgf[wslot].astype(BF16)
            wub[...] = wuf[wslot].astype(BF16)
            wdb[...] = wdf[wslot].astype(BF16)

        x = x_ref[...].astype(BF16)
        gate = jnp.dot(x, wgb[...], preferred_element_type=F32)
        up = jnp.dot(x, wub[...], preferred_element_type=F32)
        act = (jax.nn.silu(gate) * up).astype(BF16)
        o_ref[...] = jnp.dot(act, wdb[...], preferred_element_type=F32)

    @pl.when(i >= n_used)
    def _():
        o_ref[...] = jnp.zeros(o_ref.shape, o_ref.dtype)


def _experts(x_sorted, tables, wg, wu, wd, layer, n_blk):
    hbm = pl.BlockSpec(memory_space=pl.ANY)
    grid_spec = pltpu.PrefetchScalarGridSpec(
        num_scalar_prefetch=4,
        grid=(n_blk,),
        in_specs=[pl.BlockSpec((MOE_BM, D_MODEL), lambda i, be, par, nxt, nu: (jnp.minimum(i, nu[0] - 1), 0)),
                  hbm, hbm, hbm],
        out_specs=pl.BlockSpec((MOE_BM, D_MODEL), lambda i, *_: (i, 0)),
        scratch_shapes=[
            pltpu.VMEM((2, D_MODEL, D_EXPERT), F32),
            pltpu.VMEM((2, D_MODEL, D_EXPERT), F32),
            pltpu.VMEM((2, D_EXPERT, D_MODEL), F32),
            pltpu.VMEM((D_MODEL, D_EXPERT), BF16),
            pltpu.VMEM((D_MODEL, D_EXPERT), BF16),
            pltpu.VMEM((D_EXPERT, D_MODEL), BF16),
            pltpu.SemaphoreType.DMA((2,)),
        ],
    )
    return pl.pallas_call(
        functools.partial(_expert_kernel, layer=layer),
        grid_spec=grid_spec,
        out_shape=jax.ShapeDtypeStruct((n_blk * MOE_BM, D_MODEL), F32),
        compiler_params=_params(("arbitrary",)),
        name="moe_experts",
    )(tables["blk_expert"], tables["blk_parity"], tables["blk_next"], tables["n_used"], x_sorted, wg, wu, wd)


def _combine_kernel(base_ref, loc_ref, units_ref, tile_units_ref, h_ref, wt_ref, lrow_ref, g_ref, y_hbm, o_ref,
                    ybuf, sem, *, final_norm):
    j = pl.program_id(0)
    n = pl.num_programs(0)
    slot = j % 2
    tt, rt = h_ref.shape[0], ybuf.shape[1]

    @pl.when(j == 0)
    def _():
        ybuf[...] = jnp.zeros(ybuf.shape, F32)
        _segment_copies(0, y_hbm, ybuf.at[0], sem.at[0], False, base_ref, loc_ref, units_ref)

    @pl.when(j + 1 < n)
    def _():
        _segment_copies(j + 1, y_hbm, ybuf.at[1 - slot], sem.at[1 - slot], False, base_ref, loc_ref, units_ref)

    _wait_unit_copies(tile_units_ref[j], y_hbm, ybuf.at[slot], sem.at[slot])
    lrow = lrow_ref[...]
    wt = wt_ref[...]
    col = lax.broadcasted_iota(jnp.int32, (tt, rt), 1)
    pick0 = (col == lrow[:, 0:1]).astype(BF16)
    pick1 = (col == lrow[:, 1:2]).astype(BF16)
    for c in range(D_MODEL // OUT_TN):
        cols = slice(c * OUT_TN, (c + 1) * OUT_TN)
        y = ybuf[slot, :, cols].astype(BF16)
        y0 = jnp.dot(pick0, y, preferred_element_type=F32)
        y1 = jnp.dot(pick1, y, preferred_element_type=F32)
        o_ref[:, cols] = h_ref[:, cols] + wt[:, 0:1] * y0 + wt[:, 1:2] * y1
    if final_norm:
        o_ref[...] = _rms_norm(o_ref[...], g_ref[...])


def _combine(h, wts, lrows, y_sorted, tables, g, tt, final_norm):
    rows = h.shape[0]
    rt = _moe_tile_rows(tt)
    grid_spec = pltpu.PrefetchScalarGridSpec(
        num_scalar_prefetch=4,
        grid=(rows // tt,),
        in_specs=[
            pl.BlockSpec((tt, D_MODEL), lambda j, *_: (j, 0)),
            pl.BlockSpec((tt, LANES), lambda j, *_: (j, 0)),
            pl.BlockSpec((tt, LANES), lambda j, *_: (j, 0)),
            pl.BlockSpec((1, D_MODEL), lambda j, *_: (0, 0)),
            pl.BlockSpec(memory_space=pl.ANY),
        ],
        out_specs=pl.BlockSpec((tt, D_MODEL), lambda j, *_: (j, 0)),
        scratch_shapes=[pltpu.VMEM((2, rt, D_MODEL), F32), pltpu.SemaphoreType.DMA((2,))],
    )
    return pl.pallas_call(
        functools.partial(_combine_kernel, final_norm=final_norm),
        grid_spec=grid_spec,
        out_shape=jax.ShapeDtypeStruct((rows, D_MODEL), F32),
        compiler_params=_params(("arbitrary",)),
        name="moe_combine",
    )(tables["base"], tables["loc"], tables["units"], tables["tile_units"], h, wts, lrows, g, y_sorted)


def _moe(h, hn, wts, ids, ranks, tile_cnt, wg, wu, wd, layer, final_g):
    rows = h.shape[0]
    n_tiles = tile_cnt.shape[0]
    tt = rows // n_tiles
    n_rows = 2 * rows + n_tiles * N_EXPERTS * (MOE_UNIT - 1) + N_EXPERTS * (MOE_BM - 1)
    n_blk = -(-n_rows // MOE_BM)
    tables = _moe_tables(tile_cnt, n_blk)
    x_sorted, lrows = _dispatch(hn, ids, ranks, tables, tt, n_blk * MOE_BM)
    y_sorted = _experts(x_sorted, tables, wg, wu, wd, layer, n_blk)
    final_norm = final_g is not None
    g = final_g if final_norm else jnp.ones((1, D_MODEL), F32)
    return _combine(h, wts, lrows, y_sorted, tables, g, tt, final_norm)


def _rope_tables(batch, lp):
    half = HEAD_DIM // 2
    inv = ROPE_THETA ** (-jnp.arange(half, dtype=F32) / half)
    pos = (jnp.arange(lp, dtype=jnp.int32) - PAD).astype(F32)
    ang = pos[:, None] * inv[None, :]
    cos = jnp.tile(jnp.cos(ang), (batch, LANES // half))
    sin = jnp.sin(ang)
    sin = jnp.tile(jnp.concatenate([-sin, sin], axis=1), (batch, LANES // HEAD_DIM))
    return cos, sin


def kernel(x, meta_tokens, attn_norm_g, ffn_norm_g, w_in, swa_sinks, ssm_lambda_re, ssm_lambda_im, ssm_b_re, ssm_b_im, ssm_c_re, ssm_c_im, ssm_d, ssm_log_dt, ssm_w_glu, diff_lambda_q1, diff_lambda_k1, diff_lambda_q2, diff_lambda_k2, diff_subln_g, w_out, router_group_w, router_group_b, router_expert_w, router_expert_b, moe_w_gate, moe_w_up, moe_w_down, final_norm_g):
    batch, seq, d = x.shape
    depth = w_in.shape[0]
    lp = PAD + N_META + seq
    nb = lp // BLK
    meta = jnp.broadcast_to(meta_tokens[None].astype(x.dtype), (batch, N_META, d))
    h = jnp.concatenate([jnp.zeros((batch, PAD, d), x.dtype), meta, x], axis=1).reshape(batch * lp, d)
    cos, sin = _rope_tables(batch, lp)
    for l in range(depth):
        proj = _norm_inproj(h, attn_norm_g[l][None], w_in[l].astype(BF16), cos, sin)
        o_a = _swa(proj, swa_sinks[l].astype(F32), batch, nb)
        kflat, bw, cw, a_t = _ssm_operators(ssm_lambda_re[l], ssm_lambda_im[l], ssm_b_re[l], ssm_b_im[l],
                                            ssm_c_re[l], ssm_c_im[l], ssm_log_dt[l])
        yg = _ssm(proj, kflat, bw, cw, a_t, ssm_d[l].astype(F32).reshape(SSM_PACKS, 1, LANES), batch, lp)
        lam_init = 0.8 - 0.6 * math.exp(-0.3 * l)
        lam_vecs = jnp.stack([diff_lambda_q1[l], diff_lambda_k1[l], diff_lambda_q2[l], diff_lambda_k2[l]]).astype(F32)
        o_c = _diff_attn(proj, lam_vecs, diff_subln_g[l][None].astype(F32), lam_init, batch, nb)
        wo = w_out[l].astype(BF16)
        rw = jnp.concatenate([router_group_w[l], router_expert_w[l]], axis=1).astype(F32)
        rw = jnp.pad(rw, ((0, 0), (0, LANES - rw.shape[1])))
        rb = jnp.concatenate([router_group_b[l], router_expert_b[l]]).astype(F32)
        rb = jnp.pad(rb, (0, LANES - rb.shape[0]))[None]
        rw_hi, rw_lo = _split_bf16(rw)
        g_ffn = ffn_norm_g[l][None].astype(F32)
        h, hn, wts, ids, ranks, tile_cnt = _mixer_out(h, o_a, yg, o_c, wo, ssm_w_glu[l].astype(BF16), g_ffn, rw_hi,
                                                      rw_lo, rb)
        final_g = final_norm_g[None].astype(F32) if l == depth - 1 else None
        h = _moe(h, hn, wts, ids, ranks, tile_cnt, moe_w_gate, moe_w_up, moe_w_down, l, final_g)
    return h.reshape(batch, lp, d)[:, PAD + N_META:]
```

```python
import functools
import math

import jax
import jax.numpy as jnp
from jax import lax
from jax.experimental import pallas as pl
from jax.experimental.pallas import tpu as pltpu

D_MODEL = 2048
N_META = 16
BLK = 128
PAD = BLK - N_META
ROPE_THETA = 10000.0
NORM_EPS = 1e-5
NEG = -1e30
SWA_HEADS = 12
SWA_KV_HEADS = 4
HEAD_DIM = 64
SWA_WIDTH = SWA_HEADS * HEAD_DIM
SSM_WIDTH = 768
SSM_GROUP = 16
SSM_GROUPS = SSM_WIDTH // SSM_GROUP
SSM_STATE = 64
DIFF_HEADS = 4
DIFF_V_DIM = 128
DIFF_WIDTH = DIFF_HEADS * DIFF_V_DIM
IN_WIDTH = 3584
N_EXPERT_GROUPS = 4
EXPERTS_PER_GROUP = 8
N_EXPERTS = N_EXPERT_GROUPS * EXPERTS_PER_GROUP
D_EXPERT = 512

COL_KA = 768
COL_VA = 1024
COL_US = 1280
COL_QD = 2048
COL_KD = 2560
COL_VD = 3072

LANES = 128
SSM_T = 16
SSM_PACK = LANES // SSM_GROUP
SSM_PACKS = SSM_GROUPS // SSM_PACK
SSM_PSTATE = SSM_PACK * SSM_STATE
MOE_BM = 256
VMEM_LIMIT = 56 * 1024 * 1024

F32 = jnp.float32
BF16 = jnp.bfloat16


def _row_tile(rows):
    for t in (512, 256, 128):
        if rows % t == 0:
            return t
    raise ValueError(f"row count {rows} is not a multiple of 128")


def _params(sem, vmem=VMEM_LIMIT):
    return pltpu.CompilerParams(dimension_semantics=sem, vmem_limit_bytes=vmem)


def _rms_norm(x, g):
    return x * lax.rsqrt(jnp.mean(x * x, axis=-1, keepdims=True) + NORM_EPS) * g


IN_TN = 512
ROPE_TILES = (0, 1, 4, 5)


def _norm_inproj_kernel(x_ref, g_ref, w_ref, cos_ref, sin_ref, o_ref, xn_ref):
    x = x_ref[...]
    ms = jnp.mean(x * x, axis=-1, keepdims=True)
    xn_ref[...] = (x * lax.rsqrt(ms + NORM_EPS) * g_ref[...]).astype(BF16)
    shape = (x.shape[0], IN_TN)
    lane = lax.broadcasted_iota(jnp.int32, shape, 1)
    first_half = (lane & (HEAD_DIM - 1)) < (HEAD_DIM // 2)
    reps = IN_TN // LANES
    cos = jnp.tile(cos_ref[...], (1, reps))
    sin = jnp.tile(sin_ref[...], (1, reps))
    for j in range(IN_WIDTH // IN_TN):
        cols = slice(j * IN_TN, (j + 1) * IN_TN)
        acc = jnp.dot(xn_ref[...], w_ref[:, cols], preferred_element_type=F32)
        if j in ROPE_TILES:
            partner = jnp.where(first_half, pltpu.roll(acc, IN_TN - HEAD_DIM // 2, 1),
                                pltpu.roll(acc, HEAD_DIM // 2, 1))
            acc = acc * cos + partner * sin
        o_ref[:, cols] = acc.astype(o_ref.dtype)


def _norm_inproj(h, g, w, cos, sin):
    rows = h.shape[0]
    tm = _row_tile(rows)
    return pl.pallas_call(
        _norm_inproj_kernel,
        grid=(rows // tm,),
        in_specs=[
            pl.BlockSpec((tm, D_MODEL), lambda i: (i, 0)),
            pl.BlockSpec((1, D_MODEL), lambda i: (0, 0)),
            pl.BlockSpec((D_MODEL, IN_WIDTH), lambda i: (0, 0)),
            pl.BlockSpec((tm, LANES), lambda i: (i, 0)),
            pl.BlockSpec((tm, LANES), lambda i: (i, 0)),
        ],
        out_specs=pl.BlockSpec((tm, IN_WIDTH), lambda i: (i, 0)),
        out_shape=jax.ShapeDtypeStruct((rows, IN_WIDTH), BF16),
        scratch_shapes=[pltpu.VMEM((tm, D_MODEL), BF16)],
        compiler_params=_params(("parallel",)),
        name="norm_inproj",
    )(h, g, w, cos, sin)


def _swa_kernel(sink_ref, q_ref, kp_ref, kc_ref, vp_ref, vc_ref, o_ref):
    n = pl.program_id(1)
    shape = (BLK, 2 * BLK)
    qi = lax.broadcasted_iota(jnp.int32, shape, 0)
    c = lax.broadcasted_iota(jnp.int32, shape, 1)
    kj = (n - 1) * BLK + c
    delta = qi + BLK - c
    ok = (delta >= 0) & (delta < BLK) & (kj >= PAD)
    grp = SWA_HEADS // SWA_KV_HEADS
    outs = []
    for j in range(SWA_KV_HEADS):
        cols = slice(j * HEAD_DIM, (j + 1) * HEAD_DIM)
        kk = jnp.concatenate([kp_ref[:, cols], kc_ref[:, cols]], axis=0)
        vv = jnp.concatenate([vp_ref[:, cols], vc_ref[:, cols]], axis=0)
        for g in range(grp):
            hd = j * grp + g
            qh = q_ref[:, hd * HEAD_DIM:(hd + 1) * HEAD_DIM]
            s = lax.dot_general(qh, kk, (((1,), (1,)), ((), ())), preferred_element_type=F32) * (HEAD_DIM ** -0.5)
            s = jnp.where(ok, s, NEG)
            sink = sink_ref[hd]
            m = jnp.maximum(jnp.max(s, axis=-1, keepdims=True), sink)
            p = jnp.exp(s - m)
            denom = jnp.sum(p, axis=-1, keepdims=True) + jnp.exp(sink - m)
            outs.append(jnp.dot(p.astype(BF16), vv, preferred_element_type=F32) / denom)
    o_ref[...] = jnp.concatenate(outs, axis=1).astype(o_ref.dtype)


def _swa(proj, sinks, batch, nb):
    rows = proj.shape[0]
    kvw = SWA_KV_HEADS * HEAD_DIM

    def cur(col):
        return lambda b, n: (b * nb + n, col)

    def prev(col):
        return lambda b, n: (b * nb + jnp.maximum(n - 1, 0), col)

    return pl.pallas_call(
        _swa_kernel,
        grid=(batch, nb),
        in_specs=[
            pl.BlockSpec(memory_space=pltpu.SMEM),
            pl.BlockSpec((BLK, SWA_WIDTH), cur(0)),
            pl.BlockSpec((BLK, kvw), prev(COL_KA // kvw)),
            pl.BlockSpec((BLK, kvw), cur(COL_KA // kvw)),
            pl.BlockSpec((BLK, kvw), prev(COL_VA // kvw)),
            pl.BlockSpec((BLK, kvw), cur(COL_VA // kvw)),
        ],
        out_specs=pl.BlockSpec((BLK, SWA_WIDTH), cur(0)),
        out_shape=jax.ShapeDtypeStruct((rows, SWA_WIDTH), BF16),
        compiler_params=_params(("parallel", "parallel")),
        name="swa",
    )(sinks, proj, proj, proj, proj, proj)


def _diff_kernel(lam_ref, g_ref, q_ref, k_ref, v_ref, o_ref, *, lam_init, nb):
    row = lax.broadcasted_iota(jnp.int32, (BLK, BLK), 0)
    col = lax.broadcasted_iota(jnp.int32, (BLK, BLK), 1)
    causal_ok = col <= row
    pad_ok = col >= PAD
    lane = lax.broadcasted_iota(jnp.int32, (BLK, LANES), 1)
    lv = lam_ref[...]
    lam = (jnp.exp(jnp.sum(lv[0:1] * lv[1:2], axis=-1, keepdims=True))
           - jnp.exp(jnp.sum(lv[2:3] * lv[3:4], axis=-1, keepdims=True)) + lam_init)
    gain = g_ref[...] * (1.0 - lam_init)
    scale = HEAD_DIM ** -0.5
    for n in range(nb):
        ke = (n + 1) * BLK
        q = q_ref[n * BLK:(n + 1) * BLK, :] * scale
        kx = k_ref[0:ke, :]
        vx = v_ref[0:ke, :]
        heads = []
        for s_idx in range(2):
            qx = jnp.where((lane >= HEAD_DIM) == bool(s_idx), q, jnp.zeros_like(q))
            s = lax.dot_general(qx, kx, (((1,), (1,)), ((), ())), preferred_element_type=F32)
            if n == 0:
                s = jnp.where(causal_ok & pad_ok, s, NEG)
            else:
                first = jnp.where(pad_ok, s[:, :BLK], NEG)
                last = jnp.where(causal_ok, s[:, ke - BLK:], NEG)
                middle = [s[:, BLK:ke - BLK]] if n > 1 else []
                s = jnp.concatenate([first] + middle + [last], axis=1)
            m = jnp.max(s, axis=-1, keepdims=True)
            p = jnp.exp(s - m)
            denom = jnp.sum(p, axis=-1, keepdims=True)
            heads.append(jnp.dot(p.astype(BF16), vx, preferred_element_type=F32) / denom)
        o = heads[0] - lam * heads[1]
        o = o * lax.rsqrt(jnp.mean(o * o, axis=-1, keepdims=True) + NORM_EPS) * gain
        o_ref[n * BLK:(n + 1) * BLK, :] = o.astype(o_ref.dtype)


def _diff_attn(proj, lam_vecs, subln_g, lam_init, batch, nb):
    rows = proj.shape[0]
    lp = nb * BLK
    return pl.pallas_call(
        functools.partial(_diff_kernel, lam_init=lam_init, nb=nb),
        grid=(batch, DIFF_HEADS),
        in_specs=[
            pl.BlockSpec((4, HEAD_DIM), lambda b, h: (0, 0)),
            pl.BlockSpec((1, DIFF_V_DIM), lambda b, h: (0, 0)),
            pl.BlockSpec((lp, LANES), lambda b, h: (b, COL_QD // LANES + h)),
            pl.BlockSpec((lp, LANES), lambda b, h: (b, COL_KD // LANES + h)),
            pl.BlockSpec((lp, LANES), lambda b, h: (b, COL_VD // LANES + h)),
        ],
        out_specs=pl.BlockSpec((lp, DIFF_V_DIM), lambda b, h: (b, h)),
        out_shape=jax.ShapeDtypeStruct((rows, DIFF_WIDTH), BF16),
        compiler_params=_params(("parallel", "parallel")),
        name="diff_attn",
    )(lam_vecs, subln_g, proj, proj, proj)


def _ssm_row_chunk(rows):
    for c in (544, 512, 384, 256, 128, 64, 32, 16):
        if rows % c == 0:
            return c
    raise ValueError(rows)


def _ssm_expand_operators(kc_ref, bc_ref, cc_ref, k_ref, bw_ref, cw_ref):
    def iota(shape, dim):
        return lax.broadcasted_iota(jnp.int32, shape, dim)

    def group(idx, width):
        return jnp.right_shift(idx, width.bit_length() - 1) & (SSM_PACK - 1)

    hs, ps, tl = SSM_GROUP, SSM_STATE, SSM_T * LANES
    spread_h = ((iota((LANES, hs), 0) & (hs - 1)) == iota((LANES, hs), 1)).astype(BF16)
    kt = jnp.dot(spread_h, kc_ref[...], preferred_element_type=F32)
    same = group(iota((LANES, tl), 0), hs) == group(iota((LANES, tl), 1), hs)
    k_ref[...] = jnp.where(same, kt, 0.0).astype(BF16)
    spread_s = (iota((ps, SSM_PSTATE), 0) == (iota((ps, SSM_PSTATE), 1) & (ps - 1))).astype(BF16)
    shape = (tl, SSM_PSTATE)
    same = group(iota(shape, 0), hs) == group(iota(shape, 1), ps)
    for part in range(2):
        bw = jnp.dot(bc_ref[part], spread_s, preferred_element_type=F32)
        bw_ref[:, part * SSM_PSTATE:(part + 1) * SSM_PSTATE] = jnp.where(same, bw, 0.0).astype(BF16)
    shape = (LANES, SSM_PSTATE)
    same = group(iota(shape, 0), hs) == group(iota(shape, 1), ps)
    for j in range(SSM_T):
        for part in range(2):
            ct = jnp.dot(spread_h, cc_ref[part, j], preferred_element_type=F32)
            cw_ref[j, :, part * SSM_PSTATE:(part + 1) * SSM_PSTATE] = jnp.where(same, ct, 0.0).astype(BF16)


def _ssm_kernel(u_ref, kc_ref, bc_ref, cc_ref, a_ref, d_ref, o_ref, up_ref, y_ref, e_ref, s_ref, k_ref, bw_ref,
                cw_ref, *, batch, lp):
    rows = batch * lp
    n_chunks = lp // SSM_T
    tot_chunks = rows // SSM_T
    _ssm_expand_operators(kc_ref, bc_ref, cc_ref, k_ref, bw_ref, cw_ref)
    contract_lanes = (((1,), (1,)), ((), ()))
    up_ref[0:SSM_T, :] = jnp.zeros((SSM_T, LANES), F32)
    up_ref[SSM_T:, :] = u_ref[...].astype(F32)
    for b in range(batch):
        up_ref[SSM_T + b * lp:SSM_T + b * lp + PAD, :] = jnp.zeros((PAD, LANES), F32)

    ch = _ssm_row_chunk(rows)
    rmod = lax.broadcasted_iota(jnp.int32, (ch, LANES), 0) & (SSM_T - 1)
    for r0 in range(0, rows, ch):
        pieces = []
        for k in range(SSM_T):
            sl = up_ref[SSM_T - k + r0:SSM_T - k + r0 + ch, :]
            pieces.append(jnp.where(rmod >= k, sl, 0.0).astype(BF16))
        y_ref[r0:r0 + ch, :] = lax.dot_general(jnp.concatenate(pieces, axis=1), k_ref[...], contract_lanes,
                                               preferred_element_type=F32)

    pieces = [up_ref[pl.ds(SSM_T + i, tot_chunks, stride=SSM_T), :].astype(BF16) for i in range(SSM_T)]
    e_ref[...] = jnp.dot(jnp.concatenate(pieces, axis=1), bw_ref[...], preferred_element_type=F32)

    a_re = a_ref[:, :SSM_PSTATE]
    a_im = a_ref[:, SSM_PSTATE:]

    def step(c, carry):
        new = []
        for b in range(batch):
            sr, si = carry[b]
            idx = b * n_chunks + c
            s_ref[pl.ds(idx, 1), :SSM_PSTATE] = sr
            s_ref[pl.ds(idx, 1), SSM_PSTATE:] = si
            er = e_ref[pl.ds(idx, 1), :SSM_PSTATE]
            ei = e_ref[pl.ds(idx, 1), SSM_PSTATE:]
            new.append((a_re * sr - a_im * si + er, a_re * si + a_im * sr + ei))
        return tuple(new)

    zero = jnp.zeros((1, SSM_PSTATE), F32)
    lax.fori_loop(0, n_chunks, step, tuple((zero, zero) for _ in range(batch)))

    sb = s_ref[...].astype(BF16)
    for j in range(SSM_T):
        z = lax.dot_general(sb, cw_ref[j], contract_lanes, preferred_element_type=F32)
        idx = pl.ds(j, tot_chunks, stride=SSM_T)
        y_ref[idx, :] = y_ref[idx, :] + z

    y = y_ref[...] + d_ref[...] * up_ref[SSM_T:, :]
    o_ref[...] = jax.nn.gelu(y, approximate=True).astype(o_ref.dtype)


def _ssm(proj, kc, bc, cc, a_t, dskip, batch, lp):
    rows = proj.shape[0]
    tot_chunks = rows // SSM_T
    return pl.pallas_call(
        functools.partial(_ssm_kernel, batch=batch, lp=lp),
        grid=(SSM_PACKS,),
        in_specs=[
            pl.BlockSpec((rows, LANES), lambda p: (0, COL_US // LANES + p)),
            pl.BlockSpec((None, SSM_GROUP, SSM_T * LANES), lambda p: (p, 0, 0)),
            pl.BlockSpec((None, 2, SSM_T * LANES, SSM_STATE), lambda p: (p, 0, 0, 0)),
            pl.BlockSpec((None, 2, SSM_T, SSM_GROUP, SSM_PSTATE), lambda p: (p, 0, 0, 0, 0)),
            pl.BlockSpec((None, 1, 2 * SSM_PSTATE), lambda p: (p, 0, 0)),
            pl.BlockSpec((None, 1, LANES), lambda p: (p, 0, 0)),
        ],
        out_specs=pl.BlockSpec((rows, LANES), lambda p: (0, p)),
        out_shape=jax.ShapeDtypeStruct((rows, SSM_WIDTH), F32),
        scratch_shapes=[
            pltpu.VMEM((rows + SSM_T, LANES), F32),
            pltpu.VMEM((rows, LANES), F32),
            pltpu.VMEM((tot_chunks, 2 * SSM_PSTATE), F32),
            pltpu.VMEM((tot_chunks, 2 * SSM_PSTATE), F32),
            pltpu.VMEM((LANES, SSM_T * LANES), BF16),
            pltpu.VMEM((SSM_T * LANES, 2 * SSM_PSTATE), BF16),
            pltpu.VMEM((SSM_T, LANES, 2 * SSM_PSTATE), BF16),
        ],
        compiler_params=_params(("parallel",)),
        name="ssm",
    )(proj, kc, bc, cc, a_t, dskip)


def _ssm_operators(lam_re, lam_im, b_re, b_im, c_re, c_im, log_dt):
    p, h = SSM_STATE, SSM_GROUP
    dt = jnp.exp(log_dt.astype(F32))[:, None]
    lr, li = lam_re.astype(F32), lam_im.astype(F32)
    mag = jnp.exp(lr * dt)
    abar_re, abar_im = mag * jnp.cos(li * dt), mag * jnp.sin(li * dt)
    den = lr * lr + li * li
    nr = abar_re - 1.0
    f_re = (nr * lr + abar_im * li) / den
    f_im = (abar_im * lr - nr * li) / den
    br, bi = b_re.astype(F32), b_im.astype(F32)
    bbar_re = f_re[..., None] * br - f_im[..., None] * bi
    bbar_im = f_re[..., None] * bi + f_im[..., None] * br
    ks = jnp.arange(SSM_T + 1, dtype=F32)[:, None, None]
    pmag = jnp.exp(ks * (lr * dt)[None])
    pw_re = pmag * jnp.cos(ks * (li * dt)[None])
    pw_im = pmag * jnp.sin(ks * (li * dt)[None])
    cr, ci = c_re.astype(F32), c_im.astype(F32)

    packs, q = SSM_PACKS, SSM_PACK

    cb_re = cr[:, :, :, None] * bbar_re[:, None, :, :] - ci[:, :, :, None] * bbar_im[:, None, :, :]
    cb_im = cr[:, :, :, None] * bbar_im[:, None, :, :] + ci[:, :, :, None] * bbar_re[:, None, :, :]
    klag = (jnp.einsum('kgs,gosi->kgoi', pw_re[:SSM_T], cb_re)
            - jnp.einsum('kgs,gosi->kgoi', pw_im[:SSM_T], cb_im))
    kc = klag.reshape(SSM_T, packs, q, h, h).transpose(1, 3, 0, 2, 4).reshape(packs, h, SSM_T * LANES)

    rev_re, rev_im = pw_re[SSM_T - 1::-1, :, None, :], pw_im[SSM_T - 1::-1, :, None, :]
    bt_re, bt_im = jnp.swapaxes(bbar_re, 1, 2)[None], jnp.swapaxes(bbar_im, 1, 2)[None]

    def inj_by_pack(x):
        return x.reshape(SSM_T, packs, q * h, p).transpose(1, 0, 2, 3).reshape(packs, SSM_T * LANES, p)

    bc = jnp.stack([inj_by_pack(rev_re * bt_re - rev_im * bt_im), inj_by_pack(rev_re * bt_im + rev_im * bt_re)], axis=1)

    ca_re = cr[None] * pw_re[1:, :, None, :] - ci[None] * pw_im[1:, :, None, :]
    ca_im = cr[None] * pw_im[1:, :, None, :] + ci[None] * pw_re[1:, :, None, :]

    def out_by_pack(x):
        return x.reshape(SSM_T, packs, q, h, p).transpose(1, 0, 3, 2, 4).reshape(packs, SSM_T, h, SSM_PSTATE)

    cc = jnp.stack([out_by_pack(ca_re), out_by_pack(-ca_im)], axis=1)

    a_t = jnp.concatenate([pw_re[SSM_T].reshape(packs, 1, SSM_PSTATE),
                           pw_im[SSM_T].reshape(packs, 1, SSM_PSTATE)], axis=-1)
    return kc.astype(BF16), bc.astype(BF16), cc.astype(BF16), a_t


OUT_TN = 512


def _split_bf16(x):
    hi = x.astype(BF16)
    return hi, (x - hi.astype(F32)).astype(BF16)


def _route(hn, rwh_ref, rwl_ref, rb_ref):
    hi, lo = _split_bf16(hn)
    logits = (jnp.dot(hi, rwh_ref[...], preferred_element_type=F32)
              + (jnp.dot(lo, rwh_ref[...], preferred_element_type=F32)
                 + jnp.dot(hi, rwl_ref[...], preferred_element_type=F32))) + rb_ref[...]
    lane = lax.broadcasted_iota(jnp.int32, logits.shape, 1)

    def first_argmax(v, vmax):
        return jnp.min(jnp.where(v == vmax, lane, LANES), axis=-1, keepdims=True)

    gl = jnp.where(lane < N_EXPERT_GROUPS, logits, NEG)
    gmax = jnp.max(gl, axis=-1, keepdims=True)
    g_sel = first_argmax(gl, gmax)
    p_g = 1.0 / jnp.sum(jnp.exp(gl - gmax), axis=-1, keepdims=True)
    e_lane = lane - N_EXPERT_GROUPS
    in_group = (e_lane >= 0) & (e_lane < N_EXPERTS) & (jnp.right_shift(e_lane, 3) == g_sel)
    el = jnp.where(in_group, logits, NEG)
    m1 = jnp.max(el, axis=-1, keepdims=True)
    i1 = first_argmax(el, m1)
    el2 = jnp.where(lane == i1, NEG, el)
    m2 = jnp.max(el2, axis=-1, keepdims=True)
    i2 = first_argmax(el2, m2)
    z = jnp.sum(jnp.exp(el - m1), axis=-1, keepdims=True)
    p1 = 1.0 / z
    p2 = jnp.exp(m2 - m1) / z
    w1 = p1 / (p1 + p2) * p_g
    w2 = p2 / (p1 + p2) * p_g
    wts = jnp.where(lane == 0, w1, jnp.where(lane == 1, w2, 0.0))
    ids = jnp.where(lane == 0, i1 - N_EXPERT_GROUPS, jnp.where(lane == 1, i2 - N_EXPERT_GROUPS, 0))
    return wts, ids


def _assignment_ranks(ids):
    tm = ids.shape[0]
    lane = lax.broadcasted_iota(jnp.int32, ids.shape, 1)
    oh0 = lane == ids[:, 0:1]
    oh1 = lane == ids[:, 1:2]
    hits = (oh0 | oh1).astype(BF16)
    r = lax.broadcasted_iota(jnp.int32, (tm, tm), 0)
    c = lax.broadcasted_iota(jnp.int32, (tm, tm), 1)
    before = jnp.dot((c < r).astype(BF16), hits, preferred_element_type=F32)
    rank0 = jnp.sum(jnp.where(oh0, before, 0.0), axis=-1, keepdims=True)
    rank1 = jnp.sum(jnp.where(oh1, before, 0.0), axis=-1, keepdims=True)
    ranks = jnp.where(lane == 0, rank0, jnp.where(lane == 1, rank1, 0.0)).astype(jnp.int32)
    return ranks, jnp.sum(hits.astype(F32), axis=0, keepdims=True).astype(jnp.int32)


def _mixer_out_kernel(h_ref, a_ref, y_ref, c_ref, wo_ref, wglu_ref, g_ref, rwh_ref, rwl_ref, rb_ref,
                      o_ref, hn_ref, wt_ref, id_ref, rank_ref, cnt_ref):
    y = y_ref[...]
    gate = jnp.dot(y.astype(BF16), wglu_ref[...], preferred_element_type=F32)
    b = (y * jax.nn.sigmoid(gate)).astype(BF16)
    mixed = jnp.concatenate([a_ref[...], b, c_ref[...]], axis=1)
    for j in range(D_MODEL // OUT_TN):
        cols = slice(j * OUT_TN, (j + 1) * OUT_TN)
        o_ref[:, cols] = h_ref[:, cols] + jnp.dot(mixed, wo_ref[:, cols], preferred_element_type=F32)
    x = o_ref[...]
    hn = _rms_norm(x, g_ref[...])
    hn_ref[...] = hn.astype(BF16)
    wts, ids = _route(hn, rwh_ref, rwl_ref, rb_ref)
    wt_ref[...] = wts
    id_ref[...] = ids
    ranks, counts = _assignment_ranks(ids)
    rank_ref[...] = ranks
    cnt_ref[...] = jnp.broadcast_to(counts, cnt_ref.shape)


def _mixer_out(h, oa, yg, oc, wo, wglu, g, rw_hi, rw_lo, rb):
    rows = h.shape[0]
    tm = _row_tile(rows)

    def row_spec(width):
        return pl.BlockSpec((tm, width), lambda i: (i, 0))

    def whole(shape):
        return pl.BlockSpec(shape, lambda i: (0, 0))

    return pl.pallas_call(
        _mixer_out_kernel,
        grid=(rows // tm,),
        in_specs=[
            row_spec(D_MODEL), row_spec(SWA_WIDTH), row_spec(SSM_WIDTH), row_spec(DIFF_WIDTH),
            whole((D_MODEL, D_MODEL)), whole((SSM_WIDTH, SSM_WIDTH)), whole((1, D_MODEL)),
            whole((D_MODEL, LANES)), whole((D_MODEL, LANES)), whole((1, LANES)),
        ],
        out_specs=[row_spec(D_MODEL), row_spec(D_MODEL), row_spec(LANES), row_spec(LANES), row_spec(LANES),
                   pl.BlockSpec((None, 8, LANES), lambda i: (i, 0, 0))],
        out_shape=[
            jax.ShapeDtypeStruct((rows, D_MODEL), F32),
            jax.ShapeDtypeStruct((rows, D_MODEL), BF16),
            jax.ShapeDtypeStruct((rows, LANES), F32),
            jax.ShapeDtypeStruct((rows, LANES), jnp.int32),
            jax.ShapeDtypeStruct((rows, LANES), jnp.int32),
            jax.ShapeDtypeStruct((rows // tm, 8, LANES), jnp.int32),
        ],
        compiler_params=_params(("parallel",)),
        name="mixer_out",
    )(h, oa, yg, oc, wo, wglu, g, rw_hi, rw_lo, rb)


MOE_UNIT = 16
MOE_DTYPE = BF16


def _moe_tile_rows(tt):
    return 2 * tt + N_EXPERTS * MOE_UNIT


def _moe_tables(tile_cnt, n_blk):
    cnt = tile_cnt[:, 0, :N_EXPERTS]
    seg = (cnt + MOE_UNIT - 1) // MOE_UNIT * MOE_UNIT
    size = jnp.sum(seg, axis=0)
    padded = (size + MOE_BM - 1) // MOE_BM * MOE_BM
    pend = jnp.cumsum(padded)
    pstart = pend - padded
    base = pstart[None, :] + jnp.cumsum(seg, axis=0) - seg
    loc = jnp.cumsum(seg, axis=1) - seg
    units = seg // MOE_UNIT
    blk_start = jnp.arange(n_blk, dtype=jnp.int32) * MOE_BM
    blk_expert = jnp.minimum(jnp.sum((pend[None, :] <= blk_start[:, None]).astype(jnp.int32), axis=1), N_EXPERTS - 1)
    ex = jnp.arange(N_EXPERTS, dtype=jnp.int32)
    nonempty = size > 0
    run_parity = (jnp.cumsum(nonempty.astype(jnp.int32)) - 1) & 1
    later = (ex[None, :] > ex[:, None]) & nonempty[None, :]
    nxt = jnp.min(jnp.where(later, ex[None, :], N_EXPERTS), axis=1)
    nxt = jnp.where(nxt == N_EXPERTS, -1, nxt)
    i32 = jnp.int32
    loc_lanes = jnp.pad(loc, ((0, 0), (0, LANES - N_EXPERTS)))
    return dict(
        base=base.reshape(-1).astype(i32), loc=loc.reshape(-1).astype(i32), units=units.reshape(-1).astype(i32),
        tile_units=jnp.sum(units, axis=1).astype(i32),
        tail_start=(pstart + size).astype(i32), tail_units=((padded - size) // MOE_UNIT).astype(i32),
        loc_lanes=jnp.broadcast_to(loc_lanes[:, None, :], (loc.shape[0], 8, LANES)).astype(i32),
        blk_expert=blk_expert.astype(i32), blk_parity=run_parity[blk_expert].astype(i32),
        blk_next=nxt[blk_expert].astype(i32), n_used=(pend[-1] // MOE_BM).astype(i32).reshape(1))


def _segment_copies(tile, hbm, buf, sem, to_hbm, base_ref, loc_ref, units_ref):
    for e in range(N_EXPERTS):
        k = tile * N_EXPERTS + e
        row_hbm, row_buf = base_ref[k], loc_ref[k]

        def start(u, carry):
            a = hbm.at[pl.ds(pl.multiple_of(row_hbm + u * MOE_UNIT, MOE_UNIT), MOE_UNIT), :]
            b = buf.at[pl.ds(pl.multiple_of(row_buf + u * MOE_UNIT, MOE_UNIT), MOE_UNIT), :]
            (pltpu.make_async_copy(b, a, sem) if to_hbm else pltpu.make_async_copy(a, b, sem)).start()
            return carry

        lax.fori_loop(0, units_ref[k], start, 0)


def _wait_unit_copies(count, hbm, buf, sem):
    def wait(u, carry):
        pltpu.make_async_copy(hbm.at[pl.ds(0, MOE_UNIT), :], buf.at[pl.ds(0, MOE_UNIT), :], sem).wait()
        return carry

    lax.fori_loop(0, count, wait, 0)


def _local_rows(ids, ranks, loc_lanes):
    lane = lax.broadcasted_iota(jnp.int32, ids.shape, 1)
    loc = loc_lanes.astype(F32)
    start0 = jnp.sum(jnp.where(lane == ids[:, 0:1], loc, 0.0), axis=-1, keepdims=True).astype(jnp.int32)
    start1 = jnp.sum(jnp.where(lane == ids[:, 1:2], loc, 0.0), axis=-1, keepdims=True).astype(jnp.int32)
    return start0 + ranks[:, 0:1], start1 + ranks[:, 1:2]


def _dispatch_kernel(base_ref, loc_ref, units_ref, tile_units_ref, tail_start_ref, tail_units_ref, nused_ref,
                     hn_ref, ids_ref, rank_ref, locv_ref, x_hbm, lrow_ref, ybuf, zbuf, sem):
    j = pl.program_id(0)
    n = pl.num_programs(0)
    slot = j % 2
    tt, rt = hn_ref.shape[0], ybuf.shape[1]

    @pl.when(j >= 2)
    def _():
        _wait_unit_copies(tile_units_ref[j - 2], x_hbm, ybuf.at[slot], sem.at[slot])

    row0, row1 = _local_rows(ids_ref[...], rank_ref[...], locv_ref[0:1, :])
    lane = lax.broadcasted_iota(jnp.int32, lrow_ref.shape, 1)
    lrow_ref[...] = jnp.where(lane == 0, row0, jnp.where(lane == 1, row1, 0))
    col = lax.broadcasted_iota(jnp.int32, (tt, rt), 1)
    onehot = ((col == row0) | (col == row1)).astype(BF16)
    compacted = lax.dot_general(onehot, hn_ref[...], (((0,), (0,)), ((), ())), preferred_element_type=F32)
    ybuf[slot] = compacted.astype(ybuf.dtype)
    _segment_copies(j, x_hbm, ybuf.at[slot], sem.at[slot], True, base_ref, loc_ref, units_ref)

    @pl.when(j == n - 1)
    def _():
        zbuf[...] = jnp.zeros(zbuf.shape, zbuf.dtype)
        zunit = zbuf.at[pl.ds(0, MOE_UNIT), :]
        total = 0
        for e in range(N_EXPERTS):
            row = tail_start_ref[e]

            def start(u, carry):
                dst = x_hbm.at[pl.ds(pl.multiple_of(row + u * MOE_UNIT, MOE_UNIT), MOE_UNIT), :]
                pltpu.make_async_copy(zunit, dst, sem.at[2]).start()
                return carry

            lax.fori_loop(0, tail_units_ref[e], start, 0)
            total = total + tail_units_ref[e]
        _wait_unit_copies(total, x_hbm, zunit, sem.at[2])
        n_blk = x_hbm.shape[0] // MOE_BM

        def block_copy(blk):
            dst = x_hbm.at[pl.ds(pl.multiple_of(blk * MOE_BM, MOE_BM), MOE_BM), :]
            return pltpu.make_async_copy(zbuf, dst, sem.at[3])

        def start_block(blk, carry):
            block_copy(blk).start()
            return carry

        def wait_block(blk, carry):
            block_copy(blk).wait()
            return carry

        lax.fori_loop(nused_ref[0], n_blk, start_block, 0)
        lax.fori_loop(nused_ref[0], n_blk, wait_block, 0)

        @pl.when(j >= 1)
        def _():
            _wait_unit_copies(tile_units_ref[jnp.maximum(j - 1, 0)], x_hbm, ybuf.at[1 - slot], sem.at[1 - slot])

        _wait_unit_copies(tile_units_ref[j], x_hbm, ybuf.at[slot], sem.at[slot])


def _dispatch(hn, ids, ranks, tables, tt, n_rows):
    rows = hn.shape[0]
    rt = _moe_tile_rows(tt)
    row_spec = pl.BlockSpec((tt, LANES), lambda j, *_: (j, 0))
    grid_spec = pltpu.PrefetchScalarGridSpec(
        num_scalar_prefetch=7,
        grid=(rows // tt,),
        in_specs=[pl.BlockSpec((tt, D_MODEL), lambda j, *_: (j, 0)), row_spec, row_spec,
                  pl.BlockSpec((None, 8, LANES), lambda j, *_: (j, 0, 0))],
        out_specs=[pl.BlockSpec(memory_space=pl.ANY), row_spec],
        scratch_shapes=[pltpu.VMEM((2, rt, D_MODEL), MOE_DTYPE), pltpu.VMEM((MOE_BM, D_MODEL), MOE_DTYPE),
                        pltpu.SemaphoreType.DMA((4,))],
    )
    return pl.pallas_call(
        _dispatch_kernel,
        grid_spec=grid_spec,
        out_shape=[jax.ShapeDtypeStruct((n_rows, D_MODEL), MOE_DTYPE),
                   jax.ShapeDtypeStruct((rows, LANES), jnp.int32)],
        compiler_params=_params(("arbitrary",)),
        name="moe_dispatch",
    )(tables["base"], tables["loc"], tables["units"], tables["tile_units"], tables["tail_start"],
      tables["tail_units"], tables["n_used"], hn, ids, ranks, tables["loc_lanes"])


WEIGHT_DMA_PRIORITY = 1


def _expert_kernel(be_ref, par_ref, nxt_ref, nused_ref, x_ref, wg_hbm, wu_hbm, wd_hbm, o_ref,
                   wgf, wuf, wdf, wgb, wub, wdb, wsem, *, layer):
    i = pl.program_id(0)
    n_used = nused_ref[0]

    def weight_copies(expert, wslot):
        pairs = ((wg_hbm, wgf), (wu_hbm, wuf), (wd_hbm, wdf))
        return [pltpu.make_async_copy(w.at[layer, expert], buf.at[wslot], wsem.at[wslot]) for w, buf in pairs]

    @pl.when(i == 0)
    def _():
        for c in weight_copies(be_ref[0], 0):
            c.start(priority=WEIGHT_DMA_PRIORITY)

    @pl.when(i < n_used)
    def _():
        @pl.when((i == 0) | (be_ref[i] != be_ref[jnp.maximum(i - 1, 0)]))
        def _():
            wslot = par_ref[i]
            for c in weight_copies(be_ref[i], wslot):
                c.wait()

            @pl.when(nxt_ref[i] >= 0)
            def _():
                for c in weight_copies(nxt_ref[i], 1 - wslot):
                    c.start(priority=WEIGHT_DMA_PRIORITY)

            wgb[...] = wgf[wslot].astype(BF16)
            wub[...] = wuf[wslot].astype(BF16)
            wdb[...] = wdf[wslot].astype(BF16)

        x = x_ref[...]
        gate = jnp.dot(x, wgb[...], preferred_element_type=F32)
        up = jnp.dot(x, wub[...], preferred_element_type=F32)
        act = (jax.nn.silu(gate) * up).astype(BF16)
        o_ref[...] = jnp.dot(act, wdb[...], preferred_element_type=F32).astype(o_ref.dtype)

    @pl.when(i >= n_used)
    def _():
        o_ref[...] = jnp.zeros(o_ref.shape, o_ref.dtype)


def _experts(x_sorted, tables, wg, wu, wd, layer, n_blk):
    hbm = pl.BlockSpec(memory_space=pl.ANY)
    grid_spec = pltpu.PrefetchScalarGridSpec(
        num_scalar_prefetch=4,
        grid=(n_blk,),
        in_specs=[pl.BlockSpec((MOE_BM, D_MODEL), lambda i, be, par, nxt, nu: (jnp.minimum(i, nu[0] - 1), 0)),
                  hbm, hbm, hbm],
        out_specs=pl.BlockSpec((MOE_BM, D_MODEL), lambda i, *_: (i, 0)),
        scratch_shapes=[
            pltpu.VMEM((2, D_MODEL, D_EXPERT), F32),
            pltpu.VMEM((2, D_MODEL, D_EXPERT), F32),
            pltpu.VMEM((2, D_EXPERT, D_MODEL), F32),
            pltpu.VMEM((D_MODEL, D_EXPERT), BF16),
            pltpu.VMEM((D_MODEL, D_EXPERT), BF16),
            pltpu.VMEM((D_EXPERT, D_MODEL), BF16),
            pltpu.SemaphoreType.DMA((2,)),
        ],
    )
    return pl.pallas_call(
        functools.partial(_expert_kernel, layer=layer),
        grid_spec=grid_spec,
        out_shape=jax.ShapeDtypeStruct((n_blk * MOE_BM, D_MODEL), MOE_DTYPE),
        compiler_params=_params(("arbitrary",)),
        name="moe_experts",
    )(tables["blk_expert"], tables["blk_parity"], tables["blk_next"], tables["n_used"], x_sorted, wg, wu, wd)


def _combine_kernel(base_ref, loc_ref, units_ref, tile_units_ref, h_ref, wt_ref, lrow_ref, g_ref, y_hbm, o_ref,
                    ybuf, sem, *, final_norm):
    j = pl.program_id(0)
    n = pl.num_programs(0)
    slot = j % 2
    tt, rt = h_ref.shape[0], ybuf.shape[1]

    @pl.when(j == 0)
    def _():
        ybuf[...] = jnp.zeros(ybuf.shape, ybuf.dtype)
        _segment_copies(0, y_hbm, ybuf.at[0], sem.at[0], False, base_ref, loc_ref, units_ref)

    @pl.when(j + 1 < n)
    def _():
        _segment_copies(j + 1, y_hbm, ybuf.at[1 - slot], sem.at[1 - slot], False, base_ref, loc_ref, units_ref)

    _wait_unit_copies(tile_units_ref[j], y_hbm, ybuf.at[slot], sem.at[slot])
    lrow = lrow_ref[...]
    wt = wt_ref[...]
    col = lax.broadcasted_iota(jnp.int32, (tt, rt), 1)
    pick0 = (col == lrow[:, 0:1]).astype(BF16)
    pick1 = (col == lrow[:, 1:2]).astype(BF16)
    for c in range(D_MODEL // OUT_TN):
        cols = slice(c * OUT_TN, (c + 1) * OUT_TN)
        y = ybuf[slot, :, cols]
        y0 = jnp.dot(pick0, y, preferred_element_type=F32)
        y1 = jnp.dot(pick1, y, preferred_element_type=F32)
        o_ref[:, cols] = h_ref[:, cols] + wt[:, 0:1] * y0 + wt[:, 1:2] * y1
    if final_norm:
        o_ref[...] = _rms_norm(o_ref[...], g_ref[...])


def _combine(h, wts, lrows, y_sorted, tables, g, tt, final_norm):
    rows = h.shape[0]
    rt = _moe_tile_rows(tt)
    grid_spec = pltpu.PrefetchScalarGridSpec(
        num_scalar_prefetch=4,
        grid=(rows // tt,),
        in_specs=[
            pl.BlockSpec((tt, D_MODEL), lambda j, *_: (j, 0)),
            pl.BlockSpec((tt, LANES), lambda j, *_: (j, 0)),
            pl.BlockSpec((tt, LANES), lambda j, *_: (j, 0)),
            pl.BlockSpec((1, D_MODEL), lambda j, *_: (0, 0)),
            pl.BlockSpec(memory_space=pl.ANY),
        ],
        out_specs=pl.BlockSpec((tt, D_MODEL), lambda j, *_: (j, 0)),
        scratch_shapes=[pltpu.VMEM((2, rt, D_MODEL), MOE_DTYPE), pltpu.SemaphoreType.DMA((2,))],
    )
    return pl.pallas_call(
        functools.partial(_combine_kernel, final_norm=final_norm),
        grid_spec=grid_spec,
        out_shape=jax.ShapeDtypeStruct((rows, D_MODEL), F32),
        compiler_params=_params(("arbitrary",)),
        name="moe_combine",
    )(tables["base"], tables["loc"], tables["units"], tables["tile_units"], h, wts, lrows, g, y_sorted)


def _moe(h, hn, wts, ids, ranks, tile_cnt, wg, wu, wd, layer, final_g):
    rows = h.shape[0]
    n_tiles = tile_cnt.shape[0]
    tt = rows // n_tiles
    n_rows = 2 * rows + n_tiles * N_EXPERTS * (MOE_UNIT - 1) + N_EXPERTS * (MOE_BM - 1)
    n_blk = -(-n_rows // MOE_BM)
    tables = _moe_tables(tile_cnt, n_blk)
    x_sorted, lrows = _dispatch(hn, ids, ranks, tables, tt, n_blk * MOE_BM)
    y_sorted = _experts(x_sorted, tables, wg, wu, wd, layer, n_blk)
    final_norm = final_g is not None
    g = final_g if final_norm else jnp.ones((1, D_MODEL), F32)
    return _combine(h, wts, lrows, y_sorted, tables, g, tt, final_norm)


def _rope_tables(batch, lp):
    half = HEAD_DIM // 2
    inv = ROPE_THETA ** (-jnp.arange(half, dtype=F32) / half)
    pos = (jnp.arange(lp, dtype=jnp.int32) - PAD).astype(F32)
    ang = pos[:, None] * inv[None, :]
    cos = jnp.tile(jnp.cos(ang), (batch, LANES // half))
    sin = jnp.sin(ang)
    sin = jnp.tile(jnp.concatenate([-sin, sin], axis=1), (batch, LANES // HEAD_DIM))
    return cos, sin


def kernel(x, meta_tokens, attn_norm_g, ffn_norm_g, w_in, swa_sinks, ssm_lambda_re, ssm_lambda_im, ssm_b_re, ssm_b_im, ssm_c_re, ssm_c_im, ssm_d, ssm_log_dt, ssm_w_glu, diff_lambda_q1, diff_lambda_k1, diff_lambda_q2, diff_lambda_k2, diff_subln_g, w_out, router_group_w, router_group_b, router_expert_w, router_expert_b, moe_w_gate, moe_w_up, moe_w_down, final_norm_g):
    batch, seq, d = x.shape
    depth = w_in.shape[0]
    lp = PAD + N_META + seq
    nb = lp // BLK
    meta = jnp.broadcast_to(meta_tokens[None].astype(x.dtype), (batch, N_META, d))
    h = jnp.concatenate([jnp.zeros((batch, PAD, d), x.dtype), meta, x], axis=1).reshape(batch * lp, d)
    cos, sin = _rope_tables(batch, lp)
    for l in range(depth):
        proj = _norm_inproj(h, attn_norm_g[l][None], w_in[l].astype(BF16), cos, sin)
        o_a = _swa(proj, swa_sinks[l].astype(F32), batch, nb)
        kflat, bw, cw, a_t = _ssm_operators(ssm_lambda_re[l], ssm_lambda_im[l], ssm_b_re[l], ssm_b_im[l],
                                            ssm_c_re[l], ssm_c_im[l], ssm_log_dt[l])
        yg = _ssm(proj, kflat, bw, cw, a_t, ssm_d[l].astype(F32).reshape(SSM_PACKS, 1, LANES), batch, lp)
        lam_init = 0.8 - 0.6 * math.exp(-0.3 * l)
        lam_vecs = jnp.stack([diff_lambda_q1[l], diff_lambda_k1[l], diff_lambda_q2[l], diff_lambda_k2[l]]).astype(F32)
        o_c = _diff_attn(proj, lam_vecs, diff_subln_g[l][None].astype(F32), lam_init, batch, nb)
        wo = w_out[l].astype(BF16)
        rw = jnp.concatenate([router_group_w[l], router_expert_w[l]], axis=1).astype(F32)
        rw = jnp.pad(rw, ((0, 0), (0, LANES - rw.shape[1])))
        rb = jnp.concatenate([router_group_b[l], router_expert_b[l]]).astype(F32)
        rb = jnp.pad(rb, (0, LANES - rb.shape[0]))[None]
        rw_hi, rw_lo = _split_bf16(rw)
        g_ffn = ffn_norm_g[l][None].astype(F32)
        h, hn, wts, ids, ranks, tile_cnt = _mixer_out(h, o_a, yg, o_c, wo, ssm_w_glu[l].astype(BF16), g_ffn, rw_hi,
                                                      rw_lo, rb)
        final_g = final_norm_g[None].astype(F32) if l == depth - 1 else None
        h = _moe(h, hn, wts, ids, ranks, tile_cnt, moe_w_gate, moe_w_up, moe_w_down, l, final_g)
    return h.reshape(batch, lp, d)[:, PAD + N_META:]
```

```python
import functools
import math

import jax
import jax.numpy as jnp
from jax import lax
from jax.experimental import pallas as pl
from jax.experimental.pallas import tpu as pltpu

D_MODEL = 2048
N_META = 16
BLK = 128
PAD = BLK - N_META
ROPE_THETA = 10000.0
NORM_EPS = 1e-5
NEG = -1e30
SWA_HEADS = 12
SWA_KV_HEADS = 4
HEAD_DIM = 64
SWA_WIDTH = SWA_HEADS * HEAD_DIM
SSM_WIDTH = 768
SSM_GROUP = 16
SSM_GROUPS = SSM_WIDTH // SSM_GROUP
SSM_STATE = 64
DIFF_HEADS = 4
DIFF_V_DIM = 128
DIFF_WIDTH = DIFF_HEADS * DIFF_V_DIM
IN_WIDTH = 3584
N_EXPERT_GROUPS = 4
EXPERTS_PER_GROUP = 8
N_EXPERTS = N_EXPERT_GROUPS * EXPERTS_PER_GROUP
D_EXPERT = 512

COL_KA = 768
COL_VA = 1024
COL_US = 1280
COL_QD = 2048
COL_KD = 2560
COL_VD = 3072

LANES = 128
SSM_T = 16
SSM_PACK = LANES // SSM_GROUP
SSM_PACKS = SSM_GROUPS // SSM_PACK
SSM_PSTATE = SSM_PACK * SSM_STATE
MOE_BM = 256
VMEM_LIMIT = 56 * 1024 * 1024

F32 = jnp.float32
BF16 = jnp.bfloat16


def _row_tile(rows):
    for t in (512, 256, 128):
        if rows % t == 0:
            return t
    raise ValueError(f"row count {rows} is not a multiple of 128")


def _params(sem, vmem=VMEM_LIMIT):
    return pltpu.CompilerParams(dimension_semantics=sem, vmem_limit_bytes=vmem)


def _rms_norm(x, g):
    return x * lax.rsqrt(jnp.mean(x * x, axis=-1, keepdims=True) + NORM_EPS) * g


IN_TN = 512
ROPE_TILES = (0, 1, 4, 5)


def _norm_inproj_kernel(x_ref, g_ref, w_ref, cos_ref, sin_ref, att_ref, us_ref, qd_ref, kd_ref, vd_ref, xn_ref):
    blocked = ((COL_VD, vd_ref), (COL_KD, kd_ref), (COL_QD, qd_ref), (COL_US, us_ref))

    def store(col, piece):
        for start, ref in blocked:
            if col >= start:
                ref[(col - start) // LANES] = piece
                return
        att_ref[:, col:col + LANES] = piece

    x = x_ref[...]
    ms = jnp.mean(x * x, axis=-1, keepdims=True)
    xn_ref[...] = (x * lax.rsqrt(ms + NORM_EPS) * g_ref[...]).astype(BF16)
    shape = (x.shape[0], IN_TN)
    lane = lax.broadcasted_iota(jnp.int32, shape, 1)
    first_half = (lane & (HEAD_DIM - 1)) < (HEAD_DIM // 2)
    reps = IN_TN // LANES
    cos = jnp.tile(cos_ref[...], (1, reps))
    sin = jnp.tile(sin_ref[...], (1, reps))
    for j in range(IN_WIDTH // IN_TN):
        cols = slice(j * IN_TN, (j + 1) * IN_TN)
        acc = jnp.dot(xn_ref[...], w_ref[:, cols], preferred_element_type=F32)
        if j in ROPE_TILES:
            partner = jnp.where(first_half, pltpu.roll(acc, IN_TN - HEAD_DIM // 2, 1),
                                pltpu.roll(acc, HEAD_DIM // 2, 1))
            acc = acc * cos + partner * sin
        acc = acc.astype(BF16)
        for c in range(IN_TN // LANES):
            store(j * IN_TN + c * LANES, acc[:, c * LANES:(c + 1) * LANES])


def _norm_inproj(h, g, w, cos, sin):
    rows = h.shape[0]
    tm = _row_tile(rows)

    def blocked(width):
        n = width // LANES
        return pl.BlockSpec((n, tm, LANES), lambda i: (0, i, 0)), jax.ShapeDtypeStruct((n, rows, LANES), BF16)

    specs, shapes = zip(
        (pl.BlockSpec((tm, COL_US), lambda i: (i, 0)), jax.ShapeDtypeStruct((rows, COL_US), BF16)),
        blocked(SSM_WIDTH), blocked(DIFF_WIDTH), blocked(DIFF_WIDTH), blocked(DIFF_WIDTH))
    return pl.pallas_call(
        _norm_inproj_kernel,
        grid=(rows // tm,),
        in_specs=[
            pl.BlockSpec((tm, D_MODEL), lambda i: (i, 0)),
            pl.BlockSpec((1, D_MODEL), lambda i: (0, 0)),
            pl.BlockSpec((D_MODEL, IN_WIDTH), lambda i: (0, 0)),
            pl.BlockSpec((tm, LANES), lambda i: (i, 0)),
            pl.BlockSpec((tm, LANES), lambda i: (i, 0)),
        ],
        out_specs=list(specs),
        out_shape=list(shapes),
        scratch_shapes=[pltpu.VMEM((tm, D_MODEL), BF16)],
        compiler_params=_params(("parallel",)),
        name="norm_inproj",
    )(h, g, w, cos, sin)


def _swa_kernel(sink_ref, q_ref, kp_ref, kc_ref, vp_ref, vc_ref, o_ref):
    n = pl.program_id(1)
    shape = (BLK, 2 * BLK)
    qi = lax.broadcasted_iota(jnp.int32, shape, 0)
    c = lax.broadcasted_iota(jnp.int32, shape, 1)
    kj = (n - 1) * BLK + c
    delta = qi + BLK - c
    ok = (delta >= 0) & (delta < BLK) & (kj >= PAD)
    grp = SWA_HEADS // SWA_KV_HEADS
    outs = []
    for j in range(SWA_KV_HEADS):
        cols = slice(j * HEAD_DIM, (j + 1) * HEAD_DIM)
        kk = jnp.concatenate([kp_ref[:, cols], kc_ref[:, cols]], axis=0)
        vv = jnp.concatenate([vp_ref[:, cols], vc_ref[:, cols]], axis=0)
        for g in range(grp):
            hd = j * grp + g
            qh = q_ref[:, hd * HEAD_DIM:(hd + 1) * HEAD_DIM]
            s = lax.dot_general(qh, kk, (((1,), (1,)), ((), ())), preferred_element_type=F32) * (HEAD_DIM ** -0.5)
            s = jnp.where(ok, s, NEG)
            sink = sink_ref[hd]
            m = jnp.maximum(jnp.max(s, axis=-1, keepdims=True), sink)
            p = jnp.exp(s - m)
            denom = jnp.sum(p, axis=-1, keepdims=True) + jnp.exp(sink - m)
            outs.append(jnp.dot(p.astype(BF16), vv, preferred_element_type=F32) / denom)
    o_ref[...] = jnp.concatenate(outs, axis=1).astype(o_ref.dtype)


def _swa(proj, sinks, batch, nb):
    rows = proj.shape[0]
    kvw = SWA_KV_HEADS * HEAD_DIM

    def cur(col):
        return lambda b, n: (b * nb + n, col)

    def prev(col):
        return lambda b, n: (b * nb + jnp.maximum(n - 1, 0), col)

    return pl.pallas_call(
        _swa_kernel,
        grid=(batch, nb),
        in_specs=[
            pl.BlockSpec(memory_space=pltpu.SMEM),
            pl.BlockSpec((BLK, SWA_WIDTH), cur(0)),
            pl.BlockSpec((BLK, kvw), prev(COL_KA // kvw)),
            pl.BlockSpec((BLK, kvw), cur(COL_KA // kvw)),
            pl.BlockSpec((BLK, kvw), prev(COL_VA // kvw)),
            pl.BlockSpec((BLK, kvw), cur(COL_VA // kvw)),
        ],
        out_specs=pl.BlockSpec((BLK, SWA_WIDTH), cur(0)),
        out_shape=jax.ShapeDtypeStruct((rows, SWA_WIDTH), BF16),
        compiler_params=_params(("parallel", "parallel")),
        name="swa",
    )(sinks, proj, proj, proj, proj, proj)


def _diff_kernel(lam_ref, g_ref, q_ref, k_ref, v_ref, o_ref, *, lam_init, nb):
    row = lax.broadcasted_iota(jnp.int32, (BLK, BLK), 0)
    col = lax.broadcasted_iota(jnp.int32, (BLK, BLK), 1)
    causal_ok = col <= row
    pad_ok = col >= PAD
    lane = lax.broadcasted_iota(jnp.int32, (BLK, LANES), 1)
    lv = lam_ref[...]
    lam = (jnp.exp(jnp.sum(lv[0:1] * lv[1:2], axis=-1, keepdims=True))
           - jnp.exp(jnp.sum(lv[2:3] * lv[3:4], axis=-1, keepdims=True)) + lam_init)
    gain = g_ref[...] * (1.0 - lam_init)
    scale = HEAD_DIM ** -0.5
    for n in range(nb):
        ke = (n + 1) * BLK
        q = q_ref[n * BLK:(n + 1) * BLK, :] * scale
        kx = k_ref[0:ke, :]
        vx = v_ref[0:ke, :]
        heads = []
        for s_idx in range(2):
            qx = jnp.where((lane >= HEAD_DIM) == bool(s_idx), q, jnp.zeros_like(q))
            s = lax.dot_general(qx, kx, (((1,), (1,)), ((), ())), preferred_element_type=F32)
            if n == 0:
                s = jnp.where(causal_ok & pad_ok, s, NEG)
            else:
                first = jnp.where(pad_ok, s[:, :BLK], NEG)
                last = jnp.where(causal_ok, s[:, ke - BLK:], NEG)
                middle = [s[:, BLK:ke - BLK]] if n > 1 else []
                s = jnp.concatenate([first] + middle + [last], axis=1)
            m = jnp.max(s, axis=-1, keepdims=True)
            p = jnp.exp(s - m)
            denom = jnp.sum(p, axis=-1, keepdims=True)
            heads.append(jnp.dot(p.astype(BF16), vx, preferred_element_type=F32) / denom)
        o = heads[0] - lam * heads[1]
        o = o * lax.rsqrt(jnp.mean(o * o, axis=-1, keepdims=True) + NORM_EPS) * gain
        o_ref[n * BLK:(n + 1) * BLK, :] = o.astype(o_ref.dtype)


def _diff_attn(qd, kd, vd, lam_vecs, subln_g, lam_init, batch, nb):
    rows = qd.shape[1]
    lp = nb * BLK
    head_rows = pl.BlockSpec((None, lp, LANES), lambda b, h: (h, b, 0))
    return pl.pallas_call(
        functools.partial(_diff_kernel, lam_init=lam_init, nb=nb),
        grid=(batch, DIFF_HEADS),
        in_specs=[
            pl.BlockSpec((4, HEAD_DIM), lambda b, h: (0, 0)),
            pl.BlockSpec((1, DIFF_V_DIM), lambda b, h: (0, 0)),
            head_rows, head_rows, head_rows,
        ],
        out_specs=head_rows,
        out_shape=jax.ShapeDtypeStruct((DIFF_HEADS, rows, DIFF_V_DIM), BF16),
        compiler_params=_params(("parallel", "parallel")),
        name="diff_attn",
    )(lam_vecs, subln_g, qd, kd, vd)


def _ssm_row_chunk(rows):
    for c in (544, 512, 384, 256, 128, 64, 32, 16):
        if rows % c == 0:
            return c
    raise ValueError(rows)


def _ssm_expand_operators(kc_ref, bc_ref, cc_ref, k_ref, bw_ref, cw_ref):
    def iota(shape, dim):
        return lax.broadcasted_iota(jnp.int32, shape, dim)

    def group(idx, width):
        return jnp.right_shift(idx, width.bit_length() - 1) & (SSM_PACK - 1)

    hs, ps, tl = SSM_GROUP, SSM_STATE, SSM_T * LANES
    spread_h = ((iota((LANES, hs), 0) & (hs - 1)) == iota((LANES, hs), 1)).astype(BF16)
    kt = jnp.dot(spread_h, kc_ref[...], preferred_element_type=F32)
    same = group(iota((LANES, tl), 0), hs) == group(iota((LANES, tl), 1), hs)
    k_ref[...] = jnp.where(same, kt, 0.0).astype(BF16)
    spread_s = (iota((ps, SSM_PSTATE), 0) == (iota((ps, SSM_PSTATE), 1) & (ps - 1))).astype(BF16)
    shape = (tl, SSM_PSTATE)
    same = group(iota(shape, 0), hs) == group(iota(shape, 1), ps)
    for part in range(2):
        bw = jnp.dot(bc_ref[part], spread_s, preferred_element_type=F32)
        bw_ref[:, part * SSM_PSTATE:(part + 1) * SSM_PSTATE] = jnp.where(same, bw, 0.0).astype(BF16)
    shape = (LANES, SSM_PSTATE)
    same = group(iota(shape, 0), hs) == group(iota(shape, 1), ps)
    for j in range(SSM_T):
        for part in range(2):
            ct = jnp.dot(spread_h, cc_ref[part, j], preferred_element_type=F32)
            cw_ref[j, :, part * SSM_PSTATE:(part + 1) * SSM_PSTATE] = jnp.where(same, ct, 0.0).astype(BF16)


def _ssm_kernel(u_ref, kc_ref, bc_ref, cc_ref, a_ref, d_ref, o_ref, up_ref, y_ref, e_ref, s_ref, k_ref, bw_ref,
                cw_ref, *, batch, lp):
    rows = batch * lp
    n_chunks = lp // SSM_T
    tot_chunks = rows // SSM_T
    _ssm_expand_operators(kc_ref, bc_ref, cc_ref, k_ref, bw_ref, cw_ref)
    contract_lanes = (((1,), (1,)), ((), ()))
    up_ref[0:SSM_T, :] = jnp.zeros((SSM_T, LANES), F32)
    up_ref[SSM_T:, :] = u_ref[...].astype(F32)
    for b in range(batch):
        up_ref[SSM_T + b * lp:SSM_T + b * lp + PAD, :] = jnp.zeros((PAD, LANES), F32)

    ch = _ssm_row_chunk(rows)
    rmod = lax.broadcasted_iota(jnp.int32, (ch, LANES), 0) & (SSM_T - 1)
    for r0 in range(0, rows, ch):
        pieces = []
        for k in range(SSM_T):
            sl = up_ref[SSM_T - k + r0:SSM_T - k + r0 + ch, :]
            pieces.append(jnp.where(rmod >= k, sl, 0.0).astype(BF16))
        y_ref[r0:r0 + ch, :] = lax.dot_general(jnp.concatenate(pieces, axis=1), k_ref[...], contract_lanes,
                                               preferred_element_type=F32)

    pieces = [up_ref[pl.ds(SSM_T + i, tot_chunks, stride=SSM_T), :].astype(BF16) for i in range(SSM_T)]
    e_ref[...] = jnp.dot(jnp.concatenate(pieces, axis=1), bw_ref[...], preferred_element_type=F32)

    a_re = a_ref[:, :SSM_PSTATE]
    a_im = a_ref[:, SSM_PSTATE:]

    def step(c, carry):
        new = []
        for b in range(batch):
            sr, si = carry[b]
            idx = b * n_chunks + c
            s_ref[pl.ds(idx, 1), :SSM_PSTATE] = sr
            s_ref[pl.ds(idx, 1), SSM_PSTATE:] = si
            er = e_ref[pl.ds(idx, 1), :SSM_PSTATE]
            ei = e_ref[pl.ds(idx, 1), SSM_PSTATE:]
            new.append((a_re * sr - a_im * si + er, a_re * si + a_im * sr + ei))
        return tuple(new)

    zero = jnp.zeros((1, SSM_PSTATE), F32)
    lax.fori_loop(0, n_chunks, step, tuple((zero, zero) for _ in range(batch)))

    sb = s_ref[...].astype(BF16)
    for j in range(SSM_T):
        z = lax.dot_general(sb, cw_ref[j], contract_lanes, preferred_element_type=F32)
        idx = pl.ds(j, tot_chunks, stride=SSM_T)
        y_ref[idx, :] = y_ref[idx, :] + z

    y = y_ref[...] + d_ref[...] * up_ref[SSM_T:, :]
    o_ref[...] = jax.nn.gelu(y, approximate=True).astype(o_ref.dtype)


def _ssm(us, kc, bc, cc, a_t, dskip, batch, lp):
    rows = us.shape[1]
    tot_chunks = rows // SSM_T
    return pl.pallas_call(
        functools.partial(_ssm_kernel, batch=batch, lp=lp),
        grid=(SSM_PACKS,),
        in_specs=[
            pl.BlockSpec((None, rows, LANES), lambda p: (p, 0, 0)),
            pl.BlockSpec((None, SSM_GROUP, SSM_T * LANES), lambda p: (p, 0, 0)),
            pl.BlockSpec((None, 2, SSM_T * LANES, SSM_STATE), lambda p: (p, 0, 0, 0)),
            pl.BlockSpec((None, 2, SSM_T, SSM_GROUP, SSM_PSTATE), lambda p: (p, 0, 0, 0, 0)),
            pl.BlockSpec((None, 1, 2 * SSM_PSTATE), lambda p: (p, 0, 0)),
            pl.BlockSpec((None, 1, LANES), lambda p: (p, 0, 0)),
        ],
        out_specs=pl.BlockSpec((None, rows, LANES), lambda p: (p, 0, 0)),
        out_shape=jax.ShapeDtypeStruct((SSM_PACKS, rows, LANES), F32),
        scratch_shapes=[
            pltpu.VMEM((rows + SSM_T, LANES), F32),
            pltpu.VMEM((rows, LANES), F32),
            pltpu.VMEM((tot_chunks, 2 * SSM_PSTATE), F32),
            pltpu.VMEM((tot_chunks, 2 * SSM_PSTATE), F32),
            pltpu.VMEM((LANES, SSM_T * LANES), BF16),
            pltpu.VMEM((SSM_T * LANES, 2 * SSM_PSTATE), BF16),
            pltpu.VMEM((SSM_T, LANES, 2 * SSM_PSTATE), BF16),
        ],
        compiler_params=_params(("parallel",)),
        name="ssm",
    )(us, kc, bc, cc, a_t, dskip)


def _ssm_operators(lam_re, lam_im, b_re, b_im, c_re, c_im, log_dt):
    p, h = SSM_STATE, SSM_GROUP
    dt = jnp.exp(log_dt.astype(F32))[:, None]
    lr, li = lam_re.astype(F32), lam_im.astype(F32)
    mag = jnp.exp(lr * dt)
    abar_re, abar_im = mag * jnp.cos(li * dt), mag * jnp.sin(li * dt)
    den = lr * lr + li * li
    nr = abar_re - 1.0
    f_re = (nr * lr + abar_im * li) / den
    f_im = (abar_im * lr - nr * li) / den
    br, bi = b_re.astype(F32), b_im.astype(F32)
    bbar_re = f_re[..., None] * br - f_im[..., None] * bi
    bbar_im = f_re[..., None] * bi + f_im[..., None] * br
    ks = jnp.arange(SSM_T + 1, dtype=F32)[:, None, None]
    pmag = jnp.exp(ks * (lr * dt)[None])
    pw_re = pmag * jnp.cos(ks * (li * dt)[None])
    pw_im = pmag * jnp.sin(ks * (li * dt)[None])
    cr, ci = c_re.astype(F32), c_im.astype(F32)

    packs, q = SSM_PACKS, SSM_PACK

    cb_re = cr[:, :, :, None] * bbar_re[:, None, :, :] - ci[:, :, :, None] * bbar_im[:, None, :, :]
    cb_im = cr[:, :, :, None] * bbar_im[:, None, :, :] + ci[:, :, :, None] * bbar_re[:, None, :, :]
    klag = (jnp.einsum('kgs,gosi->kgoi', pw_re[:SSM_T], cb_re)
            - jnp.einsum('kgs,gosi->kgoi', pw_im[:SSM_T], cb_im))
    kc = klag.reshape(SSM_T, packs, q, h, h).transpose(1, 3, 0, 2, 4).reshape(packs, h, SSM_T * LANES)

    rev_re, rev_im = pw_re[SSM_T - 1::-1, :, None, :], pw_im[SSM_T - 1::-1, :, None, :]
    bt_re, bt_im = jnp.swapaxes(bbar_re, 1, 2)[None], jnp.swapaxes(bbar_im, 1, 2)[None]

    def inj_by_pack(x):
        return x.reshape(SSM_T, packs, q * h, p).transpose(1, 0, 2, 3).reshape(packs, SSM_T * LANES, p)

    bc = jnp.stack([inj_by_pack(rev_re * bt_re - rev_im * bt_im), inj_by_pack(rev_re * bt_im + rev_im * bt_re)], axis=1)

    ca_re = cr[None] * pw_re[1:, :, None, :] - ci[None] * pw_im[1:, :, None, :]
    ca_im = cr[None] * pw_im[1:, :, None, :] + ci[None] * pw_re[1:, :, None, :]

    def out_by_pack(x):
        return x.reshape(SSM_T, packs, q, h, p).transpose(1, 0, 3, 2, 4).reshape(packs, SSM_T, h, SSM_PSTATE)

    cc = jnp.stack([out_by_pack(ca_re), out_by_pack(-ca_im)], axis=1)

    a_t = jnp.concatenate([pw_re[SSM_T].reshape(packs, 1, SSM_PSTATE),
                           pw_im[SSM_T].reshape(packs, 1, SSM_PSTATE)], axis=-1)
    return kc.astype(BF16), bc.astype(BF16), cc.astype(BF16), a_t


OUT_TN = 512


def _split_bf16(x):
    hi = x.astype(BF16)
    return hi, (x - hi.astype(F32)).astype(BF16)


def _route(hn, rwh_ref, rwl_ref, rb_ref):
    hi, lo = _split_bf16(hn)
    logits = (jnp.dot(hi, rwh_ref[...], preferred_element_type=F32)
              + (jnp.dot(lo, rwh_ref[...], preferred_element_type=F32)
                 + jnp.dot(hi, rwl_ref[...], preferred_element_type=F32))) + rb_ref[...]
    lane = lax.broadcasted_iota(jnp.int32, logits.shape, 1)

    def first_argmax(v, vmax):
        return jnp.min(jnp.where(v == vmax, lane, LANES), axis=-1, keepdims=True)

    gl = jnp.where(lane < N_EXPERT_GROUPS, logits, NEG)
    gmax = jnp.max(gl, axis=-1, keepdims=True)
    g_sel = first_argmax(gl, gmax)
    p_g = 1.0 / jnp.sum(jnp.exp(gl - gmax), axis=-1, keepdims=True)
    e_lane = lane - N_EXPERT_GROUPS
    in_group = (e_lane >= 0) & (e_lane < N_EXPERTS) & (jnp.right_shift(e_lane, 3) == g_sel)
    el = jnp.where(in_group, logits, NEG)
    m1 = jnp.max(el, axis=-1, keepdims=True)
    i1 = first_argmax(el, m1)
    el2 = jnp.where(lane == i1, NEG, el)
    m2 = jnp.max(el2, axis=-1, keepdims=True)
    i2 = first_argmax(el2, m2)
    z = jnp.sum(jnp.exp(el - m1), axis=-1, keepdims=True)
    p1 = 1.0 / z
    p2 = jnp.exp(m2 - m1) / z
    w1 = p1 / (p1 + p2) * p_g
    w2 = p2 / (p1 + p2) * p_g
    wts = jnp.where(lane == 0, w1, jnp.where(lane == 1, w2, 0.0))
    ids = jnp.where(lane == 0, i1 - N_EXPERT_GROUPS, jnp.where(lane == 1, i2 - N_EXPERT_GROUPS, 0))
    return wts, ids


def _assignment_ranks(ids):
    tm = ids.shape[0]
    lane = lax.broadcasted_iota(jnp.int32, ids.shape, 1)
    oh0 = lane == ids[:, 0:1]
    oh1 = lane == ids[:, 1:2]
    hits = (oh0 | oh1).astype(BF16)
    r = lax.broadcasted_iota(jnp.int32, (tm, tm), 0)
    c = lax.broadcasted_iota(jnp.int32, (tm, tm), 1)
    before = jnp.dot((c < r).astype(BF16), hits, preferred_element_type=F32)
    rank0 = jnp.sum(jnp.where(oh0, before, 0.0), axis=-1, keepdims=True)
    rank1 = jnp.sum(jnp.where(oh1, before, 0.0), axis=-1, keepdims=True)
    ranks = jnp.where(lane == 0, rank0, jnp.where(lane == 1, rank1, 0.0)).astype(jnp.int32)
    return ranks, jnp.sum(hits.astype(F32), axis=0, keepdims=True).astype(jnp.int32)


def _mixer_out_kernel(h_ref, a_ref, y_ref, c_ref, wo_ref, wglu_ref, g_ref, rwh_ref, rwl_ref, rb_ref,
                      o_ref, hn_ref, wt_ref, id_ref, rank_ref, cnt_ref):
    y = jnp.concatenate([y_ref[p] for p in range(y_ref.shape[0])], axis=1)
    c = jnp.concatenate([c_ref[hd] for hd in range(c_ref.shape[0])], axis=1)
    gate = jnp.dot(y.astype(BF16), wglu_ref[...], preferred_element_type=F32)
    b = (y * jax.nn.sigmoid(gate)).astype(BF16)
    mixed = jnp.concatenate([a_ref[...], b, c], axis=1)
    for j in range(D_MODEL // OUT_TN):
        cols = slice(j * OUT_TN, (j + 1) * OUT_TN)
        o_ref[:, cols] = h_ref[:, cols] + jnp.dot(mixed, wo_ref[:, cols], preferred_element_type=F32)
    x = o_ref[...]
    hn = _rms_norm(x, g_ref[...])
    hn_ref[...] = hn.astype(BF16)
    wts, ids = _route(hn, rwh_ref, rwl_ref, rb_ref)
    wt_ref[...] = wts
    id_ref[...] = ids
    ranks, counts = _assignment_ranks(ids)
    rank_ref[...] = ranks
    cnt_ref[...] = jnp.broadcast_to(counts, cnt_ref.shape)


def _mixer_out(h, oa, yg, oc, wo, wglu, g, rw_hi, rw_lo, rb):
    rows = h.shape[0]
    tm = _row_tile(rows)

    def row_spec(width):
        return pl.BlockSpec((tm, width), lambda i: (i, 0))

    def whole(shape):
        return pl.BlockSpec(shape, lambda i: (0, 0))

    def blocked(width):
        return pl.BlockSpec((width // LANES, tm, LANES), lambda i: (0, i, 0))

    return pl.pallas_call(
        _mixer_out_kernel,
        grid=(rows // tm,),
        in_specs=[
            row_spec(D_MODEL), row_spec(SWA_WIDTH), blocked(SSM_WIDTH), blocked(DIFF_WIDTH),
            whole((D_MODEL, D_MODEL)), whole((SSM_WIDTH, SSM_WIDTH)), whole((1, D_MODEL)),
            whole((D_MODEL, LANES)), whole((D_MODEL, LANES)), whole((1, LANES)),
        ],
        out_specs=[row_spec(D_MODEL), row_spec(D_MODEL), row_spec(LANES), row_spec(LANES), row_spec(LANES),
                   pl.BlockSpec((None, 8, LANES), lambda i: (i, 0, 0))],
        out_shape=[
            jax.ShapeDtypeStruct((rows, D_MODEL), F32),
            jax.ShapeDtypeStruct((rows, D_MODEL), BF16),
            jax.ShapeDtypeStruct((rows, LANES), F32),
            jax.ShapeDtypeStruct((rows, LANES), jnp.int32),
            jax.ShapeDtypeStruct((rows, LANES), jnp.int32),
            jax.ShapeDtypeStruct((rows // tm, 8, LANES), jnp.int32),
        ],
        compiler_params=_params(("parallel",)),
        name="mixer_out",
    )(h, oa, yg, oc, wo, wglu, g, rw_hi, rw_lo, rb)


MOE_UNIT = 16
MOE_DTYPE = BF16


def _moe_tile_rows(tt):
    return 2 * tt + N_EXPERTS * MOE_UNIT


def _moe_tables(tile_cnt, n_blk):
    cnt = tile_cnt[:, 0, :N_EXPERTS]
    seg = (cnt + MOE_UNIT - 1) // MOE_UNIT * MOE_UNIT
    size = jnp.sum(seg, axis=0)
    padded = (size + MOE_BM - 1) // MOE_BM * MOE_BM
    pend = jnp.cumsum(padded)
    pstart = pend - padded
    base = pstart[None, :] + jnp.cumsum(seg, axis=0) - seg
    loc = jnp.cumsum(seg, axis=1) - seg
    units = seg // MOE_UNIT
    blk_start = jnp.arange(n_blk, dtype=jnp.int32) * MOE_BM
    blk_expert = jnp.minimum(jnp.sum((pend[None, :] <= blk_start[:, None]).astype(jnp.int32), axis=1), N_EXPERTS - 1)
    ex = jnp.arange(N_EXPERTS, dtype=jnp.int32)
    nonempty = size > 0
    run_parity = (jnp.cumsum(nonempty.astype(jnp.int32)) - 1) & 1
    later = (ex[None, :] > ex[:, None]) & nonempty[None, :]
    nxt = jnp.min(jnp.where(later, ex[None, :], N_EXPERTS), axis=1)
    nxt = jnp.where(nxt == N_EXPERTS, -1, nxt)
    i32 = jnp.int32
    loc_lanes = jnp.pad(loc, ((0, 0), (0, LANES - N_EXPERTS)))
    return dict(
        base=base.reshape(-1).astype(i32), loc=loc.reshape(-1).astype(i32), units=units.reshape(-1).astype(i32),
        tile_units=jnp.sum(units, axis=1).astype(i32),
        tail_start=(pstart + size).astype(i32), tail_units=((padded - size) // MOE_UNIT).astype(i32),
        loc_lanes=jnp.broadcast_to(loc_lanes[:, None, :], (loc.shape[0], 8, LANES)).astype(i32),
        blk_expert=blk_expert.astype(i32), blk_parity=run_parity[blk_expert].astype(i32),
        blk_next=nxt[blk_expert].astype(i32), n_used=(pend[-1] // MOE_BM).astype(i32).reshape(1))


def _segment_copies(tile, hbm, buf, sem, to_hbm, base_ref, loc_ref, units_ref):
    for e in range(N_EXPERTS):
        k = tile * N_EXPERTS + e
        row_hbm, row_buf = base_ref[k], loc_ref[k]

        def start(u, carry):
            a = hbm.at[pl.ds(pl.multiple_of(row_hbm + u * MOE_UNIT, MOE_UNIT), MOE_UNIT), :]
            b = buf.at[pl.ds(pl.multiple_of(row_buf + u * MOE_UNIT, MOE_UNIT), MOE_UNIT), :]
            (pltpu.make_async_copy(b, a, sem) if to_hbm else pltpu.make_async_copy(a, b, sem)).start()
            return carry

        lax.fori_loop(0, units_ref[k], start, 0)


def _wait_unit_copies(count, hbm, buf, sem):
    def wait(u, carry):
        pltpu.make_async_copy(hbm.at[pl.ds(0, MOE_UNIT), :], buf.at[pl.ds(0, MOE_UNIT), :], sem).wait()
        return carry

    lax.fori_loop(0, count, wait, 0)


def _local_rows(ids, ranks, loc_lanes):
    lane = lax.broadcasted_iota(jnp.int32, ids.shape, 1)
    loc = loc_lanes.astype(F32)
    start0 = jnp.sum(jnp.where(lane == ids[:, 0:1], loc, 0.0), axis=-1, keepdims=True).astype(jnp.int32)
    start1 = jnp.sum(jnp.where(lane == ids[:, 1:2], loc, 0.0), axis=-1, keepdims=True).astype(jnp.int32)
    return start0 + ranks[:, 0:1], start1 + ranks[:, 1:2]


def _dispatch_kernel(base_ref, loc_ref, units_ref, tile_units_ref, tail_start_ref, tail_units_ref, nused_ref,
                     hn_ref, ids_ref, rank_ref, locv_ref, x_hbm, lrow_ref, ybuf, zbuf, sem):
    j = pl.program_id(0)
    n = pl.num_programs(0)
    slot = j % 2
    tt, rt = hn_ref.shape[0], ybuf.shape[1]

    @pl.when(j >= 2)
    def _():
        _wait_unit_copies(tile_units_ref[j - 2], x_hbm, ybuf.at[slot], sem.at[slot])

    row0, row1 = _local_rows(ids_ref[...], rank_ref[...], locv_ref[0:1, :])
    lane = lax.broadcasted_iota(jnp.int32, lrow_ref.shape, 1)
    lrow_ref[...] = jnp.where(lane == 0, row0, jnp.where(lane == 1, row1, 0))
    col = lax.broadcasted_iota(jnp.int32, (tt, rt), 1)
    onehot = ((col == row0) | (col == row1)).astype(BF16)
    compacted = lax.dot_general(onehot, hn_ref[...], (((0,), (0,)), ((), ())), preferred_element_type=F32)
    ybuf[slot] = compacted.astype(ybuf.dtype)
    _segment_copies(j, x_hbm, ybuf.at[slot], sem.at[slot], True, base_ref, loc_ref, units_ref)

    @pl.when(j == n - 1)
    def _():
        zbuf[...] = jnp.zeros(zbuf.shape, zbuf.dtype)
        zunit = zbuf.at[pl.ds(0, MOE_UNIT), :]
        total = 0
        for e in range(N_EXPERTS):
            row = tail_start_ref[e]

            def start(u, carry):
                dst = x_hbm.at[pl.ds(pl.multiple_of(row + u * MOE_UNIT, MOE_UNIT), MOE_UNIT), :]
                pltpu.make_async_copy(zunit, dst, sem.at[2]).start()
                return carry

            lax.fori_loop(0, tail_units_ref[e], start, 0)
            total = total + tail_units_ref[e]
        _wait_unit_copies(total, x_hbm, zunit, sem.at[2])
        n_blk = x_hbm.shape[0] // MOE_BM

        def block_copy(blk):
            dst = x_hbm.at[pl.ds(pl.multiple_of(blk * MOE_BM, MOE_BM), MOE_BM), :]
            return pltpu.make_async_copy(zbuf, dst, sem.at[3])

        def start_block(blk, carry):
            block_copy(blk).start()
            return carry

        def wait_block(blk, carry):
            block_copy(blk).wait()
            return carry

        lax.fori_loop(nused_ref[0], n_blk, start_block, 0)
        lax.fori_loop(nused_ref[0], n_blk, wait_block, 0)

        @pl.when(j >= 1)
        def _():
            _wait_unit_copies(tile_units_ref[jnp.maximum(j - 1, 0)], x_hbm, ybuf.at[1 - slot], sem.at[1 - slot])

        _wait_unit_copies(tile_units_ref[j], x_hbm, ybuf.at[slot], sem.at[slot])


def _dispatch(hn, ids, ranks, tables, tt, n_rows):
    rows = hn.shape[0]
    rt = _moe_tile_rows(tt)
    row_spec = pl.BlockSpec((tt, LANES), lambda j, *_: (j, 0))
    grid_spec = pltpu.PrefetchScalarGridSpec(
        num_scalar_prefetch=7,
        grid=(rows // tt,),
        in_specs=[pl.BlockSpec((tt, D_MODEL), lambda j, *_: (j, 0)), row_spec, row_spec,
                  pl.BlockSpec((None, 8, LANES), lambda j, *_: (j, 0, 0))],
        out_specs=[pl.BlockSpec(memory_space=pl.ANY), row_spec],
        scratch_shapes=[pltpu.VMEM((2, rt, D_MODEL), MOE_DTYPE), pltpu.VMEM((MOE_BM, D_MODEL), MOE_DTYPE),
                        pltpu.SemaphoreType.DMA((4,))],
    )
    return pl.pallas_call(
        _dispatch_kernel,
        grid_spec=grid_spec,
        out_shape=[jax.ShapeDtypeStruct((n_rows, D_MODEL), MOE_DTYPE),
                   jax.ShapeDtypeStruct((rows, LANES), jnp.int32)],
        compiler_params=_params(("arbitrary",)),
        name="moe_dispatch",
    )(tables["base"], tables["loc"], tables["units"], tables["tile_units"], tables["tail_start"],
      tables["tail_units"], tables["n_used"], hn, ids, ranks, tables["loc_lanes"])


WEIGHT_DMA_PRIORITY = 1


def _expert_kernel(be_ref, par_ref, nxt_ref, nused_ref, x_ref, wg_hbm, wu_hbm, wd_hbm, o_ref,
                   wgf, wuf, wdf, wgb, wub, wdb, wsem, *, layer):
    i = pl.program_id(0)
    n_used = nused_ref[0]

    def weight_copies(expert, wslot):
        pairs = ((wg_hbm, wgf), (wu_hbm, wuf), (wd_hbm, wdf))
        return [pltpu.make_async_copy(w.at[layer, expert], buf.at[wslot], wsem.at[wslot]) for w, buf in pairs]

    @pl.when(i == 0)
    def _():
        for c in weight_copies(be_ref[0], 0):
            c.start(priority=WEIGHT_DMA_PRIORITY)

    @pl.when(i < n_used)
    def _():
        @pl.when((i == 0) | (be_ref[i] != be_ref[jnp.maximum(i - 1, 0)]))
        def _():
            wslot = par_ref[i]
            for c in weight_copies(be_ref[i], wslot):
                c.wait()

            @pl.when(nxt_ref[i] >= 0)
            def _():
                for c in weight_copies(nxt_ref[i], 1 - wslot):
                    c.start(priority=WEIGHT_DMA_PRIORITY)

            wgb[...] = wgf[wslot].astype(BF16)
            wub[...] = wuf[wslot].astype(BF16)
            wdb[...] = wdf[wslot].astype(BF16)

        x = x_ref[...]
        gate = jnp.dot(x, wgb[...], preferred_element_type=F32)
        up = jnp.dot(x, wub[...], preferred_element_type=F32)
        act = (jax.nn.silu(gate) * up).astype(BF16)
        o_ref[...] = jnp.dot(act, wdb[...], preferred_element_type=F32).astype(o_ref.dtype)

    @pl.when(i >= n_used)
    def _():
        o_ref[...] = jnp.zeros(o_ref.shape, o_ref.dtype)


def _experts(x_sorted, tables, wg, wu, wd, layer, n_blk):
    hbm = pl.BlockSpec(memory_space=pl.ANY)
    grid_spec = pltpu.PrefetchScalarGridSpec(
        num_scalar_prefetch=4,
        grid=(n_blk,),
        in_specs=[pl.BlockSpec((MOE_BM, D_MODEL), lambda i, be, par, nxt, nu: (jnp.minimum(i, nu[0] - 1), 0)),
                  hbm, hbm, hbm],
        out_specs=pl.BlockSpec((MOE_BM, D_MODEL), lambda i, *_: (i, 0)),
        scratch_shapes=[
            pltpu.VMEM((2, D_MODEL, D_EXPERT), F32),
            pltpu.VMEM((2, D_MODEL, D_EXPERT), F32),
            pltpu.VMEM((2, D_EXPERT, D_MODEL), F32),
            pltpu.VMEM((D_MODEL, D_EXPERT), BF16),
            pltpu.VMEM((D_MODEL, D_EXPERT), BF16),
            pltpu.VMEM((D_EXPERT, D_MODEL), BF16),
            pltpu.SemaphoreType.DMA((2,)),
        ],
    )
    return pl.pallas_call(
        functools.partial(_expert_kernel, layer=layer),
        grid_spec=grid_spec,
        out_shape=jax.ShapeDtypeStruct((n_blk * MOE_BM, D_MODEL), MOE_DTYPE),
        compiler_params=_params(("arbitrary",)),
        name="moe_experts",
    )(tables["blk_expert"], tables["blk_parity"], tables["blk_next"], tables["n_used"], x_sorted, wg, wu, wd)


def _combine_kernel(base_ref, loc_ref, units_ref, tile_units_ref, h_ref, wt_ref, lrow_ref, g_ref, y_hbm, o_ref,
                    ybuf, sem, *, final_norm):
    j = pl.program_id(0)
    n = pl.num_programs(0)
    slot = j % 2
    tt, rt = h_ref.shape[0], ybuf.shape[1]

    @pl.when(j == 0)
    def _():
        ybuf[...] = jnp.zeros(ybuf.shape, ybuf.dtype)
        _segment_copies(0, y_hbm, ybuf.at[0], sem.at[0], False, base_ref, loc_ref, units_ref)

    @pl.when(j + 1 < n)
    def _():
        _segment_copies(j + 1, y_hbm, ybuf.at[1 - slot], sem.at[1 - slot], False, base_ref, loc_ref, units_ref)

    _wait_unit_copies(tile_units_ref[j], y_hbm, ybuf.at[slot], sem.at[slot])
    lrow = lrow_ref[...]
    wt = wt_ref[...]
    col = lax.broadcasted_iota(jnp.int32, (tt, rt), 1)
    pick0 = (col == lrow[:, 0:1]).astype(BF16)
    pick1 = (col == lrow[:, 1:2]).astype(BF16)
    for c in range(D_MODEL // OUT_TN):
        cols = slice(c * OUT_TN, (c + 1) * OUT_TN)
        y = ybuf[slot, :, cols]
        y0 = jnp.dot(pick0, y, preferred_element_type=F32)
        y1 = jnp.dot(pick1, y, preferred_element_type=F32)
        o_ref[:, cols] = h_ref[:, cols] + wt[:, 0:1] * y0 + wt[:, 1:2] * y1
    if final_norm:
        o_ref[...] = _rms_norm(o_ref[...], g_ref[...])


def _combine(h, wts, lrows, y_sorted, tables, g, tt, final_norm):
    rows = h.shape[0]
    rt = _moe_tile_rows(tt)
    grid_spec = pltpu.PrefetchScalarGridSpec(
        num_scalar_prefetch=4,
        grid=(rows // tt,),
        in_specs=[
            pl.BlockSpec((tt, D_MODEL), lambda j, *_: (j, 0)),
            pl.BlockSpec((tt, LANES), lambda j, *_: (j, 0)),
            pl.BlockSpec((tt, LANES), lambda j, *_: (j, 0)),
            pl.BlockSpec((1, D_MODEL), lambda j, *_: (0, 0)),
            pl.BlockSpec(memory_space=pl.ANY),
        ],
        out_specs=pl.BlockSpec((tt, D_MODEL), lambda j, *_: (j, 0)),
        scratch_shapes=[pltpu.VMEM((2, rt, D_MODEL), MOE_DTYPE), pltpu.SemaphoreType.DMA((2,))],
    )
    return pl.pallas_call(
        functools.partial(_combine_kernel, final_norm=final_norm),
        grid_spec=grid_spec,
        out_shape=jax.ShapeDtypeStruct((rows, D_MODEL), F32),
        compiler_params=_params(("arbitrary",)),
        name="moe_combine",
    )(tables["base"], tables["loc"], tables["units"], tables["tile_units"], h, wts, lrows, g, y_sorted)


def _moe(h, hn, wts, ids, ranks, tile_cnt, wg, wu, wd, layer, final_g):
    rows = h.shape[0]
    n_tiles = tile_cnt.shape[0]
    tt = rows // n_tiles
    n_rows = 2 * rows + n_tiles * N_EXPERTS * (MOE_UNIT - 1) + N_EXPERTS * (MOE_BM - 1)
    n_blk = -(-n_rows // MOE_BM)
    tables = _moe_tables(tile_cnt, n_blk)
    x_sorted, lrows = _dispatch(hn, ids, ranks, tables, tt, n_blk * MOE_BM)
    y_sorted = _experts(x_sorted, tables, wg, wu, wd, layer, n_blk)
    final_norm = final_g is not None
    g = final_g if final_norm else jnp.ones((1, D_MODEL), F32)
    return _combine(h, wts, lrows, y_sorted, tables, g, tt, final_norm)


def _rope_tables(batch, lp):
    half = HEAD_DIM // 2
    inv = ROPE_THETA ** (-jnp.arange(half, dtype=F32) / half)
    pos = (jnp.arange(lp, dtype=jnp.int32) - PAD).astype(F32)
    ang = pos[:, None] * inv[None, :]
    cos = jnp.tile(jnp.cos(ang), (batch, LANES // half))
    sin = jnp.sin(ang)
    sin = jnp.tile(jnp.concatenate([-sin, sin], axis=1), (batch, LANES // HEAD_DIM))
    return cos, sin


def kernel(x, meta_tokens, attn_norm_g, ffn_norm_g, w_in, swa_sinks, ssm_lambda_re, ssm_lambda_im, ssm_b_re, ssm_b_im, ssm_c_re, ssm_c_im, ssm_d, ssm_log_dt, ssm_w_glu, diff_lambda_q1, diff_lambda_k1, diff_lambda_q2, diff_lambda_k2, diff_subln_g, w_out, router_group_w, router_group_b, router_expert_w, router_expert_b, moe_w_gate, moe_w_up, moe_w_down, final_norm_g):
    batch, seq, d = x.shape
    depth = w_in.shape[0]
    lp = PAD + N_META + seq
    nb = lp // BLK
    meta = jnp.broadcast_to(meta_tokens[None].astype(x.dtype), (batch, N_META, d))
    h = jnp.concatenate([jnp.zeros((batch, PAD, d), x.dtype), meta, x], axis=1).reshape(batch * lp, d)
    cos, sin = _rope_tables(batch, lp)
    for l in range(depth):
        att, us, qd, kd, vd = _norm_inproj(h, attn_norm_g[l][None], w_in[l].astype(BF16), cos, sin)
        o_a = _swa(att, swa_sinks[l].astype(F32), batch, nb)
        kc, bc, cc, a_t = _ssm_operators(ssm_lambda_re[l], ssm_lambda_im[l], ssm_b_re[l], ssm_b_im[l],
                                         ssm_c_re[l], ssm_c_im[l], ssm_log_dt[l])
        yg = _ssm(us, kc, bc, cc, a_t, ssm_d[l].astype(F32).reshape(SSM_PACKS, 1, LANES), batch, lp)
        lam_init = 0.8 - 0.6 * math.exp(-0.3 * l)
        lam_vecs = jnp.stack([diff_lambda_q1[l], diff_lambda_k1[l], diff_lambda_q2[l], diff_lambda_k2[l]]).astype(F32)
        o_c = _diff_attn(qd, kd, vd, lam_vecs, diff_subln_g[l][None].astype(F32), lam_init, batch, nb)
        wo = w_out[l].astype(BF16)
        rw = jnp.concatenate([router_group_w[l], router_expert_w[l]], axis=1).astype(F32)
        rw = jnp.pad(rw, ((0, 0), (0, LANES - rw.shape[1])))
        rb = jnp.concatenate([router_group_b[l], router_expert_b[l]]).astype(F32)
        rb = jnp.pad(rb, (0, LANES - rb.shape[0]))[None]
        rw_hi, rw_lo = _split_bf16(rw)
        g_ffn = ffn_norm_g[l][None].astype(F32)
        h, hn, wts, ids, ranks, tile_cnt = _mixer_out(h, o_a, yg, o_c, wo, ssm_w_glu[l].astype(BF16), g_ffn, rw_hi,
                                                      rw_lo, rb)
        final_g = final_norm_g[None].astype(F32) if l == depth - 1 else None
        h = _moe(h, hn, wts, ids, ranks, tile_cnt, moe_w_gate, moe_w_up, moe_w_down, l, final_g)
    return h.reshape(batch, lp, d)[:, PAD + N_META:]
```

```python
import functools
import math

import jax
import jax.numpy as jnp
from jax import lax
from jax.experimental import pallas as pl
from jax.experimental.pallas import tpu as pltpu

D_MODEL = 2048
N_META = 16
BLK = 128
PAD = BLK - N_META
ROPE_THETA = 10000.0
NORM_EPS = 1e-5
NEG = -1e30
SWA_HEADS = 12
SWA_KV_HEADS = 4
HEAD_DIM = 64
SWA_WIDTH = SWA_HEADS * HEAD_DIM
SSM_WIDTH = 768
SSM_GROUP = 16
SSM_GROUPS = SSM_WIDTH // SSM_GROUP
SSM_STATE = 64
DIFF_HEADS = 4
DIFF_V_DIM = 128
DIFF_WIDTH = DIFF_HEADS * DIFF_V_DIM
IN_WIDTH = 3584
N_EXPERT_GROUPS = 4
EXPERTS_PER_GROUP = 8
N_EXPERTS = N_EXPERT_GROUPS * EXPERTS_PER_GROUP
D_EXPERT = 512

COL_KA = 768
COL_VA = 1024
COL_US = 1280
COL_QD = 2048
COL_KD = 2560
COL_VD = 3072

LANES = 128
SSM_T = 16
SSM_PACK = LANES // SSM_GROUP
SSM_PACKS = SSM_GROUPS // SSM_PACK
SSM_PSTATE = SSM_PACK * SSM_STATE
MOE_BM = 256
VMEM_LIMIT = 56 * 1024 * 1024

F32 = jnp.float32
BF16 = jnp.bfloat16


def _row_tile(rows):
    for t in (512, 256, 128):
        if rows % t == 0:
            return t
    raise ValueError(f"row count {rows} is not a multiple of 128")


def _params(sem, vmem=VMEM_LIMIT):
    return pltpu.CompilerParams(dimension_semantics=sem, vmem_limit_bytes=vmem)


def _rms_norm(x, g):
    return x * lax.rsqrt(jnp.mean(x * x, axis=-1, keepdims=True) + NORM_EPS) * g


IN_TN = 512
ROPE_TILES = (0, 1, 4, 5)


def _norm_inproj_kernel(x_ref, g_ref, w_ref, cos_ref, sin_ref, att_ref, us_ref, qd_ref, kd_ref, vd_ref, xn_ref):
    blocked = ((COL_VD, vd_ref), (COL_KD, kd_ref), (COL_QD, qd_ref), (COL_US, us_ref))

    def store(col, piece):
        for start, ref in blocked:
            if col >= start:
                ref[(col - start) // LANES] = piece
                return
        att_ref[:, col:col + LANES] = piece

    x = x_ref[...]
    ms = jnp.mean(x * x, axis=-1, keepdims=True)
    xn_ref[...] = (x * lax.rsqrt(ms + NORM_EPS) * g_ref[...]).astype(BF16)
    shape = (x.shape[0], IN_TN)
    lane = lax.broadcasted_iota(jnp.int32, shape, 1)
    first_half = (lane & (HEAD_DIM - 1)) < (HEAD_DIM // 2)
    reps = IN_TN // LANES
    cos = jnp.tile(cos_ref[...], (1, reps))
    sin = jnp.tile(sin_ref[...], (1, reps))
    for j in range(IN_WIDTH // IN_TN):
        cols = slice(j * IN_TN, (j + 1) * IN_TN)
        acc = jnp.dot(xn_ref[...], w_ref[:, cols], preferred_element_type=F32)
        if j in ROPE_TILES:
            partner = jnp.where(first_half, pltpu.roll(acc, IN_TN - HEAD_DIM // 2, 1),
                                pltpu.roll(acc, HEAD_DIM // 2, 1))
            acc = acc * cos + partner * sin
        acc = acc.astype(BF16)
        for c in range(IN_TN // LANES):
            store(j * IN_TN + c * LANES, acc[:, c * LANES:(c + 1) * LANES])


def _norm_inproj(h, g, w, cos, sin):
    rows = h.shape[0]
    tm = _row_tile(rows)

    def blocked(width):
        n = width // LANES
        return pl.BlockSpec((n, tm, LANES), lambda i: (0, i, 0)), jax.ShapeDtypeStruct((n, rows, LANES), BF16)

    specs, shapes = zip(
        (pl.BlockSpec((tm, COL_US), lambda i: (i, 0)), jax.ShapeDtypeStruct((rows, COL_US), BF16)),
        blocked(SSM_WIDTH), blocked(DIFF_WIDTH), blocked(DIFF_WIDTH), blocked(DIFF_WIDTH))
    return pl.pallas_call(
        _norm_inproj_kernel,
        grid=(rows // tm,),
        in_specs=[
            pl.BlockSpec((tm, D_MODEL), lambda i: (i, 0)),
            pl.BlockSpec((1, D_MODEL), lambda i: (0, 0)),
            pl.BlockSpec((D_MODEL, IN_WIDTH), lambda i: (0, 0)),
            pl.BlockSpec((tm, LANES), lambda i: (i, 0)),
            pl.BlockSpec((tm, LANES), lambda i: (i, 0)),
        ],
        out_specs=list(specs),
        out_shape=list(shapes),
        scratch_shapes=[pltpu.VMEM((tm, D_MODEL), BF16)],
        compiler_params=_params(("parallel",)),
        name="norm_inproj",
    )(h, g, w, cos, sin)


def _swa_kernel(sink_ref, q_ref, kp_ref, kc_ref, vp_ref, vc_ref, o_ref):
    n = pl.program_id(1)
    shape = (BLK, 2 * BLK)
    qi = lax.broadcasted_iota(jnp.int32, shape, 0)
    c = lax.broadcasted_iota(jnp.int32, shape, 1)
    kj = (n - 1) * BLK + c
    delta = qi + BLK - c
    ok = (delta >= 0) & (delta < BLK) & (kj >= PAD)
    grp = SWA_HEADS // SWA_KV_HEADS
    outs = []
    for j in range(SWA_KV_HEADS):
        cols = slice(j * HEAD_DIM, (j + 1) * HEAD_DIM)
        kk = jnp.concatenate([kp_ref[:, cols], kc_ref[:, cols]], axis=0)
        vv = jnp.concatenate([vp_ref[:, cols], vc_ref[:, cols]], axis=0)
        for g in range(grp):
            hd = j * grp + g
            qh = q_ref[:, hd * HEAD_DIM:(hd + 1) * HEAD_DIM]
            s = lax.dot_general(qh, kk, (((1,), (1,)), ((), ())), preferred_element_type=F32) * (HEAD_DIM ** -0.5)
            s = jnp.where(ok, s, NEG)
            sink = sink_ref[hd]
            m = jnp.maximum(jnp.max(s, axis=-1, keepdims=True), sink)
            p = jnp.exp(s - m)
            denom = jnp.sum(p, axis=-1, keepdims=True) + jnp.exp(sink - m)
            outs.append(jnp.dot(p.astype(BF16), vv, preferred_element_type=F32) / denom)
    o_ref[...] = jnp.concatenate(outs, axis=1).astype(o_ref.dtype)


def _swa(proj, sinks, batch, nb):
    rows = proj.shape[0]
    kvw = SWA_KV_HEADS * HEAD_DIM

    def cur(col):
        return lambda b, n: (b * nb + n, col)

    def prev(col):
        return lambda b, n: (b * nb + jnp.maximum(n - 1, 0), col)

    return pl.pallas_call(
        _swa_kernel,
        grid=(batch, nb),
        in_specs=[
            pl.BlockSpec(memory_space=pltpu.SMEM),
            pl.BlockSpec((BLK, SWA_WIDTH), cur(0)),
            pl.BlockSpec((BLK, kvw), prev(COL_KA // kvw)),
            pl.BlockSpec((BLK, kvw), cur(COL_KA // kvw)),
            pl.BlockSpec((BLK, kvw), prev(COL_VA // kvw)),
            pl.BlockSpec((BLK, kvw), cur(COL_VA // kvw)),
        ],
        out_specs=pl.BlockSpec((BLK, SWA_WIDTH), cur(0)),
        out_shape=jax.ShapeDtypeStruct((rows, SWA_WIDTH), BF16),
        compiler_params=_params(("parallel", "parallel")),
        name="swa",
    )(sinks, proj, proj, proj, proj, proj)


def _diff_kernel(lam_ref, g_ref, q_ref, k_ref, v_ref, o_ref, *, lam_init, nb):
    row = lax.broadcasted_iota(jnp.int32, (BLK, BLK), 0)
    col = lax.broadcasted_iota(jnp.int32, (BLK, BLK), 1)
    causal_ok = col <= row
    pad_ok = col >= PAD
    lane = lax.broadcasted_iota(jnp.int32, (BLK, LANES), 1)
    lv = lam_ref[...]
    lam = (jnp.exp(jnp.sum(lv[0:1] * lv[1:2], axis=-1, keepdims=True))
           - jnp.exp(jnp.sum(lv[2:3] * lv[3:4], axis=-1, keepdims=True)) + lam_init)
    gain = g_ref[...] * (1.0 - lam_init)
    scale = HEAD_DIM ** -0.5
    for n in range(nb):
        ke = (n + 1) * BLK
        q = q_ref[n * BLK:(n + 1) * BLK, :] * scale
        kx = k_ref[0:ke, :]
        vx = v_ref[0:ke, :]
        heads = []
        for s_idx in range(2):
            qx = jnp.where((lane >= HEAD_DIM) == bool(s_idx), q, jnp.zeros_like(q))
            s = lax.dot_general(qx, kx, (((1,), (1,)), ((), ())), preferred_element_type=F32)
            if n == 0:
                s = jnp.where(causal_ok & pad_ok, s, NEG)
            else:
                first = jnp.where(pad_ok, s[:, :BLK], NEG)
                last = jnp.where(causal_ok, s[:, ke - BLK:], NEG)
                middle = [s[:, BLK:ke - BLK]] if n > 1 else []
                s = jnp.concatenate([first] + middle + [last], axis=1)
            m = jnp.max(s, axis=-1, keepdims=True)
            p = jnp.exp(s - m)
            denom = jnp.sum(p, axis=-1, keepdims=True)
            heads.append(jnp.dot(p.astype(BF16), vx, preferred_element_type=F32) / denom)
        o = heads[0] - lam * heads[1]
        o = o * lax.rsqrt(jnp.mean(o * o, axis=-1, keepdims=True) + NORM_EPS) * gain
        o_ref[n * BLK:(n + 1) * BLK, :] = o.astype(o_ref.dtype)


def _diff_attn(qd, kd, vd, lam_vecs, subln_g, lam_init, batch, nb):
    rows = qd.shape[1]
    lp = nb * BLK
    head_rows = pl.BlockSpec((None, lp, LANES), lambda b, h: (h, b, 0))
    return pl.pallas_call(
        functools.partial(_diff_kernel, lam_init=lam_init, nb=nb),
        grid=(batch, DIFF_HEADS),
        in_specs=[
            pl.BlockSpec((4, HEAD_DIM), lambda b, h: (0, 0)),
            pl.BlockSpec((1, DIFF_V_DIM), lambda b, h: (0, 0)),
            head_rows, head_rows, head_rows,
        ],
        out_specs=head_rows,
        out_shape=jax.ShapeDtypeStruct((DIFF_HEADS, rows, DIFF_V_DIM), BF16),
        compiler_params=_params(("parallel", "parallel")),
        name="diff_attn",
    )(lam_vecs, subln_g, qd, kd, vd)


def _ssm_row_chunk(rows):
    for c in (544, 512, 384, 256, 128, 64, 32, 16):
        if rows % c == 0:
            return c
    raise ValueError(rows)


def _ssm_expand_operators(kc_ref, bc_ref, cc_ref, k_ref, bw_ref, cw_ref):
    def iota(shape, dim):
        return lax.broadcasted_iota(jnp.int32, shape, dim)

    def group(idx, width):
        return jnp.right_shift(idx, width.bit_length() - 1) & (SSM_PACK - 1)

    hs, ps, tl = SSM_GROUP, SSM_STATE, SSM_T * LANES
    spread_h = ((iota((LANES, hs), 0) & (hs - 1)) == iota((LANES, hs), 1)).astype(BF16)
    kt = jnp.dot(spread_h, kc_ref[...], preferred_element_type=F32)
    same = group(iota((LANES, tl), 0), hs) == group(iota((LANES, tl), 1), hs)
    k_ref[...] = jnp.where(same, kt, 0.0).astype(BF16)
    spread_s = (iota((ps, SSM_PSTATE), 0) == (iota((ps, SSM_PSTATE), 1) & (ps - 1))).astype(BF16)
    shape = (tl, SSM_PSTATE)
    same = group(iota(shape, 0), hs) == group(iota(shape, 1), ps)
    for part in range(2):
        bw = jnp.dot(bc_ref[part], spread_s, preferred_element_type=F32)
        bw_ref[:, part * SSM_PSTATE:(part + 1) * SSM_PSTATE] = jnp.where(same, bw, 0.0).astype(BF16)
    shape = (LANES, SSM_PSTATE)
    same = group(iota(shape, 0), hs) == group(iota(shape, 1), ps)
    for j in range(SSM_T):
        for part in range(2):
            ct = jnp.dot(spread_h, cc_ref[part, j], preferred_element_type=F32)
            cw_ref[j, :, part * SSM_PSTATE:(part + 1) * SSM_PSTATE] = jnp.where(same, ct, 0.0).astype(BF16)


def _ssm_kernel(u_ref, kc_ref, bc_ref, cc_ref, a_ref, d_ref, o_ref, up_ref, y_ref, e_ref, s_ref, k_ref, bw_ref,
                cw_ref, *, batch, lp):
    rows = batch * lp
    n_chunks = lp // SSM_T
    tot_chunks = rows // SSM_T
    _ssm_expand_operators(kc_ref, bc_ref, cc_ref, k_ref, bw_ref, cw_ref)
    contract_lanes = (((1,), (1,)), ((), ()))
    up_ref[0:SSM_T, :] = jnp.zeros((SSM_T, LANES), F32)
    up_ref[SSM_T:, :] = u_ref[...].astype(F32)
    for b in range(batch):
        up_ref[SSM_T + b * lp:SSM_T + b * lp + PAD, :] = jnp.zeros((PAD, LANES), F32)

    ch = _ssm_row_chunk(rows)
    rmod = lax.broadcasted_iota(jnp.int32, (ch, LANES), 0) & (SSM_T - 1)
    for r0 in range(0, rows, ch):
        pieces = []
        for k in range(SSM_T):
            sl = up_ref[SSM_T - k + r0:SSM_T - k + r0 + ch, :]
            pieces.append(jnp.where(rmod >= k, sl, 0.0).astype(BF16))
        y_ref[r0:r0 + ch, :] = lax.dot_general(jnp.concatenate(pieces, axis=1), k_ref[...], contract_lanes,
                                               preferred_element_type=F32)

    pieces = [up_ref[pl.ds(SSM_T + i, tot_chunks, stride=SSM_T), :].astype(BF16) for i in range(SSM_T)]
    e_ref[...] = jnp.dot(jnp.concatenate(pieces, axis=1), bw_ref[...], preferred_element_type=F32)

    a_re = a_ref[:, :SSM_PSTATE]
    a_im = a_ref[:, SSM_PSTATE:]

    def step(c, carry):
        new = []
        for b in range(batch):
            sr, si = carry[b]
            idx = b * n_chunks + c
            s_ref[pl.ds(idx, 1), :SSM_PSTATE] = sr
            s_ref[pl.ds(idx, 1), SSM_PSTATE:] = si
            er = e_ref[pl.ds(idx, 1), :SSM_PSTATE]
            ei = e_ref[pl.ds(idx, 1), SSM_PSTATE:]
            new.append((a_re * sr - a_im * si + er, a_re * si + a_im * sr + ei))
        return tuple(new)

    zero = jnp.zeros((1, SSM_PSTATE), F32)
    lax.fori_loop(0, n_chunks, step, tuple((zero, zero) for _ in range(batch)))

    sb = s_ref[...].astype(BF16)
    for j in range(SSM_T):
        z = lax.dot_general(sb, cw_ref[j], contract_lanes, preferred_element_type=F32)
        idx = pl.ds(j, tot_chunks, stride=SSM_T)
        y_ref[idx, :] = y_ref[idx, :] + z

    y = y_ref[...] + d_ref[...] * up_ref[SSM_T:, :]
    o_ref[...] = jax.nn.gelu(y, approximate=True).astype(o_ref.dtype)


def _ssm(us, kc, bc, cc, a_t, dskip, batch, lp):
    rows = us.shape[1]
    tot_chunks = rows // SSM_T
    return pl.pallas_call(
        functools.partial(_ssm_kernel, batch=batch, lp=lp),
        grid=(SSM_PACKS,),
        in_specs=[
            pl.BlockSpec((None, rows, LANES), lambda p: (p, 0, 0)),
            pl.BlockSpec((None, SSM_GROUP, SSM_T * LANES), lambda p: (p, 0, 0)),
            pl.BlockSpec((None, 2, SSM_T * LANES, SSM_STATE), lambda p: (p, 0, 0, 0)),
            pl.BlockSpec((None, 2, SSM_T, SSM_GROUP, SSM_PSTATE), lambda p: (p, 0, 0, 0, 0)),
            pl.BlockSpec((None, 1, 2 * SSM_PSTATE), lambda p: (p, 0, 0)),
            pl.BlockSpec((None, 1, LANES), lambda p: (p, 0, 0)),
        ],
        out_specs=pl.BlockSpec((None, rows, LANES), lambda p: (p, 0, 0)),
        out_shape=jax.ShapeDtypeStruct((SSM_PACKS, rows, LANES), F32),
        scratch_shapes=[
            pltpu.VMEM((rows + SSM_T, LANES), F32),
            pltpu.VMEM((rows, LANES), F32),
            pltpu.VMEM((tot_chunks, 2 * SSM_PSTATE), F32),
            pltpu.VMEM((tot_chunks, 2 * SSM_PSTATE), F32),
            pltpu.VMEM((LANES, SSM_T * LANES), BF16),
            pltpu.VMEM((SSM_T * LANES, 2 * SSM_PSTATE), BF16),
            pltpu.VMEM((SSM_T, LANES, 2 * SSM_PSTATE), BF16),
        ],
        compiler_params=_params(("parallel",)),
        name="ssm",
    )(us, kc, bc, cc, a_t, dskip)


def _ssm_operators(lam_re, lam_im, b_re, b_im, c_re, c_im, log_dt):
    p, h = SSM_STATE, SSM_GROUP
    dt = jnp.exp(log_dt.astype(F32))[:, None]
    lr, li = lam_re.astype(F32), lam_im.astype(F32)
    mag = jnp.exp(lr * dt)
    abar_re, abar_im = mag * jnp.cos(li * dt), mag * jnp.sin(li * dt)
    den = lr * lr + li * li
    nr = abar_re - 1.0
    f_re = (nr * lr + abar_im * li) / den
    f_im = (abar_im * lr - nr * li) / den
    br, bi = b_re.astype(F32), b_im.astype(F32)
    bbar_re = f_re[..., None] * br - f_im[..., None] * bi
    bbar_im = f_re[..., None] * bi + f_im[..., None] * br
    ks = jnp.arange(SSM_T + 1, dtype=F32)[:, None, None]
    pmag = jnp.exp(ks * (lr * dt)[None])
    pw_re = pmag * jnp.cos(ks * (li * dt)[None])
    pw_im = pmag * jnp.sin(ks * (li * dt)[None])
    cr, ci = c_re.astype(F32), c_im.astype(F32)

    packs, q = SSM_PACKS, SSM_PACK

    cb_re = cr[:, :, :, None] * bbar_re[:, None, :, :] - ci[:, :, :, None] * bbar_im[:, None, :, :]
    cb_im = cr[:, :, :, None] * bbar_im[:, None, :, :] + ci[:, :, :, None] * bbar_re[:, None, :, :]
    klag = (jnp.einsum('kgs,gosi->kgoi', pw_re[:SSM_T], cb_re)
            - jnp.einsum('kgs,gosi->kgoi', pw_im[:SSM_T], cb_im))
    kc = klag.reshape(SSM_T, packs, q, h, h).transpose(1, 3, 0, 2, 4).reshape(packs, h, SSM_T * LANES)

    rev_re, rev_im = pw_re[SSM_T - 1::-1, :, None, :], pw_im[SSM_T - 1::-1, :, None, :]
    bt_re, bt_im = jnp.swapaxes(bbar_re, 1, 2)[None], jnp.swapaxes(bbar_im, 1, 2)[None]

    def inj_by_pack(x):
        return x.reshape(SSM_T, packs, q * h, p).transpose(1, 0, 2, 3).reshape(packs, SSM_T * LANES, p)

    bc = jnp.stack([inj_by_pack(rev_re * bt_re - rev_im * bt_im), inj_by_pack(rev_re * bt_im + rev_im * bt_re)], axis=1)

    ca_re = cr[None] * pw_re[1:, :, None, :] - ci[None] * pw_im[1:, :, None, :]
    ca_im = cr[None] * pw_im[1:, :, None, :] + ci[None] * pw_re[1:, :, None, :]

    def out_by_pack(x):
        return x.reshape(SSM_T, packs, q, h, p).transpose(1, 0, 3, 2, 4).reshape(packs, SSM_T, h, SSM_PSTATE)

    cc = jnp.stack([out_by_pack(ca_re), out_by_pack(-ca_im)], axis=1)

    a_t = jnp.concatenate([pw_re[SSM_T].reshape(packs, 1, SSM_PSTATE),
                           pw_im[SSM_T].reshape(packs, 1, SSM_PSTATE)], axis=-1)
    return kc.astype(BF16), bc.astype(BF16), cc.astype(BF16), a_t


OUT_TN = 512


def _split_bf16(x):
    hi = x.astype(BF16)
    return hi, (x - hi.astype(F32)).astype(BF16)


def _route(hn, rwh_ref, rwl_ref, rb_ref):
    hi, lo = _split_bf16(hn)
    logits = (jnp.dot(hi, rwh_ref[...], preferred_element_type=F32)
              + (jnp.dot(lo, rwh_ref[...], preferred_element_type=F32)
                 + jnp.dot(hi, rwl_ref[...], preferred_element_type=F32))) + rb_ref[...]
    lane = lax.broadcasted_iota(jnp.int32, logits.shape, 1)

    def first_argmax(v, vmax):
        return jnp.min(jnp.where(v == vmax, lane, LANES), axis=-1, keepdims=True)

    gl = jnp.where(lane < N_EXPERT_GROUPS, logits, NEG)
    gmax = jnp.max(gl, axis=-1, keepdims=True)
    g_sel = first_argmax(gl, gmax)
    p_g = 1.0 / jnp.sum(jnp.exp(gl - gmax), axis=-1, keepdims=True)
    e_lane = lane - N_EXPERT_GROUPS
    in_group = (e_lane >= 0) & (e_lane < N_EXPERTS) & (jnp.right_shift(e_lane, 3) == g_sel)
    el = jnp.where(in_group, logits, NEG)
    m1 = jnp.max(el, axis=-1, keepdims=True)
    i1 = first_argmax(el, m1)
    el2 = jnp.where(lane == i1, NEG, el)
    m2 = jnp.max(el2, axis=-1, keepdims=True)
    i2 = first_argmax(el2, m2)
    z = jnp.sum(jnp.exp(el - m1), axis=-1, keepdims=True)
    p1 = 1.0 / z
    p2 = jnp.exp(m2 - m1) / z
    w1 = p1 / (p1 + p2) * p_g
    w2 = p2 / (p1 + p2) * p_g
    wts = jnp.where(lane == 0, w1, jnp.where(lane == 1, w2, 0.0))
    ids = jnp.where(lane == 0, i1 - N_EXPERT_GROUPS, jnp.where(lane == 1, i2 - N_EXPERT_GROUPS, 0))
    return wts, ids


def _assignment_ranks(ids):
    tm = ids.shape[0]
    lane = lax.broadcasted_iota(jnp.int32, ids.shape, 1)
    oh0 = lane == ids[:, 0:1]
    oh1 = lane == ids[:, 1:2]
    hits = (oh0 | oh1).astype(BF16)
    r = lax.broadcasted_iota(jnp.int32, (tm, tm), 0)
    c = lax.broadcasted_iota(jnp.int32, (tm, tm), 1)
    before = jnp.dot((c < r).astype(BF16), hits, preferred_element_type=F32)
    rank0 = jnp.sum(jnp.where(oh0, before, 0.0), axis=-1, keepdims=True)
    rank1 = jnp.sum(jnp.where(oh1, before, 0.0), axis=-1, keepdims=True)
    ranks = jnp.where(lane == 0, rank0, jnp.where(lane == 1, rank1, 0.0)).astype(jnp.int32)
    return ranks, jnp.sum(hits.astype(F32), axis=0, keepdims=True).astype(jnp.int32)


def _mixer_out_kernel(h_ref, a_ref, y_ref, c_ref, wo_ref, wglu_ref, g_ref, rwh_ref, rwl_ref, rb_ref,
                      o_ref, hn_ref, wt_ref, id_ref, rank_ref, cnt_ref):
    half = h_ref.shape[0] // 2
    halves = [slice(r * half, (r + 1) * half) for r in range(2)]
    for rows in halves:
        y = jnp.concatenate([y_ref[p, rows, :] for p in range(y_ref.shape[0])], axis=1)
        c = jnp.concatenate([c_ref[hd, rows, :] for hd in range(c_ref.shape[0])], axis=1)
        gate = jnp.dot(y.astype(BF16), wglu_ref[...], preferred_element_type=F32)
        b = (y * jax.nn.sigmoid(gate)).astype(BF16)
        mixed = jnp.concatenate([a_ref[rows, :], b, c], axis=1)
        for j in range(D_MODEL // OUT_TN):
            cols = slice(j * OUT_TN, (j + 1) * OUT_TN)
            o_ref[rows, cols] = h_ref[rows, cols] + jnp.dot(mixed, wo_ref[:, cols], preferred_element_type=F32)
    for rows in halves:
        hn = _rms_norm(o_ref[rows, :], g_ref[...])
        hn_ref[rows, :] = hn.astype(BF16)
        wts, ids = _route(hn, rwh_ref, rwl_ref, rb_ref)
        wt_ref[rows, :] = wts
        id_ref[rows, :] = ids
    ranks, counts = _assignment_ranks(id_ref[...])
    rank_ref[...] = ranks
    cnt_ref[...] = jnp.broadcast_to(counts, cnt_ref.shape)


def _mixer_out(h, oa, yg, oc, wo, wglu, g, rw_hi, rw_lo, rb):
    rows = h.shape[0]
    tm = _row_tile(rows)

    def row_spec(width):
        return pl.BlockSpec((tm, width), lambda i: (i, 0))

    def whole(shape):
        return pl.BlockSpec(shape, lambda i: (0, 0))

    def blocked(width):
        return pl.BlockSpec((width // LANES, tm, LANES), lambda i: (0, i, 0))

    return pl.pallas_call(
        _mixer_out_kernel,
        grid=(rows // tm,),
        in_specs=[
            row_spec(D_MODEL), row_spec(SWA_WIDTH), blocked(SSM_WIDTH), blocked(DIFF_WIDTH),
            whole((D_MODEL, D_MODEL)), whole((SSM_WIDTH, SSM_WIDTH)), whole((1, D_MODEL)),
            whole((D_MODEL, LANES)), whole((D_MODEL, LANES)), whole((1, LANES)),
        ],
        out_specs=[row_spec(D_MODEL), row_spec(D_MODEL), row_spec(LANES), row_spec(LANES), row_spec(LANES),
                   pl.BlockSpec((None, 8, LANES), lambda i: (i, 0, 0))],
        out_shape=[
            jax.ShapeDtypeStruct((rows, D_MODEL), F32),
            jax.ShapeDtypeStruct((rows, D_MODEL), BF16),
            jax.ShapeDtypeStruct((rows, LANES), F32),
            jax.ShapeDtypeStruct((rows, LANES), jnp.int32),
            jax.ShapeDtypeStruct((rows, LANES), jnp.int32),
            jax.ShapeDtypeStruct((rows // tm, 8, LANES), jnp.int32),
        ],
        compiler_params=_params(("parallel",)),
        name="mixer_out",
    )(h, oa, yg, oc, wo, wglu, g, rw_hi, rw_lo, rb)


MOE_UNIT = 16
MOE_DTYPE = BF16


def _moe_tile_rows(tt):
    return 2 * tt + N_EXPERTS * MOE_UNIT


def _moe_tables(tile_cnt, n_blk):
    cnt = tile_cnt[:, 0, :N_EXPERTS]
    seg = (cnt + MOE_UNIT - 1) // MOE_UNIT * MOE_UNIT
    size = jnp.sum(seg, axis=0)
    padded = (size + MOE_BM - 1) // MOE_BM * MOE_BM
    pend = jnp.cumsum(padded)
    pstart = pend - padded
    base = pstart[None, :] + jnp.cumsum(seg, axis=0) - seg
    loc = jnp.cumsum(seg, axis=1) - seg
    units = seg // MOE_UNIT
    blk_start = jnp.arange(n_blk, dtype=jnp.int32) * MOE_BM
    blk_expert = jnp.minimum(jnp.sum((pend[None, :] <= blk_start[:, None]).astype(jnp.int32), axis=1), N_EXPERTS - 1)
    ex = jnp.arange(N_EXPERTS, dtype=jnp.int32)
    nonempty = size > 0
    run_slot = (jnp.cumsum(nonempty.astype(jnp.int32)) - 1) % WEIGHT_SLOTS
    later = (ex[None, :] > ex[:, None]) & nonempty[None, :]
    nxt = jnp.min(jnp.where(later, ex[None, :], N_EXPERTS), axis=1)
    nxt = jnp.where(nxt == N_EXPERTS, -1, nxt)
    nxt2 = jnp.where(nxt >= 0, nxt[jnp.maximum(nxt, 0)], -1)
    i32 = jnp.int32
    loc_lanes = jnp.pad(loc, ((0, 0), (0, LANES - N_EXPERTS)))
    return dict(
        base=base.reshape(-1).astype(i32), loc=loc.reshape(-1).astype(i32), units=units.reshape(-1).astype(i32),
        tile_units=jnp.sum(units, axis=1).astype(i32),
        tail_start=(pstart + size).astype(i32), tail_units=((padded - size) // MOE_UNIT).astype(i32),
        loc_lanes=jnp.broadcast_to(loc_lanes[:, None, :], (loc.shape[0], 8, LANES)).astype(i32),
        blk_expert=blk_expert.astype(i32), blk_slot=run_slot[blk_expert].astype(i32),
        blk_next1=nxt[blk_expert].astype(i32), blk_next2=nxt2[blk_expert].astype(i32),
        n_used=(pend[-1] // MOE_BM).astype(i32).reshape(1))


def _segment_copies(tile, hbm, buf, sem, to_hbm, base_ref, loc_ref, units_ref):
    for e in range(N_EXPERTS):
        k = tile * N_EXPERTS + e
        row_hbm, row_buf = base_ref[k], loc_ref[k]

        def start(u, carry):
            a = hbm.at[pl.ds(pl.multiple_of(row_hbm + u * MOE_UNIT, MOE_UNIT), MOE_UNIT), :]
            b = buf.at[pl.ds(pl.multiple_of(row_buf + u * MOE_UNIT, MOE_UNIT), MOE_UNIT), :]
            (pltpu.make_async_copy(b, a, sem) if to_hbm else pltpu.make_async_copy(a, b, sem)).start()
            return carry

        lax.fori_loop(0, units_ref[k], start, 0)


def _wait_unit_copies(count, hbm, buf, sem):
    def wait(u, carry):
        pltpu.make_async_copy(hbm.at[pl.ds(0, MOE_UNIT), :], buf.at[pl.ds(0, MOE_UNIT), :], sem).wait()
        return carry

    lax.fori_loop(0, count, wait, 0)


def _local_rows(ids, ranks, loc_lanes):
    lane = lax.broadcasted_iota(jnp.int32, ids.shape, 1)
    loc = loc_lanes.astype(F32)
    start0 = jnp.sum(jnp.where(lane == ids[:, 0:1], loc, 0.0), axis=-1, keepdims=True).astype(jnp.int32)
    start1 = jnp.sum(jnp.where(lane == ids[:, 1:2], loc, 0.0), axis=-1, keepdims=True).astype(jnp.int32)
    return start0 + ranks[:, 0:1], start1 + ranks[:, 1:2]


def _dispatch_kernel(base_ref, loc_ref, units_ref, tile_units_ref, tail_start_ref, tail_units_ref, nused_ref,
                     hn_ref, ids_ref, rank_ref, locv_ref, x_hbm, lrow_ref, ybuf, zbuf, sem):
    j = pl.program_id(0)
    n = pl.num_programs(0)
    slot = j % 2
    tt, rt = hn_ref.shape[0], ybuf.shape[1]

    @pl.when(j >= 2)
    def _():
        _wait_unit_copies(tile_units_ref[j - 2], x_hbm, ybuf.at[slot], sem.at[slot])

    row0, row1 = _local_rows(ids_ref[...], rank_ref[...], locv_ref[0:1, :])
    lane = lax.broadcasted_iota(jnp.int32, lrow_ref.shape, 1)
    lrow_ref[...] = jnp.where(lane == 0, row0, jnp.where(lane == 1, row1, 0))
    col = lax.broadcasted_iota(jnp.int32, (tt, rt), 1)
    onehot = ((col == row0) | (col == row1)).astype(BF16)
    compacted = lax.dot_general(onehot, hn_ref[...], (((0,), (0,)), ((), ())), preferred_element_type=F32)
    ybuf[slot] = compacted.astype(ybuf.dtype)
    _segment_copies(j, x_hbm, ybuf.at[slot], sem.at[slot], True, base_ref, loc_ref, units_ref)

    @pl.when(j == n - 1)
    def _():
        zbuf[...] = jnp.zeros(zbuf.shape, zbuf.dtype)
        zunit = zbuf.at[pl.ds(0, MOE_UNIT), :]
        total = 0
        for e in range(N_EXPERTS):
            row = tail_start_ref[e]

            def start(u, carry):
                dst = x_hbm.at[pl.ds(pl.multiple_of(row + u * MOE_UNIT, MOE_UNIT), MOE_UNIT), :]
                pltpu.make_async_copy(zunit, dst, sem.at[2]).start()
                return carry

            lax.fori_loop(0, tail_units_ref[e], start, 0)
            total = total + tail_units_ref[e]
        _wait_unit_copies(total, x_hbm, zunit, sem.at[2])
        n_blk = x_hbm.shape[0] // MOE_BM

        def block_copy(blk):
            dst = x_hbm.at[pl.ds(pl.multiple_of(blk * MOE_BM, MOE_BM), MOE_BM), :]
            return pltpu.make_async_copy(zbuf, dst, sem.at[3])

        def start_block(blk, carry):
            block_copy(blk).start()
            return carry

        def wait_block(blk, carry):
            block_copy(blk).wait()
            return carry

        lax.fori_loop(nused_ref[0], n_blk, start_block, 0)
        lax.fori_loop(nused_ref[0], n_blk, wait_block, 0)

        @pl.when(j >= 1)
        def _():
            _wait_unit_copies(tile_units_ref[jnp.maximum(j - 1, 0)], x_hbm, ybuf.at[1 - slot], sem.at[1 - slot])

        _wait_unit_copies(tile_units_ref[j], x_hbm, ybuf.at[slot], sem.at[slot])


def _dispatch(hn, ids, ranks, tables, tt, n_rows):
    rows = hn.shape[0]
    rt = _moe_tile_rows(tt)
    row_spec = pl.BlockSpec((tt, LANES), lambda j, *_: (j, 0))
    grid_spec = pltpu.PrefetchScalarGridSpec(
        num_scalar_prefetch=7,
        grid=(rows // tt,),
        in_specs=[pl.BlockSpec((tt, D_MODEL), lambda j, *_: (j, 0)), row_spec, row_spec,
                  pl.BlockSpec((None, 8, LANES), lambda j, *_: (j, 0, 0))],
        out_specs=[pl.BlockSpec(memory_space=pl.ANY), row_spec],
        scratch_shapes=[pltpu.VMEM((2, rt, D_MODEL), MOE_DTYPE), pltpu.VMEM((MOE_BM, D_MODEL), MOE_DTYPE),
                        pltpu.SemaphoreType.DMA((4,))],
    )
    return pl.pallas_call(
        _dispatch_kernel,
        grid_spec=grid_spec,
        out_shape=[jax.ShapeDtypeStruct((n_rows, D_MODEL), MOE_DTYPE),
                   jax.ShapeDtypeStruct((rows, LANES), jnp.int32)],
        compiler_params=_params(("arbitrary",)),
        name="moe_dispatch",
    )(tables["base"], tables["loc"], tables["units"], tables["tile_units"], tables["tail_start"],
      tables["tail_units"], tables["n_used"], hn, ids, ranks, tables["loc_lanes"])


WEIGHT_DMA_PRIORITY = 1


WEIGHT_SLOTS = 3


def _expert_kernel(be_ref, slot_ref, nxt1_ref, nxt2_ref, nused_ref, x_ref, wg_hbm, wu_hbm, wd_hbm, o_ref,
                   wgf, wuf, wdf, wgb, wub, wdb, wsem, *, layer):
    i = pl.program_id(0)
    n_used = nused_ref[0]

    def weight_copies(expert, wslot):
        pairs = ((wg_hbm, wgf), (wu_hbm, wuf), (wd_hbm, wdf))
        return [pltpu.make_async_copy(w.at[layer, expert], buf.at[wslot], wsem.at[wslot]) for w, buf in pairs]

    @pl.when(i == 0)
    def _():
        for c in weight_copies(be_ref[0], 0):
            c.start(priority=WEIGHT_DMA_PRIORITY)

        @pl.when(nxt1_ref[0] >= 0)
        def _():
            for c in weight_copies(nxt1_ref[0], 1):
                c.start(priority=WEIGHT_DMA_PRIORITY)

    @pl.when(i < n_used)
    def _():
        @pl.when((i == 0) | (be_ref[i] != be_ref[jnp.maximum(i - 1, 0)]))
        def _():
            wslot = slot_ref[i]
            for c in weight_copies(be_ref[i], wslot):
                c.wait()

            @pl.when(nxt2_ref[i] >= 0)
            def _():
                for c in weight_copies(nxt2_ref[i], lax.rem(wslot + 2, WEIGHT_SLOTS)):
                    c.start(priority=WEIGHT_DMA_PRIORITY)

            wgb[...] = wgf[wslot].astype(BF16)
            wub[...] = wuf[wslot].astype(BF16)
            wdb[...] = wdf[wslot].astype(BF16)

        x = x_ref[...]
        gate = jnp.dot(x, wgb[...], preferred_element_type=F32)
        up = jnp.dot(x, wub[...], preferred_element_type=F32)
        act = (jax.nn.silu(gate) * up).astype(BF16)
        o_ref[...] = jnp.dot(act, wdb[...], preferred_element_type=F32).astype(o_ref.dtype)

    @pl.when(i >= n_used)
    def _():
        o_ref[...] = jnp.zeros(o_ref.shape, o_ref.dtype)


def _experts(x_sorted, tables, wg, wu, wd, layer, n_blk):
    hbm = pl.BlockSpec(memory_space=pl.ANY)
    grid_spec = pltpu.PrefetchScalarGridSpec(
        num_scalar_prefetch=5,
        grid=(n_blk,),
        in_specs=[pl.BlockSpec((MOE_BM, D_MODEL), lambda i, be, slot, n1, n2, nu: (jnp.minimum(i, nu[0] - 1), 0)),
                  hbm, hbm, hbm],
        out_specs=pl.BlockSpec((MOE_BM, D_MODEL), lambda i, *_: (i, 0)),
        scratch_shapes=[
            pltpu.VMEM((WEIGHT_SLOTS, D_MODEL, D_EXPERT), F32),
            pltpu.VMEM((WEIGHT_SLOTS, D_MODEL, D_EXPERT), F32),
            pltpu.VMEM((WEIGHT_SLOTS, D_EXPERT, D_MODEL), F32),
            pltpu.VMEM((D_MODEL, D_EXPERT), BF16),
            pltpu.VMEM((D_MODEL, D_EXPERT), BF16),
            pltpu.VMEM((D_EXPERT, D_MODEL), BF16),
            pltpu.SemaphoreType.DMA((WEIGHT_SLOTS,)),
        ],
    )
    return pl.pallas_call(
        functools.partial(_expert_kernel, layer=layer),
        grid_spec=grid_spec,
        out_shape=jax.ShapeDtypeStruct((n_blk * MOE_BM, D_MODEL), MOE_DTYPE),
        compiler_params=_params(("arbitrary",)),
        name="moe_experts",
    )(tables["blk_expert"], tables["blk_slot"], tables["blk_next1"], tables["blk_next2"], tables["n_used"],
      x_sorted, wg, wu, wd)


def _combine_kernel(base_ref, loc_ref, units_ref, tile_units_ref, h_ref, wt_ref, lrow_ref, g_ref, y_hbm, o_ref,
                    ybuf, sem, *, final_norm):
    j = pl.program_id(0)
    n = pl.num_programs(0)
    slot = j % 2
    tt, rt = h_ref.shape[0], ybuf.shape[1]

    @pl.when(j == 0)
    def _():
        ybuf[...] = jnp.zeros(ybuf.shape, ybuf.dtype)
        _segment_copies(0, y_hbm, ybuf.at[0], sem.at[0], False, base_ref, loc_ref, units_ref)

    @pl.when(j + 1 < n)
    def _():
        _segment_copies(j + 1, y_hbm, ybuf.at[1 - slot], sem.at[1 - slot], False, base_ref, loc_ref, units_ref)

    _wait_unit_copies(tile_units_ref[j], y_hbm, ybuf.at[slot], sem.at[slot])
    lrow = lrow_ref[...]
    wt = wt_ref[...]
    col = lax.broadcasted_iota(jnp.int32, (tt, rt), 1)
    pick0 = (col == lrow[:, 0:1]).astype(BF16)
    pick1 = (col == lrow[:, 1:2]).astype(BF16)
    for c in range(D_MODEL // OUT_TN):
        cols = slice(c * OUT_TN, (c + 1) * OUT_TN)
        y = ybuf[slot, :, cols]
        y0 = jnp.dot(pick0, y, preferred_element_type=F32)
        y1 = jnp.dot(pick1, y, preferred_element_type=F32)
        o_ref[:, cols] = h_ref[:, cols] + wt[:, 0:1] * y0 + wt[:, 1:2] * y1
    if final_norm:
        o_ref[...] = _rms_norm(o_ref[...], g_ref[...])


def _combine(h, wts, lrows, y_sorted, tables, g, tt, final_norm):
    rows = h.shape[0]
    rt = _moe_tile_rows(tt)
    grid_spec = pltpu.PrefetchScalarGridSpec(
        num_scalar_prefetch=4,
        grid=(rows // tt,),
        in_specs=[
            pl.BlockSpec((tt, D_MODEL), lambda j, *_: (j, 0)),
            pl.BlockSpec((tt, LANES), lambda j, *_: (j, 0)),
            pl.BlockSpec((tt, LANES), lambda j, *_: (j, 0)),
            pl.BlockSpec((1, D_MODEL), lambda j, *_: (0, 0)),
            pl.BlockSpec(memory_space=pl.ANY),
        ],
        out_specs=pl.BlockSpec((tt, D_MODEL), lambda j, *_: (j, 0)),
        scratch_shapes=[pltpu.VMEM((2, rt, D_MODEL), MOE_DTYPE), pltpu.SemaphoreType.DMA((2,))],
    )
    return pl.pallas_call(
        functools.partial(_combine_kernel, final_norm=final_norm),
        grid_spec=grid_spec,
        out_shape=jax.ShapeDtypeStruct((rows, D_MODEL), F32),
        compiler_params=_params(("arbitrary",)),
        name="moe_combine",
    )(tables["base"], tables["loc"], tables["units"], tables["tile_units"], h, wts, lrows, g, y_sorted)


def _moe(h, hn, wts, ids, ranks, tile_cnt, wg, wu, wd, layer, final_g):
    rows = h.shape[0]
    n_tiles = tile_cnt.shape[0]
    tt = rows // n_tiles
    n_rows = 2 * rows + n_tiles * N_EXPERTS * (MOE_UNIT - 1) + N_EXPERTS * (MOE_BM - 1)
    n_blk = -(-n_rows // MOE_BM)
    tables = _moe_tables(tile_cnt, n_blk)
    x_sorted, lrows = _dispatch(hn, ids, ranks, tables, tt, n_blk * MOE_BM)
    y_sorted = _experts(x_sorted, tables, wg, wu, wd, layer, n_blk)
    final_norm = final_g is not None
    g = final_g if final_norm else jnp.ones((1, D_MODEL), F32)
    return _combine(h, wts, lrows, y_sorted, tables, g, tt, final_norm)


def _rope_tables(batch, lp):
    half = HEAD_DIM // 2
    inv = ROPE_THETA ** (-jnp.arange(half, dtype=F32) / half)
    pos = (jnp.arange(lp, dtype=jnp.int32) - PAD).astype(F32)
    ang = pos[:, None] * inv[None, :]
    cos = jnp.tile(jnp.cos(ang), (batch, LANES // half))
    sin = jnp.sin(ang)
    sin = jnp.tile(jnp.concatenate([-sin, sin], axis=1), (batch, LANES // HEAD_DIM))
    return cos, sin


def kernel(x, meta_tokens, attn_norm_g, ffn_norm_g, w_in, swa_sinks, ssm_lambda_re, ssm_lambda_im, ssm_b_re, ssm_b_im, ssm_c_re, ssm_c_im, ssm_d, ssm_log_dt, ssm_w_glu, diff_lambda_q1, diff_lambda_k1, diff_lambda_q2, diff_lambda_k2, diff_subln_g, w_out, router_group_w, router_group_b, router_expert_w, router_expert_b, moe_w_gate, moe_w_up, moe_w_down, final_norm_g):
    batch, seq, d = x.shape
    depth = w_in.shape[0]
    lp = PAD + N_META + seq
    nb = lp // BLK
    meta = jnp.broadcast_to(meta_tokens[None].astype(x.dtype), (batch, N_META, d))
    h = jnp.concatenate([jnp.zeros((batch, PAD, d), x.dtype), meta, x], axis=1).reshape(batch * lp, d)
    cos, sin = _rope_tables(batch, lp)
    for l in range(depth):
        att, us, qd, kd, vd = _norm_inproj(h, attn_norm_g[l][None], w_in[l].astype(BF16), cos, sin)
        o_a = _swa(att, swa_sinks[l].astype(F32), batch, nb)
        kc, bc, cc, a_t = _ssm_operators(ssm_lambda_re[l], ssm_lambda_im[l], ssm_b_re[l], ssm_b_im[l],
                                         ssm_c_re[l], ssm_c_im[l], ssm_log_dt[l])
        yg = _ssm(us, kc, bc, cc, a_t, ssm_d[l].astype(F32).reshape(SSM_PACKS, 1, LANES), batch, lp)
        lam_init = 0.8 - 0.6 * math.exp(-0.3 * l)
        lam_vecs = jnp.stack([diff_lambda_q1[l], diff_lambda_k1[l], diff_lambda_q2[l], diff_lambda_k2[l]]).astype(F32)
        o_c = _diff_attn(qd, kd, vd, lam_vecs, diff_subln_g[l][None].astype(F32), lam_init, batch, nb)
        wo = w_out[l].astype(BF16)
        rw = jnp.concatenate([router_group_w[l], router_expert_w[l]], axis=1).astype(F32)
        rw = jnp.pad(rw, ((0, 0), (0, LANES - rw.shape[1])))
        rb = jnp.concatenate([router_group_b[l], router_expert_b[l]]).astype(F32)
        rb = jnp.pad(rb, (0, LANES - rb.shape[0]))[None]
        rw_hi, rw_lo = _split_bf16(rw)
        g_ffn = ffn_norm_g[l][None].astype(F32)
        h, hn, wts, ids, ranks, tile_cnt = _mixer_out(h, o_a, yg, o_c, wo, ssm_w_glu[l].astype(BF16), g_ffn, rw_hi,
                                                      rw_lo, rb)
        final_g = final_norm_g[None].astype(F32) if l == depth - 1 else None
        h = _moe(h, hn, wts, ids, ranks, tile_cnt, moe_w_gate, moe_w_up, moe_w_down, l, final_g)
    return h.reshape(batch, lp, d)[:, PAD + N_META:]
```

```python
import functools
import math

import jax
import jax.numpy as jnp
from jax import lax
from jax.experimental import pallas as pl
from jax.experimental.pallas import tpu as pltpu

D_MODEL = 2048
N_META = 16
BLK = 128
PAD = BLK - N_META
ROPE_THETA = 10000.0
NORM_EPS = 1e-5
NEG = -1e30
SWA_HEADS = 12
SWA_KV_HEADS = 4
HEAD_DIM = 64
SWA_WIDTH = SWA_HEADS * HEAD_DIM
SSM_WIDTH = 768
SSM_GROUP = 16
SSM_GROUPS = SSM_WIDTH // SSM_GROUP
SSM_STATE = 64
DIFF_HEADS = 4
DIFF_V_DIM = 128
DIFF_WIDTH = DIFF_HEADS * DIFF_V_DIM
IN_WIDTH = 3584
N_EXPERT_GROUPS = 4
EXPERTS_PER_GROUP = 8
N_EXPERTS = N_EXPERT_GROUPS * EXPERTS_PER_GROUP
D_EXPERT = 512

COL_KA = 768
COL_VA = 1024
COL_US = 1280
COL_QD = 2048
COL_KD = 2560
COL_VD = 3072

LANES = 128
SSM_T = 16
SSM_PACK = LANES // SSM_GROUP
SSM_PACKS = SSM_GROUPS // SSM_PACK
SSM_PSTATE = SSM_PACK * SSM_STATE
MOE_BM = 256
VMEM_LIMIT = 56 * 1024 * 1024

F32 = jnp.float32
BF16 = jnp.bfloat16


def _row_tile(rows):
    for t in (512, 256, 128):
        if rows % t == 0:
            return t
    raise ValueError(f"row count {rows} is not a multiple of 128")


def _params(sem, vmem=VMEM_LIMIT):
    return pltpu.CompilerParams(dimension_semantics=sem, vmem_limit_bytes=vmem)


def _rms_norm(x, g):
    return x * lax.rsqrt(jnp.mean(x * x, axis=-1, keepdims=True) + NORM_EPS) * g


IN_TN = 512
ROPE_TILES = (0, 1, 4, 5)


def _norm_inproj_kernel(x_ref, g_ref, w_ref, cos_ref, sin_ref, att_ref, us_ref, qd_ref, kd_ref, vd_ref, xn_ref):
    blocked = ((COL_VD, vd_ref), (COL_KD, kd_ref), (COL_QD, qd_ref), (COL_US, us_ref))

    def store(col, piece):
        for start, ref in blocked:
            if col >= start:
                ref[(col - start) // LANES] = piece
                return
        att_ref[:, col:col + LANES] = piece

    x = x_ref[...]
    ms = jnp.mean(x * x, axis=-1, keepdims=True)
    xn_ref[...] = (x * lax.rsqrt(ms + NORM_EPS) * g_ref[...]).astype(BF16)
    shape = (x.shape[0], IN_TN)
    lane = lax.broadcasted_iota(jnp.int32, shape, 1)
    first_half = (lane & (HEAD_DIM - 1)) < (HEAD_DIM // 2)
    reps = IN_TN // LANES
    cos = jnp.tile(cos_ref[...], (1, reps))
    sin = jnp.tile(sin_ref[...], (1, reps))
    for j in range(IN_WIDTH // IN_TN):
        cols = slice(j * IN_TN, (j + 1) * IN_TN)
        acc = jnp.dot(xn_ref[...], w_ref[:, cols], preferred_element_type=F32)
        if j in ROPE_TILES:
            partner = jnp.where(first_half, pltpu.roll(acc, IN_TN - HEAD_DIM // 2, 1),
                                pltpu.roll(acc, HEAD_DIM // 2, 1))
            acc = acc * cos + partner * sin
        acc = acc.astype(BF16)
        for c in range(IN_TN // LANES):
            store(j * IN_TN + c * LANES, acc[:, c * LANES:(c + 1) * LANES])


def _norm_inproj(h, g, w, cos, sin):
    rows = h.shape[0]
    tm = _row_tile(rows)

    def blocked(width):
        n = width // LANES
        return pl.BlockSpec((n, tm, LANES), lambda i: (0, i, 0)), jax.ShapeDtypeStruct((n, rows, LANES), BF16)

    specs, shapes = zip(
        (pl.BlockSpec((tm, COL_US), lambda i: (i, 0)), jax.ShapeDtypeStruct((rows, COL_US), BF16)),
        blocked(SSM_WIDTH), blocked(DIFF_WIDTH), blocked(DIFF_WIDTH), blocked(DIFF_WIDTH))
    return pl.pallas_call(
        _norm_inproj_kernel,
        grid=(rows // tm,),
        in_specs=[
            pl.BlockSpec((tm, D_MODEL), lambda i: (i, 0)),
            pl.BlockSpec((1, D_MODEL), lambda i: (0, 0)),
            pl.BlockSpec((D_MODEL, IN_WIDTH), lambda i: (0, 0)),
            pl.BlockSpec((tm, LANES), lambda i: (i, 0)),
            pl.BlockSpec((tm, LANES), lambda i: (i, 0)),
        ],
        out_specs=list(specs),
        out_shape=list(shapes),
        scratch_shapes=[pltpu.VMEM((tm, D_MODEL), BF16)],
        compiler_params=_params(("parallel",)),
        name="norm_inproj",
    )(h, g, w, cos, sin)


def _swa_kernel(sink_ref, q_ref, kp_ref, kc_ref, vp_ref, vc_ref, o_ref):
    n = pl.program_id(1)
    shape = (BLK, 2 * BLK)
    qi = lax.broadcasted_iota(jnp.int32, shape, 0)
    c = lax.broadcasted_iota(jnp.int32, shape, 1)
    kj = (n - 1) * BLK + c
    delta = qi + BLK - c
    ok = (delta >= 0) & (delta < BLK) & (kj >= PAD)
    grp = SWA_HEADS // SWA_KV_HEADS
    outs = []
    for j in range(SWA_KV_HEADS):
        cols = slice(j * HEAD_DIM, (j + 1) * HEAD_DIM)
        kk = jnp.concatenate([kp_ref[:, cols], kc_ref[:, cols]], axis=0)
        vv = jnp.concatenate([vp_ref[:, cols], vc_ref[:, cols]], axis=0)
        for g in range(grp):
            hd = j * grp + g
            qh = q_ref[:, hd * HEAD_DIM:(hd + 1) * HEAD_DIM]
            s = lax.dot_general(qh, kk, (((1,), (1,)), ((), ())), preferred_element_type=F32) * (HEAD_DIM ** -0.5)
            s = jnp.where(ok, s, NEG)
            sink = sink_ref[hd]
            m = jnp.maximum(jnp.max(s, axis=-1, keepdims=True), sink)
            p = jnp.exp(s - m)
            denom = jnp.sum(p, axis=-1, keepdims=True) + jnp.exp(sink - m)
            outs.append(jnp.dot(p.astype(BF16), vv, preferred_element_type=F32) / denom)
    o_ref[...] = jnp.concatenate(outs, axis=1).astype(o_ref.dtype)


def _swa(proj, sinks, batch, nb):
    rows = proj.shape[0]
    kvw = SWA_KV_HEADS * HEAD_DIM

    def cur(col):
        return lambda b, n: (b * nb + n, col)

    def prev(col):
        return lambda b, n: (b * nb + jnp.maximum(n - 1, 0), col)

    return pl.pallas_call(
        _swa_kernel,
        grid=(batch, nb),
        in_specs=[
            pl.BlockSpec(memory_space=pltpu.SMEM),
            pl.BlockSpec((BLK, SWA_WIDTH), cur(0)),
            pl.BlockSpec((BLK, kvw), prev(COL_KA // kvw)),
            pl.BlockSpec((BLK, kvw), cur(COL_KA // kvw)),
            pl.BlockSpec((BLK, kvw), prev(COL_VA // kvw)),
            pl.BlockSpec((BLK, kvw), cur(COL_VA // kvw)),
        ],
        out_specs=pl.BlockSpec((BLK, SWA_WIDTH), cur(0)),
        out_shape=jax.ShapeDtypeStruct((rows, SWA_WIDTH), BF16),
        compiler_params=_params(("parallel", "parallel")),
        name="swa",
    )(sinks, proj, proj, proj, proj, proj)


DIFF_QBLOCKS = 2


def _diff_kernel(lam_ref, g_ref, q_ref, k_ref, v_ref, o_ref, *, lam_init, nb):
    lane = lax.broadcasted_iota(jnp.int32, (BLK, LANES), 1)
    lv = lam_ref[...]
    lam = (jnp.exp(jnp.sum(lv[0:1] * lv[1:2], axis=-1, keepdims=True))
           - jnp.exp(jnp.sum(lv[2:3] * lv[3:4], axis=-1, keepdims=True)) + lam_init)
    gain = g_ref[...] * (1.0 - lam_init)
    scale = HEAD_DIM ** -0.5
    for n0 in range(0, nb, DIFF_QBLOCKS):
        blocks = range(n0, min(n0 + DIFF_QBLOCKS, nb))
        ke = (blocks[-1] + 1) * BLK
        kx = k_ref[0:ke, :]
        vx = v_ref[0:ke, :]
        rows = 2 * BLK * len(blocks)
        pieces = []
        for n in blocks:
            q = q_ref[n * BLK:(n + 1) * BLK, :] * scale
            zero = jnp.zeros_like(q)
            pieces += [jnp.where(lane < HEAD_DIM, q, zero), jnp.where(lane >= HEAD_DIM, q, zero)]
        s = lax.dot_general(jnp.concatenate(pieces, axis=0), kx, (((1,), (1,)), ((), ())),
                            preferred_element_type=F32)
        r = lax.broadcasted_iota(jnp.int32, (rows, BLK), 0)
        qpos = (n0 + jnp.right_shift(r, (2 * BLK).bit_length() - 1)) * BLK + (r & (BLK - 1))
        col = lax.broadcasted_iota(jnp.int32, (rows, BLK), 1)
        key_blocks = []
        for t in range(blocks[-1] + 1):
            blk = s[:, t * BLK:(t + 1) * BLK]
            ok = None
            if t >= n0:
                ok = t * BLK + col <= qpos
            if t == 0:
                ok = (col >= PAD) if ok is None else ok & (col >= PAD)
            key_blocks.append(blk if ok is None else jnp.where(ok, blk, NEG))
        s = jnp.concatenate(key_blocks, axis=1)
        m = jnp.max(s, axis=-1, keepdims=True)
        p = jnp.exp(s - m)
        denom = jnp.sum(p, axis=-1, keepdims=True)
        heads = jnp.dot(p.astype(BF16), vx, preferred_element_type=F32) * (1.0 / denom)
        for i, n in enumerate(blocks):
            o = heads[2 * i * BLK:(2 * i + 1) * BLK] - lam * heads[(2 * i + 1) * BLK:(2 * i + 2) * BLK]
            o = o * lax.rsqrt(jnp.mean(o * o, axis=-1, keepdims=True) + NORM_EPS) * gain
            o_ref[n * BLK:(n + 1) * BLK, :] = o.astype(o_ref.dtype)


def _diff_attn(qd, kd, vd, lam_vecs, subln_g, lam_init, batch, nb):
    rows = qd.shape[1]
    lp = nb * BLK
    head_rows = pl.BlockSpec((None, lp, LANES), lambda b, h: (h, b, 0))
    return pl.pallas_call(
        functools.partial(_diff_kernel, lam_init=lam_init, nb=nb),
        grid=(batch, DIFF_HEADS),
        in_specs=[
            pl.BlockSpec((4, HEAD_DIM), lambda b, h: (0, 0)),
            pl.BlockSpec((1, DIFF_V_DIM), lambda b, h: (0, 0)),
            head_rows, head_rows, head_rows,
        ],
        out_specs=head_rows,
        out_shape=jax.ShapeDtypeStruct((DIFF_HEADS, rows, DIFF_V_DIM), BF16),
        compiler_params=_params(("parallel", "parallel")),
        name="diff_attn",
    )(lam_vecs, subln_g, qd, kd, vd)


def _ssm_row_chunk(rows):
    for c in (544, 512, 384, 256, 128, 64, 32, 16):
        if rows % c == 0:
            return c
    raise ValueError(rows)


def _ssm_expand_operators(kc_ref, bc_ref, cc_ref, k_ref, bw_ref, cw_ref):
    def iota(shape, dim):
        return lax.broadcasted_iota(jnp.int32, shape, dim)

    def group(idx, width):
        return jnp.right_shift(idx, width.bit_length() - 1) & (SSM_PACK - 1)

    hs, ps, tl = SSM_GROUP, SSM_STATE, SSM_T * LANES
    spread_h = ((iota((LANES, hs), 0) & (hs - 1)) == iota((LANES, hs), 1)).astype(BF16)
    kt = jnp.dot(spread_h, kc_ref[...], preferred_element_type=F32)
    same = group(iota((LANES, tl), 0), hs) == group(iota((LANES, tl), 1), hs)
    k_ref[...] = jnp.where(same, kt, 0.0).astype(BF16)
    spread_s = (iota((ps, SSM_PSTATE), 0) == (iota((ps, SSM_PSTATE), 1) & (ps - 1))).astype(BF16)
    shape = (tl, SSM_PSTATE)
    same = group(iota(shape, 0), hs) == group(iota(shape, 1), ps)
    for part in range(2):
        bw = jnp.dot(bc_ref[part], spread_s, preferred_element_type=F32)
        bw_ref[:, part * SSM_PSTATE:(part + 1) * SSM_PSTATE] = jnp.where(same, bw, 0.0).astype(BF16)
    shape = (LANES, SSM_PSTATE)
    same = group(iota(shape, 0), hs) == group(iota(shape, 1), ps)
    for j in range(SSM_T):
        for part in range(2):
            ct = jnp.dot(spread_h, cc_ref[part, j], preferred_element_type=F32)
            cw_ref[j, :, part * SSM_PSTATE:(part + 1) * SSM_PSTATE] = jnp.where(same, ct, 0.0).astype(BF16)


def _ssm_kernel(u_ref, kc_ref, bc_ref, cc_ref, a_ref, d_ref, o_ref, up_ref, y_ref, e_ref, s_ref, k_ref, bw_ref,
                cw_ref, *, batch, lp):
    rows = batch * lp
    n_chunks = lp // SSM_T
    tot_chunks = rows // SSM_T
    _ssm_expand_operators(kc_ref, bc_ref, cc_ref, k_ref, bw_ref, cw_ref)
    contract_lanes = (((1,), (1,)), ((), ()))
    up_ref[0:SSM_T, :] = jnp.zeros((SSM_T, LANES), F32)
    up_ref[SSM_T:, :] = u_ref[...].astype(F32)
    for b in range(batch):
        up_ref[SSM_T + b * lp:SSM_T + b * lp + PAD, :] = jnp.zeros((PAD, LANES), F32)

    ch = _ssm_row_chunk(rows)
    rmod = lax.broadcasted_iota(jnp.int32, (ch, LANES), 0) & (SSM_T - 1)
    for r0 in range(0, rows, ch):
        pieces = []
        for k in range(SSM_T):
            sl = up_ref[SSM_T - k + r0:SSM_T - k + r0 + ch, :]
            pieces.append(jnp.where(rmod >= k, sl, 0.0).astype(BF16))
        y_ref[r0:r0 + ch, :] = lax.dot_general(jnp.concatenate(pieces, axis=1), k_ref[...], contract_lanes,
                                               preferred_element_type=F32)

    pieces = [up_ref[pl.ds(SSM_T + i, tot_chunks, stride=SSM_T), :].astype(BF16) for i in range(SSM_T)]
    e_ref[...] = jnp.dot(jnp.concatenate(pieces, axis=1), bw_ref[...], preferred_element_type=F32)

    a_re = a_ref[:, :SSM_PSTATE]
    a_im = a_ref[:, SSM_PSTATE:]

    def step(c, carry):
        new = []
        for b in range(batch):
            sr, si = carry[b]
            idx = b * n_chunks + c
            s_ref[pl.ds(idx, 1), :SSM_PSTATE] = sr
            s_ref[pl.ds(idx, 1), SSM_PSTATE:] = si
            er = e_ref[pl.ds(idx, 1), :SSM_PSTATE]
            ei = e_ref[pl.ds(idx, 1), SSM_PSTATE:]
            new.append((a_re * sr - a_im * si + er, a_re * si + a_im * sr + ei))
        return tuple(new)

    zero = jnp.zeros((1, SSM_PSTATE), F32)
    lax.fori_loop(0, n_chunks, step, tuple((zero, zero) for _ in range(batch)))

    sb = s_ref[...].astype(BF16)
    for j in range(SSM_T):
        z = lax.dot_general(sb, cw_ref[j], contract_lanes, preferred_element_type=F32)
        idx = pl.ds(j, tot_chunks, stride=SSM_T)
        y_ref[idx, :] = y_ref[idx, :] + z

    y = y_ref[...] + d_ref[...] * up_ref[SSM_T:, :]
    o_ref[...] = jax.nn.gelu(y, approximate=True).astype(o_ref.dtype)


def _ssm(us, kc, bc, cc, a_t, dskip, batch, lp):
    rows = us.shape[1]
    tot_chunks = rows // SSM_T
    return pl.pallas_call(
        functools.partial(_ssm_kernel, batch=batch, lp=lp),
        grid=(SSM_PACKS,),
        in_specs=[
            pl.BlockSpec((None, rows, LANES), lambda p: (p, 0, 0)),
            pl.BlockSpec((None, SSM_GROUP, SSM_T * LANES), lambda p: (p, 0, 0)),
            pl.BlockSpec((None, 2, SSM_T * LANES, SSM_STATE), lambda p: (p, 0, 0, 0)),
            pl.BlockSpec((None, 2, SSM_T, SSM_GROUP, SSM_PSTATE), lambda p: (p, 0, 0, 0, 0)),
            pl.BlockSpec((None, 1, 2 * SSM_PSTATE), lambda p: (p, 0, 0)),
            pl.BlockSpec((None, 1, LANES), lambda p: (p, 0, 0)),
        ],
        out_specs=pl.BlockSpec((None, rows, LANES), lambda p: (p, 0, 0)),
        out_shape=jax.ShapeDtypeStruct((SSM_PACKS, rows, LANES), F32),
        scratch_shapes=[
            pltpu.VMEM((rows + SSM_T, LANES), F32),
            pltpu.VMEM((rows, LANES), F32),
            pltpu.VMEM((tot_chunks, 2 * SSM_PSTATE), F32),
            pltpu.VMEM((tot_chunks, 2 * SSM_PSTATE), F32),
            pltpu.VMEM((LANES, SSM_T * LANES), BF16),
            pltpu.VMEM((SSM_T * LANES, 2 * SSM_PSTATE), BF16),
            pltpu.VMEM((SSM_T, LANES, 2 * SSM_PSTATE), BF16),
        ],
        compiler_params=_params(("parallel",)),
        name="ssm",
    )(us, kc, bc, cc, a_t, dskip)


def _ssm_operators(lam_re, lam_im, b_re, b_im, c_re, c_im, log_dt):
    p, h = SSM_STATE, SSM_GROUP
    dt = jnp.exp(log_dt.astype(F32))[:, None]
    lr, li = lam_re.astype(F32), lam_im.astype(F32)
    mag = jnp.exp(lr * dt)
    abar_re, abar_im = mag * jnp.cos(li * dt), mag * jnp.sin(li * dt)
    den = lr * lr + li * li
    nr = abar_re - 1.0
    f_re = (nr * lr + abar_im * li) / den
    f_im = (abar_im * lr - nr * li) / den
    br, bi = b_re.astype(F32), b_im.astype(F32)
    bbar_re = f_re[..., None] * br - f_im[..., None] * bi
    bbar_im = f_re[..., None] * bi + f_im[..., None] * br
    ks = jnp.arange(SSM_T + 1, dtype=F32)[:, None, None]
    pmag = jnp.exp(ks * (lr * dt)[None])
    pw_re = pmag * jnp.cos(ks * (li * dt)[None])
    pw_im = pmag * jnp.sin(ks * (li * dt)[None])
    cr, ci = c_re.astype(F32), c_im.astype(F32)

    packs, q = SSM_PACKS, SSM_PACK

    cb_re = cr[:, :, :, None] * bbar_re[:, None, :, :] - ci[:, :, :, None] * bbar_im[:, None, :, :]
    cb_im = cr[:, :, :, None] * bbar_im[:, None, :, :] + ci[:, :, :, None] * bbar_re[:, None, :, :]
    klag = (jnp.einsum('kgs,gosi->kgoi', pw_re[:SSM_T], cb_re)
            - jnp.einsum('kgs,gosi->kgoi', pw_im[:SSM_T], cb_im))
    kc = klag.reshape(SSM_T, packs, q, h, h).transpose(1, 3, 0, 2, 4).reshape(packs, h, SSM_T * LANES)

    rev_re, rev_im = pw_re[SSM_T - 1::-1, :, None, :], pw_im[SSM_T - 1::-1, :, None, :]
    bt_re, bt_im = jnp.swapaxes(bbar_re, 1, 2)[None], jnp.swapaxes(bbar_im, 1, 2)[None]

    def inj_by_pack(x):
        return x.reshape(SSM_T, packs, q * h, p).transpose(1, 0, 2, 3).reshape(packs, SSM_T * LANES, p)

    bc = jnp.stack([inj_by_pack(rev_re * bt_re - rev_im * bt_im), inj_by_pack(rev_re * bt_im + rev_im * bt_re)], axis=1)

    ca_re = cr[None] * pw_re[1:, :, None, :] - ci[None] * pw_im[1:, :, None, :]
    ca_im = cr[None] * pw_im[1:, :, None, :] + ci[None] * pw_re[1:, :, None, :]

    def out_by_pack(x):
        return x.reshape(SSM_T, packs, q, h, p).transpose(1, 0, 3, 2, 4).reshape(packs, SSM_T, h, SSM_PSTATE)

    cc = jnp.stack([out_by_pack(ca_re), out_by_pack(-ca_im)], axis=1)

    a_t = jnp.concatenate([pw_re[SSM_T].reshape(packs, 1, SSM_PSTATE),
                           pw_im[SSM_T].reshape(packs, 1, SSM_PSTATE)], axis=-1)
    return kc.astype(BF16), bc.astype(BF16), cc.astype(BF16), a_t


OUT_TN = 512


def _split_bf16(x):
    hi = x.astype(BF16)
    return hi, (x - hi.astype(F32)).astype(BF16)


def _route(hn, rwh_ref, rwl_ref, rb_ref):
    hi, lo = _split_bf16(hn)
    logits = (jnp.dot(hi, rwh_ref[...], preferred_element_type=F32)
              + (jnp.dot(lo, rwh_ref[...], preferred_element_type=F32)
                 + jnp.dot(hi, rwl_ref[...], preferred_element_type=F32))) + rb_ref[...]
    lane = lax.broadcasted_iota(jnp.int32, logits.shape, 1)

    def first_argmax(v, vmax):
        return jnp.min(jnp.where(v == vmax, lane, LANES), axis=-1, keepdims=True)

    gl = jnp.where(lane < N_EXPERT_GROUPS, logits, NEG)
    gmax = jnp.max(gl, axis=-1, keepdims=True)
    g_sel = first_argmax(gl, gmax)
    p_g = 1.0 / jnp.sum(jnp.exp(gl - gmax), axis=-1, keepdims=True)
    e_lane = lane - N_EXPERT_GROUPS
    in_group = (e_lane >= 0) & (e_lane < N_EXPERTS) & (jnp.right_shift(e_lane, 3) == g_sel)
    el = jnp.where(in_group, logits, NEG)
    m1 = jnp.max(el, axis=-1, keepdims=True)
    i1 = first_argmax(el, m1)
    el2 = jnp.where(lane == i1, NEG, el)
    m2 = jnp.max(el2, axis=-1, keepdims=True)
    i2 = first_argmax(el2, m2)
    z = jnp.sum(jnp.exp(el - m1), axis=-1, keepdims=True)
    p1 = 1.0 / z
    p2 = jnp.exp(m2 - m1) / z
    w1 = p1 / (p1 + p2) * p_g
    w2 = p2 / (p1 + p2) * p_g
    wts = jnp.where(lane == 0, w1, jnp.where(lane == 1, w2, 0.0))
    ids = jnp.where(lane == 0, i1 - N_EXPERT_GROUPS, jnp.where(lane == 1, i2 - N_EXPERT_GROUPS, 0))
    return wts, ids


def _assignment_ranks(ids):
    tm = ids.shape[0]
    lane = lax.broadcasted_iota(jnp.int32, ids.shape, 1)
    oh0 = lane == ids[:, 0:1]
    oh1 = lane == ids[:, 1:2]
    hits = (oh0 | oh1).astype(BF16)
    r = lax.broadcasted_iota(jnp.int32, (tm, tm), 0)
    c = lax.broadcasted_iota(jnp.int32, (tm, tm), 1)
    before = jnp.dot((c < r).astype(BF16), hits, preferred_element_type=F32)
    rank0 = jnp.sum(jnp.where(oh0, before, 0.0), axis=-1, keepdims=True)
    rank1 = jnp.sum(jnp.where(oh1, before, 0.0), axis=-1, keepdims=True)
    ranks = jnp.where(lane == 0, rank0, jnp.where(lane == 1, rank1, 0.0)).astype(jnp.int32)
    return ranks, jnp.sum(hits.astype(F32), axis=0, keepdims=True).astype(jnp.int32)


def _mixer_out_kernel(h_ref, a_ref, y_ref, c_ref, wo_ref, wglu_ref, g_ref, rwh_ref, rwl_ref, rb_ref,
                      o_ref, hn_ref, wt_ref, id_ref, rank_ref, cnt_ref):
    half = h_ref.shape[0] // 2
    halves = [slice(r * half, (r + 1) * half) for r in range(2)]
    for rows in halves:
        y = jnp.concatenate([y_ref[p, rows, :] for p in range(y_ref.shape[0])], axis=1)
        c = jnp.concatenate([c_ref[hd, rows, :] for hd in range(c_ref.shape[0])], axis=1)
        gate = jnp.dot(y.astype(BF16), wglu_ref[...], preferred_element_type=F32)
        b = (y * jax.nn.sigmoid(gate)).astype(BF16)
        mixed = jnp.concatenate([a_ref[rows, :], b, c], axis=1)
        for j in range(D_MODEL // OUT_TN):
            cols = slice(j * OUT_TN, (j + 1) * OUT_TN)
            o_ref[rows, cols] = h_ref[rows, cols] + jnp.dot(mixed, wo_ref[:, cols], preferred_element_type=F32)
    for rows in halves:
        hn = _rms_norm(o_ref[rows, :], g_ref[...])
        hn_ref[rows, :] = hn.astype(BF16)
        wts, ids = _route(hn, rwh_ref, rwl_ref, rb_ref)
        wt_ref[rows, :] = wts
        id_ref[rows, :] = ids
    ranks, counts = _assignment_ranks(id_ref[...])
    rank_ref[...] = ranks
    cnt_ref[...] = jnp.broadcast_to(counts, cnt_ref.shape)


def _mixer_out(h, oa, yg, oc, wo, wglu, g, rw_hi, rw_lo, rb):
    rows = h.shape[0]
    tm = _row_tile(rows)

    def row_spec(width):
        return pl.BlockSpec((tm, width), lambda i: (i, 0))

    def whole(shape):
        return pl.BlockSpec(shape, lambda i: (0, 0))

    def blocked(width):
        return pl.BlockSpec((width // LANES, tm, LANES), lambda i: (0, i, 0))

    return pl.pallas_call(
        _mixer_out_kernel,
        grid=(rows // tm,),
        in_specs=[
            row_spec(D_MODEL), row_spec(SWA_WIDTH), blocked(SSM_WIDTH), blocked(DIFF_WIDTH),
            whole((D_MODEL, D_MODEL)), whole((SSM_WIDTH, SSM_WIDTH)), whole((1, D_MODEL)),
            whole((D_MODEL, LANES)), whole((D_MODEL, LANES)), whole((1, LANES)),
        ],
        out_specs=[row_spec(D_MODEL), row_spec(D_MODEL), row_spec(LANES), row_spec(LANES), row_spec(LANES),
                   pl.BlockSpec((None, 8, LANES), lambda i: (i, 0, 0))],
        out_shape=[
            jax.ShapeDtypeStruct((rows, D_MODEL), F32),
            jax.ShapeDtypeStruct((rows, D_MODEL), BF16),
            jax.ShapeDtypeStruct((rows, LANES), F32),
            jax.ShapeDtypeStruct((rows, LANES), jnp.int32),
            jax.ShapeDtypeStruct((rows, LANES), jnp.int32),
            jax.ShapeDtypeStruct((rows // tm, 8, LANES), jnp.int32),
        ],
        compiler_params=_params(("parallel",)),
        name="mixer_out",
    )(h, oa, yg, oc, wo, wglu, g, rw_hi, rw_lo, rb)


MOE_UNIT = 16
MOE_DTYPE = BF16


def _moe_tile_rows(tt):
    return 2 * tt + N_EXPERTS * MOE_UNIT


def _moe_tables(tile_cnt, n_blk):
    cnt = tile_cnt[:, 0, :N_EXPERTS]
    seg = (cnt + MOE_UNIT - 1) // MOE_UNIT * MOE_UNIT
    size = jnp.sum(seg, axis=0)
    padded = (size + MOE_BM - 1) // MOE_BM * MOE_BM
    pend = jnp.cumsum(padded)
    pstart = pend - padded
    base = pstart[None, :] + jnp.cumsum(seg, axis=0) - seg
    loc = jnp.cumsum(seg, axis=1) - seg
    units = seg // MOE_UNIT
    blk_start = jnp.arange(n_blk, dtype=jnp.int32) * MOE_BM
    blk_expert = jnp.minimum(jnp.sum((pend[None, :] <= blk_start[:, None]).astype(jnp.int32), axis=1), N_EXPERTS - 1)
    ex = jnp.arange(N_EXPERTS, dtype=jnp.int32)
    nonempty = size > 0
    run_slot = (jnp.cumsum(nonempty.astype(jnp.int32)) - 1) % WEIGHT_SLOTS
    later = (ex[None, :] > ex[:, None]) & nonempty[None, :]
    nxt = jnp.min(jnp.where(later, ex[None, :], N_EXPERTS), axis=1)
    nxt = jnp.where(nxt == N_EXPERTS, -1, nxt)
    nxt2 = jnp.where(nxt >= 0, nxt[jnp.maximum(nxt, 0)], -1)
    i32 = jnp.int32
    loc_lanes = jnp.pad(loc, ((0, 0), (0, LANES - N_EXPERTS)))
    return dict(
        base=base.reshape(-1).astype(i32), loc=loc.reshape(-1).astype(i32), units=units.reshape(-1).astype(i32),
        tile_units=jnp.sum(units, axis=1).astype(i32),
        tail_start=(pstart + size).astype(i32), tail_units=((padded - size) // MOE_UNIT).astype(i32),
        loc_lanes=jnp.broadcast_to(loc_lanes[:, None, :], (loc.shape[0], 8, LANES)).astype(i32),
        blk_expert=blk_expert.astype(i32), blk_slot=run_slot[blk_expert].astype(i32),
        blk_next1=nxt[blk_expert].astype(i32), blk_next2=nxt2[blk_expert].astype(i32),
        n_used=(pend[-1] // MOE_BM).astype(i32).reshape(1))


def _segment_copies(tile, hbm, buf, sem, to_hbm, base_ref, loc_ref, units_ref):
    for e in range(N_EXPERTS):
        k = tile * N_EXPERTS + e
        row_hbm, row_buf = base_ref[k], loc_ref[k]

        def start(u, carry):
            a = hbm.at[pl.ds(pl.multiple_of(row_hbm + u * MOE_UNIT, MOE_UNIT), MOE_UNIT), :]
            b = buf.at[pl.ds(pl.multiple_of(row_buf + u * MOE_UNIT, MOE_UNIT), MOE_UNIT), :]
            (pltpu.make_async_copy(b, a, sem) if to_hbm else pltpu.make_async_copy(a, b, sem)).start()
            return carry

        lax.fori_loop(0, units_ref[k], start, 0)


def _wait_unit_copies(count, hbm, buf, sem):
    def wait(u, carry):
        pltpu.make_async_copy(hbm.at[pl.ds(0, MOE_UNIT), :], buf.at[pl.ds(0, MOE_UNIT), :], sem).wait()
        return carry

    lax.fori_loop(0, count, wait, 0)


def _local_rows(ids, ranks, loc_lanes):
    lane = lax.broadcasted_iota(jnp.int32, ids.shape, 1)
    loc = loc_lanes.astype(F32)
    start0 = jnp.sum(jnp.where(lane == ids[:, 0:1], loc, 0.0), axis=-1, keepdims=True).astype(jnp.int32)
    start1 = jnp.sum(jnp.where(lane == ids[:, 1:2], loc, 0.0), axis=-1, keepdims=True).astype(jnp.int32)
    return start0 + ranks[:, 0:1], start1 + ranks[:, 1:2]


def _dispatch_kernel(base_ref, loc_ref, units_ref, tile_units_ref, tail_start_ref, tail_units_ref, nused_ref,
                     hn_ref, ids_ref, rank_ref, locv_ref, x_hbm, lrow_ref, ybuf, zbuf, sem):
    j = pl.program_id(0)
    n = pl.num_programs(0)
    slot = j % 2
    tt, rt = hn_ref.shape[0], ybuf.shape[1]

    @pl.when(j >= 2)
    def _():
        _wait_unit_copies(tile_units_ref[j - 2], x_hbm, ybuf.at[slot], sem.at[slot])

    row0, row1 = _local_rows(ids_ref[...], rank_ref[...], locv_ref[0:1, :])
    lane = lax.broadcasted_iota(jnp.int32, lrow_ref.shape, 1)
    lrow_ref[...] = jnp.where(lane == 0, row0, jnp.where(lane == 1, row1, 0))
    col = lax.broadcasted_iota(jnp.int32, (tt, rt), 1)
    onehot = ((col == row0) | (col == row1)).astype(BF16)
    compacted = lax.dot_general(onehot, hn_ref[...], (((0,), (0,)), ((), ())), preferred_element_type=F32)
    ybuf[slot] = compacted.astype(ybuf.dtype)
    _segment_copies(j, x_hbm, ybuf.at[slot], sem.at[slot], True, base_ref, loc_ref, units_ref)

    @pl.when(j == n - 1)
    def _():
        zbuf[...] = jnp.zeros(zbuf.shape, zbuf.dtype)
        zunit = zbuf.at[pl.ds(0, MOE_UNIT), :]
        total = 0
        for e in range(N_EXPERTS):
            row = tail_start_ref[e]

            def start(u, carry):
                dst = x_hbm.at[pl.ds(pl.multiple_of(row + u * MOE_UNIT, MOE_UNIT), MOE_UNIT), :]
                pltpu.make_async_copy(zunit, dst, sem.at[2]).start()
                return carry

            lax.fori_loop(0, tail_units_ref[e], start, 0)
            total = total + tail_units_ref[e]
        _wait_unit_copies(total, x_hbm, zunit, sem.at[2])
        n_blk = x_hbm.shape[0] // MOE_BM

        def block_copy(blk):
            dst = x_hbm.at[pl.ds(pl.multiple_of(blk * MOE_BM, MOE_BM), MOE_BM), :]
            return pltpu.make_async_copy(zbuf, dst, sem.at[3])

        def start_block(blk, carry):
            block_copy(blk).start()
            return carry

        def wait_block(blk, carry):
            block_copy(blk).wait()
            return carry

        lax.fori_loop(nused_ref[0], n_blk, start_block, 0)
        lax.fori_loop(nused_ref[0], n_blk, wait_block, 0)

        @pl.when(j >= 1)
        def _():
            _wait_unit_copies(tile_units_ref[jnp.maximum(j - 1, 0)], x_hbm, ybuf.at[1 - slot], sem.at[1 - slot])

        _wait_unit_copies(tile_units_ref[j], x_hbm, ybuf.at[slot], sem.at[slot])


def _dispatch(hn, ids, ranks, tables, tt, n_rows):
    rows = hn.shape[0]
    rt = _moe_tile_rows(tt)
    row_spec = pl.BlockSpec((tt, LANES), lambda j, *_: (j, 0))
    grid_spec = pltpu.PrefetchScalarGridSpec(
        num_scalar_prefetch=7,
        grid=(rows // tt,),
        in_specs=[pl.BlockSpec((tt, D_MODEL), lambda j, *_: (j, 0)), row_spec, row_spec,
                  pl.BlockSpec((None, 8, LANES), lambda j, *_: (j, 0, 0))],
        out_specs=[pl.BlockSpec(memory_space=pl.ANY), row_spec],
        scratch_shapes=[pltpu.VMEM((2, rt, D_MODEL), MOE_DTYPE), pltpu.VMEM((MOE_BM, D_MODEL), MOE_DTYPE),
                        pltpu.SemaphoreType.DMA((4,))],
    )
    return pl.pallas_call(
        _dispatch_kernel,
        grid_spec=grid_spec,
        out_shape=[jax.ShapeDtypeStruct((n_rows, D_MODEL), MOE_DTYPE),
                   jax.ShapeDtypeStruct((rows, LANES), jnp.int32)],
        compiler_params=_params(("arbitrary",)),
        name="moe_dispatch",
    )(tables["base"], tables["loc"], tables["units"], tables["tile_units"], tables["tail_start"],
      tables["tail_units"], tables["n_used"], hn, ids, ranks, tables["loc_lanes"])


WEIGHT_DMA_PRIORITIES = (1, 0, 1)
WEIGHT_SLOTS = 3


def _expert_kernel(be_ref, slot_ref, nxt1_ref, nxt2_ref, nused_ref, x_ref, wg_hbm, wu_hbm, wd_hbm, o_ref,
                   wgf, wuf, wdf, wgb, wub, wdb, wsem, *, layer):
    i = pl.program_id(0)
    n_used = nused_ref[0]

    def weight_copies(expert, wslot):
        pairs = ((wg_hbm, wgf), (wu_hbm, wuf), (wd_hbm, wdf))
        return [pltpu.make_async_copy(w.at[layer, expert], buf.at[wslot], wsem.at[wslot]) for w, buf in pairs]

    def start_weights(expert, wslot):
        for c, priority in zip(weight_copies(expert, wslot), WEIGHT_DMA_PRIORITIES):
            c.start(priority=priority)

    @pl.when(i == 0)
    def _():
        start_weights(be_ref[0], 0)

        @pl.when(nxt1_ref[0] >= 0)
        def _():
            start_weights(nxt1_ref[0], 1)

    @pl.when(i < n_used)
    def _():
        @pl.when((i == 0) | (be_ref[i] != be_ref[jnp.maximum(i - 1, 0)]))
        def _():
            wslot = slot_ref[i]
            for c in weight_copies(be_ref[i], wslot):
                c.wait()

            @pl.when(nxt2_ref[i] >= 0)
            def _():
                start_weights(nxt2_ref[i], lax.rem(wslot + 2, WEIGHT_SLOTS))

            wgb[...] = wgf[wslot].astype(BF16)
            wub[...] = wuf[wslot].astype(BF16)
            wdb[...] = wdf[wslot].astype(BF16)

        x = x_ref[...]
        gate = jnp.dot(x, wgb[...], preferred_element_type=F32)
        up = jnp.dot(x, wub[...], preferred_element_type=F32)
        act = (jax.nn.silu(gate) * up).astype(BF16)
        o_ref[...] = jnp.dot(act, wdb[...], preferred_element_type=F32).astype(o_ref.dtype)

    @pl.when(i >= n_used)
    def _():
        o_ref[...] = jnp.zeros(o_ref.shape, o_ref.dtype)


def _experts(x_sorted, tables, wg, wu, wd, layer, n_blk):
    hbm = pl.BlockSpec(memory_space=pl.ANY)
    grid_spec = pltpu.PrefetchScalarGridSpec(
        num_scalar_prefetch=5,
        grid=(n_blk,),
        in_specs=[pl.BlockSpec((MOE_BM, D_MODEL), lambda i, be, slot, n1, n2, nu: (jnp.minimum(i, nu[0] - 1), 0)),
                  hbm, hbm, hbm],
        out_specs=pl.BlockSpec((MOE_BM, D_MODEL), lambda i, *_: (i, 0)),
        scratch_shapes=[
            pltpu.VMEM((WEIGHT_SLOTS, D_MODEL, D_EXPERT), F32),
            pltpu.VMEM((WEIGHT_SLOTS, D_MODEL, D_EXPERT), F32),
            pltpu.VMEM((WEIGHT_SLOTS, D_EXPERT, D_MODEL), F32),
            pltpu.VMEM((D_MODEL, D_EXPERT), BF16),
            pltpu.VMEM((D_MODEL, D_EXPERT), BF16),
            pltpu.VMEM((D_EXPERT, D_MODEL), BF16),
            pltpu.SemaphoreType.DMA((WEIGHT_SLOTS,)),
        ],
    )
    return pl.pallas_call(
        functools.partial(_expert_kernel, layer=layer),
        grid_spec=grid_spec,
        out_shape=jax.ShapeDtypeStruct((n_blk * MOE_BM, D_MODEL), MOE_DTYPE),
        compiler_params=_params(("arbitrary",)),
        name="moe_experts",
    )(tables["blk_expert"], tables["blk_slot"], tables["blk_next1"], tables["blk_next2"], tables["n_used"],
      x_sorted, wg, wu, wd)


def _combine_kernel(base_ref, loc_ref, units_ref, tile_units_ref, h_ref, wt_ref, lrow_ref, g_ref, y_hbm, o_ref,
                    ybuf, sem, *, final_norm):
    j = pl.program_id(0)
    n = pl.num_programs(0)
    slot = j % 2
    tt, rt = h_ref.shape[0], ybuf.shape[1]

    @pl.when(j == 0)
    def _():
        ybuf[...] = jnp.zeros(ybuf.shape, ybuf.dtype)
        _segment_copies(0, y_hbm, ybuf.at[0], sem.at[0], False, base_ref, loc_ref, units_ref)

    @pl.when(j + 1 < n)
    def _():
        _segment_copies(j + 1, y_hbm, ybuf.at[1 - slot], sem.at[1 - slot], False, base_ref, loc_ref, units_ref)

    _wait_unit_copies(tile_units_ref[j], y_hbm, ybuf.at[slot], sem.at[slot])
    lrow = lrow_ref[...]
    wt = wt_ref[...]
    col = lax.broadcasted_iota(jnp.int32, (tt, rt), 1)
    pick0 = (col == lrow[:, 0:1]).astype(BF16)
    pick1 = (col == lrow[:, 1:2]).astype(BF16)
    for c in range(D_MODEL // OUT_TN):
        cols = slice(c * OUT_TN, (c + 1) * OUT_TN)
        y = ybuf[slot, :, cols]
        y0 = jnp.dot(pick0, y, preferred_element_type=F32)
        y1 = jnp.dot(pick1, y, preferred_element_type=F32)
        o_ref[:, cols] = h_ref[:, cols] + wt[:, 0:1] * y0 + wt[:, 1:2] * y1
    if final_norm:
        o_ref[...] = _rms_norm(o_ref[...], g_ref[...])


def _combine(h, wts, lrows, y_sorted, tables, g, tt, final_norm):
    rows = h.shape[0]
    rt = _moe_tile_rows(tt)
    grid_spec = pltpu.PrefetchScalarGridSpec(
        num_scalar_prefetch=4,
        grid=(rows // tt,),
        in_specs=[
            pl.BlockSpec((tt, D_MODEL), lambda j, *_: (j, 0)),
            pl.BlockSpec((tt, LANES), lambda j, *_: (j, 0)),
            pl.BlockSpec((tt, LANES), lambda j, *_: (j, 0)),
            pl.BlockSpec((1, D_MODEL), lambda j, *_: (0, 0)),
            pl.BlockSpec(memory_space=pl.ANY),
        ],
        out_specs=pl.BlockSpec((tt, D_MODEL), lambda j, *_: (j, 0)),
        scratch_shapes=[pltpu.VMEM((2, rt, D_MODEL), MOE_DTYPE), pltpu.SemaphoreType.DMA((2,))],
    )
    return pl.pallas_call(
        functools.partial(_combine_kernel, final_norm=final_norm),
        grid_spec=grid_spec,
        out_shape=jax.ShapeDtypeStruct((rows, D_MODEL), F32),
        compiler_params=_params(("arbitrary",)),
        name="moe_combine",
    )(tables["base"], tables["loc"], tables["units"], tables["tile_units"], h, wts, lrows, g, y_sorted)


def _moe(h, hn, wts, ids, ranks, tile_cnt, wg, wu, wd, layer, final_g):
    rows = h.shape[0]
    n_tiles = tile_cnt.shape[0]
    tt = rows // n_tiles
    n_rows = 2 * rows + n_tiles * N_EXPERTS * (MOE_UNIT - 1) + N_EXPERTS * (MOE_BM - 1)
    n_blk = -(-n_rows // MOE_BM)
    tables = _moe_tables(tile_cnt, n_blk)
    x_sorted, lrows = _dispatch(hn, ids, ranks, tables, tt, n_blk * MOE_BM)
    y_sorted = _experts(x_sorted, tables, wg, wu, wd, layer, n_blk)
    final_norm = final_g is not None
    g = final_g if final_norm else jnp.ones((1, D_MODEL), F32)
    return _combine(h, wts, lrows, y_sorted, tables, g, tt, final_norm)


def _rope_tables(batch, lp):
    half = HEAD_DIM // 2
    inv = ROPE_THETA ** (-jnp.arange(half, dtype=F32) / half)
    pos = (jnp.arange(lp, dtype=jnp.int32) - PAD).astype(F32)
    ang = pos[:, None] * inv[None, :]
    cos = jnp.tile(jnp.cos(ang), (batch, LANES // half))
    sin = jnp.sin(ang)
    sin = jnp.tile(jnp.concatenate([-sin, sin], axis=1), (batch, LANES // HEAD_DIM))
    return cos, sin


def kernel(x, meta_tokens, attn_norm_g, ffn_norm_g, w_in, swa_sinks, ssm_lambda_re, ssm_lambda_im, ssm_b_re, ssm_b_im, ssm_c_re, ssm_c_im, ssm_d, ssm_log_dt, ssm_w_glu, diff_lambda_q1, diff_lambda_k1, diff_lambda_q2, diff_lambda_k2, diff_subln_g, w_out, router_group_w, router_group_b, router_expert_w, router_expert_b, moe_w_gate, moe_w_up, moe_w_down, final_norm_g):
    batch, seq, d = x.shape
    depth = w_in.shape[0]
    lp = PAD + N_META + seq
    nb = lp // BLK
    meta = jnp.broadcast_to(meta_tokens[None].astype(x.dtype), (batch, N_META, d))
    h = jnp.concatenate([jnp.zeros((batch, PAD, d), x.dtype), meta, x], axis=1).reshape(batch * lp, d)
    cos, sin = _rope_tables(batch, lp)
    for l in range(depth):
        att, us, qd, kd, vd = _norm_inproj(h, attn_norm_g[l][None], w_in[l].astype(BF16), cos, sin)
        o_a = _swa(att, swa_sinks[l].astype(F32), batch, nb)
        kc, bc, cc, a_t = _ssm_operators(ssm_lambda_re[l], ssm_lambda_im[l], ssm_b_re[l], ssm_b_im[l],
                                         ssm_c_re[l], ssm_c_im[l], ssm_log_dt[l])
        yg = _ssm(us, kc, bc, cc, a_t, ssm_d[l].astype(F32).reshape(SSM_PACKS, 1, LANES), batch, lp)
        lam_init = 0.8 - 0.6 * math.exp(-0.3 * l)
        lam_vecs = jnp.stack([diff_lambda_q1[l], diff_lambda_k1[l], diff_lambda_q2[l], diff_lambda_k2[l]]).astype(F32)
        o_c = _diff_attn(qd, kd, vd, lam_vecs, diff_subln_g[l][None].astype(F32), lam_init, batch, nb)
        wo = w_out[l].astype(BF16)
        rw = jnp.concatenate([router_group_w[l], router_expert_w[l]], axis=1).astype(F32)
        rw = jnp.pad(rw, ((0, 0), (0, LANES - rw.shape[1])))
        rb = jnp.concatenate([router_group_b[l], router_expert_b[l]]).astype(F32)
        rb = jnp.pad(rb, (0, LANES - rb.shape[0]))[None]
        rw_hi, rw_lo = _split_bf16(rw)
        g_ffn = ffn_norm_g[l][None].astype(F32)
        h, hn, wts, ids, ranks, tile_cnt = _mixer_out(h, o_a, yg, o_c, wo, ssm_w_glu[l].astype(BF16), g_ffn, rw_hi,
                                                      rw_lo, rb)
        final_g = final_norm_g[None].astype(F32) if l == depth - 1 else None
        h = _moe(h, hn, wts, ids, ranks, tile_cnt, moe_w_gate, moe_w_up, moe_w_down, l, final_g)
    return h.reshape(batch, lp, d)[:, PAD + N_META:]
```

```python
import functools
import math

import jax
import jax.numpy as jnp
from jax import lax
from jax.experimental import pallas as pl
from jax.experimental.pallas import tpu as pltpu

D_MODEL = 2048
N_META = 16
BLK = 128
PAD = BLK - N_META
ROPE_THETA = 10000.0
NORM_EPS = 1e-5
NEG = -1e30
SWA_HEADS = 12
SWA_KV_HEADS = 4
HEAD_DIM = 64
SWA_WIDTH = SWA_HEADS * HEAD_DIM
SSM_WIDTH = 768
SSM_GROUP = 16
SSM_GROUPS = SSM_WIDTH // SSM_GROUP
SSM_STATE = 64
DIFF_HEADS = 4
DIFF_V_DIM = 128
DIFF_WIDTH = DIFF_HEADS * DIFF_V_DIM
IN_WIDTH = 3584
N_EXPERT_GROUPS = 4
EXPERTS_PER_GROUP = 8
N_EXPERTS = N_EXPERT_GROUPS * EXPERTS_PER_GROUP
D_EXPERT = 512

COL_KA = 768
COL_VA = 1024
COL_US = 1280
COL_QD = 2048
COL_KD = 2560
COL_VD = 3072

LANES = 128
SSM_T = 16
SSM_PACK = LANES // SSM_GROUP
SSM_PACKS = SSM_GROUPS // SSM_PACK
SSM_PSTATE = SSM_PACK * SSM_STATE
MOE_BM = 256
VMEM_LIMIT = 56 * 1024 * 1024

F32 = jnp.float32
BF16 = jnp.bfloat16


def _row_tile(rows):
    for t in (512, 256, 128):
        if rows % t == 0:
            return t
    raise ValueError(f"row count {rows} is not a multiple of 128")


def _params(sem, vmem=VMEM_LIMIT):
    return pltpu.CompilerParams(dimension_semantics=sem, vmem_limit_bytes=vmem)


def _rms_norm(x, g):
    return x * lax.rsqrt(jnp.mean(x * x, axis=-1, keepdims=True) + NORM_EPS) * g


IN_TN = 512
ROPE_TILES = (0, 1, 4, 5)


def _norm_inproj_kernel(x_ref, g_ref, w_ref, cos_ref, sin_ref, att_ref, us_ref, qd_ref, kd_ref, vd_ref, xn_ref):
    blocked = ((COL_VD, vd_ref), (COL_KD, kd_ref), (COL_QD, qd_ref), (COL_US, us_ref))

    def store(col, piece):
        for start, ref in blocked:
            if col >= start:
                ref[(col - start) // LANES] = piece
                return
        att_ref[:, col:col + LANES] = piece

    x = x_ref[...]
    ms = jnp.mean(x * x, axis=-1, keepdims=True)
    xn_ref[...] = (x * lax.rsqrt(ms + NORM_EPS) * g_ref[...]).astype(BF16)
    shape = (x.shape[0], IN_TN)
    lane = lax.broadcasted_iota(jnp.int32, shape, 1)
    first_half = (lane & (HEAD_DIM - 1)) < (HEAD_DIM // 2)
    reps = IN_TN // LANES
    cos = jnp.tile(cos_ref[...], (1, reps))
    sin = jnp.tile(sin_ref[...], (1, reps))
    for j in range(IN_WIDTH // IN_TN):
        cols = slice(j * IN_TN, (j + 1) * IN_TN)
        acc = jnp.dot(xn_ref[...], w_ref[:, cols], preferred_element_type=F32)
        if j in ROPE_TILES:
            partner = jnp.where(first_half, pltpu.roll(acc, IN_TN - HEAD_DIM // 2, 1),
                                pltpu.roll(acc, HEAD_DIM // 2, 1))
            acc = acc * cos + partner * sin
        acc = acc.astype(BF16)
        for c in range(IN_TN // LANES):
            store(j * IN_TN + c * LANES, acc[:, c * LANES:(c + 1) * LANES])


def _norm_inproj(h, g, w, cos, sin):
    rows = h.shape[0]
    tm = _row_tile(rows)

    def blocked(width):
        n = width // LANES
        return pl.BlockSpec((n, tm, LANES), lambda i: (0, i, 0)), jax.ShapeDtypeStruct((n, rows, LANES), BF16)

    specs, shapes = zip(
        (pl.BlockSpec((tm, COL_US), lambda i: (i, 0)), jax.ShapeDtypeStruct((rows, COL_US), BF16)),
        blocked(SSM_WIDTH), blocked(DIFF_WIDTH), blocked(DIFF_WIDTH), blocked(DIFF_WIDTH))
    return pl.pallas_call(
        _norm_inproj_kernel,
        grid=(rows // tm,),
        in_specs=[
            pl.BlockSpec((tm, D_MODEL), lambda i: (i, 0)),
            pl.BlockSpec((1, D_MODEL), lambda i: (0, 0)),
            pl.BlockSpec((D_MODEL, IN_WIDTH), lambda i: (0, 0)),
            pl.BlockSpec((tm, LANES), lambda i: (i, 0)),
            pl.BlockSpec((tm, LANES), lambda i: (i, 0)),
        ],
        out_specs=list(specs),
        out_shape=list(shapes),
        scratch_shapes=[pltpu.VMEM((tm, D_MODEL), BF16)],
        compiler_params=_params(("parallel",)),
        name="norm_inproj",
    )(h, g, w, cos, sin)


def _swa_kernel(sink_ref, q_ref, kp_ref, kc_ref, vp_ref, vc_ref, o_ref):
    n = pl.program_id(1)
    shape = (BLK, 2 * BLK)
    qi = lax.broadcasted_iota(jnp.int32, shape, 0)
    c = lax.broadcasted_iota(jnp.int32, shape, 1)
    kj = (n - 1) * BLK + c
    delta = qi + BLK - c
    ok = (delta >= 0) & (delta < BLK) & (kj >= PAD)
    grp = SWA_HEADS // SWA_KV_HEADS
    outs = []
    for j in range(SWA_KV_HEADS):
        cols = slice(j * HEAD_DIM, (j + 1) * HEAD_DIM)
        kk = jnp.concatenate([kp_ref[:, cols], kc_ref[:, cols]], axis=0)
        vv = jnp.concatenate([vp_ref[:, cols], vc_ref[:, cols]], axis=0)
        for g in range(grp):
            hd = j * grp + g
            qh = q_ref[:, hd * HEAD_DIM:(hd + 1) * HEAD_DIM]
            s = lax.dot_general(qh, kk, (((1,), (1,)), ((), ())), preferred_element_type=F32) * (HEAD_DIM ** -0.5)
            s = jnp.where(ok, s, NEG)
            sink = sink_ref[hd]
            m = jnp.maximum(jnp.max(s, axis=-1, keepdims=True), sink)
            p = jnp.exp(s - m)
            denom = jnp.sum(p, axis=-1, keepdims=True) + jnp.exp(sink - m)
            outs.append(jnp.dot(p.astype(BF16), vv, preferred_element_type=F32) / denom)
    o_ref[...] = jnp.concatenate(outs, axis=1).astype(o_ref.dtype)


def _swa(proj, sinks, batch, nb):
    rows = proj.shape[0]
    kvw = SWA_KV_HEADS * HEAD_DIM

    def cur(col):
        return lambda b, n: (b * nb + n, col)

    def prev(col):
        return lambda b, n: (b * nb + jnp.maximum(n - 1, 0), col)

    return pl.pallas_call(
        _swa_kernel,
        grid=(batch, nb),
        in_specs=[
            pl.BlockSpec(memory_space=pltpu.SMEM),
            pl.BlockSpec((BLK, SWA_WIDTH), cur(0)),
            pl.BlockSpec((BLK, kvw), prev(COL_KA // kvw)),
            pl.BlockSpec((BLK, kvw), cur(COL_KA // kvw)),
            pl.BlockSpec((BLK, kvw), prev(COL_VA // kvw)),
            pl.BlockSpec((BLK, kvw), cur(COL_VA // kvw)),
        ],
        out_specs=pl.BlockSpec((BLK, SWA_WIDTH), cur(0)),
        out_shape=jax.ShapeDtypeStruct((rows, SWA_WIDTH), BF16),
        compiler_params=_params(("parallel", "parallel")),
        name="swa",
    )(sinks, proj, proj, proj, proj, proj)


DIFF_QBLOCKS = 2


def _diff_kernel(lam_ref, g_ref, q_ref, k_ref, v_ref, o_ref, *, lam_init, nb):
    lane = lax.broadcasted_iota(jnp.int32, (BLK, LANES), 1)
    lv = lam_ref[...]
    lam = (jnp.exp(jnp.sum(lv[0:1] * lv[1:2], axis=-1, keepdims=True))
           - jnp.exp(jnp.sum(lv[2:3] * lv[3:4], axis=-1, keepdims=True)) + lam_init)
    gain = g_ref[...] * (1.0 - lam_init)
    scale = HEAD_DIM ** -0.5
    for n0 in range(0, nb, DIFF_QBLOCKS):
        blocks = range(n0, min(n0 + DIFF_QBLOCKS, nb))
        ke = (blocks[-1] + 1) * BLK
        kx = k_ref[0:ke, :]
        vx = v_ref[0:ke, :]
        rows = 2 * BLK * len(blocks)
        pieces = []
        for n in blocks:
            q = q_ref[n * BLK:(n + 1) * BLK, :] * scale
            zero = jnp.zeros_like(q)
            pieces += [jnp.where(lane < HEAD_DIM, q, zero), jnp.where(lane >= HEAD_DIM, q, zero)]
        s = lax.dot_general(jnp.concatenate(pieces, axis=0), kx, (((1,), (1,)), ((), ())),
                            preferred_element_type=F32)
        r = lax.broadcasted_iota(jnp.int32, (rows, BLK), 0)
        qpos = (n0 + jnp.right_shift(r, (2 * BLK).bit_length() - 1)) * BLK + (r & (BLK - 1))
        col = lax.broadcasted_iota(jnp.int32, (rows, BLK), 1)
        key_blocks = []
        for t in range(blocks[-1] + 1):
            blk = s[:, t * BLK:(t + 1) * BLK]
            ok = None
            if t >= n0:
                ok = t * BLK + col <= qpos
            if t == 0:
                ok = (col >= PAD) if ok is None else ok & (col >= PAD)
            key_blocks.append(blk if ok is None else jnp.where(ok, blk, NEG))
        s = jnp.concatenate(key_blocks, axis=1)
        m = jnp.max(s, axis=-1, keepdims=True)
        p = jnp.exp(s - m)
        denom = jnp.sum(p, axis=-1, keepdims=True)
        heads = jnp.dot(p.astype(BF16), vx, preferred_element_type=F32) * (1.0 / denom)
        for i, n in enumerate(blocks):
            o = heads[2 * i * BLK:(2 * i + 1) * BLK] - lam * heads[(2 * i + 1) * BLK:(2 * i + 2) * BLK]
            o = o * lax.rsqrt(jnp.mean(o * o, axis=-1, keepdims=True) + NORM_EPS) * gain
            o_ref[n * BLK:(n + 1) * BLK, :] = o.astype(o_ref.dtype)


def _diff_attn(qd, kd, vd, lam_vecs, subln_g, lam_init, batch, nb):
    rows = qd.shape[1]
    lp = nb * BLK
    head_rows = pl.BlockSpec((None, lp, LANES), lambda b, h: (h, b, 0))
    return pl.pallas_call(
        functools.partial(_diff_kernel, lam_init=lam_init, nb=nb),
        grid=(batch, DIFF_HEADS),
        in_specs=[
            pl.BlockSpec((4, HEAD_DIM), lambda b, h: (0, 0)),
            pl.BlockSpec((1, DIFF_V_DIM), lambda b, h: (0, 0)),
            head_rows, head_rows, head_rows,
        ],
        out_specs=head_rows,
        out_shape=jax.ShapeDtypeStruct((DIFF_HEADS, rows, DIFF_V_DIM), BF16),
        compiler_params=_params(("parallel", "parallel")),
        name="diff_attn",
    )(lam_vecs, subln_g, qd, kd, vd)


def _ssm_row_chunk(rows):
    for c in (544, 512, 384, 256, 128, 64, 32, 16):
        if rows % c == 0:
            return c
    raise ValueError(rows)


def _ssm_expand_operators(kc_ref, bc_ref, cc_ref, k_ref, bw_ref, cw_ref):
    def iota(shape, dim):
        return lax.broadcasted_iota(jnp.int32, shape, dim)

    def group(idx, width):
        return jnp.right_shift(idx, width.bit_length() - 1) & (SSM_PACK - 1)

    hs, ps, tl = SSM_GROUP, SSM_STATE, SSM_T * LANES
    spread_h = ((iota((LANES, hs), 0) & (hs - 1)) == iota((LANES, hs), 1)).astype(BF16)
    kt = jnp.dot(spread_h, kc_ref[...], preferred_element_type=F32)
    same = group(iota((LANES, tl), 0), hs) == group(iota((LANES, tl), 1), hs)
    k_ref[...] = jnp.where(same, kt, 0.0).astype(BF16)
    spread_s = (iota((ps, SSM_PSTATE), 0) == (iota((ps, SSM_PSTATE), 1) & (ps - 1))).astype(BF16)
    shape = (tl, SSM_PSTATE)
    same = group(iota(shape, 0), hs) == group(iota(shape, 1), ps)
    for part in range(2):
        bw = jnp.dot(bc_ref[part], spread_s, preferred_element_type=F32)
        bw_ref[:, part * SSM_PSTATE:(part + 1) * SSM_PSTATE] = jnp.where(same, bw, 0.0).astype(BF16)
    shape = (LANES, SSM_PSTATE)
    same = group(iota(shape, 0), hs) == group(iota(shape, 1), ps)
    for j in range(SSM_T):
        for part in range(2):
            ct = jnp.dot(spread_h, cc_ref[part, j], preferred_element_type=F32)
            cw_ref[j, :, part * SSM_PSTATE:(part + 1) * SSM_PSTATE] = jnp.where(same, ct, 0.0).astype(BF16)


def _ssm_kernel(u_ref, kc_ref, bc_ref, cc_ref, a_ref, d_ref, o_ref, up_ref, y_ref, e_ref, s_ref, k_ref, bw_ref,
                cw_ref, *, batch, lp):
    rows = batch * lp
    n_chunks = lp // SSM_T
    tot_chunks = rows // SSM_T
    _ssm_expand_operators(kc_ref, bc_ref, cc_ref, k_ref, bw_ref, cw_ref)
    contract_lanes = (((1,), (1,)), ((), ()))
    up_ref[0:SSM_T, :] = jnp.zeros((SSM_T, LANES), F32)
    up_ref[SSM_T:, :] = u_ref[...].astype(F32)
    for b in range(batch):
        up_ref[SSM_T + b * lp:SSM_T + b * lp + PAD, :] = jnp.zeros((PAD, LANES), F32)

    ch = _ssm_row_chunk(rows)
    rmod = lax.broadcasted_iota(jnp.int32, (ch, LANES), 0) & (SSM_T - 1)
    for r0 in range(0, rows, ch):
        pieces = []
        for k in range(SSM_T):
            sl = up_ref[SSM_T - k + r0:SSM_T - k + r0 + ch, :]
            pieces.append(jnp.where(rmod >= k, sl, 0.0).astype(BF16))
        y_ref[r0:r0 + ch, :] = lax.dot_general(jnp.concatenate(pieces, axis=1), k_ref[...], contract_lanes,
                                               preferred_element_type=F32)

    pieces = [up_ref[pl.ds(SSM_T + i, tot_chunks, stride=SSM_T), :].astype(BF16) for i in range(SSM_T)]
    e_ref[...] = jnp.dot(jnp.concatenate(pieces, axis=1), bw_ref[...], preferred_element_type=F32)

    a_re = a_ref[:, :SSM_PSTATE]
    a_im = a_ref[:, SSM_PSTATE:]

    def step(c, carry):
        new = []
        for b in range(batch):
            sr, si = carry[b]
            idx = b * n_chunks + c
            s_ref[pl.ds(idx, 1), :SSM_PSTATE] = sr
            s_ref[pl.ds(idx, 1), SSM_PSTATE:] = si
            er = e_ref[pl.ds(idx, 1), :SSM_PSTATE]
            ei = e_ref[pl.ds(idx, 1), SSM_PSTATE:]
            new.append((a_re * sr - a_im * si + er, a_re * si + a_im * sr + ei))
        return tuple(new)

    zero = jnp.zeros((1, SSM_PSTATE), F32)
    lax.fori_loop(0, n_chunks, step, tuple((zero, zero) for _ in range(batch)))

    sb = s_ref[...].astype(BF16)
    for j in range(SSM_T):
        z = lax.dot_general(sb, cw_ref[j], contract_lanes, preferred_element_type=F32)
        idx = pl.ds(j, tot_chunks, stride=SSM_T)
        y_ref[idx, :] = y_ref[idx, :] + z

    y = y_ref[...] + d_ref[...] * up_ref[SSM_T:, :]
    o_ref[...] = jax.nn.gelu(y, approximate=True).astype(o_ref.dtype)


def _ssm(us, kc, bc, cc, a_t, dskip, batch, lp):
    rows = us.shape[1]
    tot_chunks = rows // SSM_T
    return pl.pallas_call(
        functools.partial(_ssm_kernel, batch=batch, lp=lp),
        grid=(SSM_PACKS,),
        in_specs=[
            pl.BlockSpec((None, rows, LANES), lambda p: (p, 0, 0)),
            pl.BlockSpec((None, SSM_GROUP, SSM_T * LANES), lambda p: (p, 0, 0)),
            pl.BlockSpec((None, 2, SSM_T * LANES, SSM_STATE), lambda p: (p, 0, 0, 0)),
            pl.BlockSpec((None, 2, SSM_T, SSM_GROUP, SSM_PSTATE), lambda p: (p, 0, 0, 0, 0)),
            pl.BlockSpec((None, 1, 2 * SSM_PSTATE), lambda p: (p, 0, 0)),
            pl.BlockSpec((None, 1, LANES), lambda p: (p, 0, 0)),
        ],
        out_specs=pl.BlockSpec((None, rows, LANES), lambda p: (p, 0, 0)),
        out_shape=jax.ShapeDtypeStruct((SSM_PACKS, rows, LANES), F32),
        scratch_shapes=[
            pltpu.VMEM((rows + SSM_T, LANES), F32),
            pltpu.VMEM((rows, LANES), F32),
            pltpu.VMEM((tot_chunks, 2 * SSM_PSTATE), F32),
            pltpu.VMEM((tot_chunks, 2 * SSM_PSTATE), F32),
            pltpu.VMEM((LANES, SSM_T * LANES), BF16),
            pltpu.VMEM((SSM_T * LANES, 2 * SSM_PSTATE), BF16),
            pltpu.VMEM((SSM_T, LANES, 2 * SSM_PSTATE), BF16),
        ],
        compiler_params=_params(("parallel",)),
        name="ssm",
    )(us, kc, bc, cc, a_t, dskip)


def _ssm_operators(lam_re, lam_im, b_re, b_im, c_re, c_im, log_dt):
    p, h = SSM_STATE, SSM_GROUP
    dt = jnp.exp(log_dt.astype(F32))[:, None]
    lr, li = lam_re.astype(F32), lam_im.astype(F32)
    mag = jnp.exp(lr * dt)
    abar_re, abar_im = mag * jnp.cos(li * dt), mag * jnp.sin(li * dt)
    den = lr * lr + li * li
    nr = abar_re - 1.0
    f_re = (nr * lr + abar_im * li) / den
    f_im = (abar_im * lr - nr * li) / den
    br, bi = b_re.astype(F32), b_im.astype(F32)
    bbar_re = f_re[..., None] * br - f_im[..., None] * bi
    bbar_im = f_re[..., None] * bi + f_im[..., None] * br
    ks = jnp.arange(SSM_T + 1, dtype=F32)[:, None, None]
    pmag = jnp.exp(ks * (lr * dt)[None])
    pw_re = pmag * jnp.cos(ks * (li * dt)[None])
    pw_im = pmag * jnp.sin(ks * (li * dt)[None])
    cr, ci = c_re.astype(F32), c_im.astype(F32)

    packs, q = SSM_PACKS, SSM_PACK

    cb_re = cr[:, :, :, None] * bbar_re[:, None, :, :] - ci[:, :, :, None] * bbar_im[:, None, :, :]
    cb_im = cr[:, :, :, None] * bbar_im[:, None, :, :] + ci[:, :, :, None] * bbar_re[:, None, :, :]
    klag = (jnp.einsum('kgs,gosi->kgoi', pw_re[:SSM_T], cb_re)
            - jnp.einsum('kgs,gosi->kgoi', pw_im[:SSM_T], cb_im))
    kc = klag.reshape(SSM_T, packs, q, h, h).transpose(1, 3, 0, 2, 4).reshape(packs, h, SSM_T * LANES)

    rev_re, rev_im = pw_re[SSM_T - 1::-1, :, None, :], pw_im[SSM_T - 1::-1, :, None, :]
    bt_re, bt_im = jnp.swapaxes(bbar_re, 1, 2)[None], jnp.swapaxes(bbar_im, 1, 2)[None]

    def inj_by_pack(x):
        return x.reshape(SSM_T, packs, q * h, p).transpose(1, 0, 2, 3).reshape(packs, SSM_T * LANES, p)

    bc = jnp.stack([inj_by_pack(rev_re * bt_re - rev_im * bt_im), inj_by_pack(rev_re * bt_im + rev_im * bt_re)], axis=1)

    ca_re = cr[None] * pw_re[1:, :, None, :] - ci[None] * pw_im[1:, :, None, :]
    ca_im = cr[None] * pw_im[1:, :, None, :] + ci[None] * pw_re[1:, :, None, :]

    def out_by_pack(x):
        return x.reshape(SSM_T, packs, q, h, p).transpose(1, 0, 3, 2, 4).reshape(packs, SSM_T, h, SSM_PSTATE)

    cc = jnp.stack([out_by_pack(ca_re), out_by_pack(-ca_im)], axis=1)

    a_t = jnp.concatenate([pw_re[SSM_T].reshape(packs, 1, SSM_PSTATE),
                           pw_im[SSM_T].reshape(packs, 1, SSM_PSTATE)], axis=-1)
    return kc.astype(BF16), bc.astype(BF16), cc.astype(BF16), a_t


OUT_TN = 512


def _split_bf16(x):
    hi = x.astype(BF16)
    return hi, (x - hi.astype(F32)).astype(BF16)


def _route(hn, rwh_ref, rwl_ref, rb_ref):
    hi, lo = _split_bf16(hn)
    logits = (jnp.dot(hi, rwh_ref[...], preferred_element_type=F32)
              + (jnp.dot(lo, rwh_ref[...], preferred_element_type=F32)
                 + jnp.dot(hi, rwl_ref[...], preferred_element_type=F32))) + rb_ref[...]
    lane = lax.broadcasted_iota(jnp.int32, logits.shape, 1)

    def first_argmax(v, vmax):
        return jnp.min(jnp.where(v == vmax, lane, LANES), axis=-1, keepdims=True)

    gl = jnp.where(lane < N_EXPERT_GROUPS, logits, NEG)
    gmax = jnp.max(gl, axis=-1, keepdims=True)
    g_sel = first_argmax(gl, gmax)
    p_g = 1.0 / jnp.sum(jnp.exp(gl - gmax), axis=-1, keepdims=True)
    e_lane = lane - N_EXPERT_GROUPS
    in_group = (e_lane >= 0) & (e_lane < N_EXPERTS) & (jnp.right_shift(e_lane, 3) == g_sel)
    el = jnp.where(in_group, logits, NEG)
    m1 = jnp.max(el, axis=-1, keepdims=True)
    i1 = first_argmax(el, m1)
    el2 = jnp.where(lane == i1, NEG, el)
    m2 = jnp.max(el2, axis=-1, keepdims=True)
    i2 = first_argmax(el2, m2)
    z = jnp.sum(jnp.exp(el - m1), axis=-1, keepdims=True)
    p1 = 1.0 / z
    p2 = jnp.exp(m2 - m1) / z
    w1 = p1 / (p1 + p2) * p_g
    w2 = p2 / (p1 + p2) * p_g
    wts = jnp.where(lane == 0, w1, jnp.where(lane == 1, w2, 0.0))
    ids = jnp.where(lane == 0, i1 - N_EXPERT_GROUPS, jnp.where(lane == 1, i2 - N_EXPERT_GROUPS, 0))
    return wts, ids


def _assignment_ranks(ids):
    tm = ids.shape[0]
    lane = lax.broadcasted_iota(jnp.int32, ids.shape, 1)
    oh0 = lane == ids[:, 0:1]
    oh1 = lane == ids[:, 1:2]
    hits = (oh0 | oh1).astype(BF16)
    r = lax.broadcasted_iota(jnp.int32, (tm, tm), 0)
    c = lax.broadcasted_iota(jnp.int32, (tm, tm), 1)
    before = jnp.dot((c < r).astype(BF16), hits, preferred_element_type=F32)
    rank0 = jnp.sum(jnp.where(oh0, before, 0.0), axis=-1, keepdims=True)
    rank1 = jnp.sum(jnp.where(oh1, before, 0.0), axis=-1, keepdims=True)
    ranks = jnp.where(lane == 0, rank0, jnp.where(lane == 1, rank1, 0.0)).astype(jnp.int32)
    return ranks, jnp.sum(hits.astype(F32), axis=0, keepdims=True).astype(jnp.int32)


def _mixer_out_kernel(h_ref, a_ref, y_ref, c_ref, wo_ref, wglu_ref, g_ref, rwh_ref, rwl_ref, rb_ref,
                      o_ref, hn_ref, wt_ref, id_ref, rank_ref, cnt_ref):
    half = h_ref.shape[0] // 2
    halves = [slice(r * half, (r + 1) * half) for r in range(2)]
    for rows in halves:
        y = jnp.concatenate([y_ref[p, rows, :] for p in range(y_ref.shape[0])], axis=1)
        c = jnp.concatenate([c_ref[hd, rows, :] for hd in range(c_ref.shape[0])], axis=1)
        gate = jnp.dot(y.astype(BF16), wglu_ref[...], preferred_element_type=F32)
        b = (y * jax.nn.sigmoid(gate)).astype(BF16)
        mixed = jnp.concatenate([a_ref[rows, :], b, c], axis=1)
        for j in range(D_MODEL // OUT_TN):
            cols = slice(j * OUT_TN, (j + 1) * OUT_TN)
            o_ref[rows, cols] = h_ref[rows, cols] + jnp.dot(mixed, wo_ref[:, cols], preferred_element_type=F32)
    for rows in halves:
        hn = _rms_norm(o_ref[rows, :], g_ref[...])
        hn_ref[rows, :] = hn.astype(BF16)
        wts, ids = _route(hn, rwh_ref, rwl_ref, rb_ref)
        wt_ref[rows, :] = wts
        id_ref[rows, :] = ids
    ranks, counts = _assignment_ranks(id_ref[...])
    rank_ref[...] = ranks
    cnt_ref[...] = jnp.broadcast_to(counts, cnt_ref.shape)


def _mixer_out(h, oa, yg, oc, wo, wglu, g, rw_hi, rw_lo, rb):
    rows = h.shape[0]
    tm = _row_tile(rows)

    def row_spec(width):
        return pl.BlockSpec((tm, width), lambda i: (i, 0))

    def whole(shape):
        return pl.BlockSpec(shape, lambda i: (0, 0))

    def blocked(width):
        return pl.BlockSpec((width // LANES, tm, LANES), lambda i: (0, i, 0))

    return pl.pallas_call(
        _mixer_out_kernel,
        grid=(rows // tm,),
        in_specs=[
            row_spec(D_MODEL), row_spec(SWA_WIDTH), blocked(SSM_WIDTH), blocked(DIFF_WIDTH),
            whole((D_MODEL, D_MODEL)), whole((SSM_WIDTH, SSM_WIDTH)), whole((1, D_MODEL)),
            whole((D_MODEL, LANES)), whole((D_MODEL, LANES)), whole((1, LANES)),
        ],
        out_specs=[row_spec(D_MODEL), row_spec(D_MODEL), row_spec(LANES), row_spec(LANES), row_spec(LANES),
                   pl.BlockSpec((None, 8, LANES), lambda i: (i, 0, 0))],
        out_shape=[
            jax.ShapeDtypeStruct((rows, D_MODEL), F32),
            jax.ShapeDtypeStruct((rows, D_MODEL), BF16),
            jax.ShapeDtypeStruct((rows, LANES), F32),
            jax.ShapeDtypeStruct((rows, LANES), jnp.int32),
            jax.ShapeDtypeStruct((rows, LANES), jnp.int32),
            jax.ShapeDtypeStruct((rows // tm, 8, LANES), jnp.int32),
        ],
        compiler_params=_params(("parallel",)),
        name="mixer_out",
    )(h, oa, yg, oc, wo, wglu, g, rw_hi, rw_lo, rb)


MOE_UNIT = 16
MOE_DTYPE = BF16


def _moe_tile_rows(tt):
    return 2 * tt + N_EXPERTS * MOE_UNIT


def _moe_tables(tile_cnt, n_blk):
    cnt = tile_cnt[:, 0, :N_EXPERTS]
    seg = (cnt + MOE_UNIT - 1) // MOE_UNIT * MOE_UNIT
    size = jnp.sum(seg, axis=0)
    padded = (size + MOE_BM - 1) // MOE_BM * MOE_BM
    pend = jnp.cumsum(padded)
    pstart = pend - padded
    base = pstart[None, :] + jnp.cumsum(seg, axis=0) - seg
    loc = jnp.cumsum(seg, axis=1) - seg
    units = seg // MOE_UNIT
    blk_start = jnp.arange(n_blk, dtype=jnp.int32) * MOE_BM
    blk_expert = jnp.minimum(jnp.sum((pend[None, :] <= blk_start[:, None]).astype(jnp.int32), axis=1), N_EXPERTS - 1)
    ex = jnp.arange(N_EXPERTS, dtype=jnp.int32)
    nonempty = size > 0
    run_slot = (jnp.cumsum(nonempty.astype(jnp.int32)) - 1) % WEIGHT_SLOTS
    later = (ex[None, :] > ex[:, None]) & nonempty[None, :]
    nxt = jnp.min(jnp.where(later, ex[None, :], N_EXPERTS), axis=1)
    nxt = jnp.where(nxt == N_EXPERTS, -1, nxt)
    nxt2 = jnp.where(nxt >= 0, nxt[jnp.maximum(nxt, 0)], -1)
    i32 = jnp.int32
    loc_lanes = jnp.pad(loc, ((0, 0), (0, LANES - N_EXPERTS)))
    return dict(
        base=base.reshape(-1).astype(i32), loc=loc.reshape(-1).astype(i32), units=units.reshape(-1).astype(i32),
        tile_units=jnp.sum(units, axis=1).astype(i32),
        tail_start=(pstart + size).astype(i32), tail_units=((padded - size) // MOE_UNIT).astype(i32),
        loc_lanes=jnp.broadcast_to(loc_lanes[:, None, :], (loc.shape[0], 8, LANES)).astype(i32),
        blk_expert=blk_expert.astype(i32), blk_slot=run_slot[blk_expert].astype(i32),
        blk_next1=nxt[blk_expert].astype(i32), blk_next2=nxt2[blk_expert].astype(i32),
        n_used=(pend[-1] // MOE_BM).astype(i32).reshape(1))


def _segment_copies(tile, hbm, buf, sem, to_hbm, base_ref, loc_ref, units_ref):
    for e in range(N_EXPERTS):
        k = tile * N_EXPERTS + e
        row_hbm, row_buf = base_ref[k], loc_ref[k]

        def start(u, carry):
            a = hbm.at[pl.ds(pl.multiple_of(row_hbm + u * MOE_UNIT, MOE_UNIT), MOE_UNIT), :]
            b = buf.at[pl.ds(pl.multiple_of(row_buf + u * MOE_UNIT, MOE_UNIT), MOE_UNIT), :]
            (pltpu.make_async_copy(b, a, sem) if to_hbm else pltpu.make_async_copy(a, b, sem)).start()
            return carry

        lax.fori_loop(0, units_ref[k], start, 0)


def _wait_unit_copies(count, hbm, buf, sem):
    def wait(u, carry):
        pltpu.make_async_copy(hbm.at[pl.ds(0, MOE_UNIT), :], buf.at[pl.ds(0, MOE_UNIT), :], sem).wait()
        return carry

    lax.fori_loop(0, count, wait, 0)


def _local_rows(ids, ranks, loc_lanes):
    lane = lax.broadcasted_iota(jnp.int32, ids.shape, 1)
    loc = loc_lanes.astype(F32)
    start0 = jnp.sum(jnp.where(lane == ids[:, 0:1], loc, 0.0), axis=-1, keepdims=True).astype(jnp.int32)
    start1 = jnp.sum(jnp.where(lane == ids[:, 1:2], loc, 0.0), axis=-1, keepdims=True).astype(jnp.int32)
    return start0 + ranks[:, 0:1], start1 + ranks[:, 1:2]


def _dispatch_kernel(base_ref, loc_ref, units_ref, tile_units_ref, tail_start_ref, tail_units_ref, nused_ref,
                     hn_ref, ids_ref, rank_ref, locv_ref, x_hbm, lrow_ref, ybuf, zbuf, sem):
    j = pl.program_id(0)
    n = pl.num_programs(0)
    slot = j % 2
    tt, rt = hn_ref.shape[0], ybuf.shape[1]

    @pl.when(j >= 2)
    def _():
        _wait_unit_copies(tile_units_ref[j - 2], x_hbm, ybuf.at[slot], sem.at[slot])

    row0, row1 = _local_rows(ids_ref[...], rank_ref[...], locv_ref[0:1, :])
    lane = lax.broadcasted_iota(jnp.int32, lrow_ref.shape, 1)
    lrow_ref[...] = jnp.where(lane == 0, row0, jnp.where(lane == 1, row1, 0))
    col = lax.broadcasted_iota(jnp.int32, (tt, rt), 1)
    onehot = ((col == row0) | (col == row1)).astype(BF16)
    compacted = lax.dot_general(onehot, hn_ref[...], (((0,), (0,)), ((), ())), preferred_element_type=F32)
    ybuf[slot] = compacted.astype(ybuf.dtype)
    _segment_copies(j, x_hbm, ybuf.at[slot], sem.at[slot], True, base_ref, loc_ref, units_ref)

    @pl.when(j == n - 1)
    def _():
        zbuf[...] = jnp.zeros(zbuf.shape, zbuf.dtype)
        zunit = zbuf.at[pl.ds(0, MOE_UNIT), :]
        total = 0
        for e in range(N_EXPERTS):
            row = tail_start_ref[e]

            def start(u, carry):
                dst = x_hbm.at[pl.ds(pl.multiple_of(row + u * MOE_UNIT, MOE_UNIT), MOE_UNIT), :]
                pltpu.make_async_copy(zunit, dst, sem.at[2]).start()
                return carry

            lax.fori_loop(0, tail_units_ref[e], start, 0)
            total = total + tail_units_ref[e]
        _wait_unit_copies(total, x_hbm, zunit, sem.at[2])
        n_blk = x_hbm.shape[0] // MOE_BM

        def block_copy(blk):
            dst = x_hbm.at[pl.ds(pl.multiple_of(blk * MOE_BM, MOE_BM), MOE_BM), :]
            return pltpu.make_async_copy(zbuf, dst, sem.at[3])

        def start_block(blk, carry):
            block_copy(blk).start()
            return carry

        def wait_block(blk, carry):
            block_copy(blk).wait()
            return carry

        lax.fori_loop(nused_ref[0], n_blk, start_block, 0)
        lax.fori_loop(nused_ref[0], n_blk, wait_block, 0)

        @pl.when(j >= 1)
        def _():
            _wait_unit_copies(tile_units_ref[jnp.maximum(j - 1, 0)], x_hbm, ybuf.at[1 - slot], sem.at[1 - slot])

        _wait_unit_copies(tile_units_ref[j], x_hbm, ybuf.at[slot], sem.at[slot])


def _dispatch(hn, ids, ranks, tables, tt, n_rows):
    rows = hn.shape[0]
    rt = _moe_tile_rows(tt)
    row_spec = pl.BlockSpec((tt, LANES), lambda j, *_: (j, 0))
    grid_spec = pltpu.PrefetchScalarGridSpec(
        num_scalar_prefetch=7,
        grid=(rows // tt,),
        in_specs=[pl.BlockSpec((tt, D_MODEL), lambda j, *_: (j, 0)), row_spec, row_spec,
                  pl.BlockSpec((None, 8, LANES), lambda j, *_: (j, 0, 0))],
        out_specs=[pl.BlockSpec(memory_space=pl.ANY), row_spec],
        scratch_shapes=[pltpu.VMEM((2, rt, D_MODEL), MOE_DTYPE), pltpu.VMEM((MOE_BM, D_MODEL), MOE_DTYPE),
                        pltpu.SemaphoreType.DMA((4,))],
    )
    return pl.pallas_call(
        _dispatch_kernel,
        grid_spec=grid_spec,
        out_shape=[jax.ShapeDtypeStruct((n_rows, D_MODEL), MOE_DTYPE),
                   jax.ShapeDtypeStruct((rows, LANES), jnp.int32)],
        compiler_params=_params(("arbitrary",)),
        name="moe_dispatch",
    )(tables["base"], tables["loc"], tables["units"], tables["tile_units"], tables["tail_start"],
      tables["tail_units"], tables["n_used"], hn, ids, ranks, tables["loc_lanes"])


WEIGHT_DMA_PRIORITIES = (1, 0, 1)
WEIGHT_SLOTS = 3


def _expert_kernel(be_ref, slot_ref, nxt1_ref, nxt2_ref, nused_ref, x_ref, wg_hbm, wu_hbm, wd_hbm, o_ref,
                   wgf, wuf, wdf, wgb, wub, wdb, wsem, *, layer):
    i = pl.program_id(0)
    n_used = nused_ref[0]

    def weight_copies(expert, wslot):
        pairs = ((wg_hbm, wgf), (wu_hbm, wuf), (wd_hbm, wdf))
        return [pltpu.make_async_copy(w.at[layer, expert], buf.at[wslot], wsem.at[wslot]) for w, buf in pairs]

    def start_weights(expert, wslot):
        for c, priority in zip(weight_copies(expert, wslot), WEIGHT_DMA_PRIORITIES):
            c.start(priority=priority)

    @pl.when(i == 0)
    def _():
        start_weights(be_ref[0], 0)

        @pl.when(nxt1_ref[0] >= 0)
        def _():
            start_weights(nxt1_ref[0], 1)

    @pl.when(i < n_used)
    def _():
        @pl.when((i == 0) | (be_ref[i] != be_ref[jnp.maximum(i - 1, 0)]))
        def _():
            wslot = slot_ref[i]
            for c in weight_copies(be_ref[i], wslot):
                c.wait()

            @pl.when(nxt2_ref[i] >= 0)
            def _():
                start_weights(nxt2_ref[i], lax.rem(wslot + 2, WEIGHT_SLOTS))

            wgb[...] = wgf[wslot].astype(BF16)
            wub[...] = wuf[wslot].astype(BF16)
            wdb[...] = wdf[wslot].astype(BF16)

        x = x_ref[...]
        gate = jnp.dot(x, wgb[...], preferred_element_type=F32)
        up = jnp.dot(x, wub[...], preferred_element_type=F32)
        act = (jax.nn.silu(gate) * up).astype(BF16)
        o_ref[...] = jnp.dot(act, wdb[...], preferred_element_type=F32).astype(o_ref.dtype)

    @pl.when(i >= n_used)
    def _():
        o_ref[...] = jnp.zeros(o_ref.shape, o_ref.dtype)


def _experts(x_sorted, tables, wg, wu, wd, layer, n_blk):
    hbm = pl.BlockSpec(memory_space=pl.ANY)
    grid_spec = pltpu.PrefetchScalarGridSpec(
        num_scalar_prefetch=5,
        grid=(n_blk,),
        in_specs=[pl.BlockSpec((MOE_BM, D_MODEL), lambda i, be, slot, n1, n2, nu: (jnp.minimum(i, nu[0] - 1), 0)),
                  hbm, hbm, hbm],
        out_specs=pl.BlockSpec((MOE_BM, D_MODEL), lambda i, *_: (i, 0)),
        scratch_shapes=[
            pltpu.VMEM((WEIGHT_SLOTS, D_MODEL, D_EXPERT), F32),
            pltpu.VMEM((WEIGHT_SLOTS, D_MODEL, D_EXPERT), F32),
            pltpu.VMEM((WEIGHT_SLOTS, D_EXPERT, D_MODEL), F32),
            pltpu.VMEM((D_MODEL, D_EXPERT), BF16),
            pltpu.VMEM((D_MODEL, D_EXPERT), BF16),
            pltpu.VMEM((D_EXPERT, D_MODEL), BF16),
            pltpu.SemaphoreType.DMA((WEIGHT_SLOTS,)),
        ],
    )
    return pl.pallas_call(
        functools.partial(_expert_kernel, layer=layer),
        grid_spec=grid_spec,
        out_shape=jax.ShapeDtypeStruct((n_blk * MOE_BM, D_MODEL), MOE_DTYPE),
        compiler_params=_params(("arbitrary",)),
        name="moe_experts",
    )(tables["blk_expert"], tables["blk_slot"], tables["blk_next1"], tables["blk_next2"], tables["n_used"],
      x_sorted, wg, wu, wd)


def _combine_kernel(base_ref, loc_ref, units_ref, tile_units_ref, h_ref, wt_ref, lrow_ref, g_ref, y_hbm, o_ref,
                    ybuf, sem, *, final_norm):
    j = pl.program_id(0)
    n = pl.num_programs(0)
    slot = j % 2
    tt, rt = h_ref.shape[0], ybuf.shape[1]
    assert rt % LANES == 0

    @pl.when(j == 0)
    def _():
        ybuf[...] = jnp.zeros(ybuf.shape, ybuf.dtype)
        _segment_copies(0, y_hbm, ybuf.at[0], sem.at[0], False, base_ref, loc_ref, units_ref)

    @pl.when(j + 1 < n)
    def _():
        _segment_copies(j + 1, y_hbm, ybuf.at[1 - slot], sem.at[1 - slot], False, base_ref, loc_ref, units_ref)

    _wait_unit_copies(tile_units_ref[j], y_hbm, ybuf.at[slot], sem.at[slot])
    lrow = lrow_ref[...]
    wt = wt_ref[...]
    col = lax.broadcasted_iota(jnp.int32, (tt, rt), 1)
    is0 = col == lrow[:, 0:1]
    is1 = col == lrow[:, 1:2]
    w_of_row = jnp.sum(jnp.where(is0, wt[:, 0:1], 0.0) + jnp.where(is1, wt[:, 1:2], 0.0), axis=0, keepdims=True)
    w_rows = jnp.transpose(jnp.broadcast_to(w_of_row, (LANES, rt)))
    w_rows = jnp.tile(w_rows, (1, OUT_TN // LANES))
    pick = (is0 | is1).astype(BF16)
    for c in range(D_MODEL // OUT_TN):
        cols = slice(c * OUT_TN, (c + 1) * OUT_TN)
        y = (ybuf[slot, :, cols].astype(F32) * w_rows).astype(BF16)
        o_ref[:, cols] = h_ref[:, cols] + jnp.dot(pick, y, preferred_element_type=F32)
    if final_norm:
        o_ref[...] = _rms_norm(o_ref[...], g_ref[...])


def _combine(h, wts, lrows, y_sorted, tables, g, tt, final_norm):
    rows = h.shape[0]
    rt = _moe_tile_rows(tt)
    grid_spec = pltpu.PrefetchScalarGridSpec(
        num_scalar_prefetch=4,
        grid=(rows // tt,),
        in_specs=[
            pl.BlockSpec((tt, D_MODEL), lambda j, *_: (j, 0)),
            pl.BlockSpec((tt, LANES), lambda j, *_: (j, 0)),
            pl.BlockSpec((tt, LANES), lambda j, *_: (j, 0)),
            pl.BlockSpec((1, D_MODEL), lambda j, *_: (0, 0)),
            pl.BlockSpec(memory_space=pl.ANY),
        ],
        out_specs=pl.BlockSpec((tt, D_MODEL), lambda j, *_: (j, 0)),
        scratch_shapes=[pltpu.VMEM((2, rt, D_MODEL), MOE_DTYPE), pltpu.SemaphoreType.DMA((2,))],
    )
    return pl.pallas_call(
        functools.partial(_combine_kernel, final_norm=final_norm),
        grid_spec=grid_spec,
        out_shape=jax.ShapeDtypeStruct((rows, D_MODEL), F32),
        compiler_params=_params(("arbitrary",)),
        name="moe_combine",
    )(tables["base"], tables["loc"], tables["units"], tables["tile_units"], h, wts, lrows, g, y_sorted)


def _moe(h, hn, wts, ids, ranks, tile_cnt, wg, wu, wd, layer, final_g):
    rows = h.shape[0]
    n_tiles = tile_cnt.shape[0]
    tt = rows // n_tiles
    n_rows = 2 * rows + n_tiles * N_EXPERTS * (MOE_UNIT - 1) + N_EXPERTS * (MOE_BM - 1)
    n_blk = -(-n_rows // MOE_BM)
    tables = _moe_tables(tile_cnt, n_blk)
    x_sorted, lrows = _dispatch(hn, ids, ranks, tables, tt, n_blk * MOE_BM)
    y_sorted = _experts(x_sorted, tables, wg, wu, wd, layer, n_blk)
    final_norm = final_g is not None
    g = final_g if final_norm else jnp.ones((1, D_MODEL), F32)
    return _combine(h, wts, lrows, y_sorted, tables, g, tt, final_norm)


def _rope_tables(batch, lp):
    half = HEAD_DIM // 2
    inv = ROPE_THETA ** (-jnp.arange(half, dtype=F32) / half)
    pos = (jnp.arange(lp, dtype=jnp.int32) - PAD).astype(F32)
    ang = pos[:, None] * inv[None, :]
    cos = jnp.tile(jnp.cos(ang), (batch, LANES // half))
    sin = jnp.sin(ang)
    sin = jnp.tile(jnp.concatenate([-sin, sin], axis=1), (batch, LANES // HEAD_DIM))
    return cos, sin


def kernel(x, meta_tokens, attn_norm_g, ffn_norm_g, w_in, swa_sinks, ssm_lambda_re, ssm_lambda_im, ssm_b_re, ssm_b_im, ssm_c_re, ssm_c_im, ssm_d, ssm_log_dt, ssm_w_glu, diff_lambda_q1, diff_lambda_k1, diff_lambda_q2, diff_lambda_k2, diff_subln_g, w_out, router_group_w, router_group_b, router_expert_w, router_expert_b, moe_w_gate, moe_w_up, moe_w_down, final_norm_g):
    batch, seq, d = x.shape
    depth = w_in.shape[0]
    lp = PAD + N_META + seq
    nb = lp // BLK
    meta = jnp.broadcast_to(meta_tokens[None].astype(x.dtype), (batch, N_META, d))
    h = jnp.concatenate([jnp.zeros((batch, PAD, d), x.dtype), meta, x], axis=1).reshape(batch * lp, d)
    cos, sin = _rope_tables(batch, lp)
    for l in range(depth):
        att, us, qd, kd, vd = _norm_inproj(h, attn_norm_g[l][None], w_in[l].astype(BF16), cos, sin)
        o_a = _swa(att, swa_sinks[l].astype(F32), batch, nb)
        kc, bc, cc, a_t = _ssm_operators(ssm_lambda_re[l], ssm_lambda_im[l], ssm_b_re[l], ssm_b_im[l],
                                         ssm_c_re[l], ssm_c_im[l], ssm_log_dt[l])
        yg = _ssm(us, kc, bc, cc, a_t, ssm_d[l].astype(F32).reshape(SSM_PACKS, 1, LANES), batch, lp)
        lam_init = 0.8 - 0.6 * math.exp(-0.3 * l)
        lam_vecs = jnp.stack([diff_lambda_q1[l], diff_lambda_k1[l], diff_lambda_q2[l], diff_lambda_k2[l]]).astype(F32)
        o_c = _diff_attn(qd, kd, vd, lam_vecs, diff_subln_g[l][None].astype(F32), lam_init, batch, nb)
        wo = w_out[l].astype(BF16)
        rw = jnp.concatenate([router_group_w[l], router_expert_w[l]], axis=1).astype(F32)
        rw = jnp.pad(rw, ((0, 0), (0, LANES - rw.shape[1])))
        rb = jnp.concatenate([router_group_b[l], router_expert_b[l]]).astype(F32)
        rb = jnp.pad(rb, (0, LANES - rb.shape[0]))[None]
        rw_hi, rw_lo = _split_bf16(rw)
        g_ffn = ffn_norm_g[l][None].astype(F32)
        h, hn, wts, ids, ranks, tile_cnt = _mixer_out(h, o_a, yg, o_c, wo, ssm_w_glu[l].astype(BF16), g_ffn, rw_hi,
                                                      rw_lo, rb)
        final_g = final_norm_g[None].astype(F32) if l == depth - 1 else None
        h = _moe(h, hn, wts, ids, ranks, tile_cnt, moe_w_gate, moe_w_up, moe_w_down, l, final_g)
    return h.reshape(batch, lp, d)[:, PAD + N_META:]
```

```python
import functools
import math

import jax
import jax.numpy as jnp
from jax import lax
from jax.experimental import pallas as pl
from jax.experimental.pallas import tpu as pltpu

D_MODEL = 2048
N_META = 16
BLK = 128
PAD = BLK - N_META
ROPE_THETA = 10000.0
NORM_EPS = 1e-5
NEG = -1e30
SWA_HEADS = 12
SWA_KV_HEADS = 4
HEAD_DIM = 64
SWA_WIDTH = SWA_HEADS * HEAD_DIM
SSM_WIDTH = 768
SSM_GROUP = 16
SSM_GROUPS = SSM_WIDTH // SSM_GROUP
SSM_STATE = 64
DIFF_HEADS = 4
DIFF_V_DIM = 128
DIFF_WIDTH = DIFF_HEADS * DIFF_V_DIM
IN_WIDTH = 3584
N_EXPERT_GROUPS = 4
EXPERTS_PER_GROUP = 8
N_EXPERTS = N_EXPERT_GROUPS * EXPERTS_PER_GROUP
D_EXPERT = 512

COL_KA = 768
COL_VA = 1024
COL_US = 1280
COL_QD = 2048
COL_KD = 2560
COL_VD = 3072

LANES = 128
SSM_T = 16
SSM_PACK = LANES // SSM_GROUP
SSM_PACKS = SSM_GROUPS // SSM_PACK
SSM_PSTATE = SSM_PACK * SSM_STATE
MOE_BM = 512
VMEM_LIMIT = 56 * 1024 * 1024

F32 = jnp.float32
BF16 = jnp.bfloat16


def _row_tile(rows):
    for t in (512, 256, 128):
        if rows % t == 0:
            return t
    raise ValueError(f"row count {rows} is not a multiple of 128")


def _params(sem, vmem=VMEM_LIMIT):
    return pltpu.CompilerParams(dimension_semantics=sem, vmem_limit_bytes=vmem)


def _rms_norm(x, g):
    return x * lax.rsqrt(jnp.mean(x * x, axis=-1, keepdims=True) + NORM_EPS) * g


IN_TN = 512
ROPE_TILES = (0, 1, 4, 5)


def _norm_inproj_kernel(x_ref, g_ref, w_ref, cos_ref, sin_ref, att_ref, us_ref, qd_ref, kd_ref, vd_ref, xn_ref):
    blocked = ((COL_VD, vd_ref), (COL_KD, kd_ref), (COL_QD, qd_ref), (COL_US, us_ref))

    def store(col, piece):
        for start, ref in blocked:
            if col >= start:
                ref[(col - start) // LANES] = piece
                return
        att_ref[:, col:col + LANES] = piece

    x = x_ref[...]
    ms = jnp.mean(x * x, axis=-1, keepdims=True)
    xn_ref[...] = (x * lax.rsqrt(ms + NORM_EPS) * g_ref[...]).astype(BF16)
    shape = (x.shape[0], IN_TN)
    lane = lax.broadcasted_iota(jnp.int32, shape, 1)
    first_half = (lane & (HEAD_DIM - 1)) < (HEAD_DIM // 2)
    reps = IN_TN // LANES
    cos = jnp.tile(cos_ref[...], (1, reps))
    sin = jnp.tile(sin_ref[...], (1, reps))
    for j in range(IN_WIDTH // IN_TN):
        cols = slice(j * IN_TN, (j + 1) * IN_TN)
        acc = jnp.dot(xn_ref[...], w_ref[:, cols], preferred_element_type=F32)
        if j in ROPE_TILES:
            partner = jnp.where(first_half, pltpu.roll(acc, IN_TN - HEAD_DIM // 2, 1),
                                pltpu.roll(acc, HEAD_DIM // 2, 1))
            acc = acc * cos + partner * sin
        acc = acc.astype(BF16)
        for c in range(IN_TN // LANES):
            store(j * IN_TN + c * LANES, acc[:, c * LANES:(c + 1) * LANES])


def _norm_inproj(h, g, w, cos, sin):
    rows = h.shape[0]
    tm = _row_tile(rows)

    def blocked(width):
        n = width // LANES
        return pl.BlockSpec((n, tm, LANES), lambda i: (0, i, 0)), jax.ShapeDtypeStruct((n, rows, LANES), BF16)

    specs, shapes = zip(
        (pl.BlockSpec((tm, COL_US), lambda i: (i, 0)), jax.ShapeDtypeStruct((rows, COL_US), BF16)),
        blocked(SSM_WIDTH), blocked(DIFF_WIDTH), blocked(DIFF_WIDTH), blocked(DIFF_WIDTH))
    return pl.pallas_call(
        _norm_inproj_kernel,
        grid=(rows // tm,),
        in_specs=[
            pl.BlockSpec((tm, D_MODEL), lambda i: (i, 0)),
            pl.BlockSpec((1, D_MODEL), lambda i: (0, 0)),
            pl.BlockSpec((D_MODEL, IN_WIDTH), lambda i: (0, 0)),
            pl.BlockSpec((tm, LANES), lambda i: (i, 0)),
            pl.BlockSpec((tm, LANES), lambda i: (i, 0)),
        ],
        out_specs=list(specs),
        out_shape=list(shapes),
        scratch_shapes=[pltpu.VMEM((tm, D_MODEL), BF16)],
        compiler_params=_params(("parallel",)),
        name="norm_inproj",
    )(h, g, w, cos, sin)


def _swa_kernel(sink_ref, q_ref, kp_ref, kc_ref, vp_ref, vc_ref, o_ref):
    n = pl.program_id(1)
    shape = (BLK, 2 * BLK)
    qi = lax.broadcasted_iota(jnp.int32, shape, 0)
    c = lax.broadcasted_iota(jnp.int32, shape, 1)
    kj = (n - 1) * BLK + c
    delta = qi + BLK - c
    ok = (delta >= 0) & (delta < BLK) & (kj >= PAD)
    grp = SWA_HEADS // SWA_KV_HEADS
    outs = []
    for j in range(SWA_KV_HEADS):
        cols = slice(j * HEAD_DIM, (j + 1) * HEAD_DIM)
        kk = jnp.concatenate([kp_ref[:, cols], kc_ref[:, cols]], axis=0)
        vv = jnp.concatenate([vp_ref[:, cols], vc_ref[:, cols]], axis=0)
        for g in range(grp):
            hd = j * grp + g
            qh = q_ref[:, hd * HEAD_DIM:(hd + 1) * HEAD_DIM]
            s = lax.dot_general(qh, kk, (((1,), (1,)), ((), ())), preferred_element_type=F32) * (HEAD_DIM ** -0.5)
            s = jnp.where(ok, s, NEG)
            sink = sink_ref[hd]
            m = jnp.maximum(jnp.max(s, axis=-1, keepdims=True), sink)
            p = jnp.exp(s - m)
            denom = jnp.sum(p, axis=-1, keepdims=True) + jnp.exp(sink - m)
            outs.append(jnp.dot(p.astype(BF16), vv, preferred_element_type=F32) / denom)
    o_ref[...] = jnp.concatenate(outs, axis=1).astype(o_ref.dtype)


def _swa(proj, sinks, batch, nb):
    rows = proj.shape[0]
    kvw = SWA_KV_HEADS * HEAD_DIM

    def cur(col):
        return lambda b, n: (b * nb + n, col)

    def prev(col):
        return lambda b, n: (b * nb + jnp.maximum(n - 1, 0), col)

    return pl.pallas_call(
        _swa_kernel,
        grid=(batch, nb),
        in_specs=[
            pl.BlockSpec(memory_space=pltpu.SMEM),
            pl.BlockSpec((BLK, SWA_WIDTH), cur(0)),
            pl.BlockSpec((BLK, kvw), prev(COL_KA // kvw)),
            pl.BlockSpec((BLK, kvw), cur(COL_KA // kvw)),
            pl.BlockSpec((BLK, kvw), prev(COL_VA // kvw)),
            pl.BlockSpec((BLK, kvw), cur(COL_VA // kvw)),
        ],
        out_specs=pl.BlockSpec((BLK, SWA_WIDTH), cur(0)),
        out_shape=jax.ShapeDtypeStruct((rows, SWA_WIDTH), BF16),
        compiler_params=_params(("parallel", "parallel")),
        name="swa",
    )(sinks, proj, proj, proj, proj, proj)


DIFF_QBLOCKS = 2


def _diff_kernel(lam_ref, g_ref, q_ref, k_ref, v_ref, o_ref, *, lam_init, nb):
    lane = lax.broadcasted_iota(jnp.int32, (BLK, LANES), 1)
    lv = lam_ref[...]
    lam = (jnp.exp(jnp.sum(lv[0:1] * lv[1:2], axis=-1, keepdims=True))
           - jnp.exp(jnp.sum(lv[2:3] * lv[3:4], axis=-1, keepdims=True)) + lam_init)
    gain = g_ref[...] * (1.0 - lam_init)
    scale = HEAD_DIM ** -0.5
    for n0 in range(0, nb, DIFF_QBLOCKS):
        blocks = range(n0, min(n0 + DIFF_QBLOCKS, nb))
        ke = (blocks[-1] + 1) * BLK
        kx = k_ref[0:ke, :]
        vx = v_ref[0:ke, :]
        rows = 2 * BLK * len(blocks)
        pieces = []
        for n in blocks:
            q = q_ref[n * BLK:(n + 1) * BLK, :] * scale
            zero = jnp.zeros_like(q)
            pieces += [jnp.where(lane < HEAD_DIM, q, zero), jnp.where(lane >= HEAD_DIM, q, zero)]
        s = lax.dot_general(jnp.concatenate(pieces, axis=0), kx, (((1,), (1,)), ((), ())),
                            preferred_element_type=F32)
        r = lax.broadcasted_iota(jnp.int32, (rows, BLK), 0)
        qpos = (n0 + jnp.right_shift(r, (2 * BLK).bit_length() - 1)) * BLK + (r & (BLK - 1))
        col = lax.broadcasted_iota(jnp.int32, (rows, BLK), 1)
        key_blocks = []
        for t in range(blocks[-1] + 1):
            blk = s[:, t * BLK:(t + 1) * BLK]
            ok = None
            if t >= n0:
                ok = t * BLK + col <= qpos
            if t == 0:
                ok = (col >= PAD) if ok is None else ok & (col >= PAD)
            key_blocks.append(blk if ok is None else jnp.where(ok, blk, NEG))
        s = jnp.concatenate(key_blocks, axis=1)
        m = jnp.max(s, axis=-1, keepdims=True)
        p = jnp.exp(s - m)
        denom = jnp.sum(p, axis=-1, keepdims=True)
        heads = jnp.dot(p.astype(BF16), vx, preferred_element_type=F32) * (1.0 / denom)
        for i, n in enumerate(blocks):
            o = heads[2 * i * BLK:(2 * i + 1) * BLK] - lam * heads[(2 * i + 1) * BLK:(2 * i + 2) * BLK]
            o = o * lax.rsqrt(jnp.mean(o * o, axis=-1, keepdims=True) + NORM_EPS) * gain
            o_ref[n * BLK:(n + 1) * BLK, :] = o.astype(o_ref.dtype)


def _diff_attn(qd, kd, vd, lam_vecs, subln_g, lam_init, batch, nb):
    rows = qd.shape[1]
    lp = nb * BLK
    head_rows = pl.BlockSpec((None, lp, LANES), lambda b, h: (h, b, 0))
    return pl.pallas_call(
        functools.partial(_diff_kernel, lam_init=lam_init, nb=nb),
        grid=(batch, DIFF_HEADS),
        in_specs=[
            pl.BlockSpec((4, HEAD_DIM), lambda b, h: (0, 0)),
            pl.BlockSpec((1, DIFF_V_DIM), lambda b, h: (0, 0)),
            head_rows, head_rows, head_rows,
        ],
        out_specs=head_rows,
        out_shape=jax.ShapeDtypeStruct((DIFF_HEADS, rows, DIFF_V_DIM), BF16),
        compiler_params=_params(("parallel", "parallel")),
        name="diff_attn",
    )(lam_vecs, subln_g, qd, kd, vd)


def _ssm_row_chunk(rows):
    for c in (544, 512, 384, 256, 128, 64, 32, 16):
        if rows % c == 0:
            return c
    raise ValueError(rows)


def _ssm_expand_operators(kc_ref, bc_ref, cc_ref, k_ref, bw_ref, cw_ref):
    def iota(shape, dim):
        return lax.broadcasted_iota(jnp.int32, shape, dim)

    def group(idx, width):
        return jnp.right_shift(idx, width.bit_length() - 1) & (SSM_PACK - 1)

    hs, ps, tl = SSM_GROUP, SSM_STATE, SSM_T * LANES
    spread_h = ((iota((LANES, hs), 0) & (hs - 1)) == iota((LANES, hs), 1)).astype(BF16)
    kt = jnp.dot(spread_h, kc_ref[...], preferred_element_type=F32)
    same = group(iota((LANES, tl), 0), hs) == group(iota((LANES, tl), 1), hs)
    k_ref[...] = jnp.where(same, kt, 0.0).astype(BF16)
    spread_s = (iota((ps, SSM_PSTATE), 0) == (iota((ps, SSM_PSTATE), 1) & (ps - 1))).astype(BF16)
    shape = (tl, SSM_PSTATE)
    same = group(iota(shape, 0), hs) == group(iota(shape, 1), ps)
    for part in range(2):
        bw = jnp.dot(bc_ref[part], spread_s, preferred_element_type=F32)
        bw_ref[:, part * SSM_PSTATE:(part + 1) * SSM_PSTATE] = jnp.where(same, bw, 0.0).astype(BF16)
    shape = (LANES, SSM_PSTATE)
    same = group(iota(shape, 0), hs) == group(iota(shape, 1), ps)
    for j in range(SSM_T):
        for part in range(2):
            ct = jnp.dot(spread_h, cc_ref[part, j], preferred_element_type=F32)
            cw_ref[j, :, part * SSM_PSTATE:(part + 1) * SSM_PSTATE] = jnp.where(same, ct, 0.0).astype(BF16)


def _ssm_kernel(u_ref, kc_ref, bc_ref, cc_ref, a_ref, d_ref, o_ref, up_ref, y_ref, e_ref, s_ref, k_ref, bw_ref,
                cw_ref, *, batch, lp):
    rows = batch * lp
    n_chunks = lp // SSM_T
    tot_chunks = rows // SSM_T
    _ssm_expand_operators(kc_ref, bc_ref, cc_ref, k_ref, bw_ref, cw_ref)
    contract_lanes = (((1,), (1,)), ((), ()))
    up_ref[0:SSM_T, :] = jnp.zeros((SSM_T, LANES), F32)
    up_ref[SSM_T:, :] = u_ref[...].astype(F32)
    for b in range(batch):
        up_ref[SSM_T + b * lp:SSM_T + b * lp + PAD, :] = jnp.zeros((PAD, LANES), F32)

    ch = _ssm_row_chunk(rows)
    rmod = lax.broadcasted_iota(jnp.int32, (ch, LANES), 0) & (SSM_T - 1)
    for r0 in range(0, rows, ch):
        pieces = []
        for k in range(SSM_T):
            sl = up_ref[SSM_T - k + r0:SSM_T - k + r0 + ch, :]
            pieces.append(jnp.where(rmod >= k, sl, 0.0).astype(BF16))
        y_ref[r0:r0 + ch, :] = lax.dot_general(jnp.concatenate(pieces, axis=1), k_ref[...], contract_lanes,
                                               preferred_element_type=F32)

    pieces = [up_ref[pl.ds(SSM_T + i, tot_chunks, stride=SSM_T), :].astype(BF16) for i in range(SSM_T)]
    e_ref[...] = jnp.dot(jnp.concatenate(pieces, axis=1), bw_ref[...], preferred_element_type=F32)

    a_re = a_ref[:, :SSM_PSTATE]
    a_im = a_ref[:, SSM_PSTATE:]

    def step(c, carry):
        new = []
        for b in range(batch):
            sr, si = carry[b]
            idx = b * n_chunks + c
            s_ref[pl.ds(idx, 1), :SSM_PSTATE] = sr
            s_ref[pl.ds(idx, 1), SSM_PSTATE:] = si
            er = e_ref[pl.ds(idx, 1), :SSM_PSTATE]
            ei = e_ref[pl.ds(idx, 1), SSM_PSTATE:]
            new.append((a_re * sr - a_im * si + er, a_re * si + a_im * sr + ei))
        return tuple(new)

    zero = jnp.zeros((1, SSM_PSTATE), F32)
    lax.fori_loop(0, n_chunks, step, tuple((zero, zero) for _ in range(batch)))

    sb = s_ref[...].astype(BF16)
    for j in range(SSM_T):
        z = lax.dot_general(sb, cw_ref[j], contract_lanes, preferred_element_type=F32)
        idx = pl.ds(j, tot_chunks, stride=SSM_T)
        y_ref[idx, :] = y_ref[idx, :] + z

    y = y_ref[...] + d_ref[...] * up_ref[SSM_T:, :]
    o_ref[...] = jax.nn.gelu(y, approximate=True).astype(o_ref.dtype)


def _ssm(us, kc, bc, cc, a_t, dskip, batch, lp):
    rows = us.shape[1]
    tot_chunks = rows // SSM_T
    return pl.pallas_call(
        functools.partial(_ssm_kernel, batch=batch, lp=lp),
        grid=(SSM_PACKS,),
        in_specs=[
            pl.BlockSpec((None, rows, LANES), lambda p: (p, 0, 0)),
            pl.BlockSpec((None, SSM_GROUP, SSM_T * LANES), lambda p: (p, 0, 0)),
            pl.BlockSpec((None, 2, SSM_T * LANES, SSM_STATE), lambda p: (p, 0, 0, 0)),
            pl.BlockSpec((None, 2, SSM_T, SSM_GROUP, SSM_PSTATE), lambda p: (p, 0, 0, 0, 0)),
            pl.BlockSpec((None, 1, 2 * SSM_PSTATE), lambda p: (p, 0, 0)),
            pl.BlockSpec((None, 1, LANES), lambda p: (p, 0, 0)),
        ],
        out_specs=pl.BlockSpec((None, rows, LANES), lambda p: (p, 0, 0)),
        out_shape=jax.ShapeDtypeStruct((SSM_PACKS, rows, LANES), F32),
        scratch_shapes=[
            pltpu.VMEM((rows + SSM_T, LANES), F32),
            pltpu.VMEM((rows, LANES), F32),
            pltpu.VMEM((tot_chunks, 2 * SSM_PSTATE), F32),
            pltpu.VMEM((tot_chunks, 2 * SSM_PSTATE), F32),
            pltpu.VMEM((LANES, SSM_T * LANES), BF16),
            pltpu.VMEM((SSM_T * LANES, 2 * SSM_PSTATE), BF16),
            pltpu.VMEM((SSM_T, LANES, 2 * SSM_PSTATE), BF16),
        ],
        compiler_params=_params(("parallel",)),
        name="ssm",
    )(us, kc, bc, cc, a_t, dskip)


def _ssm_operators(lam_re, lam_im, b_re, b_im, c_re, c_im, log_dt):
    p, h = SSM_STATE, SSM_GROUP
    dt = jnp.exp(log_dt.astype(F32))[:, None]
    lr, li = lam_re.astype(F32), lam_im.astype(F32)
    mag = jnp.exp(lr * dt)
    abar_re, abar_im = mag * jnp.cos(li * dt), mag * jnp.sin(li * dt)
    den = lr * lr + li * li
    nr = abar_re - 1.0
    f_re = (nr * lr + abar_im * li) / den
    f_im = (abar_im * lr - nr * li) / den
    br, bi = b_re.astype(F32), b_im.astype(F32)
    bbar_re = f_re[..., None] * br - f_im[..., None] * bi
    bbar_im = f_re[..., None] * bi + f_im[..., None] * br
    ks = jnp.arange(SSM_T + 1, dtype=F32)[:, None, None]
    pmag = jnp.exp(ks * (lr * dt)[None])
    pw_re = pmag * jnp.cos(ks * (li * dt)[None])
    pw_im = pmag * jnp.sin(ks * (li * dt)[None])
    cr, ci = c_re.astype(F32), c_im.astype(F32)

    packs, q = SSM_PACKS, SSM_PACK

    cb_re = cr[:, :, :, None] * bbar_re[:, None, :, :] - ci[:, :, :, None] * bbar_im[:, None, :, :]
    cb_im = cr[:, :, :, None] * bbar_im[:, None, :, :] + ci[:, :, :, None] * bbar_re[:, None, :, :]
    klag = (jnp.einsum('kgs,gosi->kgoi', pw_re[:SSM_T], cb_re)
            - jnp.einsum('kgs,gosi->kgoi', pw_im[:SSM_T], cb_im))
    kc = klag.reshape(SSM_T, packs, q, h, h).transpose(1, 3, 0, 2, 4).reshape(packs, h, SSM_T * LANES)

    rev_re, rev_im = pw_re[SSM_T - 1::-1, :, None, :], pw_im[SSM_T - 1::-1, :, None, :]
    bt_re, bt_im = jnp.swapaxes(bbar_re, 1, 2)[None], jnp.swapaxes(bbar_im, 1, 2)[None]

    def inj_by_pack(x):
        return x.reshape(SSM_T, packs, q * h, p).transpose(1, 0, 2, 3).reshape(packs, SSM_T * LANES, p)

    bc = jnp.stack([inj_by_pack(rev_re * bt_re - rev_im * bt_im), inj_by_pack(rev_re * bt_im + rev_im * bt_re)], axis=1)

    ca_re = cr[None] * pw_re[1:, :, None, :] - ci[None] * pw_im[1:, :, None, :]
    ca_im = cr[None] * pw_im[1:, :, None, :] + ci[None] * pw_re[1:, :, None, :]

    def out_by_pack(x):
        return x.reshape(SSM_T, packs, q, h, p).transpose(1, 0, 3, 2, 4).reshape(packs, SSM_T, h, SSM_PSTATE)

    cc = jnp.stack([out_by_pack(ca_re), out_by_pack(-ca_im)], axis=1)

    a_t = jnp.concatenate([pw_re[SSM_T].reshape(packs, 1, SSM_PSTATE),
                           pw_im[SSM_T].reshape(packs, 1, SSM_PSTATE)], axis=-1)
    return kc.astype(BF16), bc.astype(BF16), cc.astype(BF16), a_t


OUT_TN = 512


def _split_bf16(x):
    hi = x.astype(BF16)
    return hi, (x - hi.astype(F32)).astype(BF16)


def _route(hn, rwh_ref, rwl_ref, rb_ref):
    hi, lo = _split_bf16(hn)
    logits = (jnp.dot(hi, rwh_ref[...], preferred_element_type=F32)
              + (jnp.dot(lo, rwh_ref[...], preferred_element_type=F32)
                 + jnp.dot(hi, rwl_ref[...], preferred_element_type=F32))) + rb_ref[...]
    lane = lax.broadcasted_iota(jnp.int32, logits.shape, 1)

    def first_argmax(v, vmax):
        return jnp.min(jnp.where(v == vmax, lane, LANES), axis=-1, keepdims=True)

    gl = jnp.where(lane < N_EXPERT_GROUPS, logits, NEG)
    gmax = jnp.max(gl, axis=-1, keepdims=True)
    g_sel = first_argmax(gl, gmax)
    p_g = 1.0 / jnp.sum(jnp.exp(gl - gmax), axis=-1, keepdims=True)
    e_lane = lane - N_EXPERT_GROUPS
    in_group = (e_lane >= 0) & (e_lane < N_EXPERTS) & (jnp.right_shift(e_lane, 3) == g_sel)
    el = jnp.where(in_group, logits, NEG)
    m1 = jnp.max(el, axis=-1, keepdims=True)
    i1 = first_argmax(el, m1)
    el2 = jnp.where(lane == i1, NEG, el)
    m2 = jnp.max(el2, axis=-1, keepdims=True)
    i2 = first_argmax(el2, m2)
    z = jnp.sum(jnp.exp(el - m1), axis=-1, keepdims=True)
    p1 = 1.0 / z
    p2 = jnp.exp(m2 - m1) / z
    w1 = p1 / (p1 + p2) * p_g
    w2 = p2 / (p1 + p2) * p_g
    wts = jnp.where(lane == 0, w1, jnp.where(lane == 1, w2, 0.0))
    ids = jnp.where(lane == 0, i1 - N_EXPERT_GROUPS, jnp.where(lane == 1, i2 - N_EXPERT_GROUPS, 0))
    return wts, ids


def _assignment_ranks(ids):
    tm = ids.shape[0]
    lane = lax.broadcasted_iota(jnp.int32, ids.shape, 1)
    oh0 = lane == ids[:, 0:1]
    oh1 = lane == ids[:, 1:2]
    hits = (oh0 | oh1).astype(BF16)
    r = lax.broadcasted_iota(jnp.int32, (tm, tm), 0)
    c = lax.broadcasted_iota(jnp.int32, (tm, tm), 1)
    before = jnp.dot((c < r).astype(BF16), hits, preferred_element_type=F32)
    rank0 = jnp.sum(jnp.where(oh0, before, 0.0), axis=-1, keepdims=True)
    rank1 = jnp.sum(jnp.where(oh1, before, 0.0), axis=-1, keepdims=True)
    ranks = jnp.where(lane == 0, rank0, jnp.where(lane == 1, rank1, 0.0)).astype(jnp.int32)
    return ranks, jnp.sum(hits.astype(F32), axis=0, keepdims=True).astype(jnp.int32)


def _mixer_out_kernel(h_ref, a_ref, y_ref, c_ref, wo_ref, wglu_ref, g_ref, rwh_ref, rwl_ref, rb_ref,
                      o_ref, hn_ref, wt_ref, id_ref, rank_ref, cnt_ref):
    half = h_ref.shape[0] // 2
    halves = [slice(r * half, (r + 1) * half) for r in range(2)]
    for rows in halves:
        y = jnp.concatenate([y_ref[p, rows, :] for p in range(y_ref.shape[0])], axis=1)
        c = jnp.concatenate([c_ref[hd, rows, :] for hd in range(c_ref.shape[0])], axis=1)
        gate = jnp.dot(y.astype(BF16), wglu_ref[...], preferred_element_type=F32)
        b = (y * jax.nn.sigmoid(gate)).astype(BF16)
        mixed = jnp.concatenate([a_ref[rows, :], b, c], axis=1)
        for j in range(D_MODEL // OUT_TN):
            cols = slice(j * OUT_TN, (j + 1) * OUT_TN)
            o_ref[rows, cols] = h_ref[rows, cols] + jnp.dot(mixed, wo_ref[:, cols], preferred_element_type=F32)
    for rows in halves:
        hn = _rms_norm(o_ref[rows, :], g_ref[...])
        hn_ref[rows, :] = hn.astype(BF16)
        wts, ids = _route(hn, rwh_ref, rwl_ref, rb_ref)
        wt_ref[rows, :] = wts
        id_ref[rows, :] = ids
    ranks, counts = _assignment_ranks(id_ref[...])
    rank_ref[...] = ranks
    cnt_ref[...] = jnp.broadcast_to(counts, cnt_ref.shape)


def _mixer_out(h, oa, yg, oc, wo, wglu, g, rw_hi, rw_lo, rb):
    rows = h.shape[0]
    tm = _row_tile(rows)

    def row_spec(width):
        return pl.BlockSpec((tm, width), lambda i: (i, 0))

    def whole(shape):
        return pl.BlockSpec(shape, lambda i: (0, 0))

    def blocked(width):
        return pl.BlockSpec((width // LANES, tm, LANES), lambda i: (0, i, 0))

    return pl.pallas_call(
        _mixer_out_kernel,
        grid=(rows // tm,),
        in_specs=[
            row_spec(D_MODEL), row_spec(SWA_WIDTH), blocked(SSM_WIDTH), blocked(DIFF_WIDTH),
            whole((D_MODEL, D_MODEL)), whole((SSM_WIDTH, SSM_WIDTH)), whole((1, D_MODEL)),
            whole((D_MODEL, LANES)), whole((D_MODEL, LANES)), whole((1, LANES)),
        ],
        out_specs=[row_spec(D_MODEL), row_spec(D_MODEL), row_spec(LANES), row_spec(LANES), row_spec(LANES),
                   pl.BlockSpec((None, 8, LANES), lambda i: (i, 0, 0))],
        out_shape=[
            jax.ShapeDtypeStruct((rows, D_MODEL), F32),
            jax.ShapeDtypeStruct((rows, D_MODEL), BF16),
            jax.ShapeDtypeStruct((rows, LANES), F32),
            jax.ShapeDtypeStruct((rows, LANES), jnp.int32),
            jax.ShapeDtypeStruct((rows, LANES), jnp.int32),
            jax.ShapeDtypeStruct((rows // tm, 8, LANES), jnp.int32),
        ],
        compiler_params=_params(("parallel",)),
        name="mixer_out",
    )(h, oa, yg, oc, wo, wglu, g, rw_hi, rw_lo, rb)


MOE_UNIT = 16
MOE_DTYPE = BF16


def _moe_tile_rows(tt):
    return 2 * tt + N_EXPERTS * MOE_UNIT


def _moe_tables(tile_cnt, n_blk):
    cnt = tile_cnt[:, 0, :N_EXPERTS]
    seg = (cnt + MOE_UNIT - 1) // MOE_UNIT * MOE_UNIT
    size = jnp.sum(seg, axis=0)
    padded = (size + MOE_BM - 1) // MOE_BM * MOE_BM
    pend = jnp.cumsum(padded)
    pstart = pend - padded
    base = pstart[None, :] + jnp.cumsum(seg, axis=0) - seg
    loc = jnp.cumsum(seg, axis=1) - seg
    units = seg // MOE_UNIT
    blk_start = jnp.arange(n_blk, dtype=jnp.int32) * MOE_BM
    blk_expert = jnp.minimum(jnp.sum((pend[None, :] <= blk_start[:, None]).astype(jnp.int32), axis=1), N_EXPERTS - 1)
    ex = jnp.arange(N_EXPERTS, dtype=jnp.int32)
    nonempty = size > 0
    run_slot = (jnp.cumsum(nonempty.astype(jnp.int32)) - 1) % WEIGHT_SLOTS
    later = (ex[None, :] > ex[:, None]) & nonempty[None, :]
    nxt = jnp.min(jnp.where(later, ex[None, :], N_EXPERTS), axis=1)
    nxt = jnp.where(nxt == N_EXPERTS, -1, nxt)
    nxt2 = jnp.where(nxt >= 0, nxt[jnp.maximum(nxt, 0)], -1)
    i32 = jnp.int32
    loc_lanes = jnp.pad(loc, ((0, 0), (0, LANES - N_EXPERTS)))
    return dict(
        base=base.reshape(-1).astype(i32), loc=loc.reshape(-1).astype(i32), units=units.reshape(-1).astype(i32),
        tile_units=jnp.sum(units, axis=1).astype(i32),
        tail_start=(pstart + size).astype(i32), tail_units=((padded - size) // MOE_UNIT).astype(i32),
        loc_lanes=jnp.broadcast_to(loc_lanes[:, None, :], (loc.shape[0], 8, LANES)).astype(i32),
        blk_expert=blk_expert.astype(i32), blk_slot=run_slot[blk_expert].astype(i32),
        blk_next1=nxt[blk_expert].astype(i32), blk_next2=nxt2[blk_expert].astype(i32),
        n_used=(pend[-1] // MOE_BM).astype(i32).reshape(1))


def _segment_copies(tile, hbm, buf, sem, to_hbm, base_ref, loc_ref, units_ref):
    for e in range(N_EXPERTS):
        k = tile * N_EXPERTS + e
        row_hbm, row_buf = base_ref[k], loc_ref[k]

        def start(u, carry):
            a = hbm.at[pl.ds(pl.multiple_of(row_hbm + u * MOE_UNIT, MOE_UNIT), MOE_UNIT), :]
            b = buf.at[pl.ds(pl.multiple_of(row_buf + u * MOE_UNIT, MOE_UNIT), MOE_UNIT), :]
            (pltpu.make_async_copy(b, a, sem) if to_hbm else pltpu.make_async_copy(a, b, sem)).start()
            return carry

        lax.fori_loop(0, units_ref[k], start, 0)


def _wait_unit_copies(count, hbm, buf, sem):
    def wait(u, carry):
        pltpu.make_async_copy(hbm.at[pl.ds(0, MOE_UNIT), :], buf.at[pl.ds(0, MOE_UNIT), :], sem).wait()
        return carry

    lax.fori_loop(0, count, wait, 0)


def _local_rows(ids, ranks, loc_lanes):
    lane = lax.broadcasted_iota(jnp.int32, ids.shape, 1)
    loc = loc_lanes.astype(F32)
    start0 = jnp.sum(jnp.where(lane == ids[:, 0:1], loc, 0.0), axis=-1, keepdims=True).astype(jnp.int32)
    start1 = jnp.sum(jnp.where(lane == ids[:, 1:2], loc, 0.0), axis=-1, keepdims=True).astype(jnp.int32)
    return start0 + ranks[:, 0:1], start1 + ranks[:, 1:2]


def _dispatch_kernel(base_ref, loc_ref, units_ref, tile_units_ref, tail_start_ref, tail_units_ref, nused_ref,
                     hn_ref, ids_ref, rank_ref, locv_ref, x_hbm, lrow_ref, ybuf, zbuf, sem):
    j = pl.program_id(0)
    n = pl.num_programs(0)
    slot = j % 2
    tt, rt = hn_ref.shape[0], ybuf.shape[1]

    @pl.when(j >= 2)
    def _():
        _wait_unit_copies(tile_units_ref[j - 2], x_hbm, ybuf.at[slot], sem.at[slot])

    row0, row1 = _local_rows(ids_ref[...], rank_ref[...], locv_ref[0:1, :])
    lane = lax.broadcasted_iota(jnp.int32, lrow_ref.shape, 1)
    lrow_ref[...] = jnp.where(lane == 0, row0, jnp.where(lane == 1, row1, 0))
    col = lax.broadcasted_iota(jnp.int32, (tt, rt), 1)
    onehot = ((col == row0) | (col == row1)).astype(BF16)
    compacted = lax.dot_general(onehot, hn_ref[...], (((0,), (0,)), ((), ())), preferred_element_type=F32)
    ybuf[slot] = compacted.astype(ybuf.dtype)
    _segment_copies(j, x_hbm, ybuf.at[slot], sem.at[slot], True, base_ref, loc_ref, units_ref)

    @pl.when(j == n - 1)
    def _():
        zbuf[...] = jnp.zeros(zbuf.shape, zbuf.dtype)
        zunit = zbuf.at[pl.ds(0, MOE_UNIT), :]
        total = 0
        for e in range(N_EXPERTS):
            row = tail_start_ref[e]

            def start(u, carry):
                dst = x_hbm.at[pl.ds(pl.multiple_of(row + u * MOE_UNIT, MOE_UNIT), MOE_UNIT), :]
                pltpu.make_async_copy(zunit, dst, sem.at[2]).start()
                return carry

            lax.fori_loop(0, tail_units_ref[e], start, 0)
            total = total + tail_units_ref[e]
        _wait_unit_copies(total, x_hbm, zunit, sem.at[2])
        n_blk = x_hbm.shape[0] // MOE_BM

        def block_copy(blk):
            dst = x_hbm.at[pl.ds(pl.multiple_of(blk * MOE_BM, MOE_BM), MOE_BM), :]
            return pltpu.make_async_copy(zbuf, dst, sem.at[3])

        def start_block(blk, carry):
            block_copy(blk).start()
            return carry

        def wait_block(blk, carry):
            block_copy(blk).wait()
            return carry

        lax.fori_loop(nused_ref[0], n_blk, start_block, 0)
        lax.fori_loop(nused_ref[0], n_blk, wait_block, 0)

        @pl.when(j >= 1)
        def _():
            _wait_unit_copies(tile_units_ref[jnp.maximum(j - 1, 0)], x_hbm, ybuf.at[1 - slot], sem.at[1 - slot])

        _wait_unit_copies(tile_units_ref[j], x_hbm, ybuf.at[slot], sem.at[slot])


def _dispatch(hn, ids, ranks, tables, tt, n_rows):
    rows = hn.shape[0]
    rt = _moe_tile_rows(tt)
    row_spec = pl.BlockSpec((tt, LANES), lambda j, *_: (j, 0))
    grid_spec = pltpu.PrefetchScalarGridSpec(
        num_scalar_prefetch=7,
        grid=(rows // tt,),
        in_specs=[pl.BlockSpec((tt, D_MODEL), lambda j, *_: (j, 0)), row_spec, row_spec,
                  pl.BlockSpec((None, 8, LANES), lambda j, *_: (j, 0, 0))],
        out_specs=[pl.BlockSpec(memory_space=pl.ANY), row_spec],
        scratch_shapes=[pltpu.VMEM((2, rt, D_MODEL), MOE_DTYPE), pltpu.VMEM((MOE_BM, D_MODEL), MOE_DTYPE),
                        pltpu.SemaphoreType.DMA((4,))],
    )
    return pl.pallas_call(
        _dispatch_kernel,
        grid_spec=grid_spec,
        out_shape=[jax.ShapeDtypeStruct((n_rows, D_MODEL), MOE_DTYPE),
                   jax.ShapeDtypeStruct((rows, LANES), jnp.int32)],
        compiler_params=_params(("arbitrary",)),
        name="moe_dispatch",
    )(tables["base"], tables["loc"], tables["units"], tables["tile_units"], tables["tail_start"],
      tables["tail_units"], tables["n_used"], hn, ids, ranks, tables["loc_lanes"])


WEIGHT_DMA_PRIORITIES = (1, 0, 1)
WEIGHT_SLOTS = 3


def _expert_kernel(be_ref, slot_ref, nxt1_ref, nxt2_ref, nused_ref, x_ref, wg_hbm, wu_hbm, wd_hbm, o_ref,
                   wgf, wuf, wdf, wgb, wub, wdb, wsem, *, layer):
    i = pl.program_id(0)
    n_used = nused_ref[0]

    def weight_copies(expert, wslot):
        pairs = ((wg_hbm, wgf), (wu_hbm, wuf), (wd_hbm, wdf))
        return [pltpu.make_async_copy(w.at[layer, expert], buf.at[wslot], wsem.at[wslot]) for w, buf in pairs]

    def start_weights(expert, wslot):
        for c, priority in zip(weight_copies(expert, wslot), WEIGHT_DMA_PRIORITIES):
            c.start(priority=priority)

    @pl.when(i == 0)
    def _():
        start_weights(be_ref[0], 0)

        @pl.when(nxt1_ref[0] >= 0)
        def _():
            start_weights(nxt1_ref[0], 1)

    @pl.when(i < n_used)
    def _():
        @pl.when((i == 0) | (be_ref[i] != be_ref[jnp.maximum(i - 1, 0)]))
        def _():
            wslot = slot_ref[i]
            for c in weight_copies(be_ref[i], wslot):
                c.wait()

            @pl.when(nxt2_ref[i] >= 0)
            def _():
                start_weights(nxt2_ref[i], lax.rem(wslot + 2, WEIGHT_SLOTS))

            wgb[...] = wgf[wslot].astype(BF16)
            wub[...] = wuf[wslot].astype(BF16)
            wdb[...] = wdf[wslot].astype(BF16)

        x = x_ref[...]
        gate = jnp.dot(x, wgb[...], preferred_element_type=F32)
        up = jnp.dot(x, wub[...], preferred_element_type=F32)
        act = (jax.nn.silu(gate) * up).astype(BF16)
        o_ref[...] = jnp.dot(act, wdb[...], preferred_element_type=F32).astype(o_ref.dtype)

    @pl.when(i >= n_used)
    def _():
        o_ref[...] = jnp.zeros(o_ref.shape, o_ref.dtype)


def _experts(x_sorted, tables, wg, wu, wd, layer, n_blk):
    hbm = pl.BlockSpec(memory_space=pl.ANY)
    grid_spec = pltpu.PrefetchScalarGridSpec(
        num_scalar_prefetch=5,
        grid=(n_blk,),
        in_specs=[pl.BlockSpec((MOE_BM, D_MODEL), lambda i, be, slot, n1, n2, nu: (jnp.minimum(i, nu[0] - 1), 0)),
                  hbm, hbm, hbm],
        out_specs=pl.BlockSpec((MOE_BM, D_MODEL), lambda i, *_: (i, 0)),
        scratch_shapes=[
            pltpu.VMEM((WEIGHT_SLOTS, D_MODEL, D_EXPERT), F32),
            pltpu.VMEM((WEIGHT_SLOTS, D_MODEL, D_EXPERT), F32),
            pltpu.VMEM((WEIGHT_SLOTS, D_EXPERT, D_MODEL), F32),
            pltpu.VMEM((D_MODEL, D_EXPERT), BF16),
            pltpu.VMEM((D_MODEL, D_EXPERT), BF16),
            pltpu.VMEM((D_EXPERT, D_MODEL), BF16),
            pltpu.SemaphoreType.DMA((WEIGHT_SLOTS,)),
        ],
    )
    return pl.pallas_call(
        functools.partial(_expert_kernel, layer=layer),
        grid_spec=grid_spec,
        out_shape=jax.ShapeDtypeStruct((n_blk * MOE_BM, D_MODEL), MOE_DTYPE),
        compiler_params=_params(("arbitrary",)),
        name="moe_experts",
    )(tables["blk_expert"], tables["blk_slot"], tables["blk_next1"], tables["blk_next2"], tables["n_used"],
      x_sorted, wg, wu, wd)


def _combine_kernel(base_ref, loc_ref, units_ref, tile_units_ref, h_ref, wt_ref, lrow_ref, g_ref, y_hbm, o_ref,
                    ybuf, sem, *, final_norm):
    j = pl.program_id(0)
    n = pl.num_programs(0)
    slot = j % 2
    tt, rt = h_ref.shape[0], ybuf.shape[1]
    assert rt % LANES == 0

    @pl.when(j == 0)
    def _():
        ybuf[...] = jnp.zeros(ybuf.shape, ybuf.dtype)
        _segment_copies(0, y_hbm, ybuf.at[0], sem.at[0], False, base_ref, loc_ref, units_ref)

    @pl.when(j + 1 < n)
    def _():
        _segment_copies(j + 1, y_hbm, ybuf.at[1 - slot], sem.at[1 - slot], False, base_ref, loc_ref, units_ref)

    _wait_unit_copies(tile_units_ref[j], y_hbm, ybuf.at[slot], sem.at[slot])
    lrow = lrow_ref[...]
    wt = wt_ref[...]
    col = lax.broadcasted_iota(jnp.int32, (tt, rt), 1)
    is0 = col == lrow[:, 0:1]
    is1 = col == lrow[:, 1:2]
    w_of_row = jnp.sum(jnp.where(is0, wt[:, 0:1], 0.0) + jnp.where(is1, wt[:, 1:2], 0.0), axis=0, keepdims=True)
    w_rows = jnp.transpose(jnp.broadcast_to(w_of_row, (LANES, rt)))
    w_rows = jnp.tile(w_rows, (1, OUT_TN // LANES))
    pick = (is0 | is1).astype(BF16)
    for c in range(D_MODEL // OUT_TN):
        cols = slice(c * OUT_TN, (c + 1) * OUT_TN)
        y = (ybuf[slot, :, cols].astype(F32) * w_rows).astype(BF16)
        o_ref[:, cols] = h_ref[:, cols] + jnp.dot(pick, y, preferred_element_type=F32)
    if final_norm:
        o_ref[...] = _rms_norm(o_ref[...], g_ref[...])


def _combine(h, wts, lrows, y_sorted, tables, g, tt, final_norm):
    rows = h.shape[0]
    rt = _moe_tile_rows(tt)
    grid_spec = pltpu.PrefetchScalarGridSpec(
        num_scalar_prefetch=4,
        grid=(rows // tt,),
        in_specs=[
            pl.BlockSpec((tt, D_MODEL), lambda j, *_: (j, 0)),
            pl.BlockSpec((tt, LANES), lambda j, *_: (j, 0)),
            pl.BlockSpec((tt, LANES), lambda j, *_: (j, 0)),
            pl.BlockSpec((1, D_MODEL), lambda j, *_: (0, 0)),
            pl.BlockSpec(memory_space=pl.ANY),
        ],
        out_specs=pl.BlockSpec((tt, D_MODEL), lambda j, *_: (j, 0)),
        scratch_shapes=[pltpu.VMEM((2, rt, D_MODEL), MOE_DTYPE), pltpu.SemaphoreType.DMA((2,))],
    )
    return pl.pallas_call(
        functools.partial(_combine_kernel, final_norm=final_norm),
        grid_spec=grid_spec,
        out_shape=jax.ShapeDtypeStruct((rows, D_MODEL), F32),
        compiler_params=_params(("arbitrary",)),
        name="moe_combine",
    )(tables["base"], tables["loc"], tables["units"], tables["tile_units"], h, wts, lrows, g, y_sorted)


def _moe(h, hn, wts, ids, ranks, tile_cnt, wg, wu, wd, layer, final_g):
    rows = h.shape[0]
    n_tiles = tile_cnt.shape[0]
    tt = rows // n_tiles
    n_rows = 2 * rows + n_tiles * N_EXPERTS * (MOE_UNIT - 1) + N_EXPERTS * (MOE_BM - 1)
    n_blk = -(-n_rows // MOE_BM)
    tables = _moe_tables(tile_cnt, n_blk)
    x_sorted, lrows = _dispatch(hn, ids, ranks, tables, tt, n_blk * MOE_BM)
    y_sorted = _experts(x_sorted, tables, wg, wu, wd, layer, n_blk)
    final_norm = final_g is not None
    g = final_g if final_norm else jnp.ones((1, D_MODEL), F32)
    return _combine(h, wts, lrows, y_sorted, tables, g, tt, final_norm)


def _rope_tables(batch, lp):
    half = HEAD_DIM // 2
    inv = ROPE_THETA ** (-jnp.arange(half, dtype=F32) / half)
    pos = (jnp.arange(lp, dtype=jnp.int32) - PAD).astype(F32)
    ang = pos[:, None] * inv[None, :]
    cos = jnp.tile(jnp.cos(ang), (batch, LANES // half))
    sin = jnp.sin(ang)
    sin = jnp.tile(jnp.concatenate([-sin, sin], axis=1), (batch, LANES // HEAD_DIM))
    return cos, sin


def kernel(x, meta_tokens, attn_norm_g, ffn_norm_g, w_in, swa_sinks, ssm_lambda_re, ssm_lambda_im, ssm_b_re, ssm_b_im, ssm_c_re, ssm_c_im, ssm_d, ssm_log_dt, ssm_w_glu, diff_lambda_q1, diff_lambda_k1, diff_lambda_q2, diff_lambda_k2, diff_subln_g, w_out, router_group_w, router_group_b, router_expert_w, router_expert_b, moe_w_gate, moe_w_up, moe_w_down, final_norm_g):
    batch, seq, d = x.shape
    depth = w_in.shape[0]
    lp = PAD + N_META + seq
    nb = lp // BLK
    meta = jnp.broadcast_to(meta_tokens[None].astype(x.dtype), (batch, N_META, d))
    h = jnp.concatenate([jnp.zeros((batch, PAD, d), x.dtype), meta, x], axis=1).reshape(batch * lp, d)
    cos, sin = _rope_tables(batch, lp)
    for l in range(depth):
        att, us, qd, kd, vd = _norm_inproj(h, attn_norm_g[l][None], w_in[l].astype(BF16), cos, sin)
        o_a = _swa(att, swa_sinks[l].astype(F32), batch, nb)
        kc, bc, cc, a_t = _ssm_operators(ssm_lambda_re[l], ssm_lambda_im[l], ssm_b_re[l], ssm_b_im[l],
                                         ssm_c_re[l], ssm_c_im[l], ssm_log_dt[l])
        yg = _ssm(us, kc, bc, cc, a_t, ssm_d[l].astype(F32).reshape(SSM_PACKS, 1, LANES), batch, lp)
        lam_init = 0.8 - 0.6 * math.exp(-0.3 * l)
        lam_vecs = jnp.stack([diff_lambda_q1[l], diff_lambda_k1[l], diff_lambda_q2[l], diff_lambda_k2[l]]).astype(F32)
        o_c = _diff_attn(qd, kd, vd, lam_vecs, diff_subln_g[l][None].astype(F32), lam_init, batch, nb)
        wo = w_out[l].astype(BF16)
        rw = jnp.concatenate([router_group_w[l], router_expert_w[l]], axis=1).astype(F32)
        rw = jnp.pad(rw, ((0, 0), (0, LANES - rw.shape[1])))
        rb = jnp.concatenate([router_group_b[l], router_expert_b[l]]).astype(F32)
        rb = jnp.pad(rb, (0, LANES - rb.shape[0]))[None]
        rw_hi, rw_lo = _split_bf16(rw)
        g_ffn = ffn_norm_g[l][None].astype(F32)
        h, hn, wts, ids, ranks, tile_cnt = _mixer_out(h, o_a, yg, o_c, wo, ssm_w_glu[l].astype(BF16), g_ffn, rw_hi,
                                                      rw_lo, rb)
        final_g = final_norm_g[None].astype(F32) if l == depth - 1 else None
        h = _moe(h, hn, wts, ids, ranks, tile_cnt, moe_w_gate, moe_w_up, moe_w_down, l, final_g)
    return h.reshape(batch, lp, d)[:, PAD + N_META:]
```

```python
import functools
import math

import jax
import jax.numpy as jnp
from jax import lax
from jax.experimental import pallas as pl
from jax.experimental.pallas import tpu as pltpu

D_MODEL = 2048
N_META = 16
BLK = 128
PAD = BLK - N_META
ROPE_THETA = 10000.0
NORM_EPS = 1e-5
NEG = -1e30
SWA_HEADS = 12
SWA_KV_HEADS = 4
HEAD_DIM = 64
SWA_WIDTH = SWA_HEADS * HEAD_DIM
SSM_WIDTH = 768
SSM_GROUP = 16
SSM_GROUPS = SSM_WIDTH // SSM_GROUP
SSM_STATE = 64
DIFF_HEADS = 4
DIFF_V_DIM = 128
DIFF_WIDTH = DIFF_HEADS * DIFF_V_DIM
IN_WIDTH = 3584
N_EXPERT_GROUPS = 4
EXPERTS_PER_GROUP = 8
N_EXPERTS = N_EXPERT_GROUPS * EXPERTS_PER_GROUP
D_EXPERT = 512

COL_KA = 768
COL_VA = 1024
COL_US = 1280
COL_QD = 2048
COL_KD = 2560
COL_VD = 3072

LANES = 128
SSM_T = 16
SSM_PACK = LANES // SSM_GROUP
SSM_PACKS = SSM_GROUPS // SSM_PACK
SSM_PSTATE = SSM_PACK * SSM_STATE
MOE_BM = 256
VMEM_LIMIT = 56 * 1024 * 1024

F32 = jnp.float32
BF16 = jnp.bfloat16


def _row_tile(rows):
    for t in (512, 256, 128):
        if rows % t == 0:
            return t
    raise ValueError(f"row count {rows} is not a multiple of 128")


def _params(sem, vmem=VMEM_LIMIT):
    return pltpu.CompilerParams(dimension_semantics=sem, vmem_limit_bytes=vmem)


def _rms_norm(x, g):
    return x * lax.rsqrt(jnp.mean(x * x, axis=-1, keepdims=True) + NORM_EPS) * g


IN_TN = 512
ROPE_TILES = (0, 1, 4, 5)


def _norm_inproj_kernel(x_ref, g_ref, w_ref, cos_ref, sin_ref, att_ref, us_ref, qd_ref, kd_ref, vd_ref, xn_ref):
    blocked = ((COL_VD, vd_ref), (COL_KD, kd_ref), (COL_QD, qd_ref), (COL_US, us_ref))

    def store(col, piece):
        for start, ref in blocked:
            if col >= start:
                ref[(col - start) // LANES] = piece
                return
        att_ref[:, col:col + LANES] = piece

    x = x_ref[...]
    ms = jnp.mean(x * x, axis=-1, keepdims=True)
    xn_ref[...] = (x * lax.rsqrt(ms + NORM_EPS) * g_ref[...]).astype(BF16)
    shape = (x.shape[0], IN_TN)
    lane = lax.broadcasted_iota(jnp.int32, shape, 1)
    first_half = (lane & (HEAD_DIM - 1)) < (HEAD_DIM // 2)
    reps = IN_TN // LANES
    cos = jnp.tile(cos_ref[...], (1, reps))
    sin = jnp.tile(sin_ref[...], (1, reps))
    for j in range(IN_WIDTH // IN_TN):
        cols = slice(j * IN_TN, (j + 1) * IN_TN)
        acc = jnp.dot(xn_ref[...], w_ref[:, cols], preferred_element_type=F32)
        if j in ROPE_TILES:
            partner = jnp.where(first_half, pltpu.roll(acc, IN_TN - HEAD_DIM // 2, 1),
                                pltpu.roll(acc, HEAD_DIM // 2, 1))
            acc = acc * cos + partner * sin
        acc = acc.astype(BF16)
        for c in range(IN_TN // LANES):
            store(j * IN_TN + c * LANES, acc[:, c * LANES:(c + 1) * LANES])


def _norm_inproj(h, g, w, layer, cos, sin):
    rows = h.shape[0]
    tm = _row_tile(rows)

    def blocked(width):
        n = width // LANES
        return pl.BlockSpec((n, tm, LANES), lambda i: (0, i, 0)), jax.ShapeDtypeStruct((n, rows, LANES), BF16)

    specs, shapes = zip(
        (pl.BlockSpec((tm, COL_US), lambda i: (i, 0)), jax.ShapeDtypeStruct((rows, COL_US), BF16)),
        blocked(SSM_WIDTH), blocked(DIFF_WIDTH), blocked(DIFF_WIDTH), blocked(DIFF_WIDTH))
    return pl.pallas_call(
        _norm_inproj_kernel,
        grid=(rows // tm,),
        in_specs=[
            pl.BlockSpec((tm, D_MODEL), lambda i: (i, 0)),
            pl.BlockSpec((1, D_MODEL), lambda i: (0, 0)),
            pl.BlockSpec((None, D_MODEL, IN_WIDTH), lambda i: (layer, 0, 0)),
            pl.BlockSpec((tm, LANES), lambda i: (i, 0)),
            pl.BlockSpec((tm, LANES), lambda i: (i, 0)),
        ],
        out_specs=list(specs),
        out_shape=list(shapes),
        scratch_shapes=[pltpu.VMEM((tm, D_MODEL), BF16)],
        compiler_params=_params(("parallel",)),
        name="norm_inproj",
    )(h, g, w, cos, sin)


def _swa_kernel(sink_ref, q_ref, kp_ref, kc_ref, vp_ref, vc_ref, o_ref):
    n = pl.program_id(1)
    shape = (BLK, 2 * BLK)
    qi = lax.broadcasted_iota(jnp.int32, shape, 0)
    c = lax.broadcasted_iota(jnp.int32, shape, 1)
    kj = (n - 1) * BLK + c
    delta = qi + BLK - c
    ok = (delta >= 0) & (delta < BLK) & (kj >= PAD)
    grp = SWA_HEADS // SWA_KV_HEADS
    outs = []
    for j in range(SWA_KV_HEADS):
        cols = slice(j * HEAD_DIM, (j + 1) * HEAD_DIM)
        kk = jnp.concatenate([kp_ref[:, cols], kc_ref[:, cols]], axis=0)
        vv = jnp.concatenate([vp_ref[:, cols], vc_ref[:, cols]], axis=0)
        for g in range(grp):
            hd = j * grp + g
            qh = q_ref[:, hd * HEAD_DIM:(hd + 1) * HEAD_DIM]
            s = lax.dot_general(qh, kk, (((1,), (1,)), ((), ())), preferred_element_type=F32) * (HEAD_DIM ** -0.5)
            s = jnp.where(ok, s, NEG)
            sink = sink_ref[hd]
            m = jnp.maximum(jnp.max(s, axis=-1, keepdims=True), sink)
            p = jnp.exp(s - m)
            denom = jnp.sum(p, axis=-1, keepdims=True) + jnp.exp(sink - m)
            outs.append(jnp.dot(p.astype(BF16), vv, preferred_element_type=F32) / denom)
    o_ref[...] = jnp.concatenate(outs, axis=1).astype(o_ref.dtype)


def _swa(proj, sinks, batch, nb):
    rows = proj.shape[0]
    kvw = SWA_KV_HEADS * HEAD_DIM

    def cur(col):
        return lambda b, n: (b * nb + n, col)

    def prev(col):
        return lambda b, n: (b * nb + jnp.maximum(n - 1, 0), col)

    return pl.pallas_call(
        _swa_kernel,
        grid=(batch, nb),
        in_specs=[
            pl.BlockSpec(memory_space=pltpu.SMEM),
            pl.BlockSpec((BLK, SWA_WIDTH), cur(0)),
            pl.BlockSpec((BLK, kvw), prev(COL_KA // kvw)),
            pl.BlockSpec((BLK, kvw), cur(COL_KA // kvw)),
            pl.BlockSpec((BLK, kvw), prev(COL_VA // kvw)),
            pl.BlockSpec((BLK, kvw), cur(COL_VA // kvw)),
        ],
        out_specs=pl.BlockSpec((BLK, SWA_WIDTH), cur(0)),
        out_shape=jax.ShapeDtypeStruct((rows, SWA_WIDTH), BF16),
        compiler_params=_params(("parallel", "parallel")),
        name="swa",
    )(sinks, proj, proj, proj, proj, proj)


DIFF_QBLOCKS = 2


def _diff_kernel(lam_ref, g_ref, q_ref, k_ref, v_ref, o_ref, *, lam_init, nb):
    lane = lax.broadcasted_iota(jnp.int32, (BLK, LANES), 1)
    lv = lam_ref[...]
    lam = (jnp.exp(jnp.sum(lv[0:1] * lv[1:2], axis=-1, keepdims=True))
           - jnp.exp(jnp.sum(lv[2:3] * lv[3:4], axis=-1, keepdims=True)) + lam_init)
    gain = g_ref[...] * (1.0 - lam_init)
    scale = HEAD_DIM ** -0.5
    for n0 in range(0, nb, DIFF_QBLOCKS):
        blocks = range(n0, min(n0 + DIFF_QBLOCKS, nb))
        ke = (blocks[-1] + 1) * BLK
        kx = k_ref[0:ke, :]
        vx = v_ref[0:ke, :]
        rows = 2 * BLK * len(blocks)
        pieces = []
        for n in blocks:
            q = q_ref[n * BLK:(n + 1) * BLK, :] * scale
            zero = jnp.zeros_like(q)
            pieces += [jnp.where(lane < HEAD_DIM, q, zero), jnp.where(lane >= HEAD_DIM, q, zero)]
        s = lax.dot_general(jnp.concatenate(pieces, axis=0), kx, (((1,), (1,)), ((), ())),
                            preferred_element_type=F32)
        r = lax.broadcasted_iota(jnp.int32, (rows, BLK), 0)
        qpos = (n0 + jnp.right_shift(r, (2 * BLK).bit_length() - 1)) * BLK + (r & (BLK - 1))
        col = lax.broadcasted_iota(jnp.int32, (rows, BLK), 1)
        key_blocks = []
        for t in range(blocks[-1] + 1):
            blk = s[:, t * BLK:(t + 1) * BLK]
            ok = None
            if t >= n0:
                ok = t * BLK + col <= qpos
            if t == 0:
                ok = (col >= PAD) if ok is None else ok & (col >= PAD)
            key_blocks.append(blk if ok is None else jnp.where(ok, blk, NEG))
        s = jnp.concatenate(key_blocks, axis=1)
        m = jnp.max(s, axis=-1, keepdims=True)
        p = jnp.exp(s - m)
        denom = jnp.sum(p, axis=-1, keepdims=True)
        heads = jnp.dot(p.astype(BF16), vx, preferred_element_type=F32) * (1.0 / denom)
        for i, n in enumerate(blocks):
            o = heads[2 * i * BLK:(2 * i + 1) * BLK] - lam * heads[(2 * i + 1) * BLK:(2 * i + 2) * BLK]
            o = o * lax.rsqrt(jnp.mean(o * o, axis=-1, keepdims=True) + NORM_EPS) * gain
            o_ref[n * BLK:(n + 1) * BLK, :] = o.astype(o_ref.dtype)


def _diff_attn(qd, kd, vd, lam_vecs, subln_g, lam_init, batch, nb):
    rows = qd.shape[1]
    lp = nb * BLK
    head_rows = pl.BlockSpec((None, lp, LANES), lambda b, h: (h, b, 0))
    return pl.pallas_call(
        functools.partial(_diff_kernel, lam_init=lam_init, nb=nb),
        grid=(batch, DIFF_HEADS),
        in_specs=[
            pl.BlockSpec((4, HEAD_DIM), lambda b, h: (0, 0)),
            pl.BlockSpec((1, DIFF_V_DIM), lambda b, h: (0, 0)),
            head_rows, head_rows, head_rows,
        ],
        out_specs=head_rows,
        out_shape=jax.ShapeDtypeStruct((DIFF_HEADS, rows, DIFF_V_DIM), BF16),
        compiler_params=_params(("parallel", "parallel")),
        name="diff_attn",
    )(lam_vecs, subln_g, qd, kd, vd)


def _ssm_row_chunk(rows):
    for c in (544, 512, 384, 256, 128, 64, 32, 16):
        if rows % c == 0:
            return c
    raise ValueError(rows)


def _ssm_expand_operators(kc_ref, bc_ref, cc_ref, k_ref, bw_ref, cw_ref):
    def iota(shape, dim):
        return lax.broadcasted_iota(jnp.int32, shape, dim)

    def group(idx, width):
        return jnp.right_shift(idx, width.bit_length() - 1) & (SSM_PACK - 1)

    hs, ps, tl = SSM_GROUP, SSM_STATE, SSM_T * LANES
    spread_h = ((iota((LANES, hs), 0) & (hs - 1)) == iota((LANES, hs), 1)).astype(BF16)
    kt = jnp.dot(spread_h, kc_ref[...], preferred_element_type=F32)
    same = group(iota((LANES, tl), 0), hs) == group(iota((LANES, tl), 1), hs)
    k_ref[...] = jnp.where(same, kt, 0.0).astype(BF16)
    spread_s = (iota((ps, SSM_PSTATE), 0) == (iota((ps, SSM_PSTATE), 1) & (ps - 1))).astype(BF16)
    shape = (tl, SSM_PSTATE)
    same = group(iota(shape, 0), hs) == group(iota(shape, 1), ps)
    for part in range(2):
        bw = jnp.dot(bc_ref[part], spread_s, preferred_element_type=F32)
        bw_ref[:, part * SSM_PSTATE:(part + 1) * SSM_PSTATE] = jnp.where(same, bw, 0.0).astype(BF16)
    shape = (LANES, SSM_PSTATE)
    same = group(iota(shape, 0), hs) == group(iota(shape, 1), ps)
    for j in range(SSM_T):
        for part in range(2):
            ct = jnp.dot(spread_h, cc_ref[part, j], preferred_element_type=F32)
            cw_ref[j, :, part * SSM_PSTATE:(part + 1) * SSM_PSTATE] = jnp.where(same, ct, 0.0).astype(BF16)


def _ssm_kernel(u_ref, kc_ref, bc_ref, cc_ref, a_ref, d_ref, o_ref, up_ref, y_ref, e_ref, s_ref, k_ref, bw_ref,
                cw_ref, *, batch, lp):
    rows = batch * lp
    n_chunks = lp // SSM_T
    tot_chunks = rows // SSM_T
    _ssm_expand_operators(kc_ref, bc_ref, cc_ref, k_ref, bw_ref, cw_ref)
    contract_lanes = (((1,), (1,)), ((), ()))
    up_ref[0:SSM_T, :] = jnp.zeros((SSM_T, LANES), F32)
    up_ref[SSM_T:, :] = u_ref[...].astype(F32)
    for b in range(batch):
        up_ref[SSM_T + b * lp:SSM_T + b * lp + PAD, :] = jnp.zeros((PAD, LANES), F32)

    ch = _ssm_row_chunk(rows)
    rmod = lax.broadcasted_iota(jnp.int32, (ch, LANES), 0) & (SSM_T - 1)
    for r0 in range(0, rows, ch):
        pieces = []
        for k in range(SSM_T):
            sl = up_ref[SSM_T - k + r0:SSM_T - k + r0 + ch, :]
            pieces.append(jnp.where(rmod >= k, sl, 0.0).astype(BF16))
        y_ref[r0:r0 + ch, :] = lax.dot_general(jnp.concatenate(pieces, axis=1), k_ref[...], contract_lanes,
                                               preferred_element_type=F32)

    pieces = [up_ref[pl.ds(SSM_T + i, tot_chunks, stride=SSM_T), :].astype(BF16) for i in range(SSM_T)]
    e_ref[...] = jnp.dot(jnp.concatenate(pieces, axis=1), bw_ref[...], preferred_element_type=F32)

    a_re = a_ref[:, :SSM_PSTATE]
    a_im = a_ref[:, SSM_PSTATE:]

    def step(c, carry):
        new = []
        for b in range(batch):
            sr, si = carry[b]
            idx = b * n_chunks + c
            s_ref[pl.ds(idx, 1), :SSM_PSTATE] = sr
            s_ref[pl.ds(idx, 1), SSM_PSTATE:] = si
            er = e_ref[pl.ds(idx, 1), :SSM_PSTATE]
            ei = e_ref[pl.ds(idx, 1), SSM_PSTATE:]
            new.append((a_re * sr - a_im * si + er, a_re * si + a_im * sr + ei))
        return tuple(new)

    zero = jnp.zeros((1, SSM_PSTATE), F32)
    lax.fori_loop(0, n_chunks, step, tuple((zero, zero) for _ in range(batch)))

    sb = s_ref[...].astype(BF16)
    for j in range(SSM_T):
        z = lax.dot_general(sb, cw_ref[j], contract_lanes, preferred_element_type=F32)
        idx = pl.ds(j, tot_chunks, stride=SSM_T)
        y_ref[idx, :] = y_ref[idx, :] + z

    y = y_ref[...] + d_ref[...] * up_ref[SSM_T:, :]
    o_ref[...] = jax.nn.gelu(y, approximate=True).astype(o_ref.dtype)


def _ssm(us, kc, bc, cc, a_t, dskip, batch, lp):
    rows = us.shape[1]
    tot_chunks = rows // SSM_T
    return pl.pallas_call(
        functools.partial(_ssm_kernel, batch=batch, lp=lp),
        grid=(SSM_PACKS,),
        in_specs=[
            pl.BlockSpec((None, rows, LANES), lambda p: (p, 0, 0)),
            pl.BlockSpec((None, SSM_GROUP, SSM_T * LANES), lambda p: (p, 0, 0)),
            pl.BlockSpec((None, 2, SSM_T * LANES, SSM_STATE), lambda p: (p, 0, 0, 0)),
            pl.BlockSpec((None, 2, SSM_T, SSM_GROUP, SSM_PSTATE), lambda p: (p, 0, 0, 0, 0)),
            pl.BlockSpec((None, 1, 2 * SSM_PSTATE), lambda p: (p, 0, 0)),
            pl.BlockSpec((None, 1, LANES), lambda p: (p, 0, 0)),
        ],
        out_specs=pl.BlockSpec((None, rows, LANES), lambda p: (p, 0, 0)),
        out_shape=jax.ShapeDtypeStruct((SSM_PACKS, rows, LANES), F32),
        scratch_shapes=[
            pltpu.VMEM((rows + SSM_T, LANES), F32),
            pltpu.VMEM((rows, LANES), F32),
            pltpu.VMEM((tot_chunks, 2 * SSM_PSTATE), F32),
            pltpu.VMEM((tot_chunks, 2 * SSM_PSTATE), F32),
            pltpu.VMEM((LANES, SSM_T * LANES), BF16),
            pltpu.VMEM((SSM_T * LANES, 2 * SSM_PSTATE), BF16),
            pltpu.VMEM((SSM_T, LANES, 2 * SSM_PSTATE), BF16),
        ],
        compiler_params=_params(("parallel",)),
        name="ssm",
    )(us, kc, bc, cc, a_t, dskip)


def _ssm_operators(lam_re, lam_im, b_re, b_im, c_re, c_im, log_dt):
    p, h = SSM_STATE, SSM_GROUP
    dt = jnp.exp(log_dt.astype(F32))[:, None]
    lr, li = lam_re.astype(F32), lam_im.astype(F32)
    mag = jnp.exp(lr * dt)
    abar_re, abar_im = mag * jnp.cos(li * dt), mag * jnp.sin(li * dt)
    den = lr * lr + li * li
    nr = abar_re - 1.0
    f_re = (nr * lr + abar_im * li) / den
    f_im = (abar_im * lr - nr * li) / den
    br, bi = b_re.astype(F32), b_im.astype(F32)
    bbar_re = f_re[..., None] * br - f_im[..., None] * bi
    bbar_im = f_re[..., None] * bi + f_im[..., None] * br
    ks = jnp.arange(SSM_T + 1, dtype=F32)[:, None, None]
    pmag = jnp.exp(ks * (lr * dt)[None])
    pw_re = pmag * jnp.cos(ks * (li * dt)[None])
    pw_im = pmag * jnp.sin(ks * (li * dt)[None])
    cr, ci = c_re.astype(F32), c_im.astype(F32)

    packs, q = SSM_PACKS, SSM_PACK

    cb_re = cr[:, :, :, None] * bbar_re[:, None, :, :] - ci[:, :, :, None] * bbar_im[:, None, :, :]
    cb_im = cr[:, :, :, None] * bbar_im[:, None, :, :] + ci[:, :, :, None] * bbar_re[:, None, :, :]
    klag = (jnp.einsum('kgs,gosi->kgoi', pw_re[:SSM_T], cb_re)
            - jnp.einsum('kgs,gosi->kgoi', pw_im[:SSM_T], cb_im))
    kc = klag.reshape(SSM_T, packs, q, h, h).transpose(1, 3, 0, 2, 4).reshape(packs, h, SSM_T * LANES)

    rev_re, rev_im = pw_re[SSM_T - 1::-1, :, None, :], pw_im[SSM_T - 1::-1, :, None, :]
    bt_re, bt_im = jnp.swapaxes(bbar_re, 1, 2)[None], jnp.swapaxes(bbar_im, 1, 2)[None]

    def inj_by_pack(x):
        return x.reshape(SSM_T, packs, q * h, p).transpose(1, 0, 2, 3).reshape(packs, SSM_T * LANES, p)

    bc = jnp.stack([inj_by_pack(rev_re * bt_re - rev_im * bt_im), inj_by_pack(rev_re * bt_im + rev_im * bt_re)], axis=1)

    ca_re = cr[None] * pw_re[1:, :, None, :] - ci[None] * pw_im[1:, :, None, :]
    ca_im = cr[None] * pw_im[1:, :, None, :] + ci[None] * pw_re[1:, :, None, :]

    def out_by_pack(x):
        return x.reshape(SSM_T, packs, q, h, p).transpose(1, 0, 3, 2, 4).reshape(packs, SSM_T, h, SSM_PSTATE)

    cc = jnp.stack([out_by_pack(ca_re), out_by_pack(-ca_im)], axis=1)

    a_t = jnp.concatenate([pw_re[SSM_T].reshape(packs, 1, SSM_PSTATE),
                           pw_im[SSM_T].reshape(packs, 1, SSM_PSTATE)], axis=-1)
    return kc.astype(BF16), bc.astype(BF16), cc.astype(BF16), a_t


OUT_TN = 512


def _split_bf16(x):
    hi = x.astype(BF16)
    return hi, (x - hi.astype(F32)).astype(BF16)


def _route(hn, rwh_ref, rwl_ref, rb_ref):
    hi, lo = _split_bf16(hn)
    logits = (jnp.dot(hi, rwh_ref[...], preferred_element_type=F32)
              + (jnp.dot(lo, rwh_ref[...], preferred_element_type=F32)
                 + jnp.dot(hi, rwl_ref[...], preferred_element_type=F32))) + rb_ref[...]
    lane = lax.broadcasted_iota(jnp.int32, logits.shape, 1)

    def first_argmax(v, vmax):
        return jnp.min(jnp.where(v == vmax, lane, LANES), axis=-1, keepdims=True)

    gl = jnp.where(lane < N_EXPERT_GROUPS, logits, NEG)
    gmax = jnp.max(gl, axis=-1, keepdims=True)
    g_sel = first_argmax(gl, gmax)
    p_g = 1.0 / jnp.sum(jnp.exp(gl - gmax), axis=-1, keepdims=True)
    e_lane = lane - N_EXPERT_GROUPS
    in_group = (e_lane >= 0) & (e_lane < N_EXPERTS) & (jnp.right_shift(e_lane, 3) == g_sel)
    el = jnp.where(in_group, logits, NEG)
    m1 = jnp.max(el, axis=-1, keepdims=True)
    i1 = first_argmax(el, m1)
    el2 = jnp.where(lane == i1, NEG, el)
    m2 = jnp.max(el2, axis=-1, keepdims=True)
    i2 = first_argmax(el2, m2)
    z = jnp.sum(jnp.exp(el - m1), axis=-1, keepdims=True)
    p1 = 1.0 / z
    p2 = jnp.exp(m2 - m1) / z
    w1 = p1 / (p1 + p2) * p_g
    w2 = p2 / (p1 + p2) * p_g
    wts = jnp.where(lane == 0, w1, jnp.where(lane == 1, w2, 0.0))
    ids = jnp.where(lane == 0, i1 - N_EXPERT_GROUPS, jnp.where(lane == 1, i2 - N_EXPERT_GROUPS, 0))
    return wts, ids


def _assignment_ranks(ids):
    tm = ids.shape[0]
    lane = lax.broadcasted_iota(jnp.int32, ids.shape, 1)
    oh0 = lane == ids[:, 0:1]
    oh1 = lane == ids[:, 1:2]
    hits = (oh0 | oh1).astype(BF16)
    r = lax.broadcasted_iota(jnp.int32, (tm, tm), 0)
    c = lax.broadcasted_iota(jnp.int32, (tm, tm), 1)
    before = jnp.dot((c < r).astype(BF16), hits, preferred_element_type=F32)
    rank0 = jnp.sum(jnp.where(oh0, before, 0.0), axis=-1, keepdims=True)
    rank1 = jnp.sum(jnp.where(oh1, before, 0.0), axis=-1, keepdims=True)
    ranks = jnp.where(lane == 0, rank0, jnp.where(lane == 1, rank1, 0.0)).astype(jnp.int32)
    return ranks, jnp.sum(hits.astype(F32), axis=0, keepdims=True).astype(jnp.int32)


def _mixer_out_kernel(h_ref, a_ref, y_ref, c_ref, wo_ref, wglu_ref, g_ref, rwh_ref, rwl_ref, rb_ref,
                      o_ref, hn_ref, wt_ref, id_ref, rank_ref, cnt_ref):
    half = h_ref.shape[0] // 2
    halves = [slice(r * half, (r + 1) * half) for r in range(2)]
    for rows in halves:
        y = jnp.concatenate([y_ref[p, rows, :] for p in range(y_ref.shape[0])], axis=1)
        c = jnp.concatenate([c_ref[hd, rows, :] for hd in range(c_ref.shape[0])], axis=1)
        gate = jnp.dot(y.astype(BF16), wglu_ref[...], preferred_element_type=F32)
        b = (y * jax.nn.sigmoid(gate)).astype(BF16)
        mixed = jnp.concatenate([a_ref[rows, :], b, c], axis=1)
        for j in range(D_MODEL // OUT_TN):
            cols = slice(j * OUT_TN, (j + 1) * OUT_TN)
            o_ref[rows, cols] = h_ref[rows, cols] + jnp.dot(mixed, wo_ref[:, cols], preferred_element_type=F32)
    for rows in halves:
        hn = _rms_norm(o_ref[rows, :], g_ref[...])
        hn_ref[rows, :] = hn.astype(BF16)
        wts, ids = _route(hn, rwh_ref, rwl_ref, rb_ref)
        wt_ref[rows, :] = wts
        id_ref[rows, :] = ids
    ranks, counts = _assignment_ranks(id_ref[...])
    rank_ref[...] = ranks
    cnt_ref[...] = jnp.broadcast_to(counts, cnt_ref.shape)


def _mixer_out(h, oa, yg, oc, wo, wglu, layer, g, rw_hi, rw_lo, rb):
    rows = h.shape[0]
    tm = _row_tile(rows)

    def layer_weight(shape):
        return pl.BlockSpec((None,) + shape, lambda i: (layer, 0, 0))

    def row_spec(width):
        return pl.BlockSpec((tm, width), lambda i: (i, 0))

    def whole(shape):
        return pl.BlockSpec(shape, lambda i: (0, 0))

    def blocked(width):
        return pl.BlockSpec((width // LANES, tm, LANES), lambda i: (0, i, 0))

    return pl.pallas_call(
        _mixer_out_kernel,
        grid=(rows // tm,),
        in_specs=[
            row_spec(D_MODEL), row_spec(SWA_WIDTH), blocked(SSM_WIDTH), blocked(DIFF_WIDTH),
            layer_weight((D_MODEL, D_MODEL)), layer_weight((SSM_WIDTH, SSM_WIDTH)), whole((1, D_MODEL)),
            whole((D_MODEL, LANES)), whole((D_MODEL, LANES)), whole((1, LANES)),
        ],
        out_specs=[row_spec(D_MODEL), row_spec(D_MODEL), row_spec(LANES), row_spec(LANES), row_spec(LANES),
                   pl.BlockSpec((None, 8, LANES), lambda i: (i, 0, 0))],
        out_shape=[
            jax.ShapeDtypeStruct((rows, D_MODEL), F32),
            jax.ShapeDtypeStruct((rows, D_MODEL), BF16),
            jax.ShapeDtypeStruct((rows, LANES), F32),
            jax.ShapeDtypeStruct((rows, LANES), jnp.int32),
            jax.ShapeDtypeStruct((rows, LANES), jnp.int32),
            jax.ShapeDtypeStruct((rows // tm, 8, LANES), jnp.int32),
        ],
        compiler_params=_params(("parallel",)),
        name="mixer_out",
    )(h, oa, yg, oc, wo, wglu, g, rw_hi, rw_lo, rb)


MOE_UNIT = 16
MOE_DTYPE = BF16


def _moe_tile_rows(tt):
    return 2 * tt + N_EXPERTS * MOE_UNIT


def _moe_tables(tile_cnt, n_blk):
    cnt = tile_cnt[:, 0, :N_EXPERTS]
    seg = (cnt + MOE_UNIT - 1) // MOE_UNIT * MOE_UNIT
    size = jnp.sum(seg, axis=0)
    padded = (size + MOE_BM - 1) // MOE_BM * MOE_BM
    pend = jnp.cumsum(padded)
    pstart = pend - padded
    base = pstart[None, :] + jnp.cumsum(seg, axis=0) - seg
    loc = jnp.cumsum(seg, axis=1) - seg
    units = seg // MOE_UNIT
    blk_start = jnp.arange(n_blk, dtype=jnp.int32) * MOE_BM
    blk_expert = jnp.minimum(jnp.sum((pend[None, :] <= blk_start[:, None]).astype(jnp.int32), axis=1), N_EXPERTS - 1)
    ex = jnp.arange(N_EXPERTS, dtype=jnp.int32)
    nonempty = size > 0
    run_slot = (jnp.cumsum(nonempty.astype(jnp.int32)) - 1) % WEIGHT_SLOTS
    later = (ex[None, :] > ex[:, None]) & nonempty[None, :]
    nxt = jnp.min(jnp.where(later, ex[None, :], N_EXPERTS), axis=1)
    nxt = jnp.where(nxt == N_EXPERTS, -1, nxt)
    nxt2 = jnp.where(nxt >= 0, nxt[jnp.maximum(nxt, 0)], -1)
    i32 = jnp.int32
    loc_lanes = jnp.pad(loc, ((0, 0), (0, LANES - N_EXPERTS)))
    return dict(
        base=base.reshape(-1).astype(i32), loc=loc.reshape(-1).astype(i32), units=units.reshape(-1).astype(i32),
        tile_units=jnp.sum(units, axis=1).astype(i32),
        tail_start=(pstart + size).astype(i32), tail_units=((padded - size) // MOE_UNIT).astype(i32),
        loc_lanes=jnp.broadcast_to(loc_lanes[:, None, :], (loc.shape[0], 8, LANES)).astype(i32),
        blk_expert=blk_expert.astype(i32), blk_slot=run_slot[blk_expert].astype(i32),
        blk_next1=nxt[blk_expert].astype(i32), blk_next2=nxt2[blk_expert].astype(i32),
        n_used=(pend[-1] // MOE_BM).astype(i32).reshape(1))


def _segment_copies(tile, hbm, buf, sem, to_hbm, base_ref, loc_ref, units_ref):
    for e in range(N_EXPERTS):
        k = tile * N_EXPERTS + e
        row_hbm, row_buf = base_ref[k], loc_ref[k]

        def start(u, carry):
            a = hbm.at[pl.ds(pl.multiple_of(row_hbm + u * MOE_UNIT, MOE_UNIT), MOE_UNIT), :]
            b = buf.at[pl.ds(pl.multiple_of(row_buf + u * MOE_UNIT, MOE_UNIT), MOE_UNIT), :]
            (pltpu.make_async_copy(b, a, sem) if to_hbm else pltpu.make_async_copy(a, b, sem)).start()
            return carry

        lax.fori_loop(0, units_ref[k], start, 0)


def _wait_unit_copies(count, hbm, buf, sem):
    def wait(u, carry):
        pltpu.make_async_copy(hbm.at[pl.ds(0, MOE_UNIT), :], buf.at[pl.ds(0, MOE_UNIT), :], sem).wait()
        return carry

    lax.fori_loop(0, count, wait, 0)


def _local_rows(ids, ranks, loc_lanes):
    lane = lax.broadcasted_iota(jnp.int32, ids.shape, 1)
    loc = loc_lanes.astype(F32)
    start0 = jnp.sum(jnp.where(lane == ids[:, 0:1], loc, 0.0), axis=-1, keepdims=True).astype(jnp.int32)
    start1 = jnp.sum(jnp.where(lane == ids[:, 1:2], loc, 0.0), axis=-1, keepdims=True).astype(jnp.int32)
    return start0 + ranks[:, 0:1], start1 + ranks[:, 1:2]


def _dispatch_kernel(base_ref, loc_ref, units_ref, tile_units_ref, tail_start_ref, tail_units_ref, nused_ref,
                     hn_ref, ids_ref, rank_ref, locv_ref, x_hbm, lrow_ref, ybuf, zbuf, sem):
    j = pl.program_id(0)
    n = pl.num_programs(0)
    slot = j % 2
    tt, rt = hn_ref.shape[0], ybuf.shape[1]

    @pl.when(j >= 2)
    def _():
        _wait_unit_copies(tile_units_ref[j - 2], x_hbm, ybuf.at[slot], sem.at[slot])

    row0, row1 = _local_rows(ids_ref[...], rank_ref[...], locv_ref[0:1, :])
    lane = lax.broadcasted_iota(jnp.int32, lrow_ref.shape, 1)
    lrow_ref[...] = jnp.where(lane == 0, row0, jnp.where(lane == 1, row1, 0))
    col = lax.broadcasted_iota(jnp.int32, (tt, rt), 1)
    onehot = ((col == row0) | (col == row1)).astype(BF16)
    compacted = lax.dot_general(onehot, hn_ref[...], (((0,), (0,)), ((), ())), preferred_element_type=F32)
    ybuf[slot] = compacted.astype(ybuf.dtype)
    _segment_copies(j, x_hbm, ybuf.at[slot], sem.at[slot], True, base_ref, loc_ref, units_ref)

    @pl.when(j == n - 1)
    def _():
        zbuf[...] = jnp.zeros(zbuf.shape, zbuf.dtype)
        zunit = zbuf.at[pl.ds(0, MOE_UNIT), :]
        total = 0
        for e in range(N_EXPERTS):
            row = tail_start_ref[e]

            def start(u, carry):
                dst = x_hbm.at[pl.ds(pl.multiple_of(row + u * MOE_UNIT, MOE_UNIT), MOE_UNIT), :]
                pltpu.make_async_copy(zunit, dst, sem.at[2]).start()
                return carry

            lax.fori_loop(0, tail_units_ref[e], start, 0)
            total = total + tail_units_ref[e]
        _wait_unit_copies(total, x_hbm, zunit, sem.at[2])
        n_blk = x_hbm.shape[0] // MOE_BM

        def block_copy(blk):
            dst = x_hbm.at[pl.ds(pl.multiple_of(blk * MOE_BM, MOE_BM), MOE_BM), :]
            return pltpu.make_async_copy(zbuf, dst, sem.at[3])

        def start_block(blk, carry):
            block_copy(blk).start()
            return carry

        def wait_block(blk, carry):
            block_copy(blk).wait()
            return carry

        lax.fori_loop(nused_ref[0], n_blk, start_block, 0)
        lax.fori_loop(nused_ref[0], n_blk, wait_block, 0)

        @pl.when(j >= 1)
        def _():
            _wait_unit_copies(tile_units_ref[jnp.maximum(j - 1, 0)], x_hbm, ybuf.at[1 - slot], sem.at[1 - slot])

        _wait_unit_copies(tile_units_ref[j], x_hbm, ybuf.at[slot], sem.at[slot])


def _dispatch(hn, ids, ranks, tables, tt, n_rows):
    rows = hn.shape[0]
    rt = _moe_tile_rows(tt)
    row_spec = pl.BlockSpec((tt, LANES), lambda j, *_: (j, 0))
    grid_spec = pltpu.PrefetchScalarGridSpec(
        num_scalar_prefetch=7,
        grid=(rows // tt,),
        in_specs=[pl.BlockSpec((tt, D_MODEL), lambda j, *_: (j, 0)), row_spec, row_spec,
                  pl.BlockSpec((None, 8, LANES), lambda j, *_: (j, 0, 0))],
        out_specs=[pl.BlockSpec(memory_space=pl.ANY), row_spec],
        scratch_shapes=[pltpu.VMEM((2, rt, D_MODEL), MOE_DTYPE), pltpu.VMEM((MOE_BM, D_MODEL), MOE_DTYPE),
                        pltpu.SemaphoreType.DMA((4,))],
    )
    return pl.pallas_call(
        _dispatch_kernel,
        grid_spec=grid_spec,
        out_shape=[jax.ShapeDtypeStruct((n_rows, D_MODEL), MOE_DTYPE),
                   jax.ShapeDtypeStruct((rows, LANES), jnp.int32)],
        compiler_params=_params(("arbitrary",)),
        name="moe_dispatch",
    )(tables["base"], tables["loc"], tables["units"], tables["tile_units"], tables["tail_start"],
      tables["tail_units"], tables["n_used"], hn, ids, ranks, tables["loc_lanes"])


WEIGHT_DMA_PRIORITIES = (1, 0, 1)
WEIGHT_SLOTS = 3


def _expert_kernel(be_ref, slot_ref, nxt1_ref, nxt2_ref, nused_ref, x_ref, wg_hbm, wu_hbm, wd_hbm, o_ref,
                   wgf, wuf, wdf, wgb, wub, wdb, wsem, *, layer):
    i = pl.program_id(0)
    n_used = nused_ref[0]

    def weight_copies(expert, wslot):
        pairs = ((wg_hbm, wgf), (wu_hbm, wuf), (wd_hbm, wdf))
        return [pltpu.make_async_copy(w.at[layer, expert], buf.at[wslot], wsem.at[wslot]) for w, buf in pairs]

    def start_weights(expert, wslot):
        for c, priority in zip(weight_copies(expert, wslot), WEIGHT_DMA_PRIORITIES):
            c.start(priority=priority)

    @pl.when(i == 0)
    def _():
        start_weights(be_ref[0], 0)

        @pl.when(nxt1_ref[0] >= 0)
        def _():
            start_weights(nxt1_ref[0], 1)

    @pl.when(i < n_used)
    def _():
        @pl.when((i == 0) | (be_ref[i] != be_ref[jnp.maximum(i - 1, 0)]))
        def _():
            wslot = slot_ref[i]
            for c in weight_copies(be_ref[i], wslot):
                c.wait()

            @pl.when(nxt2_ref[i] >= 0)
            def _():
                start_weights(nxt2_ref[i], lax.rem(wslot + 2, WEIGHT_SLOTS))

            wgb[...] = wgf[wslot].astype(BF16)
            wub[...] = wuf[wslot].astype(BF16)
            wdb[...] = wdf[wslot].astype(BF16)

        x = x_ref[...]
        gate = jnp.dot(x, wgb[...], preferred_element_type=F32)
        up = jnp.dot(x, wub[...], preferred_element_type=F32)
        act = (jax.nn.silu(gate) * up).astype(BF16)
        o_ref[...] = jnp.dot(act, wdb[...], preferred_element_type=F32).astype(o_ref.dtype)

    @pl.when(i >= n_used)
    def _():
        o_ref[...] = jnp.zeros(o_ref.shape, o_ref.dtype)


def _experts(x_sorted, tables, wg, wu, wd, layer, n_blk):
    hbm = pl.BlockSpec(memory_space=pl.ANY)
    grid_spec = pltpu.PrefetchScalarGridSpec(
        num_scalar_prefetch=5,
        grid=(n_blk,),
        in_specs=[pl.BlockSpec((MOE_BM, D_MODEL), lambda i, be, slot, n1, n2, nu: (jnp.minimum(i, nu[0] - 1), 0)),
                  hbm, hbm, hbm],
        out_specs=pl.BlockSpec((MOE_BM, D_MODEL), lambda i, *_: (i, 0)),
        scratch_shapes=[
            pltpu.VMEM((WEIGHT_SLOTS, D_MODEL, D_EXPERT), F32),
            pltpu.VMEM((WEIGHT_SLOTS, D_MODEL, D_EXPERT), F32),
            pltpu.VMEM((WEIGHT_SLOTS, D_EXPERT, D_MODEL), F32),
            pltpu.VMEM((D_MODEL, D_EXPERT), BF16),
            pltpu.VMEM((D_MODEL, D_EXPERT), BF16),
            pltpu.VMEM((D_EXPERT, D_MODEL), BF16),
            pltpu.SemaphoreType.DMA((WEIGHT_SLOTS,)),
        ],
    )
    return pl.pallas_call(
        functools.partial(_expert_kernel, layer=layer),
        grid_spec=grid_spec,
        out_shape=jax.ShapeDtypeStruct((n_blk * MOE_BM, D_MODEL), MOE_DTYPE),
        compiler_params=_params(("arbitrary",)),
        name="moe_experts",
    )(tables["blk_expert"], tables["blk_slot"], tables["blk_next1"], tables["blk_next2"], tables["n_used"],
      x_sorted, wg, wu, wd)


def _combine_kernel(base_ref, loc_ref, units_ref, tile_units_ref, h_ref, wt_ref, lrow_ref, g_ref, y_hbm, o_ref,
                    ybuf, sem, *, final_norm):
    j = pl.program_id(0)
    n = pl.num_programs(0)
    slot = j % 2
    tt, rt = h_ref.shape[0], ybuf.shape[1]
    assert rt % LANES == 0

    @pl.when(j == 0)
    def _():
        ybuf[...] = jnp.zeros(ybuf.shape, ybuf.dtype)
        _segment_copies(0, y_hbm, ybuf.at[0], sem.at[0], False, base_ref, loc_ref, units_ref)

    @pl.when(j + 1 < n)
    def _():
        _segment_copies(j + 1, y_hbm, ybuf.at[1 - slot], sem.at[1 - slot], False, base_ref, loc_ref, units_ref)

    _wait_unit_copies(tile_units_ref[j], y_hbm, ybuf.at[slot], sem.at[slot])
    lrow = lrow_ref[...]
    wt = wt_ref[...]
    col = lax.broadcasted_iota(jnp.int32, (tt, rt), 1)
    is0 = col == lrow[:, 0:1]
    is1 = col == lrow[:, 1:2]
    w_of_row = jnp.sum(jnp.where(is0, wt[:, 0:1], 0.0) + jnp.where(is1, wt[:, 1:2], 0.0), axis=0, keepdims=True)
    w_rows = jnp.transpose(jnp.broadcast_to(w_of_row, (LANES, rt)))
    w_rows = jnp.tile(w_rows, (1, OUT_TN // LANES))
    pick = (is0 | is1).astype(BF16)
    for c in range(D_MODEL // OUT_TN):
        cols = slice(c * OUT_TN, (c + 1) * OUT_TN)
        y = (ybuf[slot, :, cols].astype(F32) * w_rows).astype(BF16)
        o_ref[:, cols] = h_ref[:, cols] + jnp.dot(pick, y, preferred_element_type=F32)
    if final_norm:
        o_ref[...] = _rms_norm(o_ref[...], g_ref[...])


def _combine(h, wts, lrows, y_sorted, tables, g, tt, final_norm):
    rows = h.shape[0]
    rt = _moe_tile_rows(tt)
    grid_spec = pltpu.PrefetchScalarGridSpec(
        num_scalar_prefetch=4,
        grid=(rows // tt,),
        in_specs=[
            pl.BlockSpec((tt, D_MODEL), lambda j, *_: (j, 0)),
            pl.BlockSpec((tt, LANES), lambda j, *_: (j, 0)),
            pl.BlockSpec((tt, LANES), lambda j, *_: (j, 0)),
            pl.BlockSpec((1, D_MODEL), lambda j, *_: (0, 0)),
            pl.BlockSpec(memory_space=pl.ANY),
        ],
        out_specs=pl.BlockSpec((tt, D_MODEL), lambda j, *_: (j, 0)),
        scratch_shapes=[pltpu.VMEM((2, rt, D_MODEL), MOE_DTYPE), pltpu.SemaphoreType.DMA((2,))],
    )
    return pl.pallas_call(
        functools.partial(_combine_kernel, final_norm=final_norm),
        grid_spec=grid_spec,
        out_shape=jax.ShapeDtypeStruct((rows, D_MODEL), F32),
        compiler_params=_params(("arbitrary",)),
        name="moe_combine",
    )(tables["base"], tables["loc"], tables["units"], tables["tile_units"], h, wts, lrows, g, y_sorted)


def _moe(h, hn, wts, ids, ranks, tile_cnt, wg, wu, wd, layer, final_g):
    rows = h.shape[0]
    n_tiles = tile_cnt.shape[0]
    tt = rows // n_tiles
    n_rows = 2 * rows + n_tiles * N_EXPERTS * (MOE_UNIT - 1) + N_EXPERTS * (MOE_BM - 1)
    n_blk = -(-n_rows // MOE_BM)
    tables = _moe_tables(tile_cnt, n_blk)
    x_sorted, lrows = _dispatch(hn, ids, ranks, tables, tt, n_blk * MOE_BM)
    y_sorted = _experts(x_sorted, tables, wg, wu, wd, layer, n_blk)
    final_norm = final_g is not None
    g = final_g if final_norm else jnp.ones((1, D_MODEL), F32)
    return _combine(h, wts, lrows, y_sorted, tables, g, tt, final_norm)


def _rope_tables(batch, lp):
    half = HEAD_DIM // 2
    inv = ROPE_THETA ** (-jnp.arange(half, dtype=F32) / half)
    pos = (jnp.arange(lp, dtype=jnp.int32) - PAD).astype(F32)
    ang = pos[:, None] * inv[None, :]
    cos = jnp.tile(jnp.cos(ang), (batch, LANES // half))
    sin = jnp.sin(ang)
    sin = jnp.tile(jnp.concatenate([-sin, sin], axis=1), (batch, LANES // HEAD_DIM))
    return cos, sin


def kernel(x, meta_tokens, attn_norm_g, ffn_norm_g, w_in, swa_sinks, ssm_lambda_re, ssm_lambda_im, ssm_b_re, ssm_b_im, ssm_c_re, ssm_c_im, ssm_d, ssm_log_dt, ssm_w_glu, diff_lambda_q1, diff_lambda_k1, diff_lambda_q2, diff_lambda_k2, diff_subln_g, w_out, router_group_w, router_group_b, router_expert_w, router_expert_b, moe_w_gate, moe_w_up, moe_w_down, final_norm_g):
    batch, seq, d = x.shape
    depth = w_in.shape[0]
    lp = PAD + N_META + seq
    nb = lp // BLK
    meta = jnp.broadcast_to(meta_tokens[None].astype(x.dtype), (batch, N_META, d))
    h = jnp.concatenate([jnp.zeros((batch, PAD, d), x.dtype), meta, x], axis=1).reshape(batch * lp, d)
    cos, sin = _rope_tables(batch, lp)
    w_in_bf, w_out_bf, w_glu_bf = w_in.astype(BF16), w_out.astype(BF16), ssm_w_glu.astype(BF16)
    for l in range(depth):
        att, us, qd, kd, vd = _norm_inproj(h, attn_norm_g[l][None], w_in_bf, l, cos, sin)
        o_a = _swa(att, swa_sinks[l].astype(F32), batch, nb)
        kc, bc, cc, a_t = _ssm_operators(ssm_lambda_re[l], ssm_lambda_im[l], ssm_b_re[l], ssm_b_im[l],
                                         ssm_c_re[l], ssm_c_im[l], ssm_log_dt[l])
        yg = _ssm(us, kc, bc, cc, a_t, ssm_d[l].astype(F32).reshape(SSM_PACKS, 1, LANES), batch, lp)
        lam_init = 0.8 - 0.6 * math.exp(-0.3 * l)
        lam_vecs = jnp.stack([diff_lambda_q1[l], diff_lambda_k1[l], diff_lambda_q2[l], diff_lambda_k2[l]]).astype(F32)
        o_c = _diff_attn(qd, kd, vd, lam_vecs, diff_subln_g[l][None].astype(F32), lam_init, batch, nb)
        rw = jnp.concatenate([router_group_w[l], router_expert_w[l]], axis=1).astype(F32)
        rw = jnp.pad(rw, ((0, 0), (0, LANES - rw.shape[1])))
        rb = jnp.concatenate([router_group_b[l], router_expert_b[l]]).astype(F32)
        rb = jnp.pad(rb, (0, LANES - rb.shape[0]))[None]
        rw_hi, rw_lo = _split_bf16(rw)
        g_ffn = ffn_norm_g[l][None].astype(F32)
        h, hn, wts, ids, ranks, tile_cnt = _mixer_out(h, o_a, yg, o_c, w_out_bf, w_glu_bf, l, g_ffn, rw_hi, rw_lo, rb)
        final_g = final_norm_g[None].astype(F32) if l == depth - 1 else None
        h = _moe(h, hn, wts, ids, ranks, tile_cnt, moe_w_gate, moe_w_up, moe_w_down, l, final_g)
    return h.reshape(batch, lp, d)[:, PAD + N_META:]
```

```python
import functools
import math

import jax
import jax.numpy as jnp
from jax import lax
from jax.experimental import pallas as pl
from jax.experimental.pallas import tpu as pltpu

D_MODEL = 2048
N_META = 16
BLK = 128
PAD = BLK - N_META
ROPE_THETA = 10000.0
NORM_EPS = 1e-5
NEG = -1e30
SWA_HEADS = 12
SWA_KV_HEADS = 4
HEAD_DIM = 64
SWA_WIDTH = SWA_HEADS * HEAD_DIM
SSM_WIDTH = 768
SSM_GROUP = 16
SSM_GROUPS = SSM_WIDTH // SSM_GROUP
SSM_STATE = 64
DIFF_HEADS = 4
DIFF_V_DIM = 128
DIFF_WIDTH = DIFF_HEADS * DIFF_V_DIM
IN_WIDTH = 3584
N_EXPERT_GROUPS = 4
EXPERTS_PER_GROUP = 8
N_EXPERTS = N_EXPERT_GROUPS * EXPERTS_PER_GROUP
D_EXPERT = 512

COL_KA = 768
COL_VA = 1024
COL_US = 1280
COL_QD = 2048
COL_KD = 2560
COL_VD = 3072

LANES = 128
SSM_T = 16
SSM_PACK = LANES // SSM_GROUP
SSM_PACKS = SSM_GROUPS // SSM_PACK
SSM_PSTATE = SSM_PACK * SSM_STATE
MOE_BM = 128
VMEM_LIMIT = 56 * 1024 * 1024

F32 = jnp.float32
BF16 = jnp.bfloat16


def _row_tile(rows):
    for t in (512, 256, 128):
        if rows % t == 0:
            return t
    raise ValueError(f"row count {rows} is not a multiple of 128")


def _params(sem, vmem=VMEM_LIMIT):
    return pltpu.CompilerParams(dimension_semantics=sem, vmem_limit_bytes=vmem)


def _rms_norm(x, g):
    return x * lax.rsqrt(jnp.mean(x * x, axis=-1, keepdims=True) + NORM_EPS) * g


IN_TN = 512
ROPE_TILES = (0, 1, 4, 5)


def _norm_inproj_kernel(x_ref, g_ref, w_ref, cos_ref, sin_ref, att_ref, us_ref, qd_ref, kd_ref, vd_ref, xn_ref):
    blocked = ((COL_VD, vd_ref), (COL_KD, kd_ref), (COL_QD, qd_ref), (COL_US, us_ref))

    def store(col, piece):
        for start, ref in blocked:
            if col >= start:
                ref[(col - start) // LANES] = piece
                return
        att_ref[:, col:col + LANES] = piece

    x = x_ref[...]
    ms = jnp.mean(x * x, axis=-1, keepdims=True)
    xn_ref[...] = (x * lax.rsqrt(ms + NORM_EPS) * g_ref[...]).astype(BF16)
    shape = (x.shape[0], IN_TN)
    lane = lax.broadcasted_iota(jnp.int32, shape, 1)
    first_half = (lane & (HEAD_DIM - 1)) < (HEAD_DIM // 2)
    reps = IN_TN // LANES
    cos = jnp.tile(cos_ref[...], (1, reps))
    sin = jnp.tile(sin_ref[...], (1, reps))
    for j in range(IN_WIDTH // IN_TN):
        cols = slice(j * IN_TN, (j + 1) * IN_TN)
        acc = jnp.dot(xn_ref[...], w_ref[:, cols], preferred_element_type=F32)
        if j in ROPE_TILES:
            partner = jnp.where(first_half, pltpu.roll(acc, IN_TN - HEAD_DIM // 2, 1),
                                pltpu.roll(acc, HEAD_DIM // 2, 1))
            acc = acc * cos + partner * sin
        acc = acc.astype(BF16)
        for c in range(IN_TN // LANES):
            store(j * IN_TN + c * LANES, acc[:, c * LANES:(c + 1) * LANES])


def _norm_inproj(h, g, w, layer, cos, sin):
    rows = h.shape[0]
    tm = _row_tile(rows)

    def blocked(width):
        n = width // LANES
        return pl.BlockSpec((n, tm, LANES), lambda i: (0, i, 0)), jax.ShapeDtypeStruct((n, rows, LANES), BF16)

    specs, shapes = zip(
        (pl.BlockSpec((tm, COL_US), lambda i: (i, 0)), jax.ShapeDtypeStruct((rows, COL_US), BF16)),
        blocked(SSM_WIDTH), blocked(DIFF_WIDTH), blocked(DIFF_WIDTH), blocked(DIFF_WIDTH))
    return pl.pallas_call(
        _norm_inproj_kernel,
        grid=(rows // tm,),
        in_specs=[
            pl.BlockSpec((tm, D_MODEL), lambda i: (i, 0)),
            pl.BlockSpec((1, D_MODEL), lambda i: (0, 0)),
            pl.BlockSpec((None, D_MODEL, IN_WIDTH), lambda i: (layer, 0, 0)),
            pl.BlockSpec((tm, LANES), lambda i: (i, 0)),
            pl.BlockSpec((tm, LANES), lambda i: (i, 0)),
        ],
        out_specs=list(specs),
        out_shape=list(shapes),
        scratch_shapes=[pltpu.VMEM((tm, D_MODEL), BF16)],
        compiler_params=_params(("parallel",)),
        name="norm_inproj",
    )(h, g, w, cos, sin)


def _swa_kernel(sink_ref, q_ref, kp_ref, kc_ref, vp_ref, vc_ref, o_ref):
    n = pl.program_id(1)
    shape = (BLK, 2 * BLK)
    qi = lax.broadcasted_iota(jnp.int32, shape, 0)
    c = lax.broadcasted_iota(jnp.int32, shape, 1)
    kj = (n - 1) * BLK + c
    delta = qi + BLK - c
    ok = (delta >= 0) & (delta < BLK) & (kj >= PAD)
    grp = SWA_HEADS // SWA_KV_HEADS
    outs = []
    for j in range(SWA_KV_HEADS):
        cols = slice(j * HEAD_DIM, (j + 1) * HEAD_DIM)
        kk = jnp.concatenate([kp_ref[:, cols], kc_ref[:, cols]], axis=0)
        vv = jnp.concatenate([vp_ref[:, cols], vc_ref[:, cols]], axis=0)
        for g in range(grp):
            hd = j * grp + g
            qh = q_ref[:, hd * HEAD_DIM:(hd + 1) * HEAD_DIM] * (HEAD_DIM ** -0.5)
            s = lax.dot_general(qh, kk, (((1,), (1,)), ((), ())), preferred_element_type=F32)
            s = jnp.where(ok, s, NEG)
            sink = sink_ref[hd]
            m = jnp.maximum(jnp.max(s, axis=-1, keepdims=True), sink)
            p = jnp.exp(s - m)
            denom = jnp.sum(p, axis=-1, keepdims=True) + jnp.exp(sink - m)
            outs.append(jnp.dot(p.astype(BF16), vv, preferred_element_type=F32) * (1.0 / denom))
    o_ref[...] = jnp.concatenate(outs, axis=1).astype(o_ref.dtype)


def _swa(proj, sinks, batch, nb):
    rows = proj.shape[0]
    kvw = SWA_KV_HEADS * HEAD_DIM

    def cur(col):
        return lambda b, n: (b * nb + n, col)

    def prev(col):
        return lambda b, n: (b * nb + jnp.maximum(n - 1, 0), col)

    return pl.pallas_call(
        _swa_kernel,
        grid=(batch, nb),
        in_specs=[
            pl.BlockSpec(memory_space=pltpu.SMEM),
            pl.BlockSpec((BLK, SWA_WIDTH), cur(0)),
            pl.BlockSpec((BLK, kvw), prev(COL_KA // kvw)),
            pl.BlockSpec((BLK, kvw), cur(COL_KA // kvw)),
            pl.BlockSpec((BLK, kvw), prev(COL_VA // kvw)),
            pl.BlockSpec((BLK, kvw), cur(COL_VA // kvw)),
        ],
        out_specs=pl.BlockSpec((BLK, SWA_WIDTH), cur(0)),
        out_shape=jax.ShapeDtypeStruct((rows, SWA_WIDTH), BF16),
        compiler_params=_params(("parallel", "parallel")),
        name="swa",
    )(sinks, proj, proj, proj, proj, proj)


DIFF_QBLOCKS = 2


def _diff_kernel(lam_ref, g_ref, q_ref, k_ref, v_ref, o_ref, *, lam_init, nb):
    lane = lax.broadcasted_iota(jnp.int32, (BLK, LANES), 1)
    lv = lam_ref[...]
    lam = (jnp.exp(jnp.sum(lv[0:1] * lv[1:2], axis=-1, keepdims=True))
           - jnp.exp(jnp.sum(lv[2:3] * lv[3:4], axis=-1, keepdims=True)) + lam_init)
    gain = g_ref[...] * (1.0 - lam_init)
    scale = HEAD_DIM ** -0.5
    for n0 in range(0, nb, DIFF_QBLOCKS):
        blocks = range(n0, min(n0 + DIFF_QBLOCKS, nb))
        ke = (blocks[-1] + 1) * BLK
        kx = k_ref[0:ke, :]
        vx = v_ref[0:ke, :]
        rows = 2 * BLK * len(blocks)
        pieces = []
        for n in blocks:
            q = q_ref[n * BLK:(n + 1) * BLK, :] * scale
            zero = jnp.zeros_like(q)
            pieces += [jnp.where(lane < HEAD_DIM, q, zero), jnp.where(lane >= HEAD_DIM, q, zero)]
        s = lax.dot_general(jnp.concatenate(pieces, axis=0), kx, (((1,), (1,)), ((), ())),
                            preferred_element_type=F32)
        r = lax.broadcasted_iota(jnp.int32, (rows, BLK), 0)
        qpos = (n0 + jnp.right_shift(r, (2 * BLK).bit_length() - 1)) * BLK + (r & (BLK - 1))
        col = lax.broadcasted_iota(jnp.int32, (rows, BLK), 1)
        key_blocks = []
        for t in range(blocks[-1] + 1):
            blk = s[:, t * BLK:(t + 1) * BLK]
            ok = None
            if t >= n0:
                ok = t * BLK + col <= qpos
            if t == 0:
                ok = (col >= PAD) if ok is None else ok & (col >= PAD)
            key_blocks.append(blk if ok is None else jnp.where(ok, blk, NEG))
        s = jnp.concatenate(key_blocks, axis=1)
        m = jnp.max(s, axis=-1, keepdims=True)
        p = jnp.exp(s - m)
        denom = jnp.sum(p, axis=-1, keepdims=True)
        heads = jnp.dot(p.astype(BF16), vx, preferred_element_type=F32) * (1.0 / denom)
        for i, n in enumerate(blocks):
            o = heads[2 * i * BLK:(2 * i + 1) * BLK] - lam * heads[(2 * i + 1) * BLK:(2 * i + 2) * BLK]
            o = o * lax.rsqrt(jnp.mean(o * o, axis=-1, keepdims=True) + NORM_EPS) * gain
            o_ref[n * BLK:(n + 1) * BLK, :] = o.astype(o_ref.dtype)


def _diff_attn(qd, kd, vd, lam_vecs, subln_g, lam_init, batch, nb):
    rows = qd.shape[1]
    lp = nb * BLK
    head_rows = pl.BlockSpec((None, lp, LANES), lambda b, h: (h, b, 0))
    return pl.pallas_call(
        functools.partial(_diff_kernel, lam_init=lam_init, nb=nb),
        grid=(batch, DIFF_HEADS),
        in_specs=[
            pl.BlockSpec((4, HEAD_DIM), lambda b, h: (0, 0)),
            pl.BlockSpec((1, DIFF_V_DIM), lambda b, h: (0, 0)),
            head_rows, head_rows, head_rows,
        ],
        out_specs=head_rows,
        out_shape=jax.ShapeDtypeStruct((DIFF_HEADS, rows, DIFF_V_DIM), BF16),
        compiler_params=_params(("parallel", "parallel")),
        name="diff_attn",
    )(lam_vecs, subln_g, qd, kd, vd)


def _ssm_row_chunk(rows):
    for c in (544, 512, 384, 256, 128, 64, 32, 16):
        if rows % c == 0:
            return c
    raise ValueError(rows)


def _ssm_expand_operators(kc_ref, bc_ref, cc_ref, k_ref, bw_ref, cw_ref):
    def iota(shape, dim):
        return lax.broadcasted_iota(jnp.int32, shape, dim)

    def group(idx, width):
        return jnp.right_shift(idx, width.bit_length() - 1) & (SSM_PACK - 1)

    hs, ps, tl = SSM_GROUP, SSM_STATE, SSM_T * LANES
    spread_h = ((iota((LANES, hs), 0) & (hs - 1)) == iota((LANES, hs), 1)).astype(BF16)
    kt = jnp.dot(spread_h, kc_ref[...], preferred_element_type=F32)
    same = group(iota((LANES, tl), 0), hs) == group(iota((LANES, tl), 1), hs)
    k_ref[...] = jnp.where(same, kt, 0.0).astype(BF16)
    spread_s = (iota((ps, SSM_PSTATE), 0) == (iota((ps, SSM_PSTATE), 1) & (ps - 1))).astype(BF16)
    shape = (tl, SSM_PSTATE)
    same = group(iota(shape, 0), hs) == group(iota(shape, 1), ps)
    for part in range(2):
        bw = jnp.dot(bc_ref[part], spread_s, preferred_element_type=F32)
        bw_ref[:, part * SSM_PSTATE:(part + 1) * SSM_PSTATE] = jnp.where(same, bw, 0.0).astype(BF16)
    shape = (LANES, SSM_PSTATE)
    same = group(iota(shape, 0), hs) == group(iota(shape, 1), ps)
    for j in range(SSM_T):
        for part in range(2):
            ct = jnp.dot(spread_h, cc_ref[part, j], preferred_element_type=F32)
            cw_ref[j, :, part * SSM_PSTATE:(part + 1) * SSM_PSTATE] = jnp.where(same, ct, 0.0).astype(BF16)


def _ssm_kernel(u_ref, kc_ref, bc_ref, cc_ref, a_ref, d_ref, o_ref, up_ref, y_ref, e_ref, s_ref, k_ref, bw_ref,
                cw_ref, *, batch, lp):
    rows = batch * lp
    n_chunks = lp // SSM_T
    tot_chunks = rows // SSM_T
    _ssm_expand_operators(kc_ref, bc_ref, cc_ref, k_ref, bw_ref, cw_ref)
    contract_lanes = (((1,), (1,)), ((), ()))
    up_ref[0:SSM_T, :] = jnp.zeros((SSM_T, LANES), F32)
    up_ref[SSM_T:, :] = u_ref[...].astype(F32)
    for b in range(batch):
        up_ref[SSM_T + b * lp:SSM_T + b * lp + PAD, :] = jnp.zeros((PAD, LANES), F32)

    ch = _ssm_row_chunk(rows)
    rmod = lax.broadcasted_iota(jnp.int32, (ch, LANES), 0) & (SSM_T - 1)
    for r0 in range(0, rows, ch):
        pieces = []
        for k in range(SSM_T):
            sl = up_ref[SSM_T - k + r0:SSM_T - k + r0 + ch, :]
            pieces.append(jnp.where(rmod >= k, sl, 0.0).astype(BF16))
        y_ref[r0:r0 + ch, :] = lax.dot_general(jnp.concatenate(pieces, axis=1), k_ref[...], contract_lanes,
                                               preferred_element_type=F32)

    pieces = [up_ref[pl.ds(SSM_T + i, tot_chunks, stride=SSM_T), :].astype(BF16) for i in range(SSM_T)]
    e_ref[...] = jnp.dot(jnp.concatenate(pieces, axis=1), bw_ref[...], preferred_element_type=F32)

    a_re = a_ref[:, :SSM_PSTATE]
    a_im = a_ref[:, SSM_PSTATE:]

    def step(c, carry):
        new = []
        for b in range(batch):
            sr, si = carry[b]
            idx = b * n_chunks + c
            s_ref[pl.ds(idx, 1), :SSM_PSTATE] = sr
            s_ref[pl.ds(idx, 1), SSM_PSTATE:] = si
            er = e_ref[pl.ds(idx, 1), :SSM_PSTATE]
            ei = e_ref[pl.ds(idx, 1), SSM_PSTATE:]
            new.append((a_re * sr - a_im * si + er, a_re * si + a_im * sr + ei))
        return tuple(new)

    zero = jnp.zeros((1, SSM_PSTATE), F32)
    lax.fori_loop(0, n_chunks, step, tuple((zero, zero) for _ in range(batch)))

    sb = s_ref[...].astype(BF16)
    for j in range(SSM_T):
        z = lax.dot_general(sb, cw_ref[j], contract_lanes, preferred_element_type=F32)
        idx = pl.ds(j, tot_chunks, stride=SSM_T)
        y_ref[idx, :] = y_ref[idx, :] + z

    y = y_ref[...] + d_ref[...] * up_ref[SSM_T:, :]
    o_ref[...] = jax.nn.gelu(y, approximate=True).astype(o_ref.dtype)


def _ssm(us, kc, bc, cc, a_t, dskip, batch, lp):
    rows = us.shape[1]
    tot_chunks = rows // SSM_T
    return pl.pallas_call(
        functools.partial(_ssm_kernel, batch=batch, lp=lp),
        grid=(SSM_PACKS,),
        in_specs=[
            pl.BlockSpec((None, rows, LANES), lambda p: (p, 0, 0)),
            pl.BlockSpec((None, SSM_GROUP, SSM_T * LANES), lambda p: (p, 0, 0)),
            pl.BlockSpec((None, 2, SSM_T * LANES, SSM_STATE), lambda p: (p, 0, 0, 0)),
            pl.BlockSpec((None, 2, SSM_T, SSM_GROUP, SSM_PSTATE), lambda p: (p, 0, 0, 0, 0)),
            pl.BlockSpec((None, 1, 2 * SSM_PSTATE), lambda p: (p, 0, 0)),
            pl.BlockSpec((None, 1, LANES), lambda p: (p, 0, 0)),
        ],
        out_specs=pl.BlockSpec((None, rows, LANES), lambda p: (p, 0, 0)),
        out_shape=jax.ShapeDtypeStruct((SSM_PACKS, rows, LANES), F32),
        scratch_shapes=[
            pltpu.VMEM((rows + SSM_T, LANES), F32),
            pltpu.VMEM((rows, LANES), F32),
            pltpu.VMEM((tot_chunks, 2 * SSM_PSTATE), F32),
            pltpu.VMEM((tot_chunks, 2 * SSM_PSTATE), F32),
            pltpu.VMEM((LANES, SSM_T * LANES), BF16),
            pltpu.VMEM((SSM_T * LANES, 2 * SSM_PSTATE), BF16),
            pltpu.VMEM((SSM_T, LANES, 2 * SSM_PSTATE), BF16),
        ],
        compiler_params=_params(("parallel",)),
        name="ssm",
    )(us, kc, bc, cc, a_t, dskip)


def _ssm_operators(lam_re, lam_im, b_re, b_im, c_re, c_im, log_dt):
    p, h = SSM_STATE, SSM_GROUP
    dt = jnp.exp(log_dt.astype(F32))[:, None]
    lr, li = lam_re.astype(F32), lam_im.astype(F32)
    mag = jnp.exp(lr * dt)
    abar_re, abar_im = mag * jnp.cos(li * dt), mag * jnp.sin(li * dt)
    den = lr * lr + li * li
    nr = abar_re - 1.0
    f_re = (nr * lr + abar_im * li) / den
    f_im = (abar_im * lr - nr * li) / den
    br, bi = b_re.astype(F32), b_im.astype(F32)
    bbar_re = f_re[..., None] * br - f_im[..., None] * bi
    bbar_im = f_re[..., None] * bi + f_im[..., None] * br
    ks = jnp.arange(SSM_T + 1, dtype=F32)[:, None, None]
    pmag = jnp.exp(ks * (lr * dt)[None])
    pw_re = pmag * jnp.cos(ks * (li * dt)[None])
    pw_im = pmag * jnp.sin(ks * (li * dt)[None])
    cr, ci = c_re.astype(F32), c_im.astype(F32)

    packs, q = SSM_PACKS, SSM_PACK

    cb_re = cr[:, :, :, None] * bbar_re[:, None, :, :] - ci[:, :, :, None] * bbar_im[:, None, :, :]
    cb_im = cr[:, :, :, None] * bbar_im[:, None, :, :] + ci[:, :, :, None] * bbar_re[:, None, :, :]
    klag = (jnp.einsum('kgs,gosi->kgoi', pw_re[:SSM_T], cb_re)
            - jnp.einsum('kgs,gosi->kgoi', pw_im[:SSM_T], cb_im))
    kc = klag.reshape(SSM_T, packs, q, h, h).transpose(1, 3, 0, 2, 4).reshape(packs, h, SSM_T * LANES)

    rev_re, rev_im = pw_re[SSM_T - 1::-1, :, None, :], pw_im[SSM_T - 1::-1, :, None, :]
    bt_re, bt_im = jnp.swapaxes(bbar_re, 1, 2)[None], jnp.swapaxes(bbar_im, 1, 2)[None]

    def inj_by_pack(x):
        return x.reshape(SSM_T, packs, q * h, p).transpose(1, 0, 2, 3).reshape(packs, SSM_T * LANES, p)

    bc = jnp.stack([inj_by_pack(rev_re * bt_re - rev_im * bt_im), inj_by_pack(rev_re * bt_im + rev_im * bt_re)], axis=1)

    ca_re = cr[None] * pw_re[1:, :, None, :] - ci[None] * pw_im[1:, :, None, :]
    ca_im = cr[None] * pw_im[1:, :, None, :] + ci[None] * pw_re[1:, :, None, :]

    def out_by_pack(x):
        return x.reshape(SSM_T, packs, q, h, p).transpose(1, 0, 3, 2, 4).reshape(packs, SSM_T, h, SSM_PSTATE)

    cc = jnp.stack([out_by_pack(ca_re), out_by_pack(-ca_im)], axis=1)

    a_t = jnp.concatenate([pw_re[SSM_T].reshape(packs, 1, SSM_PSTATE),
                           pw_im[SSM_T].reshape(packs, 1, SSM_PSTATE)], axis=-1)
    return kc.astype(BF16), bc.astype(BF16), cc.astype(BF16), a_t


OUT_TN = 512


def _split_bf16(x):
    hi = x.astype(BF16)
    return hi, (x - hi.astype(F32)).astype(BF16)


def _route(hn, rwh_ref, rwl_ref, rb_ref):
    hi, lo = _split_bf16(hn)
    logits = (jnp.dot(hi, rwh_ref[...], preferred_element_type=F32)
              + (jnp.dot(lo, rwh_ref[...], preferred_element_type=F32)
                 + jnp.dot(hi, rwl_ref[...], preferred_element_type=F32))) + rb_ref[...]
    lane = lax.broadcasted_iota(jnp.int32, logits.shape, 1)

    def first_argmax(v, vmax):
        return jnp.min(jnp.where(v == vmax, lane, LANES), axis=-1, keepdims=True)

    gl = jnp.where(lane < N_EXPERT_GROUPS, logits, NEG)
    gmax = jnp.max(gl, axis=-1, keepdims=True)
    g_sel = first_argmax(gl, gmax)
    p_g = 1.0 / jnp.sum(jnp.exp(gl - gmax), axis=-1, keepdims=True)
    e_lane = lane - N_EXPERT_GROUPS
    in_group = (e_lane >= 0) & (e_lane < N_EXPERTS) & (jnp.right_shift(e_lane, 3) == g_sel)
    el = jnp.where(in_group, logits, NEG)
    m1 = jnp.max(el, axis=-1, keepdims=True)
    i1 = first_argmax(el, m1)
    el2 = jnp.where(lane == i1, NEG, el)
    m2 = jnp.max(el2, axis=-1, keepdims=True)
    i2 = first_argmax(el2, m2)
    z = jnp.sum(jnp.exp(el - m1), axis=-1, keepdims=True)
    p1 = 1.0 / z
    p2 = jnp.exp(m2 - m1) / z
    w1 = p1 / (p1 + p2) * p_g
    w2 = p2 / (p1 + p2) * p_g
    wts = jnp.where(lane == 0, w1, jnp.where(lane == 1, w2, 0.0))
    ids = jnp.where(lane == 0, i1 - N_EXPERT_GROUPS, jnp.where(lane == 1, i2 - N_EXPERT_GROUPS, 0))
    return wts, ids


def _assignment_ranks(ids):
    tm = ids.shape[0]
    lane = lax.broadcasted_iota(jnp.int32, ids.shape, 1)
    oh0 = lane == ids[:, 0:1]
    oh1 = lane == ids[:, 1:2]
    hits = (oh0 | oh1).astype(BF16)
    r = lax.broadcasted_iota(jnp.int32, (tm, tm), 0)
    c = lax.broadcasted_iota(jnp.int32, (tm, tm), 1)
    before = jnp.dot((c < r).astype(BF16), hits, preferred_element_type=F32)
    rank0 = jnp.sum(jnp.where(oh0, before, 0.0), axis=-1, keepdims=True)
    rank1 = jnp.sum(jnp.where(oh1, before, 0.0), axis=-1, keepdims=True)
    ranks = jnp.where(lane == 0, rank0, jnp.where(lane == 1, rank1, 0.0)).astype(jnp.int32)
    return ranks, jnp.sum(hits.astype(F32), axis=0, keepdims=True).astype(jnp.int32)


def _mixer_out_kernel(h_ref, a_ref, y_ref, c_ref, wo_ref, wglu_ref, g_ref, rwh_ref, rwl_ref, rb_ref,
                      o_ref, hn_ref, wt_ref, id_ref, rank_ref, cnt_ref):
    half = h_ref.shape[0] // 2
    halves = [slice(r * half, (r + 1) * half) for r in range(2)]
    for rows in halves:
        y = jnp.concatenate([y_ref[p, rows, :] for p in range(y_ref.shape[0])], axis=1)
        c = jnp.concatenate([c_ref[hd, rows, :] for hd in range(c_ref.shape[0])], axis=1)
        gate = jnp.dot(y.astype(BF16), wglu_ref[...], preferred_element_type=F32)
        b = (y * jax.nn.sigmoid(gate)).astype(BF16)
        mixed = jnp.concatenate([a_ref[rows, :], b, c], axis=1)
        for j in range(D_MODEL // OUT_TN):
            cols = slice(j * OUT_TN, (j + 1) * OUT_TN)
            o_ref[rows, cols] = h_ref[rows, cols] + jnp.dot(mixed, wo_ref[:, cols], preferred_element_type=F32)
    for rows in halves:
        hn = _rms_norm(o_ref[rows, :], g_ref[...])
        hn_ref[rows, :] = hn.astype(BF16)
        wts, ids = _route(hn, rwh_ref, rwl_ref, rb_ref)
        wt_ref[rows, :] = wts
        id_ref[rows, :] = ids
    ranks, counts = _assignment_ranks(id_ref[...])
    rank_ref[...] = ranks
    cnt_ref[...] = jnp.broadcast_to(counts, cnt_ref.shape)


def _mixer_out(h, oa, yg, oc, wo, wglu, layer, g, rw_hi, rw_lo, rb):
    rows = h.shape[0]
    tm = _row_tile(rows)

    def layer_weight(shape):
        return pl.BlockSpec((None,) + shape, lambda i: (layer, 0, 0))

    def row_spec(width):
        return pl.BlockSpec((tm, width), lambda i: (i, 0))

    def whole(shape):
        return pl.BlockSpec(shape, lambda i: (0, 0))

    def blocked(width):
        return pl.BlockSpec((width // LANES, tm, LANES), lambda i: (0, i, 0))

    return pl.pallas_call(
        _mixer_out_kernel,
        grid=(rows // tm,),
        in_specs=[
            row_spec(D_MODEL), row_spec(SWA_WIDTH), blocked(SSM_WIDTH), blocked(DIFF_WIDTH),
            layer_weight((D_MODEL, D_MODEL)), layer_weight((SSM_WIDTH, SSM_WIDTH)), whole((1, D_MODEL)),
            whole((D_MODEL, LANES)), whole((D_MODEL, LANES)), whole((1, LANES)),
        ],
        out_specs=[row_spec(D_MODEL), row_spec(D_MODEL), row_spec(LANES), row_spec(LANES), row_spec(LANES),
                   pl.BlockSpec((None, 8, LANES), lambda i: (i, 0, 0))],
        out_shape=[
            jax.ShapeDtypeStruct((rows, D_MODEL), F32),
            jax.ShapeDtypeStruct((rows, D_MODEL), BF16),
            jax.ShapeDtypeStruct((rows, LANES), F32),
            jax.ShapeDtypeStruct((rows, LANES), jnp.int32),
            jax.ShapeDtypeStruct((rows, LANES), jnp.int32),
            jax.ShapeDtypeStruct((rows // tm, 8, LANES), jnp.int32),
        ],
        compiler_params=_params(("parallel",)),
        name="mixer_out",
    )(h, oa, yg, oc, wo, wglu, g, rw_hi, rw_lo, rb)


MOE_UNIT = 16
MOE_DTYPE = BF16


def _moe_tile_rows(tt):
    return 2 * tt + N_EXPERTS * MOE_UNIT


def _moe_tables(tile_cnt, n_blk):
    cnt = tile_cnt[:, 0, :N_EXPERTS]
    seg = (cnt + MOE_UNIT - 1) // MOE_UNIT * MOE_UNIT
    size = jnp.sum(seg, axis=0)
    padded = (size + MOE_BM - 1) // MOE_BM * MOE_BM
    pend = jnp.cumsum(padded)
    pstart = pend - padded
    base = pstart[None, :] + jnp.cumsum(seg, axis=0) - seg
    loc = jnp.cumsum(seg, axis=1) - seg
    units = seg // MOE_UNIT
    blk_start = jnp.arange(n_blk, dtype=jnp.int32) * MOE_BM
    blk_expert = jnp.minimum(jnp.sum((pend[None, :] <= blk_start[:, None]).astype(jnp.int32), axis=1), N_EXPERTS - 1)
    ex = jnp.arange(N_EXPERTS, dtype=jnp.int32)
    nonempty = size > 0
    run_slot = (jnp.cumsum(nonempty.astype(jnp.int32)) - 1) % WEIGHT_SLOTS
    later = (ex[None, :] > ex[:, None]) & nonempty[None, :]
    nxt = jnp.min(jnp.where(later, ex[None, :], N_EXPERTS), axis=1)
    nxt = jnp.where(nxt == N_EXPERTS, -1, nxt)
    nxt2 = jnp.where(nxt >= 0, nxt[jnp.maximum(nxt, 0)], -1)
    i32 = jnp.int32
    loc_lanes = jnp.pad(loc, ((0, 0), (0, LANES - N_EXPERTS)))
    return dict(
        base=base.reshape(-1).astype(i32), loc=loc.reshape(-1).astype(i32), units=units.reshape(-1).astype(i32),
        tile_units=jnp.sum(units, axis=1).astype(i32),
        tail_start=(pstart + size).astype(i32), tail_units=((padded - size) // MOE_UNIT).astype(i32),
        loc_lanes=jnp.broadcast_to(loc_lanes[:, None, :], (loc.shape[0], 8, LANES)).astype(i32),
        blk_expert=blk_expert.astype(i32), blk_slot=run_slot[blk_expert].astype(i32),
        blk_next1=nxt[blk_expert].astype(i32), blk_next2=nxt2[blk_expert].astype(i32),
        n_used=(pend[-1] // MOE_BM).astype(i32).reshape(1))


def _segment_copies(tile, hbm, buf, sem, to_hbm, base_ref, loc_ref, units_ref):
    for e in range(N_EXPERTS):
        k = tile * N_EXPERTS + e
        row_hbm, row_buf = base_ref[k], loc_ref[k]

        def start(u, carry):
            a = hbm.at[pl.ds(pl.multiple_of(row_hbm + u * MOE_UNIT, MOE_UNIT), MOE_UNIT), :]
            b = buf.at[pl.ds(pl.multiple_of(row_buf + u * MOE_UNIT, MOE_UNIT), MOE_UNIT), :]
            (pltpu.make_async_copy(b, a, sem) if to_hbm else pltpu.make_async_copy(a, b, sem)).start()
            return carry

        lax.fori_loop(0, units_ref[k], start, 0)


def _wait_unit_copies(count, hbm, buf, sem):
    def wait(u, carry):
        pltpu.make_async_copy(hbm.at[pl.ds(0, MOE_UNIT), :], buf.at[pl.ds(0, MOE_UNIT), :], sem).wait()
        return carry

    lax.fori_loop(0, count, wait, 0)


def _local_rows(ids, ranks, loc_lanes):
    lane = lax.broadcasted_iota(jnp.int32, ids.shape, 1)
    loc = loc_lanes.astype(F32)
    start0 = jnp.sum(jnp.where(lane == ids[:, 0:1], loc, 0.0), axis=-1, keepdims=True).astype(jnp.int32)
    start1 = jnp.sum(jnp.where(lane == ids[:, 1:2], loc, 0.0), axis=-1, keepdims=True).astype(jnp.int32)
    return start0 + ranks[:, 0:1], start1 + ranks[:, 1:2]


def _dispatch_kernel(base_ref, loc_ref, units_ref, tile_units_ref, tail_start_ref, tail_units_ref, nused_ref,
                     hn_ref, ids_ref, rank_ref, locv_ref, x_hbm, lrow_ref, ybuf, zbuf, sem):
    j = pl.program_id(0)
    n = pl.num_programs(0)
    slot = j % 2
    tt, rt = hn_ref.shape[0], ybuf.shape[1]

    @pl.when(j >= 2)
    def _():
        _wait_unit_copies(tile_units_ref[j - 2], x_hbm, ybuf.at[slot], sem.at[slot])

    row0, row1 = _local_rows(ids_ref[...], rank_ref[...], locv_ref[0:1, :])
    lane = lax.broadcasted_iota(jnp.int32, lrow_ref.shape, 1)
    lrow_ref[...] = jnp.where(lane == 0, row0, jnp.where(lane == 1, row1, 0))
    col = lax.broadcasted_iota(jnp.int32, (tt, rt), 1)
    onehot = ((col == row0) | (col == row1)).astype(BF16)
    compacted = lax.dot_general(onehot, hn_ref[...], (((0,), (0,)), ((), ())), preferred_element_type=F32)
    ybuf[slot] = compacted.astype(ybuf.dtype)
    _segment_copies(j, x_hbm, ybuf.at[slot], sem.at[slot], True, base_ref, loc_ref, units_ref)

    @pl.when(j == n - 1)
    def _():
        zbuf[...] = jnp.zeros(zbuf.shape, zbuf.dtype)
        zunit = zbuf.at[pl.ds(0, MOE_UNIT), :]
        total = 0
        for e in range(N_EXPERTS):
            row = tail_start_ref[e]

            def start(u, carry):
                dst = x_hbm.at[pl.ds(pl.multiple_of(row + u * MOE_UNIT, MOE_UNIT), MOE_UNIT), :]
                pltpu.make_async_copy(zunit, dst, sem.at[2]).start()
                return carry

            lax.fori_loop(0, tail_units_ref[e], start, 0)
            total = total + tail_units_ref[e]
        _wait_unit_copies(total, x_hbm, zunit, sem.at[2])
        n_blk = x_hbm.shape[0] // MOE_BM

        def block_copy(blk):
            dst = x_hbm.at[pl.ds(pl.multiple_of(blk * MOE_BM, MOE_BM), MOE_BM), :]
            return pltpu.make_async_copy(zbuf, dst, sem.at[3])

        def start_block(blk, carry):
            block_copy(blk).start()
            return carry

        def wait_block(blk, carry):
            block_copy(blk).wait()
            return carry

        lax.fori_loop(nused_ref[0], n_blk, start_block, 0)
        lax.fori_loop(nused_ref[0], n_blk, wait_block, 0)

        @pl.when(j >= 1)
        def _():
            _wait_unit_copies(tile_units_ref[jnp.maximum(j - 1, 0)], x_hbm, ybuf.at[1 - slot], sem.at[1 - slot])

        _wait_unit_copies(tile_units_ref[j], x_hbm, ybuf.at[slot], sem.at[slot])


def _dispatch(hn, ids, ranks, tables, tt, n_rows):
    rows = hn.shape[0]
    rt = _moe_tile_rows(tt)
    row_spec = pl.BlockSpec((tt, LANES), lambda j, *_: (j, 0))
    grid_spec = pltpu.PrefetchScalarGridSpec(
        num_scalar_prefetch=7,
        grid=(rows // tt,),
        in_specs=[pl.BlockSpec((tt, D_MODEL), lambda j, *_: (j, 0)), row_spec, row_spec,
                  pl.BlockSpec((None, 8, LANES), lambda j, *_: (j, 0, 0))],
        out_specs=[pl.BlockSpec(memory_space=pl.ANY), row_spec],
        scratch_shapes=[pltpu.VMEM((2, rt, D_MODEL), MOE_DTYPE), pltpu.VMEM((MOE_BM, D_MODEL), MOE_DTYPE),
                        pltpu.SemaphoreType.DMA((4,))],
    )
    return pl.pallas_call(
        _dispatch_kernel,
        grid_spec=grid_spec,
        out_shape=[jax.ShapeDtypeStruct((n_rows, D_MODEL), MOE_DTYPE),
                   jax.ShapeDtypeStruct((rows, LANES), jnp.int32)],
        compiler_params=_params(("arbitrary",)),
        name="moe_dispatch",
    )(tables["base"], tables["loc"], tables["units"], tables["tile_units"], tables["tail_start"],
      tables["tail_units"], tables["n_used"], hn, ids, ranks, tables["loc_lanes"])


WEIGHT_DMA_PRIORITIES = (1, 0, 1)
WEIGHT_SLOTS = 3


def _expert_kernel(be_ref, slot_ref, nxt1_ref, nxt2_ref, nused_ref, x_ref, wg_hbm, wu_hbm, wd_hbm, o_ref,
                   wgf, wuf, wdf, wgb, wub, wdb, wsem, *, layer):
    i = pl.program_id(0)
    n_used = nused_ref[0]

    def weight_copies(expert, wslot):
        pairs = ((wg_hbm, wgf), (wu_hbm, wuf), (wd_hbm, wdf))
        return [pltpu.make_async_copy(w.at[layer, expert], buf.at[wslot], wsem.at[wslot]) for w, buf in pairs]

    def start_weights(expert, wslot):
        for c, priority in zip(weight_copies(expert, wslot), WEIGHT_DMA_PRIORITIES):
            c.start(priority=priority)

    @pl.when(i == 0)
    def _():
        start_weights(be_ref[0], 0)

        @pl.when(nxt1_ref[0] >= 0)
        def _():
            start_weights(nxt1_ref[0], 1)

    @pl.when(i < n_used)
    def _():
        @pl.when((i == 0) | (be_ref[i] != be_ref[jnp.maximum(i - 1, 0)]))
        def _():
            wslot = slot_ref[i]
            for c in weight_copies(be_ref[i], wslot):
                c.wait()

            @pl.when(nxt2_ref[i] >= 0)
            def _():
                start_weights(nxt2_ref[i], lax.rem(wslot + 2, WEIGHT_SLOTS))

            wgb[...] = wgf[wslot].astype(BF16)
            wub[...] = wuf[wslot].astype(BF16)
            wdb[...] = wdf[wslot].astype(BF16)

        x = x_ref[...]
        gate = jnp.dot(x, wgb[...], preferred_element_type=F32)
        up = jnp.dot(x, wub[...], preferred_element_type=F32)
        act = (jax.nn.silu(gate) * up).astype(BF16)
        o_ref[...] = jnp.dot(act, wdb[...], preferred_element_type=F32).astype(o_ref.dtype)

    @pl.when(i >= n_used)
    def _():
        o_ref[...] = jnp.zeros(o_ref.shape, o_ref.dtype)


def _experts(x_sorted, tables, wg, wu, wd, layer, n_blk):
    hbm = pl.BlockSpec(memory_space=pl.ANY)
    grid_spec = pltpu.PrefetchScalarGridSpec(
        num_scalar_prefetch=5,
        grid=(n_blk,),
        in_specs=[pl.BlockSpec((MOE_BM, D_MODEL), lambda i, be, slot, n1, n2, nu: (jnp.minimum(i, nu[0] - 1), 0)),
                  hbm, hbm, hbm],
        out_specs=pl.BlockSpec((MOE_BM, D_MODEL), lambda i, *_: (i, 0)),
        scratch_shapes=[
            pltpu.VMEM((WEIGHT_SLOTS, D_MODEL, D_EXPERT), F32),
            pltpu.VMEM((WEIGHT_SLOTS, D_MODEL, D_EXPERT), F32),
            pltpu.VMEM((WEIGHT_SLOTS, D_EXPERT, D_MODEL), F32),
            pltpu.VMEM((D_MODEL, D_EXPERT), BF16),
            pltpu.VMEM((D_MODEL, D_EXPERT), BF16),
            pltpu.VMEM((D_EXPERT, D_MODEL), BF16),
            pltpu.SemaphoreType.DMA((WEIGHT_SLOTS,)),
        ],
    )
    return pl.pallas_call(
        functools.partial(_expert_kernel, layer=layer),
        grid_spec=grid_spec,
        out_shape=jax.ShapeDtypeStruct((n_blk * MOE_BM, D_MODEL), MOE_DTYPE),
        compiler_params=_params(("arbitrary",)),
        name="moe_experts",
    )(tables["blk_expert"], tables["blk_slot"], tables["blk_next1"], tables["blk_next2"], tables["n_used"],
      x_sorted, wg, wu, wd)


def _combine_kernel(base_ref, loc_ref, units_ref, tile_units_ref, h_ref, wt_ref, lrow_ref, g_ref, y_hbm, o_ref,
                    ybuf, sem, *, final_norm):
    j = pl.program_id(0)
    n = pl.num_programs(0)
    slot = j % 2
    tt, rt = h_ref.shape[0], ybuf.shape[1]
    assert rt % LANES == 0

    @pl.when(j == 0)
    def _():
        ybuf[...] = jnp.zeros(ybuf.shape, ybuf.dtype)
        _segment_copies(0, y_hbm, ybuf.at[0], sem.at[0], False, base_ref, loc_ref, units_ref)

    @pl.when(j + 1 < n)
    def _():
        _segment_copies(j + 1, y_hbm, ybuf.at[1 - slot], sem.at[1 - slot], False, base_ref, loc_ref, units_ref)

    _wait_unit_copies(tile_units_ref[j], y_hbm, ybuf.at[slot], sem.at[slot])
    lrow = lrow_ref[...]
    wt = wt_ref[...]
    col = lax.broadcasted_iota(jnp.int32, (tt, rt), 1)
    is0 = col == lrow[:, 0:1]
    is1 = col == lrow[:, 1:2]
    w_of_row = jnp.sum(jnp.where(is0, wt[:, 0:1], 0.0) + jnp.where(is1, wt[:, 1:2], 0.0), axis=0, keepdims=True)
    w_rows = jnp.transpose(jnp.broadcast_to(w_of_row, (LANES, rt)))
    w_rows = jnp.tile(w_rows, (1, OUT_TN // LANES))
    pick = (is0 | is1).astype(BF16)
    for c in range(D_MODEL // OUT_TN):
        cols = slice(c * OUT_TN, (c + 1) * OUT_TN)
        y = (ybuf[slot, :, cols].astype(F32) * w_rows).astype(BF16)
        o_ref[:, cols] = h_ref[:, cols] + jnp.dot(pick, y, preferred_element_type=F32)
    if final_norm:
        o_ref[...] = _rms_norm(o_ref[...], g_ref[...])


def _combine(h, wts, lrows, y_sorted, tables, g, tt, final_norm):
    rows = h.shape[0]
    rt = _moe_tile_rows(tt)
    grid_spec = pltpu.PrefetchScalarGridSpec(
        num_scalar_prefetch=4,
        grid=(rows // tt,),
        in_specs=[
            pl.BlockSpec((tt, D_MODEL), lambda j, *_: (j, 0)),
            pl.BlockSpec((tt, LANES), lambda j, *_: (j, 0)),
            pl.BlockSpec((tt, LANES), lambda j, *_: (j, 0)),
            pl.BlockSpec((1, D_MODEL), lambda j, *_: (0, 0)),
            pl.BlockSpec(memory_space=pl.ANY),
        ],
        out_specs=pl.BlockSpec((tt, D_MODEL), lambda j, *_: (j, 0)),
        scratch_shapes=[pltpu.VMEM((2, rt, D_MODEL), MOE_DTYPE), pltpu.SemaphoreType.DMA((2,))],
    )
    return pl.pallas_call(
        functools.partial(_combine_kernel, final_norm=final_norm),
        grid_spec=grid_spec,
        out_shape=jax.ShapeDtypeStruct((rows, D_MODEL), F32),
        compiler_params=_params(("arbitrary",)),
        name="moe_combine",
    )(tables["base"], tables["loc"], tables["units"], tables["tile_units"], h, wts, lrows, g, y_sorted)


def _moe(h, hn, wts, ids, ranks, tile_cnt, wg, wu, wd, layer, final_g):
    rows = h.shape[0]
    n_tiles = tile_cnt.shape[0]
    tt = rows // n_tiles
    n_rows = 2 * rows + n_tiles * N_EXPERTS * (MOE_UNIT - 1) + N_EXPERTS * (MOE_BM - 1)
    n_blk = -(-n_rows // MOE_BM)
    tables = _moe_tables(tile_cnt, n_blk)
    x_sorted, lrows = _dispatch(hn, ids, ranks, tables, tt, n_blk * MOE_BM)
    y_sorted = _experts(x_sorted, tables, wg, wu, wd, layer, n_blk)
    final_norm = final_g is not None
    g = final_g if final_norm else jnp.ones((1, D_MODEL), F32)
    return _combine(h, wts, lrows, y_sorted, tables, g, tt, final_norm)


def _rope_tables(batch, lp):
    half = HEAD_DIM // 2
    inv = ROPE_THETA ** (-jnp.arange(half, dtype=F32) / half)
    pos = (jnp.arange(lp, dtype=jnp.int32) - PAD).astype(F32)
    ang = pos[:, None] * inv[None, :]
    cos = jnp.tile(jnp.cos(ang), (batch, LANES // half))
    sin = jnp.sin(ang)
    sin = jnp.tile(jnp.concatenate([-sin, sin], axis=1), (batch, LANES // HEAD_DIM))
    return cos, sin


def kernel(x, meta_tokens, attn_norm_g, ffn_norm_g, w_in, swa_sinks, ssm_lambda_re, ssm_lambda_im, ssm_b_re, ssm_b_im, ssm_c_re, ssm_c_im, ssm_d, ssm_log_dt, ssm_w_glu, diff_lambda_q1, diff_lambda_k1, diff_lambda_q2, diff_lambda_k2, diff_subln_g, w_out, router_group_w, router_group_b, router_expert_w, router_expert_b, moe_w_gate, moe_w_up, moe_w_down, final_norm_g):
    batch, seq, d = x.shape
    depth = w_in.shape[0]
    lp = PAD + N_META + seq
    nb = lp // BLK
    meta = jnp.broadcast_to(meta_tokens[None].astype(x.dtype), (batch, N_META, d))
    h = jnp.concatenate([jnp.zeros((batch, PAD, d), x.dtype), meta, x], axis=1).reshape(batch * lp, d)
    cos, sin = _rope_tables(batch, lp)
    w_in_bf, w_out_bf, w_glu_bf = w_in.astype(BF16), w_out.astype(BF16), ssm_w_glu.astype(BF16)
    for l in range(depth):
        att, us, qd, kd, vd = _norm_inproj(h, attn_norm_g[l][None], w_in_bf, l, cos, sin)
        o_a = _swa(att, swa_sinks[l].astype(F32), batch, nb)
        kc, bc, cc, a_t = _ssm_operators(ssm_lambda_re[l], ssm_lambda_im[l], ssm_b_re[l], ssm_b_im[l],
                                         ssm_c_re[l], ssm_c_im[l], ssm_log_dt[l])
        yg = _ssm(us, kc, bc, cc, a_t, ssm_d[l].astype(F32).reshape(SSM_PACKS, 1, LANES), batch, lp)
        lam_init = 0.8 - 0.6 * math.exp(-0.3 * l)
        lam_vecs = jnp.stack([diff_lambda_q1[l], diff_lambda_k1[l], diff_lambda_q2[l], diff_lambda_k2[l]]).astype(F32)
        o_c = _diff_attn(qd, kd, vd, lam_vecs, diff_subln_g[l][None].astype(F32), lam_init, batch, nb)
        rw = jnp.concatenate([router_group_w[l], router_expert_w[l]], axis=1).astype(F32)
        rw = jnp.pad(rw, ((0, 0), (0, LANES - rw.shape[1])))
        rb = jnp.concatenate([router_group_b[l], router_expert_b[l]]).astype(F32)
        rb = jnp.pad(rb, (0, LANES - rb.shape[0]))[None]
        rw_hi, rw_lo = _split_bf16(rw)
        g_ffn = ffn_norm_g[l][None].astype(F32)
        h, hn, wts, ids, ranks, tile_cnt = _mixer_out(h, o_a, yg, o_c, w_out_bf, w_glu_bf, l, g_ffn, rw_hi, rw_lo, rb)
        final_g = final_norm_g[None].astype(F32) if l == depth - 1 else None
        h = _moe(h, hn, wts, ids, ranks, tile_cnt, moe_w_gate, moe_w_up, moe_w_down, l, final_g)
    return h.reshape(batch, lp, d)[:, PAD + N_META:]
```

```python
import functools
import math

import jax
import jax.numpy as jnp
from jax import lax
from jax.experimental import pallas as pl
from jax.experimental.pallas import tpu as pltpu

D_MODEL = 2048
N_META = 16
BLK = 128
PAD = BLK - N_META
ROPE_THETA = 10000.0
NORM_EPS = 1e-5
NEG = -1e30
SWA_HEADS = 12
SWA_KV_HEADS = 4
HEAD_DIM = 64
SWA_WIDTH = SWA_HEADS * HEAD_DIM
SSM_WIDTH = 768
SSM_GROUP = 16
SSM_GROUPS = SSM_WIDTH // SSM_GROUP
SSM_STATE = 64
DIFF_HEADS = 4
DIFF_V_DIM = 128
DIFF_WIDTH = DIFF_HEADS * DIFF_V_DIM
IN_WIDTH = 3584
N_EXPERT_GROUPS = 4
EXPERTS_PER_GROUP = 8
N_EXPERTS = N_EXPERT_GROUPS * EXPERTS_PER_GROUP
D_EXPERT = 512

COL_KA = 768
COL_VA = 1024
COL_US = 1280
COL_QD = 2048
COL_KD = 2560
COL_VD = 3072

LANES = 128
SSM_T = 16
SSM_PACK = LANES // SSM_GROUP
SSM_PACKS = SSM_GROUPS // SSM_PACK
SSM_PSTATE = SSM_PACK * SSM_STATE
MOE_BM = 256
VMEM_LIMIT = 56 * 1024 * 1024

F32 = jnp.float32
BF16 = jnp.bfloat16


def _row_tile(rows):
    for t in (512, 256, 128):
        if rows % t == 0:
            return t
    raise ValueError(f"row count {rows} is not a multiple of 128")


def _params(sem, vmem=VMEM_LIMIT):
    return pltpu.CompilerParams(dimension_semantics=sem, vmem_limit_bytes=vmem)


def _rms_norm(x, g):
    return x * lax.rsqrt(jnp.mean(x * x, axis=-1, keepdims=True) + NORM_EPS) * g


IN_TN = 512
ROPE_TILES = (0, 1, 4, 5)


def _norm_inproj_kernel(x_ref, g_ref, w_ref, cos_ref, sin_ref, att_ref, us_ref, qd_ref, kd_ref, vd_ref, xn_ref):
    blocked = ((COL_VD, vd_ref), (COL_KD, kd_ref), (COL_QD, qd_ref), (COL_US, us_ref))

    def store(col, piece):
        for start, ref in blocked:
            if col >= start:
                ref[(col - start) // LANES] = piece
                return
        att_ref[:, col:col + LANES] = piece

    x = x_ref[...]
    ms = jnp.mean(x * x, axis=-1, keepdims=True)
    xn_ref[...] = (x * lax.rsqrt(ms + NORM_EPS) * g_ref[...]).astype(BF16)
    shape = (x.shape[0], IN_TN)
    lane = lax.broadcasted_iota(jnp.int32, shape, 1)
    first_half = (lane & (HEAD_DIM - 1)) < (HEAD_DIM // 2)
    reps = IN_TN // LANES
    cos = jnp.tile(cos_ref[...], (1, reps))
    sin = jnp.tile(sin_ref[...], (1, reps))
    for j in range(IN_WIDTH // IN_TN):
        cols = slice(j * IN_TN, (j + 1) * IN_TN)
        acc = jnp.dot(xn_ref[...], w_ref[:, cols], preferred_element_type=F32)
        if j in ROPE_TILES:
            partner = jnp.where(first_half, pltpu.roll(acc, IN_TN - HEAD_DIM // 2, 1),
                                pltpu.roll(acc, HEAD_DIM // 2, 1))
            acc = acc * cos + partner * sin
        acc = acc.astype(BF16)
        for c in range(IN_TN // LANES):
            store(j * IN_TN + c * LANES, acc[:, c * LANES:(c + 1) * LANES])


def _norm_inproj(h, g, w, layer, cos, sin):
    rows = h.shape[0]
    tm = _row_tile(rows)

    def blocked(width):
        n = width // LANES
        return pl.BlockSpec((n, tm, LANES), lambda i: (0, i, 0)), jax.ShapeDtypeStruct((n, rows, LANES), BF16)

    specs, shapes = zip(
        (pl.BlockSpec((tm, COL_US), lambda i: (i, 0)), jax.ShapeDtypeStruct((rows, COL_US), BF16)),
        blocked(SSM_WIDTH), blocked(DIFF_WIDTH), blocked(DIFF_WIDTH), blocked(DIFF_WIDTH))
    return pl.pallas_call(
        _norm_inproj_kernel,
        grid=(rows // tm,),
        in_specs=[
            pl.BlockSpec((tm, D_MODEL), lambda i: (i, 0)),
            pl.BlockSpec((1, D_MODEL), lambda i: (0, 0)),
            pl.BlockSpec((None, D_MODEL, IN_WIDTH), lambda i: (layer, 0, 0)),
            pl.BlockSpec((tm, LANES), lambda i: (i, 0)),
            pl.BlockSpec((tm, LANES), lambda i: (i, 0)),
        ],
        out_specs=list(specs),
        out_shape=list(shapes),
        scratch_shapes=[pltpu.VMEM((tm, D_MODEL), BF16)],
        compiler_params=_params(("parallel",)),
        name="norm_inproj",
    )(h, g, w, cos, sin)


def _swa_kernel(sink_ref, q_ref, kp_ref, kc_ref, vp_ref, vc_ref, o_ref):
    n = pl.program_id(1)
    shape = (BLK, 2 * BLK)
    qi = lax.broadcasted_iota(jnp.int32, shape, 0)
    c = lax.broadcasted_iota(jnp.int32, shape, 1)
    kj = (n - 1) * BLK + c
    delta = qi + BLK - c
    ok = (delta >= 0) & (delta < BLK) & (kj >= PAD)
    grp = SWA_HEADS // SWA_KV_HEADS
    outs = []
    for j in range(SWA_KV_HEADS):
        cols = slice(j * HEAD_DIM, (j + 1) * HEAD_DIM)
        kk = jnp.concatenate([kp_ref[:, cols], kc_ref[:, cols]], axis=0)
        vv = jnp.concatenate([vp_ref[:, cols], vc_ref[:, cols]], axis=0)
        for g in range(grp):
            hd = j * grp + g
            qh = q_ref[:, hd * HEAD_DIM:(hd + 1) * HEAD_DIM] * (HEAD_DIM ** -0.5)
            s = lax.dot_general(qh, kk, (((1,), (1,)), ((), ())), preferred_element_type=F32)
            s = jnp.where(ok, s, NEG)
            sink = sink_ref[hd]
            m = jnp.maximum(jnp.max(s, axis=-1, keepdims=True), sink)
            p = jnp.exp(s - m)
            denom = jnp.sum(p, axis=-1, keepdims=True) + jnp.exp(sink - m)
            outs.append(jnp.dot(p.astype(BF16), vv, preferred_element_type=F32) * (1.0 / denom))
    o_ref[...] = jnp.concatenate(outs, axis=1).astype(o_ref.dtype)


def _swa(proj, sinks, batch, nb):
    rows = proj.shape[0]
    kvw = SWA_KV_HEADS * HEAD_DIM

    def cur(col):
        return lambda b, n: (b * nb + n, col)

    def prev(col):
        return lambda b, n: (b * nb + jnp.maximum(n - 1, 0), col)

    return pl.pallas_call(
        _swa_kernel,
        grid=(batch, nb),
        in_specs=[
            pl.BlockSpec(memory_space=pltpu.SMEM),
            pl.BlockSpec((BLK, SWA_WIDTH), cur(0)),
            pl.BlockSpec((BLK, kvw), prev(COL_KA // kvw)),
            pl.BlockSpec((BLK, kvw), cur(COL_KA // kvw)),
            pl.BlockSpec((BLK, kvw), prev(COL_VA // kvw)),
            pl.BlockSpec((BLK, kvw), cur(COL_VA // kvw)),
        ],
        out_specs=pl.BlockSpec((BLK, SWA_WIDTH), cur(0)),
        out_shape=jax.ShapeDtypeStruct((rows, SWA_WIDTH), BF16),
        compiler_params=_params(("parallel", "parallel")),
        name="swa",
    )(sinks, proj, proj, proj, proj, proj)


DIFF_QBLOCKS = 2


def _diff_kernel(lam_ref, g_ref, q_ref, k_ref, v_ref, o_ref, *, lam_init, nb):
    lane = lax.broadcasted_iota(jnp.int32, (BLK, LANES), 1)
    lv = lam_ref[...]
    lam = (jnp.exp(jnp.sum(lv[0:1] * lv[1:2], axis=-1, keepdims=True))
           - jnp.exp(jnp.sum(lv[2:3] * lv[3:4], axis=-1, keepdims=True)) + lam_init)
    gain = g_ref[...] * (1.0 - lam_init)
    scale = HEAD_DIM ** -0.5
    for n0 in range(0, nb, DIFF_QBLOCKS):
        blocks = range(n0, min(n0 + DIFF_QBLOCKS, nb))
        ke = (blocks[-1] + 1) * BLK
        kx = k_ref[0:ke, :]
        vx = v_ref[0:ke, :]
        rows = 2 * BLK * len(blocks)
        pieces = []
        for n in blocks:
            q = q_ref[n * BLK:(n + 1) * BLK, :] * scale
            zero = jnp.zeros_like(q)
            pieces += [jnp.where(lane < HEAD_DIM, q, zero), jnp.where(lane >= HEAD_DIM, q, zero)]
        s = lax.dot_general(jnp.concatenate(pieces, axis=0), kx, (((1,), (1,)), ((), ())),
                            preferred_element_type=F32)
        r = lax.broadcasted_iota(jnp.int32, (rows, BLK), 0)
        qpos = (n0 + jnp.right_shift(r, (2 * BLK).bit_length() - 1)) * BLK + (r & (BLK - 1))
        col = lax.broadcasted_iota(jnp.int32, (rows, BLK), 1)
        key_blocks = []
        for t in range(blocks[-1] + 1):
            blk = s[:, t * BLK:(t + 1) * BLK]
            ok = None
            if t >= n0:
                ok = t * BLK + col <= qpos
            if t == 0:
                ok = (col >= PAD) if ok is None else ok & (col >= PAD)
            key_blocks.append(blk if ok is None else jnp.where(ok, blk, NEG))
        s = jnp.concatenate(key_blocks, axis=1)
        m = jnp.max(s, axis=-1, keepdims=True)
        p = jnp.exp(s - m)
        denom = jnp.sum(p, axis=-1, keepdims=True)
        heads = jnp.dot(p.astype(BF16), vx, preferred_element_type=F32) * (1.0 / denom)
        for i, n in enumerate(blocks):
            o = heads[2 * i * BLK:(2 * i + 1) * BLK] - lam * heads[(2 * i + 1) * BLK:(2 * i + 2) * BLK]
            o = o * lax.rsqrt(jnp.mean(o * o, axis=-1, keepdims=True) + NORM_EPS) * gain
            o_ref[n * BLK:(n + 1) * BLK, :] = o.astype(o_ref.dtype)


def _diff_attn(qd, kd, vd, lam_vecs, subln_g, lam_init, batch, nb):
    rows = qd.shape[1]
    lp = nb * BLK
    head_rows = pl.BlockSpec((None, lp, LANES), lambda b, h: (h, b, 0))
    return pl.pallas_call(
        functools.partial(_diff_kernel, lam_init=lam_init, nb=nb),
        grid=(batch, DIFF_HEADS),
        in_specs=[
            pl.BlockSpec((4, HEAD_DIM), lambda b, h: (0, 0)),
            pl.BlockSpec((1, DIFF_V_DIM), lambda b, h: (0, 0)),
            head_rows, head_rows, head_rows,
        ],
        out_specs=head_rows,
        out_shape=jax.ShapeDtypeStruct((DIFF_HEADS, rows, DIFF_V_DIM), BF16),
        compiler_params=_params(("parallel", "parallel")),
        name="diff_attn",
    )(lam_vecs, subln_g, qd, kd, vd)


def _ssm_row_chunk(rows):
    for c in (544, 512, 384, 256, 128, 64, 32, 16):
        if rows % c == 0:
            return c
    raise ValueError(rows)


def _ssm_expand_operators(kc_ref, bc_ref, cc_ref, k_ref, bw_ref, cw_ref):
    def iota(shape, dim):
        return lax.broadcasted_iota(jnp.int32, shape, dim)

    def group(idx, width):
        return jnp.right_shift(idx, width.bit_length() - 1) & (SSM_PACK - 1)

    hs, ps, tl = SSM_GROUP, SSM_STATE, SSM_T * LANES
    spread_h = ((iota((LANES, hs), 0) & (hs - 1)) == iota((LANES, hs), 1)).astype(BF16)
    kt = jnp.dot(spread_h, kc_ref[...], preferred_element_type=F32)
    same = group(iota((LANES, tl), 0), hs) == group(iota((LANES, tl), 1), hs)
    k_ref[...] = jnp.where(same, kt, 0.0).astype(BF16)
    spread_s = (iota((ps, SSM_PSTATE), 0) == (iota((ps, SSM_PSTATE), 1) & (ps - 1))).astype(BF16)
    shape = (tl, SSM_PSTATE)
    same = group(iota(shape, 0), hs) == group(iota(shape, 1), ps)
    for part in range(2):
        bw = jnp.dot(bc_ref[part], spread_s, preferred_element_type=F32)
        bw_ref[:, part * SSM_PSTATE:(part + 1) * SSM_PSTATE] = jnp.where(same, bw, 0.0).astype(BF16)
    shape = (LANES, SSM_PSTATE)
    same = group(iota(shape, 0), hs) == group(iota(shape, 1), ps)
    for j in range(SSM_T):
        for part in range(2):
            ct = jnp.dot(spread_h, cc_ref[part, j], preferred_element_type=F32)
            cw_ref[j, :, part * SSM_PSTATE:(part + 1) * SSM_PSTATE] = jnp.where(same, ct, 0.0).astype(BF16)


def _ssm_kernel(u_ref, kc_ref, bc_ref, cc_ref, a_ref, d_ref, o_ref, up_ref, y_ref, e_ref, s_ref, k_ref, bw_ref,
                cw_ref, *, batch, lp):
    rows = batch * lp
    n_chunks = lp // SSM_T
    tot_chunks = rows // SSM_T
    _ssm_expand_operators(kc_ref, bc_ref, cc_ref, k_ref, bw_ref, cw_ref)
    contract_lanes = (((1,), (1,)), ((), ()))
    up_ref[0:SSM_T, :] = jnp.zeros((SSM_T, LANES), F32)
    up_ref[SSM_T:, :] = u_ref[...].astype(F32)
    for b in range(batch):
        up_ref[SSM_T + b * lp:SSM_T + b * lp + PAD, :] = jnp.zeros((PAD, LANES), F32)

    ch = _ssm_row_chunk(rows)
    rmod = lax.broadcasted_iota(jnp.int32, (ch, LANES), 0) & (SSM_T - 1)
    for r0 in range(0, rows, ch):
        pieces = []
        for k in range(SSM_T):
            sl = up_ref[SSM_T - k + r0:SSM_T - k + r0 + ch, :]
            pieces.append(jnp.where(rmod >= k, sl, 0.0).astype(BF16))
        y_ref[r0:r0 + ch, :] = lax.dot_general(jnp.concatenate(pieces, axis=1), k_ref[...], contract_lanes,
                                               preferred_element_type=F32)

    pieces = [up_ref[pl.ds(SSM_T + i, tot_chunks, stride=SSM_T), :].astype(BF16) for i in range(SSM_T)]
    e_ref[...] = jnp.dot(jnp.concatenate(pieces, axis=1), bw_ref[...], preferred_element_type=F32)

    a_re = a_ref[:, :SSM_PSTATE]
    a_im = a_ref[:, SSM_PSTATE:]

    def step(c, carry):
        new = []
        for b in range(batch):
            sr, si = carry[b]
            idx = b * n_chunks + c
            s_ref[pl.ds(idx, 1), :SSM_PSTATE] = sr
            s_ref[pl.ds(idx, 1), SSM_PSTATE:] = si
            er = e_ref[pl.ds(idx, 1), :SSM_PSTATE]
            ei = e_ref[pl.ds(idx, 1), SSM_PSTATE:]
            new.append((a_re * sr - a_im * si + er, a_re * si + a_im * sr + ei))
        return tuple(new)

    zero = jnp.zeros((1, SSM_PSTATE), F32)
    lax.fori_loop(0, n_chunks, step, tuple((zero, zero) for _ in range(batch)))

    sb = s_ref[...].astype(BF16)
    for j in range(SSM_T):
        z = lax.dot_general(sb, cw_ref[j], contract_lanes, preferred_element_type=F32)
        idx = pl.ds(j, tot_chunks, stride=SSM_T)
        y_ref[idx, :] = y_ref[idx, :] + z

    y = y_ref[...] + d_ref[...] * up_ref[SSM_T:, :]
    o_ref[...] = jax.nn.gelu(y, approximate=True).astype(o_ref.dtype)


def _ssm(us, kc, bc, cc, a_t, dskip, batch, lp):
    rows = us.shape[1]
    tot_chunks = rows // SSM_T
    return pl.pallas_call(
        functools.partial(_ssm_kernel, batch=batch, lp=lp),
        grid=(SSM_PACKS,),
        in_specs=[
            pl.BlockSpec((None, rows, LANES), lambda p: (p, 0, 0)),
            pl.BlockSpec((None, SSM_GROUP, SSM_T * LANES), lambda p: (p, 0, 0)),
            pl.BlockSpec((None, 2, SSM_T * LANES, SSM_STATE), lambda p: (p, 0, 0, 0)),
            pl.BlockSpec((None, 2, SSM_T, SSM_GROUP, SSM_PSTATE), lambda p: (p, 0, 0, 0, 0)),
            pl.BlockSpec((None, 1, 2 * SSM_PSTATE), lambda p: (p, 0, 0)),
            pl.BlockSpec((None, 1, LANES), lambda p: (p, 0, 0)),
        ],
        out_specs=pl.BlockSpec((None, rows, LANES), lambda p: (p, 0, 0)),
        out_shape=jax.ShapeDtypeStruct((SSM_PACKS, rows, LANES), F32),
        scratch_shapes=[
            pltpu.VMEM((rows + SSM_T, LANES), F32),
            pltpu.VMEM((rows, LANES), F32),
            pltpu.VMEM((tot_chunks, 2 * SSM_PSTATE), F32),
            pltpu.VMEM((tot_chunks, 2 * SSM_PSTATE), F32),
            pltpu.VMEM((LANES, SSM_T * LANES), BF16),
            pltpu.VMEM((SSM_T * LANES, 2 * SSM_PSTATE), BF16),
            pltpu.VMEM((SSM_T, LANES, 2 * SSM_PSTATE), BF16),
        ],
        compiler_params=_params(("parallel",)),
        name="ssm",
    )(us, kc, bc, cc, a_t, dskip)


def _ssm_operators(lam_re, lam_im, b_re, b_im, c_re, c_im, log_dt):
    p, h = SSM_STATE, SSM_GROUP
    dt = jnp.exp(log_dt.astype(F32))[:, None]
    lr, li = lam_re.astype(F32), lam_im.astype(F32)
    mag = jnp.exp(lr * dt)
    abar_re, abar_im = mag * jnp.cos(li * dt), mag * jnp.sin(li * dt)
    den = lr * lr + li * li
    nr = abar_re - 1.0
    f_re = (nr * lr + abar_im * li) / den
    f_im = (abar_im * lr - nr * li) / den
    br, bi = b_re.astype(F32), b_im.astype(F32)
    bbar_re = f_re[..., None] * br - f_im[..., None] * bi
    bbar_im = f_re[..., None] * bi + f_im[..., None] * br
    ks = jnp.arange(SSM_T + 1, dtype=F32)[:, None, None]
    pmag = jnp.exp(ks * (lr * dt)[None])
    pw_re = pmag * jnp.cos(ks * (li * dt)[None])
    pw_im = pmag * jnp.sin(ks * (li * dt)[None])
    cr, ci = c_re.astype(F32), c_im.astype(F32)

    packs, q = SSM_PACKS, SSM_PACK

    cb_re = cr[:, :, :, None] * bbar_re[:, None, :, :] - ci[:, :, :, None] * bbar_im[:, None, :, :]
    cb_im = cr[:, :, :, None] * bbar_im[:, None, :, :] + ci[:, :, :, None] * bbar_re[:, None, :, :]
    klag = (jnp.einsum('kgs,gosi->kgoi', pw_re[:SSM_T], cb_re)
            - jnp.einsum('kgs,gosi->kgoi', pw_im[:SSM_T], cb_im))
    kc = klag.reshape(SSM_T, packs, q, h, h).transpose(1, 3, 0, 2, 4).reshape(packs, h, SSM_T * LANES)

    rev_re, rev_im = pw_re[SSM_T - 1::-1, :, None, :], pw_im[SSM_T - 1::-1, :, None, :]
    bt_re, bt_im = jnp.swapaxes(bbar_re, 1, 2)[None], jnp.swapaxes(bbar_im, 1, 2)[None]

    def inj_by_pack(x):
        return x.reshape(SSM_T, packs, q * h, p).transpose(1, 0, 2, 3).reshape(packs, SSM_T * LANES, p)

    bc = jnp.stack([inj_by_pack(rev_re * bt_re - rev_im * bt_im), inj_by_pack(rev_re * bt_im + rev_im * bt_re)], axis=1)

    ca_re = cr[None] * pw_re[1:, :, None, :] - ci[None] * pw_im[1:, :, None, :]
    ca_im = cr[None] * pw_im[1:, :, None, :] + ci[None] * pw_re[1:, :, None, :]

    def out_by_pack(x):
        return x.reshape(SSM_T, packs, q, h, p).transpose(1, 0, 3, 2, 4).reshape(packs, SSM_T, h, SSM_PSTATE)

    cc = jnp.stack([out_by_pack(ca_re), out_by_pack(-ca_im)], axis=1)

    a_t = jnp.concatenate([pw_re[SSM_T].reshape(packs, 1, SSM_PSTATE),
                           pw_im[SSM_T].reshape(packs, 1, SSM_PSTATE)], axis=-1)
    return kc.astype(BF16), bc.astype(BF16), cc.astype(BF16), a_t


OUT_TN = 512


def _split_bf16(x):
    hi = x.astype(BF16)
    return hi, (x - hi.astype(F32)).astype(BF16)


def _route(hn, rwh_ref, rwl_ref, rb_ref):
    hi, lo = _split_bf16(hn)
    logits = (jnp.dot(hi, rwh_ref[...], preferred_element_type=F32)
              + (jnp.dot(lo, rwh_ref[...], preferred_element_type=F32)
                 + jnp.dot(hi, rwl_ref[...], preferred_element_type=F32))) + rb_ref[...]
    lane = lax.broadcasted_iota(jnp.int32, logits.shape, 1)

    def first_argmax(v, vmax):
        return jnp.min(jnp.where(v == vmax, lane, LANES), axis=-1, keepdims=True)

    gl = jnp.where(lane < N_EXPERT_GROUPS, logits, NEG)
    gmax = jnp.max(gl, axis=-1, keepdims=True)
    g_sel = first_argmax(gl, gmax)
    p_g = 1.0 / jnp.sum(jnp.exp(gl - gmax), axis=-1, keepdims=True)
    e_lane = lane - N_EXPERT_GROUPS
    in_group = (e_lane >= 0) & (e_lane < N_EXPERTS) & (jnp.right_shift(e_lane, 3) == g_sel)
    el = jnp.where(in_group, logits, NEG)
    m1 = jnp.max(el, axis=-1, keepdims=True)
    i1 = first_argmax(el, m1)
    el2 = jnp.where(lane == i1, NEG, el)
    m2 = jnp.max(el2, axis=-1, keepdims=True)
    i2 = first_argmax(el2, m2)
    z = jnp.sum(jnp.exp(el - m1), axis=-1, keepdims=True)
    p1 = 1.0 / z
    p2 = jnp.exp(m2 - m1) / z
    w1 = p1 / (p1 + p2) * p_g
    w2 = p2 / (p1 + p2) * p_g
    wts = jnp.where(lane == 0, w1, jnp.where(lane == 1, w2, 0.0))
    ids = jnp.where(lane == 0, i1 - N_EXPERT_GROUPS, jnp.where(lane == 1, i2 - N_EXPERT_GROUPS, 0))
    return wts, ids


def _assignment_ranks(ids):
    tm = ids.shape[0]
    lane = lax.broadcasted_iota(jnp.int32, ids.shape, 1)
    oh0 = lane == ids[:, 0:1]
    oh1 = lane == ids[:, 1:2]
    hits = (oh0 | oh1).astype(BF16)
    r = lax.broadcasted_iota(jnp.int32, (tm, tm), 0)
    c = lax.broadcasted_iota(jnp.int32, (tm, tm), 1)
    before = jnp.dot((c < r).astype(BF16), hits, preferred_element_type=F32)
    rank0 = jnp.sum(jnp.where(oh0, before, 0.0), axis=-1, keepdims=True)
    rank1 = jnp.sum(jnp.where(oh1, before, 0.0), axis=-1, keepdims=True)
    ranks = jnp.where(lane == 0, rank0, jnp.where(lane == 1, rank1, 0.0)).astype(jnp.int32)
    return ranks, jnp.sum(hits.astype(F32), axis=0, keepdims=True).astype(jnp.int32)


def _mixer_out_kernel(h_ref, a_ref, y_ref, c_ref, wo_ref, wglu_ref, g_ref, rwh_ref, rwl_ref, rb_ref,
                      o_ref, hn_ref, wt_ref, id_ref, rank_ref, cnt_ref):
    half = h_ref.shape[0] // 2
    halves = [slice(r * half, (r + 1) * half) for r in range(2)]
    for rows in halves:
        y = jnp.concatenate([y_ref[p, rows, :] for p in range(y_ref.shape[0])], axis=1)
        c = jnp.concatenate([c_ref[hd, rows, :] for hd in range(c_ref.shape[0])], axis=1)
        gate = jnp.dot(y.astype(BF16), wglu_ref[...], preferred_element_type=F32)
        b = (y * jax.nn.sigmoid(gate)).astype(BF16)
        mixed = jnp.concatenate([a_ref[rows, :], b, c], axis=1)
        for j in range(D_MODEL // OUT_TN):
            cols = slice(j * OUT_TN, (j + 1) * OUT_TN)
            o_ref[rows, cols] = h_ref[rows, cols] + jnp.dot(mixed, wo_ref[:, cols], preferred_element_type=F32)
    for rows in halves:
        hn = _rms_norm(o_ref[rows, :], g_ref[...])
        hn_ref[rows, :] = hn.astype(BF16)
        wts, ids = _route(hn, rwh_ref, rwl_ref, rb_ref)
        wt_ref[rows, :] = wts
        id_ref[rows, :] = ids
    ranks, counts = _assignment_ranks(id_ref[...])
    rank_ref[...] = ranks
    cnt_ref[...] = jnp.broadcast_to(counts, cnt_ref.shape)


def _mixer_out(h, oa, yg, oc, wo, wglu, layer, g, rw_hi, rw_lo, rb):
    rows = h.shape[0]
    tm = _row_tile(rows)

    def layer_weight(shape):
        return pl.BlockSpec((None,) + shape, lambda i: (layer, 0, 0))

    def row_spec(width):
        return pl.BlockSpec((tm, width), lambda i: (i, 0))

    def whole(shape):
        return pl.BlockSpec(shape, lambda i: (0, 0))

    def blocked(width):
        return pl.BlockSpec((width // LANES, tm, LANES), lambda i: (0, i, 0))

    return pl.pallas_call(
        _mixer_out_kernel,
        grid=(rows // tm,),
        in_specs=[
            row_spec(D_MODEL), row_spec(SWA_WIDTH), blocked(SSM_WIDTH), blocked(DIFF_WIDTH),
            layer_weight((D_MODEL, D_MODEL)), layer_weight((SSM_WIDTH, SSM_WIDTH)), whole((1, D_MODEL)),
            whole((D_MODEL, LANES)), whole((D_MODEL, LANES)), whole((1, LANES)),
        ],
        out_specs=[row_spec(D_MODEL), row_spec(D_MODEL), row_spec(LANES), row_spec(LANES), row_spec(LANES),
                   pl.BlockSpec((None, 8, LANES), lambda i: (i, 0, 0))],
        out_shape=[
            jax.ShapeDtypeStruct((rows, D_MODEL), F32),
            jax.ShapeDtypeStruct((rows, D_MODEL), BF16),
            jax.ShapeDtypeStruct((rows, LANES), F32),
            jax.ShapeDtypeStruct((rows, LANES), jnp.int32),
            jax.ShapeDtypeStruct((rows, LANES), jnp.int32),
            jax.ShapeDtypeStruct((rows // tm, 8, LANES), jnp.int32),
        ],
        compiler_params=_params(("parallel",)),
        name="mixer_out",
    )(h, oa, yg, oc, wo, wglu, g, rw_hi, rw_lo, rb)


MOE_UNIT = 16
MOE_DTYPE = BF16


def _moe_tile_rows(tt):
    return 2 * tt + N_EXPERTS * MOE_UNIT


def _moe_tables(tile_cnt, n_blk):
    cnt = tile_cnt[:, 0, :N_EXPERTS]
    seg = (cnt + MOE_UNIT - 1) // MOE_UNIT * MOE_UNIT
    size = jnp.sum(seg, axis=0)
    padded = (size + MOE_BM - 1) // MOE_BM * MOE_BM
    pend = jnp.cumsum(padded)
    pstart = pend - padded
    base = pstart[None, :] + jnp.cumsum(seg, axis=0) - seg
    loc = jnp.cumsum(seg, axis=1) - seg
    units = seg // MOE_UNIT
    blk_start = jnp.arange(n_blk, dtype=jnp.int32) * MOE_BM
    blk_expert = jnp.minimum(jnp.sum((pend[None, :] <= blk_start[:, None]).astype(jnp.int32), axis=1), N_EXPERTS - 1)
    ex = jnp.arange(N_EXPERTS, dtype=jnp.int32)
    nonempty = size > 0
    run_slot = (jnp.cumsum(nonempty.astype(jnp.int32)) - 1) % WEIGHT_SLOTS
    later = (ex[None, :] > ex[:, None]) & nonempty[None, :]
    nxt = jnp.min(jnp.where(later, ex[None, :], N_EXPERTS), axis=1)
    nxt = jnp.where(nxt == N_EXPERTS, -1, nxt)
    nxt2 = jnp.where(nxt >= 0, nxt[jnp.maximum(nxt, 0)], -1)
    i32 = jnp.int32
    loc_lanes = jnp.pad(loc, ((0, 0), (0, LANES - N_EXPERTS)))
    return dict(
        base=base.reshape(-1).astype(i32), loc=loc.reshape(-1).astype(i32), units=units.reshape(-1).astype(i32),
        tile_units=jnp.sum(units, axis=1).astype(i32),
        tail_start=(pstart + size).astype(i32), tail_units=((padded - size) // MOE_UNIT).astype(i32),
        loc_lanes=jnp.broadcast_to(loc_lanes[:, None, :], (loc.shape[0], 8, LANES)).astype(i32),
        blk_expert=blk_expert.astype(i32), blk_slot=run_slot[blk_expert].astype(i32),
        blk_next1=nxt[blk_expert].astype(i32), blk_next2=nxt2[blk_expert].astype(i32),
        n_used=(pend[-1] // MOE_BM).astype(i32).reshape(1))


def _segment_copies(tile, hbm, buf, sem, to_hbm, base_ref, loc_ref, units_ref):
    for e in range(N_EXPERTS):
        k = tile * N_EXPERTS + e
        row_hbm, row_buf = base_ref[k], loc_ref[k]

        def start(u, carry):
            a = hbm.at[pl.ds(pl.multiple_of(row_hbm + u * MOE_UNIT, MOE_UNIT), MOE_UNIT), :]
            b = buf.at[pl.ds(pl.multiple_of(row_buf + u * MOE_UNIT, MOE_UNIT), MOE_UNIT), :]
            (pltpu.make_async_copy(b, a, sem) if to_hbm else pltpu.make_async_copy(a, b, sem)).start()
            return carry

        lax.fori_loop(0, units_ref[k], start, 0)


def _wait_unit_copies(count, hbm, buf, sem):
    def wait(u, carry):
        pltpu.make_async_copy(hbm.at[pl.ds(0, MOE_UNIT), :], buf.at[pl.ds(0, MOE_UNIT), :], sem).wait()
        return carry

    lax.fori_loop(0, count, wait, 0)


def _local_rows(ids, ranks, loc_lanes):
    lane = lax.broadcasted_iota(jnp.int32, ids.shape, 1)
    loc = loc_lanes.astype(F32)
    start0 = jnp.sum(jnp.where(lane == ids[:, 0:1], loc, 0.0), axis=-1, keepdims=True).astype(jnp.int32)
    start1 = jnp.sum(jnp.where(lane == ids[:, 1:2], loc, 0.0), axis=-1, keepdims=True).astype(jnp.int32)
    return start0 + ranks[:, 0:1], start1 + ranks[:, 1:2]


def _dispatch_kernel(base_ref, loc_ref, units_ref, tile_units_ref, tail_start_ref, tail_units_ref, nused_ref,
                     hn_ref, ids_ref, rank_ref, locv_ref, x_hbm, lrow_ref, ybuf, zbuf, sem):
    j = pl.program_id(0)
    n = pl.num_programs(0)
    slot = j % 2
    tt, rt = hn_ref.shape[0], ybuf.shape[1]

    @pl.when(j >= 2)
    def _():
        _wait_unit_copies(tile_units_ref[j - 2], x_hbm, ybuf.at[slot], sem.at[slot])

    row0, row1 = _local_rows(ids_ref[...], rank_ref[...], locv_ref[0:1, :])
    lane = lax.broadcasted_iota(jnp.int32, lrow_ref.shape, 1)
    lrow_ref[...] = jnp.where(lane == 0, row0, jnp.where(lane == 1, row1, 0))
    col = lax.broadcasted_iota(jnp.int32, (tt, rt), 1)
    onehot = ((col == row0) | (col == row1)).astype(BF16)
    compacted = lax.dot_general(onehot, hn_ref[...], (((0,), (0,)), ((), ())), preferred_element_type=F32)
    ybuf[slot] = compacted.astype(ybuf.dtype)
    _segment_copies(j, x_hbm, ybuf.at[slot], sem.at[slot], True, base_ref, loc_ref, units_ref)

    @pl.when(j == n - 1)
    def _():
        zbuf[...] = jnp.zeros(zbuf.shape, zbuf.dtype)
        zunit = zbuf.at[pl.ds(0, MOE_UNIT), :]
        total = 0
        for e in range(N_EXPERTS):
            row = tail_start_ref[e]

            def start(u, carry):
                dst = x_hbm.at[pl.ds(pl.multiple_of(row + u * MOE_UNIT, MOE_UNIT), MOE_UNIT), :]
                pltpu.make_async_copy(zunit, dst, sem.at[2]).start()
                return carry

            lax.fori_loop(0, tail_units_ref[e], start, 0)
            total = total + tail_units_ref[e]
        _wait_unit_copies(total, x_hbm, zunit, sem.at[2])
        n_blk = x_hbm.shape[0] // MOE_BM

        def block_copy(blk):
            dst = x_hbm.at[pl.ds(pl.multiple_of(blk * MOE_BM, MOE_BM), MOE_BM), :]
            return pltpu.make_async_copy(zbuf, dst, sem.at[3])

        def start_block(blk, carry):
            block_copy(blk).start()
            return carry

        def wait_block(blk, carry):
            block_copy(blk).wait()
            return carry

        lax.fori_loop(nused_ref[0], n_blk, start_block, 0)
        lax.fori_loop(nused_ref[0], n_blk, wait_block, 0)

        @pl.when(j >= 1)
        def _():
            _wait_unit_copies(tile_units_ref[jnp.maximum(j - 1, 0)], x_hbm, ybuf.at[1 - slot], sem.at[1 - slot])

        _wait_unit_copies(tile_units_ref[j], x_hbm, ybuf.at[slot], sem.at[slot])


def _dispatch(hn, ids, ranks, tables, tt, n_rows):
    rows = hn.shape[0]
    rt = _moe_tile_rows(tt)
    row_spec = pl.BlockSpec((tt, LANES), lambda j, *_: (j, 0))
    grid_spec = pltpu.PrefetchScalarGridSpec(
        num_scalar_prefetch=7,
        grid=(rows // tt,),
        in_specs=[pl.BlockSpec((tt, D_MODEL), lambda j, *_: (j, 0)), row_spec, row_spec,
                  pl.BlockSpec((None, 8, LANES), lambda j, *_: (j, 0, 0))],
        out_specs=[pl.BlockSpec(memory_space=pl.ANY), row_spec],
        scratch_shapes=[pltpu.VMEM((2, rt, D_MODEL), MOE_DTYPE), pltpu.VMEM((MOE_BM, D_MODEL), MOE_DTYPE),
                        pltpu.SemaphoreType.DMA((4,))],
    )
    return pl.pallas_call(
        _dispatch_kernel,
        grid_spec=grid_spec,
        out_shape=[jax.ShapeDtypeStruct((n_rows, D_MODEL), MOE_DTYPE),
                   jax.ShapeDtypeStruct((rows, LANES), jnp.int32)],
        compiler_params=_params(("arbitrary",)),
        name="moe_dispatch",
    )(tables["base"], tables["loc"], tables["units"], tables["tile_units"], tables["tail_start"],
      tables["tail_units"], tables["n_used"], hn, ids, ranks, tables["loc_lanes"])


WEIGHT_DMA_PRIORITIES = (1, 0, 1)
WEIGHT_SLOTS = 3


def _expert_kernel(be_ref, slot_ref, nxt1_ref, nxt2_ref, nused_ref, x_ref, wg_hbm, wu_hbm, wd_hbm, o_ref,
                   wgf, wuf, wdf, wgb, wub, wdb, wsem, *, layer):
    i = pl.program_id(0)
    n_used = nused_ref[0]

    def weight_copies(expert, wslot):
        pairs = ((wg_hbm, wgf), (wu_hbm, wuf), (wd_hbm, wdf))
        return [pltpu.make_async_copy(w.at[layer, expert], buf.at[wslot], wsem.at[wslot]) for w, buf in pairs]

    def start_weights(expert, wslot):
        for c, priority in zip(weight_copies(expert, wslot), WEIGHT_DMA_PRIORITIES):
            c.start(priority=priority)

    @pl.when(i == 0)
    def _():
        start_weights(be_ref[0], 0)

        @pl.when(nxt1_ref[0] >= 0)
        def _():
            start_weights(nxt1_ref[0], 1)

    @pl.when(i < n_used)
    def _():
        @pl.when((i == 0) | (be_ref[i] != be_ref[jnp.maximum(i - 1, 0)]))
        def _():
            wslot = slot_ref[i]
            for c in weight_copies(be_ref[i], wslot):
                c.wait()

            @pl.when(nxt2_ref[i] >= 0)
            def _():
                start_weights(nxt2_ref[i], lax.rem(wslot + 2, WEIGHT_SLOTS))

            wgb[...] = wgf[wslot].astype(BF16)
            wub[...] = wuf[wslot].astype(BF16)
            wdb[...] = wdf[wslot].astype(BF16)

        x = x_ref[...]
        gate = jnp.dot(x, wgb[...], preferred_element_type=F32)
        up = jnp.dot(x, wub[...], preferred_element_type=F32)
        act = (jax.nn.silu(gate) * up).astype(BF16)
        o_ref[...] = jnp.dot(act, wdb[...], preferred_element_type=F32).astype(o_ref.dtype)

    @pl.when(i >= n_used)
    def _():
        o_ref[...] = jnp.zeros(o_ref.shape, o_ref.dtype)


def _experts(x_sorted, tables, wg, wu, wd, layer, n_blk):
    hbm = pl.BlockSpec(memory_space=pl.ANY)
    grid_spec = pltpu.PrefetchScalarGridSpec(
        num_scalar_prefetch=5,
        grid=(n_blk,),
        in_specs=[pl.BlockSpec((MOE_BM, D_MODEL), lambda i, be, slot, n1, n2, nu: (jnp.minimum(i, nu[0] - 1), 0)),
                  hbm, hbm, hbm],
        out_specs=pl.BlockSpec((MOE_BM, D_MODEL), lambda i, *_: (i, 0)),
        scratch_shapes=[
            pltpu.VMEM((WEIGHT_SLOTS, D_MODEL, D_EXPERT), F32),
            pltpu.VMEM((WEIGHT_SLOTS, D_MODEL, D_EXPERT), F32),
            pltpu.VMEM((WEIGHT_SLOTS, D_EXPERT, D_MODEL), F32),
            pltpu.VMEM((D_MODEL, D_EXPERT), BF16),
            pltpu.VMEM((D_MODEL, D_EXPERT), BF16),
            pltpu.VMEM((D_EXPERT, D_MODEL), BF16),
            pltpu.SemaphoreType.DMA((WEIGHT_SLOTS,)),
        ],
    )
    return pl.pallas_call(
        functools.partial(_expert_kernel, layer=layer),
        grid_spec=grid_spec,
        out_shape=jax.ShapeDtypeStruct((n_blk * MOE_BM, D_MODEL), MOE_DTYPE),
        compiler_params=_params(("arbitrary",)),
        name="moe_experts",
    )(tables["blk_expert"], tables["blk_slot"], tables["blk_next1"], tables["blk_next2"], tables["n_used"],
      x_sorted, wg, wu, wd)


def _combine_kernel(base_ref, loc_ref, units_ref, tile_units_ref, outtab_ref, h_ref, wt_ref, lrow_ref, g_ref, y_hbm,
                    o_ref, ybuf, sem, *final_scratch, final_norm):
    j = pl.program_id(0)
    n = pl.num_programs(0)
    slot = j % 2
    tt, rt = h_ref.shape[0], ybuf.shape[1]
    assert rt % LANES == 0

    @pl.when(j == 0)
    def _():
        ybuf[...] = jnp.zeros(ybuf.shape, ybuf.dtype)
        _segment_copies(0, y_hbm, ybuf.at[0], sem.at[0], False, base_ref, loc_ref, units_ref)

    @pl.when(j + 1 < n)
    def _():
        _segment_copies(j + 1, y_hbm, ybuf.at[1 - slot], sem.at[1 - slot], False, base_ref, loc_ref, units_ref)

    _wait_unit_copies(tile_units_ref[j], y_hbm, ybuf.at[slot], sem.at[slot])
    lrow = lrow_ref[...]
    wt = wt_ref[...]
    col = lax.broadcasted_iota(jnp.int32, (tt, rt), 1)
    is0 = col == lrow[:, 0:1]
    is1 = col == lrow[:, 1:2]
    w_of_row = jnp.sum(jnp.where(is0, wt[:, 0:1], 0.0) + jnp.where(is1, wt[:, 1:2], 0.0), axis=0, keepdims=True)
    w_rows = jnp.transpose(jnp.broadcast_to(w_of_row, (LANES, rt)))
    w_rows = jnp.tile(w_rows, (1, OUT_TN // LANES))
    pick = (is0 | is1).astype(BF16)
    if final_norm:
        out_hbm, (obuf, osem) = o_ref, final_scratch
        tile_out = obuf.at[slot]

        def range_copies(tile, slot_, wait):
            for r in range(2):
                k = (tile * 2 + r) * 3
                unit0, n_units, row0 = outtab_ref[k], outtab_ref[k + 1], outtab_ref[k + 2]

                def one(u, carry):
                    src = obuf.at[slot_, pl.ds(pl.multiple_of((unit0 + u) * 8, 8), 8), :]
                    dst = out_hbm.at[pl.ds(pl.multiple_of(row0 + u * 8, 8), 8), :]
                    copy = pltpu.make_async_copy(src, dst, osem.at[slot_])
                    copy.wait() if wait else copy.start()
                    return carry

                lax.fori_loop(0, n_units, one, 0)

        @pl.when(j >= 2)
        def _():
            range_copies(j - 2, slot, True)
    else:
        tile_out = o_ref
    for c in range(D_MODEL // OUT_TN):
        cols = slice(c * OUT_TN, (c + 1) * OUT_TN)
        y = (ybuf[slot, :, cols].astype(F32) * w_rows).astype(BF16)
        tile_out[:, cols] = h_ref[:, cols] + jnp.dot(pick, y, preferred_element_type=F32)
    if final_norm:
        tile_out[...] = _rms_norm(tile_out[...], g_ref[...])
        range_copies(j, slot, False)

        @pl.when(j == n - 1)
        def _():
            @pl.when(j >= 1)
            def _():
                range_copies(jnp.maximum(j - 1, 0), 1 - slot, True)

            range_copies(j, slot, True)


def _kept_row_runs(n_tiles, tt, lp):
    drop = PAD + N_META
    table = []
    for j in range(n_tiles):
        runs, u = [], 0
        while u < tt // 8:
            p = j * tt + u * 8
            if p % lp >= drop:
                b = p // lp
                n = (min((b + 1) * lp, (j + 1) * tt) - p) // 8
                runs.append((u, n, p - (b + 1) * drop))
                u += n
            else:
                u += 1
        assert len(runs) <= 2
        runs += [(0, 0, 0)] * (2 - len(runs))
        table += [v for run in runs for v in run]
    return jnp.asarray(table, jnp.int32)


def _combine(h, wts, lrows, y_sorted, tables, g, tt, final_norm, lp):
    rows = h.shape[0]
    n_tiles = rows // tt
    rt = _moe_tile_rows(tt)
    scratch = [pltpu.VMEM((2, rt, D_MODEL), MOE_DTYPE), pltpu.SemaphoreType.DMA((2,))]
    if final_norm:
        out_rows = rows - (rows // lp) * (PAD + N_META)
        out_spec = pl.BlockSpec(memory_space=pl.ANY)
        scratch += [pltpu.VMEM((2, tt, D_MODEL), F32), pltpu.SemaphoreType.DMA((2,))]
        out_table = _kept_row_runs(n_tiles, tt, lp)
    else:
        out_rows = rows
        out_spec = pl.BlockSpec((tt, D_MODEL), lambda j, *_: (j, 0))
        out_table = jnp.zeros((1,), jnp.int32)
    grid_spec = pltpu.PrefetchScalarGridSpec(
        num_scalar_prefetch=5,
        grid=(n_tiles,),
        in_specs=[
            pl.BlockSpec((tt, D_MODEL), lambda j, *_: (j, 0)),
            pl.BlockSpec((tt, LANES), lambda j, *_: (j, 0)),
            pl.BlockSpec((tt, LANES), lambda j, *_: (j, 0)),
            pl.BlockSpec((1, D_MODEL), lambda j, *_: (0, 0)),
            pl.BlockSpec(memory_space=pl.ANY),
        ],
        out_specs=out_spec,
        scratch_shapes=scratch,
    )
    return pl.pallas_call(
        functools.partial(_combine_kernel, final_norm=final_norm),
        grid_spec=grid_spec,
        out_shape=jax.ShapeDtypeStruct((out_rows, D_MODEL), F32),
        compiler_params=_params(("arbitrary",)),
        name="moe_combine",
    )(tables["base"], tables["loc"], tables["units"], tables["tile_units"], out_table, h, wts, lrows, g, y_sorted)


def _moe(h, hn, wts, ids, ranks, tile_cnt, wg, wu, wd, layer, final_g, lp):
    rows = h.shape[0]
    n_tiles = tile_cnt.shape[0]
    tt = rows // n_tiles
    n_rows = 2 * rows + n_tiles * N_EXPERTS * (MOE_UNIT - 1) + N_EXPERTS * (MOE_BM - 1)
    n_blk = -(-n_rows // MOE_BM)
    tables = _moe_tables(tile_cnt, n_blk)
    x_sorted, lrows = _dispatch(hn, ids, ranks, tables, tt, n_blk * MOE_BM)
    y_sorted = _experts(x_sorted, tables, wg, wu, wd, layer, n_blk)
    final_norm = final_g is not None
    g = final_g if final_norm else jnp.ones((1, D_MODEL), F32)
    return _combine(h, wts, lrows, y_sorted, tables, g, tt, final_norm, lp)


def _rope_tables(batch, lp):
    half = HEAD_DIM // 2
    inv = ROPE_THETA ** (-jnp.arange(half, dtype=F32) / half)
    pos = (jnp.arange(lp, dtype=jnp.int32) - PAD).astype(F32)
    ang = pos[:, None] * inv[None, :]
    cos = jnp.tile(jnp.cos(ang), (batch, LANES // half))
    sin = jnp.sin(ang)
    sin = jnp.tile(jnp.concatenate([-sin, sin], axis=1), (batch, LANES // HEAD_DIM))
    return cos, sin


def kernel(x, meta_tokens, attn_norm_g, ffn_norm_g, w_in, swa_sinks, ssm_lambda_re, ssm_lambda_im, ssm_b_re, ssm_b_im, ssm_c_re, ssm_c_im, ssm_d, ssm_log_dt, ssm_w_glu, diff_lambda_q1, diff_lambda_k1, diff_lambda_q2, diff_lambda_k2, diff_subln_g, w_out, router_group_w, router_group_b, router_expert_w, router_expert_b, moe_w_gate, moe_w_up, moe_w_down, final_norm_g):
    batch, seq, d = x.shape
    depth = w_in.shape[0]
    lp = PAD + N_META + seq
    nb = lp // BLK
    meta = jnp.broadcast_to(meta_tokens[None].astype(x.dtype), (batch, N_META, d))
    h = jnp.concatenate([jnp.zeros((batch, PAD, d), x.dtype), meta, x], axis=1).reshape(batch * lp, d)
    cos, sin = _rope_tables(batch, lp)
    w_in_bf, w_out_bf, w_glu_bf = w_in.astype(BF16), w_out.astype(BF16), ssm_w_glu.astype(BF16)
    for l in range(depth):
        att, us, qd, kd, vd = _norm_inproj(h, attn_norm_g[l][None], w_in_bf, l, cos, sin)
        o_a = _swa(att, swa_sinks[l].astype(F32), batch, nb)
        kc, bc, cc, a_t = _ssm_operators(ssm_lambda_re[l], ssm_lambda_im[l], ssm_b_re[l], ssm_b_im[l],
                                         ssm_c_re[l], ssm_c_im[l], ssm_log_dt[l])
        yg = _ssm(us, kc, bc, cc, a_t, ssm_d[l].astype(F32).reshape(SSM_PACKS, 1, LANES), batch, lp)
        lam_init = 0.8 - 0.6 * math.exp(-0.3 * l)
        lam_vecs = jnp.stack([diff_lambda_q1[l], diff_lambda_k1[l], diff_lambda_q2[l], diff_lambda_k2[l]]).astype(F32)
        o_c = _diff_attn(qd, kd, vd, lam_vecs, diff_subln_g[l][None].astype(F32), lam_init, batch, nb)
        rw = jnp.concatenate([router_group_w[l], router_expert_w[l]], axis=1).astype(F32)
        rw = jnp.pad(rw, ((0, 0), (0, LANES - rw.shape[1])))
        rb = jnp.concatenate([router_group_b[l], router_expert_b[l]]).astype(F32)
        rb = jnp.pad(rb, (0, LANES - rb.shape[0]))[None]
        rw_hi, rw_lo = _split_bf16(rw)
        g_ffn = ffn_norm_g[l][None].astype(F32)
        h, hn, wts, ids, ranks, tile_cnt = _mixer_out(h, o_a, yg, o_c, w_out_bf, w_glu_bf, l, g_ffn, rw_hi, rw_lo, rb)
        final_g = final_norm_g[None].astype(F32) if l == depth - 1 else None
        h = _moe(h, hn, wts, ids, ranks, tile_cnt, moe_w_gate, moe_w_up, moe_w_down, l, final_g, lp)
    return h.reshape(batch, seq, d)
```

```python
import functools
import math

import jax
import jax.numpy as jnp
from jax import lax
from jax.experimental import pallas as pl
from jax.experimental.pallas import tpu as pltpu

D_MODEL = 2048
N_META = 16
BLK = 128
PAD = BLK - N_META
ROPE_THETA = 10000.0
NORM_EPS = 1e-5
NEG = -1e30
SWA_HEADS = 12
SWA_KV_HEADS = 4
HEAD_DIM = 64
SWA_WIDTH = SWA_HEADS * HEAD_DIM
SSM_WIDTH = 768
SSM_GROUP = 16
SSM_GROUPS = SSM_WIDTH // SSM_GROUP
SSM_STATE = 64
DIFF_HEADS = 4
DIFF_V_DIM = 128
DIFF_WIDTH = DIFF_HEADS * DIFF_V_DIM
IN_WIDTH = 3584
N_EXPERT_GROUPS = 4
EXPERTS_PER_GROUP = 8
N_EXPERTS = N_EXPERT_GROUPS * EXPERTS_PER_GROUP
D_EXPERT = 512

COL_KA = 768
COL_VA = 1024
COL_US = 1280
COL_QD = 2048
COL_KD = 2560
COL_VD = 3072

LANES = 128
SSM_T = 16
SSM_PACK = LANES // SSM_GROUP
SSM_PACKS = SSM_GROUPS // SSM_PACK
SSM_PSTATE = SSM_PACK * SSM_STATE
MOE_BM = 256
VMEM_LIMIT = 56 * 1024 * 1024

F32 = jnp.float32
BF16 = jnp.bfloat16


def _row_tile(rows):
    for t in (512, 256, 128):
        if rows % t == 0:
            return t
    raise ValueError(f"row count {rows} is not a multiple of 128")


def _params(sem, vmem=VMEM_LIMIT):
    return pltpu.CompilerParams(dimension_semantics=sem, vmem_limit_bytes=vmem)


def _rms_norm(x, g):
    return x * lax.rsqrt(jnp.mean(x * x, axis=-1, keepdims=True) + NORM_EPS) * g


IN_TN = 512
ROPE_TILES = (0, 1, 4, 5)


def _norm_inproj_kernel(x_ref, g_ref, w_ref, cos_ref, sin_ref, att_ref, us_ref, qd_ref, kd_ref, vd_ref, xn_ref):
    blocked = ((COL_VD, vd_ref), (COL_KD, kd_ref), (COL_QD, qd_ref), (COL_US, us_ref))

    def store(col, piece):
        for start, ref in blocked:
            if col >= start:
                ref[(col - start) // LANES] = piece
                return
        att_ref[:, col:col + LANES] = piece

    x = x_ref[...]
    ms = jnp.mean(x * x, axis=-1, keepdims=True)
    xn_ref[...] = (x * lax.rsqrt(ms + NORM_EPS) * g_ref[...]).astype(BF16)
    shape = (x.shape[0], IN_TN)
    lane = lax.broadcasted_iota(jnp.int32, shape, 1)
    first_half = (lane & (HEAD_DIM - 1)) < (HEAD_DIM // 2)
    reps = IN_TN // LANES
    cos = jnp.tile(cos_ref[...], (1, reps))
    sin = jnp.tile(sin_ref[...], (1, reps))
    for j in range(IN_WIDTH // IN_TN):
        cols = slice(j * IN_TN, (j + 1) * IN_TN)
        acc = jnp.dot(xn_ref[...], w_ref[:, cols], preferred_element_type=F32)
        if j in ROPE_TILES:
            partner = jnp.where(first_half, pltpu.roll(acc, IN_TN - HEAD_DIM // 2, 1),
                                pltpu.roll(acc, HEAD_DIM // 2, 1))
            acc = acc * cos + partner * sin
        acc = acc.astype(BF16)
        for c in range(IN_TN // LANES):
            store(j * IN_TN + c * LANES, acc[:, c * LANES:(c + 1) * LANES])


def _norm_inproj(h, g, w, layer, cos, sin):
    rows = h.shape[0]
    tm = _row_tile(rows)

    def blocked(width):
        n = width // LANES
        return pl.BlockSpec((n, tm, LANES), lambda i: (0, i, 0)), jax.ShapeDtypeStruct((n, rows, LANES), BF16)

    specs, shapes = zip(
        (pl.BlockSpec((tm, COL_US), lambda i: (i, 0)), jax.ShapeDtypeStruct((rows, COL_US), BF16)),
        blocked(SSM_WIDTH), blocked(DIFF_WIDTH), blocked(DIFF_WIDTH), blocked(DIFF_WIDTH))
    return pl.pallas_call(
        _norm_inproj_kernel,
        grid=(rows // tm,),
        in_specs=[
            pl.BlockSpec((tm, D_MODEL), lambda i: (i, 0)),
            pl.BlockSpec((1, D_MODEL), lambda i: (0, 0)),
            pl.BlockSpec((None, D_MODEL, IN_WIDTH), lambda i: (layer, 0, 0)),
            pl.BlockSpec((tm, LANES), lambda i: (i, 0)),
            pl.BlockSpec((tm, LANES), lambda i: (i, 0)),
        ],
        out_specs=list(specs),
        out_shape=list(shapes),
        scratch_shapes=[pltpu.VMEM((tm, D_MODEL), BF16)],
        compiler_params=_params(("parallel",)),
        name="norm_inproj",
    )(h, g, w, cos, sin)


def _swa_kernel(sink_ref, q_ref, k_ref, v_ref, o_ref, *, nb):
    shape = (BLK, 2 * BLK)
    qi = lax.broadcasted_iota(jnp.int32, shape, 0)
    c = lax.broadcasted_iota(jnp.int32, shape, 1)
    delta = qi + BLK - c
    in_window = (delta >= 0) & (delta < BLK)
    grp = SWA_HEADS // SWA_KV_HEADS

    def one_block(n, kv_rows):
        kj = (n - 1) * BLK + c
        ok = in_window & (kj >= PAD)
        q_rows = pl.ds(pl.multiple_of(n * BLK, BLK), BLK)
        outs = []
        for j in range(SWA_KV_HEADS):
            cols = slice(j * HEAD_DIM, (j + 1) * HEAD_DIM)
            if kv_rows is None:
                kk = jnp.concatenate([k_ref[0:BLK, cols]] * 2, axis=0)
                vv = jnp.concatenate([v_ref[0:BLK, cols]] * 2, axis=0)
            else:
                kk = k_ref[kv_rows, cols]
                vv = v_ref[kv_rows, cols]
            for g in range(grp):
                hd = j * grp + g
                qh = q_ref[q_rows, hd * HEAD_DIM:(hd + 1) * HEAD_DIM] * (HEAD_DIM ** -0.5)
                s = lax.dot_general(qh, kk, (((1,), (1,)), ((), ())), preferred_element_type=F32)
                s = jnp.where(ok, s, NEG)
                sink = sink_ref[hd]
                m = jnp.maximum(jnp.max(s, axis=-1, keepdims=True), sink)
                p = jnp.exp(s - m)
                denom = jnp.sum(p, axis=-1, keepdims=True) + jnp.exp(sink - m)
                outs.append(jnp.dot(p.astype(BF16), vv, preferred_element_type=F32) * (1.0 / denom))
        o_ref[q_rows, :] = jnp.concatenate(outs, axis=1).astype(o_ref.dtype)

    one_block(0, None)

    def body(n, carry):
        one_block(n, pl.ds(pl.multiple_of((n - 1) * BLK, BLK), 2 * BLK))
        return carry

    lax.fori_loop(1, nb, body, 0)


def _swa(proj, sinks, batch, nb):
    rows = proj.shape[0]
    lp = nb * BLK
    kvw = SWA_KV_HEADS * HEAD_DIM
    return pl.pallas_call(
        functools.partial(_swa_kernel, nb=nb),
        grid=(batch,),
        in_specs=[
            pl.BlockSpec(memory_space=pltpu.SMEM),
            pl.BlockSpec((lp, SWA_WIDTH), lambda b: (b, 0)),
            pl.BlockSpec((lp, kvw), lambda b: (b, COL_KA // kvw)),
            pl.BlockSpec((lp, kvw), lambda b: (b, COL_VA // kvw)),
        ],
        out_specs=pl.BlockSpec((lp, SWA_WIDTH), lambda b: (b, 0)),
        out_shape=jax.ShapeDtypeStruct((rows, SWA_WIDTH), BF16),
        compiler_params=_params(("parallel",)),
        name="swa",
    )(sinks, proj, proj, proj)


DIFF_QBLOCKS = 2


def _diff_kernel(lam_ref, g_ref, q_ref, k_ref, v_ref, o_ref, *, lam_init, nb):
    lane = lax.broadcasted_iota(jnp.int32, (BLK, LANES), 1)
    lv = lam_ref[...]
    lam = (jnp.exp(jnp.sum(lv[0:1] * lv[1:2], axis=-1, keepdims=True))
           - jnp.exp(jnp.sum(lv[2:3] * lv[3:4], axis=-1, keepdims=True)) + lam_init)
    gain = g_ref[...] * (1.0 - lam_init)
    scale = HEAD_DIM ** -0.5
    for n0 in range(0, nb, DIFF_QBLOCKS):
        blocks = range(n0, min(n0 + DIFF_QBLOCKS, nb))
        ke = (blocks[-1] + 1) * BLK
        kx = k_ref[0:ke, :]
        vx = v_ref[0:ke, :]
        rows = 2 * BLK * len(blocks)
        pieces = []
        for n in blocks:
            q = q_ref[n * BLK:(n + 1) * BLK, :] * scale
            zero = jnp.zeros_like(q)
            pieces += [jnp.where(lane < HEAD_DIM, q, zero), jnp.where(lane >= HEAD_DIM, q, zero)]
        s = lax.dot_general(jnp.concatenate(pieces, axis=0), kx, (((1,), (1,)), ((), ())),
                            preferred_element_type=F32)
        r = lax.broadcasted_iota(jnp.int32, (rows, BLK), 0)
        qpos = (n0 + jnp.right_shift(r, (2 * BLK).bit_length() - 1)) * BLK + (r & (BLK - 1))
        col = lax.broadcasted_iota(jnp.int32, (rows, BLK), 1)
        key_blocks = []
        for t in range(blocks[-1] + 1):
            blk = s[:, t * BLK:(t + 1) * BLK]
            ok = None
            if t >= n0:
                ok = t * BLK + col <= qpos
            if t == 0:
                ok = (col >= PAD) if ok is None else ok & (col >= PAD)
            key_blocks.append(blk if ok is None else jnp.where(ok, blk, NEG))
        s = jnp.concatenate(key_blocks, axis=1)
        m = jnp.max(s, axis=-1, keepdims=True)
        p = jnp.exp(s - m)
        denom = jnp.sum(p, axis=-1, keepdims=True)
        heads = jnp.dot(p.astype(BF16), vx, preferred_element_type=F32) * (1.0 / denom)
        for i, n in enumerate(blocks):
            o = heads[2 * i * BLK:(2 * i + 1) * BLK] - lam * heads[(2 * i + 1) * BLK:(2 * i + 2) * BLK]
            o = o * lax.rsqrt(jnp.mean(o * o, axis=-1, keepdims=True) + NORM_EPS) * gain
            o_ref[n * BLK:(n + 1) * BLK, :] = o.astype(o_ref.dtype)


def _diff_attn(qd, kd, vd, lam_vecs, subln_g, lam_init, batch, nb):
    rows = qd.shape[1]
    lp = nb * BLK
    head_rows = pl.BlockSpec((None, lp, LANES), lambda b, h: (h, b, 0))
    return pl.pallas_call(
        functools.partial(_diff_kernel, lam_init=lam_init, nb=nb),
        grid=(batch, DIFF_HEADS),
        in_specs=[
            pl.BlockSpec((4, HEAD_DIM), lambda b, h: (0, 0)),
            pl.BlockSpec((1, DIFF_V_DIM), lambda b, h: (0, 0)),
            head_rows, head_rows, head_rows,
        ],
        out_specs=head_rows,
        out_shape=jax.ShapeDtypeStruct((DIFF_HEADS, rows, DIFF_V_DIM), BF16),
        compiler_params=_params(("parallel", "parallel")),
        name="diff_attn",
    )(lam_vecs, subln_g, qd, kd, vd)


def _ssm_row_chunk(rows):
    for c in (544, 512, 384, 256, 128, 64, 32, 16):
        if rows % c == 0:
            return c
    raise ValueError(rows)


def _ssm_expand_operators(kc_ref, bc_ref, cc_ref, k_ref, bw_ref, cw_ref):
    def iota(shape, dim):
        return lax.broadcasted_iota(jnp.int32, shape, dim)

    def group(idx, width):
        return jnp.right_shift(idx, width.bit_length() - 1) & (SSM_PACK - 1)

    hs, ps, tl = SSM_GROUP, SSM_STATE, SSM_T * LANES
    spread_h = ((iota((LANES, hs), 0) & (hs - 1)) == iota((LANES, hs), 1)).astype(BF16)
    kt = jnp.dot(spread_h, kc_ref[...], preferred_element_type=F32)
    same = group(iota((LANES, tl), 0), hs) == group(iota((LANES, tl), 1), hs)
    k_ref[...] = jnp.where(same, kt, 0.0).astype(BF16)
    spread_s = (iota((ps, SSM_PSTATE), 0) == (iota((ps, SSM_PSTATE), 1) & (ps - 1))).astype(BF16)
    shape = (tl, SSM_PSTATE)
    same = group(iota(shape, 0), hs) == group(iota(shape, 1), ps)
    for part in range(2):
        bw = jnp.dot(bc_ref[part], spread_s, preferred_element_type=F32)
        bw_ref[:, part * SSM_PSTATE:(part + 1) * SSM_PSTATE] = jnp.where(same, bw, 0.0).astype(BF16)
    shape = (LANES, SSM_PSTATE)
    same = group(iota(shape, 0), hs) == group(iota(shape, 1), ps)
    for j in range(SSM_T):
        for part in range(2):
            ct = jnp.dot(spread_h, cc_ref[part, j], preferred_element_type=F32)
            cw_ref[j, :, part * SSM_PSTATE:(part + 1) * SSM_PSTATE] = jnp.where(same, ct, 0.0).astype(BF16)


def _ssm_kernel(u_ref, kc_ref, bc_ref, cc_ref, a_ref, d_ref, o_ref, up_ref, y_ref, e_ref, s_ref, k_ref, bw_ref,
                cw_ref, *, batch, lp):
    rows = batch * lp
    n_chunks = lp // SSM_T
    tot_chunks = rows // SSM_T
    _ssm_expand_operators(kc_ref, bc_ref, cc_ref, k_ref, bw_ref, cw_ref)
    contract_lanes = (((1,), (1,)), ((), ()))
    up_ref[0:SSM_T, :] = jnp.zeros((SSM_T, LANES), F32)
    up_ref[SSM_T:, :] = u_ref[...].astype(F32)
    for b in range(batch):
        up_ref[SSM_T + b * lp:SSM_T + b * lp + PAD, :] = jnp.zeros((PAD, LANES), F32)

    ch = _ssm_row_chunk(rows)
    rmod = lax.broadcasted_iota(jnp.int32, (ch, LANES), 0) & (SSM_T - 1)
    for r0 in range(0, rows, ch):
        pieces = []
        for k in range(SSM_T):
            sl = up_ref[SSM_T - k + r0:SSM_T - k + r0 + ch, :]
            pieces.append(jnp.where(rmod >= k, sl, 0.0).astype(BF16))
        y_ref[r0:r0 + ch, :] = lax.dot_general(jnp.concatenate(pieces, axis=1), k_ref[...], contract_lanes,
                                               preferred_element_type=F32)

    pieces = [up_ref[pl.ds(SSM_T + i, tot_chunks, stride=SSM_T), :].astype(BF16) for i in range(SSM_T)]
    e_ref[...] = jnp.dot(jnp.concatenate(pieces, axis=1), bw_ref[...], preferred_element_type=F32)

    a_re = a_ref[:, :SSM_PSTATE]
    a_im = a_ref[:, SSM_PSTATE:]

    def step(c, carry):
        new = []
        for b in range(batch):
            sr, si = carry[b]
            idx = b * n_chunks + c
            s_ref[pl.ds(idx, 1), :SSM_PSTATE] = sr
            s_ref[pl.ds(idx, 1), SSM_PSTATE:] = si
            er = e_ref[pl.ds(idx, 1), :SSM_PSTATE]
            ei = e_ref[pl.ds(idx, 1), SSM_PSTATE:]
            new.append((a_re * sr - a_im * si + er, a_re * si + a_im * sr + ei))
        return tuple(new)

    zero = jnp.zeros((1, SSM_PSTATE), F32)
    lax.fori_loop(0, n_chunks, step, tuple((zero, zero) for _ in range(batch)))

    sb = s_ref[...].astype(BF16)
    for j in range(SSM_T):
        z = lax.dot_general(sb, cw_ref[j], contract_lanes, preferred_element_type=F32)
        idx = pl.ds(j, tot_chunks, stride=SSM_T)
        y_ref[idx, :] = y_ref[idx, :] + z

    y = y_ref[...] + d_ref[...] * up_ref[SSM_T:, :]
    o_ref[...] = jax.nn.gelu(y, approximate=True).astype(o_ref.dtype)


def _ssm(us, kc, bc, cc, a_t, dskip, batch, lp):
    rows = us.shape[1]
    tot_chunks = rows // SSM_T
    return pl.pallas_call(
        functools.partial(_ssm_kernel, batch=batch, lp=lp),
        grid=(SSM_PACKS,),
        in_specs=[
            pl.BlockSpec((None, rows, LANES), lambda p: (p, 0, 0)),
            pl.BlockSpec((None, SSM_GROUP, SSM_T * LANES), lambda p: (p, 0, 0)),
            pl.BlockSpec((None, 2, SSM_T * LANES, SSM_STATE), lambda p: (p, 0, 0, 0)),
            pl.BlockSpec((None, 2, SSM_T, SSM_GROUP, SSM_PSTATE), lambda p: (p, 0, 0, 0, 0)),
            pl.BlockSpec((None, 1, 2 * SSM_PSTATE), lambda p: (p, 0, 0)),
            pl.BlockSpec((None, 1, LANES), lambda p: (p, 0, 0)),
        ],
        out_specs=pl.BlockSpec((None, rows, LANES), lambda p: (p, 0, 0)),
        out_shape=jax.ShapeDtypeStruct((SSM_PACKS, rows, LANES), F32),
        scratch_shapes=[
            pltpu.VMEM((rows + SSM_T, LANES), F32),
            pltpu.VMEM((rows, LANES), F32),
            pltpu.VMEM((tot_chunks, 2 * SSM_PSTATE), F32),
            pltpu.VMEM((tot_chunks, 2 * SSM_PSTATE), F32),
            pltpu.VMEM((LANES, SSM_T * LANES), BF16),
            pltpu.VMEM((SSM_T * LANES, 2 * SSM_PSTATE), BF16),
            pltpu.VMEM((SSM_T, LANES, 2 * SSM_PSTATE), BF16),
        ],
        compiler_params=_params(("parallel",)),
        name="ssm",
    )(us, kc, bc, cc, a_t, dskip)


def _ssm_operators(lam_re, lam_im, b_re, b_im, c_re, c_im, log_dt):
    p, h = SSM_STATE, SSM_GROUP
    dt = jnp.exp(log_dt.astype(F32))[:, None]
    lr, li = lam_re.astype(F32), lam_im.astype(F32)
    mag = jnp.exp(lr * dt)
    abar_re, abar_im = mag * jnp.cos(li * dt), mag * jnp.sin(li * dt)
    den = lr * lr + li * li
    nr = abar_re - 1.0
    f_re = (nr * lr + abar_im * li) / den
    f_im = (abar_im * lr - nr * li) / den
    br, bi = b_re.astype(F32), b_im.astype(F32)
    bbar_re = f_re[..., None] * br - f_im[..., None] * bi
    bbar_im = f_re[..., None] * bi + f_im[..., None] * br
    ks = jnp.arange(SSM_T + 1, dtype=F32)[:, None, None]
    pmag = jnp.exp(ks * (lr * dt)[None])
    pw_re = pmag * jnp.cos(ks * (li * dt)[None])
    pw_im = pmag * jnp.sin(ks * (li * dt)[None])
    cr, ci = c_re.astype(F32), c_im.astype(F32)

    packs, q = SSM_PACKS, SSM_PACK

    cb_re = cr[:, :, :, None] * bbar_re[:, None, :, :] - ci[:, :, :, None] * bbar_im[:, None, :, :]
    cb_im = cr[:, :, :, None] * bbar_im[:, None, :, :] + ci[:, :, :, None] * bbar_re[:, None, :, :]
    klag = (jnp.einsum('kgs,gosi->kgoi', pw_re[:SSM_T], cb_re)
            - jnp.einsum('kgs,gosi->kgoi', pw_im[:SSM_T], cb_im))
    kc = klag.reshape(SSM_T, packs, q, h, h).transpose(1, 3, 0, 2, 4).reshape(packs, h, SSM_T * LANES)

    rev_re, rev_im = pw_re[SSM_T - 1::-1, :, None, :], pw_im[SSM_T - 1::-1, :, None, :]
    bt_re, bt_im = jnp.swapaxes(bbar_re, 1, 2)[None], jnp.swapaxes(bbar_im, 1, 2)[None]

    def inj_by_pack(x):
        return x.reshape(SSM_T, packs, q * h, p).transpose(1, 0, 2, 3).reshape(packs, SSM_T * LANES, p)

    bc = jnp.stack([inj_by_pack(rev_re * bt_re - rev_im * bt_im), inj_by_pack(rev_re * bt_im + rev_im * bt_re)], axis=1)

    ca_re = cr[None] * pw_re[1:, :, None, :] - ci[None] * pw_im[1:, :, None, :]
    ca_im = cr[None] * pw_im[1:, :, None, :] + ci[None] * pw_re[1:, :, None, :]

    def out_by_pack(x):
        return x.reshape(SSM_T, packs, q, h, p).transpose(1, 0, 3, 2, 4).reshape(packs, SSM_T, h, SSM_PSTATE)

    cc = jnp.stack([out_by_pack(ca_re), out_by_pack(-ca_im)], axis=1)

    a_t = jnp.concatenate([pw_re[SSM_T].reshape(packs, 1, SSM_PSTATE),
                           pw_im[SSM_T].reshape(packs, 1, SSM_PSTATE)], axis=-1)
    return kc.astype(BF16), bc.astype(BF16), cc.astype(BF16), a_t


OUT_TN = 512


def _split_bf16(x):
    hi = x.astype(BF16)
    return hi, (x - hi.astype(F32)).astype(BF16)


def _route(hn, rwh_ref, rwl_ref, rb_ref):
    hi, lo = _split_bf16(hn)
    logits = (jnp.dot(hi, rwh_ref[...], preferred_element_type=F32)
              + (jnp.dot(lo, rwh_ref[...], preferred_element_type=F32)
                 + jnp.dot(hi, rwl_ref[...], preferred_element_type=F32))) + rb_ref[...]
    lane = lax.broadcasted_iota(jnp.int32, logits.shape, 1)

    def first_argmax(v, vmax):
        return jnp.min(jnp.where(v == vmax, lane, LANES), axis=-1, keepdims=True)

    gl = jnp.where(lane < N_EXPERT_GROUPS, logits, NEG)
    gmax = jnp.max(gl, axis=-1, keepdims=True)
    g_sel = first_argmax(gl, gmax)
    p_g = 1.0 / jnp.sum(jnp.exp(gl - gmax), axis=-1, keepdims=True)
    e_lane = lane - N_EXPERT_GROUPS
    in_group = (e_lane >= 0) & (e_lane < N_EXPERTS) & (jnp.right_shift(e_lane, 3) == g_sel)
    el = jnp.where(in_group, logits, NEG)
    m1 = jnp.max(el, axis=-1, keepdims=True)
    i1 = first_argmax(el, m1)
    el2 = jnp.where(lane == i1, NEG, el)
    m2 = jnp.max(el2, axis=-1, keepdims=True)
    i2 = first_argmax(el2, m2)
    z = jnp.sum(jnp.exp(el - m1), axis=-1, keepdims=True)
    p1 = 1.0 / z
    p2 = jnp.exp(m2 - m1) / z
    w1 = p1 / (p1 + p2) * p_g
    w2 = p2 / (p1 + p2) * p_g
    wts = jnp.where(lane == 0, w1, jnp.where(lane == 1, w2, 0.0))
    ids = jnp.where(lane == 0, i1 - N_EXPERT_GROUPS, jnp.where(lane == 1, i2 - N_EXPERT_GROUPS, 0))
    return wts, ids


def _assignment_ranks(ids):
    tm = ids.shape[0]
    lane = lax.broadcasted_iota(jnp.int32, ids.shape, 1)
    oh0 = lane == ids[:, 0:1]
    oh1 = lane == ids[:, 1:2]
    hits = (oh0 | oh1).astype(BF16)
    r = lax.broadcasted_iota(jnp.int32, (tm, tm), 0)
    c = lax.broadcasted_iota(jnp.int32, (tm, tm), 1)
    before = jnp.dot((c < r).astype(BF16), hits, preferred_element_type=F32)
    rank0 = jnp.sum(jnp.where(oh0, before, 0.0), axis=-1, keepdims=True)
    rank1 = jnp.sum(jnp.where(oh1, before, 0.0), axis=-1, keepdims=True)
    ranks = jnp.where(lane == 0, rank0, jnp.where(lane == 1, rank1, 0.0)).astype(jnp.int32)
    return ranks, jnp.sum(hits.astype(F32), axis=0, keepdims=True).astype(jnp.int32)


def _mixer_out_kernel(h_ref, a_ref, y_ref, c_ref, wo_ref, wglu_ref, g_ref, rwh_ref, rwl_ref, rb_ref,
                      o_ref, hn_ref, wt_ref, id_ref, rank_ref, cnt_ref):
    half = h_ref.shape[0] // 2
    halves = [slice(r * half, (r + 1) * half) for r in range(2)]
    for rows in halves:
        y = jnp.concatenate([y_ref[p, rows, :] for p in range(y_ref.shape[0])], axis=1)
        c = jnp.concatenate([c_ref[hd, rows, :] for hd in range(c_ref.shape[0])], axis=1)
        gate = jnp.dot(y.astype(BF16), wglu_ref[...], preferred_element_type=F32)
        b = (y * jax.nn.sigmoid(gate)).astype(BF16)
        mixed = jnp.concatenate([a_ref[rows, :], b, c], axis=1)
        for j in range(D_MODEL // OUT_TN):
            cols = slice(j * OUT_TN, (j + 1) * OUT_TN)
            o_ref[rows, cols] = h_ref[rows, cols] + jnp.dot(mixed, wo_ref[:, cols], preferred_element_type=F32)
    for rows in halves:
        hn = _rms_norm(o_ref[rows, :], g_ref[...])
        hn_ref[rows, :] = hn.astype(BF16)
        wts, ids = _route(hn, rwh_ref, rwl_ref, rb_ref)
        wt_ref[rows, :] = wts
        id_ref[rows, :] = ids
    ranks, counts = _assignment_ranks(id_ref[...])
    rank_ref[...] = ranks
    cnt_ref[...] = jnp.broadcast_to(counts, cnt_ref.shape)


def _mixer_out(h, oa, yg, oc, wo, wglu, layer, g, rw_hi, rw_lo, rb):
    rows = h.shape[0]
    tm = _row_tile(rows)

    def layer_weight(shape):
        return pl.BlockSpec((None,) + shape, lambda i: (layer, 0, 0))

    def row_spec(width):
        return pl.BlockSpec((tm, width), lambda i: (i, 0))

    def whole(shape):
        return pl.BlockSpec(shape, lambda i: (0, 0))

    def blocked(width):
        return pl.BlockSpec((width // LANES, tm, LANES), lambda i: (0, i, 0))

    return pl.pallas_call(
        _mixer_out_kernel,
        grid=(rows // tm,),
        in_specs=[
            row_spec(D_MODEL), row_spec(SWA_WIDTH), blocked(SSM_WIDTH), blocked(DIFF_WIDTH),
            layer_weight((D_MODEL, D_MODEL)), layer_weight((SSM_WIDTH, SSM_WIDTH)), whole((1, D_MODEL)),
            whole((D_MODEL, LANES)), whole((D_MODEL, LANES)), whole((1, LANES)),
        ],
        out_specs=[row_spec(D_MODEL), row_spec(D_MODEL), row_spec(LANES), row_spec(LANES), row_spec(LANES),
                   pl.BlockSpec((None, 8, LANES), lambda i: (i, 0, 0))],
        out_shape=[
            jax.ShapeDtypeStruct((rows, D_MODEL), F32),
            jax.ShapeDtypeStruct((rows, D_MODEL), BF16),
            jax.ShapeDtypeStruct((rows, LANES), F32),
            jax.ShapeDtypeStruct((rows, LANES), jnp.int32),
            jax.ShapeDtypeStruct((rows, LANES), jnp.int32),
            jax.ShapeDtypeStruct((rows // tm, 8, LANES), jnp.int32),
        ],
        compiler_params=_params(("parallel",)),
        name="mixer_out",
    )(h, oa, yg, oc, wo, wglu, g, rw_hi, rw_lo, rb)


MOE_UNIT = 16
MOE_DTYPE = BF16


def _moe_tile_rows(tt):
    return 2 * tt + N_EXPERTS * MOE_UNIT


def _moe_tables(tile_cnt, n_blk):
    cnt = tile_cnt[:, 0, :N_EXPERTS]
    seg = (cnt + MOE_UNIT - 1) // MOE_UNIT * MOE_UNIT
    size = jnp.sum(seg, axis=0)
    padded = (size + MOE_BM - 1) // MOE_BM * MOE_BM
    pend = jnp.cumsum(padded)
    pstart = pend - padded
    base = pstart[None, :] + jnp.cumsum(seg, axis=0) - seg
    loc = jnp.cumsum(seg, axis=1) - seg
    units = seg // MOE_UNIT
    blk_start = jnp.arange(n_blk, dtype=jnp.int32) * MOE_BM
    blk_expert = jnp.minimum(jnp.sum((pend[None, :] <= blk_start[:, None]).astype(jnp.int32), axis=1), N_EXPERTS - 1)
    ex = jnp.arange(N_EXPERTS, dtype=jnp.int32)
    nonempty = size > 0
    run_slot = (jnp.cumsum(nonempty.astype(jnp.int32)) - 1) % WEIGHT_SLOTS
    later = (ex[None, :] > ex[:, None]) & nonempty[None, :]
    nxt = jnp.min(jnp.where(later, ex[None, :], N_EXPERTS), axis=1)
    nxt = jnp.where(nxt == N_EXPERTS, -1, nxt)
    nxt2 = jnp.where(nxt >= 0, nxt[jnp.maximum(nxt, 0)], -1)
    i32 = jnp.int32
    loc_lanes = jnp.pad(loc, ((0, 0), (0, LANES - N_EXPERTS)))
    return dict(
        base=base.reshape(-1).astype(i32), loc=loc.reshape(-1).astype(i32), units=units.reshape(-1).astype(i32),
        tile_units=jnp.sum(units, axis=1).astype(i32),
        tail_start=(pstart + size).astype(i32), tail_units=((padded - size) // MOE_UNIT).astype(i32),
        loc_lanes=jnp.broadcast_to(loc_lanes[:, None, :], (loc.shape[0], 8, LANES)).astype(i32),
        blk_expert=blk_expert.astype(i32), blk_slot=run_slot[blk_expert].astype(i32),
        blk_next1=nxt[blk_expert].astype(i32), blk_next2=nxt2[blk_expert].astype(i32),
        n_used=(pend[-1] // MOE_BM).astype(i32).reshape(1))


def _segment_copies(tile, hbm, buf, sem, to_hbm, base_ref, loc_ref, units_ref):
    for e in range(N_EXPERTS):
        k = tile * N_EXPERTS + e
        row_hbm, row_buf = base_ref[k], loc_ref[k]

        def start(u, carry):
            a = hbm.at[pl.ds(pl.multiple_of(row_hbm + u * MOE_UNIT, MOE_UNIT), MOE_UNIT), :]
            b = buf.at[pl.ds(pl.multiple_of(row_buf + u * MOE_UNIT, MOE_UNIT), MOE_UNIT), :]
            (pltpu.make_async_copy(b, a, sem) if to_hbm else pltpu.make_async_copy(a, b, sem)).start()
            return carry

        lax.fori_loop(0, units_ref[k], start, 0)


def _wait_unit_copies(count, hbm, buf, sem):
    def wait(u, carry):
        pltpu.make_async_copy(hbm.at[pl.ds(0, MOE_UNIT), :], buf.at[pl.ds(0, MOE_UNIT), :], sem).wait()
        return carry

    lax.fori_loop(0, count, wait, 0)


def _local_rows(ids, ranks, loc_lanes):
    lane = lax.broadcasted_iota(jnp.int32, ids.shape, 1)
    loc = loc_lanes.astype(F32)
    start0 = jnp.sum(jnp.where(lane == ids[:, 0:1], loc, 0.0), axis=-1, keepdims=True).astype(jnp.int32)
    start1 = jnp.sum(jnp.where(lane == ids[:, 1:2], loc, 0.0), axis=-1, keepdims=True).astype(jnp.int32)
    return start0 + ranks[:, 0:1], start1 + ranks[:, 1:2]


def _dispatch_kernel(base_ref, loc_ref, units_ref, tile_units_ref, tail_start_ref, tail_units_ref, nused_ref,
                     hn_ref, ids_ref, rank_ref, locv_ref, x_hbm, lrow_ref, ybuf, zbuf, sem):
    j = pl.program_id(0)
    n = pl.num_programs(0)
    slot = j % 2
    tt, rt = hn_ref.shape[0], ybuf.shape[1]

    @pl.when(j >= 2)
    def _():
        _wait_unit_copies(tile_units_ref[j - 2], x_hbm, ybuf.at[slot], sem.at[slot])

    row0, row1 = _local_rows(ids_ref[...], rank_ref[...], locv_ref[0:1, :])
    lane = lax.broadcasted_iota(jnp.int32, lrow_ref.shape, 1)
    lrow_ref[...] = jnp.where(lane == 0, row0, jnp.where(lane == 1, row1, 0))
    col = lax.broadcasted_iota(jnp.int32, (tt, rt), 1)
    onehot = ((col == row0) | (col == row1)).astype(BF16)
    compacted = lax.dot_general(onehot, hn_ref[...], (((0,), (0,)), ((), ())), preferred_element_type=F32)
    ybuf[slot] = compacted.astype(ybuf.dtype)
    _segment_copies(j, x_hbm, ybuf.at[slot], sem.at[slot], True, base_ref, loc_ref, units_ref)

    @pl.when(j == n - 1)
    def _():
        zbuf[...] = jnp.zeros(zbuf.shape, zbuf.dtype)
        zunit = zbuf.at[pl.ds(0, MOE_UNIT), :]
        total = 0
        for e in range(N_EXPERTS):
            row = tail_start_ref[e]

            def start(u, carry):
                dst = x_hbm.at[pl.ds(pl.multiple_of(row + u * MOE_UNIT, MOE_UNIT), MOE_UNIT), :]
                pltpu.make_async_copy(zunit, dst, sem.at[2]).start()
                return carry

            lax.fori_loop(0, tail_units_ref[e], start, 0)
            total = total + tail_units_ref[e]
        _wait_unit_copies(total, x_hbm, zunit, sem.at[2])
        n_blk = x_hbm.shape[0] // MOE_BM

        def block_copy(blk):
            dst = x_hbm.at[pl.ds(pl.multiple_of(blk * MOE_BM, MOE_BM), MOE_BM), :]
            return pltpu.make_async_copy(zbuf, dst, sem.at[3])

        def start_block(blk, carry):
            block_copy(blk).start()
            return carry

        def wait_block(blk, carry):
            block_copy(blk).wait()
            return carry

        lax.fori_loop(nused_ref[0], n_blk, start_block, 0)
        lax.fori_loop(nused_ref[0], n_blk, wait_block, 0)

        @pl.when(j >= 1)
        def _():
            _wait_unit_copies(tile_units_ref[jnp.maximum(j - 1, 0)], x_hbm, ybuf.at[1 - slot], sem.at[1 - slot])

        _wait_unit_copies(tile_units_ref[j], x_hbm, ybuf.at[slot], sem.at[slot])


def _dispatch(hn, ids, ranks, tables, tt, n_rows):
    rows = hn.shape[0]
    rt = _moe_tile_rows(tt)
    row_spec = pl.BlockSpec((tt, LANES), lambda j, *_: (j, 0))
    grid_spec = pltpu.PrefetchScalarGridSpec(
        num_scalar_prefetch=7,
        grid=(rows // tt,),
        in_specs=[pl.BlockSpec((tt, D_MODEL), lambda j, *_: (j, 0)), row_spec, row_spec,
                  pl.BlockSpec((None, 8, LANES), lambda j, *_: (j, 0, 0))],
        out_specs=[pl.BlockSpec(memory_space=pl.ANY), row_spec],
        scratch_shapes=[pltpu.VMEM((2, rt, D_MODEL), MOE_DTYPE), pltpu.VMEM((MOE_BM, D_MODEL), MOE_DTYPE),
                        pltpu.SemaphoreType.DMA((4,))],
    )
    return pl.pallas_call(
        _dispatch_kernel,
        grid_spec=grid_spec,
        out_shape=[jax.ShapeDtypeStruct((n_rows, D_MODEL), MOE_DTYPE),
                   jax.ShapeDtypeStruct((rows, LANES), jnp.int32)],
        compiler_params=_params(("arbitrary",)),
        name="moe_dispatch",
    )(tables["base"], tables["loc"], tables["units"], tables["tile_units"], tables["tail_start"],
      tables["tail_units"], tables["n_used"], hn, ids, ranks, tables["loc_lanes"])


WEIGHT_DMA_PRIORITIES = (1, 0, 1)
WEIGHT_SLOTS = 3


def _expert_kernel(be_ref, slot_ref, nxt1_ref, nxt2_ref, nused_ref, x_ref, wg_hbm, wu_hbm, wd_hbm, o_ref,
                   wgf, wuf, wdf, wgb, wub, wdb, wsem, *, layer):
    i = pl.program_id(0)
    n_used = nused_ref[0]

    def weight_copies(expert, wslot):
        pairs = ((wg_hbm, wgf), (wu_hbm, wuf), (wd_hbm, wdf))
        return [pltpu.make_async_copy(w.at[layer, expert], buf.at[wslot], wsem.at[wslot]) for w, buf in pairs]

    def start_weights(expert, wslot):
        for c, priority in zip(weight_copies(expert, wslot), WEIGHT_DMA_PRIORITIES):
            c.start(priority=priority)

    @pl.when(i == 0)
    def _():
        start_weights(be_ref[0], 0)

        @pl.when(nxt1_ref[0] >= 0)
        def _():
            start_weights(nxt1_ref[0], 1)

    @pl.when(i < n_used)
    def _():
        @pl.when((i == 0) | (be_ref[i] != be_ref[jnp.maximum(i - 1, 0)]))
        def _():
            wslot = slot_ref[i]
            for c in weight_copies(be_ref[i], wslot):
                c.wait()

            @pl.when(nxt2_ref[i] >= 0)
            def _():
                start_weights(nxt2_ref[i], lax.rem(wslot + 2, WEIGHT_SLOTS))

            wgb[...] = wgf[wslot].astype(BF16)
            wub[...] = wuf[wslot].astype(BF16)
            wdb[...] = wdf[wslot].astype(BF16)

        x = x_ref[...]
        gate = jnp.dot(x, wgb[...], preferred_element_type=F32)
        up = jnp.dot(x, wub[...], preferred_element_type=F32)
        act = (jax.nn.silu(gate) * up).astype(BF16)
        o_ref[...] = jnp.dot(act, wdb[...], preferred_element_type=F32).astype(o_ref.dtype)

    @pl.when(i >= n_used)
    def _():
        o_ref[...] = jnp.zeros(o_ref.shape, o_ref.dtype)


def _experts(x_sorted, tables, wg, wu, wd, layer, n_blk):
    hbm = pl.BlockSpec(memory_space=pl.ANY)
    grid_spec = pltpu.PrefetchScalarGridSpec(
        num_scalar_prefetch=5,
        grid=(n_blk,),
        in_specs=[pl.BlockSpec((MOE_BM, D_MODEL), lambda i, be, slot, n1, n2, nu: (jnp.minimum(i, nu[0] - 1), 0)),
                  hbm, hbm, hbm],
        out_specs=pl.BlockSpec((MOE_BM, D_MODEL), lambda i, *_: (i, 0)),
        scratch_shapes=[
            pltpu.VMEM((WEIGHT_SLOTS, D_MODEL, D_EXPERT), F32),
            pltpu.VMEM((WEIGHT_SLOTS, D_MODEL, D_EXPERT), F32),
            pltpu.VMEM((WEIGHT_SLOTS, D_EXPERT, D_MODEL), F32),
            pltpu.VMEM((D_MODEL, D_EXPERT), BF16),
            pltpu.VMEM((D_MODEL, D_EXPERT), BF16),
            pltpu.VMEM((D_EXPERT, D_MODEL), BF16),
            pltpu.SemaphoreType.DMA((WEIGHT_SLOTS,)),
        ],
    )
    return pl.pallas_call(
        functools.partial(_expert_kernel, layer=layer),
        grid_spec=grid_spec,
        out_shape=jax.ShapeDtypeStruct((n_blk * MOE_BM, D_MODEL), MOE_DTYPE),
        compiler_params=_params(("arbitrary",)),
        name="moe_experts",
    )(tables["blk_expert"], tables["blk_slot"], tables["blk_next1"], tables["blk_next2"], tables["n_used"],
      x_sorted, wg, wu, wd)


def _combine_kernel(base_ref, loc_ref, units_ref, tile_units_ref, outtab_ref, h_ref, wt_ref, lrow_ref, g_ref, y_hbm,
                    o_ref, ybuf, sem, *final_scratch, final_norm):
    j = pl.program_id(0)
    n = pl.num_programs(0)
    slot = j % 2
    tt, rt = h_ref.shape[0], ybuf.shape[1]
    assert rt % LANES == 0

    @pl.when(j == 0)
    def _():
        ybuf[...] = jnp.zeros(ybuf.shape, ybuf.dtype)
        _segment_copies(0, y_hbm, ybuf.at[0], sem.at[0], False, base_ref, loc_ref, units_ref)

    @pl.when(j + 1 < n)
    def _():
        _segment_copies(j + 1, y_hbm, ybuf.at[1 - slot], sem.at[1 - slot], False, base_ref, loc_ref, units_ref)

    _wait_unit_copies(tile_units_ref[j], y_hbm, ybuf.at[slot], sem.at[slot])
    lrow = lrow_ref[...]
    wt = wt_ref[...]
    col = lax.broadcasted_iota(jnp.int32, (tt, rt), 1)
    is0 = col == lrow[:, 0:1]
    is1 = col == lrow[:, 1:2]
    w_of_row = jnp.sum(jnp.where(is0, wt[:, 0:1], 0.0) + jnp.where(is1, wt[:, 1:2], 0.0), axis=0, keepdims=True)
    w_rows = jnp.transpose(jnp.broadcast_to(w_of_row, (LANES, rt)))
    w_rows = jnp.tile(w_rows, (1, OUT_TN // LANES))
    pick = (is0 | is1).astype(BF16)
    if final_norm:
        out_hbm, (obuf, osem) = o_ref, final_scratch
        tile_out = obuf.at[slot]

        def range_copies(tile, slot_, wait):
            for r in range(2):
                k = (tile * 2 + r) * 3
                unit0, n_units, row0 = outtab_ref[k], outtab_ref[k + 1], outtab_ref[k + 2]

                def one(u, carry):
                    src = obuf.at[slot_, pl.ds(pl.multiple_of((unit0 + u) * 8, 8), 8), :]
                    dst = out_hbm.at[pl.ds(pl.multiple_of(row0 + u * 8, 8), 8), :]
                    copy = pltpu.make_async_copy(src, dst, osem.at[slot_])
                    copy.wait() if wait else copy.start()
                    return carry

                lax.fori_loop(0, n_units, one, 0)

        @pl.when(j >= 2)
        def _():
            range_copies(j - 2, slot, True)
    else:
        tile_out = o_ref
    for c in range(D_MODEL // OUT_TN):
        cols = slice(c * OUT_TN, (c + 1) * OUT_TN)
        y = (ybuf[slot, :, cols].astype(F32) * w_rows).astype(BF16)
        tile_out[:, cols] = h_ref[:, cols] + jnp.dot(pick, y, preferred_element_type=F32)
    if final_norm:
        tile_out[...] = _rms_norm(tile_out[...], g_ref[...])
        range_copies(j, slot, False)

        @pl.when(j == n - 1)
        def _():
            @pl.when(j >= 1)
            def _():
                range_copies(jnp.maximum(j - 1, 0), 1 - slot, True)

            range_copies(j, slot, True)


def _kept_row_runs(n_tiles, tt, lp):
    drop = PAD + N_META
    table = []
    for j in range(n_tiles):
        runs, u = [], 0
        while u < tt // 8:
            p = j * tt + u * 8
            if p % lp >= drop:
                b = p // lp
                n = (min((b + 1) * lp, (j + 1) * tt) - p) // 8
                runs.append((u, n, p - (b + 1) * drop))
                u += n
            else:
                u += 1
        assert len(runs) <= 2
        runs += [(0, 0, 0)] * (2 - len(runs))
        table += [v for run in runs for v in run]
    return jnp.asarray(table, jnp.int32)


def _combine(h, wts, lrows, y_sorted, tables, g, tt, final_norm, lp):
    rows = h.shape[0]
    n_tiles = rows // tt
    rt = _moe_tile_rows(tt)
    scratch = [pltpu.VMEM((2, rt, D_MODEL), MOE_DTYPE), pltpu.SemaphoreType.DMA((2,))]
    if final_norm:
        out_rows = rows - (rows // lp) * (PAD + N_META)
        out_spec = pl.BlockSpec(memory_space=pl.ANY)
        scratch += [pltpu.VMEM((2, tt, D_MODEL), F32), pltpu.SemaphoreType.DMA((2,))]
        out_table = _kept_row_runs(n_tiles, tt, lp)
    else:
        out_rows = rows
        out_spec = pl.BlockSpec((tt, D_MODEL), lambda j, *_: (j, 0))
        out_table = jnp.zeros((1,), jnp.int32)
    grid_spec = pltpu.PrefetchScalarGridSpec(
        num_scalar_prefetch=5,
        grid=(n_tiles,),
        in_specs=[
            pl.BlockSpec((tt, D_MODEL), lambda j, *_: (j, 0)),
            pl.BlockSpec((tt, LANES), lambda j, *_: (j, 0)),
            pl.BlockSpec((tt, LANES), lambda j, *_: (j, 0)),
            pl.BlockSpec((1, D_MODEL), lambda j, *_: (0, 0)),
            pl.BlockSpec(memory_space=pl.ANY),
        ],
        out_specs=out_spec,
        scratch_shapes=scratch,
    )
    return pl.pallas_call(
        functools.partial(_combine_kernel, final_norm=final_norm),
        grid_spec=grid_spec,
        out_shape=jax.ShapeDtypeStruct((out_rows, D_MODEL), F32),
        compiler_params=_params(("arbitrary",)),
        name="moe_combine",
    )(tables["base"], tables["loc"], tables["units"], tables["tile_units"], out_table, h, wts, lrows, g, y_sorted)


def _moe(h, hn, wts, ids, ranks, tile_cnt, wg, wu, wd, layer, final_g, lp):
    rows = h.shape[0]
    n_tiles = tile_cnt.shape[0]
    tt = rows // n_tiles
    n_rows = 2 * rows + n_tiles * N_EXPERTS * (MOE_UNIT - 1) + N_EXPERTS * (MOE_BM - 1)
    n_blk = -(-n_rows // MOE_BM)
    tables = _moe_tables(tile_cnt, n_blk)
    x_sorted, lrows = _dispatch(hn, ids, ranks, tables, tt, n_blk * MOE_BM)
    y_sorted = _experts(x_sorted, tables, wg, wu, wd, layer, n_blk)
    final_norm = final_g is not None
    g = final_g if final_norm else jnp.ones((1, D_MODEL), F32)
    return _combine(h, wts, lrows, y_sorted, tables, g, tt, final_norm, lp)


def _rope_tables(batch, lp):
    half = HEAD_DIM // 2
    inv = ROPE_THETA ** (-jnp.arange(half, dtype=F32) / half)
    pos = (jnp.arange(lp, dtype=jnp.int32) - PAD).astype(F32)
    ang = pos[:, None] * inv[None, :]
    cos = jnp.tile(jnp.cos(ang), (batch, LANES // half))
    sin = jnp.sin(ang)
    sin = jnp.tile(jnp.concatenate([-sin, sin], axis=1), (batch, LANES // HEAD_DIM))
    return cos, sin


def kernel(x, meta_tokens, attn_norm_g, ffn_norm_g, w_in, swa_sinks, ssm_lambda_re, ssm_lambda_im, ssm_b_re, ssm_b_im, ssm_c_re, ssm_c_im, ssm_d, ssm_log_dt, ssm_w_glu, diff_lambda_q1, diff_lambda_k1, diff_lambda_q2, diff_lambda_k2, diff_subln_g, w_out, router_group_w, router_group_b, router_expert_w, router_expert_b, moe_w_gate, moe_w_up, moe_w_down, final_norm_g):
    batch, seq, d = x.shape
    depth = w_in.shape[0]
    lp = PAD + N_META + seq
    nb = lp // BLK
    meta = jnp.broadcast_to(meta_tokens[None].astype(x.dtype), (batch, N_META, d))
    h = jnp.concatenate([jnp.zeros((batch, PAD, d), x.dtype), meta, x], axis=1).reshape(batch * lp, d)
    cos, sin = _rope_tables(batch, lp)
    w_in_bf, w_out_bf, w_glu_bf = w_in.astype(BF16), w_out.astype(BF16), ssm_w_glu.astype(BF16)
    for l in range(depth):
        att, us, qd, kd, vd = _norm_inproj(h, attn_norm_g[l][None], w_in_bf, l, cos, sin)
        o_a = _swa(att, swa_sinks[l].astype(F32), batch, nb)
        kc, bc, cc, a_t = _ssm_operators(ssm_lambda_re[l], ssm_lambda_im[l], ssm_b_re[l], ssm_b_im[l],
                                         ssm_c_re[l], ssm_c_im[l], ssm_log_dt[l])
        yg = _ssm(us, kc, bc, cc, a_t, ssm_d[l].astype(F32).reshape(SSM_PACKS, 1, LANES), batch, lp)
        lam_init = 0.8 - 0.6 * math.exp(-0.3 * l)
        lam_vecs = jnp.stack([diff_lambda_q1[l], diff_lambda_k1[l], diff_lambda_q2[l], diff_lambda_k2[l]]).astype(F32)
        o_c = _diff_attn(qd, kd, vd, lam_vecs, diff_subln_g[l][None].astype(F32), lam_init, batch, nb)
        rw = jnp.concatenate([router_group_w[l], router_expert_w[l]], axis=1).astype(F32)
        rw = jnp.pad(rw, ((0, 0), (0, LANES - rw.shape[1])))
        rb = jnp.concatenate([router_group_b[l], router_expert_b[l]]).astype(F32)
        rb = jnp.pad(rb, (0, LANES - rb.shape[0]))[None]
        rw_hi, rw_lo = _split_bf16(rw)
        g_ffn = ffn_norm_g[l][None].astype(F32)
        h, hn, wts, ids, ranks, tile_cnt = _mixer_out(h, o_a, yg, o_c, w_out_bf, w_glu_bf, l, g_ffn, rw_hi, rw_lo, rb)
        final_g = final_norm_g[None].astype(F32) if l == depth - 1 else None
        h = _moe(h, hn, wts, ids, ranks, tile_cnt, moe_w_gate, moe_w_up, moe_w_down, l, final_g, lp)
    return h.reshape(batch, seq, d)
```

```python
import functools
import math

import jax
import jax.numpy as jnp
from jax import lax
from jax.experimental import pallas as pl
from jax.experimental.pallas import tpu as pltpu

D_MODEL = 2048
N_META = 16
BLK = 128
PAD = BLK - N_META
ROPE_THETA = 10000.0
NORM_EPS = 1e-5
NEG = -1e30
SWA_HEADS = 12
SWA_KV_HEADS = 4
HEAD_DIM = 64
SWA_WIDTH = SWA_HEADS * HEAD_DIM
SSM_WIDTH = 768
SSM_GROUP = 16
SSM_GROUPS = SSM_WIDTH // SSM_GROUP
SSM_STATE = 64
DIFF_HEADS = 4
DIFF_V_DIM = 128
DIFF_WIDTH = DIFF_HEADS * DIFF_V_DIM
IN_WIDTH = 3584
N_EXPERT_GROUPS = 4
EXPERTS_PER_GROUP = 8
N_EXPERTS = N_EXPERT_GROUPS * EXPERTS_PER_GROUP
D_EXPERT = 512

COL_KA = 768
COL_VA = 1024
COL_US = 1280
COL_QD = 2048
COL_KD = 2560
COL_VD = 3072

LANES = 128
SSM_T = 8
SSM_PACK = LANES // SSM_GROUP
SSM_PACKS = SSM_GROUPS // SSM_PACK
SSM_PSTATE = SSM_PACK * SSM_STATE
MOE_BM = 256
VMEM_LIMIT = 56 * 1024 * 1024

F32 = jnp.float32
BF16 = jnp.bfloat16


def _row_tile(rows):
    for t in (512, 256, 128):
        if rows % t == 0:
            return t
    raise ValueError(f"row count {rows} is not a multiple of 128")


def _params(sem, vmem=VMEM_LIMIT):
    return pltpu.CompilerParams(dimension_semantics=sem, vmem_limit_bytes=vmem)


def _rms_norm(x, g):
    return x * lax.rsqrt(jnp.mean(x * x, axis=-1, keepdims=True) + NORM_EPS) * g


IN_TN = 512
ROPE_TILES = (0, 1, 4, 5)


def _norm_inproj_kernel(x_ref, g_ref, w_ref, cos_ref, sin_ref, att_ref, us_ref, qd_ref, kd_ref, vd_ref, xn_ref):
    blocked = ((COL_VD, vd_ref), (COL_KD, kd_ref), (COL_QD, qd_ref), (COL_US, us_ref))

    def store(col, piece):
        for start, ref in blocked:
            if col >= start:
                ref[(col - start) // LANES] = piece
                return
        att_ref[:, col:col + LANES] = piece

    x = x_ref[...]
    ms = jnp.mean(x * x, axis=-1, keepdims=True)
    xn_ref[...] = (x * lax.rsqrt(ms + NORM_EPS) * g_ref[...]).astype(BF16)
    shape = (x.shape[0], IN_TN)
    lane = lax.broadcasted_iota(jnp.int32, shape, 1)
    first_half = (lane & (HEAD_DIM - 1)) < (HEAD_DIM // 2)
    reps = IN_TN // LANES
    cos = jnp.tile(cos_ref[...], (1, reps))
    sin = jnp.tile(sin_ref[...], (1, reps))
    for j in range(IN_WIDTH // IN_TN):
        cols = slice(j * IN_TN, (j + 1) * IN_TN)
        acc = jnp.dot(xn_ref[...], w_ref[:, cols], preferred_element_type=F32)
        if j in ROPE_TILES:
            partner = jnp.where(first_half, pltpu.roll(acc, IN_TN - HEAD_DIM // 2, 1),
                                pltpu.roll(acc, HEAD_DIM // 2, 1))
            acc = acc * cos + partner * sin
        acc = acc.astype(BF16)
        for c in range(IN_TN // LANES):
            store(j * IN_TN + c * LANES, acc[:, c * LANES:(c + 1) * LANES])


def _norm_inproj(h, g, w, layer, cos, sin):
    rows = h.shape[0]
    tm = _row_tile(rows)

    def blocked(width):
        n = width // LANES
        return pl.BlockSpec((n, tm, LANES), lambda i: (0, i, 0)), jax.ShapeDtypeStruct((n, rows, LANES), BF16)

    specs, shapes = zip(
        (pl.BlockSpec((tm, COL_US), lambda i: (i, 0)), jax.ShapeDtypeStruct((rows, COL_US), BF16)),
        blocked(SSM_WIDTH), blocked(DIFF_WIDTH), blocked(DIFF_WIDTH), blocked(DIFF_WIDTH))
    return pl.pallas_call(
        _norm_inproj_kernel,
        grid=(rows // tm,),
        in_specs=[
            pl.BlockSpec((tm, D_MODEL), lambda i: (i, 0)),
            pl.BlockSpec((1, D_MODEL), lambda i: (0, 0)),
            pl.BlockSpec((None, D_MODEL, IN_WIDTH), lambda i: (layer, 0, 0)),
            pl.BlockSpec((tm, LANES), lambda i: (i, 0)),
            pl.BlockSpec((tm, LANES), lambda i: (i, 0)),
        ],
        out_specs=list(specs),
        out_shape=list(shapes),
        scratch_shapes=[pltpu.VMEM((tm, D_MODEL), BF16)],
        compiler_params=_params(("parallel",)),
        name="norm_inproj",
    )(h, g, w, cos, sin)


def _swa_kernel(sink_ref, q_ref, k_ref, v_ref, o_ref, *, nb):
    shape = (BLK, 2 * BLK)
    qi = lax.broadcasted_iota(jnp.int32, shape, 0)
    c = lax.broadcasted_iota(jnp.int32, shape, 1)
    delta = qi + BLK - c
    in_window = (delta >= 0) & (delta < BLK)
    grp = SWA_HEADS // SWA_KV_HEADS

    def one_block(n, kv_rows):
        kj = (n - 1) * BLK + c
        ok = in_window & (kj >= PAD)
        q_rows = pl.ds(pl.multiple_of(n * BLK, BLK), BLK)
        outs = []
        for j in range(SWA_KV_HEADS):
            cols = slice(j * HEAD_DIM, (j + 1) * HEAD_DIM)
            if kv_rows is None:
                kk = jnp.concatenate([k_ref[0:BLK, cols]] * 2, axis=0)
                vv = jnp.concatenate([v_ref[0:BLK, cols]] * 2, axis=0)
            else:
                kk = k_ref[kv_rows, cols]
                vv = v_ref[kv_rows, cols]
            for g in range(grp):
                hd = j * grp + g
                qh = q_ref[q_rows, hd * HEAD_DIM:(hd + 1) * HEAD_DIM] * (HEAD_DIM ** -0.5)
                s = lax.dot_general(qh, kk, (((1,), (1,)), ((), ())), preferred_element_type=F32)
                s = jnp.where(ok, s, NEG)
                sink = sink_ref[hd]
                m = jnp.maximum(jnp.max(s, axis=-1, keepdims=True), sink)
                p = jnp.exp(s - m)
                denom = jnp.sum(p, axis=-1, keepdims=True) + jnp.exp(sink - m)
                outs.append(jnp.dot(p.astype(BF16), vv, preferred_element_type=F32) * (1.0 / denom))
        o_ref[q_rows, :] = jnp.concatenate(outs, axis=1).astype(o_ref.dtype)

    one_block(0, None)

    def body(n, carry):
        one_block(n, pl.ds(pl.multiple_of((n - 1) * BLK, BLK), 2 * BLK))
        return carry

    lax.fori_loop(1, nb, body, 0)


def _swa(proj, sinks, batch, nb):
    rows = proj.shape[0]
    lp = nb * BLK
    kvw = SWA_KV_HEADS * HEAD_DIM
    return pl.pallas_call(
        functools.partial(_swa_kernel, nb=nb),
        grid=(batch,),
        in_specs=[
            pl.BlockSpec(memory_space=pltpu.SMEM),
            pl.BlockSpec((lp, SWA_WIDTH), lambda b: (b, 0)),
            pl.BlockSpec((lp, kvw), lambda b: (b, COL_KA // kvw)),
            pl.BlockSpec((lp, kvw), lambda b: (b, COL_VA // kvw)),
        ],
        out_specs=pl.BlockSpec((lp, SWA_WIDTH), lambda b: (b, 0)),
        out_shape=jax.ShapeDtypeStruct((rows, SWA_WIDTH), BF16),
        compiler_params=_params(("parallel",)),
        name="swa",
    )(sinks, proj, proj, proj)


DIFF_QBLOCKS = 2


def _diff_kernel(lam_ref, g_ref, q_ref, k_ref, v_ref, o_ref, *, lam_init, nb):
    lane = lax.broadcasted_iota(jnp.int32, (BLK, LANES), 1)
    lv = lam_ref[...]
    lam = (jnp.exp(jnp.sum(lv[0:1] * lv[1:2], axis=-1, keepdims=True))
           - jnp.exp(jnp.sum(lv[2:3] * lv[3:4], axis=-1, keepdims=True)) + lam_init)
    gain = g_ref[...] * (1.0 - lam_init)
    scale = HEAD_DIM ** -0.5
    for n0 in range(0, nb, DIFF_QBLOCKS):
        blocks = range(n0, min(n0 + DIFF_QBLOCKS, nb))
        ke = (blocks[-1] + 1) * BLK
        kx = k_ref[0:ke, :]
        vx = v_ref[0:ke, :]
        rows = 2 * BLK * len(blocks)
        pieces = []
        for n in blocks:
            q = q_ref[n * BLK:(n + 1) * BLK, :] * scale
            zero = jnp.zeros_like(q)
            pieces += [jnp.where(lane < HEAD_DIM, q, zero), jnp.where(lane >= HEAD_DIM, q, zero)]
        s = lax.dot_general(jnp.concatenate(pieces, axis=0), kx, (((1,), (1,)), ((), ())),
                            preferred_element_type=F32)
        r = lax.broadcasted_iota(jnp.int32, (rows, BLK), 0)
        qpos = (n0 + jnp.right_shift(r, (2 * BLK).bit_length() - 1)) * BLK + (r & (BLK - 1))
        col = lax.broadcasted_iota(jnp.int32, (rows, BLK), 1)
        key_blocks = []
        for t in range(blocks[-1] + 1):
            blk = s[:, t * BLK:(t + 1) * BLK]
            ok = None
            if t >= n0:
                ok = t * BLK + col <= qpos
            if t == 0:
                ok = (col >= PAD) if ok is None else ok & (col >= PAD)
            key_blocks.append(blk if ok is None else jnp.where(ok, blk, NEG))
        s = jnp.concatenate(key_blocks, axis=1)
        m = jnp.max(s, axis=-1, keepdims=True)
        p = jnp.exp(s - m)
        denom = jnp.sum(p, axis=-1, keepdims=True)
        heads = jnp.dot(p.astype(BF16), vx, preferred_element_type=F32) * (1.0 / denom)
        for i, n in enumerate(blocks):
            o = heads[2 * i * BLK:(2 * i + 1) * BLK] - lam * heads[(2 * i + 1) * BLK:(2 * i + 2) * BLK]
            o = o * lax.rsqrt(jnp.mean(o * o, axis=-1, keepdims=True) + NORM_EPS) * gain
            o_ref[n * BLK:(n + 1) * BLK, :] = o.astype(o_ref.dtype)


def _diff_attn(qd, kd, vd, lam_vecs, subln_g, lam_init, batch, nb):
    rows = qd.shape[1]
    lp = nb * BLK
    head_rows = pl.BlockSpec((None, lp, LANES), lambda b, h: (h, b, 0))
    return pl.pallas_call(
        functools.partial(_diff_kernel, lam_init=lam_init, nb=nb),
        grid=(batch, DIFF_HEADS),
        in_specs=[
            pl.BlockSpec((4, HEAD_DIM), lambda b, h: (0, 0)),
            pl.BlockSpec((1, DIFF_V_DIM), lambda b, h: (0, 0)),
            head_rows, head_rows, head_rows,
        ],
        out_specs=head_rows,
        out_shape=jax.ShapeDtypeStruct((DIFF_HEADS, rows, DIFF_V_DIM), BF16),
        compiler_params=_params(("parallel", "parallel")),
        name="diff_attn",
    )(lam_vecs, subln_g, qd, kd, vd)


def _ssm_row_chunk(rows):
    for c in (544, 512, 384, 256, 128, 64, 32, 16):
        if rows % c == 0:
            return c
    raise ValueError(rows)


def _ssm_expand_operators(kc_ref, bc_ref, cc_ref, k_ref, bw_ref, cw_ref):
    def iota(shape, dim):
        return lax.broadcasted_iota(jnp.int32, shape, dim)

    def group(idx, width):
        return jnp.right_shift(idx, width.bit_length() - 1) & (SSM_PACK - 1)

    hs, ps, tl = SSM_GROUP, SSM_STATE, SSM_T * LANES
    spread_h = ((iota((LANES, hs), 0) & (hs - 1)) == iota((LANES, hs), 1)).astype(BF16)
    kt = jnp.dot(spread_h, kc_ref[...], preferred_element_type=F32)
    same = group(iota((LANES, tl), 0), hs) == group(iota((LANES, tl), 1), hs)
    k_ref[...] = jnp.where(same, kt, 0.0).astype(BF16)
    spread_s = (iota((ps, SSM_PSTATE), 0) == (iota((ps, SSM_PSTATE), 1) & (ps - 1))).astype(BF16)
    shape = (tl, SSM_PSTATE)
    same = group(iota(shape, 0), hs) == group(iota(shape, 1), ps)
    for part in range(2):
        bw = jnp.dot(bc_ref[part], spread_s, preferred_element_type=F32)
        bw_ref[:, part * SSM_PSTATE:(part + 1) * SSM_PSTATE] = jnp.where(same, bw, 0.0).astype(BF16)
    shape = (LANES, SSM_PSTATE)
    same = group(iota(shape, 0), hs) == group(iota(shape, 1), ps)
    for j in range(SSM_T):
        for part in range(2):
            ct = jnp.dot(spread_h, cc_ref[part, j], preferred_element_type=F32)
            cw_ref[j, :, part * SSM_PSTATE:(part + 1) * SSM_PSTATE] = jnp.where(same, ct, 0.0).astype(BF16)


def _ssm_kernel(u_ref, kc_ref, bc_ref, cc_ref, a_ref, d_ref, o_ref, up_ref, y_ref, e_ref, s_ref, k_ref, bw_ref,
                cw_ref, *, batch, lp):
    rows = batch * lp
    n_chunks = lp // SSM_T
    tot_chunks = rows // SSM_T
    _ssm_expand_operators(kc_ref, bc_ref, cc_ref, k_ref, bw_ref, cw_ref)
    contract_lanes = (((1,), (1,)), ((), ()))
    up_ref[0:SSM_T, :] = jnp.zeros((SSM_T, LANES), F32)
    up_ref[SSM_T:, :] = u_ref[...].astype(F32)
    for b in range(batch):
        up_ref[SSM_T + b * lp:SSM_T + b * lp + PAD, :] = jnp.zeros((PAD, LANES), F32)

    ch = _ssm_row_chunk(rows)
    rmod = lax.broadcasted_iota(jnp.int32, (ch, LANES), 0) & (SSM_T - 1)
    for r0 in range(0, rows, ch):
        pieces = []
        for k in range(SSM_T):
            sl = up_ref[SSM_T - k + r0:SSM_T - k + r0 + ch, :]
            pieces.append(jnp.where(rmod >= k, sl, 0.0).astype(BF16))
        y_ref[r0:r0 + ch, :] = lax.dot_general(jnp.concatenate(pieces, axis=1), k_ref[...], contract_lanes,
                                               preferred_element_type=F32)

    pieces = [up_ref[pl.ds(SSM_T + i, tot_chunks, stride=SSM_T), :].astype(BF16) for i in range(SSM_T)]
    e_ref[...] = jnp.dot(jnp.concatenate(pieces, axis=1), bw_ref[...], preferred_element_type=F32)

    a_re = a_ref[:, :SSM_PSTATE]
    a_im = a_ref[:, SSM_PSTATE:]

    def step(c, carry):
        new = []
        for b in range(batch):
            sr, si = carry[b]
            idx = b * n_chunks + c
            s_ref[pl.ds(idx, 1), :SSM_PSTATE] = sr
            s_ref[pl.ds(idx, 1), SSM_PSTATE:] = si
            er = e_ref[pl.ds(idx, 1), :SSM_PSTATE]
            ei = e_ref[pl.ds(idx, 1), SSM_PSTATE:]
            new.append((a_re * sr - a_im * si + er, a_re * si + a_im * sr + ei))
        return tuple(new)

    zero = jnp.zeros((1, SSM_PSTATE), F32)
    lax.fori_loop(0, n_chunks, step, tuple((zero, zero) for _ in range(batch)))

    sb = s_ref[...].astype(BF16)
    for j in range(SSM_T):
        z = lax.dot_general(sb, cw_ref[j], contract_lanes, preferred_element_type=F32)
        idx = pl.ds(j, tot_chunks, stride=SSM_T)
        y_ref[idx, :] = y_ref[idx, :] + z

    y = y_ref[...] + d_ref[...] * up_ref[SSM_T:, :]
    o_ref[...] = jax.nn.gelu(y, approximate=True).astype(o_ref.dtype)


def _ssm(us, kc, bc, cc, a_t, dskip, batch, lp):
    rows = us.shape[1]
    tot_chunks = rows // SSM_T
    return pl.pallas_call(
        functools.partial(_ssm_kernel, batch=batch, lp=lp),
        grid=(SSM_PACKS,),
        in_specs=[
            pl.BlockSpec((None, rows, LANES), lambda p: (p, 0, 0)),
            pl.BlockSpec((None, SSM_GROUP, SSM_T * LANES), lambda p: (p, 0, 0)),
            pl.BlockSpec((None, 2, SSM_T * LANES, SSM_STATE), lambda p: (p, 0, 0, 0)),
            pl.BlockSpec((None, 2, SSM_T, SSM_GROUP, SSM_PSTATE), lambda p: (p, 0, 0, 0, 0)),
            pl.BlockSpec((None, 1, 2 * SSM_PSTATE), lambda p: (p, 0, 0)),
            pl.BlockSpec((None, 1, LANES), lambda p: (p, 0, 0)),
        ],
        out_specs=pl.BlockSpec((None, rows, LANES), lambda p: (p, 0, 0)),
        out_shape=jax.ShapeDtypeStruct((SSM_PACKS, rows, LANES), F32),
        scratch_shapes=[
            pltpu.VMEM((rows + SSM_T, LANES), F32),
            pltpu.VMEM((rows, LANES), F32),
            pltpu.VMEM((tot_chunks, 2 * SSM_PSTATE), F32),
            pltpu.VMEM((tot_chunks, 2 * SSM_PSTATE), F32),
            pltpu.VMEM((LANES, SSM_T * LANES), BF16),
            pltpu.VMEM((SSM_T * LANES, 2 * SSM_PSTATE), BF16),
            pltpu.VMEM((SSM_T, LANES, 2 * SSM_PSTATE), BF16),
        ],
        compiler_params=_params(("parallel",)),
        name="ssm",
    )(us, kc, bc, cc, a_t, dskip)


def _ssm_operators(lam_re, lam_im, b_re, b_im, c_re, c_im, log_dt):
    p, h = SSM_STATE, SSM_GROUP
    dt = jnp.exp(log_dt.astype(F32))[:, None]
    lr, li = lam_re.astype(F32), lam_im.astype(F32)
    mag = jnp.exp(lr * dt)
    abar_re, abar_im = mag * jnp.cos(li * dt), mag * jnp.sin(li * dt)
    den = lr * lr + li * li
    nr = abar_re - 1.0
    f_re = (nr * lr + abar_im * li) / den
    f_im = (abar_im * lr - nr * li) / den
    br, bi = b_re.astype(F32), b_im.astype(F32)
    bbar_re = f_re[..., None] * br - f_im[..., None] * bi
    bbar_im = f_re[..., None] * bi + f_im[..., None] * br
    ks = jnp.arange(SSM_T + 1, dtype=F32)[:, None, None]
    pmag = jnp.exp(ks * (lr * dt)[None])
    pw_re = pmag * jnp.cos(ks * (li * dt)[None])
    pw_im = pmag * jnp.sin(ks * (li * dt)[None])
    cr, ci = c_re.astype(F32), c_im.astype(F32)

    packs, q = SSM_PACKS, SSM_PACK

    cb_re = cr[:, :, :, None] * bbar_re[:, None, :, :] - ci[:, :, :, None] * bbar_im[:, None, :, :]
    cb_im = cr[:, :, :, None] * bbar_im[:, None, :, :] + ci[:, :, :, None] * bbar_re[:, None, :, :]
    klag = (jnp.einsum('kgs,gosi->kgoi', pw_re[:SSM_T], cb_re)
            - jnp.einsum('kgs,gosi->kgoi', pw_im[:SSM_T], cb_im))
    kc = klag.reshape(SSM_T, packs, q, h, h).transpose(1, 3, 0, 2, 4).reshape(packs, h, SSM_T * LANES)

    rev_re, rev_im = pw_re[SSM_T - 1::-1, :, None, :], pw_im[SSM_T - 1::-1, :, None, :]
    bt_re, bt_im = jnp.swapaxes(bbar_re, 1, 2)[None], jnp.swapaxes(bbar_im, 1, 2)[None]

    def inj_by_pack(x):
        return x.reshape(SSM_T, packs, q * h, p).transpose(1, 0, 2, 3).reshape(packs, SSM_T * LANES, p)

    bc = jnp.stack([inj_by_pack(rev_re * bt_re - rev_im * bt_im), inj_by_pack(rev_re * bt_im + rev_im * bt_re)], axis=1)

    ca_re = cr[None] * pw_re[1:, :, None, :] - ci[None] * pw_im[1:, :, None, :]
    ca_im = cr[None] * pw_im[1:, :, None, :] + ci[None] * pw_re[1:, :, None, :]

    def out_by_pack(x):
        return x.reshape(SSM_T, packs, q, h, p).transpose(1, 0, 3, 2, 4).reshape(packs, SSM_T, h, SSM_PSTATE)

    cc = jnp.stack([out_by_pack(ca_re), out_by_pack(-ca_im)], axis=1)

    a_t = jnp.concatenate([pw_re[SSM_T].reshape(packs, 1, SSM_PSTATE),
                           pw_im[SSM_T].reshape(packs, 1, SSM_PSTATE)], axis=-1)
    return kc.astype(BF16), bc.astype(BF16), cc.astype(BF16), a_t


OUT_TN = 512


def _split_bf16(x):
    hi = x.astype(BF16)
    return hi, (x - hi.astype(F32)).astype(BF16)


def _route(hn, rwh_ref, rwl_ref, rb_ref):
    hi, lo = _split_bf16(hn)
    logits = (jnp.dot(hi, rwh_ref[...], preferred_element_type=F32)
              + (jnp.dot(lo, rwh_ref[...], preferred_element_type=F32)
                 + jnp.dot(hi, rwl_ref[...], preferred_element_type=F32))) + rb_ref[...]
    lane = lax.broadcasted_iota(jnp.int32, logits.shape, 1)

    def first_argmax(v, vmax):
        return jnp.min(jnp.where(v == vmax, lane, LANES), axis=-1, keepdims=True)

    gl = jnp.where(lane < N_EXPERT_GROUPS, logits, NEG)
    gmax = jnp.max(gl, axis=-1, keepdims=True)
    g_sel = first_argmax(gl, gmax)
    p_g = 1.0 / jnp.sum(jnp.exp(gl - gmax), axis=-1, keepdims=True)
    e_lane = lane - N_EXPERT_GROUPS
    in_group = (e_lane >= 0) & (e_lane < N_EXPERTS) & (jnp.right_shift(e_lane, 3) == g_sel)
    el = jnp.where(in_group, logits, NEG)
    m1 = jnp.max(el, axis=-1, keepdims=True)
    i1 = first_argmax(el, m1)
    el2 = jnp.where(lane == i1, NEG, el)
    m2 = jnp.max(el2, axis=-1, keepdims=True)
    i2 = first_argmax(el2, m2)
    z = jnp.sum(jnp.exp(el - m1), axis=-1, keepdims=True)
    p1 = 1.0 / z
    p2 = jnp.exp(m2 - m1) / z
    w1 = p1 / (p1 + p2) * p_g
    w2 = p2 / (p1 + p2) * p_g
    wts = jnp.where(lane == 0, w1, jnp.where(lane == 1, w2, 0.0))
    ids = jnp.where(lane == 0, i1 - N_EXPERT_GROUPS, jnp.where(lane == 1, i2 - N_EXPERT_GROUPS, 0))
    return wts, ids


def _assignment_ranks(ids):
    tm = ids.shape[0]
    lane = lax.broadcasted_iota(jnp.int32, ids.shape, 1)
    oh0 = lane == ids[:, 0:1]
    oh1 = lane == ids[:, 1:2]
    hits = (oh0 | oh1).astype(BF16)
    r = lax.broadcasted_iota(jnp.int32, (tm, tm), 0)
    c = lax.broadcasted_iota(jnp.int32, (tm, tm), 1)
    before = jnp.dot((c < r).astype(BF16), hits, preferred_element_type=F32)
    rank0 = jnp.sum(jnp.where(oh0, before, 0.0), axis=-1, keepdims=True)
    rank1 = jnp.sum(jnp.where(oh1, before, 0.0), axis=-1, keepdims=True)
    ranks = jnp.where(lane == 0, rank0, jnp.where(lane == 1, rank1, 0.0)).astype(jnp.int32)
    return ranks, jnp.sum(hits.astype(F32), axis=0, keepdims=True).astype(jnp.int32)


def _mixer_out_kernel(h_ref, a_ref, y_ref, c_ref, wo_ref, wglu_ref, g_ref, rwh_ref, rwl_ref, rb_ref,
                      o_ref, hn_ref, wt_ref, id_ref, rank_ref, cnt_ref):
    half = h_ref.shape[0] // 2
    halves = [slice(r * half, (r + 1) * half) for r in range(2)]
    for rows in halves:
        y = jnp.concatenate([y_ref[p, rows, :] for p in range(y_ref.shape[0])], axis=1)
        c = jnp.concatenate([c_ref[hd, rows, :] for hd in range(c_ref.shape[0])], axis=1)
        gate = jnp.dot(y.astype(BF16), wglu_ref[...], preferred_element_type=F32)
        b = (y * jax.nn.sigmoid(gate)).astype(BF16)
        mixed = jnp.concatenate([a_ref[rows, :], b, c], axis=1)
        for j in range(D_MODEL // OUT_TN):
            cols = slice(j * OUT_TN, (j + 1) * OUT_TN)
            o_ref[rows, cols] = h_ref[rows, cols] + jnp.dot(mixed, wo_ref[:, cols], preferred_element_type=F32)
    for rows in halves:
        hn = _rms_norm(o_ref[rows, :], g_ref[...])
        hn_ref[rows, :] = hn.astype(BF16)
        wts, ids = _route(hn, rwh_ref, rwl_ref, rb_ref)
        wt_ref[rows, :] = wts
        id_ref[rows, :] = ids
    ranks, counts = _assignment_ranks(id_ref[...])
    rank_ref[...] = ranks
    cnt_ref[...] = jnp.broadcast_to(counts, cnt_ref.shape)


def _mixer_out(h, oa, yg, oc, wo, wglu, layer, g, rw_hi, rw_lo, rb):
    rows = h.shape[0]
    tm = _row_tile(rows)

    def layer_weight(shape):
        return pl.BlockSpec((None,) + shape, lambda i: (layer, 0, 0))

    def row_spec(width):
        return pl.BlockSpec((tm, width), lambda i: (i, 0))

    def whole(shape):
        return pl.BlockSpec(shape, lambda i: (0, 0))

    def blocked(width):
        return pl.BlockSpec((width // LANES, tm, LANES), lambda i: (0, i, 0))

    return pl.pallas_call(
        _mixer_out_kernel,
        grid=(rows // tm,),
        in_specs=[
            row_spec(D_MODEL), row_spec(SWA_WIDTH), blocked(SSM_WIDTH), blocked(DIFF_WIDTH),
            layer_weight((D_MODEL, D_MODEL)), layer_weight((SSM_WIDTH, SSM_WIDTH)), whole((1, D_MODEL)),
            whole((D_MODEL, LANES)), whole((D_MODEL, LANES)), whole((1, LANES)),
        ],
        out_specs=[row_spec(D_MODEL), row_spec(D_MODEL), row_spec(LANES), row_spec(LANES), row_spec(LANES),
                   pl.BlockSpec((None, 8, LANES), lambda i: (i, 0, 0))],
        out_shape=[
            jax.ShapeDtypeStruct((rows, D_MODEL), F32),
            jax.ShapeDtypeStruct((rows, D_MODEL), BF16),
            jax.ShapeDtypeStruct((rows, LANES), F32),
            jax.ShapeDtypeStruct((rows, LANES), jnp.int32),
            jax.ShapeDtypeStruct((rows, LANES), jnp.int32),
            jax.ShapeDtypeStruct((rows // tm, 8, LANES), jnp.int32),
        ],
        compiler_params=_params(("parallel",)),
        name="mixer_out",
    )(h, oa, yg, oc, wo, wglu, g, rw_hi, rw_lo, rb)


MOE_UNIT = 16
MOE_DTYPE = BF16


def _moe_tile_rows(tt):
    return 2 * tt + N_EXPERTS * MOE_UNIT


def _moe_tables(tile_cnt, n_blk):
    cnt = tile_cnt[:, 0, :N_EXPERTS]
    seg = (cnt + MOE_UNIT - 1) // MOE_UNIT * MOE_UNIT
    size = jnp.sum(seg, axis=0)
    padded = (size + MOE_BM - 1) // MOE_BM * MOE_BM
    pend = jnp.cumsum(padded)
    pstart = pend - padded
    base = pstart[None, :] + jnp.cumsum(seg, axis=0) - seg
    loc = jnp.cumsum(seg, axis=1) - seg
    units = seg // MOE_UNIT
    blk_start = jnp.arange(n_blk, dtype=jnp.int32) * MOE_BM
    blk_expert = jnp.minimum(jnp.sum((pend[None, :] <= blk_start[:, None]).astype(jnp.int32), axis=1), N_EXPERTS - 1)
    ex = jnp.arange(N_EXPERTS, dtype=jnp.int32)
    nonempty = size > 0
    run_slot = (jnp.cumsum(nonempty.astype(jnp.int32)) - 1) % WEIGHT_SLOTS
    later = (ex[None, :] > ex[:, None]) & nonempty[None, :]
    nxt = jnp.min(jnp.where(later, ex[None, :], N_EXPERTS), axis=1)
    nxt = jnp.where(nxt == N_EXPERTS, -1, nxt)
    nxt2 = jnp.where(nxt >= 0, nxt[jnp.maximum(nxt, 0)], -1)
    i32 = jnp.int32
    loc_lanes = jnp.pad(loc, ((0, 0), (0, LANES - N_EXPERTS)))
    return dict(
        base=base.reshape(-1).astype(i32), loc=loc.reshape(-1).astype(i32), units=units.reshape(-1).astype(i32),
        tile_units=jnp.sum(units, axis=1).astype(i32),
        tail_start=(pstart + size).astype(i32), tail_units=((padded - size) // MOE_UNIT).astype(i32),
        loc_lanes=jnp.broadcast_to(loc_lanes[:, None, :], (loc.shape[0], 8, LANES)).astype(i32),
        blk_expert=blk_expert.astype(i32), blk_slot=run_slot[blk_expert].astype(i32),
        blk_next1=nxt[blk_expert].astype(i32), blk_next2=nxt2[blk_expert].astype(i32),
        n_used=(pend[-1] // MOE_BM).astype(i32).reshape(1))


def _segment_copies(tile, hbm, buf, sem, to_hbm, base_ref, loc_ref, units_ref):
    for e in range(N_EXPERTS):
        k = tile * N_EXPERTS + e
        row_hbm, row_buf = base_ref[k], loc_ref[k]

        def start(u, carry):
            a = hbm.at[pl.ds(pl.multiple_of(row_hbm + u * MOE_UNIT, MOE_UNIT), MOE_UNIT), :]
            b = buf.at[pl.ds(pl.multiple_of(row_buf + u * MOE_UNIT, MOE_UNIT), MOE_UNIT), :]
            (pltpu.make_async_copy(b, a, sem) if to_hbm else pltpu.make_async_copy(a, b, sem)).start()
            return carry

        lax.fori_loop(0, units_ref[k], start, 0)


def _wait_unit_copies(count, hbm, buf, sem):
    def wait(u, carry):
        pltpu.make_async_copy(hbm.at[pl.ds(0, MOE_UNIT), :], buf.at[pl.ds(0, MOE_UNIT), :], sem).wait()
        return carry

    lax.fori_loop(0, count, wait, 0)


def _local_rows(ids, ranks, loc_lanes):
    lane = lax.broadcasted_iota(jnp.int32, ids.shape, 1)
    loc = loc_lanes.astype(F32)
    start0 = jnp.sum(jnp.where(lane == ids[:, 0:1], loc, 0.0), axis=-1, keepdims=True).astype(jnp.int32)
    start1 = jnp.sum(jnp.where(lane == ids[:, 1:2], loc, 0.0), axis=-1, keepdims=True).astype(jnp.int32)
    return start0 + ranks[:, 0:1], start1 + ranks[:, 1:2]


def _dispatch_kernel(base_ref, loc_ref, units_ref, tile_units_ref, tail_start_ref, tail_units_ref, nused_ref,
                     hn_ref, ids_ref, rank_ref, locv_ref, x_hbm, lrow_ref, ybuf, zbuf, sem):
    j = pl.program_id(0)
    n = pl.num_programs(0)
    slot = j % 2
    tt, rt = hn_ref.shape[0], ybuf.shape[1]

    @pl.when(j >= 2)
    def _():
        _wait_unit_copies(tile_units_ref[j - 2], x_hbm, ybuf.at[slot], sem.at[slot])

    row0, row1 = _local_rows(ids_ref[...], rank_ref[...], locv_ref[0:1, :])
    lane = lax.broadcasted_iota(jnp.int32, lrow_ref.shape, 1)
    lrow_ref[...] = jnp.where(lane == 0, row0, jnp.where(lane == 1, row1, 0))
    col = lax.broadcasted_iota(jnp.int32, (tt, rt), 1)
    onehot = ((col == row0) | (col == row1)).astype(BF16)
    compacted = lax.dot_general(onehot, hn_ref[...], (((0,), (0,)), ((), ())), preferred_element_type=F32)
    ybuf[slot] = compacted.astype(ybuf.dtype)
    _segment_copies(j, x_hbm, ybuf.at[slot], sem.at[slot], True, base_ref, loc_ref, units_ref)

    @pl.when(j == n - 1)
    def _():
        zbuf[...] = jnp.zeros(zbuf.shape, zbuf.dtype)
        zunit = zbuf.at[pl.ds(0, MOE_UNIT), :]
        total = 0
        for e in range(N_EXPERTS):
            row = tail_start_ref[e]

            def start(u, carry):
                dst = x_hbm.at[pl.ds(pl.multiple_of(row + u * MOE_UNIT, MOE_UNIT), MOE_UNIT), :]
                pltpu.make_async_copy(zunit, dst, sem.at[2]).start()
                return carry

            lax.fori_loop(0, tail_units_ref[e], start, 0)
            total = total + tail_units_ref[e]
        _wait_unit_copies(total, x_hbm, zunit, sem.at[2])
        n_blk = x_hbm.shape[0] // MOE_BM

        def block_copy(blk):
            dst = x_hbm.at[pl.ds(pl.multiple_of(blk * MOE_BM, MOE_BM), MOE_BM), :]
            return pltpu.make_async_copy(zbuf, dst, sem.at[3])

        def start_block(blk, carry):
            block_copy(blk).start()
            return carry

        def wait_block(blk, carry):
            block_copy(blk).wait()
            return carry

        lax.fori_loop(nused_ref[0], n_blk, start_block, 0)
        lax.fori_loop(nused_ref[0], n_blk, wait_block, 0)

        @pl.when(j >= 1)
        def _():
            _wait_unit_copies(tile_units_ref[jnp.maximum(j - 1, 0)], x_hbm, ybuf.at[1 - slot], sem.at[1 - slot])

        _wait_unit_copies(tile_units_ref[j], x_hbm, ybuf.at[slot], sem.at[slot])


def _dispatch(hn, ids, ranks, tables, tt, n_rows):
    rows = hn.shape[0]
    rt = _moe_tile_rows(tt)
    row_spec = pl.BlockSpec((tt, LANES), lambda j, *_: (j, 0))
    grid_spec = pltpu.PrefetchScalarGridSpec(
        num_scalar_prefetch=7,
        grid=(rows // tt,),
        in_specs=[pl.BlockSpec((tt, D_MODEL), lambda j, *_: (j, 0)), row_spec, row_spec,
                  pl.BlockSpec((None, 8, LANES), lambda j, *_: (j, 0, 0))],
        out_specs=[pl.BlockSpec(memory_space=pl.ANY), row_spec],
        scratch_shapes=[pltpu.VMEM((2, rt, D_MODEL), MOE_DTYPE), pltpu.VMEM((MOE_BM, D_MODEL), MOE_DTYPE),
                        pltpu.SemaphoreType.DMA((4,))],
    )
    return pl.pallas_call(
        _dispatch_kernel,
        grid_spec=grid_spec,
        out_shape=[jax.ShapeDtypeStruct((n_rows, D_MODEL), MOE_DTYPE),
                   jax.ShapeDtypeStruct((rows, LANES), jnp.int32)],
        compiler_params=_params(("arbitrary",)),
        name="moe_dispatch",
    )(tables["base"], tables["loc"], tables["units"], tables["tile_units"], tables["tail_start"],
      tables["tail_units"], tables["n_used"], hn, ids, ranks, tables["loc_lanes"])


WEIGHT_DMA_PRIORITIES = (1, 0, 1)
WEIGHT_SLOTS = 3


def _expert_kernel(be_ref, slot_ref, nxt1_ref, nxt2_ref, nused_ref, x_ref, wg_hbm, wu_hbm, wd_hbm, o_ref,
                   wgf, wuf, wdf, wgb, wub, wdb, wsem, *, layer):
    i = pl.program_id(0)
    n_used = nused_ref[0]

    def weight_copies(expert, wslot):
        pairs = ((wg_hbm, wgf), (wu_hbm, wuf), (wd_hbm, wdf))
        return [pltpu.make_async_copy(w.at[layer, expert], buf.at[wslot], wsem.at[wslot]) for w, buf in pairs]

    def start_weights(expert, wslot):
        for c, priority in zip(weight_copies(expert, wslot), WEIGHT_DMA_PRIORITIES):
            c.start(priority=priority)

    @pl.when(i == 0)
    def _():
        start_weights(be_ref[0], 0)

        @pl.when(nxt1_ref[0] >= 0)
        def _():
            start_weights(nxt1_ref[0], 1)

    @pl.when(i < n_used)
    def _():
        @pl.when((i == 0) | (be_ref[i] != be_ref[jnp.maximum(i - 1, 0)]))
        def _():
            wslot = slot_ref[i]
            for c in weight_copies(be_ref[i], wslot):
                c.wait()

            @pl.when(nxt2_ref[i] >= 0)
            def _():
                start_weights(nxt2_ref[i], lax.rem(wslot + 2, WEIGHT_SLOTS))

            wgb[...] = wgf[wslot].astype(BF16)
            wub[...] = wuf[wslot].astype(BF16)
            wdb[...] = wdf[wslot].astype(BF16)

        x = x_ref[...]
        gate = jnp.dot(x, wgb[...], preferred_element_type=F32)
        up = jnp.dot(x, wub[...], preferred_element_type=F32)
        act = (jax.nn.silu(gate) * up).astype(BF16)
        o_ref[...] = jnp.dot(act, wdb[...], preferred_element_type=F32).astype(o_ref.dtype)

    @pl.when(i >= n_used)
    def _():
        o_ref[...] = jnp.zeros(o_ref.shape, o_ref.dtype)


def _experts(x_sorted, tables, wg, wu, wd, layer, n_blk):
    hbm = pl.BlockSpec(memory_space=pl.ANY)
    grid_spec = pltpu.PrefetchScalarGridSpec(
        num_scalar_prefetch=5,
        grid=(n_blk,),
        in_specs=[pl.BlockSpec((MOE_BM, D_MODEL), lambda i, be, slot, n1, n2, nu: (jnp.minimum(i, nu[0] - 1), 0)),
                  hbm, hbm, hbm],
        out_specs=pl.BlockSpec((MOE_BM, D_MODEL), lambda i, *_: (i, 0)),
        scratch_shapes=[
            pltpu.VMEM((WEIGHT_SLOTS, D_MODEL, D_EXPERT), F32),
            pltpu.VMEM((WEIGHT_SLOTS, D_MODEL, D_EXPERT), F32),
            pltpu.VMEM((WEIGHT_SLOTS, D_EXPERT, D_MODEL), F32),
            pltpu.VMEM((D_MODEL, D_EXPERT), BF16),
            pltpu.VMEM((D_MODEL, D_EXPERT), BF16),
            pltpu.VMEM((D_EXPERT, D_MODEL), BF16),
            pltpu.SemaphoreType.DMA((WEIGHT_SLOTS,)),
        ],
    )
    return pl.pallas_call(
        functools.partial(_expert_kernel, layer=layer),
        grid_spec=grid_spec,
        out_shape=jax.ShapeDtypeStruct((n_blk * MOE_BM, D_MODEL), MOE_DTYPE),
        compiler_params=_params(("arbitrary",)),
        name="moe_experts",
    )(tables["blk_expert"], tables["blk_slot"], tables["blk_next1"], tables["blk_next2"], tables["n_used"],
      x_sorted, wg, wu, wd)


def _combine_kernel(base_ref, loc_ref, units_ref, tile_units_ref, outtab_ref, h_ref, wt_ref, lrow_ref, g_ref, y_hbm,
                    o_ref, ybuf, sem, *final_scratch, final_norm):
    j = pl.program_id(0)
    n = pl.num_programs(0)
    slot = j % 2
    tt, rt = h_ref.shape[0], ybuf.shape[1]
    assert rt % LANES == 0

    @pl.when(j == 0)
    def _():
        ybuf[...] = jnp.zeros(ybuf.shape, ybuf.dtype)
        _segment_copies(0, y_hbm, ybuf.at[0], sem.at[0], False, base_ref, loc_ref, units_ref)

    @pl.when(j + 1 < n)
    def _():
        _segment_copies(j + 1, y_hbm, ybuf.at[1 - slot], sem.at[1 - slot], False, base_ref, loc_ref, units_ref)

    _wait_unit_copies(tile_units_ref[j], y_hbm, ybuf.at[slot], sem.at[slot])
    lrow = lrow_ref[...]
    wt = wt_ref[...]
    col = lax.broadcasted_iota(jnp.int32, (tt, rt), 1)
    is0 = col == lrow[:, 0:1]
    is1 = col == lrow[:, 1:2]
    w_of_row = jnp.sum(jnp.where(is0, wt[:, 0:1], 0.0) + jnp.where(is1, wt[:, 1:2], 0.0), axis=0, keepdims=True)
    w_rows = jnp.transpose(jnp.broadcast_to(w_of_row, (LANES, rt)))
    w_rows = jnp.tile(w_rows, (1, OUT_TN // LANES))
    pick = (is0 | is1).astype(BF16)
    if final_norm:
        out_hbm, (obuf, osem) = o_ref, final_scratch
        tile_out = obuf.at[slot]

        def range_copies(tile, slot_, wait):
            for r in range(2):
                k = (tile * 2 + r) * 3
                unit0, n_units, row0 = outtab_ref[k], outtab_ref[k + 1], outtab_ref[k + 2]

                def one(u, carry):
                    src = obuf.at[slot_, pl.ds(pl.multiple_of((unit0 + u) * 8, 8), 8), :]
                    dst = out_hbm.at[pl.ds(pl.multiple_of(row0 + u * 8, 8), 8), :]
                    copy = pltpu.make_async_copy(src, dst, osem.at[slot_])
                    copy.wait() if wait else copy.start()
                    return carry

                lax.fori_loop(0, n_units, one, 0)

        @pl.when(j >= 2)
        def _():
            range_copies(j - 2, slot, True)
    else:
        tile_out = o_ref
    for c in range(D_MODEL // OUT_TN):
        cols = slice(c * OUT_TN, (c + 1) * OUT_TN)
        y = (ybuf[slot, :, cols].astype(F32) * w_rows).astype(BF16)
        tile_out[:, cols] = h_ref[:, cols] + jnp.dot(pick, y, preferred_element_type=F32)
    if final_norm:
        tile_out[...] = _rms_norm(tile_out[...], g_ref[...])
        range_copies(j, slot, False)

        @pl.when(j == n - 1)
        def _():
            @pl.when(j >= 1)
            def _():
                range_copies(jnp.maximum(j - 1, 0), 1 - slot, True)

            range_copies(j, slot, True)


def _kept_row_runs(n_tiles, tt, lp):
    drop = PAD + N_META
    table = []
    for j in range(n_tiles):
        runs, u = [], 0
        while u < tt // 8:
            p = j * tt + u * 8
            if p % lp >= drop:
                b = p // lp
                n = (min((b + 1) * lp, (j + 1) * tt) - p) // 8
                runs.append((u, n, p - (b + 1) * drop))
                u += n
            else:
                u += 1
        assert len(runs) <= 2
        runs += [(0, 0, 0)] * (2 - len(runs))
        table += [v for run in runs for v in run]
    return jnp.asarray(table, jnp.int32)


def _combine(h, wts, lrows, y_sorted, tables, g, tt, final_norm, lp):
    rows = h.shape[0]
    n_tiles = rows // tt
    rt = _moe_tile_rows(tt)
    scratch = [pltpu.VMEM((2, rt, D_MODEL), MOE_DTYPE), pltpu.SemaphoreType.DMA((2,))]
    if final_norm:
        out_rows = rows - (rows // lp) * (PAD + N_META)
        out_spec = pl.BlockSpec(memory_space=pl.ANY)
        scratch += [pltpu.VMEM((2, tt, D_MODEL), F32), pltpu.SemaphoreType.DMA((2,))]
        out_table = _kept_row_runs(n_tiles, tt, lp)
    else:
        out_rows = rows
        out_spec = pl.BlockSpec((tt, D_MODEL), lambda j, *_: (j, 0))
        out_table = jnp.zeros((1,), jnp.int32)
    grid_spec = pltpu.PrefetchScalarGridSpec(
        num_scalar_prefetch=5,
        grid=(n_tiles,),
        in_specs=[
            pl.BlockSpec((tt, D_MODEL), lambda j, *_: (j, 0)),
            pl.BlockSpec((tt, LANES), lambda j, *_: (j, 0)),
            pl.BlockSpec((tt, LANES), lambda j, *_: (j, 0)),
            pl.BlockSpec((1, D_MODEL), lambda j, *_: (0, 0)),
            pl.BlockSpec(memory_space=pl.ANY),
        ],
        out_specs=out_spec,
        scratch_shapes=scratch,
    )
    return pl.pallas_call(
        functools.partial(_combine_kernel, final_norm=final_norm),
        grid_spec=grid_spec,
        out_shape=jax.ShapeDtypeStruct((out_rows, D_MODEL), F32),
        compiler_params=_params(("arbitrary",)),
        name="moe_combine",
    )(tables["base"], tables["loc"], tables["units"], tables["tile_units"], out_table, h, wts, lrows, g, y_sorted)


def _moe(h, hn, wts, ids, ranks, tile_cnt, wg, wu, wd, layer, final_g, lp):
    rows = h.shape[0]
    n_tiles = tile_cnt.shape[0]
    tt = rows // n_tiles
    n_rows = 2 * rows + n_tiles * N_EXPERTS * (MOE_UNIT - 1) + N_EXPERTS * (MOE_BM - 1)
    n_blk = -(-n_rows // MOE_BM)
    tables = _moe_tables(tile_cnt, n_blk)
    x_sorted, lrows = _dispatch(hn, ids, ranks, tables, tt, n_blk * MOE_BM)
    y_sorted = _experts(x_sorted, tables, wg, wu, wd, layer, n_blk)
    final_norm = final_g is not None
    g = final_g if final_norm else jnp.ones((1, D_MODEL), F32)
    return _combine(h, wts, lrows, y_sorted, tables, g, tt, final_norm, lp)


def _rope_tables(batch, lp):
    half = HEAD_DIM // 2
    inv = ROPE_THETA ** (-jnp.arange(half, dtype=F32) / half)
    pos = (jnp.arange(lp, dtype=jnp.int32) - PAD).astype(F32)
    ang = pos[:, None] * inv[None, :]
    cos = jnp.tile(jnp.cos(ang), (batch, LANES // half))
    sin = jnp.sin(ang)
    sin = jnp.tile(jnp.concatenate([-sin, sin], axis=1), (batch, LANES // HEAD_DIM))
    return cos, sin


def kernel(x, meta_tokens, attn_norm_g, ffn_norm_g, w_in, swa_sinks, ssm_lambda_re, ssm_lambda_im, ssm_b_re, ssm_b_im, ssm_c_re, ssm_c_im, ssm_d, ssm_log_dt, ssm_w_glu, diff_lambda_q1, diff_lambda_k1, diff_lambda_q2, diff_lambda_k2, diff_subln_g, w_out, router_group_w, router_group_b, router_expert_w, router_expert_b, moe_w_gate, moe_w_up, moe_w_down, final_norm_g):
    batch, seq, d = x.shape
    depth = w_in.shape[0]
    lp = PAD + N_META + seq
    nb = lp // BLK
    meta = jnp.broadcast_to(meta_tokens[None].astype(x.dtype), (batch, N_META, d))
    h = jnp.concatenate([jnp.zeros((batch, PAD, d), x.dtype), meta, x], axis=1).reshape(batch * lp, d)
    cos, sin = _rope_tables(batch, lp)
    w_in_bf, w_out_bf, w_glu_bf = w_in.astype(BF16), w_out.astype(BF16), ssm_w_glu.astype(BF16)
    for l in range(depth):
        att, us, qd, kd, vd = _norm_inproj(h, attn_norm_g[l][None], w_in_bf, l, cos, sin)
        o_a = _swa(att, swa_sinks[l].astype(F32), batch, nb)
        kc, bc, cc, a_t = _ssm_operators(ssm_lambda_re[l], ssm_lambda_im[l], ssm_b_re[l], ssm_b_im[l],
                                         ssm_c_re[l], ssm_c_im[l], ssm_log_dt[l])
        yg = _ssm(us, kc, bc, cc, a_t, ssm_d[l].astype(F32).reshape(SSM_PACKS, 1, LANES), batch, lp)
        lam_init = 0.8 - 0.6 * math.exp(-0.3 * l)
        lam_vecs = jnp.stack([diff_lambda_q1[l], diff_lambda_k1[l], diff_lambda_q2[l], diff_lambda_k2[l]]).astype(F32)
        o_c = _diff_attn(qd, kd, vd, lam_vecs, diff_subln_g[l][None].astype(F32), lam_init, batch, nb)
        rw = jnp.concatenate([router_group_w[l], router_expert_w[l]], axis=1).astype(F32)
        rw = jnp.pad(rw, ((0, 0), (0, LANES - rw.shape[1])))
        rb = jnp.concatenate([router_group_b[l], router_expert_b[l]]).astype(F32)
        rb = jnp.pad(rb, (0, LANES - rb.shape[0]))[None]
        rw_hi, rw_lo = _split_bf16(rw)
        g_ffn = ffn_norm_g[l][None].astype(F32)
        h, hn, wts, ids, ranks, tile_cnt = _mixer_out(h, o_a, yg, o_c, w_out_bf, w_glu_bf, l, g_ffn, rw_hi, rw_lo, rb)
        final_g = final_norm_g[None].astype(F32) if l == depth - 1 else None
        h = _moe(h, hn, wts, ids, ranks, tile_cnt, moe_w_gate, moe_w_up, moe_w_down, l, final_g, lp)
    return h.reshape(batch, seq, d)
```
